```python
import math
import jax, jax.numpy as jnp
from jax import lax
import numpy as np

D_MODEL = 1024
BATCH = 8
SEQ = 8192
DEPTH = 4

HEAD_DIM = 64
H_FOX = 8
H_SB = 8
N_HEADS = H_FOX + H_SB
MIX_WIDTH = N_HEADS * HEAD_DIM
ATTN_IN = 3 * MIX_WIDTH + H_FOX
CONV_WIDTH = D_MODEL
CONV_K = 3
FFN_CONV_K = 3
D_FF = 2816
Q_BLOCK = 128
N_ATTN = (DEPTH + 1) // 2
N_CONV = DEPTH // 2
EPS = 1e-6

kernel_name = 'fox_stickbreak_shortconv_hybrid'


def rms_norm(x, g):
    xf = x.astype(jnp.float32)
    y = xf * lax.rsqrt(jnp.mean(xf * xf, axis=-1, keepdims=True) + EPS)
    return (y * g.astype(jnp.float32)).astype(x.dtype)


def causal_dwconv(x, w):
    k = w.shape[0]
    s = x.shape[1]
    xp = jnp.pad(x, ((0, 0), (k - 1, 0), (0, 0)))
    return sum(xp[:, j:j + s] * w[j] for j in range(k))


def split_query_blocks(t):
    b, h, s = t.shape[:3]
    rest = t.shape[3:]
    t = t.reshape((b, h, s // Q_BLOCK, Q_BLOCK) + rest)
    return jnp.moveaxis(t, 2, 0)


def merge_query_blocks(t):
    nb, b, h, qb, d = t.shape
    return jnp.moveaxis(t, 0, 2).reshape(b, h, nb * qb, d)


def forgetting_attention(q, k, v, log_f):
    s, dh = q.shape[2], q.shape[3]
    cum_f = jnp.cumsum(log_f, axis=-1)
    key_pos = jnp.arange(s)
    scale = dh ** -0.5
    nb = s // Q_BLOCK

    def one_block(args):
        i, qi, fi = args
        q_pos = i * Q_BLOCK + jnp.arange(Q_BLOCK)
        logits = jnp.einsum('bhqd,bhkd->bhqk', qi, k,
                            preferred_element_type=jnp.float32) * scale
        logits = logits + fi[..., :, None] - cum_f[:, :, None, :]
        causal = key_pos[None, :] <= q_pos[:, None]
        logits = jnp.where(causal, logits, -jnp.inf)
        p = jax.nn.softmax(logits, axis=-1)
        return jnp.einsum('bhqk,bhkd->bhqd', p.astype(v.dtype), v)

    out = lax.map(one_block, (jnp.arange(nb), split_query_blocks(q), split_query_blocks(cum_f)))
    return merge_query_blocks(out)


def stick_breaking_attention(q, k, v):
    s, dh = q.shape[2], q.shape[3]
    key_pos = jnp.arange(s)
    scale = dh ** -0.5
    nb = s // Q_BLOCK

    def one_block(args):
        i, qi = args
        q_pos = i * Q_BLOCK + jnp.arange(Q_BLOCK)
        z = jnp.einsum('bhqd,bhkd->bhqk', qi, k,
                       preferred_element_type=jnp.float32) * scale
        strict = key_pos[None, :] < q_pos[:, None]
        log_beta = jax.nn.log_sigmoid(z)
        log_one_minus = jnp.where(strict, jax.nn.log_sigmoid(-z), 0.0)
        key_axis = log_one_minus.ndim - 1
        later = lax.cumsum(log_one_minus, axis=key_axis, reverse=True) - log_one_minus
        w = jnp.where(strict, jnp.exp(log_beta + later), 0.0)
        return jnp.einsum('bhqk,bhkd->bhqd', w.astype(v.dtype), v)

    out = lax.map(one_block, (jnp.arange(nb), split_query_blocks(q)))
    return merge_query_blocks(out)


def attention_mixer(h, norm_g, w_in, f_bias, fox_q_g, fox_k_g, sb_q_g, sb_k_g, w_out):
    b, s, _ = h.shape
    xn = rms_norm(h, norm_g)
    proj = xn @ w_in

    def heads(t):
        return t.reshape(b, s, N_HEADS, HEAD_DIM).transpose(0, 2, 1, 3)

    q = heads(proj[..., :MIX_WIDTH])
    k = heads(proj[..., MIX_WIDTH:2 * MIX_WIDTH])
    v = heads(proj[..., 2 * MIX_WIDTH:3 * MIX_WIDTH])
    f_logit = proj[..., 3 * MIX_WIDTH:].astype(jnp.float32) + f_bias.astype(jnp.float32)
    log_f = jax.nn.log_sigmoid(f_logit).transpose(0, 2, 1)

    q_fox = rms_norm(q[:, :H_FOX], fox_q_g)
    k_fox = rms_norm(k[:, :H_FOX], fox_k_g)
    q_sb = rms_norm(q[:, H_FOX:], sb_q_g)
    k_sb = rms_norm(k[:, H_FOX:], sb_k_g)

    o_fox = forgetting_attention(q_fox, k_fox, v[:, :H_FOX], log_f)
    o_sb = stick_breaking_attention(q_sb, k_sb, v[:, H_FOX:])
    o = jnp.concatenate([o_fox, o_sb], axis=1)
    o = o.transpose(0, 2, 1, 3).reshape(b, s, MIX_WIDTH)
    return o @ w_out


def short_conv_mixer(h, norm_g, w_in, conv_w, w_out):
    xn = rms_norm(h, norm_g)
    proj = xn @ w_in
    gate_b = proj[..., :CONV_WIDTH]
    gate_c = proj[..., CONV_WIDTH:2 * CONV_WIDTH]
    u = proj[..., 2 * CONV_WIDTH:]
    y = gate_b * causal_dwconv(gate_c * u, conv_w)
    return y @ w_out


def conv_ffn(h, norm_g, w_up, conv_w, w_down):
    xn = rms_norm(h, norm_g)
    u = causal_dwconv(xn @ w_up, conv_w)
    g, val = u[..., :D_FF], u[..., D_FF:]
    return (jax.nn.silu(g) * val) @ w_down


def _fwd_setup_inputs(seed: int = 0) -> dict:
    key = jax.random.key(seed)
    ks = jax.random.split(key, 20)
    f32 = jnp.float32
    out_scale = (2 * DEPTH) ** -0.5

    def normal(k, shape, scale):
        return scale * jax.random.normal(k, shape, f32)

    def gain(k, shape):
        return 1.0 + 0.05 * jax.random.normal(k, shape, f32)

    x = jax.random.normal(ks[0], (BATCH, SEQ, D_MODEL), f32)
    attn_norm = gain(ks[1], (N_ATTN, D_MODEL))
    attn_w_in = normal(ks[2], (N_ATTN, D_MODEL, ATTN_IN), D_MODEL ** -0.5)
    attn_f_bias = jnp.linspace(1.0, 6.0, H_FOX, dtype=f32)[None, :] + normal(ks[3], (N_ATTN, H_FOX), 0.1)
    fox_q_gain = gain(ks[4], (N_ATTN, HEAD_DIM))
    fox_k_gain = gain(ks[5], (N_ATTN, HEAD_DIM))
    sb_q_gain = gain(ks[6], (N_ATTN, HEAD_DIM))
    sb_k_gain = gain(ks[7], (N_ATTN, HEAD_DIM))
    attn_w_out = normal(ks[8], (N_ATTN, MIX_WIDTH, D_MODEL), out_scale * MIX_WIDTH ** -0.5)
    conv_norm = gain(ks[9], (N_CONV, D_MODEL))
    conv_w_in = normal(ks[10], (N_CONV, D_MODEL, 3 * CONV_WIDTH), D_MODEL ** -0.5)
    conv_kernel = normal(ks[11], (N_CONV, CONV_K, CONV_WIDTH), CONV_K ** -0.5)
    conv_w_out = normal(ks[12], (N_CONV, CONV_WIDTH, D_MODEL), out_scale * CONV_WIDTH ** -0.5)
    ffn_norm = gain(ks[13], (DEPTH, D_MODEL))
    ffn_w_up = normal(ks[14], (DEPTH, D_MODEL, 2 * D_FF), D_MODEL ** -0.5)
    ffn_conv = normal(ks[15], (DEPTH, FFN_CONV_K, 2 * D_FF), FFN_CONV_K ** -0.5)
    ffn_w_down = normal(ks[16], (DEPTH, D_FF, D_MODEL), out_scale * D_FF ** -0.5)
    return {
        'x': x,
        'attn_norm': attn_norm, 'attn_w_in': attn_w_in, 'attn_f_bias': attn_f_bias,
        'fox_q_gain': fox_q_gain, 'fox_k_gain': fox_k_gain,
        'sb_q_gain': sb_q_gain, 'sb_k_gain': sb_k_gain, 'attn_w_out': attn_w_out,
        'conv_norm': conv_norm, 'conv_w_in': conv_w_in, 'conv_kernel': conv_kernel,
        'conv_w_out': conv_w_out,
        'ffn_norm': ffn_norm, 'ffn_w_up': ffn_w_up, 'ffn_conv': ffn_conv, 'ffn_w_down': ffn_w_down,
    }


def _fwd_reference(x, attn_norm, attn_w_in, attn_f_bias, fox_q_gain, fox_k_gain, sb_q_gain,
              sb_k_gain, attn_w_out, conv_norm, conv_w_in, conv_kernel, conv_w_out,
              ffn_norm, ffn_w_up, ffn_conv, ffn_w_down):
    h = x
    for layer in range(DEPTH):
        i = layer // 2
        if layer % 2 == 0:
            h = h + attention_mixer(h, attn_norm[i], attn_w_in[i], attn_f_bias[i],
                                    fox_q_gain[i], fox_k_gain[i], sb_q_gain[i], sb_k_gain[i],
                                    attn_w_out[i])
        else:
            h = h + short_conv_mixer(h, conv_norm[i], conv_w_in[i], conv_kernel[i], conv_w_out[i])
        h = h + conv_ffn(h, ffn_norm[layer], ffn_w_up[layer], ffn_conv[layer], ffn_w_down[layer])
    return h


import jax as _jax
import jax.numpy as _jnp

TWIN_FORMAT = 'train_step'
FWD_PARAMS = ['x', 'attn_norm', 'attn_w_in', 'attn_f_bias', 'fox_q_gain', 'fox_k_gain', 'sb_q_gain', 'sb_k_gain', 'attn_w_out', 'conv_norm', 'conv_w_in', 'conv_kernel', 'conv_w_out', 'ffn_norm', 'ffn_w_up', 'ffn_conv', 'ffn_w_down']
TWIN_WEIGHTS = ['attn_norm', 'attn_w_in', 'attn_f_bias', 'fox_q_gain', 'fox_k_gain', 'sb_q_gain', 'sb_k_gain', 'attn_w_out', 'conv_norm', 'conv_w_in', 'conv_kernel', 'conv_w_out', 'ffn_norm', 'ffn_w_up', 'ffn_conv', 'ffn_w_down']
TWIN_DIFF_INPUT = 'x'
TWIN_INPUTS = ['x', 'attn_norm', 'attn_w_in', 'attn_f_bias', 'fox_q_gain', 'fox_k_gain', 'sb_q_gain', 'sb_k_gain', 'attn_w_out', 'conv_norm', 'conv_w_in', 'conv_kernel', 'conv_w_out', 'ffn_norm', 'ffn_w_up', 'ffn_conv', 'ffn_w_down', 'loss_target', 'm_attn_norm', 'm_attn_w_in', 'm_attn_f_bias', 'm_fox_q_gain', 'm_fox_k_gain', 'm_sb_q_gain', 'm_sb_k_gain', 'm_attn_w_out', 'm_conv_norm', 'm_conv_w_in', 'm_conv_kernel', 'm_conv_w_out', 'm_ffn_norm', 'm_ffn_w_up', 'm_ffn_conv', 'm_ffn_w_down', 'v_attn_norm', 'v_attn_w_in', 'v_attn_f_bias', 'v_fox_q_gain', 'v_fox_k_gain', 'v_sb_q_gain', 'v_sb_k_gain', 'v_attn_w_out', 'v_conv_norm', 'v_conv_w_in', 'v_conv_kernel', 'v_conv_w_out', 'v_ffn_norm', 'v_ffn_w_up', 'v_ffn_conv', 'v_ffn_w_down']
TWIN_OUTPUTS = ['loss', 'grad_x', 'grad_attn_norm', 'grad_attn_w_in', 'grad_attn_f_bias', 'grad_fox_q_gain', 'grad_fox_k_gain', 'grad_sb_q_gain', 'grad_sb_k_gain', 'grad_attn_w_out', 'grad_conv_norm', 'grad_conv_w_in', 'grad_conv_kernel', 'grad_conv_w_out', 'grad_ffn_norm', 'grad_ffn_w_up', 'grad_ffn_conv', 'grad_ffn_w_down', 'delta_attn_norm', 'delta_attn_w_in', 'delta_attn_f_bias', 'delta_fox_q_gain', 'delta_fox_k_gain', 'delta_sb_q_gain', 'delta_sb_k_gain', 'delta_attn_w_out', 'delta_conv_norm', 'delta_conv_w_in', 'delta_conv_kernel', 'delta_conv_w_out', 'delta_ffn_norm', 'delta_ffn_w_up', 'delta_ffn_conv', 'delta_ffn_w_down', 'new_m_attn_norm', 'new_m_attn_w_in', 'new_m_attn_f_bias', 'new_m_fox_q_gain', 'new_m_fox_k_gain', 'new_m_sb_q_gain', 'new_m_sb_k_gain', 'new_m_attn_w_out', 'new_m_conv_norm', 'new_m_conv_w_in', 'new_m_conv_kernel', 'new_m_conv_w_out', 'new_m_ffn_norm', 'new_m_ffn_w_up', 'new_m_ffn_conv', 'new_m_ffn_w_down', 'new_v_attn_norm', 'new_v_attn_w_in', 'new_v_attn_f_bias', 'new_v_fox_q_gain', 'new_v_fox_k_gain', 'new_v_sb_q_gain', 'new_v_sb_k_gain', 'new_v_attn_w_out', 'new_v_conv_norm', 'new_v_conv_w_in', 'new_v_conv_kernel', 'new_v_conv_w_out', 'new_v_ffn_norm', 'new_v_ffn_w_up', 'new_v_ffn_conv', 'new_v_ffn_w_down']
TWIN_LEAF_KINDS = {'loss': 'loss', 'grad_x': 'grad_x', 'grad_attn_norm': 'grad_w', 'grad_attn_w_in': 'grad_w', 'grad_attn_f_bias': 'grad_w', 'grad_fox_q_gain': 'grad_w', 'grad_fox_k_gain': 'grad_w', 'grad_sb_q_gain': 'grad_w', 'grad_sb_k_gain': 'grad_w', 'grad_attn_w_out': 'grad_w', 'grad_conv_norm': 'grad_w', 'grad_conv_w_in': 'grad_w', 'grad_conv_kernel': 'grad_w', 'grad_conv_w_out': 'grad_w', 'grad_ffn_norm': 'grad_w', 'grad_ffn_w_up': 'grad_w', 'grad_ffn_conv': 'grad_w', 'grad_ffn_w_down': 'grad_w', 'delta_attn_norm': 'delta_w', 'delta_attn_w_in': 'delta_w', 'delta_attn_f_bias': 'delta_w', 'delta_fox_q_gain': 'delta_w', 'delta_fox_k_gain': 'delta_w', 'delta_sb_q_gain': 'delta_w', 'delta_sb_k_gain': 'delta_w', 'delta_attn_w_out': 'delta_w', 'delta_conv_norm': 'delta_w', 'delta_conv_w_in': 'delta_w', 'delta_conv_kernel': 'delta_w', 'delta_conv_w_out': 'delta_w', 'delta_ffn_norm': 'delta_w', 'delta_ffn_w_up': 'delta_w', 'delta_ffn_conv': 'delta_w', 'delta_ffn_w_down': 'delta_w', 'new_m_attn_norm': 'new_m', 'new_m_attn_w_in': 'new_m', 'new_m_attn_f_bias': 'new_m', 'new_m_fox_q_gain': 'new_m', 'new_m_fox_k_gain': 'new_m', 'new_m_sb_q_gain': 'new_m', 'new_m_sb_k_gain': 'new_m', 'new_m_attn_w_out': 'new_m', 'new_m_conv_norm': 'new_m', 'new_m_conv_w_in': 'new_m', 'new_m_conv_kernel': 'new_m', 'new_m_conv_w_out': 'new_m', 'new_m_ffn_norm': 'new_m', 'new_m_ffn_w_up': 'new_m', 'new_m_ffn_conv': 'new_m', 'new_m_ffn_w_down': 'new_m', 'new_v_attn_norm': 'new_v', 'new_v_attn_w_in': 'new_v', 'new_v_attn_f_bias': 'new_v', 'new_v_fox_q_gain': 'new_v', 'new_v_fox_k_gain': 'new_v', 'new_v_sb_q_gain': 'new_v', 'new_v_sb_k_gain': 'new_v', 'new_v_attn_w_out': 'new_v', 'new_v_conv_norm': 'new_v', 'new_v_conv_w_in': 'new_v', 'new_v_conv_kernel': 'new_v', 'new_v_conv_w_out': 'new_v', 'new_v_ffn_norm': 'new_v', 'new_v_ffn_w_up': 'new_v', 'new_v_ffn_conv': 'new_v', 'new_v_ffn_w_down': 'new_v'}


def _forward(args):
    return _fwd_reference(*[args[k] for k in FWD_PARAMS])


def _output_shape():
    def fwd():
        inp = _fwd_setup_inputs(0)
        return _fwd_reference(*[inp[k] for k in FWD_PARAMS])
    out = _jax.eval_shape(fwd)
    return out.shape, out.dtype

N_MICROBATCH = 1
ADAM_LR = 0.001
ADAM_B1 = 0.9
ADAM_B2 = 0.999
ADAM_EPS = 1e-08
ADAM_WD = 0.01
ADAM_STEP = 10
PER_EXAMPLE_BATCH_AXIS = {'x': 0, 'loss_target': 0}
SHARED_INPUTS = []
_WEIGHT_DTYPES = {'attn_norm': _jnp.float32, 'attn_w_in': _jnp.float32, 'attn_f_bias': _jnp.float32, 'fox_q_gain': _jnp.float32, 'fox_k_gain': _jnp.float32, 'sb_q_gain': _jnp.float32, 'sb_k_gain': _jnp.float32, 'attn_w_out': _jnp.float32, 'conv_norm': _jnp.float32, 'conv_w_in': _jnp.float32, 'conv_kernel': _jnp.float32, 'conv_w_out': _jnp.float32, 'ffn_norm': _jnp.float32, 'ffn_w_up': _jnp.float32, 'ffn_conv': _jnp.float32, 'ffn_w_down': _jnp.float32}
MOMENT_SCALE = {'attn_norm': 1.976158e+00, 'attn_w_in': 9.011850e-02, 'attn_f_bias': 1.752482e+01, 'fox_q_gain': 1.931709e+00, 'fox_k_gain': 1.926844e+00, 'sb_q_gain': 3.837790e+00, 'sb_k_gain': 3.878717e+00, 'attn_w_out': 3.631971e-01, 'conv_norm': 2.392511e+01, 'conv_w_in': 2.738373e-01, 'conv_kernel': 4.531633e+00, 'conv_w_out': 7.195080e-01, 'ffn_norm': 6.318236e+00, 'ffn_w_up': 8.474346e-02, 'ffn_conv': 8.285499e-01, 'ffn_w_down': 3.962545e-01}


def _to_microbatches(a, axis):
    t = _jnp.moveaxis(a, axis, 0)
    t = t.reshape((N_MICROBATCH, t.shape[0] // N_MICROBATCH) + t.shape[1:])
    return _jnp.moveaxis(t, 1, axis + 1)


def setup_inputs(seed: int = 0) -> dict:
    inp = _fwd_setup_inputs(seed)
    key = _jax.random.fold_in(_jax.random.key(seed), 7919)
    shape, _ = _output_shape()
    out = dict(inp)
    out["loss_target"] = _jax.random.normal(_jax.random.fold_in(key, 0), shape, _jnp.float32)
    for i, name in enumerate(TWIN_WEIGHTS):
        w = inp[name].astype(_jnp.float32)
        if MOMENT_SCALE is None:
            s = _jnp.sqrt(_jnp.mean(_jnp.square(w)) + 1e-30)
        else:
            s = MOMENT_SCALE[name]
        km, kv = _jax.random.split(_jax.random.fold_in(key, i + 1))
        out[name] = w
        out["m_" + name] = s * _jax.random.normal(km, w.shape, _jnp.float32)
        out["v_" + name] = (s * s) * _jax.random.uniform(kv, w.shape, _jnp.float32, 0.5, 1.5)
    if N_MICROBATCH > 1:
        for name, axis in PER_EXAMPLE_BATCH_AXIS.items():
            out[name] = _to_microbatches(out[name], axis)
    return {'x': out['x'], 'attn_norm': out['attn_norm'], 'attn_w_in': out['attn_w_in'], 'attn_f_bias': out['attn_f_bias'], 'fox_q_gain': out['fox_q_gain'], 'fox_k_gain': out['fox_k_gain'], 'sb_q_gain': out['sb_q_gain'], 'sb_k_gain': out['sb_k_gain'], 'attn_w_out': out['attn_w_out'], 'conv_norm': out['conv_norm'], 'conv_w_in': out['conv_w_in'], 'conv_kernel': out['conv_kernel'], 'conv_w_out': out['conv_w_out'], 'ffn_norm': out['ffn_norm'], 'ffn_w_up': out['ffn_w_up'], 'ffn_conv': out['ffn_conv'], 'ffn_w_down': out['ffn_w_down'], 'loss_target': out['loss_target'], 'm_attn_norm': out['m_attn_norm'], 'm_attn_w_in': out['m_attn_w_in'], 'm_attn_f_bias': out['m_attn_f_bias'], 'm_fox_q_gain': out['m_fox_q_gain'], 'm_fox_k_gain': out['m_fox_k_gain'], 'm_sb_q_gain': out['m_sb_q_gain'], 'm_sb_k_gain': out['m_sb_k_gain'], 'm_attn_w_out': out['m_attn_w_out'], 'm_conv_norm': out['m_conv_norm'], 'm_conv_w_in': out['m_conv_w_in'], 'm_conv_kernel': out['m_conv_kernel'], 'm_conv_w_out': out['m_conv_w_out'], 'm_ffn_norm': out['m_ffn_norm'], 'm_ffn_w_up': out['m_ffn_w_up'], 'm_ffn_conv': out['m_ffn_conv'], 'm_ffn_w_down': out['m_ffn_w_down'], 'v_attn_norm': out['v_attn_norm'], 'v_attn_w_in': out['v_attn_w_in'], 'v_attn_f_bias': out['v_attn_f_bias'], 'v_fox_q_gain': out['v_fox_q_gain'], 'v_fox_k_gain': out['v_fox_k_gain'], 'v_sb_q_gain': out['v_sb_q_gain'], 'v_sb_k_gain': out['v_sb_k_gain'], 'v_attn_w_out': out['v_attn_w_out'], 'v_conv_norm': out['v_conv_norm'], 'v_conv_w_in': out['v_conv_w_in'], 'v_conv_kernel': out['v_conv_kernel'], 'v_conv_w_out': out['v_conv_w_out'], 'v_ffn_norm': out['v_ffn_norm'], 'v_ffn_w_up': out['v_ffn_w_up'], 'v_ffn_conv': out['v_ffn_conv'], 'v_ffn_w_down': out['v_ffn_w_down']}


def _loss(weights, diff, rest, loss_target):
    with _jax.named_scope("forward"):
        args = {**rest, TWIN_DIFF_INPUT: diff, **{k: w.astype(_WEIGHT_DTYPES[k]) for k, w in weights.items()}}
        y = _forward(args)
    with _jax.named_scope("loss_head"):
        err = _jnp.square(y.astype(_jnp.float32) - loss_target)
        return 0.5 * _jnp.sum(_jnp.mean(err, axis=-1)) if err.ndim else 0.5 * err


def _adamw(w, g, m, v):
    m = ADAM_B1 * m + (1.0 - ADAM_B1) * g
    v = ADAM_B2 * v + (1.0 - ADAM_B2) * _jnp.square(g)
    m_hat = m / (1.0 - ADAM_B1 ** ADAM_STEP)
    v_hat = v / (1.0 - ADAM_B2 ** ADAM_STEP)
    delta = -ADAM_LR * (m_hat / (_jnp.sqrt(v_hat) + ADAM_EPS) + ADAM_WD * w)
    return delta, m, v


def reference(x, attn_norm, attn_w_in, attn_f_bias, fox_q_gain, fox_k_gain, sb_q_gain, sb_k_gain, attn_w_out, conv_norm, conv_w_in, conv_kernel, conv_w_out, ffn_norm, ffn_w_up, ffn_conv, ffn_w_down, loss_target, m_attn_norm, m_attn_w_in, m_attn_f_bias, m_fox_q_gain, m_fox_k_gain, m_sb_q_gain, m_sb_k_gain, m_attn_w_out, m_conv_norm, m_conv_w_in, m_conv_kernel, m_conv_w_out, m_ffn_norm, m_ffn_w_up, m_ffn_conv, m_ffn_w_down, v_attn_norm, v_attn_w_in, v_attn_f_bias, v_fox_q_gain, v_fox_k_gain, v_sb_q_gain, v_sb_k_gain, v_attn_w_out, v_conv_norm, v_conv_w_in, v_conv_kernel, v_conv_w_out, v_ffn_norm, v_ffn_w_up, v_ffn_conv, v_ffn_w_down):
    given = dict(x=x, attn_norm=attn_norm, attn_w_in=attn_w_in, attn_f_bias=attn_f_bias, fox_q_gain=fox_q_gain, fox_k_gain=fox_k_gain, sb_q_gain=sb_q_gain, sb_k_gain=sb_k_gain, attn_w_out=attn_w_out, conv_norm=conv_norm, conv_w_in=conv_w_in, conv_kernel=conv_kernel, conv_w_out=conv_w_out, ffn_norm=ffn_norm, ffn_w_up=ffn_w_up, ffn_conv=ffn_conv, ffn_w_down=ffn_w_down, loss_target=loss_target, m_attn_norm=m_attn_norm, m_attn_w_in=m_attn_w_in, m_attn_f_bias=m_attn_f_bias, m_fox_q_gain=m_fox_q_gain, m_fox_k_gain=m_fox_k_gain, m_sb_q_gain=m_sb_q_gain, m_sb_k_gain=m_sb_k_gain, m_attn_w_out=m_attn_w_out, m_conv_norm=m_conv_norm, m_conv_w_in=m_conv_w_in, m_conv_kernel=m_conv_kernel, m_conv_w_out=m_conv_w_out, m_ffn_norm=m_ffn_norm, m_ffn_w_up=m_ffn_w_up, m_ffn_conv=m_ffn_conv, m_ffn_w_down=m_ffn_w_down, v_attn_norm=v_attn_norm, v_attn_w_in=v_attn_w_in, v_attn_f_bias=v_attn_f_bias, v_fox_q_gain=v_fox_q_gain, v_fox_k_gain=v_fox_k_gain, v_sb_q_gain=v_sb_q_gain, v_sb_k_gain=v_sb_k_gain, v_attn_w_out=v_attn_w_out, v_conv_norm=v_conv_norm, v_conv_w_in=v_conv_w_in, v_conv_kernel=v_conv_kernel, v_conv_w_out=v_conv_w_out, v_ffn_norm=v_ffn_norm, v_ffn_w_up=v_ffn_w_up, v_ffn_conv=v_ffn_conv, v_ffn_w_down=v_ffn_w_down)
    weights = {n: given[n] for n in TWIN_WEIGHTS}
    shared = {n: given[n] for n in SHARED_INPUTS}
    per_example = {n: given[n] for n in ['x']}
    grad_fn = _jax.value_and_grad(_loss, argnums=(0, 1))

    def one_microbatch(ex, loss_target):
        ex = dict(ex)
        diff = ex.pop(TWIN_DIFF_INPUT)
        return grad_fn(weights, diff, {**shared, **ex}, loss_target)

    if N_MICROBATCH == 1:
        loss, (grad_w, grad_x) = one_microbatch(per_example, given["loss_target"])
    else:
        def body(carry, xs):
            loss_sum, grad_sum = carry
            l_k, (gw_k, gx_k) = one_microbatch(xs[0], xs[1])
            with _jax.named_scope("update"):
                return (loss_sum + l_k, _jax.tree.map(_jnp.add, grad_sum, gw_k)), gx_k

        init = (_jnp.zeros((), _jnp.float32), _jax.tree.map(_jnp.zeros_like, weights))
        (loss, grad_w), grad_x = _jax.lax.scan(body, init, (per_example, given["loss_target"]))
    with _jax.named_scope("update"):
        delta_w, new_m, new_v = {}, {}, {}
        for n in TWIN_WEIGHTS:
            delta_w[n], new_m[n], new_v[n] = _adamw(weights[n], grad_w[n], given["m_" + n], given["v_" + n])
    return (loss, grad_x, *[grad_w[n] for n in TWIN_WEIGHTS], *[delta_w[n] for n in TWIN_WEIGHTS],
            *[new_m[n] for n in TWIN_WEIGHTS], *[new_v[n] for n in TWIN_WEIGHTS])
```

```python
import functools

import jax
import jax.numpy as jnp
from jax import lax
from jax.experimental import pallas as pl
from jax.experimental.pallas import tpu as pltpu

F32 = jnp.float32
BF16 = jnp.bfloat16

D_MODEL = 1024
HEAD_DIM = 64
H_FOX = 8
MIX = 1024
ATTN_IN = 3 * MIX + H_FOX
ATTN_IN_PAD = 3 * MIX + 128
D_FF = 2816
EPS = 1e-6
NEG = -1e30
LANES = 128
N_DEV = 8
N_CHIP = 4

ADAM_LR = 0.001
ADAM_B1 = 0.9
ADAM_B2 = 0.999
ADAM_EPS = 1e-08
ADAM_WD = 0.01
ADAM_STEP = 10

VMEM_LIMIT = 56 * 1024 * 1024
MESH = pl.DeviceIdType.MESH
ANY = pl.BlockSpec(memory_space=pl.ANY)


def _params(*sem):
    return pltpu.CompilerParams(dimension_semantics=sem, vmem_limit_bytes=VMEM_LIMIT)


def _tile(n, target, mult=LANES):
    best = None
    for t in range(mult, min(n, target) + 1, mult):
        if n % t == 0:
            best = t
    return best if best is not None else n


def _dot(a, b):
    return jnp.dot(a, b, preferred_element_type=F32)


def _dot_nt(a, b):
    return lax.dot_general(a, b, (((1,), (1,)), ((), ())), preferred_element_type=F32)


def _dot_tn(a, b):
    return lax.dot_general(a, b, (((0,), (0,)), ((), ())), preferred_element_type=F32)


def _split_dot(x, m, passes):
    acc = None
    rem = x
    for _ in range(passes):
        part = rem.astype(BF16)
        term = _dot(part, m)
        acc = term if acc is None else acc + term
        rem = rem - part.astype(F32)
    return acc


def _matmul(a, b, name, add=None, out_dtype=F32, tm=1024, tn=512, tk=1024):
    m, k = a.shape
    n = b.shape[1]
    tm, tn, tk = _tile(m, tm, 8), _tile(n, tn), _tile(k, tk)
    nk = k // tk
    has_add = add is not None

    def body(*refs):
        a_ref, b_ref = refs[0], refs[1]
        add_ref = refs[2] if has_add else None
        o_ref = refs[2 + has_add]

        def finish(acc):
            if has_add:
                acc = acc + add_ref[...]
            o_ref[...] = acc.astype(out_dtype)

        p = _dot(a_ref[...].astype(BF16), b_ref[...].astype(BF16))
        if nk == 1:
            finish(p)
        else:
            acc_ref = refs[-1]
            kk = pl.program_id(2)

            @pl.when(kk == 0)
            def _():
                acc_ref[...] = p

            @pl.when(kk > 0)
            def _():
                acc_ref[...] += p

            @pl.when(kk == nk - 1)
            def _():
                finish(acc_ref[...])

    in_specs = [pl.BlockSpec((tm, tk), lambda i, j, kk: (i, kk)),
                pl.BlockSpec((tk, tn), lambda i, j, kk: (kk, j))]
    args = [a, b]
    if has_add:
        in_specs.append(pl.BlockSpec((tm, tn), lambda i, j, kk: (i, j)))
        args.append(add)
    return pl.pallas_call(
        body, name=name, grid=(m // tm, n // tn, nk), in_specs=in_specs,
        out_specs=pl.BlockSpec((tm, tn), lambda i, j, kk: (i, j)),
        out_shape=jax.ShapeDtypeStruct((m, n), out_dtype),
        scratch_shapes=[pltpu.VMEM((tm, tn), F32)] if nk > 1 else [],
        compiler_params=_params("parallel", "parallel", "arbitrary"),
    )(*args)


def _matmul_tn(a, b, name, tm=1024, tn=512, ts=1024):
    s, m = a.shape
    n = b.shape[1]
    tm, tn, ts = _tile(m, tm), _tile(n, tn), _tile(s, ts, 8)

    def body(a_ref, b_ref, o_ref):
        kk = pl.program_id(2)
        p = _dot_tn(a_ref[...].astype(BF16), b_ref[...].astype(BF16))

        @pl.when(kk == 0)
        def _():
            o_ref[...] = p

        @pl.when(kk > 0)
        def _():
            o_ref[...] += p

    return pl.pallas_call(
        body, name=name, grid=(m // tm, n // tn, s // ts),
        in_specs=[pl.BlockSpec((ts, tm), lambda i, j, kk: (kk, i)),
                  pl.BlockSpec((ts, tn), lambda i, j, kk: (kk, j))],
        out_specs=pl.BlockSpec((tm, tn), lambda i, j, kk: (i, j)),
        out_shape=jax.ShapeDtypeStruct((m, n), F32),
        compiler_params=_params("parallel", "parallel", "arbitrary"),
    )(a, b)


def _rms_fwd(h, g, name, ts=512):
    s, d = h.shape
    ts = _tile(s, ts, 8)

    def body(h_ref, g_ref, o_ref):
        x = h_ref[...]
        r = lax.rsqrt(jnp.mean(x * x, axis=-1, keepdims=True) + EPS)
        o_ref[...] = (x * r * g_ref[...]).astype(BF16)

    return pl.pallas_call(
        body, name=name, grid=(s // ts,),
        in_specs=[pl.BlockSpec((ts, d), lambda i: (i, 0)), pl.BlockSpec((1, d), lambda i: (0, 0))],
        out_specs=pl.BlockSpec((ts, d), lambda i: (i, 0)),
        out_shape=jax.ShapeDtypeStruct((s, d), BF16),
        compiler_params=_params("parallel"),
    )(h, g.reshape(1, d))


def _rms_bwd(h, dxn, g, dh_in, name, ts=512):
    s, d = h.shape
    ts = _tile(s, ts, 8)

    def body(h_ref, dxn_ref, g_ref, dhin_ref, dh_ref, dg_ref):
        i = pl.program_id(0)
        x = h_ref[...]
        r = lax.rsqrt(jnp.mean(x * x, axis=-1, keepdims=True) + EPS)
        xh = x * r
        dxn_v = dxn_ref[...]

        @pl.when(i == 0)
        def _():
            dg_ref[...] = jnp.zeros_like(dg_ref)

        dg_ref[0:1, :] += jnp.sum(dxn_v * xh, axis=0, keepdims=True)
        dxh = dxn_v * g_ref[...]
        dx = r * (dxh - xh * jnp.mean(dxh * xh, axis=-1, keepdims=True))
        dh_ref[...] = dhin_ref[...] + dx

    row = pl.BlockSpec((ts, d), lambda i: (i, 0))
    dh, dg = pl.pallas_call(
        body, name=name, grid=(s // ts,),
        in_specs=[row, row, pl.BlockSpec((1, d), lambda i: (0, 0)), row],
        out_specs=[row, pl.BlockSpec((8, d), lambda i: (0, 0))],
        out_shape=[jax.ShapeDtypeStruct((s, d), F32), jax.ShapeDtypeStruct((8, d), F32)],
        compiler_params=_params("arbitrary"),
    )(h, dxn, g.reshape(1, d), dh_in)
    return dh, dg[0]


def _shift_down(x, prev):
    rows = lax.broadcasted_iota(jnp.int32, x.shape, 0)
    p1, p2 = prev[7:8, :], prev[6:7, :]
    x1 = jnp.where(rows == 0, p1, pltpu.roll(x, 1, 0))
    x2 = jnp.where(rows == 0, p2, jnp.where(rows == 1, p1, pltpu.roll(x, 2, 0)))
    return x1, x2


def _shift_up(x, nxt):
    n = x.shape[0]
    rows = lax.broadcasted_iota(jnp.int32, x.shape, 0)
    n0, n1 = nxt[0:1, :], nxt[1:2, :]
    x1 = jnp.where(rows == n - 1, n0, pltpu.roll(x, n - 1, 0))
    x2 = jnp.where(rows == n - 1, n1, jnp.where(rows == n - 2, n0, pltpu.roll(x, n - 2, 0)))
    return x1, x2


def _conv(x, x1, x2, w):
    return w[2:3, :] * x + w[1:2, :] * x1 + w[0:1, :] * x2


def _halo_specs(ts, tc, col, n_time_blocks):
    r8 = ts // 8
    main = pl.BlockSpec((ts, tc), lambda j, i: (i, j + col))
    prev = pl.BlockSpec((8, tc), lambda j, i: (jnp.maximum(i * r8 - 1, 0), j + col))
    nxt = pl.BlockSpec((8, tc), lambda j, i: (jnp.minimum((i + 1) * r8, n_time_blocks * r8 - 1), j + col))
    return main, prev, nxt


def _silu_parts(g):
    sig = 1.0 / (1.0 + jnp.exp(-g))
    return sig, g * sig


def _ffn_act_fwd(up, cw, name, ts=256, tc=1408):
    s = up.shape[0]
    ts, tc = _tile(s, ts, 8), _tile(D_FF, tc)
    nc, nt = D_FF // tc, s // ts

    def body(g_ref, gp_ref, v_ref, vp_ref, wg_ref, wv_ref, o_ref):
        first = pl.program_id(1) == 0

        def conv(x_ref, p_ref, w_ref):
            x = x_ref[...]
            prev = jnp.where(first, 0.0, p_ref[...])
            x1, x2 = _shift_down(x, prev)
            return _conv(x, x1, x2, w_ref[...])

        ug = conv(g_ref, gp_ref, wg_ref)
        uv = conv(v_ref, vp_ref, wv_ref)
        _, silu = _silu_parts(ug)
        o_ref[...] = (silu * uv).astype(BF16)

    g_main, g_prev, _ = _halo_specs(ts, tc, 0, nt)
    v_main, v_prev, _ = _halo_specs(ts, tc, nc, nt)
    return pl.pallas_call(
        body, name=name, grid=(nc, nt),
        in_specs=[g_main, g_prev, v_main, v_prev,
                  pl.BlockSpec((3, tc), lambda j, i: (0, j)), pl.BlockSpec((3, tc), lambda j, i: (0, j + nc))],
        out_specs=pl.BlockSpec((ts, tc), lambda j, i: (i, j)),
        out_shape=jax.ShapeDtypeStruct((s, D_FF), BF16),
        compiler_params=_params("parallel", "parallel"),
    )(up, up, up, up, cw, cw)


def _ffn_act_bwd(up, cw, da, name, ts=256, tc=1408):
    s = up.shape[0]
    ts, tc = _tile(s, ts, 8), _tile(D_FF, tc)
    nc, nt = D_FF // tc, s // ts

    def body(g_ref, gp_ref, gn_ref, v_ref, vp_ref, vn_ref, da_ref, dan_ref, wg_ref, wv_ref,
             dg_ref, dv_ref, dwg_ref, dwv_ref):
        i = pl.program_id(1)
        first, last = i == 0, i == nt - 1
        wg, wv = wg_ref[...], wv_ref[...]
        g, v = g_ref[...], v_ref[...]
        g1, g2 = _shift_down(g, jnp.where(first, 0.0, gp_ref[...]))
        v1, v2 = _shift_down(v, jnp.where(first, 0.0, vp_ref[...]))

        def d_u(ug, uv, da_v):
            sig, silu = _silu_parts(ug)
            return da_v * uv * (sig * (1.0 + ug * (1.0 - sig))), da_v * silu

        dug, duv = d_u(_conv(g, g1, g2, wg), _conv(v, v1, v2, wv), da_ref[...])
        gn, vn = gn_ref[...], vn_ref[...]
        gn1, gn2 = _shift_down(gn, g[ts - 8:, :])
        vn1, vn2 = _shift_down(vn, v[ts - 8:, :])
        dugn, duvn = d_u(_conv(gn, gn1, gn2, wg), _conv(vn, vn1, vn2, wv), dan_ref[...])
        dugn = jnp.where(last, 0.0, dugn)
        duvn = jnp.where(last, 0.0, duvn)

        def finish(du, dun, x, x1, x2, w, dx_ref, dw_ref):
            d1, d2 = _shift_up(du, dun)
            dx_ref[...] = (w[2:3, :] * du + w[1:2, :] * d1 + w[0:1, :] * d2).astype(BF16)

            @pl.when(first)
            def _():
                dw_ref[...] = jnp.zeros_like(dw_ref)

            dw_ref[0:1, :] += jnp.sum(du * x2, axis=0, keepdims=True)
            dw_ref[1:2, :] += jnp.sum(du * x1, axis=0, keepdims=True)
            dw_ref[2:3, :] += jnp.sum(du * x, axis=0, keepdims=True)

        finish(dug, dugn, g, g1, g2, wg, dg_ref, dwg_ref)
        finish(duv, duvn, v, v1, v2, wv, dv_ref, dwv_ref)

    g_specs = _halo_specs(ts, tc, 0, nt)
    v_specs = _halo_specs(ts, tc, nc, nt)
    da_main, _, da_next = _halo_specs(ts, tc, 0, nt)
    tile = pl.BlockSpec((ts, tc), lambda j, i: (i, j))
    taps = pl.BlockSpec((8, tc), lambda j, i: (0, j))
    dg, dv, dwg, dwv = pl.pallas_call(
        body, name=name, grid=(nc, nt),
        in_specs=[*g_specs, *v_specs, da_main, da_next,
                  pl.BlockSpec((3, tc), lambda j, i: (0, j)), pl.BlockSpec((3, tc), lambda j, i: (0, j + nc))],
        out_specs=[tile, tile, taps, taps],
        out_shape=[jax.ShapeDtypeStruct((s, D_FF), BF16), jax.ShapeDtypeStruct((s, D_FF), BF16),
                   jax.ShapeDtypeStruct((8, D_FF), F32), jax.ShapeDtypeStruct((8, D_FF), F32)],
        compiler_params=_params("parallel", "arbitrary"),
    )(up, up, up, up, up, up, da, da, cw, cw)
    return dg, dv, dwg[:3], dwv[:3]


def _sconv_fwd(proj, ck, name, ts=256, tc=512):
    s = proj.shape[0]
    w = D_MODEL
    ts, tc = _tile(s, ts, 8), _tile(w, tc)
    nc, nt = w // tc, s // ts

    def body(b_ref, c_ref, cp_ref, u_ref, up_ref, w_ref, o_ref):
        first = pl.program_id(1) == 0
        cu = c_ref[...] * u_ref[...]
        cup = jnp.where(first, 0.0, cp_ref[...] * up_ref[...])
        x1, x2 = _shift_down(cu, cup)
        o_ref[...] = (b_ref[...] * _conv(cu, x1, x2, w_ref[...])).astype(BF16)

    b_main, _, _ = _halo_specs(ts, tc, 0, nt)
    c_main, c_prev, _ = _halo_specs(ts, tc, nc, nt)
    u_main, u_prev, _ = _halo_specs(ts, tc, 2 * nc, nt)
    return pl.pallas_call(
        body, name=name, grid=(nc, nt),
        in_specs=[b_main, c_main, c_prev, u_main, u_prev, pl.BlockSpec((3, tc), lambda j, i: (0, j))],
        out_specs=pl.BlockSpec((ts, tc), lambda j, i: (i, j)),
        out_shape=jax.ShapeDtypeStruct((s, w), BF16),
        compiler_params=_params("parallel", "parallel"),
    )(proj, proj, proj, proj, proj, ck)


def _sconv_bwd(proj, ck, dy, name, ts=256, tc=512):
    s = proj.shape[0]
    w = D_MODEL
    ts, tc = _tile(s, ts, 8), _tile(w, tc)
    nc, nt = w // tc, s // ts

    def body(b_ref, bn_ref, c_ref, cp_ref, u_ref, up_ref, dy_ref, dyn_ref, w_ref,
             db_ref, dc_ref, du_ref, dw_ref):
        i = pl.program_id(1)
        first, last = i == 0, i == nt - 1
        wv = w_ref[...]
        b, c, u, dy_v = b_ref[...], c_ref[...], u_ref[...], dy_ref[...]
        cu = c * u
        cup = jnp.where(first, 0.0, cp_ref[...] * up_ref[...])
        x1, x2 = _shift_down(cu, cup)
        db_ref[...] = (dy_v * _conv(cu, x1, x2, wv)).astype(BF16)
        dcv = dy_v * b
        dcvn = jnp.where(last, 0.0, dyn_ref[...] * bn_ref[...])
        d1, d2 = _shift_up(dcv, dcvn)
        dcu = wv[2:3, :] * dcv + wv[1:2, :] * d1 + wv[0:1, :] * d2
        dc_ref[...] = (dcu * u).astype(BF16)
        du_ref[...] = (dcu * c).astype(BF16)

        @pl.when(first)
        def _():
            dw_ref[...] = jnp.zeros_like(dw_ref)

        dw_ref[0:1, :] += jnp.sum(dcv * x2, axis=0, keepdims=True)
        dw_ref[1:2, :] += jnp.sum(dcv * x1, axis=0, keepdims=True)
        dw_ref[2:3, :] += jnp.sum(dcv * cu, axis=0, keepdims=True)

    b_main, _, b_next = _halo_specs(ts, tc, 0, nt)
    c_main, c_prev, _ = _halo_specs(ts, tc, nc, nt)
    u_main, u_prev, _ = _halo_specs(ts, tc, 2 * nc, nt)
    dy_main, _, dy_next = _halo_specs(ts, tc, 0, nt)
    tile = pl.BlockSpec((ts, tc), lambda j, i: (i, j))
    out = jax.ShapeDtypeStruct((s, w), BF16)
    db, dc, du, dw = pl.pallas_call(
        body, name=name, grid=(nc, nt),
        in_specs=[b_main, b_next, c_main, c_prev, u_main, u_prev, dy_main, dy_next,
                  pl.BlockSpec((3, tc), lambda j, i: (0, j))],
        out_specs=[tile, tile, tile, pl.BlockSpec((8, tc), lambda j, i: (0, j))],
        out_shape=[out, out, out, jax.ShapeDtypeStruct((8, w), F32)],
        compiler_params=_params("parallel", "arbitrary"),
    )(proj, proj, proj, proj, proj, proj, dy, dy, ck)
    return db, dc, du, dw[:3]


def _low_lanes(shape):
    return lax.broadcasted_iota(jnp.int32, shape, 1) < HEAD_DIM


def _norm_pair(x):
    lo = _low_lanes(x.shape)
    sq = x * x
    s_lo = jnp.sum(jnp.where(lo, sq, 0.0), axis=1, keepdims=True)
    s_hi = jnp.sum(jnp.where(lo, 0.0, sq), axis=1, keepdims=True)
    r = lax.rsqrt(jnp.where(lo, s_lo, s_hi) * (1.0 / HEAD_DIM) + EPS)
    return x * r, r


def _mean_pair(x):
    lo = _low_lanes(x.shape)
    s_lo = jnp.sum(jnp.where(lo, x, 0.0), axis=1, keepdims=True)
    s_hi = jnp.sum(jnp.where(lo, 0.0, x), axis=1, keepdims=True)
    return jnp.where(lo, s_lo, s_hi) * (1.0 / HEAD_DIM)


def _qkv_prep(proj, gq, gk, name, ts=512):
    s = proj.shape[0]
    ts = _tile(s, ts, 8)
    npair = MIX // LANES
    scale = HEAD_DIM ** -0.5

    def body(q_ref, k_ref, v_ref, gq_ref, gk_ref, qo_ref, ko_ref, vo_ref):
        qn, _ = _norm_pair(q_ref[...])
        kn, _ = _norm_pair(k_ref[...])
        qo_ref[...] = (qn * gq_ref[0] * scale).astype(BF16)
        ko_ref[...] = (kn * gk_ref[0]).astype(BF16)
        vo_ref[...] = v_ref[...].astype(BF16)

    gain = pl.BlockSpec((1, 1, LANES), lambda i, p: (p, 0, 0))
    tile = pl.BlockSpec((ts, LANES), lambda i, p: (i, p))
    out = jax.ShapeDtypeStruct((s, MIX), BF16)
    return pl.pallas_call(
        body, name=name, grid=(s // ts, npair),
        in_specs=[tile, pl.BlockSpec((ts, LANES), lambda i, p: (i, p + npair)),
                  pl.BlockSpec((ts, LANES), lambda i, p: (i, p + 2 * npair)), gain, gain],
        out_specs=[tile, tile, tile], out_shape=[out, out, out],
        compiler_params=_params("parallel", "parallel"),
    )(proj, proj, proj, gq, gk)


def _qkv_prep_bwd(proj, gq, gk, dqs, dks, dvs, name, ts=512):
    s = proj.shape[0]
    ts = _tile(s, ts, 8)
    npair = MIX // LANES
    half = npair // 2
    scale = HEAD_DIM ** -0.5

    def body(q_ref, k_ref, gq_ref, gk_ref, dqf_ref, dqs_ref, dkf_ref, dks_ref, dvf_ref, dvs_ref,
             dq_ref, dk_ref, dv_ref, dgq_ref, dgk_ref):
        p, i = pl.program_id(0), pl.program_id(1)
        fox = p < half

        def one(x_ref, g_ref, df_ref, ds_ref, dx_ref, dg_ref, mult):
            dn = jnp.where(fox, df_ref[...], ds_ref[...]) * mult
            xh, r = _norm_pair(x_ref[...])

            @pl.when(i == 0)
            def _():
                dg_ref[...] = jnp.zeros_like(dg_ref)

            dg_ref[0, 0:1, :] += jnp.sum(dn * xh, axis=0, keepdims=True)
            dxh = dn * g_ref[0]
            dx_ref[...] = (r * (dxh - xh * _mean_pair(dxh * xh))).astype(BF16)

        one(q_ref, gq_ref, dqf_ref, dqs_ref, dq_ref, dgq_ref, scale)
        one(k_ref, gk_ref, dkf_ref, dks_ref, dk_ref, dgk_ref, 1.0)
        dv_ref[...] = jnp.where(fox, dvf_ref[...], dvs_ref[...]).astype(BF16)

    gain = pl.BlockSpec((1, 1, LANES), lambda p, i: (p, 0, 0))
    tile = pl.BlockSpec((ts, LANES), lambda p, i: (i, p))
    fpart = pl.BlockSpec((ts, LANES), lambda p, i: (i, jnp.minimum(p, half - 1)))
    spart = pl.BlockSpec((ts, LANES), lambda p, i: (i, jnp.maximum(p - half, 0)))
    dgain = pl.BlockSpec((1, 8, LANES), lambda p, i: (p, 0, 0))
    out = jax.ShapeDtypeStruct((s, MIX), BF16)
    gshape = jax.ShapeDtypeStruct((npair, 8, LANES), F32)
    dq, dk, dv, dgq, dgk = pl.pallas_call(
        body, name=name, grid=(npair, s // ts),
        in_specs=[tile, pl.BlockSpec((ts, LANES), lambda p, i: (i, p + npair)), gain, gain,
                  fpart, spart, fpart, spart, fpart, spart],
        out_specs=[tile, tile, tile, dgain, dgain], out_shape=[out, out, out, gshape, gshape],
        compiler_params=_params("parallel", "arbitrary"),
    )(proj, proj, gq, gk, dqs[0], dqs[1], dks[0], dks[1], dvs[0], dvs[1])
    return dq, dk, dv, dgq[:, 0, :], dgk[:, 0, :]


def _tri(n, rel):
    a = lax.broadcasted_iota(jnp.int32, (n, n), 0)
    b = lax.broadcasted_iota(jnp.int32, (n, n), 1)
    return rel(a, b).astype(BF16)


def _fgate_fwd(logit, bias, name):
    nh, r, _ = logit.shape

    def body(x_ref, b_ref, o_ref):
        within = _tri(LANES, lambda a, b: a <= b)
        before = _tri(r, lambda a, b: b < a)
        for hh in range(nh):
            x = x_ref[hh] + b_ref[hh]
            lf = jnp.minimum(x, 0.0) - jnp.log1p(jnp.exp(-jnp.abs(x)))
            c = _split_dot(lf, within, 3)
            tot = jnp.broadcast_to(c[:, LANES - 1:LANES], (r, LANES))
            o_ref[hh] = c + _split_dot_left(before, tot, 3)

    return pl.pallas_call(
        body, name=name, out_shape=jax.ShapeDtypeStruct((nh, r, LANES), F32),
        in_specs=[pl.BlockSpec(memory_space=pltpu.VMEM), pl.BlockSpec(memory_space=pltpu.SMEM)],
        out_specs=pl.BlockSpec(memory_space=pltpu.VMEM),
    )(logit, bias)


def _split_dot_left(m, x, passes):
    acc = None
    rem = x
    for _ in range(passes):
        part = rem.astype(BF16)
        term = _dot(m, part)
        acc = term if acc is None else acc + term
        rem = rem - part.astype(F32)
    return acc


def _fgate_bwd(logit, bias, dcum, name):
    nh, r, _ = logit.shape

    def body(x_ref, b_ref, d_ref, dx_ref, db_ref):
        within = _tri(LANES, lambda a, b: a >= b)
        after = _tri(r, lambda a, b: b > a)
        for hh in range(nh):
            x = x_ref[hh] + b_ref[hh]
            d = d_ref[hh]
            c = _split_dot(d, within, 3)
            tot = jnp.broadcast_to(c[:, 0:1], (r, LANES))
            dlf = c + _split_dot_left(after, tot, 3)
            dx = dlf * (1.0 / (1.0 + jnp.exp(x)))
            dx_ref[hh] = dx
            db_ref[hh:hh + 1, :] = jnp.broadcast_to(jnp.sum(dx, keepdims=True).reshape(1, 1), (1, LANES))

    return pl.pallas_call(
        body, name=name,
        out_shape=[jax.ShapeDtypeStruct((nh, r, LANES), F32), jax.ShapeDtypeStruct((nh, LANES), F32)],
        in_specs=[pl.BlockSpec(memory_space=pltpu.VMEM), pl.BlockSpec(memory_space=pltpu.SMEM),
                  pl.BlockSpec(memory_space=pltpu.VMEM)],
        out_specs=[pl.BlockSpec(memory_space=pltpu.VMEM), pl.BlockSpec(memory_space=pltpu.VMEM)],
    )(logit, bias, dcum)


def _pair_masks(x):
    lo = _low_lanes(x.shape)
    zero = jnp.zeros_like(x)
    return jnp.where(lo, x, zero), jnp.where(lo, zero, x)


def _pair_rowsum(x):
    lo = _low_lanes(x.shape)
    return (jnp.sum(jnp.where(lo, x, 0.0), axis=1, keepdims=True),
            jnp.sum(jnp.where(lo, 0.0, x), axis=1, keepdims=True))


def _attn_specs(s, tq, col0):
    qtile = pl.BlockSpec((tq, LANES), lambda p, i: (i, p + col0))
    resident = pl.BlockSpec((s, LANES), lambda p, i: (0, p + col0))
    return qtile, resident


def _fox_fwd(qh, kh, vb, frow, fcol, name, tq=256):
    s = qh.shape[0]
    tq = _tile(s, tq)
    nq, half = s // tq, MIX // LANES // 2

    def body(q_ref, k_ref, v_ref, fr_ref, fc_ref, o_ref, lse_ref, m_s, l_s, acc_s):
        i = pl.program_id(1)
        qs = _pair_masks(q_ref[...])
        ft = fc_ref[0]
        causal = (lax.broadcasted_iota(jnp.int32, (tq, tq), 1) <= lax.broadcasted_iota(jnp.int32, (tq, tq), 0))
        m_s[...] = jnp.full(m_s.shape, NEG, F32)
        l_s[...] = jnp.zeros_like(l_s)
        acc_s[...] = jnp.zeros_like(acc_s)

        def step(j, masked):
            off = pl.multiple_of(j * tq, tq)
            k, v = k_ref[pl.ds(off, tq), :], v_ref[pl.ds(off, tq), :]
            fs = fr_ref[0, :, pl.ds(off, tq)]
            for hh in range(2):
                sc = _dot_nt(qs[hh], k) + (ft[:, hh:hh + 1] - fs[hh:hh + 1, :])
                if masked:
                    sc = jnp.where(causal, sc, NEG)
                m_old = m_s[hh]
                m_new = jnp.maximum(m_old, jnp.max(sc, axis=1, keepdims=True))
                alpha = jnp.exp(m_old - m_new)
                pr = jnp.exp(sc - m_new)
                l_s[hh] = alpha * l_s[hh] + jnp.sum(pr, axis=1, keepdims=True)
                acc_s[hh] = alpha * acc_s[hh] + _dot(pr.astype(BF16), v)
                m_s[hh] = m_new

        def loop(j, carry):
            step(j, False)
            return carry

        lax.fori_loop(0, i, loop, 0)
        step(i, True)
        lo = _low_lanes((tq, LANES))
        o_ref[...] = jnp.where(lo, acc_s[0] / l_s[0], acc_s[1] / l_s[1])
        lse_ref[0, :, 0:1] = m_s[0] + jnp.log(l_s[0])
        lse_ref[0, :, 1:2] = m_s[1] + jnp.log(l_s[1])

    qtile, resident = _attn_specs(s, tq, 0)
    stat = pl.BlockSpec((1, tq, 2), lambda p, i: (p, i, 0))
    return pl.pallas_call(
        body, name=name, grid=(half, nq),
        in_specs=[qtile, resident, resident, pl.BlockSpec((1, 2, s), lambda p, i: (p, 0, 0)), stat],
        out_specs=[pl.BlockSpec((tq, LANES), lambda p, i: (i, p)), stat],
        out_shape=[jax.ShapeDtypeStruct((s, MIX // 2), F32), jax.ShapeDtypeStruct((half, s, 2), F32)],
        scratch_shapes=[pltpu.VMEM((2, tq, 1), F32), pltpu.VMEM((2, tq, 1), F32), pltpu.VMEM((2, tq, LANES), F32)],
        compiler_params=_params("parallel", "arbitrary"),
    )(qh, kh, vb, frow, fcol)


def _fox_bwd(qh, kh, vb, frow, fcol, lse, o, do, name, tq=256):
    s = qh.shape[0]
    tq = _tile(s, tq)
    nq, half = s // tq, MIX // LANES // 2

    def body(q_ref, k_ref, v_ref, fr_ref, fc_ref, lse_ref, o_ref, do_ref,
             dq_ref, dk_ref, dv_ref, df_ref, dfq_ref, dq_s, rs_s):
        i = pl.program_id(1)

        @pl.when(i == 0)
        def _():
            dk_ref[...] = jnp.zeros_like(dk_ref)
            dv_ref[...] = jnp.zeros_like(dv_ref)
            df_ref[...] = jnp.zeros_like(df_ref)

        rs_s[...] = jnp.zeros_like(rs_s)

        qs = _pair_masks(q_ref[...])
        do_v = do_ref[...]
        dos = _pair_masks(do_v.astype(BF16))
        dsum = _pair_rowsum(do_v * o_ref[...])
        ft, ls = fc_ref[0], lse_ref[0]
        causal = (lax.broadcasted_iota(jnp.int32, (tq, tq), 1) <= lax.broadcasted_iota(jnp.int32, (tq, tq), 0))
        dq_s[...] = jnp.zeros_like(dq_s)

        def step(j, masked):
            off = pl.multiple_of(j * tq, tq)
            rows = pl.ds(off, tq)
            k, v = k_ref[rows, :], v_ref[rows, :]
            ks = _pair_masks(k)
            fs = fr_ref[0, :, rows]
            for hh in range(2):
                sc = _dot_nt(qs[hh], k) + (ft[:, hh:hh + 1] - fs[hh:hh + 1, :])
                pr = jnp.exp(sc - ls[:, hh:hh + 1])
                if masked:
                    pr = jnp.where(causal, pr, 0.0)
                dp = _dot_nt(dos[hh], v)
                ds = pr * (dp - dsum[hh])
                dsb = ds.astype(BF16)
                dv_ref[rows, :] += _dot_tn(pr.astype(BF16), dos[hh])
                dk_ref[rows, :] += _dot_tn(dsb, qs[hh])
                dq_s[...] += _dot(dsb, ks[hh])
                df_ref[0, hh:hh + 1, rows] += -jnp.sum(ds, axis=0, keepdims=True)
                rs_s[hh] += jnp.sum(ds, axis=1, keepdims=True)

        def loop(j, carry):
            step(j, False)
            return carry

        lax.fori_loop(0, i, loop, 0)
        step(i, True)
        dq_ref[...] = dq_s[...]
        dfq_ref[0, :, 0:1] = rs_s[0]
        dfq_ref[0, :, 1:2] = rs_s[1]

    qtile, resident = _attn_specs(s, tq, 0)
    stat = pl.BlockSpec((1, tq, 2), lambda p, i: (p, i, 0))
    frow_spec = pl.BlockSpec((1, 2, s), lambda p, i: (p, 0, 0))
    otile = pl.BlockSpec((tq, LANES), lambda p, i: (i, p))
    oresident = pl.BlockSpec((s, LANES), lambda p, i: (0, p))
    out = jax.ShapeDtypeStruct((s, MIX // 2), F32)
    return pl.pallas_call(
        body, name=name, grid=(half, nq),
        in_specs=[qtile, resident, resident, frow_spec, stat, stat, otile, qtile],
        out_specs=[otile, oresident, oresident, frow_spec, stat],
        out_shape=[out, out, out, jax.ShapeDtypeStruct((half, 2, s), F32), jax.ShapeDtypeStruct((half, s, 2), F32)],
        scratch_shapes=[pltpu.VMEM((tq, LANES), F32), pltpu.VMEM((2, tq, 1), F32)],
        compiler_params=_params("parallel", "arbitrary"),
    )(qh, kh, vb, frow, fcol, lse, o, do)


def _log_sig_pair(z):
    t = jnp.log1p(jnp.exp(-jnp.abs(z)))
    return jnp.minimum(z, 0.0) - t, jnp.minimum(-z, 0.0) - t


def _sb_fwd(qh, kh, vb, name, tq=256):
    s = qh.shape[0]
    tq = _tile(s, tq)
    nq, half = s // tq, MIX // LANES // 2

    def body(q_ref, k_ref, v_ref, o_ref, tot_ref, c_s, acc_s):
        i = pl.program_id(1)
        qs = _pair_masks(q_ref[...])
        strict = (lax.broadcasted_iota(jnp.int32, (tq, tq), 1) < lax.broadcasted_iota(jnp.int32, (tq, tq), 0))
        later = _tri(tq, lambda a, b: a > b)
        c_s[...] = jnp.zeros_like(c_s)
        acc_s[...] = jnp.zeros_like(acc_s)

        def step(j, masked):
            off = pl.multiple_of(j * tq, tq)
            k, v = k_ref[pl.ds(off, tq), :], v_ref[pl.ds(off, tq), :]
            for hh in range(2):
                lb, lom = _log_sig_pair(_dot_nt(qs[hh], k))
                if masked:
                    lom = jnp.where(strict, lom, 0.0)
                w = jnp.exp(lb + _split_dot(lom, later, 2) + c_s[hh])
                if masked:
                    w = jnp.where(strict, w, 0.0)
                acc_s[hh] += _dot(w.astype(BF16), v)
                c_s[hh] += jnp.sum(lom, axis=1, keepdims=True)

        def loop(jj, carry):
            step(i - 1 - jj, False)
            return carry

        step(i, True)
        lax.fori_loop(0, i, loop, 0)
        o_ref[...] = jnp.where(_low_lanes((tq, LANES)), acc_s[0], acc_s[1])
        tot_ref[0, :, 0:1] = c_s[0]
        tot_ref[0, :, 1:2] = c_s[1]

    qtile, resident = _attn_specs(s, tq, half)
    stat = pl.BlockSpec((1, tq, 2), lambda p, i: (p, i, 0))
    return pl.pallas_call(
        body, name=name, grid=(half, nq),
        in_specs=[qtile, resident, resident],
        out_specs=[pl.BlockSpec((tq, LANES), lambda p, i: (i, p)), stat],
        out_shape=[jax.ShapeDtypeStruct((s, MIX // 2), F32), jax.ShapeDtypeStruct((half, s, 2), F32)],
        scratch_shapes=[pltpu.VMEM((2, tq, 1), F32), pltpu.VMEM((2, tq, LANES), F32)],
        compiler_params=_params("parallel", "arbitrary"),
    )(qh, kh, vb)


def _sb_bwd(qh, kh, vb, tot, do, name, tq=256):
    s = qh.shape[0]
    tq = _tile(s, tq)
    nq, half = s // tq, MIX // LANES // 2

    def body(q_ref, k_ref, v_ref, tot_ref, do_ref, dq_ref, dk_ref, dv_ref, rem_s, pg_s, dq_s):
        i = pl.program_id(1)

        @pl.when(i == 0)
        def _():
            dk_ref[...] = jnp.zeros_like(dk_ref)
            dv_ref[...] = jnp.zeros_like(dv_ref)

        qs = _pair_masks(q_ref[...])
        dos = _pair_masks(do_ref[...].astype(BF16))
        strict = (lax.broadcasted_iota(jnp.int32, (tq, tq), 1) < lax.broadcasted_iota(jnp.int32, (tq, tq), 0))
        upto = _tri(tq, lambda a, b: a <= b)
        before = _tri(tq, lambda a, b: a < b)
        tv = tot_ref[0]
        rem_s[0] = tv[:, 0:1]
        rem_s[1] = tv[:, 1:2]
        pg_s[...] = jnp.zeros_like(pg_s)
        dq_s[...] = jnp.zeros_like(dq_s)

        def step(j, masked):
            off = pl.multiple_of(j * tq, tq)
            rows = pl.ds(off, tq)
            k, v = k_ref[rows, :], v_ref[rows, :]
            ks = _pair_masks(k)
            for hh in range(2):
                lb, lom = _log_sig_pair(_dot_nt(qs[hh], k))
                if masked:
                    lom = jnp.where(strict, lom, 0.0)
                w = jnp.exp(lb + (rem_s[hh] - _split_dot(lom, upto, 2)))
                if masked:
                    w = jnp.where(strict, w, 0.0)
                g = _dot_nt(dos[hh], v) * w
                cg = pg_s[hh] + _split_dot(g, before, 2)
                dz = g - jnp.exp(lb) * (g + cg)
                if masked:
                    dz = jnp.where(strict, dz, 0.0)
                dzb = dz.astype(BF16)
                dv_ref[rows, :] += _dot_tn(w.astype(BF16), dos[hh])
                dk_ref[rows, :] += _dot_tn(dzb, qs[hh])
                dq_s[...] += _dot(dzb, ks[hh])
                rem_s[hh] -= jnp.sum(lom, axis=1, keepdims=True)
                pg_s[hh] += jnp.sum(g, axis=1, keepdims=True)

        def loop(j, carry):
            step(j, False)
            return carry

        lax.fori_loop(0, i, loop, 0)
        step(i, True)
        dq_ref[...] = dq_s[...]

    qtile, resident = _attn_specs(s, tq, half)
    stat = pl.BlockSpec((1, tq, 2), lambda p, i: (p, i, 0))
    otile = pl.BlockSpec((tq, LANES), lambda p, i: (i, p))
    oresident = pl.BlockSpec((s, LANES), lambda p, i: (0, p))
    out = jax.ShapeDtypeStruct((s, MIX // 2), F32)
    return pl.pallas_call(
        body, name=name, grid=(half, nq),
        in_specs=[qtile, resident, resident, stat, qtile],
        out_specs=[otile, oresident, oresident], out_shape=[out, out, out],
        scratch_shapes=[pltpu.VMEM((2, tq, 1), F32), pltpu.VMEM((2, tq, 1), F32), pltpu.VMEM((tq, LANES), F32)],
        compiler_params=_params("parallel", "arbitrary"),
    )(qh, kh, vb, tot, do)


def _loss_head(y, target, name, ts=512):
    s, d = y.shape
    ts = _tile(s, ts, 8)
    nt = s // ts

    def body(y_ref, t_ref, dy_ref, l_ref, acc):
        i = pl.program_id(0)
        err = y_ref[...] - t_ref[...]
        dy_ref[...] = err * (1.0 / d)

        @pl.when(i == 0)
        def _():
            acc[...] = jnp.zeros_like(acc)

        acc[...] += jnp.sum(err * err, axis=0, keepdims=True)

        @pl.when(i == nt - 1)
        def _():
            tot = jnp.sum(acc[...], keepdims=True).reshape(1, 1) * (0.5 / d)
            l_ref[...] = jnp.broadcast_to(tot, l_ref.shape)

    row = pl.BlockSpec((ts, d), lambda i: (i, 0))
    dy, l = pl.pallas_call(
        body, name=name, grid=(nt,), in_specs=[row, row],
        out_specs=[row, pl.BlockSpec((8, LANES), lambda i: (0, 0))],
        out_shape=[jax.ShapeDtypeStruct((s, d), F32), jax.ShapeDtypeStruct((8, LANES), F32)],
        scratch_shapes=[pltpu.VMEM((1, d), F32)],
        compiler_params=_params("arbitrary"),
    )(y, target)
    return l[0, 0], dy


def _coords():
    return lax.axis_index("x"), lax.axis_index("y"), lax.axis_index("c")


def _all_gather(x, name):
    r, c = x.shape

    def body(x_ref, out_ref, send_sems, recv_sems, local_sem):
        xi, yi, ci = _coords()
        me, sibling = (xi, yi, ci), (xi, yi, 1 - ci)
        chips = [(1 - xi, yi), (xi, 1 - yi), (1 - xi, 1 - yi)]

        def slot(px, py, pc):
            return out_ref.at[4 * px + 2 * py + pc]

        def copy(k, block, to, src=None):
            return pltpu.make_async_remote_copy(
                src_ref=slot(*block) if src is None else src, dst_ref=slot(*block),
                send_sem=send_sems.at[k], recv_sem=recv_sems.at[k], device_id=to, device_id_type=MESH)

        mine = pltpu.make_async_copy(x_ref, slot(*me), local_sem)
        mine.start()
        first = [copy(0, me, sibling, src=x_ref)]
        first += [copy(1 + j, me, (*chip, ci), src=x_ref) for j, chip in enumerate(chips)]
        for cp in first:
            cp.start()
        passed = [copy(4 + j, (*chip, ci), sibling) for j, chip in enumerate(chips)]
        for j, chip in enumerate(chips):
            copy(1 + j, (*chip, ci), me).wait_recv()
            passed[j].start()
        copy(0, sibling, me).wait_recv()
        for j, chip in enumerate(chips):
            copy(4 + j, (*chip, 1 - ci), me).wait_recv()
        for cp in first + passed:
            cp.wait_send()
        mine.wait()

    return pl.pallas_call(
        body, name=name, out_shape=jax.ShapeDtypeStruct((N_DEV, r, c), x.dtype),
        in_specs=[ANY], out_specs=ANY,
        scratch_shapes=[pltpu.SemaphoreType.DMA((7,)), pltpu.SemaphoreType.DMA((7,)), pltpu.SemaphoreType.DMA],
    )(x)


def _sibling_exchange(g2, name):
    _, nchip, r, c = g2.shape

    def body(g_ref, recv_ref, send_sem, recv_sem):
        xi, yi, ci = _coords()
        cp = pltpu.make_async_remote_copy(
            src_ref=g_ref.at[1 - ci], dst_ref=recv_ref, send_sem=send_sem, recv_sem=recv_sem,
            device_id=(xi, yi, 1 - ci), device_id_type=MESH)
        cp.start()
        cp.wait()

    return pl.pallas_call(
        body, name=name, out_shape=jax.ShapeDtypeStruct((nchip, r, c), g2.dtype),
        in_specs=[ANY], out_specs=ANY,
        scratch_shapes=[pltpu.SemaphoreType.DMA, pltpu.SemaphoreType.DMA],
    )(g2)


def _pair_add(g2, recv, ids, name, tr=256):
    _, nchip, r, c = g2.shape
    tr = _tile(r, tr, 16)

    def body(ids_ref, g_ref, r_ref, p_ref, own_ref):
        kk = pl.program_id(1)
        tot = g_ref[0, 0].astype(F32) + r_ref[0].astype(F32)
        p_ref[0] = tot.astype(BF16)

        @pl.when(kk == ids_ref[1])
        def _():
            own_ref[...] = tot

    grid_spec = pltpu.PrefetchScalarGridSpec(
        num_scalar_prefetch=1, grid=(r // tr, nchip),
        in_specs=[pl.BlockSpec((1, 1, tr, c), lambda i, kk, ids: (ids[0], kk, i, 0)),
                  pl.BlockSpec((1, tr, c), lambda i, kk, ids: (kk, i, 0))],
        out_specs=[pl.BlockSpec((1, tr, c), lambda i, kk, ids: (kk, i, 0)),
                   pl.BlockSpec((tr, c), lambda i, kk, ids: (i, 0))])
    return pl.pallas_call(
        body, name=name, grid_spec=grid_spec,
        out_shape=[jax.ShapeDtypeStruct((nchip, r, c), BF16), jax.ShapeDtypeStruct((r, c), F32)],
        compiler_params=_params("parallel", "arbitrary"),
    )(ids, g2, recv)


def _chip_exchange(p, name):
    nchip, r, c = p.shape

    def body(p_ref, recv_ref, send_sems, recv_sems):
        xi, yi, ci = _coords()
        mine = 2 * xi + yi
        chips = [(1 - xi, yi), (xi, 1 - yi), (1 - xi, 1 - yi)]

        def copy(k, cx, cy):
            return pltpu.make_async_remote_copy(
                src_ref=p_ref.at[2 * cx + cy], dst_ref=recv_ref.at[mine],
                send_sem=send_sems.at[k], recv_sem=recv_sems.at[k], device_id=(cx, cy, ci), device_id_type=MESH)

        def landed(k, cx, cy):
            return pltpu.make_async_remote_copy(
                src_ref=p_ref.at[mine], dst_ref=recv_ref.at[2 * cx + cy],
                send_sem=send_sems.at[k], recv_sem=recv_sems.at[k], device_id=(cx, cy, ci), device_id_type=MESH)

        sends = [copy(k, cx, cy) for k, (cx, cy) in enumerate(chips)]
        for cp in sends:
            cp.start()
        for k, (cx, cy) in enumerate(chips):
            landed(k, cx, cy).wait_recv()
        for cp in sends:
            cp.wait_send()

    return pl.pallas_call(
        body, name=name, out_shape=jax.ShapeDtypeStruct((nchip, r, c), p.dtype),
        in_specs=[ANY], out_specs=ANY,
        scratch_shapes=[pltpu.SemaphoreType.DMA((3,)), pltpu.SemaphoreType.DMA((3,))],
    )(p)


def _adamw_math(w, g, m, v):
    m = ADAM_B1 * m + (1.0 - ADAM_B1) * g
    v = ADAM_B2 * v + (1.0 - ADAM_B2) * (g * g)
    m_hat = m / (1.0 - ADAM_B1 ** ADAM_STEP)
    v_hat = v / (1.0 - ADAM_B2 ** ADAM_STEP)
    delta = -ADAM_LR * (m_hat / (jnp.sqrt(v_hat) + ADAM_EPS) + ADAM_WD * w)
    return delta, m, v


def _adamw_reduce(own, recv, ids, w, m, v, name, tr=256):
    r, c = w.shape
    nchip = recv.shape[0]
    tr = _tile(r, tr, 16)

    def body(ids_ref, own_ref, recv_ref, w_ref, m_ref, v_ref, g_ref, d_ref, mo_ref, vo_ref):
        mine = ids_ref[1]
        g = None
        for kk in range(nchip):
            term = jnp.where(mine == kk, own_ref[...], recv_ref[kk].astype(F32))
            g = term if g is None else g + term
        delta, m_new, v_new = _adamw_math(w_ref[...], g, m_ref[...], v_ref[...])
        g_ref[...] = g
        d_ref[...] = delta
        mo_ref[...] = m_new
        vo_ref[...] = v_new

    row = pl.BlockSpec((tr, c), lambda i, ids: (i, 0))
    grid_spec = pltpu.PrefetchScalarGridSpec(
        num_scalar_prefetch=1, grid=(r // tr,),
        in_specs=[row, pl.BlockSpec((nchip, tr, c), lambda i, ids: (0, i, 0)), row, row, row],
        out_specs=[row, row, row, row])
    out = jax.ShapeDtypeStruct((r, c), F32)
    return pl.pallas_call(
        body, name=name, grid_spec=grid_spec, out_shape=[out, out, out, out],
        compiler_params=_params("parallel"),
    )(ids, own, recv, w, m, v)


def _sum_sources(a, name):
    n, r, c = a.shape

    def body(a_ref, o_ref):
        tot = a_ref[0]
        for kk in range(1, n):
            tot = tot + a_ref[kk]
        o_ref[...] = tot

    return pl.pallas_call(
        body, name=name, out_shape=jax.ShapeDtypeStruct((r, c), F32),
        in_specs=[pl.BlockSpec(memory_space=pltpu.VMEM)], out_specs=pl.BlockSpec(memory_space=pltpu.VMEM),
    )(a)


def _adamw_small(w, g, m, v, name):
    def body(w_ref, g_ref, m_ref, v_ref, d_ref, mo_ref, vo_ref):
        delta, m_new, v_new = _adamw_math(w_ref[...], g_ref[...], m_ref[...], v_ref[...])
        d_ref[...] = delta
        mo_ref[...] = m_new
        vo_ref[...] = v_new

    vm = pl.BlockSpec(memory_space=pltpu.VMEM)
    out = jax.ShapeDtypeStruct(w.shape, F32)
    return pl.pallas_call(body, name=name, out_shape=[out, out, out], in_specs=[vm] * 4, out_specs=[vm] * 3)(w, g, m, v)


def _pack(parts, width, row_mult):
    flat = jnp.concatenate([p.reshape(-1) for p in parts])
    rows = -(-flat.shape[0] // width)
    rows = -(-rows // row_mult) * row_mult
    return jnp.pad(flat, (0, rows * width - flat.shape[0])).reshape(rows, width)


def _pack_per_device(parts, width, row_mult):
    flat = jnp.concatenate([p.reshape(N_DEV, -1) for p in parts], axis=1)
    rows = -(-flat.shape[1] // width)
    rows = -(-rows // row_mult) * row_mult
    return jnp.pad(flat, ((0, 0), (0, rows * width - flat.shape[1]))).reshape(N_DEV, rows, width)


def _unpack(flat, shapes):
    out, off = [], 0
    lead = flat.shape[:-1]
    for shp in shapes:
        n = 1
        for dd in shp:
            n *= dd
        out.append(flat[..., off:off + n].reshape(lead + tuple(shp)))
        off += n
    return out


def _cols_to_dev(g):
    l, k, n = g.shape
    return g.reshape(l, k, N_DEV, n // N_DEV).transpose(2, 0, 1, 3)


def _rows_to_dev(g):
    l, k, n = g.shape
    return g.reshape(l, N_DEV, k // N_DEV, n).transpose(1, 0, 2, 3)


def _dev_to_cols(a):
    _, l, k, cs = a.shape
    return a.transpose(1, 2, 0, 3).reshape(l, k, N_DEV * cs)


def _dev_to_rows(a):
    _, l, rs, n = a.shape
    return a.transpose(1, 0, 2, 3).reshape(l, N_DEV * rs, n)


def kernel(x, attn_norm, attn_w_in, attn_f_bias, fox_q_gain, fox_k_gain, sb_q_gain, sb_k_gain, attn_w_out, conv_norm, conv_w_in, conv_kernel, conv_w_out, ffn_norm, ffn_w_up, ffn_conv, ffn_w_down, loss_target, m_attn_norm, m_attn_w_in, m_attn_f_bias, m_fox_q_gain, m_fox_k_gain, m_sb_q_gain, m_sb_k_gain, m_attn_w_out, m_conv_norm, m_conv_w_in, m_conv_kernel, m_conv_w_out, m_ffn_norm, m_ffn_w_up, m_ffn_conv, m_ffn_w_down, v_attn_norm, v_attn_w_in, v_attn_f_bias, v_fox_q_gain, v_fox_k_gain, v_sb_q_gain, v_sb_k_gain, v_attn_w_out, v_conv_norm, v_conv_w_in, v_conv_kernel, v_conv_w_out, v_ffn_norm, v_ffn_w_up, v_ffn_conv, v_ffn_w_down):
    s = x.shape[1]
    n_attn, n_conv, depth = attn_w_in.shape[0], conv_w_in.shape[0], ffn_w_up.shape[0]
    xi, yi, ci = _coords()
    me = 4 * xi + 2 * yi + ci
    ids = jnp.stack([ci, 2 * xi + yi]).astype(jnp.int32)

    big = [attn_w_in, attn_w_out, conv_w_in, conv_w_out, ffn_w_up, ffn_w_down]
    big_m = [m_attn_w_in, m_attn_w_out, m_conv_w_in, m_conv_w_out, m_ffn_w_up, m_ffn_w_down]
    big_v = [v_attn_w_in, v_attn_w_out, v_conv_w_in, v_conv_w_out, v_ffn_w_up, v_ffn_w_down]
    big_shapes = [w.shape for w in big]
    small_sh = [conv_norm, conv_kernel, ffn_conv]
    small_sh_shapes = [w.shape for w in small_sh]
    rep = [attn_norm, attn_f_bias, fox_q_gain, fox_k_gain, sb_q_gain, sb_k_gain, ffn_norm]
    rep_shapes = [w.shape for w in rep]

    w_pack = _pack(big, D_MODEL, 16)
    gathered = _all_gather(w_pack.astype(BF16), "gather_weights")
    g_flat = gathered.reshape(N_DEV, -1)
    a_w_in, a_w_out, c_w_in, c_w_out, f_w_up, f_w_down = _unpack(g_flat, big_shapes)
    a_w_in = _dev_to_cols(a_w_in)
    a_w_in = jnp.pad(a_w_in, ((0, 0), (0, 0), (0, ATTN_IN_PAD - ATTN_IN)))
    a_w_out = _dev_to_rows(a_w_out)
    c_w_in = _dev_to_cols(c_w_in)
    c_w_out = _dev_to_rows(c_w_out)
    f_w_up = _dev_to_cols(f_w_up)
    f_w_down = _dev_to_rows(f_w_down)

    small_pack = _pack(small_sh, LANES, 8)
    small_g = _all_gather(small_pack, "gather_small").reshape(N_DEV, -1)
    cn, ckern, fconv = _unpack(small_g, small_sh_shapes)
    conv_norm_f = cn.transpose(1, 0, 2).reshape(n_conv, D_MODEL)
    conv_kernel_f = ckern.transpose(1, 2, 0, 3).reshape(n_conv, 3, D_MODEL)
    ffn_conv_f = fconv.transpose(1, 2, 0, 3).reshape(depth, 3, 2 * D_FF)

    def pair_gain(fox_g, sb_g):
        f2 = jnp.concatenate([fox_g, fox_g])
        s2 = jnp.concatenate([sb_g, sb_g])
        return jnp.concatenate([jnp.tile(f2[None], (4, 1)), jnp.tile(s2[None], (4, 1))])[:, None, :]

    h = x[0]
    saved = []
    for layer in range(depth):
        i = layer // 2
        tag = "l%d" % layer
        rec = {"h_in": h}
        if layer % 2 == 0:
            xn = _rms_fwd(h, attn_norm[i], tag + "_attn_rms")
            proj = _matmul(xn, a_w_in[i], tag + "_attn_in", tn=640)
            gq, gk = pair_gain(fox_q_gain[i], sb_q_gain[i]), pair_gain(fox_k_gain[i], sb_k_gain[i])
            qh, kh, vb = _qkv_prep(proj, gq, gk, tag + "_qkv_prep")
            logit = proj[:, 3 * MIX:3 * MIX + H_FOX].T.reshape(H_FOX, s // LANES, LANES)
            cum = _fgate_fwd(logit, attn_f_bias[i], tag + "_fgate")
            cum = cum.reshape(H_FOX // 2, 2, s)
            frow, fcol = cum, cum.transpose(0, 2, 1)
            o_fox, lse = _fox_fwd(qh, kh, vb, frow, fcol, tag + "_fox_fwd")
            o_sb, tot = _sb_fwd(qh, kh, vb, tag + "_sb_fwd")
            o = jnp.concatenate([o_fox, o_sb], axis=1)
            h = _matmul(o, a_w_out[i], tag + "_attn_out", add=h)
            rec.update(xn=xn, proj=proj, gq=gq, gk=gk, qh=qh, kh=kh, vb=vb, logit=logit, frow=frow, fcol=fcol,
                       o_fox=o_fox, lse=lse, tot=tot, o=o)
        else:
            xn = _rms_fwd(h, conv_norm_f[i], tag + "_conv_rms")
            proj = _matmul(xn, c_w_in[i], tag + "_conv_in")
            y = _sconv_fwd(proj, conv_kernel_f[i], tag + "_sconv_fwd")
            h = _matmul(y, c_w_out[i], tag + "_conv_out", add=h)
            rec.update(xn=xn, proj=proj, y=y)
        rec["h_mid"] = h
        xn2 = _rms_fwd(h, ffn_norm[layer], tag + "_ffn_rms")
        up = _matmul(xn2, f_w_up[layer], tag + "_ffn_up")
        act = _ffn_act_fwd(up, ffn_conv_f[layer], tag + "_ffn_act")
        h = _matmul(act, f_w_down[layer], tag + "_ffn_down", add=h, tk=1408)
        rec.update(xn2=xn2, up=up, act=act)
        saved.append(rec)

    loss_local, dh = _loss_head(h, loss_target[0], "loss_head")
    loss = lax.psum(loss_local, ("x", "y", "c"))

    g_attn_norm, g_attn_w_in, g_f_bias = [None] * n_attn, [None] * n_attn, [None] * n_attn
    g_fq, g_fk, g_sq, g_sk, g_attn_w_out = ([None] * n_attn for _ in range(5))
    g_conv_norm, g_conv_w_in, g_conv_kernel, g_conv_w_out = ([None] * n_conv for _ in range(4))
    g_ffn_norm, g_ffn_w_up, g_ffn_conv, g_ffn_w_down = ([None] * depth for _ in range(4))

    for layer in reversed(range(depth)):
        i = layer // 2
        tag = "l%d" % layer
        rec = saved[layer]
        da = _matmul(dh, f_w_down[layer].T, tag + "_ffn_down_dx", tn=1408)
        g_ffn_w_down[layer] = _matmul_tn(rec["act"], dh, tag + "_ffn_down_dw", tm=1408, tn=1024)
        dug, duv, dwg, dwv = _ffn_act_bwd(rec["up"], ffn_conv_f[layer], da, tag + "_ffn_act_bwd")
        g_ffn_conv[layer] = jnp.concatenate([dwg, dwv], axis=1)
        dup = jnp.concatenate([dug, duv], axis=1)
        g_ffn_w_up[layer] = _matmul_tn(rec["xn2"], dup, tag + "_ffn_up_dw", tn=1408)
        dxn = _matmul(dup, f_w_up[layer].T, tag + "_ffn_up_dx", tn=1024)
        dh, g_ffn_norm[layer] = _rms_bwd(rec["h_mid"], dxn, ffn_norm[layer], dh, tag + "_ffn_rms_bwd")
        if layer % 2 == 0:
            do = _matmul(dh, a_w_out[i].T, tag + "_attn_out_dx", tn=1024)
            g_attn_w_out[i] = _matmul_tn(rec["o"], dh, tag + "_attn_out_dw", tn=1024)
            dq_f, dk_f, dv_f, dfrow, dfcol = _fox_bwd(rec["qh"], rec["kh"], rec["vb"], rec["frow"], rec["fcol"],
                                               rec["lse"], rec["o_fox"], do, tag + "_fox_bwd")
            dq_s, dk_s, dv_s = _sb_bwd(rec["qh"], rec["kh"], rec["vb"], rec["tot"], do, tag + "_sb_bwd")
            dq, dk, dv, dgq, dgk = _qkv_prep_bwd(rec["proj"], rec["gq"], rec["gk"], (dq_f, dq_s), (dk_f, dk_s),
                                                 (dv_f, dv_s), tag + "_qkv_prep_bwd")
            dcum = (dfrow + dfcol.transpose(0, 2, 1)).reshape(H_FOX, s // LANES, LANES)
            dlogit, dbias = _fgate_bwd(rec["logit"], attn_f_bias[i], dcum, tag + "_fgate_bwd")
            g_f_bias[i] = dbias[:, 0]
            dgate = jnp.pad(dlogit.reshape(H_FOX, s).T, ((0, 0), (0, LANES - H_FOX))).astype(BF16)
            dproj = jnp.concatenate([dq, dk, dv, dgate], axis=1)

            def fold(dg):
                per_head = dg.reshape(16, HEAD_DIM)
                return jnp.sum(per_head[:8], axis=0), jnp.sum(per_head[8:], axis=0)

            g_fq[i], g_sq[i] = fold(dgq)
            g_fk[i], g_sk[i] = fold(dgk)
            g_attn_w_in[i] = _matmul_tn(rec["xn"], dproj, tag + "_attn_in_dw", tn=640)[:, :ATTN_IN]
            dxn = _matmul(dproj, a_w_in[i].T, tag + "_attn_in_dx", tn=1024, tk=640)
            dh, g_attn_norm[i] = _rms_bwd(rec["h_in"], dxn, attn_norm[i], dh, tag + "_attn_rms_bwd")
        else:
            dy = _matmul(dh, c_w_out[i].T, tag + "_conv_out_dx", tn=1024)
            g_conv_w_out[i] = _matmul_tn(rec["y"], dh, tag + "_conv_out_dw", tn=1024)
            db, dc, du, g_conv_kernel[i] = _sconv_bwd(rec["proj"], conv_kernel_f[i], dy, tag + "_sconv_bwd")
            dproj = jnp.concatenate([db, dc, du], axis=1)
            g_conv_w_in[i] = _matmul_tn(rec["xn"], dproj, tag + "_conv_in_dw", tn=1024)
            dxn = _matmul(dproj, c_w_in[i].T, tag + "_conv_in_dx", tn=1024)
            dh, g_conv_norm[i] = _rms_bwd(rec["h_in"], dxn, conv_norm_f[i], dh, tag + "_conv_rms_bwd")
    grad_x = dh[None]

    gb = [_cols_to_dev(jnp.stack(g_attn_w_in).astype(BF16)), _rows_to_dev(jnp.stack(g_attn_w_out).astype(BF16)),
          _cols_to_dev(jnp.stack(g_conv_w_in).astype(BF16)), _rows_to_dev(jnp.stack(g_conv_w_out).astype(BF16)),
          _cols_to_dev(jnp.stack(g_ffn_w_up).astype(BF16)), _rows_to_dev(jnp.stack(g_ffn_w_down).astype(BF16))]
    g_pack = _pack_per_device(gb, D_MODEL, 16)
    rows = g_pack.shape[1]
    g2 = g_pack.reshape(N_CHIP, 2, rows, D_MODEL).transpose(1, 0, 2, 3)
    from_sibling = _sibling_exchange(g2, "reduce_sibling")
    pair, own = _pair_add(g2, from_sibling, ids, "reduce_pair_add")
    from_chips = _chip_exchange(pair, "reduce_chips")
    g_big, d_big, m_big, v_big = _adamw_reduce(own, from_chips, ids, w_pack, _pack(big_m, D_MODEL, 16),
                                               _pack(big_v, D_MODEL, 16), "adamw_large")
    grads_big = _unpack(g_big.reshape(-1), big_shapes)
    delta_big = _unpack(d_big.reshape(-1), big_shapes)
    newm_big = _unpack(m_big.reshape(-1), big_shapes)
    newv_big = _unpack(v_big.reshape(-1), big_shapes)

    rep_g = [jnp.stack(g_attn_norm), jnp.stack(g_f_bias), jnp.stack(g_fq), jnp.stack(g_fk), jnp.stack(g_sq),
             jnp.stack(g_sk), jnp.stack(g_ffn_norm)]
    sh_g = [jnp.stack(g_conv_norm).reshape(n_conv, N_DEV, -1).transpose(1, 0, 2),
            jnp.stack(g_conv_kernel).reshape(n_conv, 3, N_DEV, -1).transpose(2, 0, 1, 3),
            jnp.stack(g_ffn_conv).reshape(depth, 3, N_DEV, -1).transpose(2, 0, 1, 3)]
    n_rep = sum(int(a.size) for a in rep)
    n_sh = sum(int(a.size) for a in small_sh)
    partial = _pack(rep_g + [jnp.concatenate([a.reshape(N_DEV, -1) for a in sh_g], axis=1)], LANES, 8)
    total = _sum_sources(_all_gather(partial, "gather_small_grads"), "sum_small_grads").reshape(-1)
    rep_tot = total[:n_rep]
    sh_tot = lax.dynamic_slice_in_dim(total[n_rep:n_rep + N_DEV * n_sh].reshape(N_DEV, n_sh), me, 1, axis=0)[0]
    g_small = jnp.concatenate([rep_tot, sh_tot])

    def small_pack_of(rep_list, sh_list):
        return _pack(rep_list + sh_list, LANES, 8)

    w_small = small_pack_of(rep, small_sh)
    d_small, m_small, v_small = _adamw_small(
        w_small, _pack([g_small], LANES, 8),
        small_pack_of([m_attn_norm, m_attn_f_bias, m_fox_q_gain, m_fox_k_gain, m_sb_q_gain, m_sb_k_gain, m_ffn_norm],
                      [m_conv_norm, m_conv_kernel, m_ffn_conv]),
        small_pack_of([v_attn_norm, v_attn_f_bias, v_fox_q_gain, v_fox_k_gain, v_sb_q_gain, v_sb_k_gain, v_ffn_norm],
                      [v_conv_norm, v_conv_kernel, v_ffn_conv]),
        "adamw_small")
    small_shapes = rep_shapes + small_sh_shapes

    def split_small(a):
        return _unpack(a.reshape(-1), small_shapes)

    def ordered(big_list, small_list):
        an, fb, fq, fk, sq, sk, fn, cno, cke, fco = small_list
        awi, awo, cwi, cwo, fwu, fwd = big_list
        return [an, awi, fb, fq, fk, sq, sk, awo, cno, cwi, cke, cwo, fn, fwu, fco, fwd]

    grads = ordered(grads_big, split_small(_pack([g_small], LANES, 8)))
    deltas = ordered(delta_big, split_small(d_small))
    new_m = ordered(newm_big, split_small(m_small))
    new_v = ordered(newv_big, split_small(v_small))
    return (loss, grad_x, *grads, *deltas, *new_m, *new_v)
```

```python
import jax
import jax.numpy as jnp
from jax import lax
from jax.experimental import pallas as pl
from jax.experimental.pallas import tpu as pltpu

F32 = jnp.float32
BF16 = jnp.bfloat16

D_MODEL = 1024
HEAD_DIM = 64
H_FOX = 8
MIX = 1024
ATTN_IN = 3 * MIX + H_FOX
ATTN_IN_PAD = 3 * MIX + 128
D_FF = 2816
EPS = 1e-6
NEG = -1e30
LANES = 128
N_DEV = 8
N_CHIP = 4

ADAM_LR = 0.001
ADAM_B1 = 0.9
ADAM_B2 = 0.999
ADAM_EPS = 1e-08
ADAM_WD = 0.01
ADAM_STEP = 10

VMEM_LIMIT = 56 * 1024 * 1024
MESH = pl.DeviceIdType.MESH
ANY = pl.BlockSpec(memory_space=pl.ANY)


def _params(*sem):
    return pltpu.CompilerParams(dimension_semantics=sem, vmem_limit_bytes=VMEM_LIMIT)


def _tile(n, target, mult=LANES):
    best = None
    for t in range(mult, min(n, target) + 1, mult):
        if n % t == 0:
            best = t
    return best if best is not None else n


def _dot(a, b):
    return jnp.dot(a, b, preferred_element_type=F32)


def _dot_tn(a, b):
    return lax.dot_general(a, b, (((0,), (0,)), ((), ())), preferred_element_type=F32)


def _split_dot(x, m, passes):
    acc = None
    rem = x
    for _ in range(passes):
        part = rem.astype(BF16)
        term = _dot(part, m)
        acc = term if acc is None else acc + term
        rem = rem - part.astype(F32)
    return acc


def _split_dot_left(m, x, passes):
    acc = None
    rem = x
    for _ in range(passes):
        part = rem.astype(BF16)
        term = _dot(m, part)
        acc = term if acc is None else acc + term
        rem = rem - part.astype(F32)
    return acc


def _matmul(a, b, name, add=None, out_dtype=F32, tm=1024, tn=512, tk=1024):
    m, k = a.shape
    n = b.shape[1]
    tm, tn, tk = _tile(m, tm, 8), _tile(n, tn), _tile(k, tk)
    nk = k // tk
    has_add = add is not None

    def body(*refs):
        a_ref, b_ref = refs[0], refs[1]
        add_ref = refs[2] if has_add else None
        o_ref = refs[2 + has_add]

        def finish(acc):
            if has_add:
                acc = acc + add_ref[...]
            o_ref[...] = acc.astype(out_dtype)

        p = _dot(a_ref[...].astype(BF16), b_ref[...].astype(BF16))
        if nk == 1:
            finish(p)
        else:
            acc_ref = refs[-1]
            kk = pl.program_id(2)

            @pl.when(kk == 0)
            def _():
                acc_ref[...] = p

            @pl.when(kk > 0)
            def _():
                acc_ref[...] += p

            @pl.when(kk == nk - 1)
            def _():
                finish(acc_ref[...])

    in_specs = [pl.BlockSpec((tm, tk), lambda i, j, kk: (i, kk)),
                pl.BlockSpec((tk, tn), lambda i, j, kk: (kk, j))]
    args = [a, b]
    if has_add:
        in_specs.append(pl.BlockSpec((tm, tn), lambda i, j, kk: (i, j)))
        args.append(add)
    return pl.pallas_call(
        body, name=name, grid=(m // tm, n // tn, nk), in_specs=in_specs,
        out_specs=pl.BlockSpec((tm, tn), lambda i, j, kk: (i, j)),
        out_shape=jax.ShapeDtypeStruct((m, n), out_dtype),
        scratch_shapes=[pltpu.VMEM((tm, tn), F32)] if nk > 1 else [],
        compiler_params=_params("parallel", "parallel", "arbitrary"),
    )(*args)


def _matmul_tn(a, b, name, tm=1024, tn=512, ts=1024):
    s, m = a.shape
    n = b.shape[1]
    tm, tn, ts = _tile(m, tm), _tile(n, tn), _tile(s, ts, 8)

    def body(a_ref, b_ref, o_ref):
        kk = pl.program_id(2)
        p = _dot_tn(a_ref[...].astype(BF16), b_ref[...].astype(BF16))

        @pl.when(kk == 0)
        def _():
            o_ref[...] = p

        @pl.when(kk > 0)
        def _():
            o_ref[...] += p

    return pl.pallas_call(
        body, name=name, grid=(m // tm, n // tn, s // ts),
        in_specs=[pl.BlockSpec((ts, tm), lambda i, j, kk: (kk, i)),
                  pl.BlockSpec((ts, tn), lambda i, j, kk: (kk, j))],
        out_specs=pl.BlockSpec((tm, tn), lambda i, j, kk: (i, j)),
        out_shape=jax.ShapeDtypeStruct((m, n), F32),
        compiler_params=_params("parallel", "parallel", "arbitrary"),
    )(a, b)


def _rms_fwd(h, g, name, ts=512):
    s, d = h.shape
    ts = _tile(s, ts, 8)

    def body(h_ref, g_ref, o_ref):
        x = h_ref[...]
        r = lax.rsqrt(jnp.mean(x * x, axis=-1, keepdims=True) + EPS)
        o_ref[...] = (x * r * g_ref[...]).astype(BF16)

    return pl.pallas_call(
        body, name=name, grid=(s // ts,),
        in_specs=[pl.BlockSpec((ts, d), lambda i: (i, 0)), pl.BlockSpec((1, d), lambda i: (0, 0))],
        out_specs=pl.BlockSpec((ts, d), lambda i: (i, 0)),
        out_shape=jax.ShapeDtypeStruct((s, d), BF16),
        compiler_params=_params("parallel"),
    )(h, g.reshape(1, d))


def _rms_bwd(h, dxn, g, dh_in, name, ts=512):
    s, d = h.shape
    ts = _tile(s, ts, 8)

    def body(h_ref, dxn_ref, g_ref, dhin_ref, dh_ref, dg_ref):
        i = pl.program_id(0)
        x = h_ref[...]
        r = lax.rsqrt(jnp.mean(x * x, axis=-1, keepdims=True) + EPS)
        xh = x * r
        dxn_v = dxn_ref[...]

        @pl.when(i == 0)
        def _():
            dg_ref[...] = jnp.zeros_like(dg_ref)

        dg_ref[0:1, :] += jnp.sum(dxn_v * xh, axis=0, keepdims=True)
        dxh = dxn_v * g_ref[...]
        dx = r * (dxh - xh * jnp.mean(dxh * xh, axis=-1, keepdims=True))
        dh_ref[...] = dhin_ref[...] + dx

    row = pl.BlockSpec((ts, d), lambda i: (i, 0))
    dh, dg = pl.pallas_call(
        body, name=name, grid=(s // ts,),
        in_specs=[row, row, pl.BlockSpec((1, d), lambda i: (0, 0)), row],
        out_specs=[row, pl.BlockSpec((8, d), lambda i: (0, 0))],
        out_shape=[jax.ShapeDtypeStruct((s, d), F32), jax.ShapeDtypeStruct((8, d), F32)],
        compiler_params=_params("arbitrary"),
    )(h, dxn, g.reshape(1, d), dh_in)
    return dh, dg[0]


def _shift_down(x, prev):
    rows = lax.broadcasted_iota(jnp.int32, x.shape, 0)
    p1, p2 = prev[7:8, :], prev[6:7, :]
    x1 = jnp.where(rows == 0, p1, pltpu.roll(x, 1, 0))
    x2 = jnp.where(rows == 0, p2, jnp.where(rows == 1, p1, pltpu.roll(x, 2, 0)))
    return x1, x2


def _shift_up(x, nxt):
    n = x.shape[0]
    rows = lax.broadcasted_iota(jnp.int32, x.shape, 0)
    n0, n1 = nxt[0:1, :], nxt[1:2, :]
    x1 = jnp.where(rows == n - 1, n0, pltpu.roll(x, n - 1, 0))
    x2 = jnp.where(rows == n - 1, n1, jnp.where(rows == n - 2, n0, pltpu.roll(x, n - 2, 0)))
    return x1, x2


def _conv(x, x1, x2, w):
    return w[2:3, :] * x + w[1:2, :] * x1 + w[0:1, :] * x2


def _halo_specs(ts, tc, col, n_time_blocks):
    r8 = ts // 8
    main = pl.BlockSpec((ts, tc), lambda j, i: (i, j + col))
    prev = pl.BlockSpec((8, tc), lambda j, i: (jnp.maximum(i * r8 - 1, 0), j + col))
    nxt = pl.BlockSpec((8, tc), lambda j, i: (jnp.minimum((i + 1) * r8, n_time_blocks * r8 - 1), j + col))
    return main, prev, nxt


def _silu_parts(g):
    sig = 1.0 / (1.0 + jnp.exp(-g))
    return sig, g * sig


def _ffn_act_fwd(up, cw, name, ts=256, tc=1408):
    s = up.shape[0]
    ts, tc = _tile(s, ts, 8), _tile(D_FF, tc)
    nc, nt = D_FF // tc, s // ts

    def body(g_ref, gp_ref, v_ref, vp_ref, wg_ref, wv_ref, o_ref):
        first = pl.program_id(1) == 0

        def conv(x_ref, p_ref, w_ref):
            x = x_ref[...]
            prev = jnp.where(first, 0.0, p_ref[...])
            x1, x2 = _shift_down(x, prev)
            return _conv(x, x1, x2, w_ref[...])

        ug = conv(g_ref, gp_ref, wg_ref)
        uv = conv(v_ref, vp_ref, wv_ref)
        _, silu = _silu_parts(ug)
        o_ref[...] = (silu * uv).astype(BF16)

    g_main, g_prev, _ = _halo_specs(ts, tc, 0, nt)
    v_main, v_prev, _ = _halo_specs(ts, tc, nc, nt)
    return pl.pallas_call(
        body, name=name, grid=(nc, nt),
        in_specs=[g_main, g_prev, v_main, v_prev,
                  pl.BlockSpec((3, tc), lambda j, i: (0, j)), pl.BlockSpec((3, tc), lambda j, i: (0, j + nc))],
        out_specs=pl.BlockSpec((ts, tc), lambda j, i: (i, j)),
        out_shape=jax.ShapeDtypeStruct((s, D_FF), BF16),
        compiler_params=_params("parallel", "parallel"),
    )(up, up, up, up, cw, cw)


def _ffn_act_bwd(up, cw, da, name, ts=256, tc=1408):
    s = up.shape[0]
    ts, tc = _tile(s, ts, 8), _tile(D_FF, tc)
    nc, nt = D_FF // tc, s // ts

    def body(g_ref, gp_ref, gn_ref, v_ref, vp_ref, vn_ref, da_ref, dan_ref, wg_ref, wv_ref,
             dg_ref, dv_ref, dwg_ref, dwv_ref):
        i = pl.program_id(1)
        first, last = i == 0, i == nt - 1
        wg, wv = wg_ref[...], wv_ref[...]
        g, v = g_ref[...], v_ref[...]
        g1, g2 = _shift_down(g, jnp.where(first, 0.0, gp_ref[...]))
        v1, v2 = _shift_down(v, jnp.where(first, 0.0, vp_ref[...]))

        def d_u(ug, uv, da_v):
            sig, silu = _silu_parts(ug)
            return da_v * uv * (sig * (1.0 + ug * (1.0 - sig))), da_v * silu

        dug, duv = d_u(_conv(g, g1, g2, wg), _conv(v, v1, v2, wv), da_ref[...])
        gn, vn = gn_ref[...], vn_ref[...]
        gn1, gn2 = _shift_down(gn, g[ts - 8:, :])
        vn1, vn2 = _shift_down(vn, v[ts - 8:, :])
        dugn, duvn = d_u(_conv(gn, gn1, gn2, wg), _conv(vn, vn1, vn2, wv), dan_ref[...])
        dugn = jnp.where(last, 0.0, dugn)
        duvn = jnp.where(last, 0.0, duvn)

        def finish(du, dun, x, x1, x2, w, dx_ref, dw_ref):
            d1, d2 = _shift_up(du, dun)
            dx_ref[...] = (w[2:3, :] * du + w[1:2, :] * d1 + w[0:1, :] * d2).astype(BF16)

            @pl.when(first)
            def _():
                dw_ref[...] = jnp.zeros_like(dw_ref)

            dw_ref[0:1, :] += jnp.sum(du * x2, axis=0, keepdims=True)
            dw_ref[1:2, :] += jnp.sum(du * x1, axis=0, keepdims=True)
            dw_ref[2:3, :] += jnp.sum(du * x, axis=0, keepdims=True)

        finish(dug, dugn, g, g1, g2, wg, dg_ref, dwg_ref)
        finish(duv, duvn, v, v1, v2, wv, dv_ref, dwv_ref)

    g_specs = _halo_specs(ts, tc, 0, nt)
    v_specs = _halo_specs(ts, tc, nc, nt)
    da_main, _, da_next = _halo_specs(ts, tc, 0, nt)
    tile = pl.BlockSpec((ts, tc), lambda j, i: (i, j))
    taps = pl.BlockSpec((8, tc), lambda j, i: (0, j))
    dg, dv, dwg, dwv = pl.pallas_call(
        body, name=name, grid=(nc, nt),
        in_specs=[*g_specs, *v_specs, da_main, da_next,
                  pl.BlockSpec((3, tc), lambda j, i: (0, j)), pl.BlockSpec((3, tc), lambda j, i: (0, j + nc))],
        out_specs=[tile, tile, taps, taps],
        out_shape=[jax.ShapeDtypeStruct((s, D_FF), BF16), jax.ShapeDtypeStruct((s, D_FF), BF16),
                   jax.ShapeDtypeStruct((8, D_FF), F32), jax.ShapeDtypeStruct((8, D_FF), F32)],
        compiler_params=_params("parallel", "arbitrary"),
    )(up, up, up, up, up, up, da, da, cw, cw)
    return dg, dv, dwg[:3], dwv[:3]


def _sconv_fwd(proj, ck, name, ts=256, tc=512):
    s = proj.shape[0]
    w = D_MODEL
    ts, tc = _tile(s, ts, 8), _tile(w, tc)
    nc, nt = w // tc, s // ts

    def body(b_ref, c_ref, cp_ref, u_ref, up_ref, w_ref, o_ref):
        first = pl.program_id(1) == 0
        cu = c_ref[...] * u_ref[...]
        cup = jnp.where(first, 0.0, cp_ref[...] * up_ref[...])
        x1, x2 = _shift_down(cu, cup)
        o_ref[...] = (b_ref[...] * _conv(cu, x1, x2, w_ref[...])).astype(BF16)

    b_main, _, _ = _halo_specs(ts, tc, 0, nt)
    c_main, c_prev, _ = _halo_specs(ts, tc, nc, nt)
    u_main, u_prev, _ = _halo_specs(ts, tc, 2 * nc, nt)
    return pl.pallas_call(
        body, name=name, grid=(nc, nt),
        in_specs=[b_main, c_main, c_prev, u_main, u_prev, pl.BlockSpec((3, tc), lambda j, i: (0, j))],
        out_specs=pl.BlockSpec((ts, tc), lambda j, i: (i, j)),
        out_shape=jax.ShapeDtypeStruct((s, w), BF16),
        compiler_params=_params("parallel", "parallel"),
    )(proj, proj, proj, proj, proj, ck)


def _sconv_bwd(proj, ck, dy, name, ts=256, tc=512):
    s = proj.shape[0]
    w = D_MODEL
    ts, tc = _tile(s, ts, 8), _tile(w, tc)
    nc, nt = w // tc, s // ts

    def body(b_ref, bn_ref, c_ref, cp_ref, u_ref, up_ref, dy_ref, dyn_ref, w_ref,
             db_ref, dc_ref, du_ref, dw_ref):
        i = pl.program_id(1)
        first, last = i == 0, i == nt - 1
        wv = w_ref[...]
        b, c, u, dy_v = b_ref[...], c_ref[...], u_ref[...], dy_ref[...]
        cu = c * u
        cup = jnp.where(first, 0.0, cp_ref[...] * up_ref[...])
        x1, x2 = _shift_down(cu, cup)
        db_ref[...] = (dy_v * _conv(cu, x1, x2, wv)).astype(BF16)
        dcv = dy_v * b
        dcvn = jnp.where(last, 0.0, dyn_ref[...] * bn_ref[...])
        d1, d2 = _shift_up(dcv, dcvn)
        dcu = wv[2:3, :] * dcv + wv[1:2, :] * d1 + wv[0:1, :] * d2
        dc_ref[...] = (dcu * u).astype(BF16)
        du_ref[...] = (dcu * c).astype(BF16)

        @pl.when(first)
        def _():
            dw_ref[...] = jnp.zeros_like(dw_ref)

        dw_ref[0:1, :] += jnp.sum(dcv * x2, axis=0, keepdims=True)
        dw_ref[1:2, :] += jnp.sum(dcv * x1, axis=0, keepdims=True)
        dw_ref[2:3, :] += jnp.sum(dcv * cu, axis=0, keepdims=True)

    b_main, _, b_next = _halo_specs(ts, tc, 0, nt)
    c_main, c_prev, _ = _halo_specs(ts, tc, nc, nt)
    u_main, u_prev, _ = _halo_specs(ts, tc, 2 * nc, nt)
    dy_main, _, dy_next = _halo_specs(ts, tc, 0, nt)
    tile = pl.BlockSpec((ts, tc), lambda j, i: (i, j))
    out = jax.ShapeDtypeStruct((s, w), BF16)
    db, dc, du, dw = pl.pallas_call(
        body, name=name, grid=(nc, nt),
        in_specs=[b_main, b_next, c_main, c_prev, u_main, u_prev, dy_main, dy_next,
                  pl.BlockSpec((3, tc), lambda j, i: (0, j))],
        out_specs=[tile, tile, tile, pl.BlockSpec((8, tc), lambda j, i: (0, j))],
        out_shape=[out, out, out, jax.ShapeDtypeStruct((8, w), F32)],
        compiler_params=_params("parallel", "arbitrary"),
    )(proj, proj, proj, proj, proj, proj, dy, dy, ck)
    return db, dc, du, dw[:3]


def _low_lanes(shape):
    return lax.broadcasted_iota(jnp.int32, shape, 1) < HEAD_DIM


def _top_rows(shape):
    return lax.broadcasted_iota(jnp.int32, shape, 0) < HEAD_DIM


def _norm_pair(x):
    lo = _low_lanes(x.shape)
    sq = x * x
    s_lo = jnp.sum(jnp.where(lo, sq, 0.0), axis=1, keepdims=True)
    s_hi = jnp.sum(jnp.where(lo, 0.0, sq), axis=1, keepdims=True)
    r = lax.rsqrt(jnp.where(lo, s_lo, s_hi) * (1.0 / HEAD_DIM) + EPS)
    return x * r, r


def _mean_pair(x):
    lo = _low_lanes(x.shape)
    s_lo = jnp.sum(jnp.where(lo, x, 0.0), axis=1, keepdims=True)
    s_hi = jnp.sum(jnp.where(lo, 0.0, x), axis=1, keepdims=True)
    return jnp.where(lo, s_lo, s_hi) * (1.0 / HEAD_DIM)


def _qkv_prep(proj, gq, gk, name, ts=512):
    s = proj.shape[0]
    ts = _tile(s, ts)
    npair = MIX // LANES
    scale = HEAD_DIM ** -0.5

    def body(q_ref, k_ref, v_ref, gq_ref, gk_ref, qo_ref, ko_ref, vo_ref, qt_ref, kt_ref, vt_ref):
        qn, _ = _norm_pair(q_ref[...])
        kn, _ = _norm_pair(k_ref[...])
        q = qn * gq_ref[0] * scale
        k = kn * gk_ref[0]
        v = v_ref[...]
        qo_ref[...] = q.astype(BF16)
        ko_ref[...] = k.astype(BF16)
        vo_ref[...] = v.astype(BF16)
        qt_ref[...] = q.T.astype(BF16)
        kt_ref[...] = k.T.astype(BF16)
        vt_ref[...] = v.T.astype(BF16)

    gain = pl.BlockSpec((1, 1, LANES), lambda i, p: (p, 0, 0))
    tile = pl.BlockSpec((ts, LANES), lambda i, p: (i, p))
    tile_t = pl.BlockSpec((LANES, ts), lambda i, p: (p, i))
    out = jax.ShapeDtypeStruct((s, MIX), BF16)
    out_t = jax.ShapeDtypeStruct((MIX, s), BF16)
    return pl.pallas_call(
        body, name=name, grid=(s // ts, npair),
        in_specs=[tile, pl.BlockSpec((ts, LANES), lambda i, p: (i, p + npair)),
                  pl.BlockSpec((ts, LANES), lambda i, p: (i, p + 2 * npair)), gain, gain],
        out_specs=[tile, tile, tile, tile_t, tile_t, tile_t], out_shape=[out, out, out, out_t, out_t, out_t],
        compiler_params=_params("parallel", "parallel"),
    )(proj, proj, proj, gq, gk)


def _qkv_prep_bwd(proj, gq, gk, dqs, dks, dvs, name, ts=512):
    s = proj.shape[0]
    ts = _tile(s, ts, 8)
    npair = MIX // LANES
    half = npair // 2
    scale = HEAD_DIM ** -0.5

    def body(q_ref, k_ref, gq_ref, gk_ref, dqf_ref, dqs_ref, dkf_ref, dks_ref, dvf_ref, dvs_ref,
             dq_ref, dk_ref, dv_ref, dgq_ref, dgk_ref):
        p, i = pl.program_id(0), pl.program_id(1)
        fox = p < half

        def one(x_ref, g_ref, df_ref, ds_ref, dx_ref, dg_ref, mult):
            dn = jnp.where(fox, df_ref[...], ds_ref[...]) * mult
            xh, r = _norm_pair(x_ref[...])

            @pl.when(i == 0)
            def _():
                dg_ref[...] = jnp.zeros_like(dg_ref)

            dg_ref[0, 0:1, :] += jnp.sum(dn * xh, axis=0, keepdims=True)
            dxh = dn * g_ref[0]
            dx_ref[...] = (r * (dxh - xh * _mean_pair(dxh * xh))).astype(BF16)

        one(q_ref, gq_ref, dqf_ref, dqs_ref, dq_ref, dgq_ref, scale)
        one(k_ref, gk_ref, dkf_ref, dks_ref, dk_ref, dgk_ref, 1.0)
        dv_ref[...] = jnp.where(fox, dvf_ref[...], dvs_ref[...]).astype(BF16)

    gain = pl.BlockSpec((1, 1, LANES), lambda p, i: (p, 0, 0))
    tile = pl.BlockSpec((ts, LANES), lambda p, i: (i, p))
    fpart = pl.BlockSpec((ts, LANES), lambda p, i: (i, jnp.minimum(p, half - 1)))
    spart = pl.BlockSpec((ts, LANES), lambda p, i: (i, jnp.maximum(p - half, 0)))
    dgain = pl.BlockSpec((1, 8, LANES), lambda p, i: (p, 0, 0))
    out = jax.ShapeDtypeStruct((s, MIX), BF16)
    gshape = jax.ShapeDtypeStruct((npair, 8, LANES), F32)
    dq, dk, dv, dgq, dgk = pl.pallas_call(
        body, name=name, grid=(npair, s // ts),
        in_specs=[tile, pl.BlockSpec((ts, LANES), lambda p, i: (i, p + npair)), gain, gain,
                  fpart, spart, fpart, spart, fpart, spart],
        out_specs=[tile, tile, tile, dgain, dgain], out_shape=[out, out, out, gshape, gshape],
        compiler_params=_params("parallel", "arbitrary"),
    )(proj, proj, gq, gk, dqs[0], dqs[1], dks[0], dks[1], dvs[0], dvs[1])
    return dq, dk, dv, dgq[:, 0, :], dgk[:, 0, :]


def _tri(n, rel):
    a = lax.broadcasted_iota(jnp.int32, (n, n), 0)
    b = lax.broadcasted_iota(jnp.int32, (n, n), 1)
    return rel(a, b).astype(BF16)


def _fgate_fwd(logit, bias, name):
    nh, r, _ = logit.shape

    def body(x_ref, b_ref, o_ref):
        within = _tri(LANES, lambda a, b: a <= b)
        before = _tri(r, lambda a, b: b < a)
        for hh in range(nh):
            x = x_ref[hh] + b_ref[hh]
            lf = jnp.minimum(x, 0.0) - jnp.log1p(jnp.exp(-jnp.abs(x)))
            c = _split_dot(lf, within, 3)
            tot = jnp.broadcast_to(c[:, LANES - 1:LANES], (r, LANES))
            o_ref[hh] = c + _split_dot_left(before, tot, 3)

    return pl.pallas_call(
        body, name=name, out_shape=jax.ShapeDtypeStruct((nh, r, LANES), F32),
        in_specs=[pl.BlockSpec(memory_space=pltpu.VMEM), pl.BlockSpec(memory_space=pltpu.SMEM)],
        out_specs=pl.BlockSpec(memory_space=pltpu.VMEM),
    )(logit, bias)


def _fgate_bwd(logit, bias, dcum, name):
    nh, r, _ = logit.shape

    def body(x_ref, b_ref, d_ref, dx_ref, db_ref):
        within = _tri(LANES, lambda a, b: a >= b)
        after = _tri(r, lambda a, b: b > a)
        for hh in range(nh):
            x = x_ref[hh] + b_ref[hh]
            d = d_ref[hh]
            c = _split_dot(d, within, 3)
            tot = jnp.broadcast_to(c[:, 0:1], (r, LANES))
            dlf = c + _split_dot_left(after, tot, 3)
            dx = dlf * (1.0 / (1.0 + jnp.exp(x)))
            dx_ref[hh] = dx
            db_ref[hh:hh + 1, :] = jnp.broadcast_to(jnp.sum(dx, keepdims=True).reshape(1, 1), (1, LANES))

    return pl.pallas_call(
        body, name=name,
        out_shape=[jax.ShapeDtypeStruct((nh, r, LANES), F32), jax.ShapeDtypeStruct((nh, LANES), F32)],
        in_specs=[pl.BlockSpec(memory_space=pltpu.VMEM), pl.BlockSpec(memory_space=pltpu.SMEM),
                  pl.BlockSpec(memory_space=pltpu.VMEM)],
        out_specs=[pl.BlockSpec(memory_space=pltpu.VMEM), pl.BlockSpec(memory_space=pltpu.VMEM)],
    )(logit, bias, dcum)


def _pair_masks(x):
    lo = _low_lanes(x.shape)
    zero = jnp.zeros_like(x)
    return jnp.where(lo, x, zero), jnp.where(lo, zero, x)


def _pair_masks_t(x):
    top = _top_rows(x.shape)
    zero = jnp.zeros_like(x)
    return jnp.where(top, x, zero), jnp.where(top, zero, x)


def _pair_colsum_t(x):
    top = _top_rows(x.shape)
    return (jnp.sum(jnp.where(top, x, 0.0), axis=0, keepdims=True),
            jnp.sum(jnp.where(top, 0.0, x), axis=0, keepdims=True))


def _key_query_iotas(t):
    return lax.broadcasted_iota(jnp.int32, (t, t), 0), lax.broadcasted_iota(jnp.int32, (t, t), 1)


def _attn_specs(s, tq, pair0):
    q_nat = pl.BlockSpec((tq, LANES), lambda p, i: (i, p + pair0))
    q_t = pl.BlockSpec((LANES, tq), lambda p, i: (p + pair0, i))
    k_nat = pl.BlockSpec((s, LANES), lambda p, i: (0, p + pair0))
    k_t = pl.BlockSpec((LANES, s), lambda p, i: (p + pair0, 0))
    return q_nat, q_t, k_nat, k_t


def _fox_fwd(qt, kh, vt, frow, fcol, name, tq=256):
    s = kh.shape[0]
    tq = _tile(s, tq)
    nq, half = s // tq, MIX // LANES // 2

    def body(qt_ref, k_ref, vt_ref, fr_ref, fc_ref, o_ref, lse_ref, m_s, l_s, acc_s):
        i = pl.program_id(1)
        qt_v = qt_ref[...]
        ft = fr_ref[0]
        key, qry = _key_query_iotas(tq)
        causal = key <= qry
        m_s[...] = jnp.full(m_s.shape, NEG, F32)
        l_s[...] = jnp.zeros_like(l_s)
        acc_s[...] = jnp.zeros_like(acc_s)

        def step(j, masked):
            rows = pl.ds(pl.multiple_of(j * tq, tq), tq)
            ks = _pair_masks(k_ref[rows, :])
            vt_v = vt_ref[:, rows]
            fk = fc_ref[0, rows, :]
            for hh in range(2):
                sc = _dot(ks[hh], qt_v) + (ft[hh:hh + 1, :] - fk[:, hh:hh + 1])
                if masked:
                    sc = jnp.where(causal, sc, NEG)
                m_old = m_s[hh]
                m_new = jnp.maximum(m_old, jnp.max(sc, axis=0, keepdims=True))
                alpha = jnp.exp(m_old - m_new)
                pr = jnp.exp(sc - m_new)
                l_s[hh] = alpha * l_s[hh] + jnp.sum(pr, axis=0, keepdims=True)
                acc_s[hh] = alpha * acc_s[hh] + _dot(vt_v, pr.astype(BF16))
                m_s[hh] = m_new

        def loop(j, carry):
            step(j, False)
            return carry

        lax.fori_loop(0, i, loop, 0)
        step(i, True)
        ot = jnp.where(_top_rows((LANES, tq)), acc_s[0] / l_s[0], acc_s[1] / l_s[1])
        o_ref[...] = ot.T
        lse_ref[0, 0:1, :] = m_s[0] + jnp.log(l_s[0])
        lse_ref[0, 1:2, :] = m_s[1] + jnp.log(l_s[1])

    _, q_t, k_nat, k_t = _attn_specs(s, tq, 0)
    qstat = pl.BlockSpec((1, 2, tq), lambda p, i: (p, 0, i))
    return pl.pallas_call(
        body, name=name, grid=(half, nq),
        in_specs=[q_t, k_nat, k_t, qstat, pl.BlockSpec((1, s, 2), lambda p, i: (p, 0, 0))],
        out_specs=[pl.BlockSpec((tq, LANES), lambda p, i: (i, p)), qstat],
        out_shape=[jax.ShapeDtypeStruct((s, MIX // 2), F32), jax.ShapeDtypeStruct((half, 2, s), F32)],
        scratch_shapes=[pltpu.VMEM((2, 1, tq), F32), pltpu.VMEM((2, 1, tq), F32), pltpu.VMEM((2, LANES, tq), F32)],
        compiler_params=_params("parallel", "arbitrary"),
    )(qt, kh, vt, frow, fcol)


def _fox_bwd(qh, qt, kh, kt, vb, frow, fcol, lse, o, do, name, tq=256):
    s = kh.shape[0]
    tq = _tile(s, tq)
    nq, half = s // tq, MIX // LANES // 2

    def body(q_ref, qt_ref, k_ref, kt_ref, v_ref, fr_ref, fc_ref, lse_ref, o_ref, do_ref,
             dq_ref, dk_ref, dv_ref, dfk_ref, dfq_ref, dq_s, rs_s):
        i = pl.program_id(1)

        @pl.when(i == 0)
        def _():
            dk_ref[...] = jnp.zeros_like(dk_ref)
            dv_ref[...] = jnp.zeros_like(dv_ref)
            dfk_ref[...] = jnp.zeros_like(dfk_ref)

        qs = _pair_masks(q_ref[...])
        qt_v = qt_ref[...]
        do_v = do_ref[...]
        dos = _pair_masks(do_v.astype(BF16))
        dot_v = do_v.T.astype(BF16)
        dsum = _pair_colsum_t((do_v * o_ref[...]).T)
        ft, ls = fr_ref[0], lse_ref[0]
        key, qry = _key_query_iotas(tq)
        causal = key <= qry
        lane = lax.broadcasted_iota(jnp.int32, (tq, LANES), 1)
        pick = ((lane == 0).astype(BF16), (lane == 1).astype(BF16))
        dq_s[...] = jnp.zeros_like(dq_s)
        rs_s[...] = jnp.zeros_like(rs_s)

        def step(j, masked):
            rows = pl.ds(pl.multiple_of(j * tq, tq), tq)
            ks = _pair_masks(k_ref[rows, :])
            kts = _pair_masks_t(kt_ref[:, rows])
            vs = _pair_masks(v_ref[rows, :])
            fk = fc_ref[0, rows, :]
            for hh in range(2):
                sc = _dot(ks[hh], qt_v) + (ft[hh:hh + 1, :] - fk[:, hh:hh + 1])
                pr = jnp.exp(sc - ls[hh:hh + 1, :])
                if masked:
                    pr = jnp.where(causal, pr, 0.0)
                ds = pr * (_dot(vs[hh], dot_v) - dsum[hh])
                dsb = ds.astype(BF16)
                dv_ref[rows, :] += _dot(pr.astype(BF16), dos[hh])
                dk_ref[rows, :] += _dot(dsb, qs[hh])
                dq_s[...] += _dot(kts[hh], dsb)
                dfk_ref[0, rows, :] -= _dot(dsb, pick[hh])
                rs_s[hh] += jnp.sum(dsb.astype(F32), axis=0, keepdims=True)

        def loop(j, carry):
            step(j, False)
            return carry

        lax.fori_loop(0, i, loop, 0)
        step(i, True)
        dq_ref[...] = dq_s[...].T
        dfq_ref[0, 0:1, :] = rs_s[0]
        dfq_ref[0, 1:2, :] = rs_s[1]

    q_nat, q_t, k_nat, k_t = _attn_specs(s, tq, 0)
    qstat = pl.BlockSpec((1, 2, tq), lambda p, i: (p, 0, i))
    otile = pl.BlockSpec((tq, LANES), lambda p, i: (i, p))
    oresident = pl.BlockSpec((s, LANES), lambda p, i: (0, p))
    out = jax.ShapeDtypeStruct((s, MIX // 2), F32)
    return pl.pallas_call(
        body, name=name, grid=(half, nq),
        in_specs=[q_nat, q_t, k_nat, k_t, k_nat, qstat, pl.BlockSpec((1, s, 2), lambda p, i: (p, 0, 0)), qstat,
                  otile, q_nat],
        out_specs=[otile, oresident, oresident, pl.BlockSpec((1, s, LANES), lambda p, i: (p, 0, 0)), qstat],
        out_shape=[out, out, out, jax.ShapeDtypeStruct((half, s, LANES), F32),
                   jax.ShapeDtypeStruct((half, 2, s), F32)],
        scratch_shapes=[pltpu.VMEM((LANES, tq), F32), pltpu.VMEM((2, 1, tq), F32)],
        compiler_params=_params("parallel", "arbitrary"),
    )(qh, qt, kh, kt, vb, frow, fcol, lse, o, do)


def _log_sig_pair(z):
    lb = jnp.minimum(z, 0.0) - jnp.log(1.0 + jnp.exp(-jnp.abs(z)))
    return lb, lb - z


def _sb_fwd(qt, kh, vt, name, tq=256):
    s = kh.shape[0]
    tq = _tile(s, tq)
    nq, half = s // tq, MIX // LANES // 2

    def body(qt_ref, k_ref, vt_ref, o_ref, tot_ref, c_s, acc_s):
        i = pl.program_id(1)
        qt_v = qt_ref[...]
        key, qry = _key_query_iotas(tq)
        strict = key < qry
        later = _tri(tq, lambda a, b: b > a)
        c_s[...] = jnp.zeros_like(c_s)
        acc_s[...] = jnp.zeros_like(acc_s)

        def step(j, masked):
            rows = pl.ds(pl.multiple_of(j * tq, tq), tq)
            ks = _pair_masks(k_ref[rows, :])
            vt_v = vt_ref[:, rows]
            for hh in range(2):
                lb, lom = _log_sig_pair(_dot(ks[hh], qt_v))
                if masked:
                    lom = jnp.where(strict, lom, 0.0)
                w = jnp.exp(lb + _split_dot_left(later, lom, 2) + c_s[hh])
                if masked:
                    w = jnp.where(strict, w, 0.0)
                acc_s[hh] += _dot(vt_v, w.astype(BF16))
                c_s[hh] += jnp.sum(lom, axis=0, keepdims=True)

        def loop(jj, carry):
            step(i - 1 - jj, False)
            return carry

        step(i, True)
        lax.fori_loop(0, i, loop, 0)
        o_ref[...] = jnp.where(_top_rows((LANES, tq)), acc_s[0], acc_s[1]).T
        tot_ref[0, 0:1, :] = c_s[0]
        tot_ref[0, 1:2, :] = c_s[1]

    _, q_t, k_nat, k_t = _attn_specs(s, tq, half)
    qstat = pl.BlockSpec((1, 2, tq), lambda p, i: (p, 0, i))
    return pl.pallas_call(
        body, name=name, grid=(half, nq),
        in_specs=[q_t, k_nat, k_t],
        out_specs=[pl.BlockSpec((tq, LANES), lambda p, i: (i, p)), qstat],
        out_shape=[jax.ShapeDtypeStruct((s, MIX // 2), F32), jax.ShapeDtypeStruct((half, 2, s), F32)],
        scratch_shapes=[pltpu.VMEM((2, 1, tq), F32), pltpu.VMEM((2, LANES, tq), F32)],
        compiler_params=_params("parallel", "arbitrary"),
    )(qt, kh, vt)


def _sb_bwd(qh, qt, kh, kt, vb, tot, do, name, tq=256):
    s = kh.shape[0]
    tq = _tile(s, tq)
    nq, half = s // tq, MIX // LANES // 2

    def body(q_ref, qt_ref, k_ref, kt_ref, v_ref, tot_ref, do_ref, dq_ref, dk_ref, dv_ref, rem_s, pg_s, dq_s):
        i = pl.program_id(1)

        @pl.when(i == 0)
        def _():
            dk_ref[...] = jnp.zeros_like(dk_ref)
            dv_ref[...] = jnp.zeros_like(dv_ref)

        qs = _pair_masks(q_ref[...])
        qt_v = qt_ref[...]
        do_v = do_ref[...]
        dos = _pair_masks(do_v.astype(BF16))
        dot_v = do_v.T.astype(BF16)
        key, qry = _key_query_iotas(tq)
        strict = key < qry
        upto = _tri(tq, lambda a, b: b <= a)
        before = _tri(tq, lambda a, b: b < a)
        tv = tot_ref[0]
        rem_s[0] = tv[0:1, :]
        rem_s[1] = tv[1:2, :]
        pg_s[...] = jnp.zeros_like(pg_s)
        dq_s[...] = jnp.zeros_like(dq_s)

        def step(j, masked):
            rows = pl.ds(pl.multiple_of(j * tq, tq), tq)
            ks = _pair_masks(k_ref[rows, :])
            kts = _pair_masks_t(kt_ref[:, rows])
            vs = _pair_masks(v_ref[rows, :])
            for hh in range(2):
                lb, lom = _log_sig_pair(_dot(ks[hh], qt_v))
                if masked:
                    lom = jnp.where(strict, lom, 0.0)
                w = jnp.exp(lb + (rem_s[hh] - _split_dot_left(upto, lom, 2)))
                if masked:
                    w = jnp.where(strict, w, 0.0)
                g = _dot(vs[hh], dot_v) * w
                cg = pg_s[hh] + _split_dot_left(before, g, 2)
                dz = g - jnp.exp(lb) * (g + cg)
                if masked:
                    dz = jnp.where(strict, dz, 0.0)
                dzb = dz.astype(BF16)
                dv_ref[rows, :] += _dot(w.astype(BF16), dos[hh])
                dk_ref[rows, :] += _dot(dzb, qs[hh])
                dq_s[...] += _dot(kts[hh], dzb)
                rem_s[hh] -= jnp.sum(lom, axis=0, keepdims=True)
                pg_s[hh] += jnp.sum(g, axis=0, keepdims=True)

        def loop(j, carry):
            step(j, False)
            return carry

        lax.fori_loop(0, i, loop, 0)
        step(i, True)
        dq_ref[...] = dq_s[...].T

    q_nat, q_t, k_nat, k_t = _attn_specs(s, tq, half)
    qstat = pl.BlockSpec((1, 2, tq), lambda p, i: (p, 0, i))
    otile = pl.BlockSpec((tq, LANES), lambda p, i: (i, p))
    oresident = pl.BlockSpec((s, LANES), lambda p, i: (0, p))
    out = jax.ShapeDtypeStruct((s, MIX // 2), F32)
    return pl.pallas_call(
        body, name=name, grid=(half, nq),
        in_specs=[q_nat, q_t, k_nat, k_t, k_nat, qstat, q_nat],
        out_specs=[otile, oresident, oresident], out_shape=[out, out, out],
        scratch_shapes=[pltpu.VMEM((2, 1, tq), F32), pltpu.VMEM((2, 1, tq), F32), pltpu.VMEM((LANES, tq), F32)],
        compiler_params=_params("parallel", "arbitrary"),
    )(qh, qt, kh, kt, vb, tot, do)


def _loss_head(y, target, name, ts=512):
    s, d = y.shape
    ts = _tile(s, ts, 8)
    nt = s // ts

    def body(y_ref, t_ref, dy_ref, l_ref, acc):
        i = pl.program_id(0)
        err = y_ref[...] - t_ref[...]
        dy_ref[...] = err * (1.0 / d)

        @pl.when(i == 0)
        def _():
            acc[...] = jnp.zeros_like(acc)

        acc[...] += jnp.sum(err * err, axis=0, keepdims=True)

        @pl.when(i == nt - 1)
        def _():
            tot = jnp.sum(acc[...], keepdims=True).reshape(1, 1) * (0.5 / d)
            l_ref[...] = jnp.broadcast_to(tot, l_ref.shape)

    row = pl.BlockSpec((ts, d), lambda i: (i, 0))
    dy, l = pl.pallas_call(
        body, name=name, grid=(nt,), in_specs=[row, row],
        out_specs=[row, pl.BlockSpec((8, LANES), lambda i: (0, 0))],
        out_shape=[jax.ShapeDtypeStruct((s, d), F32), jax.ShapeDtypeStruct((8, LANES), F32)],
        scratch_shapes=[pltpu.VMEM((1, d), F32)],
        compiler_params=_params("arbitrary"),
    )(y, target)
    return l[0, 0], dy


def _coords():
    return lax.axis_index("x"), lax.axis_index("y"), lax.axis_index("c")


def _other_chips(xi, yi):
    return [(1 - xi, yi), (xi, 1 - yi), (1 - xi, 1 - yi)]


def _all_gather(xs, name):
    n = len(xs)

    def body(*refs):
        x_refs, out_refs = refs[:n], refs[n:2 * n]
        send_sems, recv_sems, local_sems = refs[2 * n:]
        xi, yi, ci = _coords()
        me, sibling = (xi, yi, ci), (xi, yi, 1 - ci)
        chips = _other_chips(xi, yi)

        def slot(a, px, py, pc):
            return out_refs[a].at[4 * px + 2 * py + pc]

        def copy(a, k, block, to, src=None):
            return pltpu.make_async_remote_copy(
                src_ref=slot(a, *block) if src is None else src, dst_ref=slot(a, *block),
                send_sem=send_sems.at[a, k], recv_sem=recv_sems.at[a, k], device_id=to, device_id_type=MESH)

        mine = [pltpu.make_async_copy(x_refs[a], slot(a, *me), local_sems.at[a]) for a in range(n)]
        for cp in mine:
            cp.start()
        first = []
        for a in range(n):
            first.append(copy(a, 0, me, sibling, src=x_refs[a]))
            first += [copy(a, 1 + j, me, (*chip, ci), src=x_refs[a]) for j, chip in enumerate(chips)]
        for cp in first:
            cp.start()
        passed = []
        for j, chip in enumerate(chips):
            for a in range(n):
                copy(a, 1 + j, (*chip, ci), me).wait_recv()
                fwd = copy(a, 4 + j, (*chip, ci), sibling)
                fwd.start()
                passed.append(fwd)
        for a in range(n):
            copy(a, 0, sibling, me).wait_recv()
            for j, chip in enumerate(chips):
                copy(a, 4 + j, (*chip, 1 - ci), me).wait_recv()
        for cp in first + passed:
            cp.wait_send()
        for cp in mine:
            cp.wait()

    return pl.pallas_call(
        body, name=name, out_shape=[jax.ShapeDtypeStruct((N_DEV,) + x.shape, x.dtype) for x in xs],
        in_specs=[ANY] * n, out_specs=[ANY] * n,
        scratch_shapes=[pltpu.SemaphoreType.DMA((n, 7)), pltpu.SemaphoreType.DMA((n, 7)),
                        pltpu.SemaphoreType.DMA((n,))],
    )(*xs)


def _sibling_exchange(gs, name):
    n = len(gs)

    def body(*refs):
        g_refs, recv_refs = refs[:n], refs[n:2 * n]
        send_sems, recv_sems = refs[2 * n:]
        xi, yi, ci = _coords()
        cps = [pltpu.make_async_remote_copy(
            src_ref=g_refs[a].at[2 * chip + (1 - ci)], dst_ref=recv_refs[a].at[chip],
            send_sem=send_sems.at[a, chip], recv_sem=recv_sems.at[a, chip],
            device_id=(xi, yi, 1 - ci), device_id_type=MESH) for a in range(n) for chip in range(N_CHIP)]
        for cp in cps:
            cp.start()
        for cp in cps:
            cp.wait()

    return pl.pallas_call(
        body, name=name, out_shape=[jax.ShapeDtypeStruct((N_CHIP,) + g.shape[1:], g.dtype) for g in gs],
        in_specs=[ANY] * n, out_specs=[ANY] * n,
        scratch_shapes=[pltpu.SemaphoreType.DMA((n, N_CHIP)), pltpu.SemaphoreType.DMA((n, N_CHIP))],
    )(*gs)


def _pair_add(g, recv, ids, name, tr=256):
    _, r, c = g.shape
    tr = _tile(r, tr, 16)

    def body(ids_ref, g_ref, r_ref, p_ref, own_ref):
        kk = pl.program_id(1)
        tot = g_ref[0].astype(F32) + r_ref[0].astype(F32)
        p_ref[0] = tot.astype(BF16)

        @pl.when(kk == ids_ref[1])
        def _():
            own_ref[...] = tot

    grid_spec = pltpu.PrefetchScalarGridSpec(
        num_scalar_prefetch=1, grid=(r // tr, N_CHIP),
        in_specs=[pl.BlockSpec((1, tr, c), lambda i, kk, ids: (2 * kk + ids[0], i, 0)),
                  pl.BlockSpec((1, tr, c), lambda i, kk, ids: (kk, i, 0))],
        out_specs=[pl.BlockSpec((1, tr, c), lambda i, kk, ids: (kk, i, 0)),
                   pl.BlockSpec((tr, c), lambda i, kk, ids: (i, 0))])
    return pl.pallas_call(
        body, name=name, grid_spec=grid_spec,
        out_shape=[jax.ShapeDtypeStruct((N_CHIP, r, c), BF16), jax.ShapeDtypeStruct((r, c), F32)],
        compiler_params=_params("parallel", "arbitrary"),
    )(ids, g, recv)


def _chip_exchange(ps, name):
    n = len(ps)

    def body(*refs):
        p_refs, recv_refs = refs[:n], refs[n:2 * n]
        send_sems, recv_sems = refs[2 * n:]
        xi, yi, ci = _coords()
        mine = 2 * xi + yi
        chips = _other_chips(xi, yi)

        def copy(a, k, cx, cy):
            return pltpu.make_async_remote_copy(
                src_ref=p_refs[a].at[2 * cx + cy], dst_ref=recv_refs[a].at[mine],
                send_sem=send_sems.at[a, k], recv_sem=recv_sems.at[a, k],
                device_id=(cx, cy, ci), device_id_type=MESH)

        def landed(a, k, cx, cy):
            return pltpu.make_async_remote_copy(
                src_ref=p_refs[a].at[mine], dst_ref=recv_refs[a].at[2 * cx + cy],
                send_sem=send_sems.at[a, k], recv_sem=recv_sems.at[a, k],
                device_id=(cx, cy, ci), device_id_type=MESH)

        sends = [copy(a, k, cx, cy) for a in range(n) for k, (cx, cy) in enumerate(chips)]
        for cp in sends:
            cp.start()
        for a in range(n):
            for k, (cx, cy) in enumerate(chips):
                landed(a, k, cx, cy).wait_recv()
        for cp in sends:
            cp.wait_send()

    return pl.pallas_call(
        body, name=name, out_shape=[jax.ShapeDtypeStruct(p.shape, p.dtype) for p in ps],
        in_specs=[ANY] * n, out_specs=[ANY] * n,
        scratch_shapes=[pltpu.SemaphoreType.DMA((n, 3)), pltpu.SemaphoreType.DMA((n, 3))],
    )(*ps)


def _adamw_math(w, g, m, v):
    m = ADAM_B1 * m + (1.0 - ADAM_B1) * g
    v = ADAM_B2 * v + (1.0 - ADAM_B2) * (g * g)
    m_hat = m / (1.0 - ADAM_B1 ** ADAM_STEP)
    v_hat = v / (1.0 - ADAM_B2 ** ADAM_STEP)
    delta = -ADAM_LR * (m_hat / (jnp.sqrt(v_hat) + ADAM_EPS) + ADAM_WD * w)
    return delta, m, v


def _adamw_reduce(own, recv, ids, w, m, v, name, tr=256):
    r, c = w.shape
    tr = _tile(r, tr, 16)

    def body(ids_ref, own_ref, recv_ref, w_ref, m_ref, v_ref, g_ref, d_ref, mo_ref, vo_ref):
        mine = ids_ref[1]
        g = None
        for kk in range(N_CHIP):
            term = jnp.where(mine == kk, own_ref[...], recv_ref[kk].astype(F32))
            g = term if g is None else g + term
        delta, m_new, v_new = _adamw_math(w_ref[...], g, m_ref[...], v_ref[...])
        g_ref[...] = g
        d_ref[...] = delta
        mo_ref[...] = m_new
        vo_ref[...] = v_new

    row = pl.BlockSpec((tr, c), lambda i, ids: (i, 0))
    grid_spec = pltpu.PrefetchScalarGridSpec(
        num_scalar_prefetch=1, grid=(r // tr,),
        in_specs=[row, pl.BlockSpec((N_CHIP, tr, c), lambda i, ids: (0, i, 0)), row, row, row],
        out_specs=[row, row, row, row])
    out = jax.ShapeDtypeStruct((r, c), F32)
    return pl.pallas_call(
        body, name=name, grid_spec=grid_spec, out_shape=[out, out, out, out],
        compiler_params=_params("parallel"),
    )(ids, own, recv, w, m, v)


def _sum_sources(a, name):
    n, r, c = a.shape

    def body(a_ref, o_ref):
        tot = a_ref[0]
        for kk in range(1, n):
            tot = tot + a_ref[kk]
        o_ref[...] = tot

    return pl.pallas_call(
        body, name=name, out_shape=jax.ShapeDtypeStruct((r, c), F32),
        in_specs=[pl.BlockSpec(memory_space=pltpu.VMEM)], out_specs=pl.BlockSpec(memory_space=pltpu.VMEM),
    )(a)


def _adamw_small(w, g, m, v, name):
    def body(w_ref, g_ref, m_ref, v_ref, d_ref, mo_ref, vo_ref):
        delta, m_new, v_new = _adamw_math(w_ref[...], g_ref[...], m_ref[...], v_ref[...])
        d_ref[...] = delta
        mo_ref[...] = m_new
        vo_ref[...] = v_new

    vm = pl.BlockSpec(memory_space=pltpu.VMEM)
    out = jax.ShapeDtypeStruct(w.shape, F32)
    return pl.pallas_call(body, name=name, out_shape=[out, out, out], in_specs=[vm] * 4, out_specs=[vm] * 3)(w, g, m, v)


def _pack(parts, width, row_mult):
    flat = jnp.concatenate([p.reshape(-1) for p in parts])
    rows = -(-flat.shape[0] // width)
    rows = -(-rows // row_mult) * row_mult
    return jnp.pad(flat, (0, rows * width - flat.shape[0])).reshape(rows, width)


def _unpack(flat, shapes):
    out, off = [], 0
    lead = flat.shape[:-1]
    for shp in shapes:
        n = 1
        for dd in shp:
            n *= dd
        out.append(flat[..., off:off + n].reshape(lead + tuple(shp)))
        off += n
    return out


def _rows2d(w):
    return w.reshape(w.shape[0] * w.shape[1], w.shape[2])


def _cols_to_dev(g):
    l, k, n = g.shape
    return g.reshape(l * k, N_DEV, n // N_DEV).transpose(1, 0, 2)


def _rows_to_dev(g):
    l, k, n = g.shape
    rs = k // N_DEV
    return g.reshape(l, N_DEV, rs, n).transpose(1, 0, 2, 3).reshape(N_DEV, l * rs, n)


def _dev_to_cols(a, l):
    _, lk, cs = a.shape
    return a.transpose(1, 0, 2).reshape(l, lk // l, N_DEV * cs)


def _dev_to_rows(a, l):
    _, lr, n = a.shape
    rs = lr // l
    return a.reshape(N_DEV, l, rs, n).transpose(1, 0, 2, 3).reshape(l, N_DEV * rs, n)


def kernel(x, attn_norm, attn_w_in, attn_f_bias, fox_q_gain, fox_k_gain, sb_q_gain, sb_k_gain, attn_w_out, conv_norm, conv_w_in, conv_kernel, conv_w_out, ffn_norm, ffn_w_up, ffn_conv, ffn_w_down, loss_target, m_attn_norm, m_attn_w_in, m_attn_f_bias, m_fox_q_gain, m_fox_k_gain, m_sb_q_gain, m_sb_k_gain, m_attn_w_out, m_conv_norm, m_conv_w_in, m_conv_kernel, m_conv_w_out, m_ffn_norm, m_ffn_w_up, m_ffn_conv, m_ffn_w_down, v_attn_norm, v_attn_w_in, v_attn_f_bias, v_fox_q_gain, v_fox_k_gain, v_sb_q_gain, v_sb_k_gain, v_attn_w_out, v_conv_norm, v_conv_w_in, v_conv_kernel, v_conv_w_out, v_ffn_norm, v_ffn_w_up, v_ffn_conv, v_ffn_w_down):
    s = x.shape[1]
    n_attn, n_conv, depth = attn_w_in.shape[0], conv_w_in.shape[0], ffn_w_up.shape[0]
    xi, yi, ci = _coords()
    me = 4 * xi + 2 * yi + ci
    ids = jnp.stack([ci, 2 * xi + yi]).astype(jnp.int32)

    big = [attn_w_in, attn_w_out, conv_w_in, conv_w_out, ffn_w_up, ffn_w_down]
    big_m = [m_attn_w_in, m_attn_w_out, m_conv_w_in, m_conv_w_out, m_ffn_w_up, m_ffn_w_down]
    big_v = [v_attn_w_in, v_attn_w_out, v_conv_w_in, v_conv_w_out, v_ffn_w_up, v_ffn_w_down]
    big_names = ["attn_w_in", "attn_w_out", "conv_w_in", "conv_w_out", "ffn_w_up", "ffn_w_down"]
    small_sh = [conv_norm, conv_kernel, ffn_conv]
    small_sh_shapes = [w.shape for w in small_sh]
    rep = [attn_norm, attn_f_bias, fox_q_gain, fox_k_gain, sb_q_gain, sb_k_gain, ffn_norm]
    rep_shapes = [w.shape for w in rep]

    small_pack = _pack(small_sh, LANES, 8)
    gathered = _all_gather([_rows2d(w).astype(BF16) for w in big] + [small_pack], "gather_weights")
    a_w_in = _dev_to_cols(gathered[0], n_attn)
    a_w_in = jnp.pad(a_w_in, ((0, 0), (0, 0), (0, ATTN_IN_PAD - ATTN_IN)))
    a_w_out = _dev_to_rows(gathered[1], n_attn)
    c_w_in = _dev_to_cols(gathered[2], n_conv)
    c_w_out = _dev_to_rows(gathered[3], n_conv)
    f_w_up = _dev_to_cols(gathered[4], depth)
    f_w_down = _dev_to_rows(gathered[5], depth)
    cn, ckern, fconv = _unpack(gathered[6].reshape(N_DEV, -1), small_sh_shapes)
    conv_norm_f = cn.transpose(1, 0, 2).reshape(n_conv, D_MODEL)
    conv_kernel_f = ckern.transpose(1, 2, 0, 3).reshape(n_conv, 3, D_MODEL)
    ffn_conv_f = fconv.transpose(1, 2, 0, 3).reshape(depth, 3, 2 * D_FF)

    def pair_gain(fox_g, sb_g):
        f2 = jnp.concatenate([fox_g, fox_g])
        s2 = jnp.concatenate([sb_g, sb_g])
        return jnp.concatenate([jnp.tile(f2[None], (4, 1)), jnp.tile(s2[None], (4, 1))])[:, None, :]

    h = x[0]
    saved = []
    for layer in range(depth):
        i = layer // 2
        tag = "l%d" % layer
        rec = {"h_in": h}
        if layer % 2 == 0:
            xn = _rms_fwd(h, attn_norm[i], tag + "_attn_rms")
            proj = _matmul(xn, a_w_in[i], tag + "_attn_in", tn=640)
            gq, gk = pair_gain(fox_q_gain[i], sb_q_gain[i]), pair_gain(fox_k_gain[i], sb_k_gain[i])
            qh, kh, vb, qt, kt, vt = _qkv_prep(proj, gq, gk, tag + "_qkv_prep")
            logit = proj[:, 3 * MIX:3 * MIX + H_FOX].T.reshape(H_FOX, s // LANES, LANES)
            cum = _fgate_fwd(logit, attn_f_bias[i], tag + "_fgate")
            frow = cum.reshape(H_FOX // 2, 2, s)
            fcol = frow.transpose(0, 2, 1)
            o_fox, lse = _fox_fwd(qt, kh, vt, frow, fcol, tag + "_fox_fwd")
            o_sb, tot = _sb_fwd(qt, kh, vt, tag + "_sb_fwd")
            o = jnp.concatenate([o_fox, o_sb], axis=1)
            h = _matmul(o, a_w_out[i], tag + "_attn_out", add=h)
            rec.update(xn=xn, proj=proj, gq=gq, gk=gk, qh=qh, kh=kh, vb=vb, qt=qt, kt=kt, logit=logit, frow=frow,
                       fcol=fcol, o_fox=o_fox, lse=lse, tot=tot, o=o)
        else:
            xn = _rms_fwd(h, conv_norm_f[i], tag + "_conv_rms")
            proj = _matmul(xn, c_w_in[i], tag + "_conv_in")
            y = _sconv_fwd(proj, conv_kernel_f[i], tag + "_sconv_fwd")
            h = _matmul(y, c_w_out[i], tag + "_conv_out", add=h)
            rec.update(xn=xn, proj=proj, y=y)
        rec["h_mid"] = h
        xn2 = _rms_fwd(h, ffn_norm[layer], tag + "_ffn_rms")
        up = _matmul(xn2, f_w_up[layer], tag + "_ffn_up")
        act = _ffn_act_fwd(up, ffn_conv_f[layer], tag + "_ffn_act")
        h = _matmul(act, f_w_down[layer], tag + "_ffn_down", add=h, tk=1408)
        rec.update(xn2=xn2, up=up, act=act)
        saved.append(rec)

    loss_local, dh = _loss_head(h, loss_target[0], "loss_head")
    loss = lax.psum(loss_local, ("x", "y", "c"))

    g_attn_norm, g_attn_w_in, g_f_bias = [None] * n_attn, [None] * n_attn, [None] * n_attn
    g_fq, g_fk, g_sq, g_sk, g_attn_w_out = ([None] * n_attn for _ in range(5))
    g_conv_norm, g_conv_w_in, g_conv_kernel, g_conv_w_out = ([None] * n_conv for _ in range(4))
    g_ffn_norm, g_ffn_w_up, g_ffn_conv, g_ffn_w_down = ([None] * depth for _ in range(4))

    for layer in reversed(range(depth)):
        i = layer // 2
        tag = "l%d" % layer
        rec = saved[layer]
        da = _matmul(dh, f_w_down[layer].T, tag + "_ffn_down_dx", tn=1408)
        g_ffn_w_down[layer] = _matmul_tn(rec["act"], dh, tag + "_ffn_down_dw", tm=1408, tn=1024)
        dug, duv, dwg, dwv = _ffn_act_bwd(rec["up"], ffn_conv_f[layer], da, tag + "_ffn_act_bwd")
        g_ffn_conv[layer] = jnp.concatenate([dwg, dwv], axis=1)
        dup = jnp.concatenate([dug, duv], axis=1)
        g_ffn_w_up[layer] = _matmul_tn(rec["xn2"], dup, tag + "_ffn_up_dw", tn=1408)
        dxn = _matmul(dup, f_w_up[layer].T, tag + "_ffn_up_dx", tn=1024)
        dh, g_ffn_norm[layer] = _rms_bwd(rec["h_mid"], dxn, ffn_norm[layer], dh, tag + "_ffn_rms_bwd")
        if layer % 2 == 0:
            do = _matmul(dh, a_w_out[i].T, tag + "_attn_out_dx", tn=1024)
            g_attn_w_out[i] = _matmul_tn(rec["o"], dh, tag + "_attn_out_dw", tn=1024)
            dq_f, dk_f, dv_f, dfk, dfq = _fox_bwd(rec["qh"], rec["qt"], rec["kh"], rec["kt"], rec["vb"], rec["frow"],
                                                  rec["fcol"], rec["lse"], rec["o_fox"], do, tag + "_fox_bwd")
            dq_s, dk_s, dv_s = _sb_bwd(rec["qh"], rec["qt"], rec["kh"], rec["kt"], rec["vb"], rec["tot"], do,
                                       tag + "_sb_bwd")
            dq, dk, dv, dgq, dgk = _qkv_prep_bwd(rec["proj"], rec["gq"], rec["gk"], (dq_f, dq_s), (dk_f, dk_s),
                                                 (dv_f, dv_s), tag + "_qkv_prep_bwd")
            dcum = (dfq + dfk[:, :, 0:2].transpose(0, 2, 1)).reshape(H_FOX, s // LANES, LANES)
            dlogit, dbias = _fgate_bwd(rec["logit"], attn_f_bias[i], dcum, tag + "_fgate_bwd")
            g_f_bias[i] = dbias[:, 0]
            dgate = jnp.pad(dlogit.reshape(H_FOX, s).T, ((0, 0), (0, LANES - H_FOX))).astype(BF16)
            dproj = jnp.concatenate([dq, dk, dv, dgate], axis=1)

            def fold(dg):
                per_head = dg.reshape(16, HEAD_DIM)
                return jnp.sum(per_head[:8], axis=0), jnp.sum(per_head[8:], axis=0)

            g_fq[i], g_sq[i] = fold(dgq)
            g_fk[i], g_sk[i] = fold(dgk)
            g_attn_w_in[i] = _matmul_tn(rec["xn"], dproj, tag + "_attn_in_dw", tn=640)[:, :ATTN_IN]
            dxn = _matmul(dproj, a_w_in[i].T, tag + "_attn_in_dx", tn=1024, tk=640)
            dh, g_attn_norm[i] = _rms_bwd(rec["h_in"], dxn, attn_norm[i], dh, tag + "_attn_rms_bwd")
        else:
            dy = _matmul(dh, c_w_out[i].T, tag + "_conv_out_dx", tn=1024)
            g_conv_w_out[i] = _matmul_tn(rec["y"], dh, tag + "_conv_out_dw", tn=1024)
            db, dc, du, g_conv_kernel[i] = _sconv_bwd(rec["proj"], conv_kernel_f[i], dy, tag + "_sconv_bwd")
            dproj = jnp.concatenate([db, dc, du], axis=1)
            g_conv_w_in[i] = _matmul_tn(rec["xn"], dproj, tag + "_conv_in_dw", tn=1024)
            dxn = _matmul(dproj, c_w_in[i].T, tag + "_conv_in_dx", tn=1024)
            dh, g_conv_norm[i] = _rms_bwd(rec["h_in"], dxn, conv_norm_f[i], dh, tag + "_conv_rms_bwd")
    grad_x = dh[None]

    gb = [_cols_to_dev(jnp.stack(g_attn_w_in).astype(BF16)), _rows_to_dev(jnp.stack(g_attn_w_out).astype(BF16)),
          _cols_to_dev(jnp.stack(g_conv_w_in).astype(BF16)), _rows_to_dev(jnp.stack(g_conv_w_out).astype(BF16)),
          _cols_to_dev(jnp.stack(g_ffn_w_up).astype(BF16)), _rows_to_dev(jnp.stack(g_ffn_w_down).astype(BF16))]
    from_sibling = _sibling_exchange(gb, "reduce_sibling")
    pairs = [_pair_add(g, r, ids, "reduce_pair_add_" + nm) for g, r, nm in zip(gb, from_sibling, big_names)]
    from_chips = _chip_exchange([p[0] for p in pairs], "reduce_chips")
    grads_big, delta_big, newm_big, newv_big = [], [], [], []
    for a, nm in enumerate(big_names):
        g_a, d_a, m_a, v_a = _adamw_reduce(pairs[a][1], from_chips[a], ids, _rows2d(big[a]), _rows2d(big_m[a]),
                                           _rows2d(big_v[a]), "adamw_" + nm)
        shp = big[a].shape
        grads_big.append(g_a.reshape(shp))
        delta_big.append(d_a.reshape(shp))
        newm_big.append(m_a.reshape(shp))
        newv_big.append(v_a.reshape(shp))

    rep_g = [jnp.stack(g_attn_norm), jnp.stack(g_f_bias), jnp.stack(g_fq), jnp.stack(g_fk), jnp.stack(g_sq),
             jnp.stack(g_sk), jnp.stack(g_ffn_norm)]
    sh_g = [jnp.stack(g_conv_norm).reshape(n_conv, N_DEV, -1).transpose(1, 0, 2),
            jnp.stack(g_conv_kernel).reshape(n_conv, 3, N_DEV, -1).transpose(2, 0, 1, 3),
            jnp.stack(g_ffn_conv).reshape(depth, 3, N_DEV, -1).transpose(2, 0, 1, 3)]
    n_rep = sum(int(a.size) for a in rep)
    n_sh = sum(int(a.size) for a in small_sh)
    partial = _pack(rep_g + [jnp.concatenate([a.reshape(N_DEV, -1) for a in sh_g], axis=1)], LANES, 8)
    total = _sum_sources(_all_gather([partial], "gather_small_grads")[0], "sum_small_grads").reshape(-1)
    rep_tot = total[:n_rep]
    sh_tot = lax.dynamic_slice_in_dim(total[n_rep:n_rep + N_DEV * n_sh].reshape(N_DEV, n_sh), me, 1, axis=0)[0]
    g_small = _pack([rep_tot, sh_tot], LANES, 8)

    def small_pack_of(rep_list, sh_list):
        return _pack(rep_list + sh_list, LANES, 8)

    d_small, m_small, v_small = _adamw_small(
        small_pack_of(rep, small_sh), g_small,
        small_pack_of([m_attn_norm, m_attn_f_bias, m_fox_q_gain, m_fox_k_gain, m_sb_q_gain, m_sb_k_gain, m_ffn_norm],
                      [m_conv_norm, m_conv_kernel, m_ffn_conv]),
        small_pack_of([v_attn_norm, v_attn_f_bias, v_fox_q_gain, v_fox_k_gain, v_sb_q_gain, v_sb_k_gain, v_ffn_norm],
                      [v_conv_norm, v_conv_kernel, v_ffn_conv]),
        "adamw_small")
    small_shapes = rep_shapes + small_sh_shapes

    def split_small(a):
        return _unpack(a.reshape(-1), small_shapes)

    def ordered(big_list, small_list):
        an, fb, fq, fk, sq, sk, fn, cno, cke, fco = small_list
        awi, awo, cwi, cwo, fwu, fwd = big_list
        return [an, awi, fb, fq, fk, sq, sk, awo, cno, cwi, cke, cwo, fn, fwu, fco, fwd]

    grads = ordered(grads_big, split_small(g_small))
    deltas = ordered(delta_big, split_small(d_small))
    new_m = ordered(newm_big, split_small(m_small))
    new_v = ordered(newv_big, split_small(v_small))
    return (loss, grad_x, *grads, *deltas, *new_m, *new_v)
```

```python
import jax
import jax.numpy as jnp
from jax import lax
from jax.experimental import pallas as pl
from jax.experimental.pallas import tpu as pltpu

F32 = jnp.float32
BF16 = jnp.bfloat16

D_MODEL = 1024
HEAD_DIM = 64
H_FOX = 8
MIX = 1024
ATTN_IN = 3 * MIX + H_FOX
ATTN_IN_PAD = 3 * MIX + 128
D_FF = 2816
EPS = 1e-6
NEG = -1e30
LANES = 128
N_DEV = 8
N_CHIP = 4

ADAM_LR = 0.001
ADAM_B1 = 0.9
ADAM_B2 = 0.999
ADAM_EPS = 1e-08
ADAM_WD = 0.01
ADAM_STEP = 10

VMEM_LIMIT = 56 * 1024 * 1024
MESH = pl.DeviceIdType.MESH
ANY = pl.BlockSpec(memory_space=pl.ANY)


def _params(*sem):
    return pltpu.CompilerParams(dimension_semantics=sem, vmem_limit_bytes=VMEM_LIMIT)


def _tile(n, target, mult=LANES):
    best = None
    for t in range(mult, min(n, target) + 1, mult):
        if n % t == 0:
            best = t
    return best if best is not None else n


def _dot(a, b):
    return jnp.dot(a, b, preferred_element_type=F32)


def _dot_tn(a, b):
    return lax.dot_general(a, b, (((0,), (0,)), ((), ())), preferred_element_type=F32)


def _split_dot(x, m, passes):
    acc = None
    rem = x
    for _ in range(passes):
        part = rem.astype(BF16)
        term = _dot(part, m)
        acc = term if acc is None else acc + term
        rem = rem - part.astype(F32)
    return acc


def _split_dot_left(m, x, passes):
    acc = None
    rem = x
    for _ in range(passes):
        part = rem.astype(BF16)
        term = _dot(m, part)
        acc = term if acc is None else acc + term
        rem = rem - part.astype(F32)
    return acc


def _matmul(a, b, name, add=None, out_dtype=F32, tm=1024, tn=512, tk=1024):
    m, k = a.shape
    n = b.shape[1]
    tm, tn, tk = _tile(m, tm, 8), _tile(n, tn), _tile(k, tk)
    nk = k // tk
    has_add = add is not None

    def body(*refs):
        a_ref, b_ref = refs[0], refs[1]
        add_ref = refs[2] if has_add else None
        o_ref = refs[2 + has_add]

        def finish(acc):
            if has_add:
                acc = acc + add_ref[...]
            o_ref[...] = acc.astype(out_dtype)

        p = _dot(a_ref[...].astype(BF16), b_ref[...].astype(BF16))
        if nk == 1:
            finish(p)
        else:
            acc_ref = refs[-1]
            kk = pl.program_id(2)

            @pl.when(kk == 0)
            def _():
                acc_ref[...] = p

            @pl.when(kk > 0)
            def _():
                acc_ref[...] += p

            @pl.when(kk == nk - 1)
            def _():
                finish(acc_ref[...])

    in_specs = [pl.BlockSpec((tm, tk), lambda i, j, kk: (i, kk)),
                pl.BlockSpec((tk, tn), lambda i, j, kk: (kk, j))]
    args = [a, b]
    if has_add:
        in_specs.append(pl.BlockSpec((tm, tn), lambda i, j, kk: (i, j)))
        args.append(add)
    return pl.pallas_call(
        body, name=name, grid=(m // tm, n // tn, nk), in_specs=in_specs,
        out_specs=pl.BlockSpec((tm, tn), lambda i, j, kk: (i, j)),
        out_shape=jax.ShapeDtypeStruct((m, n), out_dtype),
        scratch_shapes=[pltpu.VMEM((tm, tn), F32)] if nk > 1 else [],
        compiler_params=_params("parallel", "parallel", "arbitrary"),
    )(*args)


def _matmul_tn(a, b, name, tm=1024, tn=512, ts=1024):
    s, m = a.shape
    n = b.shape[1]
    tm, tn, ts = _tile(m, tm), _tile(n, tn), _tile(s, ts, 8)

    def body(a_ref, b_ref, o_ref):
        kk = pl.program_id(2)
        p = _dot_tn(a_ref[...].astype(BF16), b_ref[...].astype(BF16))

        @pl.when(kk == 0)
        def _():
            o_ref[...] = p

        @pl.when(kk > 0)
        def _():
            o_ref[...] += p

    return pl.pallas_call(
        body, name=name, grid=(m // tm, n // tn, s // ts),
        in_specs=[pl.BlockSpec((ts, tm), lambda i, j, kk: (kk, i)),
                  pl.BlockSpec((ts, tn), lambda i, j, kk: (kk, j))],
        out_specs=pl.BlockSpec((tm, tn), lambda i, j, kk: (i, j)),
        out_shape=jax.ShapeDtypeStruct((m, n), F32),
        compiler_params=_params("parallel", "parallel", "arbitrary"),
    )(a, b)


def _rms_fwd(h, g, name, ts=512):
    s, d = h.shape
    ts = _tile(s, ts, 8)

    def body(h_ref, g_ref, o_ref):
        x = h_ref[...]
        r = lax.rsqrt(jnp.mean(x * x, axis=-1, keepdims=True) + EPS)
        o_ref[...] = (x * r * g_ref[...]).astype(BF16)

    return pl.pallas_call(
        body, name=name, grid=(s // ts,),
        in_specs=[pl.BlockSpec((ts, d), lambda i: (i, 0)), pl.BlockSpec((1, d), lambda i: (0, 0))],
        out_specs=pl.BlockSpec((ts, d), lambda i: (i, 0)),
        out_shape=jax.ShapeDtypeStruct((s, d), BF16),
        compiler_params=_params("parallel"),
    )(h, g.reshape(1, d))


def _rms_bwd(h, dxn, g, dh_in, name, ts=512):
    s, d = h.shape
    ts = _tile(s, ts, 8)

    def body(h_ref, dxn_ref, g_ref, dhin_ref, dh_ref, dg_ref):
        i = pl.program_id(0)
        x = h_ref[...]
        r = lax.rsqrt(jnp.mean(x * x, axis=-1, keepdims=True) + EPS)
        xh = x * r
        dxn_v = dxn_ref[...]

        @pl.when(i == 0)
        def _():
            dg_ref[...] = jnp.zeros_like(dg_ref)

        dg_ref[0:1, :] += jnp.sum(dxn_v * xh, axis=0, keepdims=True)
        dxh = dxn_v * g_ref[...]
        dx = r * (dxh - xh * jnp.mean(dxh * xh, axis=-1, keepdims=True))
        dh_ref[...] = dhin_ref[...] + dx

    row = pl.BlockSpec((ts, d), lambda i: (i, 0))
    dh, dg = pl.pallas_call(
        body, name=name, grid=(s // ts,),
        in_specs=[row, row, pl.BlockSpec((1, d), lambda i: (0, 0)), row],
        out_specs=[row, pl.BlockSpec((8, d), lambda i: (0, 0))],
        out_shape=[jax.ShapeDtypeStruct((s, d), F32), jax.ShapeDtypeStruct((8, d), F32)],
        compiler_params=_params("arbitrary"),
    )(h, dxn, g.reshape(1, d), dh_in)
    return dh, dg[0]


def _shift_down(x, prev):
    rows = lax.broadcasted_iota(jnp.int32, x.shape, 0)
    p1, p2 = prev[7:8, :], prev[6:7, :]
    x1 = jnp.where(rows == 0, p1, pltpu.roll(x, 1, 0))
    x2 = jnp.where(rows == 0, p2, jnp.where(rows == 1, p1, pltpu.roll(x, 2, 0)))
    return x1, x2


def _shift_up(x, nxt):
    n = x.shape[0]
    rows = lax.broadcasted_iota(jnp.int32, x.shape, 0)
    n0, n1 = nxt[0:1, :], nxt[1:2, :]
    x1 = jnp.where(rows == n - 1, n0, pltpu.roll(x, n - 1, 0))
    x2 = jnp.where(rows == n - 1, n1, jnp.where(rows == n - 2, n0, pltpu.roll(x, n - 2, 0)))
    return x1, x2


def _conv(x, x1, x2, w):
    return w[2:3, :] * x + w[1:2, :] * x1 + w[0:1, :] * x2


def _halo_specs(ts, tc, col, n_time_blocks):
    r8 = ts // 8
    main = pl.BlockSpec((ts, tc), lambda j, i: (i, j + col))
    prev = pl.BlockSpec((8, tc), lambda j, i: (jnp.maximum(i * r8 - 1, 0), j + col))
    nxt = pl.BlockSpec((8, tc), lambda j, i: (jnp.minimum((i + 1) * r8, n_time_blocks * r8 - 1), j + col))
    return main, prev, nxt


def _silu_parts(g):
    sig = 1.0 / (1.0 + jnp.exp(-g))
    return sig, g * sig


def _ffn_act_fwd(up, cw, name, ts=256, tc=1408):
    s = up.shape[0]
    ts, tc = _tile(s, ts, 8), _tile(D_FF, tc)
    nc, nt = D_FF // tc, s // ts

    def body(g_ref, gp_ref, v_ref, vp_ref, wg_ref, wv_ref, o_ref):
        first = pl.program_id(1) == 0

        def conv(x_ref, p_ref, w_ref):
            x = x_ref[...]
            prev = jnp.where(first, 0.0, p_ref[...])
            x1, x2 = _shift_down(x, prev)
            return _conv(x, x1, x2, w_ref[...])

        ug = conv(g_ref, gp_ref, wg_ref)
        uv = conv(v_ref, vp_ref, wv_ref)
        _, silu = _silu_parts(ug)
        o_ref[...] = (silu * uv).astype(BF16)

    g_main, g_prev, _ = _halo_specs(ts, tc, 0, nt)
    v_main, v_prev, _ = _halo_specs(ts, tc, nc, nt)
    return pl.pallas_call(
        body, name=name, grid=(nc, nt),
        in_specs=[g_main, g_prev, v_main, v_prev,
                  pl.BlockSpec((3, tc), lambda j, i: (0, j)), pl.BlockSpec((3, tc), lambda j, i: (0, j + nc))],
        out_specs=pl.BlockSpec((ts, tc), lambda j, i: (i, j)),
        out_shape=jax.ShapeDtypeStruct((s, D_FF), BF16),
        compiler_params=_params("parallel", "parallel"),
    )(up, up, up, up, cw, cw)


def _ffn_act_bwd(up, cw, da, name, ts=256, tc=1408):
    s = up.shape[0]
    ts, tc = _tile(s, ts, 8), _tile(D_FF, tc)
    nc, nt = D_FF // tc, s // ts

    def body(g_ref, gp_ref, gn_ref, v_ref, vp_ref, vn_ref, da_ref, dan_ref, wg_ref, wv_ref,
             dg_ref, dv_ref, dwg_ref, dwv_ref):
        i = pl.program_id(1)
        first, last = i == 0, i == nt - 1
        wg, wv = wg_ref[...], wv_ref[...]
        g, v = g_ref[...], v_ref[...]
        g1, g2 = _shift_down(g, jnp.where(first, 0.0, gp_ref[...]))
        v1, v2 = _shift_down(v, jnp.where(first, 0.0, vp_ref[...]))

        def d_u(ug, uv, da_v):
            sig, silu = _silu_parts(ug)
            return da_v * uv * (sig * (1.0 + ug * (1.0 - sig))), da_v * silu

        dug, duv = d_u(_conv(g, g1, g2, wg), _conv(v, v1, v2, wv), da_ref[...])
        gn, vn = gn_ref[...], vn_ref[...]
        gn1, gn2 = _shift_down(gn, g[ts - 8:, :])
        vn1, vn2 = _shift_down(vn, v[ts - 8:, :])
        dugn, duvn = d_u(_conv(gn, gn1, gn2, wg), _conv(vn, vn1, vn2, wv), dan_ref[...])
        dugn = jnp.where(last, 0.0, dugn)
        duvn = jnp.where(last, 0.0, duvn)

        def finish(du, dun, x, x1, x2, w, dx_ref, dw_ref):
            d1, d2 = _shift_up(du, dun)
            dx_ref[...] = (w[2:3, :] * du + w[1:2, :] * d1 + w[0:1, :] * d2).astype(BF16)

            @pl.when(first)
            def _():
                dw_ref[...] = jnp.zeros_like(dw_ref)

            dw_ref[0:1, :] += jnp.sum(du * x2, axis=0, keepdims=True)
            dw_ref[1:2, :] += jnp.sum(du * x1, axis=0, keepdims=True)
            dw_ref[2:3, :] += jnp.sum(du * x, axis=0, keepdims=True)

        finish(dug, dugn, g, g1, g2, wg, dg_ref, dwg_ref)
        finish(duv, duvn, v, v1, v2, wv, dv_ref, dwv_ref)

    g_specs = _halo_specs(ts, tc, 0, nt)
    v_specs = _halo_specs(ts, tc, nc, nt)
    da_main, _, da_next = _halo_specs(ts, tc, 0, nt)
    tile = pl.BlockSpec((ts, tc), lambda j, i: (i, j))
    taps = pl.BlockSpec((8, tc), lambda j, i: (0, j))
    dg, dv, dwg, dwv = pl.pallas_call(
        body, name=name, grid=(nc, nt),
        in_specs=[*g_specs, *v_specs, da_main, da_next,
                  pl.BlockSpec((3, tc), lambda j, i: (0, j)), pl.BlockSpec((3, tc), lambda j, i: (0, j + nc))],
        out_specs=[tile, tile, taps, taps],
        out_shape=[jax.ShapeDtypeStruct((s, D_FF), BF16), jax.ShapeDtypeStruct((s, D_FF), BF16),
                   jax.ShapeDtypeStruct((8, D_FF), F32), jax.ShapeDtypeStruct((8, D_FF), F32)],
        compiler_params=_params("parallel", "arbitrary"),
    )(up, up, up, up, up, up, da, da, cw, cw)
    return dg, dv, dwg[:3], dwv[:3]


def _sconv_fwd(proj, ck, name, ts=256, tc=512):
    s = proj.shape[0]
    w = D_MODEL
    ts, tc = _tile(s, ts, 8), _tile(w, tc)
    nc, nt = w // tc, s // ts

    def body(b_ref, c_ref, cp_ref, u_ref, up_ref, w_ref, o_ref):
        first = pl.program_id(1) == 0
        cu = c_ref[...] * u_ref[...]
        cup = jnp.where(first, 0.0, cp_ref[...] * up_ref[...])
        x1, x2 = _shift_down(cu, cup)
        o_ref[...] = (b_ref[...] * _conv(cu, x1, x2, w_ref[...])).astype(BF16)

    b_main, _, _ = _halo_specs(ts, tc, 0, nt)
    c_main, c_prev, _ = _halo_specs(ts, tc, nc, nt)
    u_main, u_prev, _ = _halo_specs(ts, tc, 2 * nc, nt)
    return pl.pallas_call(
        body, name=name, grid=(nc, nt),
        in_specs=[b_main, c_main, c_prev, u_main, u_prev, pl.BlockSpec((3, tc), lambda j, i: (0, j))],
        out_specs=pl.BlockSpec((ts, tc), lambda j, i: (i, j)),
        out_shape=jax.ShapeDtypeStruct((s, w), BF16),
        compiler_params=_params("parallel", "parallel"),
    )(proj, proj, proj, proj, proj, ck)


def _sconv_bwd(proj, ck, dy, name, ts=256, tc=512):
    s = proj.shape[0]
    w = D_MODEL
    ts, tc = _tile(s, ts, 8), _tile(w, tc)
    nc, nt = w // tc, s // ts

    def body(b_ref, bn_ref, c_ref, cp_ref, u_ref, up_ref, dy_ref, dyn_ref, w_ref,
             db_ref, dc_ref, du_ref, dw_ref):
        i = pl.program_id(1)
        first, last = i == 0, i == nt - 1
        wv = w_ref[...]
        b, c, u, dy_v = b_ref[...], c_ref[...], u_ref[...], dy_ref[...]
        cu = c * u
        cup = jnp.where(first, 0.0, cp_ref[...] * up_ref[...])
        x1, x2 = _shift_down(cu, cup)
        db_ref[...] = (dy_v * _conv(cu, x1, x2, wv)).astype(BF16)
        dcv = dy_v * b
        dcvn = jnp.where(last, 0.0, dyn_ref[...] * bn_ref[...])
        d1, d2 = _shift_up(dcv, dcvn)
        dcu = wv[2:3, :] * dcv + wv[1:2, :] * d1 + wv[0:1, :] * d2
        dc_ref[...] = (dcu * u).astype(BF16)
        du_ref[...] = (dcu * c).astype(BF16)

        @pl.when(first)
        def _():
            dw_ref[...] = jnp.zeros_like(dw_ref)

        dw_ref[0:1, :] += jnp.sum(dcv * x2, axis=0, keepdims=True)
        dw_ref[1:2, :] += jnp.sum(dcv * x1, axis=0, keepdims=True)
        dw_ref[2:3, :] += jnp.sum(dcv * cu, axis=0, keepdims=True)

    b_main, _, b_next = _halo_specs(ts, tc, 0, nt)
    c_main, c_prev, _ = _halo_specs(ts, tc, nc, nt)
    u_main, u_prev, _ = _halo_specs(ts, tc, 2 * nc, nt)
    dy_main, _, dy_next = _halo_specs(ts, tc, 0, nt)
    tile = pl.BlockSpec((ts, tc), lambda j, i: (i, j))
    out = jax.ShapeDtypeStruct((s, w), BF16)
    db, dc, du, dw = pl.pallas_call(
        body, name=name, grid=(nc, nt),
        in_specs=[b_main, b_next, c_main, c_prev, u_main, u_prev, dy_main, dy_next,
                  pl.BlockSpec((3, tc), lambda j, i: (0, j))],
        out_specs=[tile, tile, tile, pl.BlockSpec((8, tc), lambda j, i: (0, j))],
        out_shape=[out, out, out, jax.ShapeDtypeStruct((8, w), F32)],
        compiler_params=_params("parallel", "arbitrary"),
    )(proj, proj, proj, proj, proj, proj, dy, dy, ck)
    return db, dc, du, dw[:3]


def _low_lanes(shape):
    return lax.broadcasted_iota(jnp.int32, shape, 1) < HEAD_DIM


def _top_rows(shape):
    return lax.broadcasted_iota(jnp.int32, shape, 0) < HEAD_DIM


def _norm_pair(x):
    lo = _low_lanes(x.shape)
    sq = x * x
    s_lo = jnp.sum(jnp.where(lo, sq, 0.0), axis=1, keepdims=True)
    s_hi = jnp.sum(jnp.where(lo, 0.0, sq), axis=1, keepdims=True)
    r = lax.rsqrt(jnp.where(lo, s_lo, s_hi) * (1.0 / HEAD_DIM) + EPS)
    return x * r, r


def _mean_pair(x):
    lo = _low_lanes(x.shape)
    s_lo = jnp.sum(jnp.where(lo, x, 0.0), axis=1, keepdims=True)
    s_hi = jnp.sum(jnp.where(lo, 0.0, x), axis=1, keepdims=True)
    return jnp.where(lo, s_lo, s_hi) * (1.0 / HEAD_DIM)


def _qkv_prep(proj, gq, gk, name, ts=512):
    s = proj.shape[0]
    ts = _tile(s, ts)
    npair = MIX // LANES
    scale = HEAD_DIM ** -0.5

    def body(q_ref, k_ref, v_ref, gq_ref, gk_ref, qo_ref, ko_ref, vo_ref, qt_ref, kt_ref, vt_ref):
        qn, _ = _norm_pair(q_ref[...])
        kn, _ = _norm_pair(k_ref[...])
        q = qn * gq_ref[0] * scale
        k = kn * gk_ref[0]
        v = v_ref[...]
        qo_ref[...] = q.astype(BF16)
        ko_ref[...] = k.astype(BF16)
        vo_ref[...] = v.astype(BF16)
        qt_ref[...] = q.T.astype(BF16)
        kt_ref[...] = k.T.astype(BF16)
        vt_ref[...] = v.T.astype(BF16)

    gain = pl.BlockSpec((1, 1, LANES), lambda i, p: (p, 0, 0))
    tile = pl.BlockSpec((ts, LANES), lambda i, p: (i, p))
    tile_t = pl.BlockSpec((LANES, ts), lambda i, p: (p, i))
    out = jax.ShapeDtypeStruct((s, MIX), BF16)
    out_t = jax.ShapeDtypeStruct((MIX, s), BF16)
    return pl.pallas_call(
        body, name=name, grid=(s // ts, npair),
        in_specs=[tile, pl.BlockSpec((ts, LANES), lambda i, p: (i, p + npair)),
                  pl.BlockSpec((ts, LANES), lambda i, p: (i, p + 2 * npair)), gain, gain],
        out_specs=[tile, tile, tile, tile_t, tile_t, tile_t], out_shape=[out, out, out, out_t, out_t, out_t],
        compiler_params=_params("parallel", "parallel"),
    )(proj, proj, proj, gq, gk)


def _qkv_prep_bwd(proj, gq, gk, dqs, dks, dvs, name, ts=512):
    s = proj.shape[0]
    ts = _tile(s, ts, 8)
    npair = MIX // LANES
    half = npair // 2
    scale = HEAD_DIM ** -0.5

    def body(q_ref, k_ref, gq_ref, gk_ref, dqf_ref, dqs_ref, dkf_ref, dks_ref, dvf_ref, dvs_ref,
             dq_ref, dk_ref, dv_ref, dgq_ref, dgk_ref):
        p, i = pl.program_id(0), pl.program_id(1)
        fox = p < half

        def one(x_ref, g_ref, df_ref, ds_ref, dx_ref, dg_ref, mult):
            dn = jnp.where(fox, df_ref[...], ds_ref[...]) * mult
            xh, r = _norm_pair(x_ref[...])

            @pl.when(i == 0)
            def _():
                dg_ref[...] = jnp.zeros_like(dg_ref)

            dg_ref[0, 0:1, :] += jnp.sum(dn * xh, axis=0, keepdims=True)
            dxh = dn * g_ref[0]
            dx_ref[...] = (r * (dxh - xh * _mean_pair(dxh * xh))).astype(BF16)

        one(q_ref, gq_ref, dqf_ref, dqs_ref, dq_ref, dgq_ref, scale)
        one(k_ref, gk_ref, dkf_ref, dks_ref, dk_ref, dgk_ref, 1.0)
        dv_ref[...] = jnp.where(fox, dvf_ref[...], dvs_ref[...]).astype(BF16)

    gain = pl.BlockSpec((1, 1, LANES), lambda p, i: (p, 0, 0))
    tile = pl.BlockSpec((ts, LANES), lambda p, i: (i, p))
    fpart = pl.BlockSpec((ts, LANES), lambda p, i: (i, jnp.minimum(p, half - 1)))
    spart = pl.BlockSpec((ts, LANES), lambda p, i: (i, jnp.maximum(p - half, 0)))
    dgain = pl.BlockSpec((1, 8, LANES), lambda p, i: (p, 0, 0))
    out = jax.ShapeDtypeStruct((s, MIX), BF16)
    gshape = jax.ShapeDtypeStruct((npair, 8, LANES), F32)
    dq, dk, dv, dgq, dgk = pl.pallas_call(
        body, name=name, grid=(npair, s // ts),
        in_specs=[tile, pl.BlockSpec((ts, LANES), lambda p, i: (i, p + npair)), gain, gain,
                  fpart, spart, fpart, spart, fpart, spart],
        out_specs=[tile, tile, tile, dgain, dgain], out_shape=[out, out, out, gshape, gshape],
        compiler_params=_params("parallel", "arbitrary"),
    )(proj, proj, gq, gk, dqs[0], dqs[1], dks[0], dks[1], dvs[0], dvs[1])
    return dq, dk, dv, dgq[:, 0, :], dgk[:, 0, :]


def _tri(n, rel):
    a = lax.broadcasted_iota(jnp.int32, (n, n), 0)
    b = lax.broadcasted_iota(jnp.int32, (n, n), 1)
    return rel(a, b).astype(BF16)


def _fgate_fwd(logit, bias, name):
    nh, r, _ = logit.shape

    def body(x_ref, b_ref, o_ref):
        within = _tri(LANES, lambda a, b: a <= b)
        before = _tri(r, lambda a, b: b < a)
        for hh in range(nh):
            x = x_ref[hh] + b_ref[hh]
            lf = jnp.minimum(x, 0.0) - jnp.log1p(jnp.exp(-jnp.abs(x)))
            c = _split_dot(lf, within, 3)
            tot = jnp.broadcast_to(c[:, LANES - 1:LANES], (r, LANES))
            o_ref[hh] = c + _split_dot_left(before, tot, 3)

    return pl.pallas_call(
        body, name=name, out_shape=jax.ShapeDtypeStruct((nh, r, LANES), F32),
        in_specs=[pl.BlockSpec(memory_space=pltpu.VMEM), pl.BlockSpec(memory_space=pltpu.SMEM)],
        out_specs=pl.BlockSpec(memory_space=pltpu.VMEM),
    )(logit, bias)


def _fgate_bwd(logit, bias, dcum, name):
    nh, r, _ = logit.shape

    def body(x_ref, b_ref, d_ref, dx_ref, db_ref):
        within = _tri(LANES, lambda a, b: a >= b)
        after = _tri(r, lambda a, b: b > a)
        for hh in range(nh):
            x = x_ref[hh] + b_ref[hh]
            d = d_ref[hh]
            c = _split_dot(d, within, 3)
            tot = jnp.broadcast_to(c[:, 0:1], (r, LANES))
            dlf = c + _split_dot_left(after, tot, 3)
            dx = dlf * (1.0 / (1.0 + jnp.exp(x)))
            dx_ref[hh] = dx
            db_ref[hh:hh + 1, :] = jnp.broadcast_to(jnp.sum(dx, keepdims=True).reshape(1, 1), (1, LANES))

    return pl.pallas_call(
        body, name=name,
        out_shape=[jax.ShapeDtypeStruct((nh, r, LANES), F32), jax.ShapeDtypeStruct((nh, LANES), F32)],
        in_specs=[pl.BlockSpec(memory_space=pltpu.VMEM), pl.BlockSpec(memory_space=pltpu.SMEM),
                  pl.BlockSpec(memory_space=pltpu.VMEM)],
        out_specs=[pl.BlockSpec(memory_space=pltpu.VMEM), pl.BlockSpec(memory_space=pltpu.VMEM)],
    )(logit, bias, dcum)


def _pair_masks(x):
    lo = _low_lanes(x.shape)
    zero = jnp.zeros_like(x)
    return jnp.where(lo, x, zero), jnp.where(lo, zero, x)


def _pair_masks_t(x):
    top = _top_rows(x.shape)
    zero = jnp.zeros_like(x)
    return jnp.where(top, x, zero), jnp.where(top, zero, x)


def _stack_heads(x):
    return jnp.concatenate(_pair_masks(x), axis=0)


def _stack_heads_t(x):
    return jnp.concatenate(_pair_masks_t(x), axis=1)


def _pair_colsum_t(x):
    top = _top_rows(x.shape)
    return (jnp.sum(jnp.where(top, x, 0.0), axis=0, keepdims=True),
            jnp.sum(jnp.where(top, 0.0, x), axis=0, keepdims=True))


def _key_query_iotas(t):
    return lax.broadcasted_iota(jnp.int32, (t, t), 0), lax.broadcasted_iota(jnp.int32, (t, t), 1)


def _walk_blocks(i, step, descending):
    pairs = i // 2
    odd = i % 2 == 1
    if descending:
        step([(i, True)])

        def loop(jj, carry):
            step([(i - 1 - 2 * jj, False), (i - 2 - 2 * jj, False)])
            return carry

        lax.fori_loop(0, pairs, loop, 0)

        @pl.when(odd)
        def _():
            step([(0, False)])
    else:
        def loop(jj, carry):
            step([(2 * jj, False), (2 * jj + 1, False)])
            return carry

        lax.fori_loop(0, pairs, loop, 0)

        @pl.when(odd)
        def _():
            step([(i - 1, False)])

        step([(i, True)])


def _attn_specs(s, tq, pair0):
    q_nat = pl.BlockSpec((tq, LANES), lambda p, i: (i, p + pair0))
    q_t = pl.BlockSpec((LANES, tq), lambda p, i: (p + pair0, i))
    k_nat = pl.BlockSpec((s, LANES), lambda p, i: (0, p + pair0))
    k_t = pl.BlockSpec((LANES, s), lambda p, i: (p + pair0, 0))
    return q_nat, q_t, k_nat, k_t


def _fox_fwd(qt, kh, vt, frow, fcol, name, tq=256):
    s = kh.shape[0]
    tq = _tile(s, tq)
    nq, half = s // tq, MIX // LANES // 2

    def body(qt_ref, k_ref, vt_ref, fr_ref, fc_ref, o_ref, lse_ref, m_s, l_s, acc_s):
        i = pl.program_id(1)
        qt_v = qt_ref[...]
        ft = fr_ref[0]
        key, qry = _key_query_iotas(tq)
        causal = key <= qry
        m_s[...] = jnp.full(m_s.shape, NEG, F32)
        l_s[...] = jnp.zeros_like(l_s)
        acc_s[...] = jnp.zeros_like(acc_s)

        top = _top_rows((LANES, tq))

        def step(blocks):
            rows = [pl.ds(pl.multiple_of(j * tq, tq), tq) for j, _ in blocks]
            z = _dot(jnp.concatenate([_stack_heads(k_ref[r, :]) for r in rows], axis=0), qt_v)
            vt_all = jnp.concatenate([_stack_heads_t(vt_ref[:, r]) for r in rows], axis=1)
            scs = []
            for b, (_, masked) in enumerate(blocks):
                fk = fc_ref[0, rows[b], :]
                for hh in range(2):
                    n = 2 * b + hh
                    sc = z[n * tq:(n + 1) * tq] + (ft[hh:hh + 1, :] - fk[:, hh:hh + 1])
                    if masked:
                        sc = jnp.where(causal, sc, NEG)
                    scs.append(sc)
            alphas, m_news = [], []
            for hh in range(2):
                m_old = m_s[hh]
                m_new = m_old
                for b in range(len(blocks)):
                    m_new = jnp.maximum(m_new, jnp.max(scs[2 * b + hh], axis=0, keepdims=True))
                alphas.append(jnp.exp(m_old - m_new))
                m_news.append(m_new)
                m_s[hh] = m_new
            prs = []
            sums = [None, None]
            for b in range(len(blocks)):
                for hh in range(2):
                    pr = jnp.exp(scs[2 * b + hh] - m_news[hh])
                    part = jnp.sum(pr, axis=0, keepdims=True)
                    sums[hh] = part if sums[hh] is None else sums[hh] + part
                    prs.append(pr.astype(BF16))
            for hh in range(2):
                l_s[hh] = alphas[hh] * l_s[hh] + sums[hh]
            pv = _dot(vt_all, jnp.concatenate(prs, axis=0))
            acc_s[...] = jnp.where(top, alphas[0], alphas[1]) * acc_s[...] + pv

        _walk_blocks(i, step, descending=False)
        o_ref[...] = (acc_s[...] / jnp.where(top, l_s[0], l_s[1])).T
        lse_ref[0, 0:1, :] = m_s[0] + jnp.log(l_s[0])
        lse_ref[0, 1:2, :] = m_s[1] + jnp.log(l_s[1])

    _, q_t, k_nat, k_t = _attn_specs(s, tq, 0)
    qstat = pl.BlockSpec((1, 2, tq), lambda p, i: (p, 0, i))
    return pl.pallas_call(
        body, name=name, grid=(half, nq),
        in_specs=[q_t, k_nat, k_t, qstat, pl.BlockSpec((1, s, 2), lambda p, i: (p, 0, 0))],
        out_specs=[pl.BlockSpec((tq, LANES), lambda p, i: (i, p)), qstat],
        out_shape=[jax.ShapeDtypeStruct((s, MIX // 2), F32), jax.ShapeDtypeStruct((half, 2, s), F32)],
        scratch_shapes=[pltpu.VMEM((2, 1, tq), F32), pltpu.VMEM((2, 1, tq), F32), pltpu.VMEM((LANES, tq), F32)],
        compiler_params=_params("parallel", "arbitrary"),
    )(qt, kh, vt, frow, fcol)


def _fox_bwd(qh, qt, kh, kt, vb, frow, fcol, lse, o, do, name, tq=256):
    s = kh.shape[0]
    tq = _tile(s, tq)
    nq, half = s // tq, MIX // LANES // 2

    def body(q_ref, qt_ref, k_ref, kt_ref, v_ref, fr_ref, fc_ref, lse_ref, o_ref, do_ref,
             dq_ref, dk_ref, dv_ref, dfk_ref, dfq_ref, dq_s, rs_s):
        i = pl.program_id(1)

        @pl.when(i == 0)
        def _():
            dk_ref[...] = jnp.zeros_like(dk_ref)
            dv_ref[...] = jnp.zeros_like(dv_ref)
            dfk_ref[...] = jnp.zeros_like(dfk_ref)

        q2 = _stack_heads(q_ref[...])
        qt_v = qt_ref[...]
        do_v = do_ref[...]
        do2 = _stack_heads(do_v.astype(BF16))
        dot_v = do_v.T.astype(BF16)
        dsum = _pair_colsum_t((do_v * o_ref[...]).T)
        ft, ls = fr_ref[0], lse_ref[0]
        key, qry = _key_query_iotas(tq)
        causal = key <= qry
        lane = lax.broadcasted_iota(jnp.int32, (2 * tq, LANES), 0) // tq
        pick2 = (lax.broadcasted_iota(jnp.int32, (2 * tq, LANES), 1) == lane).astype(BF16)
        dq_s[...] = jnp.zeros_like(dq_s)
        rs_s[...] = jnp.zeros_like(rs_s)

        def step(blocks):
            rows = [pl.ds(pl.multiple_of(j * tq, tq), tq) for j, _ in blocks]
            z = _dot(jnp.concatenate([_stack_heads(k_ref[r, :]) for r in rows], axis=0), qt_v)
            dp = _dot(jnp.concatenate([_stack_heads(v_ref[r, :]) for r in rows], axis=0), dot_v)
            kt_all = jnp.concatenate([_stack_heads_t(kt_ref[:, r]) for r in rows], axis=1)
            prs, dss = [], []
            for b, (_, masked) in enumerate(blocks):
                fk = fc_ref[0, rows[b], :]
                for hh in range(2):
                    blk = slice((2 * b + hh) * tq, (2 * b + hh + 1) * tq)
                    sc = z[blk] + (ft[hh:hh + 1, :] - fk[:, hh:hh + 1])
                    pr = jnp.exp(sc - ls[hh:hh + 1, :])
                    if masked:
                        pr = jnp.where(causal, pr, 0.0)
                    dsb = (pr * (dp[blk] - dsum[hh])).astype(BF16)
                    rs_s[hh] += jnp.sum(dsb.astype(F32), axis=0, keepdims=True)
                    prs.append(pr.astype(BF16))
                    dss.append(dsb)
            for b in range(len(blocks)):
                ds_wide = jnp.concatenate(dss[2 * b:2 * b + 2], axis=1)
                dv_ref[rows[b], :] += _dot(jnp.concatenate(prs[2 * b:2 * b + 2], axis=1), do2)
                dk_ref[rows[b], :] += _dot(ds_wide, q2)
                dfk_ref[0, rows[b], :] -= _dot(ds_wide, pick2)
            dq_s[...] += _dot(kt_all, jnp.concatenate(dss, axis=0))

        _walk_blocks(i, step, descending=False)
        dq_ref[...] = dq_s[...].T
        dfq_ref[0, 0:1, :] = rs_s[0]
        dfq_ref[0, 1:2, :] = rs_s[1]

    q_nat, q_t, k_nat, k_t = _attn_specs(s, tq, 0)
    qstat = pl.BlockSpec((1, 2, tq), lambda p, i: (p, 0, i))
    otile = pl.BlockSpec((tq, LANES), lambda p, i: (i, p))
    oresident = pl.BlockSpec((s, LANES), lambda p, i: (0, p))
    out = jax.ShapeDtypeStruct((s, MIX // 2), F32)
    return pl.pallas_call(
        body, name=name, grid=(half, nq),
        in_specs=[q_nat, q_t, k_nat, k_t, k_nat, qstat, pl.BlockSpec((1, s, 2), lambda p, i: (p, 0, 0)), qstat,
                  otile, q_nat],
        out_specs=[otile, oresident, oresident, pl.BlockSpec((1, s, LANES), lambda p, i: (p, 0, 0)), qstat],
        out_shape=[out, out, out, jax.ShapeDtypeStruct((half, s, LANES), F32),
                   jax.ShapeDtypeStruct((half, 2, s), F32)],
        scratch_shapes=[pltpu.VMEM((LANES, tq), F32), pltpu.VMEM((2, 1, tq), F32)],
        compiler_params=_params("parallel", "arbitrary"),
    )(qh, qt, kh, kt, vb, frow, fcol, lse, o, do)


def _log_sig_pair(z):
    lb = jnp.minimum(z, 0.0) - jnp.log(1.0 + jnp.exp(-jnp.abs(z)))
    return lb, lb - z


def _sb_fwd(qt, kh, vt, name, tq=256):
    s = kh.shape[0]
    tq = _tile(s, tq)
    nq, half = s // tq, MIX // LANES // 2

    def body(qt_ref, k_ref, vt_ref, o_ref, tot_ref, c_s, acc_s):
        i = pl.program_id(1)
        qt_v = qt_ref[...]
        key, qry = _key_query_iotas(tq)
        strict = key < qry
        later = _tri(tq, lambda a, b: b > a)
        c_s[...] = jnp.zeros_like(c_s)
        acc_s[...] = jnp.zeros_like(acc_s)

        def step(blocks):
            rows = [pl.ds(pl.multiple_of(j * tq, tq), tq) for j, _ in blocks]
            z = _dot(jnp.concatenate([_stack_heads(k_ref[r, :]) for r in rows], axis=0), qt_v)
            vt_all = jnp.concatenate([_stack_heads_t(vt_ref[:, r]) for r in rows], axis=1)
            lbs, loms = [], []
            for b, (_, masked) in enumerate(blocks):
                for hh in range(2):
                    n = 2 * b + hh
                    lb, lom = _log_sig_pair(z[n * tq:(n + 1) * tq])
                    if masked:
                        lom = jnp.where(strict, lom, 0.0)
                    lbs.append(lb)
                    loms.append(lom)
            after = _split_dot_left(later, jnp.concatenate(loms, axis=1), 2)
            carry = [c_s[0], c_s[1]]
            ws = []
            for b, (_, masked) in enumerate(blocks):
                for hh in range(2):
                    n = 2 * b + hh
                    w = jnp.exp(lbs[n] + after[:, n * tq:(n + 1) * tq] + carry[hh])
                    if masked:
                        w = jnp.where(strict, w, 0.0)
                    ws.append(w.astype(BF16))
                    carry[hh] = carry[hh] + jnp.sum(loms[n], axis=0, keepdims=True)
            c_s[0], c_s[1] = carry
            acc_s[...] += _dot(vt_all, jnp.concatenate(ws, axis=0))

        _walk_blocks(i, step, descending=True)
        o_ref[...] = acc_s[...].T
        tot_ref[0, 0:1, :] = c_s[0]
        tot_ref[0, 1:2, :] = c_s[1]

    _, q_t, k_nat, k_t = _attn_specs(s, tq, half)
    qstat = pl.BlockSpec((1, 2, tq), lambda p, i: (p, 0, i))
    return pl.pallas_call(
        body, name=name, grid=(half, nq),
        in_specs=[q_t, k_nat, k_t],
        out_specs=[pl.BlockSpec((tq, LANES), lambda p, i: (i, p)), qstat],
        out_shape=[jax.ShapeDtypeStruct((s, MIX // 2), F32), jax.ShapeDtypeStruct((half, 2, s), F32)],
        scratch_shapes=[pltpu.VMEM((2, 1, tq), F32), pltpu.VMEM((LANES, tq), F32)],
        compiler_params=_params("parallel", "arbitrary"),
    )(qt, kh, vt)


def _sb_bwd(qh, qt, kh, kt, vb, tot, do, name, tq=256):
    s = kh.shape[0]
    tq = _tile(s, tq)
    nq, half = s // tq, MIX // LANES // 2

    def body(q_ref, qt_ref, k_ref, kt_ref, v_ref, tot_ref, do_ref, dq_ref, dk_ref, dv_ref, rem_s, pg_s, dq_s):
        i = pl.program_id(1)

        @pl.when(i == 0)
        def _():
            dk_ref[...] = jnp.zeros_like(dk_ref)
            dv_ref[...] = jnp.zeros_like(dv_ref)

        q2 = _stack_heads(q_ref[...])
        qt_v = qt_ref[...]
        do_v = do_ref[...]
        do2 = _stack_heads(do_v.astype(BF16))
        dot_v = do_v.T.astype(BF16)
        key, qry = _key_query_iotas(tq)
        strict = key < qry
        upto = _tri(tq, lambda a, b: b <= a)
        before = _tri(tq, lambda a, b: b < a)
        tv = tot_ref[0]
        rem_s[0] = tv[0:1, :]
        rem_s[1] = tv[1:2, :]
        pg_s[...] = jnp.zeros_like(pg_s)
        dq_s[...] = jnp.zeros_like(dq_s)

        def step(blocks):
            nb = len(blocks)
            rows = [pl.ds(pl.multiple_of(j * tq, tq), tq) for j, _ in blocks]
            z = _dot(jnp.concatenate([_stack_heads(k_ref[r, :]) for r in rows], axis=0), qt_v)
            dw = _dot(jnp.concatenate([_stack_heads(v_ref[r, :]) for r in rows], axis=0), dot_v)
            kt_all = jnp.concatenate([_stack_heads_t(kt_ref[:, r]) for r in rows], axis=1)
            lbs, loms = [], []
            for b, (_, masked) in enumerate(blocks):
                for hh in range(2):
                    n = 2 * b + hh
                    lb, lom = _log_sig_pair(z[n * tq:(n + 1) * tq])
                    if masked:
                        lom = jnp.where(strict, lom, 0.0)
                    lbs.append(lb)
                    loms.append(lom)
            prefix = _split_dot_left(upto, jnp.concatenate(loms, axis=1), 2)
            rem = [rem_s[0], rem_s[1]]
            ws, gs = [], []
            for b, (_, masked) in enumerate(blocks):
                for hh in range(2):
                    n = 2 * b + hh
                    blk = slice(n * tq, (n + 1) * tq)
                    w = jnp.exp(lbs[n] + (rem[hh] - prefix[:, blk]))
                    if masked:
                        w = jnp.where(strict, w, 0.0)
                    gs.append(dw[blk] * w)
                    ws.append(w.astype(BF16))
                    rem[hh] = rem[hh] - jnp.sum(loms[n], axis=0, keepdims=True)
            rem_s[0], rem_s[1] = rem
            gpre = _split_dot_left(before, jnp.concatenate(gs, axis=1), 2)
            pg = [pg_s[0], pg_s[1]]
            dzs = []
            for b, (_, masked) in enumerate(blocks):
                for hh in range(2):
                    n = 2 * b + hh
                    g = gs[n]
                    dz = g - jnp.exp(lbs[n]) * (g + (pg[hh] + gpre[:, n * tq:(n + 1) * tq]))
                    if masked:
                        dz = jnp.where(strict, dz, 0.0)
                    dzs.append(dz.astype(BF16))
                    pg[hh] = pg[hh] + jnp.sum(g, axis=0, keepdims=True)
            pg_s[0], pg_s[1] = pg
            for b in range(nb):
                dv_ref[rows[b], :] += _dot(jnp.concatenate(ws[2 * b:2 * b + 2], axis=1), do2)
                dk_ref[rows[b], :] += _dot(jnp.concatenate(dzs[2 * b:2 * b + 2], axis=1), q2)
            dq_s[...] += _dot(kt_all, jnp.concatenate(dzs, axis=0))

        _walk_blocks(i, step, descending=False)
        dq_ref[...] = dq_s[...].T

    q_nat, q_t, k_nat, k_t = _attn_specs(s, tq, half)
    qstat = pl.BlockSpec((1, 2, tq), lambda p, i: (p, 0, i))
    otile = pl.BlockSpec((tq, LANES), lambda p, i: (i, p))
    oresident = pl.BlockSpec((s, LANES), lambda p, i: (0, p))
    out = jax.ShapeDtypeStruct((s, MIX // 2), F32)
    return pl.pallas_call(
        body, name=name, grid=(half, nq),
        in_specs=[q_nat, q_t, k_nat, k_t, k_nat, qstat, q_nat],
        out_specs=[otile, oresident, oresident], out_shape=[out, out, out],
        scratch_shapes=[pltpu.VMEM((2, 1, tq), F32), pltpu.VMEM((2, 1, tq), F32), pltpu.VMEM((LANES, tq), F32)],
        compiler_params=_params("parallel", "arbitrary"),
    )(qh, qt, kh, kt, vb, tot, do)


def _loss_head(y, target, name, ts=512):
    s, d = y.shape
    ts = _tile(s, ts, 8)
    nt = s // ts

    def body(y_ref, t_ref, dy_ref, l_ref, acc):
        i = pl.program_id(0)
        err = y_ref[...] - t_ref[...]
        dy_ref[...] = err * (1.0 / d)

        @pl.when(i == 0)
        def _():
            acc[...] = jnp.zeros_like(acc)

        acc[...] += jnp.sum(err * err, axis=0, keepdims=True)

        @pl.when(i == nt - 1)
        def _():
            tot = jnp.sum(acc[...], keepdims=True).reshape(1, 1) * (0.5 / d)
            l_ref[...] = jnp.broadcast_to(tot, l_ref.shape)

    row = pl.BlockSpec((ts, d), lambda i: (i, 0))
    dy, l = pl.pallas_call(
        body, name=name, grid=(nt,), in_specs=[row, row],
        out_specs=[row, pl.BlockSpec((8, LANES), lambda i: (0, 0))],
        out_shape=[jax.ShapeDtypeStruct((s, d), F32), jax.ShapeDtypeStruct((8, LANES), F32)],
        scratch_shapes=[pltpu.VMEM((1, d), F32)],
        compiler_params=_params("arbitrary"),
    )(y, target)
    return l[0, 0], dy


def _coords():
    return lax.axis_index("x"), lax.axis_index("y"), lax.axis_index("c")


def _other_chips(xi, yi):
    return [(1 - xi, yi), (xi, 1 - yi), (1 - xi, 1 - yi)]


def _all_gather(xs, name):
    n = len(xs)

    def body(*refs):
        x_refs, out_refs = refs[:n], refs[n:2 * n]
        send_sems, recv_sems, local_sems = refs[2 * n:]
        xi, yi, ci = _coords()
        me, sibling = (xi, yi, ci), (xi, yi, 1 - ci)
        chips = _other_chips(xi, yi)

        def slot(a, px, py, pc):
            return out_refs[a].at[4 * px + 2 * py + pc]

        def copy(a, k, block, to, src=None):
            return pltpu.make_async_remote_copy(
                src_ref=slot(a, *block) if src is None else src, dst_ref=slot(a, *block),
                send_sem=send_sems.at[a, k], recv_sem=recv_sems.at[a, k], device_id=to, device_id_type=MESH)

        mine = [pltpu.make_async_copy(x_refs[a], slot(a, *me), local_sems.at[a]) for a in range(n)]
        for cp in mine:
            cp.start()
        first = []
        for a in range(n):
            first.append(copy(a, 0, me, sibling, src=x_refs[a]))
            first += [copy(a, 1 + j, me, (*chip, ci), src=x_refs[a]) for j, chip in enumerate(chips)]
        for cp in first:
            cp.start()
        passed = []
        for j, chip in enumerate(chips):
            for a in range(n):
                copy(a, 1 + j, (*chip, ci), me).wait_recv()
                fwd = copy(a, 4 + j, (*chip, ci), sibling)
                fwd.start()
                passed.append(fwd)
        for a in range(n):
            copy(a, 0, sibling, me).wait_recv()
            for j, chip in enumerate(chips):
                copy(a, 4 + j, (*chip, 1 - ci), me).wait_recv()
        for cp in first + passed:
            cp.wait_send()
        for cp in mine:
            cp.wait()

    return pl.pallas_call(
        body, name=name, out_shape=[jax.ShapeDtypeStruct((N_DEV,) + x.shape, x.dtype) for x in xs],
        in_specs=[ANY] * n, out_specs=[ANY] * n,
        scratch_shapes=[pltpu.SemaphoreType.DMA((n, 7)), pltpu.SemaphoreType.DMA((n, 7)),
                        pltpu.SemaphoreType.DMA((n,))],
    )(*xs)


def _sibling_exchange(gs, name):
    n = len(gs)

    def body(*refs):
        g_refs, recv_refs = refs[:n], refs[n:2 * n]
        send_sems, recv_sems = refs[2 * n:]
        xi, yi, ci = _coords()
        cps = [pltpu.make_async_remote_copy(
            src_ref=g_refs[a].at[2 * chip + (1 - ci)], dst_ref=recv_refs[a].at[chip],
            send_sem=send_sems.at[a, chip], recv_sem=recv_sems.at[a, chip],
            device_id=(xi, yi, 1 - ci), device_id_type=MESH) for a in range(n) for chip in range(N_CHIP)]
        for cp in cps:
            cp.start()
        for cp in cps:
            cp.wait()

    return pl.pallas_call(
        body, name=name, out_shape=[jax.ShapeDtypeStruct((N_CHIP,) + g.shape[1:], g.dtype) for g in gs],
        in_specs=[ANY] * n, out_specs=[ANY] * n,
        scratch_shapes=[pltpu.SemaphoreType.DMA((n, N_CHIP)), pltpu.SemaphoreType.DMA((n, N_CHIP))],
    )(*gs)


def _pair_add(g, recv, ids, name, tr=256):
    _, r, c = g.shape
    tr = _tile(r, tr, 16)

    def body(ids_ref, g_ref, r_ref, p_ref, own_ref):
        kk = pl.program_id(1)
        tot = g_ref[0].astype(F32) + r_ref[0].astype(F32)
        p_ref[0] = tot.astype(BF16)

        @pl.when(kk == ids_ref[1])
        def _():
            own_ref[...] = tot

    grid_spec = pltpu.PrefetchScalarGridSpec(
        num_scalar_prefetch=1, grid=(r // tr, N_CHIP),
        in_specs=[pl.BlockSpec((1, tr, c), lambda i, kk, ids: (2 * kk + ids[0], i, 0)),
                  pl.BlockSpec((1, tr, c), lambda i, kk, ids: (kk, i, 0))],
        out_specs=[pl.BlockSpec((1, tr, c), lambda i, kk, ids: (kk, i, 0)),
                   pl.BlockSpec((tr, c), lambda i, kk, ids: (i, 0))])
    return pl.pallas_call(
        body, name=name, grid_spec=grid_spec,
        out_shape=[jax.ShapeDtypeStruct((N_CHIP, r, c), BF16), jax.ShapeDtypeStruct((r, c), F32)],
        compiler_params=_params("parallel", "arbitrary"),
    )(ids, g, recv)


def _chip_exchange(ps, name):
    n = len(ps)

    def body(*refs):
        p_refs, recv_refs = refs[:n], refs[n:2 * n]
        send_sems, recv_sems = refs[2 * n:]
        xi, yi, ci = _coords()
        mine = 2 * xi + yi
        chips = _other_chips(xi, yi)

        def copy(a, k, cx, cy):
            return pltpu.make_async_remote_copy(
                src_ref=p_refs[a].at[2 * cx + cy], dst_ref=recv_refs[a].at[mine],
                send_sem=send_sems.at[a, k], recv_sem=recv_sems.at[a, k],
                device_id=(cx, cy, ci), device_id_type=MESH)

        def landed(a, k, cx, cy):
            return pltpu.make_async_remote_copy(
                src_ref=p_refs[a].at[mine], dst_ref=recv_refs[a].at[2 * cx + cy],
                send_sem=send_sems.at[a, k], recv_sem=recv_sems.at[a, k],
                device_id=(cx, cy, ci), device_id_type=MESH)

        sends = [copy(a, k, cx, cy) for a in range(n) for k, (cx, cy) in enumerate(chips)]
        for cp in sends:
            cp.start()
        for a in range(n):
            for k, (cx, cy) in enumerate(chips):
                landed(a, k, cx, cy).wait_recv()
        for cp in sends:
            cp.wait_send()

    return pl.pallas_call(
        body, name=name, out_shape=[jax.ShapeDtypeStruct(p.shape, p.dtype) for p in ps],
        in_specs=[ANY] * n, out_specs=[ANY] * n,
        scratch_shapes=[pltpu.SemaphoreType.DMA((n, 3)), pltpu.SemaphoreType.DMA((n, 3))],
    )(*ps)


def _adamw_math(w, g, m, v):
    m = ADAM_B1 * m + (1.0 - ADAM_B1) * g
    v = ADAM_B2 * v + (1.0 - ADAM_B2) * (g * g)
    m_hat = m / (1.0 - ADAM_B1 ** ADAM_STEP)
    v_hat = v / (1.0 - ADAM_B2 ** ADAM_STEP)
    delta = -ADAM_LR * (m_hat / (jnp.sqrt(v_hat) + ADAM_EPS) + ADAM_WD * w)
    return delta, m, v


def _adamw_reduce(own, recv, ids, w, m, v, name, tr=256):
    r, c = w.shape
    tr = _tile(r, tr, 16)

    def body(ids_ref, own_ref, recv_ref, w_ref, m_ref, v_ref, g_ref, d_ref, mo_ref, vo_ref):
        mine = ids_ref[1]
        g = None
        for kk in range(N_CHIP):
            term = jnp.where(mine == kk, own_ref[...], recv_ref[kk].astype(F32))
            g = term if g is None else g + term
        delta, m_new, v_new = _adamw_math(w_ref[...], g, m_ref[...], v_ref[...])
        g_ref[...] = g
        d_ref[...] = delta
        mo_ref[...] = m_new
        vo_ref[...] = v_new

    row = pl.BlockSpec((tr, c), lambda i, ids: (i, 0))
    grid_spec = pltpu.PrefetchScalarGridSpec(
        num_scalar_prefetch=1, grid=(r // tr,),
        in_specs=[row, pl.BlockSpec((N_CHIP, tr, c), lambda i, ids: (0, i, 0)), row, row, row],
        out_specs=[row, row, row, row])
    out = jax.ShapeDtypeStruct((r, c), F32)
    return pl.pallas_call(
        body, name=name, grid_spec=grid_spec, out_shape=[out, out, out, out],
        compiler_params=_params("parallel"),
    )(ids, own, recv, w, m, v)


def _sum_sources(a, name):
    n, r, c = a.shape

    def body(a_ref, o_ref):
        tot = a_ref[0]
        for kk in range(1, n):
            tot = tot + a_ref[kk]
        o_ref[...] = tot

    return pl.pallas_call(
        body, name=name, out_shape=jax.ShapeDtypeStruct((r, c), F32),
        in_specs=[pl.BlockSpec(memory_space=pltpu.VMEM)], out_specs=pl.BlockSpec(memory_space=pltpu.VMEM),
    )(a)


def _adamw_small(w, g, m, v, name):
    def body(w_ref, g_ref, m_ref, v_ref, d_ref, mo_ref, vo_ref):
        delta, m_new, v_new = _adamw_math(w_ref[...], g_ref[...], m_ref[...], v_ref[...])
        d_ref[...] = delta
        mo_ref[...] = m_new
        vo_ref[...] = v_new

    vm = pl.BlockSpec(memory_space=pltpu.VMEM)
    out = jax.ShapeDtypeStruct(w.shape, F32)
    return pl.pallas_call(body, name=name, out_shape=[out, out, out], in_specs=[vm] * 4, out_specs=[vm] * 3)(w, g, m, v)


def _pack(parts, width, row_mult):
    flat = jnp.concatenate([p.reshape(-1) for p in parts])
    rows = -(-flat.shape[0] // width)
    rows = -(-rows // row_mult) * row_mult
    return jnp.pad(flat, (0, rows * width - flat.shape[0])).reshape(rows, width)


def _unpack(flat, shapes):
    out, off = [], 0
    lead = flat.shape[:-1]
    for shp in shapes:
        n = 1
        for dd in shp:
            n *= dd
        out.append(flat[..., off:off + n].reshape(lead + tuple(shp)))
        off += n
    return out


def _rows2d(w):
    return w.reshape(w.shape[0] * w.shape[1], w.shape[2])


def _cols_to_dev(g):
    l, k, n = g.shape
    return g.reshape(l * k, N_DEV, n // N_DEV).transpose(1, 0, 2)


def _rows_to_dev(g):
    l, k, n = g.shape
    rs = k // N_DEV
    return g.reshape(l, N_DEV, rs, n).transpose(1, 0, 2, 3).reshape(N_DEV, l * rs, n)


def _dev_to_cols(a, l):
    _, lk, cs = a.shape
    return a.transpose(1, 0, 2).reshape(l, lk // l, N_DEV * cs)


def _dev_to_rows(a, l):
    _, lr, n = a.shape
    rs = lr // l
    return a.reshape(N_DEV, l, rs, n).transpose(1, 0, 2, 3).reshape(l, N_DEV * rs, n)


def kernel(x, attn_norm, attn_w_in, attn_f_bias, fox_q_gain, fox_k_gain, sb_q_gain, sb_k_gain, attn_w_out, conv_norm, conv_w_in, conv_kernel, conv_w_out, ffn_norm, ffn_w_up, ffn_conv, ffn_w_down, loss_target, m_attn_norm, m_attn_w_in, m_attn_f_bias, m_fox_q_gain, m_fox_k_gain, m_sb_q_gain, m_sb_k_gain, m_attn_w_out, m_conv_norm, m_conv_w_in, m_conv_kernel, m_conv_w_out, m_ffn_norm, m_ffn_w_up, m_ffn_conv, m_ffn_w_down, v_attn_norm, v_attn_w_in, v_attn_f_bias, v_fox_q_gain, v_fox_k_gain, v_sb_q_gain, v_sb_k_gain, v_attn_w_out, v_conv_norm, v_conv_w_in, v_conv_kernel, v_conv_w_out, v_ffn_norm, v_ffn_w_up, v_ffn_conv, v_ffn_w_down):
    s = x.shape[1]
    n_attn, n_conv, depth = attn_w_in.shape[0], conv_w_in.shape[0], ffn_w_up.shape[0]
    xi, yi, ci = _coords()
    me = 4 * xi + 2 * yi + ci
    ids = jnp.stack([ci, 2 * xi + yi]).astype(jnp.int32)

    big = [attn_w_in, attn_w_out, conv_w_in, conv_w_out, ffn_w_up, ffn_w_down]
    big_m = [m_attn_w_in, m_attn_w_out, m_conv_w_in, m_conv_w_out, m_ffn_w_up, m_ffn_w_down]
    big_v = [v_attn_w_in, v_attn_w_out, v_conv_w_in, v_conv_w_out, v_ffn_w_up, v_ffn_w_down]
    big_names = ["attn_w_in", "attn_w_out", "conv_w_in", "conv_w_out", "ffn_w_up", "ffn_w_down"]
    small_sh = [conv_norm, conv_kernel, ffn_conv]
    small_sh_shapes = [w.shape for w in small_sh]
    rep = [attn_norm, attn_f_bias, fox_q_gain, fox_k_gain, sb_q_gain, sb_k_gain, ffn_norm]
    rep_shapes = [w.shape for w in rep]

    small_pack = _pack(small_sh, LANES, 8)
    gathered = _all_gather([_rows2d(w).astype(BF16) for w in big] + [small_pack], "gather_weights")
    a_w_in = _dev_to_cols(gathered[0], n_attn)
    a_w_in = jnp.pad(a_w_in, ((0, 0), (0, 0), (0, ATTN_IN_PAD - ATTN_IN)))
    a_w_out = _dev_to_rows(gathered[1], n_attn)
    c_w_in = _dev_to_cols(gathered[2], n_conv)
    c_w_out = _dev_to_rows(gathered[3], n_conv)
    f_w_up = _dev_to_cols(gathered[4], depth)
    f_w_down = _dev_to_rows(gathered[5], depth)
    cn, ckern, fconv = _unpack(gathered[6].reshape(N_DEV, -1), small_sh_shapes)
    conv_norm_f = cn.transpose(1, 0, 2).reshape(n_conv, D_MODEL)
    conv_kernel_f = ckern.transpose(1, 2, 0, 3).reshape(n_conv, 3, D_MODEL)
    ffn_conv_f = fconv.transpose(1, 2, 0, 3).reshape(depth, 3, 2 * D_FF)

    def pair_gain(fox_g, sb_g):
        f2 = jnp.concatenate([fox_g, fox_g])
        s2 = jnp.concatenate([sb_g, sb_g])
        return jnp.concatenate([jnp.tile(f2[None], (4, 1)), jnp.tile(s2[None], (4, 1))])[:, None, :]

    h = x[0]
    saved = []
    for layer in range(depth):
        i = layer // 2
        tag = "l%d" % layer
        rec = {"h_in": h}
        if layer % 2 == 0:
            xn = _rms_fwd(h, attn_norm[i], tag + "_attn_rms")
            proj = _matmul(xn, a_w_in[i], tag + "_attn_in", tn=640)
            gq, gk = pair_gain(fox_q_gain[i], sb_q_gain[i]), pair_gain(fox_k_gain[i], sb_k_gain[i])
            qh, kh, vb, qt, kt, vt = _qkv_prep(proj, gq, gk, tag + "_qkv_prep")
            logit = proj[:, 3 * MIX:3 * MIX + H_FOX].T.reshape(H_FOX, s // LANES, LANES)
            cum = _fgate_fwd(logit, attn_f_bias[i], tag + "_fgate")
            frow = cum.reshape(H_FOX // 2, 2, s)
            fcol = frow.transpose(0, 2, 1)
            o_fox, lse = _fox_fwd(qt, kh, vt, frow, fcol, tag + "_fox_fwd")
            o_sb, tot = _sb_fwd(qt, kh, vt, tag + "_sb_fwd")
            o = jnp.concatenate([o_fox, o_sb], axis=1)
            h = _matmul(o, a_w_out[i], tag + "_attn_out", add=h)
            rec.update(xn=xn, proj=proj, gq=gq, gk=gk, qh=qh, kh=kh, vb=vb, qt=qt, kt=kt, logit=logit, frow=frow,
                       fcol=fcol, o_fox=o_fox, lse=lse, tot=tot, o=o)
        else:
            xn = _rms_fwd(h, conv_norm_f[i], tag + "_conv_rms")
            proj = _matmul(xn, c_w_in[i], tag + "_conv_in")
            y = _sconv_fwd(proj, conv_kernel_f[i], tag + "_sconv_fwd")
            h = _matmul(y, c_w_out[i], tag + "_conv_out", add=h)
            rec.update(xn=xn, proj=proj, y=y)
        rec["h_mid"] = h
        xn2 = _rms_fwd(h, ffn_norm[layer], tag + "_ffn_rms")
        up = _matmul(xn2, f_w_up[layer], tag + "_ffn_up")
        act = _ffn_act_fwd(up, ffn_conv_f[layer], tag + "_ffn_act")
        h = _matmul(act, f_w_down[layer], tag + "_ffn_down", add=h, tk=1408)
        rec.update(xn2=xn2, up=up, act=act)
        saved.append(rec)

    loss_local, dh = _loss_head(h, loss_target[0], "loss_head")
    loss = lax.psum(loss_local, ("x", "y", "c"))

    g_attn_norm, g_attn_w_in, g_f_bias = [None] * n_attn, [None] * n_attn, [None] * n_attn
    g_fq, g_fk, g_sq, g_sk, g_attn_w_out = ([None] * n_attn for _ in range(5))
    g_conv_norm, g_conv_w_in, g_conv_kernel, g_conv_w_out = ([None] * n_conv for _ in range(4))
    g_ffn_norm, g_ffn_w_up, g_ffn_conv, g_ffn_w_down = ([None] * depth for _ in range(4))

    for layer in reversed(range(depth)):
        i = layer // 2
        tag = "l%d" % layer
        rec = saved[layer]
        da = _matmul(dh, f_w_down[layer].T, tag + "_ffn_down_dx", tn=1408)
        g_ffn_w_down[layer] = _matmul_tn(rec["act"], dh, tag + "_ffn_down_dw", tm=1408, tn=1024)
        dug, duv, dwg, dwv = _ffn_act_bwd(rec["up"], ffn_conv_f[layer], da, tag + "_ffn_act_bwd")
        g_ffn_conv[layer] = jnp.concatenate([dwg, dwv], axis=1)
        dup = jnp.concatenate([dug, duv], axis=1)
        g_ffn_w_up[layer] = _matmul_tn(rec["xn2"], dup, tag + "_ffn_up_dw", tn=1408)
        dxn = _matmul(dup, f_w_up[layer].T, tag + "_ffn_up_dx", tn=1024)
        dh, g_ffn_norm[layer] = _rms_bwd(rec["h_mid"], dxn, ffn_norm[layer], dh, tag + "_ffn_rms_bwd")
        if layer % 2 == 0:
            do = _matmul(dh, a_w_out[i].T, tag + "_attn_out_dx", tn=1024)
            g_attn_w_out[i] = _matmul_tn(rec["o"], dh, tag + "_attn_out_dw", tn=1024)
            dq_f, dk_f, dv_f, dfk, dfq = _fox_bwd(rec["qh"], rec["qt"], rec["kh"], rec["kt"], rec["vb"], rec["frow"],
                                                  rec["fcol"], rec["lse"], rec["o_fox"], do, tag + "_fox_bwd")
            dq_s, dk_s, dv_s = _sb_bwd(rec["qh"], rec["qt"], rec["kh"], rec["kt"], rec["vb"], rec["tot"], do,
                                       tag + "_sb_bwd")
            dq, dk, dv, dgq, dgk = _qkv_prep_bwd(rec["proj"], rec["gq"], rec["gk"], (dq_f, dq_s), (dk_f, dk_s),
                                                 (dv_f, dv_s), tag + "_qkv_prep_bwd")
            dcum = (dfq + dfk[:, :, 0:2].transpose(0, 2, 1)).reshape(H_FOX, s // LANES, LANES)
            dlogit, dbias = _fgate_bwd(rec["logit"], attn_f_bias[i], dcum, tag + "_fgate_bwd")
            g_f_bias[i] = dbias[:, 0]
            dgate = jnp.pad(dlogit.reshape(H_FOX, s).T, ((0, 0), (0, LANES - H_FOX))).astype(BF16)
            dproj = jnp.concatenate([dq, dk, dv, dgate], axis=1)

            def fold(dg):
                per_head = dg.reshape(16, HEAD_DIM)
                return jnp.sum(per_head[:8], axis=0), jnp.sum(per_head[8:], axis=0)

            g_fq[i], g_sq[i] = fold(dgq)
            g_fk[i], g_sk[i] = fold(dgk)
            g_attn_w_in[i] = _matmul_tn(rec["xn"], dproj, tag + "_attn_in_dw", tn=640)[:, :ATTN_IN]
            dxn = _matmul(dproj, a_w_in[i].T, tag + "_attn_in_dx", tn=1024, tk=640)
            dh, g_attn_norm[i] = _rms_bwd(rec["h_in"], dxn, attn_norm[i], dh, tag + "_attn_rms_bwd")
        else:
            dy = _matmul(dh, c_w_out[i].T, tag + "_conv_out_dx", tn=1024)
            g_conv_w_out[i] = _matmul_tn(rec["y"], dh, tag + "_conv_out_dw", tn=1024)
            db, dc, du, g_conv_kernel[i] = _sconv_bwd(rec["proj"], conv_kernel_f[i], dy, tag + "_sconv_bwd")
            dproj = jnp.concatenate([db, dc, du], axis=1)
            g_conv_w_in[i] = _matmul_tn(rec["xn"], dproj, tag + "_conv_in_dw", tn=1024)
            dxn = _matmul(dproj, c_w_in[i].T, tag + "_conv_in_dx", tn=1024)
            dh, g_conv_norm[i] = _rms_bwd(rec["h_in"], dxn, conv_norm_f[i], dh, tag + "_conv_rms_bwd")
    grad_x = dh[None]

    gb = [_cols_to_dev(jnp.stack(g_attn_w_in).astype(BF16)), _rows_to_dev(jnp.stack(g_attn_w_out).astype(BF16)),
          _cols_to_dev(jnp.stack(g_conv_w_in).astype(BF16)), _rows_to_dev(jnp.stack(g_conv_w_out).astype(BF16)),
          _cols_to_dev(jnp.stack(g_ffn_w_up).astype(BF16)), _rows_to_dev(jnp.stack(g_ffn_w_down).astype(BF16))]
    from_sibling = _sibling_exchange(gb, "reduce_sibling")
    pairs = [_pair_add(g, r, ids, "reduce_pair_add_" + nm) for g, r, nm in zip(gb, from_sibling, big_names)]
    from_chips = _chip_exchange([p[0] for p in pairs], "reduce_chips")
    grads_big, delta_big, newm_big, newv_big = [], [], [], []
    for a, nm in enumerate(big_names):
        g_a, d_a, m_a, v_a = _adamw_reduce(pairs[a][1], from_chips[a], ids, _rows2d(big[a]), _rows2d(big_m[a]),
                                           _rows2d(big_v[a]), "adamw_" + nm)
        shp = big[a].shape
        grads_big.append(g_a.reshape(shp))
        delta_big.append(d_a.reshape(shp))
        newm_big.append(m_a.reshape(shp))
        newv_big.append(v_a.reshape(shp))

    rep_g = [jnp.stack(g_attn_norm), jnp.stack(g_f_bias), jnp.stack(g_fq), jnp.stack(g_fk), jnp.stack(g_sq),
             jnp.stack(g_sk), jnp.stack(g_ffn_norm)]
    sh_g = [jnp.stack(g_conv_norm).reshape(n_conv, N_DEV, -1).transpose(1, 0, 2),
            jnp.stack(g_conv_kernel).reshape(n_conv, 3, N_DEV, -1).transpose(2, 0, 1, 3),
            jnp.stack(g_ffn_conv).reshape(depth, 3, N_DEV, -1).transpose(2, 0, 1, 3)]
    n_rep = sum(int(a.size) for a in rep)
    n_sh = sum(int(a.size) for a in small_sh)
    partial = _pack(rep_g + [jnp.concatenate([a.reshape(N_DEV, -1) for a in sh_g], axis=1)], LANES, 8)
    total = _sum_sources(_all_gather([partial], "gather_small_grads")[0], "sum_small_grads").reshape(-1)
    rep_tot = total[:n_rep]
    sh_tot = lax.dynamic_slice_in_dim(total[n_rep:n_rep + N_DEV * n_sh].reshape(N_DEV, n_sh), me, 1, axis=0)[0]
    g_small = _pack([rep_tot, sh_tot], LANES, 8)

    def small_pack_of(rep_list, sh_list):
        return _pack(rep_list + sh_list, LANES, 8)

    d_small, m_small, v_small = _adamw_small(
        small_pack_of(rep, small_sh), g_small,
        small_pack_of([m_attn_norm, m_attn_f_bias, m_fox_q_gain, m_fox_k_gain, m_sb_q_gain, m_sb_k_gain, m_ffn_norm],
                      [m_conv_norm, m_conv_kernel, m_ffn_conv]),
        small_pack_of([v_attn_norm, v_attn_f_bias, v_fox_q_gain, v_fox_k_gain, v_sb_q_gain, v_sb_k_gain, v_ffn_norm],
                      [v_conv_norm, v_conv_kernel, v_ffn_conv]),
        "adamw_small")
    small_shapes = rep_shapes + small_sh_shapes

    def split_small(a):
        return _unpack(a.reshape(-1), small_shapes)

    def ordered(big_list, small_list):
        an, fb, fq, fk, sq, sk, fn, cno, cke, fco = small_list
        awi, awo, cwi, cwo, fwu, fwd = big_list
        return [an, awi, fb, fq, fk, sq, sk, awo, cno, cwi, cke, cwo, fn, fwu, fco, fwd]

    grads = ordered(grads_big, split_small(g_small))
    deltas = ordered(delta_big, split_small(d_small))
    new_m = ordered(newm_big, split_small(m_small))
    new_v = ordered(newv_big, split_small(v_small))
    return (loss, grad_x, *grads, *deltas, *new_m, *new_v)
```

```python
import jax
import jax.numpy as jnp
from jax import lax
from jax.experimental import pallas as pl
from jax.experimental.pallas import tpu as pltpu

F32 = jnp.float32
BF16 = jnp.bfloat16

D_MODEL = 1024
HEAD_DIM = 64
H_FOX = 8
MIX = 1024
ATTN_IN = 3 * MIX + H_FOX
ATTN_IN_PAD = 3 * MIX + 128
D_FF = 2816
EPS = 1e-6
NEG = -1e30
LANES = 128
N_DEV = 8
N_CHIP = 4

ADAM_LR = 0.001
ADAM_B1 = 0.9
ADAM_B2 = 0.999
ADAM_EPS = 1e-08
ADAM_WD = 0.01
ADAM_STEP = 10

VMEM_LIMIT = 56 * 1024 * 1024
MESH = pl.DeviceIdType.MESH
ANY = pl.BlockSpec(memory_space=pl.ANY)


def _params(*sem):
    return pltpu.CompilerParams(dimension_semantics=sem, vmem_limit_bytes=VMEM_LIMIT)


def _tile(n, target, mult=LANES):
    best = None
    for t in range(mult, min(n, target) + 1, mult):
        if n % t == 0:
            best = t
    return best if best is not None else n


def _dot(a, b):
    return jnp.dot(a, b, preferred_element_type=F32)


def _dot_tn(a, b):
    return lax.dot_general(a, b, (((0,), (0,)), ((), ())), preferred_element_type=F32)


def _split_dot(x, m, passes):
    acc = None
    rem = x
    for _ in range(passes):
        part = rem.astype(BF16)
        term = _dot(part, m)
        acc = term if acc is None else acc + term
        rem = rem - part.astype(F32)
    return acc


def _split_dot_left(m, x, passes):
    acc = None
    rem = x
    for _ in range(passes):
        part = rem.astype(BF16)
        term = _dot(m, part)
        acc = term if acc is None else acc + term
        rem = rem - part.astype(F32)
    return acc


def _matmul(a, b, name, add=None, out_dtype=F32, tm=1024, tn=512, tk=1024):
    m, k = a.shape
    n = b.shape[1]
    tm, tn, tk = _tile(m, tm, 8), _tile(n, tn), _tile(k, tk)
    nk = k // tk
    has_add = add is not None

    def body(*refs):
        a_ref, b_ref = refs[0], refs[1]
        add_ref = refs[2] if has_add else None
        o_ref = refs[2 + has_add]

        def finish(acc):
            if has_add:
                acc = acc + add_ref[...]
            o_ref[...] = acc.astype(out_dtype)

        p = _dot(a_ref[...].astype(BF16), b_ref[...].astype(BF16))
        if nk == 1:
            finish(p)
        else:
            acc_ref = refs[-1]
            kk = pl.program_id(2)

            @pl.when(kk == 0)
            def _():
                acc_ref[...] = p

            @pl.when(kk > 0)
            def _():
                acc_ref[...] += p

            @pl.when(kk == nk - 1)
            def _():
                finish(acc_ref[...])

    in_specs = [pl.BlockSpec((tm, tk), lambda i, j, kk: (i, kk)),
                pl.BlockSpec((tk, tn), lambda i, j, kk: (kk, j))]
    args = [a, b]
    if has_add:
        in_specs.append(pl.BlockSpec((tm, tn), lambda i, j, kk: (i, j)))
        args.append(add)
    return pl.pallas_call(
        body, name=name, grid=(m // tm, n // tn, nk), in_specs=in_specs,
        out_specs=pl.BlockSpec((tm, tn), lambda i, j, kk: (i, j)),
        out_shape=jax.ShapeDtypeStruct((m, n), out_dtype),
        scratch_shapes=[pltpu.VMEM((tm, tn), F32)] if nk > 1 else [],
        compiler_params=_params("parallel", "parallel", "arbitrary"),
    )(*args)


def _matmul_tn(a, b, name, tm=1024, tn=512, ts=1024):
    s, m = a.shape
    n = b.shape[1]
    tm, tn, ts = _tile(m, tm), _tile(n, tn), _tile(s, ts, 8)

    def body(a_ref, b_ref, o_ref):
        kk = pl.program_id(2)
        p = _dot_tn(a_ref[...].astype(BF16), b_ref[...].astype(BF16))

        @pl.when(kk == 0)
        def _():
            o_ref[...] = p

        @pl.when(kk > 0)
        def _():
            o_ref[...] += p

    return pl.pallas_call(
        body, name=name, grid=(m // tm, n // tn, s // ts),
        in_specs=[pl.BlockSpec((ts, tm), lambda i, j, kk: (kk, i)),
                  pl.BlockSpec((ts, tn), lambda i, j, kk: (kk, j))],
        out_specs=pl.BlockSpec((tm, tn), lambda i, j, kk: (i, j)),
        out_shape=jax.ShapeDtypeStruct((m, n), F32),
        compiler_params=_params("parallel", "parallel", "arbitrary"),
    )(a, b)


def _rms_fwd(h, g, name, ts=512):
    s, d = h.shape
    ts = _tile(s, ts, 8)

    def body(h_ref, g_ref, o_ref):
        x = h_ref[...]
        r = lax.rsqrt(jnp.mean(x * x, axis=-1, keepdims=True) + EPS)
        o_ref[...] = (x * r * g_ref[...]).astype(BF16)

    return pl.pallas_call(
        body, name=name, grid=(s // ts,),
        in_specs=[pl.BlockSpec((ts, d), lambda i: (i, 0)), pl.BlockSpec((1, d), lambda i: (0, 0))],
        out_specs=pl.BlockSpec((ts, d), lambda i: (i, 0)),
        out_shape=jax.ShapeDtypeStruct((s, d), BF16),
        compiler_params=_params("parallel"),
    )(h, g.reshape(1, d))


def _rms_bwd(h, dxn, g, dh_in, name, ts=512):
    s, d = h.shape
    ts = _tile(s, ts, 8)

    def body(h_ref, dxn_ref, g_ref, dhin_ref, dh_ref, dg_ref):
        i = pl.program_id(0)
        x = h_ref[...]
        r = lax.rsqrt(jnp.mean(x * x, axis=-1, keepdims=True) + EPS)
        xh = x * r
        dxn_v = dxn_ref[...]

        @pl.when(i == 0)
        def _():
            dg_ref[...] = jnp.zeros_like(dg_ref)

        dg_ref[0:1, :] += jnp.sum(dxn_v * xh, axis=0, keepdims=True)
        dxh = dxn_v * g_ref[...]
        dx = r * (dxh - xh * jnp.mean(dxh * xh, axis=-1, keepdims=True))
        dh_ref[...] = dhin_ref[...] + dx

    row = pl.BlockSpec((ts, d), lambda i: (i, 0))
    dh, dg = pl.pallas_call(
        body, name=name, grid=(s // ts,),
        in_specs=[row, row, pl.BlockSpec((1, d), lambda i: (0, 0)), row],
        out_specs=[row, pl.BlockSpec((8, d), lambda i: (0, 0))],
        out_shape=[jax.ShapeDtypeStruct((s, d), F32), jax.ShapeDtypeStruct((8, d), F32)],
        compiler_params=_params("arbitrary"),
    )(h, dxn, g.reshape(1, d), dh_in)
    return dh, dg[0]


def _shift_down(x, prev):
    rows = lax.broadcasted_iota(jnp.int32, x.shape, 0)
    p1, p2 = prev[7:8, :], prev[6:7, :]
    x1 = jnp.where(rows == 0, p1, pltpu.roll(x, 1, 0))
    x2 = jnp.where(rows == 0, p2, jnp.where(rows == 1, p1, pltpu.roll(x, 2, 0)))
    return x1, x2


def _shift_up(x, nxt):
    n = x.shape[0]
    rows = lax.broadcasted_iota(jnp.int32, x.shape, 0)
    n0, n1 = nxt[0:1, :], nxt[1:2, :]
    x1 = jnp.where(rows == n - 1, n0, pltpu.roll(x, n - 1, 0))
    x2 = jnp.where(rows == n - 1, n1, jnp.where(rows == n - 2, n0, pltpu.roll(x, n - 2, 0)))
    return x1, x2


def _conv(x, x1, x2, w):
    return w[2:3, :] * x + w[1:2, :] * x1 + w[0:1, :] * x2


def _halo_specs(ts, tc, col, n_time_blocks):
    r8 = ts // 8
    main = pl.BlockSpec((ts, tc), lambda j, i: (i, j + col))
    prev = pl.BlockSpec((8, tc), lambda j, i: (jnp.maximum(i * r8 - 1, 0), j + col))
    nxt = pl.BlockSpec((8, tc), lambda j, i: (jnp.minimum((i + 1) * r8, n_time_blocks * r8 - 1), j + col))
    return main, prev, nxt


def _silu_parts(g):
    sig = 1.0 / (1.0 + jnp.exp(-g))
    return sig, g * sig


def _ffn_act_fwd(up, cw, name, ts=256, tc=1408):
    s = up.shape[0]
    ts, tc = _tile(s, ts, 8), _tile(D_FF, tc)
    nc, nt = D_FF // tc, s // ts

    def body(g_ref, gp_ref, v_ref, vp_ref, wg_ref, wv_ref, o_ref):
        first = pl.program_id(1) == 0

        def conv(x_ref, p_ref, w_ref):
            x = x_ref[...]
            prev = jnp.where(first, 0.0, p_ref[...])
            x1, x2 = _shift_down(x, prev)
            return _conv(x, x1, x2, w_ref[...])

        ug = conv(g_ref, gp_ref, wg_ref)
        uv = conv(v_ref, vp_ref, wv_ref)
        _, silu = _silu_parts(ug)
        o_ref[...] = (silu * uv).astype(BF16)

    g_main, g_prev, _ = _halo_specs(ts, tc, 0, nt)
    v_main, v_prev, _ = _halo_specs(ts, tc, nc, nt)
    return pl.pallas_call(
        body, name=name, grid=(nc, nt),
        in_specs=[g_main, g_prev, v_main, v_prev,
                  pl.BlockSpec((3, tc), lambda j, i: (0, j)), pl.BlockSpec((3, tc), lambda j, i: (0, j + nc))],
        out_specs=pl.BlockSpec((ts, tc), lambda j, i: (i, j)),
        out_shape=jax.ShapeDtypeStruct((s, D_FF), BF16),
        compiler_params=_params("parallel", "parallel"),
    )(up, up, up, up, cw, cw)


def _ffn_act_bwd(up, cw, da, name, ts=256, tc=1408):
    s = up.shape[0]
    ts, tc = _tile(s, ts, 8), _tile(D_FF, tc)
    nc, nt = D_FF // tc, s // ts

    def body(g_ref, gp_ref, gn_ref, v_ref, vp_ref, vn_ref, da_ref, dan_ref, wg_ref, wv_ref,
             dg_ref, dv_ref, dwg_ref, dwv_ref):
        i = pl.program_id(1)
        first, last = i == 0, i == nt - 1
        wg, wv = wg_ref[...], wv_ref[...]
        g, v = g_ref[...], v_ref[...]
        g1, g2 = _shift_down(g, jnp.where(first, 0.0, gp_ref[...]))
        v1, v2 = _shift_down(v, jnp.where(first, 0.0, vp_ref[...]))

        def d_u(ug, uv, da_v):
            sig, silu = _silu_parts(ug)
            return da_v * uv * (sig * (1.0 + ug * (1.0 - sig))), da_v * silu

        dug, duv = d_u(_conv(g, g1, g2, wg), _conv(v, v1, v2, wv), da_ref[...])
        gn, vn = gn_ref[...], vn_ref[...]
        gn1, gn2 = _shift_down(gn, g[ts - 8:, :])
        vn1, vn2 = _shift_down(vn, v[ts - 8:, :])
        dugn, duvn = d_u(_conv(gn, gn1, gn2, wg), _conv(vn, vn1, vn2, wv), dan_ref[...])
        dugn = jnp.where(last, 0.0, dugn)
        duvn = jnp.where(last, 0.0, duvn)

        def finish(du, dun, x, x1, x2, w, dx_ref, dw_ref):
            d1, d2 = _shift_up(du, dun)
            dx_ref[...] = (w[2:3, :] * du + w[1:2, :] * d1 + w[0:1, :] * d2).astype(BF16)

            @pl.when(first)
            def _():
                dw_ref[...] = jnp.zeros_like(dw_ref)

            dw_ref[0:1, :] += jnp.sum(du * x2, axis=0, keepdims=True)
            dw_ref[1:2, :] += jnp.sum(du * x1, axis=0, keepdims=True)
            dw_ref[2:3, :] += jnp.sum(du * x, axis=0, keepdims=True)

        finish(dug, dugn, g, g1, g2, wg, dg_ref, dwg_ref)
        finish(duv, duvn, v, v1, v2, wv, dv_ref, dwv_ref)

    g_specs = _halo_specs(ts, tc, 0, nt)
    v_specs = _halo_specs(ts, tc, nc, nt)
    da_main, _, da_next = _halo_specs(ts, tc, 0, nt)
    tile = pl.BlockSpec((ts, tc), lambda j, i: (i, j))
    taps = pl.BlockSpec((8, tc), lambda j, i: (0, j))
    dg, dv, dwg, dwv = pl.pallas_call(
        body, name=name, grid=(nc, nt),
        in_specs=[*g_specs, *v_specs, da_main, da_next,
                  pl.BlockSpec((3, tc), lambda j, i: (0, j)), pl.BlockSpec((3, tc), lambda j, i: (0, j + nc))],
        out_specs=[tile, tile, taps, taps],
        out_shape=[jax.ShapeDtypeStruct((s, D_FF), BF16), jax.ShapeDtypeStruct((s, D_FF), BF16),
                   jax.ShapeDtypeStruct((8, D_FF), F32), jax.ShapeDtypeStruct((8, D_FF), F32)],
        compiler_params=_params("parallel", "arbitrary"),
    )(up, up, up, up, up, up, da, da, cw, cw)
    return dg, dv, dwg[:3], dwv[:3]


def _sconv_fwd(proj, ck, name, ts=256, tc=512):
    s = proj.shape[0]
    w = D_MODEL
    ts, tc = _tile(s, ts, 8), _tile(w, tc)
    nc, nt = w // tc, s // ts

    def body(b_ref, c_ref, cp_ref, u_ref, up_ref, w_ref, o_ref):
        first = pl.program_id(1) == 0
        cu = c_ref[...] * u_ref[...]
        cup = jnp.where(first, 0.0, cp_ref[...] * up_ref[...])
        x1, x2 = _shift_down(cu, cup)
        o_ref[...] = (b_ref[...] * _conv(cu, x1, x2, w_ref[...])).astype(BF16)

    b_main, _, _ = _halo_specs(ts, tc, 0, nt)
    c_main, c_prev, _ = _halo_specs(ts, tc, nc, nt)
    u_main, u_prev, _ = _halo_specs(ts, tc, 2 * nc, nt)
    return pl.pallas_call(
        body, name=name, grid=(nc, nt),
        in_specs=[b_main, c_main, c_prev, u_main, u_prev, pl.BlockSpec((3, tc), lambda j, i: (0, j))],
        out_specs=pl.BlockSpec((ts, tc), lambda j, i: (i, j)),
        out_shape=jax.ShapeDtypeStruct((s, w), BF16),
        compiler_params=_params("parallel", "parallel"),
    )(proj, proj, proj, proj, proj, ck)


def _sconv_bwd(proj, ck, dy, name, ts=256, tc=512):
    s = proj.shape[0]
    w = D_MODEL
    ts, tc = _tile(s, ts, 8), _tile(w, tc)
    nc, nt = w // tc, s // ts

    def body(b_ref, bn_ref, c_ref, cp_ref, u_ref, up_ref, dy_ref, dyn_ref, w_ref,
             db_ref, dc_ref, du_ref, dw_ref):
        i = pl.program_id(1)
        first, last = i == 0, i == nt - 1
        wv = w_ref[...]
        b, c, u, dy_v = b_ref[...], c_ref[...], u_ref[...], dy_ref[...]
        cu = c * u
        cup = jnp.where(first, 0.0, cp_ref[...] * up_ref[...])
        x1, x2 = _shift_down(cu, cup)
        db_ref[...] = (dy_v * _conv(cu, x1, x2, wv)).astype(BF16)
        dcv = dy_v * b
        dcvn = jnp.where(last, 0.0, dyn_ref[...] * bn_ref[...])
        d1, d2 = _shift_up(dcv, dcvn)
        dcu = wv[2:3, :] * dcv + wv[1:2, :] * d1 + wv[0:1, :] * d2
        dc_ref[...] = (dcu * u).astype(BF16)
        du_ref[...] = (dcu * c).astype(BF16)

        @pl.when(first)
        def _():
            dw_ref[...] = jnp.zeros_like(dw_ref)

        dw_ref[0:1, :] += jnp.sum(dcv * x2, axis=0, keepdims=True)
        dw_ref[1:2, :] += jnp.sum(dcv * x1, axis=0, keepdims=True)
        dw_ref[2:3, :] += jnp.sum(dcv * cu, axis=0, keepdims=True)

    b_main, _, b_next = _halo_specs(ts, tc, 0, nt)
    c_main, c_prev, _ = _halo_specs(ts, tc, nc, nt)
    u_main, u_prev, _ = _halo_specs(ts, tc, 2 * nc, nt)
    dy_main, _, dy_next = _halo_specs(ts, tc, 0, nt)
    tile = pl.BlockSpec((ts, tc), lambda j, i: (i, j))
    out = jax.ShapeDtypeStruct((s, w), BF16)
    db, dc, du, dw = pl.pallas_call(
        body, name=name, grid=(nc, nt),
        in_specs=[b_main, b_next, c_main, c_prev, u_main, u_prev, dy_main, dy_next,
                  pl.BlockSpec((3, tc), lambda j, i: (0, j))],
        out_specs=[tile, tile, tile, pl.BlockSpec((8, tc), lambda j, i: (0, j))],
        out_shape=[out, out, out, jax.ShapeDtypeStruct((8, w), F32)],
        compiler_params=_params("parallel", "arbitrary"),
    )(proj, proj, proj, proj, proj, proj, dy, dy, ck)
    return db, dc, du, dw[:3]


def _low_lanes(shape):
    return lax.broadcasted_iota(jnp.int32, shape, 1) < HEAD_DIM


def _top_rows(shape):
    return lax.broadcasted_iota(jnp.int32, shape, 0) < HEAD_DIM


def _norm_pair(x):
    lo = _low_lanes(x.shape)
    sq = x * x
    s_lo = jnp.sum(jnp.where(lo, sq, 0.0), axis=1, keepdims=True)
    s_hi = jnp.sum(jnp.where(lo, 0.0, sq), axis=1, keepdims=True)
    r = lax.rsqrt(jnp.where(lo, s_lo, s_hi) * (1.0 / HEAD_DIM) + EPS)
    return x * r, r


def _mean_pair(x):
    lo = _low_lanes(x.shape)
    s_lo = jnp.sum(jnp.where(lo, x, 0.0), axis=1, keepdims=True)
    s_hi = jnp.sum(jnp.where(lo, 0.0, x), axis=1, keepdims=True)
    return jnp.where(lo, s_lo, s_hi) * (1.0 / HEAD_DIM)


def _qkv_prep(proj, gq, gk, name, ts=512):
    s = proj.shape[0]
    ts = _tile(s, ts)
    npair = MIX // LANES
    scale = HEAD_DIM ** -0.5

    def body(q_ref, k_ref, v_ref, gq_ref, gk_ref, qo_ref, ko_ref, vo_ref, qt_ref, kt_ref, vt_ref):
        qn, _ = _norm_pair(q_ref[...])
        kn, _ = _norm_pair(k_ref[...])
        q = qn * gq_ref[0] * scale
        k = kn * gk_ref[0]
        v = v_ref[...]
        qo_ref[...] = q.astype(BF16)
        ko_ref[...] = k.astype(BF16)
        vo_ref[...] = v.astype(BF16)
        qt_ref[...] = q.T.astype(BF16)
        kt_ref[...] = k.T.astype(BF16)
        vt_ref[...] = v.T.astype(BF16)

    gain = pl.BlockSpec((1, 1, LANES), lambda i, p: (p, 0, 0))
    tile = pl.BlockSpec((ts, LANES), lambda i, p: (i, p))
    tile_t = pl.BlockSpec((LANES, ts), lambda i, p: (p, i))
    out = jax.ShapeDtypeStruct((s, MIX), BF16)
    out_t = jax.ShapeDtypeStruct((MIX, s), BF16)
    return pl.pallas_call(
        body, name=name, grid=(s // ts, npair),
        in_specs=[tile, pl.BlockSpec((ts, LANES), lambda i, p: (i, p + npair)),
                  pl.BlockSpec((ts, LANES), lambda i, p: (i, p + 2 * npair)), gain, gain],
        out_specs=[tile, tile, tile, tile_t, tile_t, tile_t], out_shape=[out, out, out, out_t, out_t, out_t],
        compiler_params=_params("parallel", "parallel"),
    )(proj, proj, proj, gq, gk)


def _qkv_prep_bwd(proj, gq, gk, dqs, dks, dvs, name, ts=512):
    s = proj.shape[0]
    ts = _tile(s, ts, 8)
    npair = MIX // LANES
    half = npair // 2
    scale = HEAD_DIM ** -0.5

    def body(q_ref, k_ref, gq_ref, gk_ref, dqf_ref, dqs_ref, dkf_ref, dks_ref, dvf_ref, dvs_ref,
             dq_ref, dk_ref, dv_ref, dgq_ref, dgk_ref):
        p, i = pl.program_id(0), pl.program_id(1)
        fox = p < half

        def one(x_ref, g_ref, df_ref, ds_ref, dx_ref, dg_ref, mult):
            dn = jnp.where(fox, df_ref[...], ds_ref[...]) * mult
            xh, r = _norm_pair(x_ref[...])

            @pl.when(i == 0)
            def _():
                dg_ref[...] = jnp.zeros_like(dg_ref)

            dg_ref[0, 0:1, :] += jnp.sum(dn * xh, axis=0, keepdims=True)
            dxh = dn * g_ref[0]
            dx_ref[...] = (r * (dxh - xh * _mean_pair(dxh * xh))).astype(BF16)

        one(q_ref, gq_ref, dqf_ref, dqs_ref, dq_ref, dgq_ref, scale)
        one(k_ref, gk_ref, dkf_ref, dks_ref, dk_ref, dgk_ref, 1.0)
        dv_ref[...] = jnp.where(fox, dvf_ref[...], dvs_ref[...]).astype(BF16)

    gain = pl.BlockSpec((1, 1, LANES), lambda p, i: (p, 0, 0))
    tile = pl.BlockSpec((ts, LANES), lambda p, i: (i, p))
    fpart = pl.BlockSpec((ts, LANES), lambda p, i: (i, jnp.minimum(p, half - 1)))
    spart = pl.BlockSpec((ts, LANES), lambda p, i: (i, jnp.maximum(p - half, 0)))
    dgain = pl.BlockSpec((1, 8, LANES), lambda p, i: (p, 0, 0))
    out = jax.ShapeDtypeStruct((s, MIX), BF16)
    gshape = jax.ShapeDtypeStruct((npair, 8, LANES), F32)
    dq, dk, dv, dgq, dgk = pl.pallas_call(
        body, name=name, grid=(npair, s // ts),
        in_specs=[tile, pl.BlockSpec((ts, LANES), lambda p, i: (i, p + npair)), gain, gain,
                  fpart, spart, fpart, spart, fpart, spart],
        out_specs=[tile, tile, tile, dgain, dgain], out_shape=[out, out, out, gshape, gshape],
        compiler_params=_params("parallel", "arbitrary"),
    )(proj, proj, gq, gk, dqs[0], dqs[1], dks[0], dks[1], dvs[0], dvs[1])
    return dq, dk, dv, dgq[:, 0, :], dgk[:, 0, :]


def _tri(n, rel):
    a = lax.broadcasted_iota(jnp.int32, (n, n), 0)
    b = lax.broadcasted_iota(jnp.int32, (n, n), 1)
    return rel(a, b).astype(BF16)


def _fgate_fwd(logit, bias, name):
    nh, r, _ = logit.shape

    def body(x_ref, b_ref, o_ref):
        within = _tri(LANES, lambda a, b: a <= b)
        before = _tri(r, lambda a, b: b < a)
        for hh in range(nh):
            x = x_ref[hh] + b_ref[hh]
            lf = jnp.minimum(x, 0.0) - jnp.log1p(jnp.exp(-jnp.abs(x)))
            c = _split_dot(lf, within, 3)
            tot = jnp.broadcast_to(c[:, LANES - 1:LANES], (r, LANES))
            o_ref[hh] = c + _split_dot_left(before, tot, 3)

    return pl.pallas_call(
        body, name=name, out_shape=jax.ShapeDtypeStruct((nh, r, LANES), F32),
        in_specs=[pl.BlockSpec(memory_space=pltpu.VMEM), pl.BlockSpec(memory_space=pltpu.SMEM)],
        out_specs=pl.BlockSpec(memory_space=pltpu.VMEM),
    )(logit, bias)


def _fgate_bwd(logit, bias, dcum, name):
    nh, r, _ = logit.shape

    def body(x_ref, b_ref, d_ref, dx_ref, db_ref):
        within = _tri(LANES, lambda a, b: a >= b)
        after = _tri(r, lambda a, b: b > a)
        for hh in range(nh):
            x = x_ref[hh] + b_ref[hh]
            d = d_ref[hh]
            c = _split_dot(d, within, 3)
            tot = jnp.broadcast_to(c[:, 0:1], (r, LANES))
            dlf = c + _split_dot_left(after, tot, 3)
            dx = dlf * (1.0 / (1.0 + jnp.exp(x)))
            dx_ref[hh] = dx
            db_ref[hh:hh + 1, :] = jnp.broadcast_to(jnp.sum(dx, keepdims=True).reshape(1, 1), (1, LANES))

    return pl.pallas_call(
        body, name=name,
        out_shape=[jax.ShapeDtypeStruct((nh, r, LANES), F32), jax.ShapeDtypeStruct((nh, LANES), F32)],
        in_specs=[pl.BlockSpec(memory_space=pltpu.VMEM), pl.BlockSpec(memory_space=pltpu.SMEM),
                  pl.BlockSpec(memory_space=pltpu.VMEM)],
        out_specs=[pl.BlockSpec(memory_space=pltpu.VMEM), pl.BlockSpec(memory_space=pltpu.VMEM)],
    )(logit, bias, dcum)


def _pair_masks(x):
    lo = _low_lanes(x.shape)
    zero = jnp.zeros_like(x)
    return jnp.where(lo, x, zero), jnp.where(lo, zero, x)


def _pair_masks_t(x):
    top = _top_rows(x.shape)
    zero = jnp.zeros_like(x)
    return jnp.where(top, x, zero), jnp.where(top, zero, x)


def _stack_heads(x):
    return jnp.concatenate(_pair_masks(x), axis=0)


def _stack_heads_t(x):
    return jnp.concatenate(_pair_masks_t(x), axis=1)


def _pair_colsum_t(x):
    top = _top_rows(x.shape)
    return (jnp.sum(jnp.where(top, x, 0.0), axis=0, keepdims=True),
            jnp.sum(jnp.where(top, 0.0, x), axis=0, keepdims=True))


def _key_query_iotas(t):
    return lax.broadcasted_iota(jnp.int32, (t, t), 0), lax.broadcasted_iota(jnp.int32, (t, t), 1)


def _walk_blocks(i, step, descending):
    pairs = i // 2
    odd = i % 2 == 1
    if descending:
        step([(i, True)])

        def loop(jj, carry):
            step([(i - 1 - 2 * jj, False), (i - 2 - 2 * jj, False)])
            return carry

        lax.fori_loop(0, pairs, loop, 0)

        @pl.when(odd)
        def _():
            step([(0, False)])
    else:
        def loop(jj, carry):
            step([(2 * jj, False), (2 * jj + 1, False)])
            return carry

        lax.fori_loop(0, pairs, loop, 0)

        @pl.when(odd)
        def _():
            step([(i - 1, False)])

        step([(i, True)])


def _attn_specs(s, tq, pair0):
    q_nat = pl.BlockSpec((tq, LANES), lambda p, i: (i, p + pair0))
    q_t = pl.BlockSpec((LANES, tq), lambda p, i: (p + pair0, i))
    k_nat = pl.BlockSpec((s, LANES), lambda p, i: (0, p + pair0))
    k_t = pl.BlockSpec((LANES, s), lambda p, i: (p + pair0, 0))
    return q_nat, q_t, k_nat, k_t


def _fox_fwd(qt, kh, vt, frow, fcol, name, tq=256):
    s = kh.shape[0]
    tq = _tile(s, tq)
    nq, half = s // tq, MIX // LANES // 2

    def body(qt_ref, k_ref, vt_ref, fr_ref, fc_ref, o_ref, lse_ref, m_s, l_s, acc_s):
        i = pl.program_id(1)
        qt_v = qt_ref[...]
        ft = fr_ref[0]
        key, qry = _key_query_iotas(tq)
        causal = key <= qry
        m_s[...] = jnp.full(m_s.shape, NEG, F32)
        l_s[...] = jnp.zeros_like(l_s)
        acc_s[...] = jnp.zeros_like(acc_s)

        top = _top_rows((LANES, tq))

        def step(blocks):
            rows = [pl.ds(pl.multiple_of(j * tq, tq), tq) for j, _ in blocks]
            zs = [_dot(_stack_heads(k_ref[r, :]), qt_v) for r in rows]
            m_cur, l_cur = [m_s[0], m_s[1]], [l_s[0], l_s[1]]
            acc = acc_s[...]
            for b, (_, masked) in enumerate(blocks):
                fk = fc_ref[0, rows[b], :]
                prs, alphas = [], []
                for hh in range(2):
                    sc = zs[b][hh * tq:(hh + 1) * tq] + (ft[hh:hh + 1, :] - fk[:, hh:hh + 1])
                    if masked:
                        sc = jnp.where(causal, sc, NEG)
                    m_new = jnp.maximum(m_cur[hh], jnp.max(sc, axis=0, keepdims=True))
                    alpha = jnp.exp(m_cur[hh] - m_new)
                    pr = jnp.exp(sc - m_new)
                    l_cur[hh] = alpha * l_cur[hh] + jnp.sum(pr, axis=0, keepdims=True)
                    m_cur[hh] = m_new
                    prs.append(pr.astype(BF16))
                    alphas.append(alpha)
                pv = _dot(_stack_heads_t(vt_ref[:, rows[b]]), jnp.concatenate(prs, axis=0))
                acc = jnp.where(top, alphas[0], alphas[1]) * acc + pv
            acc_s[...] = acc
            for hh in range(2):
                m_s[hh] = m_cur[hh]
                l_s[hh] = l_cur[hh]

        _walk_blocks(i, step, descending=False)
        o_ref[...] = (acc_s[...] / jnp.where(top, l_s[0], l_s[1])).T
        lse_ref[0, 0:1, :] = m_s[0] + jnp.log(l_s[0])
        lse_ref[0, 1:2, :] = m_s[1] + jnp.log(l_s[1])

    _, q_t, k_nat, k_t = _attn_specs(s, tq, 0)
    qstat = pl.BlockSpec((1, 2, tq), lambda p, i: (p, 0, i))
    return pl.pallas_call(
        body, name=name, grid=(half, nq),
        in_specs=[q_t, k_nat, k_t, qstat, pl.BlockSpec((1, s, 2), lambda p, i: (p, 0, 0))],
        out_specs=[pl.BlockSpec((tq, LANES), lambda p, i: (i, p)), qstat],
        out_shape=[jax.ShapeDtypeStruct((s, MIX // 2), F32), jax.ShapeDtypeStruct((half, 2, s), F32)],
        scratch_shapes=[pltpu.VMEM((2, 1, tq), F32), pltpu.VMEM((2, 1, tq), F32), pltpu.VMEM((LANES, tq), F32)],
        compiler_params=_params("parallel", "arbitrary"),
    )(qt, kh, vt, frow, fcol)


def _fox_bwd(qh, qt, kh, kt, vb, frow, fcol, lse, o, do, name, tq=256):
    s = kh.shape[0]
    tq = _tile(s, tq)
    nq, half = s // tq, MIX // LANES // 2

    def body(q_ref, qt_ref, k_ref, kt_ref, v_ref, fr_ref, fc_ref, lse_ref, o_ref, do_ref,
             dq_ref, dk_ref, dv_ref, dfk_ref, dfq_ref, dq_s, rs_s):
        i = pl.program_id(1)

        @pl.when(i == 0)
        def _():
            dk_ref[...] = jnp.zeros_like(dk_ref)
            dv_ref[...] = jnp.zeros_like(dv_ref)
            dfk_ref[...] = jnp.zeros_like(dfk_ref)

        q2 = _stack_heads(q_ref[...])
        qt_v = qt_ref[...]
        do_v = do_ref[...]
        do2 = _stack_heads(do_v.astype(BF16))
        dot_v = do_v.T.astype(BF16)
        dsum = _pair_colsum_t((do_v * o_ref[...]).T)
        ft, ls = fr_ref[0], lse_ref[0]
        key, qry = _key_query_iotas(tq)
        causal = key <= qry
        lane = lax.broadcasted_iota(jnp.int32, (2 * tq, LANES), 0) // tq
        pick2 = (lax.broadcasted_iota(jnp.int32, (2 * tq, LANES), 1) == lane).astype(BF16)
        q2_pick = jnp.concatenate([q2, pick2], axis=1)
        dq_s[...] = jnp.zeros_like(dq_s)
        rs_s[...] = jnp.zeros_like(rs_s)

        def step(blocks):
            rows = [pl.ds(pl.multiple_of(j * tq, tq), tq) for j, _ in blocks]
            zs = [_dot(_stack_heads(k_ref[r, :]), qt_v) for r in rows]
            dps = [_dot(_stack_heads(v_ref[r, :]), dot_v) for r in rows]
            rs = [rs_s[0], rs_s[1]]
            dq = None
            for b, (_, masked) in enumerate(blocks):
                fk = fc_ref[0, rows[b], :]
                prs, dss = [], []
                for hh in range(2):
                    blk = slice(hh * tq, (hh + 1) * tq)
                    sc = zs[b][blk] + (ft[hh:hh + 1, :] - fk[:, hh:hh + 1])
                    pr = jnp.exp(sc - ls[hh:hh + 1, :])
                    if masked:
                        pr = jnp.where(causal, pr, 0.0)
                    dsb = (pr * (dps[b][blk] - dsum[hh])).astype(BF16)
                    rs[hh] = rs[hh] + jnp.sum(dsb.astype(F32), axis=0, keepdims=True)
                    prs.append(pr.astype(BF16))
                    dss.append(dsb)
                dv_ref[rows[b], :] += _dot(jnp.concatenate(prs, axis=1), do2)
                both = _dot(jnp.concatenate(dss, axis=1), q2_pick)
                dk_ref[rows[b], :] += both[:, :LANES]
                dfk_ref[0, rows[b], :] -= both[:, LANES:]
                term = _dot(_stack_heads_t(kt_ref[:, rows[b]]), jnp.concatenate(dss, axis=0))
                dq = term if dq is None else dq + term
            rs_s[0], rs_s[1] = rs
            dq_s[...] += dq

        _walk_blocks(i, step, descending=False)
        dq_ref[...] = dq_s[...].T
        dfq_ref[0, 0:1, :] = rs_s[0]
        dfq_ref[0, 1:2, :] = rs_s[1]

    q_nat, q_t, k_nat, k_t = _attn_specs(s, tq, 0)
    qstat = pl.BlockSpec((1, 2, tq), lambda p, i: (p, 0, i))
    otile = pl.BlockSpec((tq, LANES), lambda p, i: (i, p))
    oresident = pl.BlockSpec((s, LANES), lambda p, i: (0, p))
    out = jax.ShapeDtypeStruct((s, MIX // 2), F32)
    return pl.pallas_call(
        body, name=name, grid=(half, nq),
        in_specs=[q_nat, q_t, k_nat, k_t, k_nat, qstat, pl.BlockSpec((1, s, 2), lambda p, i: (p, 0, 0)), qstat,
                  otile, q_nat],
        out_specs=[otile, oresident, oresident, pl.BlockSpec((1, s, LANES), lambda p, i: (p, 0, 0)), qstat],
        out_shape=[out, out, out, jax.ShapeDtypeStruct((half, s, LANES), F32),
                   jax.ShapeDtypeStruct((half, 2, s), F32)],
        scratch_shapes=[pltpu.VMEM((LANES, tq), F32), pltpu.VMEM((2, 1, tq), F32)],
        compiler_params=_params("parallel", "arbitrary"),
    )(qh, qt, kh, kt, vb, frow, fcol, lse, o, do)


def _log_sig_pair(z):
    lb = jnp.minimum(z, 0.0) - jnp.log(1.0 + jnp.exp(-jnp.abs(z)))
    return lb, lb - z


def _sb_fwd(qt, kh, vt, name, tq=256):
    s = kh.shape[0]
    tq = _tile(s, tq)
    nq, half = s // tq, MIX // LANES // 2

    def body(qt_ref, k_ref, vt_ref, o_ref, tot_ref, c_s, acc_s):
        i = pl.program_id(1)
        qt_v = qt_ref[...]
        key, qry = _key_query_iotas(tq)
        strict = key < qry
        later = _tri(tq, lambda a, b: b > a)
        c_s[...] = jnp.zeros_like(c_s)
        acc_s[...] = jnp.zeros_like(acc_s)

        def step(blocks):
            rows = [pl.ds(pl.multiple_of(j * tq, tq), tq) for j, _ in blocks]
            zs = [_dot(_stack_heads(k_ref[r, :]), qt_v) for r in rows]
            lbs, loms, afters = [], [], []
            for b, (_, masked) in enumerate(blocks):
                for hh in range(2):
                    lb, lom = _log_sig_pair(zs[b][hh * tq:(hh + 1) * tq])
                    if masked:
                        lom = jnp.where(strict, lom, 0.0)
                    lbs.append(lb)
                    loms.append(lom)
                afters.append(_split_dot_left(later, jnp.concatenate(loms[2 * b:2 * b + 2], axis=1), 2))
            carry = [c_s[0], c_s[1]]
            pv = None
            for b, (_, masked) in enumerate(blocks):
                ws = []
                for hh in range(2):
                    n = 2 * b + hh
                    w = jnp.exp(lbs[n] + afters[b][:, hh * tq:(hh + 1) * tq] + carry[hh])
                    if masked:
                        w = jnp.where(strict, w, 0.0)
                    ws.append(w.astype(BF16))
                    carry[hh] = carry[hh] + jnp.sum(loms[n], axis=0, keepdims=True)
                term = _dot(_stack_heads_t(vt_ref[:, rows[b]]), jnp.concatenate(ws, axis=0))
                pv = term if pv is None else pv + term
            c_s[0], c_s[1] = carry
            acc_s[...] += pv

        _walk_blocks(i, step, descending=True)
        o_ref[...] = acc_s[...].T
        tot_ref[0, 0:1, :] = c_s[0]
        tot_ref[0, 1:2, :] = c_s[1]

    _, q_t, k_nat, k_t = _attn_specs(s, tq, half)
    qstat = pl.BlockSpec((1, 2, tq), lambda p, i: (p, 0, i))
    return pl.pallas_call(
        body, name=name, grid=(half, nq),
        in_specs=[q_t, k_nat, k_t],
        out_specs=[pl.BlockSpec((tq, LANES), lambda p, i: (i, p)), qstat],
        out_shape=[jax.ShapeDtypeStruct((s, MIX // 2), F32), jax.ShapeDtypeStruct((half, 2, s), F32)],
        scratch_shapes=[pltpu.VMEM((2, 1, tq), F32), pltpu.VMEM((LANES, tq), F32)],
        compiler_params=_params("parallel", "arbitrary"),
    )(qt, kh, vt)


def _sb_bwd(qh, qt, kh, kt, vb, tot, do, name, tq=256):
    s = kh.shape[0]
    tq = _tile(s, tq)
    nq, half = s // tq, MIX // LANES // 2

    def body(q_ref, qt_ref, k_ref, kt_ref, v_ref, tot_ref, do_ref, dq_ref, dk_ref, dv_ref, rem_s, pg_s, dq_s):
        i = pl.program_id(1)

        @pl.when(i == 0)
        def _():
            dk_ref[...] = jnp.zeros_like(dk_ref)
            dv_ref[...] = jnp.zeros_like(dv_ref)

        q2 = _stack_heads(q_ref[...])
        qt_v = qt_ref[...]
        do_v = do_ref[...]
        do2 = _stack_heads(do_v.astype(BF16))
        dot_v = do_v.T.astype(BF16)
        key, qry = _key_query_iotas(tq)
        strict = key < qry
        upto = _tri(tq, lambda a, b: b <= a)
        before = _tri(tq, lambda a, b: b < a)
        tv = tot_ref[0]
        rem_s[0] = tv[0:1, :]
        rem_s[1] = tv[1:2, :]
        pg_s[...] = jnp.zeros_like(pg_s)
        dq_s[...] = jnp.zeros_like(dq_s)

        def step(blocks):
            nb = len(blocks)
            rows = [pl.ds(pl.multiple_of(j * tq, tq), tq) for j, _ in blocks]
            zs = [_dot(_stack_heads(k_ref[r, :]), qt_v) for r in rows]
            dws = [_dot(_stack_heads(v_ref[r, :]), dot_v) for r in rows]
            lbs, loms, prefixes = [], [], []
            for b, (_, masked) in enumerate(blocks):
                for hh in range(2):
                    lb, lom = _log_sig_pair(zs[b][hh * tq:(hh + 1) * tq])
                    if masked:
                        lom = jnp.where(strict, lom, 0.0)
                    lbs.append(lb)
                    loms.append(lom)
                prefixes.append(_split_dot_left(upto, jnp.concatenate(loms[2 * b:2 * b + 2], axis=1), 2))
            rem = [rem_s[0], rem_s[1]]
            ws, gs, gpres = [], [], []
            for b, (_, masked) in enumerate(blocks):
                for hh in range(2):
                    n = 2 * b + hh
                    blk = slice(hh * tq, (hh + 1) * tq)
                    w = jnp.exp(lbs[n] + (rem[hh] - prefixes[b][:, blk]))
                    if masked:
                        w = jnp.where(strict, w, 0.0)
                    gs.append(dws[b][blk] * w)
                    ws.append(w.astype(BF16))
                    rem[hh] = rem[hh] - jnp.sum(loms[n], axis=0, keepdims=True)
                gpres.append(_dot(before, jnp.concatenate(gs[2 * b:2 * b + 2], axis=1).astype(BF16)))
                dv_ref[rows[b], :] += _dot(jnp.concatenate(ws[2 * b:2 * b + 2], axis=1), do2)
            rem_s[0], rem_s[1] = rem
            pg = [pg_s[0], pg_s[1]]
            dq = None
            for b, (_, masked) in enumerate(blocks):
                dzs = []
                for hh in range(2):
                    n = 2 * b + hh
                    g = gs[n]
                    dz = g - jnp.exp(lbs[n]) * (g + (pg[hh] + gpres[b][:, hh * tq:(hh + 1) * tq]))
                    if masked:
                        dz = jnp.where(strict, dz, 0.0)
                    dzs.append(dz.astype(BF16))
                    pg[hh] = pg[hh] + jnp.sum(g, axis=0, keepdims=True)
                dk_ref[rows[b], :] += _dot(jnp.concatenate(dzs, axis=1), q2)
                term = _dot(_stack_heads_t(kt_ref[:, rows[b]]), jnp.concatenate(dzs, axis=0))
                dq = term if dq is None else dq + term
            pg_s[0], pg_s[1] = pg
            dq_s[...] += dq

        _walk_blocks(i, step, descending=False)
        dq_ref[...] = dq_s[...].T

    q_nat, q_t, k_nat, k_t = _attn_specs(s, tq, half)
    qstat = pl.BlockSpec((1, 2, tq), lambda p, i: (p, 0, i))
    otile = pl.BlockSpec((tq, LANES), lambda p, i: (i, p))
    oresident = pl.BlockSpec((s, LANES), lambda p, i: (0, p))
    out = jax.ShapeDtypeStruct((s, MIX // 2), F32)
    return pl.pallas_call(
        body, name=name, grid=(half, nq),
        in_specs=[q_nat, q_t, k_nat, k_t, k_nat, qstat, q_nat],
        out_specs=[otile, oresident, oresident], out_shape=[out, out, out],
        scratch_shapes=[pltpu.VMEM((2, 1, tq), F32), pltpu.VMEM((2, 1, tq), F32), pltpu.VMEM((LANES, tq), F32)],
        compiler_params=_params("parallel", "arbitrary"),
    )(qh, qt, kh, kt, vb, tot, do)


def _loss_head(y, target, name, ts=512):
    s, d = y.shape
    ts = _tile(s, ts, 8)
    nt = s // ts

    def body(y_ref, t_ref, dy_ref, l_ref, acc):
        i = pl.program_id(0)
        err = y_ref[...] - t_ref[...]
        dy_ref[...] = err * (1.0 / d)

        @pl.when(i == 0)
        def _():
            acc[...] = jnp.zeros_like(acc)

        acc[...] += jnp.sum(err * err, axis=0, keepdims=True)

        @pl.when(i == nt - 1)
        def _():
            tot = jnp.sum(acc[...], keepdims=True).reshape(1, 1) * (0.5 / d)
            l_ref[...] = jnp.broadcast_to(tot, l_ref.shape)

    row = pl.BlockSpec((ts, d), lambda i: (i, 0))
    dy, l = pl.pallas_call(
        body, name=name, grid=(nt,), in_specs=[row, row],
        out_specs=[row, pl.BlockSpec((8, LANES), lambda i: (0, 0))],
        out_shape=[jax.ShapeDtypeStruct((s, d), F32), jax.ShapeDtypeStruct((8, LANES), F32)],
        scratch_shapes=[pltpu.VMEM((1, d), F32)],
        compiler_params=_params("arbitrary"),
    )(y, target)
    return l[0, 0], dy


def _coords():
    return lax.axis_index("x"), lax.axis_index("y"), lax.axis_index("c")


def _other_chips(xi, yi):
    return [(1 - xi, yi), (xi, 1 - yi), (1 - xi, 1 - yi)]


def _all_gather(xs, name):
    n = len(xs)

    def body(*refs):
        x_refs, out_refs = refs[:n], refs[n:2 * n]
        send_sems, recv_sems, local_sems = refs[2 * n:]
        xi, yi, ci = _coords()
        me, sibling = (xi, yi, ci), (xi, yi, 1 - ci)
        chips = _other_chips(xi, yi)

        def slot(a, px, py, pc):
            return out_refs[a].at[4 * px + 2 * py + pc]

        def copy(a, k, block, to, src=None):
            return pltpu.make_async_remote_copy(
                src_ref=slot(a, *block) if src is None else src, dst_ref=slot(a, *block),
                send_sem=send_sems.at[a, k], recv_sem=recv_sems.at[a, k], device_id=to, device_id_type=MESH)

        mine = [pltpu.make_async_copy(x_refs[a], slot(a, *me), local_sems.at[a]) for a in range(n)]
        for cp in mine:
            cp.start()
        first = []
        for a in range(n):
            first.append(copy(a, 0, me, sibling, src=x_refs[a]))
            first += [copy(a, 1 + j, me, (*chip, ci), src=x_refs[a]) for j, chip in enumerate(chips)]
        for cp in first:
            cp.start()
        passed = []
        for j, chip in enumerate(chips):
            for a in range(n):
                copy(a, 1 + j, (*chip, ci), me).wait_recv()
                fwd = copy(a, 4 + j, (*chip, ci), sibling)
                fwd.start()
                passed.append(fwd)
        for a in range(n):
            copy(a, 0, sibling, me).wait_recv()
            for j, chip in enumerate(chips):
                copy(a, 4 + j, (*chip, 1 - ci), me).wait_recv()
        for cp in first + passed:
            cp.wait_send()
        for cp in mine:
            cp.wait()

    return pl.pallas_call(
        body, name=name, out_shape=[jax.ShapeDtypeStruct((N_DEV,) + x.shape, x.dtype) for x in xs],
        in_specs=[ANY] * n, out_specs=[ANY] * n,
        scratch_shapes=[pltpu.SemaphoreType.DMA((n, 7)), pltpu.SemaphoreType.DMA((n, 7)),
                        pltpu.SemaphoreType.DMA((n,))],
    )(*xs)


def _sibling_exchange(gs, name):
    n = len(gs)

    def body(*refs):
        g_refs, recv_refs = refs[:n], refs[n:2 * n]
        send_sems, recv_sems = refs[2 * n:]
        xi, yi, ci = _coords()
        cps = [pltpu.make_async_remote_copy(
            src_ref=g_refs[a].at[2 * chip + (1 - ci)], dst_ref=recv_refs[a].at[chip],
            send_sem=send_sems.at[a, chip], recv_sem=recv_sems.at[a, chip],
            device_id=(xi, yi, 1 - ci), device_id_type=MESH) for a in range(n) for chip in range(N_CHIP)]
        for cp in cps:
            cp.start()
        for cp in cps:
            cp.wait()

    return pl.pallas_call(
        body, name=name, out_shape=[jax.ShapeDtypeStruct((N_CHIP,) + g.shape[1:], g.dtype) for g in gs],
        in_specs=[ANY] * n, out_specs=[ANY] * n,
        scratch_shapes=[pltpu.SemaphoreType.DMA((n, N_CHIP)), pltpu.SemaphoreType.DMA((n, N_CHIP))],
    )(*gs)


def _pair_add(g, recv, ids, name, tr=256):
    _, r, c = g.shape
    tr = _tile(r, tr, 16)

    def body(ids_ref, g_ref, r_ref, p_ref, own_ref):
        kk = pl.program_id(1)
        tot = g_ref[0].astype(F32) + r_ref[0].astype(F32)
        p_ref[0] = tot.astype(BF16)

        @pl.when(kk == ids_ref[1])
        def _():
            own_ref[...] = tot

    grid_spec = pltpu.PrefetchScalarGridSpec(
        num_scalar_prefetch=1, grid=(r // tr, N_CHIP),
        in_specs=[pl.BlockSpec((1, tr, c), lambda i, kk, ids: (2 * kk + ids[0], i, 0)),
                  pl.BlockSpec((1, tr, c), lambda i, kk, ids: (kk, i, 0))],
        out_specs=[pl.BlockSpec((1, tr, c), lambda i, kk, ids: (kk, i, 0)),
                   pl.BlockSpec((tr, c), lambda i, kk, ids: (i, 0))])
    return pl.pallas_call(
        body, name=name, grid_spec=grid_spec,
        out_shape=[jax.ShapeDtypeStruct((N_CHIP, r, c), BF16), jax.ShapeDtypeStruct((r, c), F32)],
        compiler_params=_params("parallel", "arbitrary"),
    )(ids, g, recv)


def _chip_exchange(ps, name):
    n = len(ps)

    def body(*refs):
        p_refs, recv_refs = refs[:n], refs[n:2 * n]
        send_sems, recv_sems = refs[2 * n:]
        xi, yi, ci = _coords()
        mine = 2 * xi + yi
        chips = _other_chips(xi, yi)

        def copy(a, k, cx, cy):
            return pltpu.make_async_remote_copy(
                src_ref=p_refs[a].at[2 * cx + cy], dst_ref=recv_refs[a].at[mine],
                send_sem=send_sems.at[a, k], recv_sem=recv_sems.at[a, k],
                device_id=(cx, cy, ci), device_id_type=MESH)

        def landed(a, k, cx, cy):
            return pltpu.make_async_remote_copy(
                src_ref=p_refs[a].at[mine], dst_ref=recv_refs[a].at[2 * cx + cy],
                send_sem=send_sems.at[a, k], recv_sem=recv_sems.at[a, k],
                device_id=(cx, cy, ci), device_id_type=MESH)

        sends = [copy(a, k, cx, cy) for a in range(n) for k, (cx, cy) in enumerate(chips)]
        for cp in sends:
            cp.start()
        for a in range(n):
            for k, (cx, cy) in enumerate(chips):
                landed(a, k, cx, cy).wait_recv()
        for cp in sends:
            cp.wait_send()

    return pl.pallas_call(
        body, name=name, out_shape=[jax.ShapeDtypeStruct(p.shape, p.dtype) for p in ps],
        in_specs=[ANY] * n, out_specs=[ANY] * n,
        scratch_shapes=[pltpu.SemaphoreType.DMA((n, 3)), pltpu.SemaphoreType.DMA((n, 3))],
    )(*ps)


def _adamw_math(w, g, m, v):
    m = ADAM_B1 * m + (1.0 - ADAM_B1) * g
    v = ADAM_B2 * v + (1.0 - ADAM_B2) * (g * g)
    m_hat = m / (1.0 - ADAM_B1 ** ADAM_STEP)
    v_hat = v / (1.0 - ADAM_B2 ** ADAM_STEP)
    delta = -ADAM_LR * (m_hat / (jnp.sqrt(v_hat) + ADAM_EPS) + ADAM_WD * w)
    return delta, m, v


def _adamw_reduce(own, recv, ids, w, m, v, name, tr=256):
    r, c = w.shape
    tr = _tile(r, tr, 16)

    def body(ids_ref, own_ref, recv_ref, w_ref, m_ref, v_ref, g_ref, d_ref, mo_ref, vo_ref):
        mine = ids_ref[1]
        g = None
        for kk in range(N_CHIP):
            term = jnp.where(mine == kk, own_ref[...], recv_ref[kk].astype(F32))
            g = term if g is None else g + term
        delta, m_new, v_new = _adamw_math(w_ref[...], g, m_ref[...], v_ref[...])
        g_ref[...] = g
        d_ref[...] = delta
        mo_ref[...] = m_new
        vo_ref[...] = v_new

    row = pl.BlockSpec((tr, c), lambda i, ids: (i, 0))
    grid_spec = pltpu.PrefetchScalarGridSpec(
        num_scalar_prefetch=1, grid=(r // tr,),
        in_specs=[row, pl.BlockSpec((N_CHIP, tr, c), lambda i, ids: (0, i, 0)), row, row, row],
        out_specs=[row, row, row, row])
    out = jax.ShapeDtypeStruct((r, c), F32)
    return pl.pallas_call(
        body, name=name, grid_spec=grid_spec, out_shape=[out, out, out, out],
        compiler_params=_params("parallel"),
    )(ids, own, recv, w, m, v)


def _sum_sources(a, name):
    n, r, c = a.shape

    def body(a_ref, o_ref):
        tot = a_ref[0]
        for kk in range(1, n):
            tot = tot + a_ref[kk]
        o_ref[...] = tot

    return pl.pallas_call(
        body, name=name, out_shape=jax.ShapeDtypeStruct((r, c), F32),
        in_specs=[pl.BlockSpec(memory_space=pltpu.VMEM)], out_specs=pl.BlockSpec(memory_space=pltpu.VMEM),
    )(a)


def _adamw_small(w, g, m, v, name):
    def body(w_ref, g_ref, m_ref, v_ref, d_ref, mo_ref, vo_ref):
        delta, m_new, v_new = _adamw_math(w_ref[...], g_ref[...], m_ref[...], v_ref[...])
        d_ref[...] = delta
        mo_ref[...] = m_new
        vo_ref[...] = v_new

    vm = pl.BlockSpec(memory_space=pltpu.VMEM)
    out = jax.ShapeDtypeStruct(w.shape, F32)
    return pl.pallas_call(body, name=name, out_shape=[out, out, out], in_specs=[vm] * 4, out_specs=[vm] * 3)(w, g, m, v)


def _pack(parts, width, row_mult):
    flat = jnp.concatenate([p.reshape(-1) for p in parts])
    rows = -(-flat.shape[0] // width)
    rows = -(-rows // row_mult) * row_mult
    return jnp.pad(flat, (0, rows * width - flat.shape[0])).reshape(rows, width)


def _unpack(flat, shapes):
    out, off = [], 0
    lead = flat.shape[:-1]
    for shp in shapes:
        n = 1
        for dd in shp:
            n *= dd
        out.append(flat[..., off:off + n].reshape(lead + tuple(shp)))
        off += n
    return out


def _rows2d(w):
    return w.reshape(w.shape[0] * w.shape[1], w.shape[2])


def _cols_to_dev(g):
    l, k, n = g.shape
    return g.reshape(l * k, N_DEV, n // N_DEV).transpose(1, 0, 2)


def _rows_to_dev(g):
    l, k, n = g.shape
    rs = k // N_DEV
    return g.reshape(l, N_DEV, rs, n).transpose(1, 0, 2, 3).reshape(N_DEV, l * rs, n)


def _dev_to_cols(a, l):
    _, lk, cs = a.shape
    return a.transpose(1, 0, 2).reshape(l, lk // l, N_DEV * cs)


def _dev_to_rows(a, l):
    _, lr, n = a.shape
    rs = lr // l
    return a.reshape(N_DEV, l, rs, n).transpose(1, 0, 2, 3).reshape(l, N_DEV * rs, n)


def kernel(x, attn_norm, attn_w_in, attn_f_bias, fox_q_gain, fox_k_gain, sb_q_gain, sb_k_gain, attn_w_out, conv_norm, conv_w_in, conv_kernel, conv_w_out, ffn_norm, ffn_w_up, ffn_conv, ffn_w_down, loss_target, m_attn_norm, m_attn_w_in, m_attn_f_bias, m_fox_q_gain, m_fox_k_gain, m_sb_q_gain, m_sb_k_gain, m_attn_w_out, m_conv_norm, m_conv_w_in, m_conv_kernel, m_conv_w_out, m_ffn_norm, m_ffn_w_up, m_ffn_conv, m_ffn_w_down, v_attn_norm, v_attn_w_in, v_attn_f_bias, v_fox_q_gain, v_fox_k_gain, v_sb_q_gain, v_sb_k_gain, v_attn_w_out, v_conv_norm, v_conv_w_in, v_conv_kernel, v_conv_w_out, v_ffn_norm, v_ffn_w_up, v_ffn_conv, v_ffn_w_down):
    s = x.shape[1]
    n_attn, n_conv, depth = attn_w_in.shape[0], conv_w_in.shape[0], ffn_w_up.shape[0]
    xi, yi, ci = _coords()
    me = 4 * xi + 2 * yi + ci
    ids = jnp.stack([ci, 2 * xi + yi]).astype(jnp.int32)

    big = [attn_w_in, attn_w_out, conv_w_in, conv_w_out, ffn_w_up, ffn_w_down]
    big_m = [m_attn_w_in, m_attn_w_out, m_conv_w_in, m_conv_w_out, m_ffn_w_up, m_ffn_w_down]
    big_v = [v_attn_w_in, v_attn_w_out, v_conv_w_in, v_conv_w_out, v_ffn_w_up, v_ffn_w_down]
    big_names = ["attn_w_in", "attn_w_out", "conv_w_in", "conv_w_out", "ffn_w_up", "ffn_w_down"]
    small_sh = [conv_norm, conv_kernel, ffn_conv]
    small_sh_shapes = [w.shape for w in small_sh]
    rep = [attn_norm, attn_f_bias, fox_q_gain, fox_k_gain, sb_q_gain, sb_k_gain, ffn_norm]
    rep_shapes = [w.shape for w in rep]

    small_pack = _pack(small_sh, LANES, 8)
    gathered = _all_gather([_rows2d(w).astype(BF16) for w in big] + [small_pack], "gather_weights")
    a_w_in = _dev_to_cols(gathered[0], n_attn)
    a_w_in = jnp.pad(a_w_in, ((0, 0), (0, 0), (0, ATTN_IN_PAD - ATTN_IN)))
    a_w_out = _dev_to_rows(gathered[1], n_attn)
    c_w_in = _dev_to_cols(gathered[2], n_conv)
    c_w_out = _dev_to_rows(gathered[3], n_conv)
    f_w_up = _dev_to_cols(gathered[4], depth)
    f_w_down = _dev_to_rows(gathered[5], depth)
    cn, ckern, fconv = _unpack(gathered[6].reshape(N_DEV, -1), small_sh_shapes)
    conv_norm_f = cn.transpose(1, 0, 2).reshape(n_conv, D_MODEL)
    conv_kernel_f = ckern.transpose(1, 2, 0, 3).reshape(n_conv, 3, D_MODEL)
    ffn_conv_f = fconv.transpose(1, 2, 0, 3).reshape(depth, 3, 2 * D_FF)

    def pair_gain(fox_g, sb_g):
        f2 = jnp.concatenate([fox_g, fox_g])
        s2 = jnp.concatenate([sb_g, sb_g])
        return jnp.concatenate([jnp.tile(f2[None], (4, 1)), jnp.tile(s2[None], (4, 1))])[:, None, :]

    h = x[0]
    saved = []
    for layer in range(depth):
        i = layer // 2
        tag = "l%d" % layer
        rec = {"h_in": h}
        if layer % 2 == 0:
            xn = _rms_fwd(h, attn_norm[i], tag + "_attn_rms")
            proj = _matmul(xn, a_w_in[i], tag + "_attn_in", tn=640)
            gq, gk = pair_gain(fox_q_gain[i], sb_q_gain[i]), pair_gain(fox_k_gain[i], sb_k_gain[i])
            qh, kh, vb, qt, kt, vt = _qkv_prep(proj, gq, gk, tag + "_qkv_prep")
            logit = proj[:, 3 * MIX:3 * MIX + H_FOX].T.reshape(H_FOX, s // LANES, LANES)
            cum = _fgate_fwd(logit, attn_f_bias[i], tag + "_fgate")
            frow = cum.reshape(H_FOX // 2, 2, s)
            fcol = frow.transpose(0, 2, 1)
            o_fox, lse = _fox_fwd(qt, kh, vt, frow, fcol, tag + "_fox_fwd")
            o_sb, tot = _sb_fwd(qt, kh, vt, tag + "_sb_fwd")
            o = jnp.concatenate([o_fox, o_sb], axis=1)
            h = _matmul(o, a_w_out[i], tag + "_attn_out", add=h)
            rec.update(xn=xn, proj=proj, gq=gq, gk=gk, qh=qh, kh=kh, vb=vb, qt=qt, kt=kt, logit=logit, frow=frow,
                       fcol=fcol, o_fox=o_fox, lse=lse, tot=tot, o=o)
        else:
            xn = _rms_fwd(h, conv_norm_f[i], tag + "_conv_rms")
            proj = _matmul(xn, c_w_in[i], tag + "_conv_in")
            y = _sconv_fwd(proj, conv_kernel_f[i], tag + "_sconv_fwd")
            h = _matmul(y, c_w_out[i], tag + "_conv_out", add=h)
            rec.update(xn=xn, proj=proj, y=y)
        rec["h_mid"] = h
        xn2 = _rms_fwd(h, ffn_norm[layer], tag + "_ffn_rms")
        up = _matmul(xn2, f_w_up[layer], tag + "_ffn_up")
        act = _ffn_act_fwd(up, ffn_conv_f[layer], tag + "_ffn_act")
        h = _matmul(act, f_w_down[layer], tag + "_ffn_down", add=h, tk=1408)
        rec.update(xn2=xn2, up=up, act=act)
        saved.append(rec)

    loss_local, dh = _loss_head(h, loss_target[0], "loss_head")
    loss = lax.psum(loss_local, ("x", "y", "c"))

    g_attn_norm, g_attn_w_in, g_f_bias = [None] * n_attn, [None] * n_attn, [None] * n_attn
    g_fq, g_fk, g_sq, g_sk, g_attn_w_out = ([None] * n_attn for _ in range(5))
    g_conv_norm, g_conv_w_in, g_conv_kernel, g_conv_w_out = ([None] * n_conv for _ in range(4))
    g_ffn_norm, g_ffn_w_up, g_ffn_conv, g_ffn_w_down = ([None] * depth for _ in range(4))

    for layer in reversed(range(depth)):
        i = layer // 2
        tag = "l%d" % layer
        rec = saved[layer]
        da = _matmul(dh, f_w_down[layer].T, tag + "_ffn_down_dx", tn=1408)
        g_ffn_w_down[layer] = _matmul_tn(rec["act"], dh, tag + "_ffn_down_dw", tm=1408, tn=1024)
        dug, duv, dwg, dwv = _ffn_act_bwd(rec["up"], ffn_conv_f[layer], da, tag + "_ffn_act_bwd")
        g_ffn_conv[layer] = jnp.concatenate([dwg, dwv], axis=1)
        dup = jnp.concatenate([dug, duv], axis=1)
        g_ffn_w_up[layer] = _matmul_tn(rec["xn2"], dup, tag + "_ffn_up_dw", tn=1408)
        dxn = _matmul(dup, f_w_up[layer].T, tag + "_ffn_up_dx", tn=1024)
        dh, g_ffn_norm[layer] = _rms_bwd(rec["h_mid"], dxn, ffn_norm[layer], dh, tag + "_ffn_rms_bwd")
        if layer % 2 == 0:
            do = _matmul(dh, a_w_out[i].T, tag + "_attn_out_dx", tn=1024)
            g_attn_w_out[i] = _matmul_tn(rec["o"], dh, tag + "_attn_out_dw", tn=1024)
            dq_f, dk_f, dv_f, dfk, dfq = _fox_bwd(rec["qh"], rec["qt"], rec["kh"], rec["kt"], rec["vb"], rec["frow"],
                                                  rec["fcol"], rec["lse"], rec["o_fox"], do, tag + "_fox_bwd")
            dq_s, dk_s, dv_s = _sb_bwd(rec["qh"], rec["qt"], rec["kh"], rec["kt"], rec["vb"], rec["tot"], do,
                                       tag + "_sb_bwd")
            dq, dk, dv, dgq, dgk = _qkv_prep_bwd(rec["proj"], rec["gq"], rec["gk"], (dq_f, dq_s), (dk_f, dk_s),
                                                 (dv_f, dv_s), tag + "_qkv_prep_bwd")
            dcum = (dfq + dfk[:, :, 0:2].transpose(0, 2, 1)).reshape(H_FOX, s // LANES, LANES)
            dlogit, dbias = _fgate_bwd(rec["logit"], attn_f_bias[i], dcum, tag + "_fgate_bwd")
            g_f_bias[i] = dbias[:, 0]
            dgate = jnp.pad(dlogit.reshape(H_FOX, s).T, ((0, 0), (0, LANES - H_FOX))).astype(BF16)
            dproj = jnp.concatenate([dq, dk, dv, dgate], axis=1)

            def fold(dg):
                per_head = dg.reshape(16, HEAD_DIM)
                return jnp.sum(per_head[:8], axis=0), jnp.sum(per_head[8:], axis=0)

            g_fq[i], g_sq[i] = fold(dgq)
            g_fk[i], g_sk[i] = fold(dgk)
            g_attn_w_in[i] = _matmul_tn(rec["xn"], dproj, tag + "_attn_in_dw", tn=640)[:, :ATTN_IN]
            dxn = _matmul(dproj, a_w_in[i].T, tag + "_attn_in_dx", tn=1024, tk=640)
            dh, g_attn_norm[i] = _rms_bwd(rec["h_in"], dxn, attn_norm[i], dh, tag + "_attn_rms_bwd")
        else:
            dy = _matmul(dh, c_w_out[i].T, tag + "_conv_out_dx", tn=1024)
            g_conv_w_out[i] = _matmul_tn(rec["y"], dh, tag + "_conv_out_dw", tn=1024)
            db, dc, du, g_conv_kernel[i] = _sconv_bwd(rec["proj"], conv_kernel_f[i], dy, tag + "_sconv_bwd")
            dproj = jnp.concatenate([db, dc, du], axis=1)
            g_conv_w_in[i] = _matmul_tn(rec["xn"], dproj, tag + "_conv_in_dw", tn=1024)
            dxn = _matmul(dproj, c_w_in[i].T, tag + "_conv_in_dx", tn=1024)
            dh, g_conv_norm[i] = _rms_bwd(rec["h_in"], dxn, conv_norm_f[i], dh, tag + "_conv_rms_bwd")
    grad_x = dh[None]

    gb = [_cols_to_dev(jnp.stack(g_attn_w_in).astype(BF16)), _rows_to_dev(jnp.stack(g_attn_w_out).astype(BF16)),
          _cols_to_dev(jnp.stack(g_conv_w_in).astype(BF16)), _rows_to_dev(jnp.stack(g_conv_w_out).astype(BF16)),
          _cols_to_dev(jnp.stack(g_ffn_w_up).astype(BF16)), _rows_to_dev(jnp.stack(g_ffn_w_down).astype(BF16))]
    from_sibling = _sibling_exchange(gb, "reduce_sibling")
    pairs = [_pair_add(g, r, ids, "reduce_pair_add_" + nm) for g, r, nm in zip(gb, from_sibling, big_names)]
    from_chips = _chip_exchange([p[0] for p in pairs], "reduce_chips")
    grads_big, delta_big, newm_big, newv_big = [], [], [], []
    for a, nm in enumerate(big_names):
        g_a, d_a, m_a, v_a = _adamw_reduce(pairs[a][1], from_chips[a], ids, _rows2d(big[a]), _rows2d(big_m[a]),
                                           _rows2d(big_v[a]), "adamw_" + nm)
        shp = big[a].shape
        grads_big.append(g_a.reshape(shp))
        delta_big.append(d_a.reshape(shp))
        newm_big.append(m_a.reshape(shp))
        newv_big.append(v_a.reshape(shp))

    rep_g = [jnp.stack(g_attn_norm), jnp.stack(g_f_bias), jnp.stack(g_fq), jnp.stack(g_fk), jnp.stack(g_sq),
             jnp.stack(g_sk), jnp.stack(g_ffn_norm)]
    sh_g = [jnp.stack(g_conv_norm).reshape(n_conv, N_DEV, -1).transpose(1, 0, 2),
            jnp.stack(g_conv_kernel).reshape(n_conv, 3, N_DEV, -1).transpose(2, 0, 1, 3),
            jnp.stack(g_ffn_conv).reshape(depth, 3, N_DEV, -1).transpose(2, 0, 1, 3)]
    n_rep = sum(int(a.size) for a in rep)
    n_sh = sum(int(a.size) for a in small_sh)
    partial = _pack(rep_g + [jnp.concatenate([a.reshape(N_DEV, -1) for a in sh_g], axis=1)], LANES, 8)
    total = _sum_sources(_all_gather([partial], "gather_small_grads")[0], "sum_small_grads").reshape(-1)
    rep_tot = total[:n_rep]
    sh_tot = lax.dynamic_slice_in_dim(total[n_rep:n_rep + N_DEV * n_sh].reshape(N_DEV, n_sh), me, 1, axis=0)[0]
    g_small = _pack([rep_tot, sh_tot], LANES, 8)

    def small_pack_of(rep_list, sh_list):
        return _pack(rep_list + sh_list, LANES, 8)

    d_small, m_small, v_small = _adamw_small(
        small_pack_of(rep, small_sh), g_small,
        small_pack_of([m_attn_norm, m_attn_f_bias, m_fox_q_gain, m_fox_k_gain, m_sb_q_gain, m_sb_k_gain, m_ffn_norm],
                      [m_conv_norm, m_conv_kernel, m_ffn_conv]),
        small_pack_of([v_attn_norm, v_attn_f_bias, v_fox_q_gain, v_fox_k_gain, v_sb_q_gain, v_sb_k_gain, v_ffn_norm],
                      [v_conv_norm, v_conv_kernel, v_ffn_conv]),
        "adamw_small")
    small_shapes = rep_shapes + small_sh_shapes

    def split_small(a):
        return _unpack(a.reshape(-1), small_shapes)

    def ordered(big_list, small_list):
        an, fb, fq, fk, sq, sk, fn, cno, cke, fco = small_list
        awi, awo, cwi, cwo, fwu, fwd = big_list
        return [an, awi, fb, fq, fk, sq, sk, awo, cno, cwi, cke, cwo, fn, fwu, fco, fwd]

    grads = ordered(grads_big, split_small(g_small))
    deltas = ordered(delta_big, split_small(d_small))
    new_m = ordered(newm_big, split_small(m_small))
    new_v = ordered(newv_big, split_small(v_small))
    return (loss, grad_x, *grads, *deltas, *new_m, *new_v)
```

```python
import jax
import jax.numpy as jnp
from jax import lax
from jax.experimental import pallas as pl
from jax.experimental.pallas import tpu as pltpu

F32 = jnp.float32
BF16 = jnp.bfloat16

D_MODEL = 1024
HEAD_DIM = 64
H_FOX = 8
MIX = 1024
ATTN_IN = 3 * MIX + H_FOX
ATTN_IN_PAD = 3 * MIX + 128
D_FF = 2816
EPS = 1e-6
NEG = -1e30
LANES = 128
N_DEV = 8
N_CHIP = 4

ADAM_LR = 0.001
ADAM_B1 = 0.9
ADAM_B2 = 0.999
ADAM_EPS = 1e-08
ADAM_WD = 0.01
ADAM_STEP = 10

VMEM_LIMIT = 56 * 1024 * 1024
MESH = pl.DeviceIdType.MESH
ANY = pl.BlockSpec(memory_space=pl.ANY)


def _params(*sem):
    return pltpu.CompilerParams(dimension_semantics=sem, vmem_limit_bytes=VMEM_LIMIT)


def _tile(n, target, mult=LANES):
    best = None
    for t in range(mult, min(n, target) + 1, mult):
        if n % t == 0:
            best = t
    return best if best is not None else n


def _dot(a, b):
    return jnp.dot(a, b, preferred_element_type=F32)


def _dot_tn(a, b):
    return lax.dot_general(a, b, (((0,), (0,)), ((), ())), preferred_element_type=F32)


def _split_dot(x, m, passes):
    acc = None
    rem = x
    for _ in range(passes):
        part = rem.astype(BF16)
        term = _dot(part, m)
        acc = term if acc is None else acc + term
        rem = rem - part.astype(F32)
    return acc


def _split_dot_left(m, x, passes):
    acc = None
    rem = x
    for _ in range(passes):
        part = rem.astype(BF16)
        term = _dot(m, part)
        acc = term if acc is None else acc + term
        rem = rem - part.astype(F32)
    return acc


def _matmul(a, b, name, add=None, out_dtype=F32, tm=1024, tn=512, tk=1024):
    m, k = a.shape
    n = b.shape[1]
    tm, tn, tk = _tile(m, tm, 8), _tile(n, tn), _tile(k, tk)
    nk = k // tk
    has_add = add is not None

    def body(*refs):
        a_ref, b_ref = refs[0], refs[1]
        add_ref = refs[2] if has_add else None
        o_ref = refs[2 + has_add]

        def finish(acc):
            if has_add:
                acc = acc + add_ref[...]
            o_ref[...] = acc.astype(out_dtype)

        p = _dot(a_ref[...].astype(BF16), b_ref[...].astype(BF16))
        if nk == 1:
            finish(p)
        else:
            acc_ref = refs[-1]
            kk = pl.program_id(2)

            @pl.when(kk == 0)
            def _():
                acc_ref[...] = p

            @pl.when(kk > 0)
            def _():
                acc_ref[...] += p

            @pl.when(kk == nk - 1)
            def _():
                finish(acc_ref[...])

    in_specs = [pl.BlockSpec((tm, tk), lambda i, j, kk: (i, kk)),
                pl.BlockSpec((tk, tn), lambda i, j, kk: (kk, j))]
    args = [a, b]
    if has_add:
        in_specs.append(pl.BlockSpec((tm, tn), lambda i, j, kk: (i, j)))
        args.append(add)
    return pl.pallas_call(
        body, name=name, grid=(m // tm, n // tn, nk), in_specs=in_specs,
        out_specs=pl.BlockSpec((tm, tn), lambda i, j, kk: (i, j)),
        out_shape=jax.ShapeDtypeStruct((m, n), out_dtype),
        scratch_shapes=[pltpu.VMEM((tm, tn), F32)] if nk > 1 else [],
        compiler_params=_params("parallel", "parallel", "arbitrary"),
    )(*args)


def _matmul_tn(a, b, name, tm=1024, tn=512, ts=2048):
    s, m = a.shape
    n = b.shape[1]
    tm, tn, ts = _tile(m, tm), _tile(n, tn), _tile(s, ts, 8)

    def body(a_ref, b_ref, o_ref):
        kk = pl.program_id(2)
        p = _dot_tn(a_ref[...].astype(BF16), b_ref[...].astype(BF16))

        @pl.when(kk == 0)
        def _():
            o_ref[...] = p

        @pl.when(kk > 0)
        def _():
            o_ref[...] += p

    return pl.pallas_call(
        body, name=name, grid=(m // tm, n // tn, s // ts),
        in_specs=[pl.BlockSpec((ts, tm), lambda i, j, kk: (kk, i)),
                  pl.BlockSpec((ts, tn), lambda i, j, kk: (kk, j))],
        out_specs=pl.BlockSpec((tm, tn), lambda i, j, kk: (i, j)),
        out_shape=jax.ShapeDtypeStruct((m, n), F32),
        compiler_params=_params("parallel", "parallel", "arbitrary"),
    )(a, b)


def _rms_fwd(h, g, name, ts=512):
    s, d = h.shape
    ts = _tile(s, ts, 8)

    def body(h_ref, g_ref, o_ref):
        x = h_ref[...]
        r = lax.rsqrt(jnp.mean(x * x, axis=-1, keepdims=True) + EPS)
        o_ref[...] = (x * r * g_ref[...]).astype(BF16)

    return pl.pallas_call(
        body, name=name, grid=(s // ts,),
        in_specs=[pl.BlockSpec((ts, d), lambda i: (i, 0)), pl.BlockSpec((1, d), lambda i: (0, 0))],
        out_specs=pl.BlockSpec((ts, d), lambda i: (i, 0)),
        out_shape=jax.ShapeDtypeStruct((s, d), BF16),
        compiler_params=_params("parallel"),
    )(h, g.reshape(1, d))


def _rms_bwd(h, dxn, g, dh_in, name, ts=512):
    s, d = h.shape
    ts = _tile(s, ts, 8)

    def body(h_ref, dxn_ref, g_ref, dhin_ref, dh_ref, dg_ref):
        i = pl.program_id(0)
        x = h_ref[...]
        r = lax.rsqrt(jnp.mean(x * x, axis=-1, keepdims=True) + EPS)
        xh = x * r
        dxn_v = dxn_ref[...]

        @pl.when(i == 0)
        def _():
            dg_ref[...] = jnp.zeros_like(dg_ref)

        dg_ref[0:1, :] += jnp.sum(dxn_v * xh, axis=0, keepdims=True)
        dxh = dxn_v * g_ref[...]
        dx = r * (dxh - xh * jnp.mean(dxh * xh, axis=-1, keepdims=True))
        dh_ref[...] = dhin_ref[...] + dx

    row = pl.BlockSpec((ts, d), lambda i: (i, 0))
    dh, dg = pl.pallas_call(
        body, name=name, grid=(s // ts,),
        in_specs=[row, row, pl.BlockSpec((1, d), lambda i: (0, 0)), row],
        out_specs=[row, pl.BlockSpec((8, d), lambda i: (0, 0))],
        out_shape=[jax.ShapeDtypeStruct((s, d), F32), jax.ShapeDtypeStruct((8, d), F32)],
        compiler_params=_params("arbitrary"),
    )(h, dxn, g.reshape(1, d), dh_in)
    return dh, dg[0]


def _shift_down(x, prev):
    rows = lax.broadcasted_iota(jnp.int32, x.shape, 0)
    p1, p2 = prev[7:8, :], prev[6:7, :]
    x1 = jnp.where(rows == 0, p1, pltpu.roll(x, 1, 0))
    x2 = jnp.where(rows == 0, p2, jnp.where(rows == 1, p1, pltpu.roll(x, 2, 0)))
    return x1, x2


def _shift_up(x, nxt):
    n = x.shape[0]
    rows = lax.broadcasted_iota(jnp.int32, x.shape, 0)
    n0, n1 = nxt[0:1, :], nxt[1:2, :]
    x1 = jnp.where(rows == n - 1, n0, pltpu.roll(x, n - 1, 0))
    x2 = jnp.where(rows == n - 1, n1, jnp.where(rows == n - 2, n0, pltpu.roll(x, n - 2, 0)))
    return x1, x2


def _conv(x, x1, x2, w):
    return w[2:3, :] * x + w[1:2, :] * x1 + w[0:1, :] * x2


def _halo_specs(ts, tc, col, n_time_blocks):
    r8 = ts // 8
    main = pl.BlockSpec((ts, tc), lambda j, i: (i, j + col))
    prev = pl.BlockSpec((8, tc), lambda j, i: (jnp.maximum(i * r8 - 1, 0), j + col))
    nxt = pl.BlockSpec((8, tc), lambda j, i: (jnp.minimum((i + 1) * r8, n_time_blocks * r8 - 1), j + col))
    return main, prev, nxt


def _silu_parts(g):
    sig = 1.0 / (1.0 + jnp.exp(-g))
    return sig, g * sig


def _ffn_act_fwd(up, cw, name, ts=256, tc=1408):
    s = up.shape[0]
    ts, tc = _tile(s, ts, 8), _tile(D_FF, tc)
    nc, nt = D_FF // tc, s // ts

    def body(g_ref, gp_ref, v_ref, vp_ref, wg_ref, wv_ref, o_ref):
        first = pl.program_id(1) == 0

        def conv(x_ref, p_ref, w_ref):
            x = x_ref[...]
            prev = jnp.where(first, 0.0, p_ref[...])
            x1, x2 = _shift_down(x, prev)
            return _conv(x, x1, x2, w_ref[...])

        ug = conv(g_ref, gp_ref, wg_ref)
        uv = conv(v_ref, vp_ref, wv_ref)
        _, silu = _silu_parts(ug)
        o_ref[...] = (silu * uv).astype(BF16)

    g_main, g_prev, _ = _halo_specs(ts, tc, 0, nt)
    v_main, v_prev, _ = _halo_specs(ts, tc, nc, nt)
    return pl.pallas_call(
        body, name=name, grid=(nc, nt),
        in_specs=[g_main, g_prev, v_main, v_prev,
                  pl.BlockSpec((3, tc), lambda j, i: (0, j)), pl.BlockSpec((3, tc), lambda j, i: (0, j + nc))],
        out_specs=pl.BlockSpec((ts, tc), lambda j, i: (i, j)),
        out_shape=jax.ShapeDtypeStruct((s, D_FF), BF16),
        compiler_params=_params("parallel", "parallel"),
    )(up, up, up, up, cw, cw)


def _ffn_act_bwd(up, cw, da, name, ts=256, tc=1408):
    s = up.shape[0]
    ts, tc = _tile(s, ts, 8), _tile(D_FF, tc)
    nc, nt = D_FF // tc, s // ts

    def body(g_ref, gp_ref, gn_ref, v_ref, vp_ref, vn_ref, da_ref, dan_ref, wg_ref, wv_ref,
             dg_ref, dv_ref, dwg_ref, dwv_ref):
        i = pl.program_id(1)
        first, last = i == 0, i == nt - 1
        wg, wv = wg_ref[...], wv_ref[...]
        g, v = g_ref[...], v_ref[...]
        g1, g2 = _shift_down(g, jnp.where(first, 0.0, gp_ref[...]))
        v1, v2 = _shift_down(v, jnp.where(first, 0.0, vp_ref[...]))

        def d_u(ug, uv, da_v):
            sig, silu = _silu_parts(ug)
            return da_v * uv * (sig * (1.0 + ug * (1.0 - sig))), da_v * silu

        dug, duv = d_u(_conv(g, g1, g2, wg), _conv(v, v1, v2, wv), da_ref[...])
        gn, vn = gn_ref[...], vn_ref[...]
        gn1, gn2 = _shift_down(gn, g[ts - 8:, :])
        vn1, vn2 = _shift_down(vn, v[ts - 8:, :])
        dugn, duvn = d_u(_conv(gn, gn1, gn2, wg), _conv(vn, vn1, vn2, wv), dan_ref[...])
        dugn = jnp.where(last, 0.0, dugn)
        duvn = jnp.where(last, 0.0, duvn)

        def finish(du, dun, x, x1, x2, w, dx_ref, dw_ref):
            d1, d2 = _shift_up(du, dun)
            dx_ref[...] = (w[2:3, :] * du + w[1:2, :] * d1 + w[0:1, :] * d2).astype(BF16)

            @pl.when(first)
            def _():
                dw_ref[...] = jnp.zeros_like(dw_ref)

            dw_ref[0:1, :] += jnp.sum(du * x2, axis=0, keepdims=True)
            dw_ref[1:2, :] += jnp.sum(du * x1, axis=0, keepdims=True)
            dw_ref[2:3, :] += jnp.sum(du * x, axis=0, keepdims=True)

        finish(dug, dugn, g, g1, g2, wg, dg_ref, dwg_ref)
        finish(duv, duvn, v, v1, v2, wv, dv_ref, dwv_ref)

    g_specs = _halo_specs(ts, tc, 0, nt)
    v_specs = _halo_specs(ts, tc, nc, nt)
    da_main, _, da_next = _halo_specs(ts, tc, 0, nt)
    tile = pl.BlockSpec((ts, tc), lambda j, i: (i, j))
    taps = pl.BlockSpec((8, tc), lambda j, i: (0, j))
    dg, dv, dwg, dwv = pl.pallas_call(
        body, name=name, grid=(nc, nt),
        in_specs=[*g_specs, *v_specs, da_main, da_next,
                  pl.BlockSpec((3, tc), lambda j, i: (0, j)), pl.BlockSpec((3, tc), lambda j, i: (0, j + nc))],
        out_specs=[tile, tile, taps, taps],
        out_shape=[jax.ShapeDtypeStruct((s, D_FF), BF16), jax.ShapeDtypeStruct((s, D_FF), BF16),
                   jax.ShapeDtypeStruct((8, D_FF), F32), jax.ShapeDtypeStruct((8, D_FF), F32)],
        compiler_params=_params("parallel", "arbitrary"),
    )(up, up, up, up, up, up, da, da, cw, cw)
    return dg, dv, dwg[:3], dwv[:3]


def _sconv_fwd(proj, ck, name, ts=256, tc=512):
    s = proj.shape[0]
    w = D_MODEL
    ts, tc = _tile(s, ts, 8), _tile(w, tc)
    nc, nt = w // tc, s // ts

    def body(b_ref, c_ref, cp_ref, u_ref, up_ref, w_ref, o_ref):
        first = pl.program_id(1) == 0
        cu = c_ref[...] * u_ref[...]
        cup = jnp.where(first, 0.0, cp_ref[...] * up_ref[...])
        x1, x2 = _shift_down(cu, cup)
        o_ref[...] = (b_ref[...] * _conv(cu, x1, x2, w_ref[...])).astype(BF16)

    b_main, _, _ = _halo_specs(ts, tc, 0, nt)
    c_main, c_prev, _ = _halo_specs(ts, tc, nc, nt)
    u_main, u_prev, _ = _halo_specs(ts, tc, 2 * nc, nt)
    return pl.pallas_call(
        body, name=name, grid=(nc, nt),
        in_specs=[b_main, c_main, c_prev, u_main, u_prev, pl.BlockSpec((3, tc), lambda j, i: (0, j))],
        out_specs=pl.BlockSpec((ts, tc), lambda j, i: (i, j)),
        out_shape=jax.ShapeDtypeStruct((s, w), BF16),
        compiler_params=_params("parallel", "parallel"),
    )(proj, proj, proj, proj, proj, ck)


def _sconv_bwd(proj, ck, dy, name, ts=256, tc=512):
    s = proj.shape[0]
    w = D_MODEL
    ts, tc = _tile(s, ts, 8), _tile(w, tc)
    nc, nt = w // tc, s // ts

    def body(b_ref, bn_ref, c_ref, cp_ref, u_ref, up_ref, dy_ref, dyn_ref, w_ref,
             db_ref, dc_ref, du_ref, dw_ref):
        i = pl.program_id(1)
        first, last = i == 0, i == nt - 1
        wv = w_ref[...]
        b, c, u, dy_v = b_ref[...], c_ref[...], u_ref[...], dy_ref[...]
        cu = c * u
        cup = jnp.where(first, 0.0, cp_ref[...] * up_ref[...])
        x1, x2 = _shift_down(cu, cup)
        db_ref[...] = (dy_v * _conv(cu, x1, x2, wv)).astype(BF16)
        dcv = dy_v * b
        dcvn = jnp.where(last, 0.0, dyn_ref[...] * bn_ref[...])
        d1, d2 = _shift_up(dcv, dcvn)
        dcu = wv[2:3, :] * dcv + wv[1:2, :] * d1 + wv[0:1, :] * d2
        dc_ref[...] = (dcu * u).astype(BF16)
        du_ref[...] = (dcu * c).astype(BF16)

        @pl.when(first)
        def _():
            dw_ref[...] = jnp.zeros_like(dw_ref)

        dw_ref[0:1, :] += jnp.sum(dcv * x2, axis=0, keepdims=True)
        dw_ref[1:2, :] += jnp.sum(dcv * x1, axis=0, keepdims=True)
        dw_ref[2:3, :] += jnp.sum(dcv * cu, axis=0, keepdims=True)

    b_main, _, b_next = _halo_specs(ts, tc, 0, nt)
    c_main, c_prev, _ = _halo_specs(ts, tc, nc, nt)
    u_main, u_prev, _ = _halo_specs(ts, tc, 2 * nc, nt)
    dy_main, _, dy_next = _halo_specs(ts, tc, 0, nt)
    tile = pl.BlockSpec((ts, tc), lambda j, i: (i, j))
    out = jax.ShapeDtypeStruct((s, w), BF16)
    db, dc, du, dw = pl.pallas_call(
        body, name=name, grid=(nc, nt),
        in_specs=[b_main, b_next, c_main, c_prev, u_main, u_prev, dy_main, dy_next,
                  pl.BlockSpec((3, tc), lambda j, i: (0, j))],
        out_specs=[tile, tile, tile, pl.BlockSpec((8, tc), lambda j, i: (0, j))],
        out_shape=[out, out, out, jax.ShapeDtypeStruct((8, w), F32)],
        compiler_params=_params("parallel", "arbitrary"),
    )(proj, proj, proj, proj, proj, proj, dy, dy, ck)
    return db, dc, du, dw[:3]


def _low_lanes(shape):
    return lax.broadcasted_iota(jnp.int32, shape, 1) < HEAD_DIM


def _top_rows(shape):
    return lax.broadcasted_iota(jnp.int32, shape, 0) < HEAD_DIM


def _norm_pair(x):
    lo = _low_lanes(x.shape)
    sq = x * x
    s_lo = jnp.sum(jnp.where(lo, sq, 0.0), axis=1, keepdims=True)
    s_hi = jnp.sum(jnp.where(lo, 0.0, sq), axis=1, keepdims=True)
    r = lax.rsqrt(jnp.where(lo, s_lo, s_hi) * (1.0 / HEAD_DIM) + EPS)
    return x * r, r


def _mean_pair(x):
    lo = _low_lanes(x.shape)
    s_lo = jnp.sum(jnp.where(lo, x, 0.0), axis=1, keepdims=True)
    s_hi = jnp.sum(jnp.where(lo, 0.0, x), axis=1, keepdims=True)
    return jnp.where(lo, s_lo, s_hi) * (1.0 / HEAD_DIM)


def _qkv_prep(proj, gq, gk, name, ts=512):
    s = proj.shape[0]
    ts = _tile(s, ts)
    npair = MIX // LANES
    scale = HEAD_DIM ** -0.5

    def body(q_ref, k_ref, v_ref, gq_ref, gk_ref, qo_ref, ko_ref, vo_ref, qt_ref, kt_ref, vt_ref):
        qn, _ = _norm_pair(q_ref[...])
        kn, _ = _norm_pair(k_ref[...])
        q = qn * gq_ref[0] * scale
        k = kn * gk_ref[0]
        v = v_ref[...]
        qo_ref[...] = q.astype(BF16)
        ko_ref[...] = k.astype(BF16)
        vo_ref[...] = v.astype(BF16)
        qt_ref[...] = q.T.astype(BF16)
        kt_ref[...] = k.T.astype(BF16)
        vt_ref[...] = v.T.astype(BF16)

    gain = pl.BlockSpec((1, 1, LANES), lambda i, p: (p, 0, 0))
    tile = pl.BlockSpec((ts, LANES), lambda i, p: (i, p))
    tile_t = pl.BlockSpec((LANES, ts), lambda i, p: (p, i))
    out = jax.ShapeDtypeStruct((s, MIX), BF16)
    out_t = jax.ShapeDtypeStruct((MIX, s), BF16)
    return pl.pallas_call(
        body, name=name, grid=(s // ts, npair),
        in_specs=[tile, pl.BlockSpec((ts, LANES), lambda i, p: (i, p + npair)),
                  pl.BlockSpec((ts, LANES), lambda i, p: (i, p + 2 * npair)), gain, gain],
        out_specs=[tile, tile, tile, tile_t, tile_t, tile_t], out_shape=[out, out, out, out_t, out_t, out_t],
        compiler_params=_params("parallel", "parallel"),
    )(proj, proj, proj, gq, gk)


def _qkv_prep_bwd(proj, gq, gk, dqs, dks, dvs, name, ts=512):
    s = proj.shape[0]
    ts = _tile(s, ts, 8)
    npair = MIX // LANES
    half = npair // 2
    scale = HEAD_DIM ** -0.5

    def body(q_ref, k_ref, gq_ref, gk_ref, dqf_ref, dqs_ref, dkf_ref, dks_ref, dvf_ref, dvs_ref,
             dq_ref, dk_ref, dv_ref, dgq_ref, dgk_ref):
        p, i = pl.program_id(0), pl.program_id(1)
        fox = p < half

        def one(x_ref, g_ref, df_ref, ds_ref, dx_ref, dg_ref, mult):
            dn = jnp.where(fox, df_ref[...], ds_ref[...]) * mult
            xh, r = _norm_pair(x_ref[...])

            @pl.when(i == 0)
            def _():
                dg_ref[...] = jnp.zeros_like(dg_ref)

            dg_ref[0, 0:1, :] += jnp.sum(dn * xh, axis=0, keepdims=True)
            dxh = dn * g_ref[0]
            dx_ref[...] = (r * (dxh - xh * _mean_pair(dxh * xh))).astype(BF16)

        one(q_ref, gq_ref, dqf_ref, dqs_ref, dq_ref, dgq_ref, scale)
        one(k_ref, gk_ref, dkf_ref, dks_ref, dk_ref, dgk_ref, 1.0)
        dv_ref[...] = jnp.where(fox, dvf_ref[...], dvs_ref[...]).astype(BF16)

    gain = pl.BlockSpec((1, 1, LANES), lambda p, i: (p, 0, 0))
    tile = pl.BlockSpec((ts, LANES), lambda p, i: (i, p))
    fpart = pl.BlockSpec((ts, LANES), lambda p, i: (i, jnp.minimum(p, half - 1)))
    spart = pl.BlockSpec((ts, LANES), lambda p, i: (i, jnp.maximum(p - half, 0)))
    dgain = pl.BlockSpec((1, 8, LANES), lambda p, i: (p, 0, 0))
    out = jax.ShapeDtypeStruct((s, MIX), BF16)
    gshape = jax.ShapeDtypeStruct((npair, 8, LANES), F32)
    dq, dk, dv, dgq, dgk = pl.pallas_call(
        body, name=name, grid=(npair, s // ts),
        in_specs=[tile, pl.BlockSpec((ts, LANES), lambda p, i: (i, p + npair)), gain, gain,
                  fpart, spart, fpart, spart, fpart, spart],
        out_specs=[tile, tile, tile, dgain, dgain], out_shape=[out, out, out, gshape, gshape],
        compiler_params=_params("parallel", "arbitrary"),
    )(proj, proj, gq, gk, dqs[0], dqs[1], dks[0], dks[1], dvs[0], dvs[1])
    return dq, dk, dv, dgq[:, 0, :], dgk[:, 0, :]


def _tri(n, rel):
    a = lax.broadcasted_iota(jnp.int32, (n, n), 0)
    b = lax.broadcasted_iota(jnp.int32, (n, n), 1)
    return rel(a, b).astype(BF16)


def _fgate_fwd(logit, bias, name):
    nh, r, _ = logit.shape

    def body(x_ref, b_ref, o_ref):
        within = _tri(LANES, lambda a, b: a <= b)
        before = _tri(r, lambda a, b: b < a)
        for hh in range(nh):
            x = x_ref[hh] + b_ref[hh]
            lf = jnp.minimum(x, 0.0) - jnp.log1p(jnp.exp(-jnp.abs(x)))
            c = _split_dot(lf, within, 3)
            tot = jnp.broadcast_to(c[:, LANES - 1:LANES], (r, LANES))
            o_ref[hh] = c + _split_dot_left(before, tot, 3)

    return pl.pallas_call(
        body, name=name, out_shape=jax.ShapeDtypeStruct((nh, r, LANES), F32),
        in_specs=[pl.BlockSpec(memory_space=pltpu.VMEM), pl.BlockSpec(memory_space=pltpu.SMEM)],
        out_specs=pl.BlockSpec(memory_space=pltpu.VMEM),
    )(logit, bias)


def _fgate_bwd(logit, bias, dcum, name):
    nh, r, _ = logit.shape

    def body(x_ref, b_ref, d_ref, dx_ref, db_ref):
        within = _tri(LANES, lambda a, b: a >= b)
        after = _tri(r, lambda a, b: b > a)
        for hh in range(nh):
            x = x_ref[hh] + b_ref[hh]
            d = d_ref[hh]
            c = _split_dot(d, within, 3)
            tot = jnp.broadcast_to(c[:, 0:1], (r, LANES))
            dlf = c + _split_dot_left(after, tot, 3)
            dx = dlf * (1.0 / (1.0 + jnp.exp(x)))
            dx_ref[hh] = dx
            db_ref[hh:hh + 1, :] = jnp.broadcast_to(jnp.sum(dx, keepdims=True).reshape(1, 1), (1, LANES))

    return pl.pallas_call(
        body, name=name,
        out_shape=[jax.ShapeDtypeStruct((nh, r, LANES), F32), jax.ShapeDtypeStruct((nh, LANES), F32)],
        in_specs=[pl.BlockSpec(memory_space=pltpu.VMEM), pl.BlockSpec(memory_space=pltpu.SMEM),
                  pl.BlockSpec(memory_space=pltpu.VMEM)],
        out_specs=[pl.BlockSpec(memory_space=pltpu.VMEM), pl.BlockSpec(memory_space=pltpu.VMEM)],
    )(logit, bias, dcum)


def _pair_masks(x):
    lo = _low_lanes(x.shape)
    zero = jnp.zeros_like(x)
    return jnp.where(lo, x, zero), jnp.where(lo, zero, x)


def _pair_masks_t(x):
    top = _top_rows(x.shape)
    zero = jnp.zeros_like(x)
    return jnp.where(top, x, zero), jnp.where(top, zero, x)


def _stack_heads(x):
    return jnp.concatenate(_pair_masks(x), axis=0)


def _stack_heads_t(x):
    return jnp.concatenate(_pair_masks_t(x), axis=1)


def _pair_colsum_t(x):
    top = _top_rows(x.shape)
    return (jnp.sum(jnp.where(top, x, 0.0), axis=0, keepdims=True),
            jnp.sum(jnp.where(top, 0.0, x), axis=0, keepdims=True))


def _key_query_iotas(t):
    return lax.broadcasted_iota(jnp.int32, (t, t), 0), lax.broadcasted_iota(jnp.int32, (t, t), 1)


def _walk_blocks(i, step, descending, group=2):
    full = i // group
    left = i - full * group

    def run(first, count):
        sign = -1 if descending else 1
        step([(first + sign * n, False) for n in range(count)])

    def leftovers():
        start = (left - 1) if descending else full * group
        sign = -1 if descending else 1
        if group == 4:
            @pl.when(left >= 2)
            def _():
                run(start, 2)

            @pl.when(left % 2 == 1)
            def _():
                run(0 if descending else i - 1, 1)
        else:
            @pl.when(left == 1)
            def _():
                run(start, 1)

    def loop(g, carry):
        run((i - 1 - group * g) if descending else group * g, group)
        return carry

    if descending:
        step([(i, True)])
        lax.fori_loop(0, full, loop, 0)
        leftovers()
    else:
        lax.fori_loop(0, full, loop, 0)
        leftovers()
        step([(i, True)])


def _attn_specs(s, tq, pair0):
    q_nat = pl.BlockSpec((tq, LANES), lambda p, i: (i, p + pair0))
    q_t = pl.BlockSpec((LANES, tq), lambda p, i: (p + pair0, i))
    k_nat = pl.BlockSpec((s, LANES), lambda p, i: (0, p + pair0))
    k_t = pl.BlockSpec((LANES, s), lambda p, i: (p + pair0, 0))
    return q_nat, q_t, k_nat, k_t


def _fox_fwd(qt, kh, vt, frow, fcol, name, tq=256):
    s = kh.shape[0]
    tq = _tile(s, tq)
    nq, half = s // tq, MIX // LANES // 2

    def body(qt_ref, k_ref, vt_ref, fr_ref, fc_ref, o_ref, lse_ref, m_s, l_s, acc_s):
        i = pl.program_id(1)
        qt_v = qt_ref[...]
        ft = fr_ref[0]
        key, qry = _key_query_iotas(tq)
        causal = key <= qry
        m_s[...] = jnp.full(m_s.shape, NEG, F32)
        l_s[...] = jnp.zeros_like(l_s)
        acc_s[...] = jnp.zeros_like(acc_s)

        top = _top_rows((LANES, tq))

        def step(blocks):
            rows = [pl.ds(pl.multiple_of(j * tq, tq), tq) for j, _ in blocks]
            zs = [_dot(_stack_heads(k_ref[r, :]), qt_v) for r in rows]
            m_cur, l_cur = [m_s[0], m_s[1]], [l_s[0], l_s[1]]
            acc = acc_s[...]
            for b, (_, masked) in enumerate(blocks):
                fk = fc_ref[0, rows[b], :]
                prs, alphas = [], []
                for hh in range(2):
                    sc = zs[b][hh * tq:(hh + 1) * tq] + (ft[hh:hh + 1, :] - fk[:, hh:hh + 1])
                    if masked:
                        sc = jnp.where(causal, sc, NEG)
                    m_new = jnp.maximum(m_cur[hh], jnp.max(sc, axis=0, keepdims=True))
                    alpha = jnp.exp(m_cur[hh] - m_new)
                    pr = jnp.exp(sc - m_new)
                    l_cur[hh] = alpha * l_cur[hh] + jnp.sum(pr, axis=0, keepdims=True)
                    m_cur[hh] = m_new
                    prs.append(pr.astype(BF16))
                    alphas.append(alpha)
                pv = _dot(_stack_heads_t(vt_ref[:, rows[b]]), jnp.concatenate(prs, axis=0))
                acc = jnp.where(top, alphas[0], alphas[1]) * acc + pv
            acc_s[...] = acc
            for hh in range(2):
                m_s[hh] = m_cur[hh]
                l_s[hh] = l_cur[hh]

        _walk_blocks(i, step, descending=False, group=4)
        o_ref[...] = (acc_s[...] / jnp.where(top, l_s[0], l_s[1])).T
        lse_ref[0, 0:1, :] = m_s[0] + jnp.log(l_s[0])
        lse_ref[0, 1:2, :] = m_s[1] + jnp.log(l_s[1])

    _, q_t, k_nat, k_t = _attn_specs(s, tq, 0)
    qstat = pl.BlockSpec((1, 2, tq), lambda p, i: (p, 0, i))
    return pl.pallas_call(
        body, name=name, grid=(half, nq),
        in_specs=[q_t, k_nat, k_t, qstat, pl.BlockSpec((1, s, 2), lambda p, i: (p, 0, 0))],
        out_specs=[pl.BlockSpec((tq, LANES), lambda p, i: (i, p)), qstat],
        out_shape=[jax.ShapeDtypeStruct((s, MIX // 2), F32), jax.ShapeDtypeStruct((half, 2, s), F32)],
        scratch_shapes=[pltpu.VMEM((2, 1, tq), F32), pltpu.VMEM((2, 1, tq), F32), pltpu.VMEM((LANES, tq), F32)],
        compiler_params=_params("parallel", "arbitrary"),
    )(qt, kh, vt, frow, fcol)


def _fox_bwd(qh, qt, kh, kt, vb, frow, fcol, lse, o, do, name, tq=256):
    s = kh.shape[0]
    tq = _tile(s, tq)
    nq, half = s // tq, MIX // LANES // 2

    def body(q_ref, qt_ref, k_ref, kt_ref, v_ref, fr_ref, fc_ref, lse_ref, o_ref, do_ref,
             dq_ref, dk_ref, dv_ref, dfk_ref, dfq_ref, dq_s, rs_s):
        i = pl.program_id(1)

        @pl.when(i == 0)
        def _():
            dk_ref[...] = jnp.zeros_like(dk_ref)
            dv_ref[...] = jnp.zeros_like(dv_ref)
            dfk_ref[...] = jnp.zeros_like(dfk_ref)

        q2 = _stack_heads(q_ref[...])
        qt_v = qt_ref[...]
        do_v = do_ref[...]
        do2 = _stack_heads(do_v.astype(BF16))
        dot_v = do_v.T.astype(BF16)
        dsum = _pair_colsum_t((do_v * o_ref[...]).T)
        ft, ls = fr_ref[0], lse_ref[0]
        key, qry = _key_query_iotas(tq)
        causal = key <= qry
        lane = lax.broadcasted_iota(jnp.int32, (2 * tq, LANES), 0) // tq
        pick2 = (lax.broadcasted_iota(jnp.int32, (2 * tq, LANES), 1) == lane).astype(BF16)
        q2_pick = jnp.concatenate([q2, pick2], axis=1)
        dq_s[...] = jnp.zeros_like(dq_s)
        rs_s[...] = jnp.zeros_like(rs_s)

        def step(blocks):
            rows = [pl.ds(pl.multiple_of(j * tq, tq), tq) for j, _ in blocks]
            zs = [_dot(_stack_heads(k_ref[r, :]), qt_v) for r in rows]
            dps = [_dot(_stack_heads(v_ref[r, :]), dot_v) for r in rows]
            rs = [rs_s[0], rs_s[1]]
            dq = None
            for b, (_, masked) in enumerate(blocks):
                fk = fc_ref[0, rows[b], :]
                prs, dss = [], []
                for hh in range(2):
                    blk = slice(hh * tq, (hh + 1) * tq)
                    sc = zs[b][blk] + (ft[hh:hh + 1, :] - fk[:, hh:hh + 1])
                    pr = jnp.exp(sc - ls[hh:hh + 1, :])
                    if masked:
                        pr = jnp.where(causal, pr, 0.0)
                    dsb = (pr * (dps[b][blk] - dsum[hh])).astype(BF16)
                    rs[hh] = rs[hh] + jnp.sum(dsb.astype(F32), axis=0, keepdims=True)
                    prs.append(pr.astype(BF16))
                    dss.append(dsb)
                dv_ref[rows[b], :] += _dot(jnp.concatenate(prs, axis=1), do2)
                both = _dot(jnp.concatenate(dss, axis=1), q2_pick)
                dk_ref[rows[b], :] += both[:, :LANES]
                dfk_ref[0, rows[b], :] -= both[:, LANES:]
                term = _dot(_stack_heads_t(kt_ref[:, rows[b]]), jnp.concatenate(dss, axis=0))
                dq = term if dq is None else dq + term
            rs_s[0], rs_s[1] = rs
            dq_s[...] += dq

        _walk_blocks(i, step, descending=False, group=2)
        dq_ref[...] = dq_s[...].T
        dfq_ref[0, 0:1, :] = rs_s[0]
        dfq_ref[0, 1:2, :] = rs_s[1]

    q_nat, q_t, k_nat, k_t = _attn_specs(s, tq, 0)
    qstat = pl.BlockSpec((1, 2, tq), lambda p, i: (p, 0, i))
    otile = pl.BlockSpec((tq, LANES), lambda p, i: (i, p))
    oresident = pl.BlockSpec((s, LANES), lambda p, i: (0, p))
    out = jax.ShapeDtypeStruct((s, MIX // 2), F32)
    return pl.pallas_call(
        body, name=name, grid=(half, nq),
        in_specs=[q_nat, q_t, k_nat, k_t, k_nat, qstat, pl.BlockSpec((1, s, 2), lambda p, i: (p, 0, 0)), qstat,
                  otile, q_nat],
        out_specs=[otile, oresident, oresident, pl.BlockSpec((1, s, LANES), lambda p, i: (p, 0, 0)), qstat],
        out_shape=[out, out, out, jax.ShapeDtypeStruct((half, s, LANES), F32),
                   jax.ShapeDtypeStruct((half, 2, s), F32)],
        scratch_shapes=[pltpu.VMEM((LANES, tq), F32), pltpu.VMEM((2, 1, tq), F32)],
        compiler_params=_params("parallel", "arbitrary"),
    )(qh, qt, kh, kt, vb, frow, fcol, lse, o, do)


def _log_sig_pair(z):
    lb = jnp.minimum(z, 0.0) - jnp.log(1.0 + jnp.exp(-jnp.abs(z)))
    return lb, lb - z


def _sb_fwd(qt, kh, vt, name, tq=256):
    s = kh.shape[0]
    tq = _tile(s, tq)
    nq, half = s // tq, MIX // LANES // 2

    def body(qt_ref, k_ref, vt_ref, o_ref, tot_ref, c_s, acc_s):
        i = pl.program_id(1)
        qt_v = qt_ref[...]
        key, qry = _key_query_iotas(tq)
        strict = key < qry
        later = _tri(tq, lambda a, b: b > a)
        c_s[...] = jnp.zeros_like(c_s)
        acc_s[...] = jnp.zeros_like(acc_s)

        def step(blocks):
            rows = [pl.ds(pl.multiple_of(j * tq, tq), tq) for j, _ in blocks]
            zs = [_dot(_stack_heads(k_ref[r, :]), qt_v) for r in rows]
            lbs, loms, afters = [], [], []
            for b, (_, masked) in enumerate(blocks):
                for hh in range(2):
                    lb, lom = _log_sig_pair(zs[b][hh * tq:(hh + 1) * tq])
                    if masked:
                        lom = jnp.where(strict, lom, 0.0)
                    lbs.append(lb)
                    loms.append(lom)
                afters.append(_split_dot_left(later, jnp.concatenate(loms[2 * b:2 * b + 2], axis=1), 2))
            carry = [c_s[0], c_s[1]]
            pv = None
            for b, (_, masked) in enumerate(blocks):
                ws = []
                for hh in range(2):
                    n = 2 * b + hh
                    w = jnp.exp(lbs[n] + afters[b][:, hh * tq:(hh + 1) * tq] + carry[hh])
                    if masked:
                        w = jnp.where(strict, w, 0.0)
                    ws.append(w.astype(BF16))
                    carry[hh] = carry[hh] + jnp.sum(loms[n], axis=0, keepdims=True)
                term = _dot(_stack_heads_t(vt_ref[:, rows[b]]), jnp.concatenate(ws, axis=0))
                pv = term if pv is None else pv + term
            c_s[0], c_s[1] = carry
            acc_s[...] += pv

        _walk_blocks(i, step, descending=True, group=4)
        o_ref[...] = acc_s[...].T
        tot_ref[0, 0:1, :] = c_s[0]
        tot_ref[0, 1:2, :] = c_s[1]

    _, q_t, k_nat, k_t = _attn_specs(s, tq, half)
    qstat = pl.BlockSpec((1, 2, tq), lambda p, i: (p, 0, i))
    return pl.pallas_call(
        body, name=name, grid=(half, nq),
        in_specs=[q_t, k_nat, k_t],
        out_specs=[pl.BlockSpec((tq, LANES), lambda p, i: (i, p)), qstat],
        out_shape=[jax.ShapeDtypeStruct((s, MIX // 2), F32), jax.ShapeDtypeStruct((half, 2, s), F32)],
        scratch_shapes=[pltpu.VMEM((2, 1, tq), F32), pltpu.VMEM((LANES, tq), F32)],
        compiler_params=_params("parallel", "arbitrary"),
    )(qt, kh, vt)


def _sb_bwd(qh, qt, kh, kt, vb, tot, do, name, tq=256):
    s = kh.shape[0]
    tq = _tile(s, tq)
    nq, half = s // tq, MIX // LANES // 2

    def body(q_ref, qt_ref, k_ref, kt_ref, v_ref, tot_ref, do_ref, dq_ref, dk_ref, dv_ref, rem_s, pg_s, dq_s):
        i = pl.program_id(1)

        @pl.when(i == 0)
        def _():
            dk_ref[...] = jnp.zeros_like(dk_ref)
            dv_ref[...] = jnp.zeros_like(dv_ref)

        q2 = _stack_heads(q_ref[...])
        qt_v = qt_ref[...]
        do_v = do_ref[...]
        do2 = _stack_heads(do_v.astype(BF16))
        dot_v = do_v.T.astype(BF16)
        key, qry = _key_query_iotas(tq)
        strict = key < qry
        upto = _tri(tq, lambda a, b: b <= a)
        before = _tri(tq, lambda a, b: b < a)
        tv = tot_ref[0]
        rem_s[0] = tv[0:1, :]
        rem_s[1] = tv[1:2, :]
        pg_s[...] = jnp.zeros_like(pg_s)
        dq_s[...] = jnp.zeros_like(dq_s)

        def step(blocks):
            nb = len(blocks)
            rows = [pl.ds(pl.multiple_of(j * tq, tq), tq) for j, _ in blocks]
            zs = [_dot(_stack_heads(k_ref[r, :]), qt_v) for r in rows]
            dws = [_dot(_stack_heads(v_ref[r, :]), dot_v) for r in rows]
            lbs, loms, prefixes = [], [], []
            for b, (_, masked) in enumerate(blocks):
                for hh in range(2):
                    lb, lom = _log_sig_pair(zs[b][hh * tq:(hh + 1) * tq])
                    if masked:
                        lom = jnp.where(strict, lom, 0.0)
                    lbs.append(lb)
                    loms.append(lom)
                prefixes.append(_split_dot_left(upto, jnp.concatenate(loms[2 * b:2 * b + 2], axis=1), 2))
            rem = [rem_s[0], rem_s[1]]
            ws, gs, gpres = [], [], []
            for b, (_, masked) in enumerate(blocks):
                for hh in range(2):
                    n = 2 * b + hh
                    blk = slice(hh * tq, (hh + 1) * tq)
                    w = jnp.exp(lbs[n] + (rem[hh] - prefixes[b][:, blk]))
                    if masked:
                        w = jnp.where(strict, w, 0.0)
                    gs.append(dws[b][blk] * w)
                    ws.append(w.astype(BF16))
                    rem[hh] = rem[hh] - jnp.sum(loms[n], axis=0, keepdims=True)
                gpres.append(_dot(before, jnp.concatenate(gs[2 * b:2 * b + 2], axis=1).astype(BF16)))
                dv_ref[rows[b], :] += _dot(jnp.concatenate(ws[2 * b:2 * b + 2], axis=1), do2)
            rem_s[0], rem_s[1] = rem
            pg = [pg_s[0], pg_s[1]]
            dq = None
            for b, (_, masked) in enumerate(blocks):
                dzs = []
                for hh in range(2):
                    n = 2 * b + hh
                    g = gs[n]
                    dz = g - jnp.exp(lbs[n]) * (g + (pg[hh] + gpres[b][:, hh * tq:(hh + 1) * tq]))
                    if masked:
                        dz = jnp.where(strict, dz, 0.0)
                    dzs.append(dz.astype(BF16))
                    pg[hh] = pg[hh] + jnp.sum(g, axis=0, keepdims=True)
                dk_ref[rows[b], :] += _dot(jnp.concatenate(dzs, axis=1), q2)
                term = _dot(_stack_heads_t(kt_ref[:, rows[b]]), jnp.concatenate(dzs, axis=0))
                dq = term if dq is None else dq + term
            pg_s[0], pg_s[1] = pg
            dq_s[...] += dq

        _walk_blocks(i, step, descending=False, group=2)
        dq_ref[...] = dq_s[...].T

    q_nat, q_t, k_nat, k_t = _attn_specs(s, tq, half)
    qstat = pl.BlockSpec((1, 2, tq), lambda p, i: (p, 0, i))
    otile = pl.BlockSpec((tq, LANES), lambda p, i: (i, p))
    oresident = pl.BlockSpec((s, LANES), lambda p, i: (0, p))
    out = jax.ShapeDtypeStruct((s, MIX // 2), F32)
    return pl.pallas_call(
        body, name=name, grid=(half, nq),
        in_specs=[q_nat, q_t, k_nat, k_t, k_nat, qstat, q_nat],
        out_specs=[otile, oresident, oresident], out_shape=[out, out, out],
        scratch_shapes=[pltpu.VMEM((2, 1, tq), F32), pltpu.VMEM((2, 1, tq), F32), pltpu.VMEM((LANES, tq), F32)],
        compiler_params=_params("parallel", "arbitrary"),
    )(qh, qt, kh, kt, vb, tot, do)


def _loss_head(y, target, name, ts=512):
    s, d = y.shape
    ts = _tile(s, ts, 8)
    nt = s // ts

    def body(y_ref, t_ref, dy_ref, l_ref, acc):
        i = pl.program_id(0)
        err = y_ref[...] - t_ref[...]
        dy_ref[...] = err * (1.0 / d)

        @pl.when(i == 0)
        def _():
            acc[...] = jnp.zeros_like(acc)

        acc[...] += jnp.sum(err * err, axis=0, keepdims=True)

        @pl.when(i == nt - 1)
        def _():
            tot = jnp.sum(acc[...], keepdims=True).reshape(1, 1) * (0.5 / d)
            l_ref[...] = jnp.broadcast_to(tot, l_ref.shape)

    row = pl.BlockSpec((ts, d), lambda i: (i, 0))
    dy, l = pl.pallas_call(
        body, name=name, grid=(nt,), in_specs=[row, row],
        out_specs=[row, pl.BlockSpec((8, LANES), lambda i: (0, 0))],
        out_shape=[jax.ShapeDtypeStruct((s, d), F32), jax.ShapeDtypeStruct((8, LANES), F32)],
        scratch_shapes=[pltpu.VMEM((1, d), F32)],
        compiler_params=_params("arbitrary"),
    )(y, target)
    return l[0, 0], dy


def _coords():
    return lax.axis_index("x"), lax.axis_index("y"), lax.axis_index("c")


def _other_chips(xi, yi):
    return [(1 - xi, yi), (xi, 1 - yi), (1 - xi, 1 - yi)]


def _all_gather(xs, name):
    n = len(xs)

    def body(*refs):
        x_refs, out_refs = refs[:n], refs[n:2 * n]
        send_sems, recv_sems, local_sems = refs[2 * n:]
        xi, yi, ci = _coords()
        me, sibling = (xi, yi, ci), (xi, yi, 1 - ci)
        chips = _other_chips(xi, yi)

        def slot(a, px, py, pc):
            return out_refs[a].at[4 * px + 2 * py + pc]

        def copy(a, k, block, to, src=None):
            return pltpu.make_async_remote_copy(
                src_ref=slot(a, *block) if src is None else src, dst_ref=slot(a, *block),
                send_sem=send_sems.at[a, k], recv_sem=recv_sems.at[a, k], device_id=to, device_id_type=MESH)

        mine = [pltpu.make_async_copy(x_refs[a], slot(a, *me), local_sems.at[a]) for a in range(n)]
        for cp in mine:
            cp.start()
        first = []
        for a in range(n):
            first.append(copy(a, 0, me, sibling, src=x_refs[a]))
            first += [copy(a, 1 + j, me, (*chip, ci), src=x_refs[a]) for j, chip in enumerate(chips)]
        for cp in first:
            cp.start()
        passed = []
        for j, chip in enumerate(chips):
            for a in range(n):
                copy(a, 1 + j, (*chip, ci), me).wait_recv()
                fwd = copy(a, 4 + j, (*chip, ci), sibling)
                fwd.start()
                passed.append(fwd)
        for a in range(n):
            copy(a, 0, sibling, me).wait_recv()
            for j, chip in enumerate(chips):
                copy(a, 4 + j, (*chip, 1 - ci), me).wait_recv()
        for cp in first + passed:
            cp.wait_send()
        for cp in mine:
            cp.wait()

    return pl.pallas_call(
        body, name=name, out_shape=[jax.ShapeDtypeStruct((N_DEV,) + x.shape, x.dtype) for x in xs],
        in_specs=[ANY] * n, out_specs=[ANY] * n,
        scratch_shapes=[pltpu.SemaphoreType.DMA((n, 7)), pltpu.SemaphoreType.DMA((n, 7)),
                        pltpu.SemaphoreType.DMA((n,))],
    )(*xs)


def _sibling_exchange(gs, name):
    n = len(gs)

    def body(*refs):
        g_refs, recv_refs = refs[:n], refs[n:2 * n]
        send_sems, recv_sems = refs[2 * n:]
        xi, yi, ci = _coords()
        cps = [pltpu.make_async_remote_copy(
            src_ref=g_refs[a].at[2 * chip + (1 - ci)], dst_ref=recv_refs[a].at[chip],
            send_sem=send_sems.at[a, chip], recv_sem=recv_sems.at[a, chip],
            device_id=(xi, yi, 1 - ci), device_id_type=MESH) for a in range(n) for chip in range(N_CHIP)]
        for cp in cps:
            cp.start()
        for cp in cps:
            cp.wait()

    return pl.pallas_call(
        body, name=name, out_shape=[jax.ShapeDtypeStruct((N_CHIP,) + g.shape[1:], g.dtype) for g in gs],
        in_specs=[ANY] * n, out_specs=[ANY] * n,
        scratch_shapes=[pltpu.SemaphoreType.DMA((n, N_CHIP)), pltpu.SemaphoreType.DMA((n, N_CHIP))],
    )(*gs)


def _pair_add(g, recv, ids, name, tr=256):
    _, r, c = g.shape
    tr = _tile(r, tr, 16)

    def body(ids_ref, g_ref, r_ref, p_ref, own_ref):
        kk = pl.program_id(1)
        tot = g_ref[0].astype(F32) + r_ref[0].astype(F32)
        p_ref[0] = tot.astype(BF16)

        @pl.when(kk == ids_ref[1])
        def _():
            own_ref[...] = tot

    grid_spec = pltpu.PrefetchScalarGridSpec(
        num_scalar_prefetch=1, grid=(r // tr, N_CHIP),
        in_specs=[pl.BlockSpec((1, tr, c), lambda i, kk, ids: (2 * kk + ids[0], i, 0)),
                  pl.BlockSpec((1, tr, c), lambda i, kk, ids: (kk, i, 0))],
        out_specs=[pl.BlockSpec((1, tr, c), lambda i, kk, ids: (kk, i, 0)),
                   pl.BlockSpec((tr, c), lambda i, kk, ids: (i, 0))])
    return pl.pallas_call(
        body, name=name, grid_spec=grid_spec,
        out_shape=[jax.ShapeDtypeStruct((N_CHIP, r, c), BF16), jax.ShapeDtypeStruct((r, c), F32)],
        compiler_params=_params("parallel", "arbitrary"),
    )(ids, g, recv)


def _chip_exchange(ps, name):
    n = len(ps)

    def body(*refs):
        p_refs, recv_refs = refs[:n], refs[n:2 * n]
        send_sems, recv_sems = refs[2 * n:]
        xi, yi, ci = _coords()
        mine = 2 * xi + yi
        chips = _other_chips(xi, yi)

        def copy(a, k, cx, cy):
            return pltpu.make_async_remote_copy(
                src_ref=p_refs[a].at[2 * cx + cy], dst_ref=recv_refs[a].at[mine],
                send_sem=send_sems.at[a, k], recv_sem=recv_sems.at[a, k],
                device_id=(cx, cy, ci), device_id_type=MESH)

        def landed(a, k, cx, cy):
            return pltpu.make_async_remote_copy(
                src_ref=p_refs[a].at[mine], dst_ref=recv_refs[a].at[2 * cx + cy],
                send_sem=send_sems.at[a, k], recv_sem=recv_sems.at[a, k],
                device_id=(cx, cy, ci), device_id_type=MESH)

        sends = [copy(a, k, cx, cy) for a in range(n) for k, (cx, cy) in enumerate(chips)]
        for cp in sends:
            cp.start()
        for a in range(n):
            for k, (cx, cy) in enumerate(chips):
                landed(a, k, cx, cy).wait_recv()
        for cp in sends:
            cp.wait_send()

    return pl.pallas_call(
        body, name=name, out_shape=[jax.ShapeDtypeStruct(p.shape, p.dtype) for p in ps],
        in_specs=[ANY] * n, out_specs=[ANY] * n,
        scratch_shapes=[pltpu.SemaphoreType.DMA((n, 3)), pltpu.SemaphoreType.DMA((n, 3))],
    )(*ps)


def _adamw_math(w, g, m, v):
    m = ADAM_B1 * m + (1.0 - ADAM_B1) * g
    v = ADAM_B2 * v + (1.0 - ADAM_B2) * (g * g)
    m_hat = m / (1.0 - ADAM_B1 ** ADAM_STEP)
    v_hat = v / (1.0 - ADAM_B2 ** ADAM_STEP)
    delta = -ADAM_LR * (m_hat / (jnp.sqrt(v_hat) + ADAM_EPS) + ADAM_WD * w)
    return delta, m, v


def _adamw_reduce(own, recv, ids, w, m, v, name, tr=256):
    r, c = w.shape
    tr = _tile(r, tr, 16)

    def body(ids_ref, own_ref, recv_ref, w_ref, m_ref, v_ref, g_ref, d_ref, mo_ref, vo_ref):
        mine = ids_ref[1]
        g = None
        for kk in range(N_CHIP):
            term = jnp.where(mine == kk, own_ref[...], recv_ref[kk].astype(F32))
            g = term if g is None else g + term
        delta, m_new, v_new = _adamw_math(w_ref[...], g, m_ref[...], v_ref[...])
        g_ref[...] = g
        d_ref[...] = delta
        mo_ref[...] = m_new
        vo_ref[...] = v_new

    row = pl.BlockSpec((tr, c), lambda i, ids: (i, 0))
    grid_spec = pltpu.PrefetchScalarGridSpec(
        num_scalar_prefetch=1, grid=(r // tr,),
        in_specs=[row, pl.BlockSpec((N_CHIP, tr, c), lambda i, ids: (0, i, 0)), row, row, row],
        out_specs=[row, row, row, row])
    out = jax.ShapeDtypeStruct((r, c), F32)
    return pl.pallas_call(
        body, name=name, grid_spec=grid_spec, out_shape=[out, out, out, out],
        compiler_params=_params("parallel"),
    )(ids, own, recv, w, m, v)


def _sum_sources(a, name):
    n, r, c = a.shape

    def body(a_ref, o_ref):
        tot = a_ref[0]
        for kk in range(1, n):
            tot = tot + a_ref[kk]
        o_ref[...] = tot

    return pl.pallas_call(
        body, name=name, out_shape=jax.ShapeDtypeStruct((r, c), F32),
        in_specs=[pl.BlockSpec(memory_space=pltpu.VMEM)], out_specs=pl.BlockSpec(memory_space=pltpu.VMEM),
    )(a)


def _adamw_small(w, g, m, v, name):
    def body(w_ref, g_ref, m_ref, v_ref, d_ref, mo_ref, vo_ref):
        delta, m_new, v_new = _adamw_math(w_ref[...], g_ref[...], m_ref[...], v_ref[...])
        d_ref[...] = delta
        mo_ref[...] = m_new
        vo_ref[...] = v_new

    vm = pl.BlockSpec(memory_space=pltpu.VMEM)
    out = jax.ShapeDtypeStruct(w.shape, F32)
    return pl.pallas_call(body, name=name, out_shape=[out, out, out], in_specs=[vm] * 4, out_specs=[vm] * 3)(w, g, m, v)


def _pack(parts, width, row_mult):
    flat = jnp.concatenate([p.reshape(-1) for p in parts])
    rows = -(-flat.shape[0] // width)
    rows = -(-rows // row_mult) * row_mult
    return jnp.pad(flat, (0, rows * width - flat.shape[0])).reshape(rows, width)


def _unpack(flat, shapes):
    out, off = [], 0
    lead = flat.shape[:-1]
    for shp in shapes:
        n = 1
        for dd in shp:
            n *= dd
        out.append(flat[..., off:off + n].reshape(lead + tuple(shp)))
        off += n
    return out


def _rows2d(w):
    return w.reshape(w.shape[0] * w.shape[1], w.shape[2])


def _cols_to_dev(g):
    l, k, n = g.shape
    return g.reshape(l * k, N_DEV, n // N_DEV).transpose(1, 0, 2)


def _rows_to_dev(g):
    l, k, n = g.shape
    rs = k // N_DEV
    return g.reshape(l, N_DEV, rs, n).transpose(1, 0, 2, 3).reshape(N_DEV, l * rs, n)


def _dev_to_cols(a, l):
    _, lk, cs = a.shape
    return a.transpose(1, 0, 2).reshape(l, lk // l, N_DEV * cs)


def _dev_to_rows(a, l):
    _, lr, n = a.shape
    rs = lr // l
    return a.reshape(N_DEV, l, rs, n).transpose(1, 0, 2, 3).reshape(l, N_DEV * rs, n)


def kernel(x, attn_norm, attn_w_in, attn_f_bias, fox_q_gain, fox_k_gain, sb_q_gain, sb_k_gain, attn_w_out, conv_norm, conv_w_in, conv_kernel, conv_w_out, ffn_norm, ffn_w_up, ffn_conv, ffn_w_down, loss_target, m_attn_norm, m_attn_w_in, m_attn_f_bias, m_fox_q_gain, m_fox_k_gain, m_sb_q_gain, m_sb_k_gain, m_attn_w_out, m_conv_norm, m_conv_w_in, m_conv_kernel, m_conv_w_out, m_ffn_norm, m_ffn_w_up, m_ffn_conv, m_ffn_w_down, v_attn_norm, v_attn_w_in, v_attn_f_bias, v_fox_q_gain, v_fox_k_gain, v_sb_q_gain, v_sb_k_gain, v_attn_w_out, v_conv_norm, v_conv_w_in, v_conv_kernel, v_conv_w_out, v_ffn_norm, v_ffn_w_up, v_ffn_conv, v_ffn_w_down):
    s = x.shape[1]
    n_attn, n_conv, depth = attn_w_in.shape[0], conv_w_in.shape[0], ffn_w_up.shape[0]
    xi, yi, ci = _coords()
    me = 4 * xi + 2 * yi + ci
    ids = jnp.stack([ci, 2 * xi + yi]).astype(jnp.int32)

    big = [attn_w_in, attn_w_out, conv_w_in, conv_w_out, ffn_w_up, ffn_w_down]
    big_m = [m_attn_w_in, m_attn_w_out, m_conv_w_in, m_conv_w_out, m_ffn_w_up, m_ffn_w_down]
    big_v = [v_attn_w_in, v_attn_w_out, v_conv_w_in, v_conv_w_out, v_ffn_w_up, v_ffn_w_down]
    big_names = ["attn_w_in", "attn_w_out", "conv_w_in", "conv_w_out", "ffn_w_up", "ffn_w_down"]
    small_sh = [conv_norm, conv_kernel, ffn_conv]
    small_sh_shapes = [w.shape for w in small_sh]
    rep = [attn_norm, attn_f_bias, fox_q_gain, fox_k_gain, sb_q_gain, sb_k_gain, ffn_norm]
    rep_shapes = [w.shape for w in rep]

    small_pack = _pack(small_sh, LANES, 8)
    gathered = _all_gather([_rows2d(w).astype(BF16) for w in big] + [small_pack], "gather_weights")
    a_w_in = _dev_to_cols(gathered[0], n_attn)
    a_w_in = jnp.pad(a_w_in, ((0, 0), (0, 0), (0, ATTN_IN_PAD - ATTN_IN)))
    a_w_out = _dev_to_rows(gathered[1], n_attn)
    c_w_in = _dev_to_cols(gathered[2], n_conv)
    c_w_out = _dev_to_rows(gathered[3], n_conv)
    f_w_up = _dev_to_cols(gathered[4], depth)
    f_w_down = _dev_to_rows(gathered[5], depth)
    cn, ckern, fconv = _unpack(gathered[6].reshape(N_DEV, -1), small_sh_shapes)
    conv_norm_f = cn.transpose(1, 0, 2).reshape(n_conv, D_MODEL)
    conv_kernel_f = ckern.transpose(1, 2, 0, 3).reshape(n_conv, 3, D_MODEL)
    ffn_conv_f = fconv.transpose(1, 2, 0, 3).reshape(depth, 3, 2 * D_FF)

    def pair_gain(fox_g, sb_g):
        f2 = jnp.concatenate([fox_g, fox_g])
        s2 = jnp.concatenate([sb_g, sb_g])
        return jnp.concatenate([jnp.tile(f2[None], (4, 1)), jnp.tile(s2[None], (4, 1))])[:, None, :]

    h = x[0]
    saved = []
    for layer in range(depth):
        i = layer // 2
        tag = "l%d" % layer
        rec = {"h_in": h}
        if layer % 2 == 0:
            xn = _rms_fwd(h, attn_norm[i], tag + "_attn_rms")
            proj = _matmul(xn, a_w_in[i], tag + "_attn_in", tn=640)
            gq, gk = pair_gain(fox_q_gain[i], sb_q_gain[i]), pair_gain(fox_k_gain[i], sb_k_gain[i])
            qh, kh, vb, qt, kt, vt = _qkv_prep(proj, gq, gk, tag + "_qkv_prep")
            logit = proj[:, 3 * MIX:3 * MIX + H_FOX].T.reshape(H_FOX, s // LANES, LANES)
            cum = _fgate_fwd(logit, attn_f_bias[i], tag + "_fgate")
            frow = cum.reshape(H_FOX // 2, 2, s)
            fcol = frow.transpose(0, 2, 1)
            o_fox, lse = _fox_fwd(qt, kh, vt, frow, fcol, tag + "_fox_fwd")
            o_sb, tot = _sb_fwd(qt, kh, vt, tag + "_sb_fwd")
            o = jnp.concatenate([o_fox, o_sb], axis=1)
            h = _matmul(o, a_w_out[i], tag + "_attn_out", add=h)
            rec.update(xn=xn, proj=proj, gq=gq, gk=gk, qh=qh, kh=kh, vb=vb, qt=qt, kt=kt, logit=logit, frow=frow,
                       fcol=fcol, o_fox=o_fox, lse=lse, tot=tot, o=o)
        else:
            xn = _rms_fwd(h, conv_norm_f[i], tag + "_conv_rms")
            proj = _matmul(xn, c_w_in[i], tag + "_conv_in")
            y = _sconv_fwd(proj, conv_kernel_f[i], tag + "_sconv_fwd")
            h = _matmul(y, c_w_out[i], tag + "_conv_out", add=h)
            rec.update(xn=xn, proj=proj, y=y)
        rec["h_mid"] = h
        xn2 = _rms_fwd(h, ffn_norm[layer], tag + "_ffn_rms")
        up = _matmul(xn2, f_w_up[layer], tag + "_ffn_up")
        act = _ffn_act_fwd(up, ffn_conv_f[layer], tag + "_ffn_act")
        h = _matmul(act, f_w_down[layer], tag + "_ffn_down", add=h, tk=2816)
        rec.update(xn2=xn2, up=up, act=act)
        saved.append(rec)

    loss_local, dh = _loss_head(h, loss_target[0], "loss_head")
    loss = lax.psum(loss_local, ("x", "y", "c"))

    g_attn_norm, g_attn_w_in, g_f_bias = [None] * n_attn, [None] * n_attn, [None] * n_attn
    g_fq, g_fk, g_sq, g_sk, g_attn_w_out = ([None] * n_attn for _ in range(5))
    g_conv_norm, g_conv_w_in, g_conv_kernel, g_conv_w_out = ([None] * n_conv for _ in range(4))
    g_ffn_norm, g_ffn_w_up, g_ffn_conv, g_ffn_w_down = ([None] * depth for _ in range(4))

    for layer in reversed(range(depth)):
        i = layer // 2
        tag = "l%d" % layer
        rec = saved[layer]
        da = _matmul(dh, f_w_down[layer].T, tag + "_ffn_down_dx", tn=1408)
        g_ffn_w_down[layer] = _matmul_tn(rec["act"], dh, tag + "_ffn_down_dw", tm=1408, tn=1024)
        dug, duv, dwg, dwv = _ffn_act_bwd(rec["up"], ffn_conv_f[layer], da, tag + "_ffn_act_bwd")
        g_ffn_conv[layer] = jnp.concatenate([dwg, dwv], axis=1)
        dup = jnp.concatenate([dug, duv], axis=1)
        g_ffn_w_up[layer] = _matmul_tn(rec["xn2"], dup, tag + "_ffn_up_dw", tn=1408)
        dxn = _matmul(dup, f_w_up[layer].T, tag + "_ffn_up_dx", tn=1024, tk=2816)
        dh, g_ffn_norm[layer] = _rms_bwd(rec["h_mid"], dxn, ffn_norm[layer], dh, tag + "_ffn_rms_bwd")
        if layer % 2 == 0:
            do = _matmul(dh, a_w_out[i].T, tag + "_attn_out_dx", tn=1024)
            g_attn_w_out[i] = _matmul_tn(rec["o"], dh, tag + "_attn_out_dw", tn=1024)
            dq_f, dk_f, dv_f, dfk, dfq = _fox_bwd(rec["qh"], rec["qt"], rec["kh"], rec["kt"], rec["vb"], rec["frow"],
                                                  rec["fcol"], rec["lse"], rec["o_fox"], do, tag + "_fox_bwd")
            dq_s, dk_s, dv_s = _sb_bwd(rec["qh"], rec["qt"], rec["kh"], rec["kt"], rec["vb"], rec["tot"], do,
                                       tag + "_sb_bwd")
            dq, dk, dv, dgq, dgk = _qkv_prep_bwd(rec["proj"], rec["gq"], rec["gk"], (dq_f, dq_s), (dk_f, dk_s),
                                                 (dv_f, dv_s), tag + "_qkv_prep_bwd")
            dcum = (dfq + dfk[:, :, 0:2].transpose(0, 2, 1)).reshape(H_FOX, s // LANES, LANES)
            dlogit, dbias = _fgate_bwd(rec["logit"], attn_f_bias[i], dcum, tag + "_fgate_bwd")
            g_f_bias[i] = dbias[:, 0]
            dgate = jnp.pad(dlogit.reshape(H_FOX, s).T, ((0, 0), (0, LANES - H_FOX))).astype(BF16)
            dproj = jnp.concatenate([dq, dk, dv, dgate], axis=1)

            def fold(dg):
                per_head = dg.reshape(16, HEAD_DIM)
                return jnp.sum(per_head[:8], axis=0), jnp.sum(per_head[8:], axis=0)

            g_fq[i], g_sq[i] = fold(dgq)
            g_fk[i], g_sk[i] = fold(dgk)
            g_attn_w_in[i] = _matmul_tn(rec["xn"], dproj, tag + "_attn_in_dw", tn=640)[:, :ATTN_IN]
            dxn = _matmul(dproj, a_w_in[i].T, tag + "_attn_in_dx", tn=1024, tk=3200)
            dh, g_attn_norm[i] = _rms_bwd(rec["h_in"], dxn, attn_norm[i], dh, tag + "_attn_rms_bwd")
        else:
            dy = _matmul(dh, c_w_out[i].T, tag + "_conv_out_dx", tn=1024)
            g_conv_w_out[i] = _matmul_tn(rec["y"], dh, tag + "_conv_out_dw", tn=1024)
            db, dc, du, g_conv_kernel[i] = _sconv_bwd(rec["proj"], conv_kernel_f[i], dy, tag + "_sconv_bwd")
            dproj = jnp.concatenate([db, dc, du], axis=1)
            g_conv_w_in[i] = _matmul_tn(rec["xn"], dproj, tag + "_conv_in_dw", tn=1024)
            dxn = _matmul(dproj, c_w_in[i].T, tag + "_conv_in_dx", tn=1024, tk=3072)
            dh, g_conv_norm[i] = _rms_bwd(rec["h_in"], dxn, conv_norm_f[i], dh, tag + "_conv_rms_bwd")
    grad_x = dh[None]

    gb = [_cols_to_dev(jnp.stack(g_attn_w_in).astype(BF16)), _rows_to_dev(jnp.stack(g_attn_w_out).astype(BF16)),
          _cols_to_dev(jnp.stack(g_conv_w_in).astype(BF16)), _rows_to_dev(jnp.stack(g_conv_w_out).astype(BF16)),
          _cols_to_dev(jnp.stack(g_ffn_w_up).astype(BF16)), _rows_to_dev(jnp.stack(g_ffn_w_down).astype(BF16))]
    from_sibling = _sibling_exchange(gb, "reduce_sibling")
    pairs = [_pair_add(g, r, ids, "reduce_pair_add_" + nm) for g, r, nm in zip(gb, from_sibling, big_names)]
    from_chips = _chip_exchange([p[0] for p in pairs], "reduce_chips")
    grads_big, delta_big, newm_big, newv_big = [], [], [], []
    for a, nm in enumerate(big_names):
        g_a, d_a, m_a, v_a = _adamw_reduce(pairs[a][1], from_chips[a], ids, _rows2d(big[a]), _rows2d(big_m[a]),
                                           _rows2d(big_v[a]), "adamw_" + nm)
        shp = big[a].shape
        grads_big.append(g_a.reshape(shp))
        delta_big.append(d_a.reshape(shp))
        newm_big.append(m_a.reshape(shp))
        newv_big.append(v_a.reshape(shp))

    rep_g = [jnp.stack(g_attn_norm), jnp.stack(g_f_bias), jnp.stack(g_fq), jnp.stack(g_fk), jnp.stack(g_sq),
             jnp.stack(g_sk), jnp.stack(g_ffn_norm)]
    sh_g = [jnp.stack(g_conv_norm).reshape(n_conv, N_DEV, -1).transpose(1, 0, 2),
            jnp.stack(g_conv_kernel).reshape(n_conv, 3, N_DEV, -1).transpose(2, 0, 1, 3),
            jnp.stack(g_ffn_conv).reshape(depth, 3, N_DEV, -1).transpose(2, 0, 1, 3)]
    n_rep = sum(int(a.size) for a in rep)
    n_sh = sum(int(a.size) for a in small_sh)
    partial = _pack(rep_g + [jnp.concatenate([a.reshape(N_DEV, -1) for a in sh_g], axis=1)], LANES, 8)
    total = _sum_sources(_all_gather([partial], "gather_small_grads")[0], "sum_small_grads").reshape(-1)
    rep_tot = total[:n_rep]
    sh_tot = lax.dynamic_slice_in_dim(total[n_rep:n_rep + N_DEV * n_sh].reshape(N_DEV, n_sh), me, 1, axis=0)[0]
    g_small = _pack([rep_tot, sh_tot], LANES, 8)

    def small_pack_of(rep_list, sh_list):
        return _pack(rep_list + sh_list, LANES, 8)

    d_small, m_small, v_small = _adamw_small(
        small_pack_of(rep, small_sh), g_small,
        small_pack_of([m_attn_norm, m_attn_f_bias, m_fox_q_gain, m_fox_k_gain, m_sb_q_gain, m_sb_k_gain, m_ffn_norm],
                      [m_conv_norm, m_conv_kernel, m_ffn_conv]),
        small_pack_of([v_attn_norm, v_attn_f_bias, v_fox_q_gain, v_fox_k_gain, v_sb_q_gain, v_sb_k_gain, v_ffn_norm],
                      [v_conv_norm, v_conv_kernel, v_ffn_conv]),
        "adamw_small")
    small_shapes = rep_shapes + small_sh_shapes

    def split_small(a):
        return _unpack(a.reshape(-1), small_shapes)

    def ordered(big_list, small_list):
        an, fb, fq, fk, sq, sk, fn, cno, cke, fco = small_list
        awi, awo, cwi, cwo, fwu, fwd = big_list
        return [an, awi, fb, fq, fk, sq, sk, awo, cno, cwi, cke, cwo, fn, fwu, fco, fwd]

    grads = ordered(grads_big, split_small(g_small))
    deltas = ordered(delta_big, split_small(d_small))
    new_m = ordered(newm_big, split_small(m_small))
    new_v = ordered(newv_big, split_small(v_small))
    return (loss, grad_x, *grads, *deltas, *new_m, *new_v)
```

```python
import jax
import jax.numpy as jnp
from jax import lax
from jax.experimental import pallas as pl
from jax.experimental.pallas import tpu as pltpu

F32 = jnp.float32
BF16 = jnp.bfloat16

D_MODEL = 1024
HEAD_DIM = 64
H_FOX = 8
MIX = 1024
ATTN_IN = 3 * MIX + H_FOX
ATTN_IN_PAD = 3 * MIX + 128
D_FF = 2816
EPS = 1e-6
NEG = -1e30
LANES = 128
N_DEV = 8
N_CHIP = 4

ADAM_LR = 0.001
ADAM_B1 = 0.9
ADAM_B2 = 0.999
ADAM_EPS = 1e-08
ADAM_WD = 0.01
ADAM_STEP = 10

VMEM_LIMIT = 56 * 1024 * 1024
MESH = pl.DeviceIdType.MESH
ANY = pl.BlockSpec(memory_space=pl.ANY)


def _params(*sem):
    return pltpu.CompilerParams(dimension_semantics=sem, vmem_limit_bytes=VMEM_LIMIT)


def _tile(n, target, mult=LANES):
    best = None
    for t in range(mult, min(n, target) + 1, mult):
        if n % t == 0:
            best = t
    return best if best is not None else n


def _dot(a, b):
    return jnp.dot(a, b, preferred_element_type=F32)


def _dot_tn(a, b):
    return lax.dot_general(a, b, (((0,), (0,)), ((), ())), preferred_element_type=F32)


def _split_dot(x, m, passes):
    acc = None
    rem = x
    for _ in range(passes):
        part = rem.astype(BF16)
        term = _dot(part, m)
        acc = term if acc is None else acc + term
        rem = rem - part.astype(F32)
    return acc


def _split_dot_left(m, x, passes):
    acc = None
    rem = x
    for _ in range(passes):
        part = rem.astype(BF16)
        term = _dot(m, part)
        acc = term if acc is None else acc + term
        rem = rem - part.astype(F32)
    return acc


def _matmul(a, b, name, add=None, out_dtype=F32, tm=1024, tn=512, tk=1024):
    split = a.shape[0] if a.ndim == 3 else 1
    m, kh = a.shape[-2:]
    k = split * kh
    n = b.shape[1]
    tm, tn, tk = _tile(m, tm, 8), _tile(n, tn), _tile(kh, tk)
    nk = k // tk
    per_slab = kh // tk
    has_add = add is not None

    def body(*refs):
        a_ref, b_ref = refs[0], refs[1]
        add_ref = refs[2] if has_add else None
        o_ref = refs[2 + has_add]

        def finish(acc):
            if has_add:
                acc = acc + add_ref[...]
            o_ref[...] = acc.astype(out_dtype)

        p = _dot(a_ref[...].astype(BF16), b_ref[...].astype(BF16))
        if nk == 1:
            finish(p)
        else:
            acc_ref = refs[-1]
            kk = pl.program_id(2)

            @pl.when(kk == 0)
            def _():
                acc_ref[...] = p

            @pl.when(kk > 0)
            def _():
                acc_ref[...] += p

            @pl.when(kk == nk - 1)
            def _():
                finish(acc_ref[...])

    if split == 1:
        a_spec = pl.BlockSpec((tm, tk), lambda i, j, kk: (i, kk))
    else:
        a_spec = pl.BlockSpec((None, tm, tk), lambda i, j, kk: (kk // per_slab, i, kk % per_slab))
    in_specs = [a_spec, pl.BlockSpec((tk, tn), lambda i, j, kk: (kk, j))]
    args = [a, b]
    if has_add:
        in_specs.append(pl.BlockSpec((tm, tn), lambda i, j, kk: (i, j)))
        args.append(add)
    return pl.pallas_call(
        body, name=name, grid=(m // tm, n // tn, nk), in_specs=in_specs,
        out_specs=pl.BlockSpec((tm, tn), lambda i, j, kk: (i, j)),
        out_shape=jax.ShapeDtypeStruct((m, n), out_dtype),
        scratch_shapes=[pltpu.VMEM((tm, tn), F32)] if nk > 1 else [],
        compiler_params=_params("parallel", "parallel", "arbitrary"),
    )(*args)


def _matmul_tn(a, b, name, tm=1024, tn=512, ts=2048):
    s, m = a.shape
    split = b.shape[0] if b.ndim == 3 else 1
    nh = b.shape[-1]
    n = split * nh
    tm, tn, ts = _tile(m, tm), _tile(nh, tn), _tile(s, ts, 8)
    per_slab = nh // tn
    if split == 1:
        b_spec = pl.BlockSpec((ts, tn), lambda i, j, kk: (kk, j))
    else:
        b_spec = pl.BlockSpec((None, ts, tn), lambda i, j, kk: (j // per_slab, kk, j % per_slab))

    def body(a_ref, b_ref, o_ref):
        kk = pl.program_id(2)
        p = _dot_tn(a_ref[...].astype(BF16), b_ref[...].astype(BF16))

        @pl.when(kk == 0)
        def _():
            o_ref[...] = p

        @pl.when(kk > 0)
        def _():
            o_ref[...] += p

    return pl.pallas_call(
        body, name=name, grid=(m // tm, n // tn, s // ts),
        in_specs=[pl.BlockSpec((ts, tm), lambda i, j, kk: (kk, i)), b_spec],
        out_specs=pl.BlockSpec((tm, tn), lambda i, j, kk: (i, j)),
        out_shape=jax.ShapeDtypeStruct((m, n), F32),
        compiler_params=_params("parallel", "parallel", "arbitrary"),
    )(a, b)


def _rms_fwd(h, g, name, ts=512):
    s, d = h.shape
    ts = _tile(s, ts, 8)

    def body(h_ref, g_ref, o_ref):
        x = h_ref[...]
        r = lax.rsqrt(jnp.mean(x * x, axis=-1, keepdims=True) + EPS)
        o_ref[...] = (x * r * g_ref[...]).astype(BF16)

    return pl.pallas_call(
        body, name=name, grid=(s // ts,),
        in_specs=[pl.BlockSpec((ts, d), lambda i: (i, 0)), pl.BlockSpec((1, d), lambda i: (0, 0))],
        out_specs=pl.BlockSpec((ts, d), lambda i: (i, 0)),
        out_shape=jax.ShapeDtypeStruct((s, d), BF16),
        compiler_params=_params("parallel"),
    )(h, g.reshape(1, d))


def _rms_bwd(h, dxn, g, dh_in, name, ts=512):
    s, d = h.shape
    ts = _tile(s, ts, 8)

    def body(h_ref, dxn_ref, g_ref, dhin_ref, dh_ref, dg_ref):
        i = pl.program_id(0)
        x = h_ref[...]
        r = lax.rsqrt(jnp.mean(x * x, axis=-1, keepdims=True) + EPS)
        xh = x * r
        dxn_v = dxn_ref[...]

        @pl.when(i == 0)
        def _():
            dg_ref[...] = jnp.zeros_like(dg_ref)

        dg_ref[0:1, :] += jnp.sum(dxn_v * xh, axis=0, keepdims=True)
        dxh = dxn_v * g_ref[...]
        dx = r * (dxh - xh * jnp.mean(dxh * xh, axis=-1, keepdims=True))
        dh_ref[...] = dhin_ref[...] + dx

    row = pl.BlockSpec((ts, d), lambda i: (i, 0))
    dh, dg = pl.pallas_call(
        body, name=name, grid=(s // ts,),
        in_specs=[row, row, pl.BlockSpec((1, d), lambda i: (0, 0)), row],
        out_specs=[row, pl.BlockSpec((8, d), lambda i: (0, 0))],
        out_shape=[jax.ShapeDtypeStruct((s, d), F32), jax.ShapeDtypeStruct((8, d), F32)],
        compiler_params=_params("arbitrary"),
    )(h, dxn, g.reshape(1, d), dh_in)
    return dh, dg[0]


def _shift_down(x, prev):
    rows = lax.broadcasted_iota(jnp.int32, (8, x.shape[1]), 0)
    p1, p2 = prev[7:8, :], prev[6:7, :]
    r1, r2 = pltpu.roll(x, 1, 0), pltpu.roll(x, 2, 0)
    top1 = jnp.where(rows == 0, p1, r1[0:8, :])
    top2 = jnp.where(rows == 0, p2, jnp.where(rows == 1, p1, r2[0:8, :]))
    if x.shape[0] == 8:
        return top1, top2
    return jnp.concatenate([top1, r1[8:, :]], axis=0), jnp.concatenate([top2, r2[8:, :]], axis=0)


def _shift_up(x, nxt):
    n = x.shape[0]
    rows = lax.broadcasted_iota(jnp.int32, (8, x.shape[1]), 0)
    n0, n1 = nxt[0:1, :], nxt[1:2, :]
    r1, r2 = pltpu.roll(x, n - 1, 0), pltpu.roll(x, n - 2, 0)
    end1 = jnp.where(rows == 7, n0, r1[n - 8:, :])
    end2 = jnp.where(rows == 7, n1, jnp.where(rows == 6, n0, r2[n - 8:, :]))
    return jnp.concatenate([r1[:n - 8, :], end1], axis=0), jnp.concatenate([r2[:n - 8, :], end2], axis=0)


def _conv(x, x1, x2, w):
    return w[2:3, :] * x + w[1:2, :] * x1 + w[0:1, :] * x2


def _halo_specs(ts, tc, col, n_time_blocks):
    r8 = ts // 8
    main = pl.BlockSpec((ts, tc), lambda j, i: (i, j + col))
    prev = pl.BlockSpec((8, tc), lambda j, i: (jnp.maximum(i * r8 - 1, 0), j + col))
    nxt = pl.BlockSpec((8, tc), lambda j, i: (jnp.minimum((i + 1) * r8, n_time_blocks * r8 - 1), j + col))
    return main, prev, nxt


def _silu_parts(g):
    sig = 1.0 / (1.0 + jnp.exp(-g))
    return sig, g * sig


def _ffn_act_fwd(up, cw, name, ts=256, tc=1408):
    s = up.shape[0]
    ts, tc = _tile(s, ts, 8), _tile(D_FF, tc)
    nc, nt = D_FF // tc, s // ts

    def body(g_ref, gp_ref, v_ref, vp_ref, wg_ref, wv_ref, o_ref):
        first = pl.program_id(1) == 0

        def conv(x_ref, p_ref, w_ref):
            x = x_ref[...]
            prev = jnp.where(first, 0.0, p_ref[...])
            x1, x2 = _shift_down(x, prev)
            return _conv(x, x1, x2, w_ref[...])

        ug = conv(g_ref, gp_ref, wg_ref)
        uv = conv(v_ref, vp_ref, wv_ref)
        _, silu = _silu_parts(ug)
        o_ref[...] = (silu * uv).astype(BF16)

    g_main, g_prev, _ = _halo_specs(ts, tc, 0, nt)
    v_main, v_prev, _ = _halo_specs(ts, tc, nc, nt)
    return pl.pallas_call(
        body, name=name, grid=(nc, nt),
        in_specs=[g_main, g_prev, v_main, v_prev,
                  pl.BlockSpec((3, tc), lambda j, i: (0, j)), pl.BlockSpec((3, tc), lambda j, i: (0, j + nc))],
        out_specs=pl.BlockSpec((ts, tc), lambda j, i: (i, j)),
        out_shape=jax.ShapeDtypeStruct((s, D_FF), BF16),
        compiler_params=_params("parallel", "parallel"),
    )(up, up, up, up, cw, cw)


def _ffn_act_bwd(up, cw, da, name, ts=256, tc=1408):
    s = up.shape[0]
    ts, tc = _tile(s, ts, 8), _tile(D_FF, tc)
    nc, nt = D_FF // tc, s // ts

    def body(g_ref, gp_ref, gn_ref, v_ref, vp_ref, vn_ref, da_ref, dan_ref, wg_ref, wv_ref,
             d_ref, dwg_ref, dwv_ref):
        i = pl.program_id(1)
        first, last = i == 0, i == nt - 1
        wg, wv = wg_ref[...], wv_ref[...]
        g, v = g_ref[...], v_ref[...]
        g1, g2 = _shift_down(g, jnp.where(first, 0.0, gp_ref[...]))
        v1, v2 = _shift_down(v, jnp.where(first, 0.0, vp_ref[...]))

        def d_u(ug, uv, da_v):
            sig, silu = _silu_parts(ug)
            return da_v * uv * (sig * (1.0 + ug * (1.0 - sig))), da_v * silu

        dug, duv = d_u(_conv(g, g1, g2, wg), _conv(v, v1, v2, wv), da_ref[...])
        gn, vn = gn_ref[...], vn_ref[...]
        gn1, gn2 = _shift_down(gn, g[ts - 8:, :])
        vn1, vn2 = _shift_down(vn, v[ts - 8:, :])
        dugn, duvn = d_u(_conv(gn, gn1, gn2, wg), _conv(vn, vn1, vn2, wv), dan_ref[...])
        dugn = jnp.where(last, 0.0, dugn)
        duvn = jnp.where(last, 0.0, duvn)

        def finish(du, dun, x, x1, x2, w, dx_ref, dw_ref):
            d1, d2 = _shift_up(du, dun)
            dx_ref[...] = (w[2:3, :] * du + w[1:2, :] * d1 + w[0:1, :] * d2).astype(BF16)

            @pl.when(first)
            def _():
                dw_ref[...] = jnp.zeros_like(dw_ref)

            dw_ref[0:1, :] += jnp.sum(du * x2, axis=0, keepdims=True)
            dw_ref[1:2, :] += jnp.sum(du * x1, axis=0, keepdims=True)
            dw_ref[2:3, :] += jnp.sum(du * x, axis=0, keepdims=True)

        finish(dug, dugn, g, g1, g2, wg, d_ref.at[0], dwg_ref)
        finish(duv, duvn, v, v1, v2, wv, d_ref.at[1], dwv_ref)

    g_specs = _halo_specs(ts, tc, 0, nt)
    v_specs = _halo_specs(ts, tc, nc, nt)
    da_main, _, da_next = _halo_specs(ts, tc, 0, nt)
    taps = pl.BlockSpec((8, tc), lambda j, i: (0, j))
    halves = pl.BlockSpec((2, ts, tc), lambda j, i: (0, i, j))
    d, dwg, dwv = pl.pallas_call(
        body, name=name, grid=(nc, nt),
        in_specs=[*g_specs, *v_specs, da_main, da_next,
                  pl.BlockSpec((3, tc), lambda j, i: (0, j)), pl.BlockSpec((3, tc), lambda j, i: (0, j + nc))],
        out_specs=[halves, taps, taps],
        out_shape=[jax.ShapeDtypeStruct((2, s, D_FF), BF16),
                   jax.ShapeDtypeStruct((8, D_FF), F32), jax.ShapeDtypeStruct((8, D_FF), F32)],
        compiler_params=_params("parallel", "arbitrary"),
    )(up, up, up, up, up, up, da, da, cw, cw)
    return d, dwg[:3], dwv[:3]


def _sconv_fwd(proj, ck, name, ts=256, tc=512):
    s = proj.shape[0]
    w = D_MODEL
    ts, tc = _tile(s, ts, 8), _tile(w, tc)
    nc, nt = w // tc, s // ts

    def body(b_ref, c_ref, cp_ref, u_ref, up_ref, w_ref, o_ref):
        first = pl.program_id(1) == 0
        cu = c_ref[...] * u_ref[...]
        cup = jnp.where(first, 0.0, cp_ref[...] * up_ref[...])
        x1, x2 = _shift_down(cu, cup)
        o_ref[...] = (b_ref[...] * _conv(cu, x1, x2, w_ref[...])).astype(BF16)

    b_main, _, _ = _halo_specs(ts, tc, 0, nt)
    c_main, c_prev, _ = _halo_specs(ts, tc, nc, nt)
    u_main, u_prev, _ = _halo_specs(ts, tc, 2 * nc, nt)
    return pl.pallas_call(
        body, name=name, grid=(nc, nt),
        in_specs=[b_main, c_main, c_prev, u_main, u_prev, pl.BlockSpec((3, tc), lambda j, i: (0, j))],
        out_specs=pl.BlockSpec((ts, tc), lambda j, i: (i, j)),
        out_shape=jax.ShapeDtypeStruct((s, w), BF16),
        compiler_params=_params("parallel", "parallel"),
    )(proj, proj, proj, proj, proj, ck)


def _sconv_bwd(proj, ck, dy, name, ts=256, tc=512):
    s = proj.shape[0]
    w = D_MODEL
    ts, tc = _tile(s, ts, 8), _tile(w, tc)
    nc, nt = w // tc, s // ts

    def body(b_ref, bn_ref, c_ref, cp_ref, u_ref, up_ref, dy_ref, dyn_ref, w_ref,
             d_ref, dw_ref):
        i = pl.program_id(1)
        first, last = i == 0, i == nt - 1
        wv = w_ref[...]
        b, c, u, dy_v = b_ref[...], c_ref[...], u_ref[...], dy_ref[...]
        cu = c * u
        cup = jnp.where(first, 0.0, cp_ref[...] * up_ref[...])
        x1, x2 = _shift_down(cu, cup)
        d_ref[0] = (dy_v * _conv(cu, x1, x2, wv)).astype(BF16)
        dcv = dy_v * b
        dcvn = jnp.where(last, 0.0, dyn_ref[...] * bn_ref[...])
        d1, d2 = _shift_up(dcv, dcvn)
        dcu = wv[2:3, :] * dcv + wv[1:2, :] * d1 + wv[0:1, :] * d2
        d_ref[1] = (dcu * u).astype(BF16)
        d_ref[2] = (dcu * c).astype(BF16)

        @pl.when(first)
        def _():
            dw_ref[...] = jnp.zeros_like(dw_ref)

        dw_ref[0:1, :] += jnp.sum(dcv * x2, axis=0, keepdims=True)
        dw_ref[1:2, :] += jnp.sum(dcv * x1, axis=0, keepdims=True)
        dw_ref[2:3, :] += jnp.sum(dcv * cu, axis=0, keepdims=True)

    b_main, _, b_next = _halo_specs(ts, tc, 0, nt)
    c_main, c_prev, _ = _halo_specs(ts, tc, nc, nt)
    u_main, u_prev, _ = _halo_specs(ts, tc, 2 * nc, nt)
    dy_main, _, dy_next = _halo_specs(ts, tc, 0, nt)
    d, dw = pl.pallas_call(
        body, name=name, grid=(nc, nt),
        in_specs=[b_main, b_next, c_main, c_prev, u_main, u_prev, dy_main, dy_next,
                  pl.BlockSpec((3, tc), lambda j, i: (0, j))],
        out_specs=[pl.BlockSpec((3, ts, tc), lambda j, i: (0, i, j)), pl.BlockSpec((8, tc), lambda j, i: (0, j))],
        out_shape=[jax.ShapeDtypeStruct((3, s, w), BF16), jax.ShapeDtypeStruct((8, w), F32)],
        compiler_params=_params("parallel", "arbitrary"),
    )(proj, proj, proj, proj, proj, proj, dy, dy, ck)
    return d, dw[:3]


def _low_lanes(shape):
    return lax.broadcasted_iota(jnp.int32, shape, 1) < HEAD_DIM


def _top_rows(shape):
    return lax.broadcasted_iota(jnp.int32, shape, 0) < HEAD_DIM


def _norm_pair(x):
    r = lax.rsqrt(_mean_pair(x * x) + EPS)
    return x * r, r


def _mean_pair(x):
    same_head = _tri(LANES, lambda a, b: a // HEAD_DIM == b // HEAD_DIM)
    return _split_dot(x, same_head, 3) * (1.0 / HEAD_DIM)


def _qkv_prep(proj, gq, gk, name, ts=512):
    s = proj.shape[0]
    ts = _tile(s, ts)
    npair = MIX // LANES
    scale = HEAD_DIM ** -0.5

    def body(q_ref, k_ref, v_ref, gq_ref, gk_ref, qo_ref, ko_ref, vo_ref, qt_ref, kt_ref, vt_ref):
        qn, _ = _norm_pair(q_ref[...])
        kn, _ = _norm_pair(k_ref[...])
        q = qn * gq_ref[0] * scale
        k = kn * gk_ref[0]
        v = v_ref[...]
        qo_ref[...] = q.astype(BF16)
        ko_ref[...] = k.astype(BF16)
        vo_ref[...] = v.astype(BF16)
        qt_ref[...] = q.T.astype(BF16)
        kt_ref[...] = k.T.astype(BF16)
        vt_ref[...] = v.T.astype(BF16)

    gain = pl.BlockSpec((1, 1, LANES), lambda i, p: (p, 0, 0))
    tile = pl.BlockSpec((ts, LANES), lambda i, p: (i, p))
    tile_t = pl.BlockSpec((LANES, ts), lambda i, p: (p, i))
    out = jax.ShapeDtypeStruct((s, MIX), BF16)
    out_t = jax.ShapeDtypeStruct((MIX, s), BF16)
    return pl.pallas_call(
        body, name=name, grid=(s // ts, npair),
        in_specs=[tile, pl.BlockSpec((ts, LANES), lambda i, p: (i, p + npair)),
                  pl.BlockSpec((ts, LANES), lambda i, p: (i, p + 2 * npair)), gain, gain],
        out_specs=[tile, tile, tile, tile_t, tile_t, tile_t], out_shape=[out, out, out, out_t, out_t, out_t],
        compiler_params=_params("parallel", "parallel"),
    )(proj, proj, proj, gq, gk)


def _qkv_prep_bwd(proj, gq, gk, dqs, dks, dvs, name, ts=512):
    s = proj.shape[0]
    ts = _tile(s, ts, 8)
    npair = MIX // LANES
    half = npair // 2
    scale = HEAD_DIM ** -0.5

    def body(q_ref, k_ref, gq_ref, gk_ref, dqf_ref, dqs_ref, dkf_ref, dks_ref, dvf_ref, dvs_ref,
             dq_ref, dk_ref, dv_ref, dgq_ref, dgk_ref):
        p, i = pl.program_id(0), pl.program_id(1)
        fox = p < half

        def one(x_ref, g_ref, df_ref, ds_ref, dx_ref, dg_ref, mult):
            dn = jnp.where(fox, df_ref[...], ds_ref[...]) * mult
            xh, r = _norm_pair(x_ref[...])

            @pl.when(i == 0)
            def _():
                dg_ref[...] = jnp.zeros_like(dg_ref)

            dg_ref[0, 0:1, :] += jnp.sum(dn * xh, axis=0, keepdims=True)
            dxh = dn * g_ref[0]
            dx_ref[...] = (r * (dxh - xh * _mean_pair(dxh * xh))).astype(BF16)

        one(q_ref, gq_ref, dqf_ref, dqs_ref, dq_ref, dgq_ref, scale)
        one(k_ref, gk_ref, dkf_ref, dks_ref, dk_ref, dgk_ref, 1.0)
        dv_ref[...] = jnp.where(fox, dvf_ref[...], dvs_ref[...]).astype(BF16)

    gain = pl.BlockSpec((1, 1, LANES), lambda p, i: (p, 0, 0))
    tile = pl.BlockSpec((ts, LANES), lambda p, i: (i, p))
    fpart = pl.BlockSpec((ts, LANES), lambda p, i: (i, jnp.minimum(p, half - 1)))
    spart = pl.BlockSpec((ts, LANES), lambda p, i: (i, jnp.maximum(p - half, 0)))
    dgain = pl.BlockSpec((1, 8, LANES), lambda p, i: (p, 0, 0))
    out = jax.ShapeDtypeStruct((s, MIX), BF16)
    gshape = jax.ShapeDtypeStruct((npair, 8, LANES), F32)
    dq, dk, dv, dgq, dgk = pl.pallas_call(
        body, name=name, grid=(npair, s // ts),
        in_specs=[tile, pl.BlockSpec((ts, LANES), lambda p, i: (i, p + npair)), gain, gain,
                  fpart, spart, fpart, spart, fpart, spart],
        out_specs=[tile, tile, tile, dgain, dgain], out_shape=[out, out, out, gshape, gshape],
        compiler_params=_params("parallel", "arbitrary"),
    )(proj, proj, gq, gk, dqs[0], dqs[1], dks[0], dks[1], dvs[0], dvs[1])
    return dq, dk, dv, dgq[:, 0, :], dgk[:, 0, :]


def _tri(n, rel):
    a = lax.broadcasted_iota(jnp.int32, (n, n), 0)
    b = lax.broadcasted_iota(jnp.int32, (n, n), 1)
    return rel(a, b).astype(BF16)


def _fgate_fwd(logit, bias, name):
    nh, r, _ = logit.shape

    def body(x_ref, b_ref, o_ref):
        within = _tri(LANES, lambda a, b: a <= b)
        before = _tri(r, lambda a, b: b < a)
        for hh in range(nh):
            x = x_ref[hh] + b_ref[hh]
            lf = jnp.minimum(x, 0.0) - jnp.log1p(jnp.exp(-jnp.abs(x)))
            c = _split_dot(lf, within, 3)
            tot = jnp.broadcast_to(c[:, LANES - 1:LANES], (r, LANES))
            o_ref[hh] = c + _split_dot_left(before, tot, 3)

    return pl.pallas_call(
        body, name=name, out_shape=jax.ShapeDtypeStruct((nh, r, LANES), F32),
        in_specs=[pl.BlockSpec(memory_space=pltpu.VMEM), pl.BlockSpec(memory_space=pltpu.SMEM)],
        out_specs=pl.BlockSpec(memory_space=pltpu.VMEM),
    )(logit, bias)


def _fgate_bwd(logit, bias, dcum, name):
    nh, r, _ = logit.shape

    def body(x_ref, b_ref, d_ref, dx_ref, db_ref):
        within = _tri(LANES, lambda a, b: a >= b)
        after = _tri(r, lambda a, b: b > a)
        for hh in range(nh):
            x = x_ref[hh] + b_ref[hh]
            d = d_ref[hh]
            c = _split_dot(d, within, 3)
            tot = jnp.broadcast_to(c[:, 0:1], (r, LANES))
            dlf = c + _split_dot_left(after, tot, 3)
            dx = dlf * (1.0 / (1.0 + jnp.exp(x)))
            dx_ref[hh] = dx
            db_ref[hh:hh + 1, :] = jnp.broadcast_to(jnp.sum(dx, keepdims=True).reshape(1, 1), (1, LANES))

    return pl.pallas_call(
        body, name=name,
        out_shape=[jax.ShapeDtypeStruct((nh, r, LANES), F32), jax.ShapeDtypeStruct((nh, LANES), F32)],
        in_specs=[pl.BlockSpec(memory_space=pltpu.VMEM), pl.BlockSpec(memory_space=pltpu.SMEM),
                  pl.BlockSpec(memory_space=pltpu.VMEM)],
        out_specs=[pl.BlockSpec(memory_space=pltpu.VMEM), pl.BlockSpec(memory_space=pltpu.VMEM)],
    )(logit, bias, dcum)


def _pair_masks(x):
    lo = _low_lanes(x.shape)
    zero = jnp.zeros_like(x)
    return jnp.where(lo, x, zero), jnp.where(lo, zero, x)


def _pair_masks_t(x):
    top = _top_rows(x.shape)
    zero = jnp.zeros_like(x)
    return jnp.where(top, x, zero), jnp.where(top, zero, x)


def _stack_heads(x):
    return jnp.concatenate(_pair_masks(x), axis=0)


def _stack_heads_t(x):
    return jnp.concatenate(_pair_masks_t(x), axis=1)


def _pair_colsum_t(x):
    top = _top_rows(x.shape)
    return (jnp.sum(jnp.where(top, x, 0.0), axis=0, keepdims=True),
            jnp.sum(jnp.where(top, 0.0, x), axis=0, keepdims=True))


def _key_query_iotas(t):
    return lax.broadcasted_iota(jnp.int32, (t, t), 0), lax.broadcasted_iota(jnp.int32, (t, t), 1)


def _walk_blocks(i, step, descending, group=2):
    full = i // group
    left = i - full * group

    def run(first, count):
        sign = -1 if descending else 1
        step([(first + sign * n, False) for n in range(count)])

    def leftovers():
        start = (left - 1) if descending else full * group
        sign = -1 if descending else 1
        if group == 4:
            @pl.when(left >= 2)
            def _():
                run(start, 2)

            @pl.when(left % 2 == 1)
            def _():
                run(0 if descending else i - 1, 1)
        else:
            @pl.when(left == 1)
            def _():
                run(start, 1)

    def loop(g, carry):
        run((i - 1 - group * g) if descending else group * g, group)
        return carry

    if descending:
        step([(i, True)])
        lax.fori_loop(0, full, loop, 0)
        leftovers()
    else:
        lax.fori_loop(0, full, loop, 0)
        leftovers()
        step([(i, True)])


def _attn_specs(s, tq, pair0):
    q_nat = pl.BlockSpec((tq, LANES), lambda p, i: (i, p + pair0))
    q_t = pl.BlockSpec((LANES, tq), lambda p, i: (p + pair0, i))
    k_nat = pl.BlockSpec((s, LANES), lambda p, i: (0, p + pair0))
    k_t = pl.BlockSpec((LANES, s), lambda p, i: (p + pair0, 0))
    return q_nat, q_t, k_nat, k_t


def _fox_fwd(qt, kh, vt, frow, fcol, name, tq=256):
    s = kh.shape[0]
    tq = _tile(s, tq)
    nq, half = s // tq, MIX // LANES // 2

    def body(qt_ref, k_ref, vt_ref, fr_ref, fc_ref, o_ref, lse_ref, m_s, l_s, acc_s):
        i = pl.program_id(1)
        qt_v = qt_ref[...]
        ft = fr_ref[0]
        key, qry = _key_query_iotas(tq)
        causal = key <= qry
        m_s[...] = jnp.full(m_s.shape, NEG, F32)
        l_s[...] = jnp.zeros_like(l_s)
        acc_s[...] = jnp.zeros_like(acc_s)

        top = _top_rows((LANES, tq))

        def step(blocks):
            rows = [pl.ds(pl.multiple_of(j * tq, tq), tq) for j, _ in blocks]
            zs = [_dot(_stack_heads(k_ref[r, :]), qt_v) for r in rows]
            m_cur, l_cur = [m_s[0], m_s[1]], [l_s[0], l_s[1]]
            acc = acc_s[...]
            for b, (_, masked) in enumerate(blocks):
                fk = fc_ref[0, rows[b], :]
                prs, alphas = [], []
                for hh in range(2):
                    sc = zs[b][hh * tq:(hh + 1) * tq] + (ft[hh:hh + 1, :] - fk[:, hh:hh + 1])
                    if masked:
                        sc = jnp.where(causal, sc, NEG)
                    m_new = jnp.maximum(m_cur[hh], jnp.max(sc, axis=0, keepdims=True))
                    alpha = jnp.exp(m_cur[hh] - m_new)
                    pr = jnp.exp(sc - m_new)
                    l_cur[hh] = alpha * l_cur[hh] + jnp.sum(pr, axis=0, keepdims=True)
                    m_cur[hh] = m_new
                    prs.append(pr.astype(BF16))
                    alphas.append(alpha)
                pv = _dot(_stack_heads_t(vt_ref[:, rows[b]]), jnp.concatenate(prs, axis=0))
                acc = jnp.where(top, alphas[0], alphas[1]) * acc + pv
            acc_s[...] = acc
            for hh in range(2):
                m_s[hh] = m_cur[hh]
                l_s[hh] = l_cur[hh]

        _walk_blocks(i, step, descending=False, group=4)
        o_ref[...] = (acc_s[...] / jnp.where(top, l_s[0], l_s[1])).T
        lse_ref[0, 0:1, :] = m_s[0] + jnp.log(l_s[0])
        lse_ref[0, 1:2, :] = m_s[1] + jnp.log(l_s[1])

    _, q_t, k_nat, k_t = _attn_specs(s, tq, 0)
    qstat = pl.BlockSpec((1, 2, tq), lambda p, i: (p, 0, i))
    return pl.pallas_call(
        body, name=name, grid=(half, nq),
        in_specs=[q_t, k_nat, k_t, qstat, pl.BlockSpec((1, s, 2), lambda p, i: (p, 0, 0))],
        out_specs=[pl.BlockSpec((tq, LANES), lambda p, i: (i, p)), qstat],
        out_shape=[jax.ShapeDtypeStruct((s, MIX // 2), F32), jax.ShapeDtypeStruct((half, 2, s), F32)],
        scratch_shapes=[pltpu.VMEM((2, 1, tq), F32), pltpu.VMEM((2, 1, tq), F32), pltpu.VMEM((LANES, tq), F32)],
        compiler_params=_params("parallel", "arbitrary"),
    )(qt, kh, vt, frow, fcol)


def _fox_bwd(qh, qt, kh, kt, vb, frow, fcol, lse, o, do, name, tq=256):
    s = kh.shape[0]
    tq = _tile(s, tq)
    nq, half = s // tq, MIX // LANES // 2

    def body(q_ref, qt_ref, k_ref, kt_ref, v_ref, fr_ref, fc_ref, lse_ref, o_ref, do_ref,
             dq_ref, dk_ref, dv_ref, dfk_ref, dfq_ref, dq_s, rs_s):
        i = pl.program_id(1)

        @pl.when(i == 0)
        def _():
            dk_ref[...] = jnp.zeros_like(dk_ref)
            dv_ref[...] = jnp.zeros_like(dv_ref)
            dfk_ref[...] = jnp.zeros_like(dfk_ref)

        q2 = _stack_heads(q_ref[...])
        qt_v = qt_ref[...]
        do_v = do_ref[...]
        do2 = _stack_heads(do_v.astype(BF16))
        dot_v = do_v.T.astype(BF16)
        dsum = _pair_colsum_t((do_v * o_ref[...]).T)
        ft, ls = fr_ref[0], lse_ref[0]
        key, qry = _key_query_iotas(tq)
        causal = key <= qry
        lane = lax.broadcasted_iota(jnp.int32, (2 * tq, LANES), 0) // tq
        pick2 = (lax.broadcasted_iota(jnp.int32, (2 * tq, LANES), 1) == lane).astype(BF16)
        q2_pick = jnp.concatenate([q2, pick2], axis=1)
        dq_s[...] = jnp.zeros_like(dq_s)
        rs_s[...] = jnp.zeros_like(rs_s)

        def step(blocks):
            rows = [pl.ds(pl.multiple_of(j * tq, tq), tq) for j, _ in blocks]
            zs = [_dot(_stack_heads(k_ref[r, :]), qt_v) for r in rows]
            dps = [_dot(_stack_heads(v_ref[r, :]), dot_v) for r in rows]
            rs = [rs_s[0], rs_s[1]]
            dq = None
            for b, (_, masked) in enumerate(blocks):
                fk = fc_ref[0, rows[b], :]
                prs, dss = [], []
                for hh in range(2):
                    blk = slice(hh * tq, (hh + 1) * tq)
                    sc = zs[b][blk] + (ft[hh:hh + 1, :] - fk[:, hh:hh + 1])
                    pr = jnp.exp(sc - ls[hh:hh + 1, :])
                    if masked:
                        pr = jnp.where(causal, pr, 0.0)
                    dsb = (pr * (dps[b][blk] - dsum[hh])).astype(BF16)
                    rs[hh] = rs[hh] + jnp.sum(dsb.astype(F32), axis=0, keepdims=True)
                    prs.append(pr.astype(BF16))
                    dss.append(dsb)
                dv_ref[rows[b], :] += _dot(jnp.concatenate(prs, axis=1), do2)
                both = _dot(jnp.concatenate(dss, axis=1), q2_pick)
                dk_ref[rows[b], :] += both[:, :LANES]
                dfk_ref[0, rows[b], :] -= both[:, LANES:]
                term = _dot(_stack_heads_t(kt_ref[:, rows[b]]), jnp.concatenate(dss, axis=0))
                dq = term if dq is None else dq + term
            rs_s[0], rs_s[1] = rs
            dq_s[...] += dq

        _walk_blocks(i, step, descending=False, group=2)
        dq_ref[...] = dq_s[...].T
        dfq_ref[0, 0:1, :] = rs_s[0]
        dfq_ref[0, 1:2, :] = rs_s[1]

    q_nat, q_t, k_nat, k_t = _attn_specs(s, tq, 0)
    qstat = pl.BlockSpec((1, 2, tq), lambda p, i: (p, 0, i))
    otile = pl.BlockSpec((tq, LANES), lambda p, i: (i, p))
    oresident = pl.BlockSpec((s, LANES), lambda p, i: (0, p))
    out = jax.ShapeDtypeStruct((s, MIX // 2), F32)
    return pl.pallas_call(
        body, name=name, grid=(half, nq),
        in_specs=[q_nat, q_t, k_nat, k_t, k_nat, qstat, pl.BlockSpec((1, s, 2), lambda p, i: (p, 0, 0)), qstat,
                  otile, q_nat],
        out_specs=[otile, oresident, oresident, pl.BlockSpec((1, s, LANES), lambda p, i: (p, 0, 0)), qstat],
        out_shape=[out, out, out, jax.ShapeDtypeStruct((half, s, LANES), F32),
                   jax.ShapeDtypeStruct((half, 2, s), F32)],
        scratch_shapes=[pltpu.VMEM((LANES, tq), F32), pltpu.VMEM((2, 1, tq), F32)],
        compiler_params=_params("parallel", "arbitrary"),
    )(qh, qt, kh, kt, vb, frow, fcol, lse, o, do)


def _log_sig_pair(z):
    lb = jnp.minimum(z, 0.0) - jnp.log(1.0 + jnp.exp(-jnp.abs(z)))
    return lb, lb - z


def _sb_fwd(qt, kh, vt, name, tq=256):
    s = kh.shape[0]
    tq = _tile(s, tq)
    nq, half = s // tq, MIX // LANES // 2

    def body(qt_ref, k_ref, vt_ref, o_ref, tot_ref, c_s, acc_s):
        i = pl.program_id(1)
        qt_v = qt_ref[...]
        key, qry = _key_query_iotas(tq)
        strict = key < qry
        later = _tri(tq, lambda a, b: b > a)
        c_s[...] = jnp.zeros_like(c_s)
        acc_s[...] = jnp.zeros_like(acc_s)

        def step(blocks):
            rows = [pl.ds(pl.multiple_of(j * tq, tq), tq) for j, _ in blocks]
            zs = [_dot(_stack_heads(k_ref[r, :]), qt_v) for r in rows]
            lbs, loms, afters = [], [], []
            for b, (_, masked) in enumerate(blocks):
                for hh in range(2):
                    lb, lom = _log_sig_pair(zs[b][hh * tq:(hh + 1) * tq])
                    if masked:
                        lom = jnp.where(strict, lom, 0.0)
                    lbs.append(lb)
                    loms.append(lom)
                afters.append(_split_dot_left(later, jnp.concatenate(loms[2 * b:2 * b + 2], axis=1), 2))
            carry = [c_s[0], c_s[1]]
            pv = None
            for b, (_, masked) in enumerate(blocks):
                ws = []
                for hh in range(2):
                    n = 2 * b + hh
                    w = jnp.exp(lbs[n] + afters[b][:, hh * tq:(hh + 1) * tq] + carry[hh])
                    if masked:
                        w = jnp.where(strict, w, 0.0)
                    ws.append(w.astype(BF16))
                    carry[hh] = carry[hh] + jnp.sum(loms[n], axis=0, keepdims=True)
                term = _dot(_stack_heads_t(vt_ref[:, rows[b]]), jnp.concatenate(ws, axis=0))
                pv = term if pv is None else pv + term
            c_s[0], c_s[1] = carry
            acc_s[...] += pv

        _walk_blocks(i, step, descending=True, group=4)
        o_ref[...] = acc_s[...].T
        tot_ref[0, 0:1, :] = c_s[0]
        tot_ref[0, 1:2, :] = c_s[1]

    _, q_t, k_nat, k_t = _attn_specs(s, tq, half)
    qstat = pl.BlockSpec((1, 2, tq), lambda p, i: (p, 0, i))
    return pl.pallas_call(
        body, name=name, grid=(half, nq),
        in_specs=[q_t, k_nat, k_t],
        out_specs=[pl.BlockSpec((tq, LANES), lambda p, i: (i, p)), qstat],
        out_shape=[jax.ShapeDtypeStruct((s, MIX // 2), F32), jax.ShapeDtypeStruct((half, 2, s), F32)],
        scratch_shapes=[pltpu.VMEM((2, 1, tq), F32), pltpu.VMEM((LANES, tq), F32)],
        compiler_params=_params("parallel", "arbitrary"),
    )(qt, kh, vt)


def _sb_bwd(qh, qt, kh, kt, vb, tot, do, name, tq=256):
    s = kh.shape[0]
    tq = _tile(s, tq)
    nq, half = s // tq, MIX // LANES // 2

    def body(q_ref, qt_ref, k_ref, kt_ref, v_ref, tot_ref, do_ref, dq_ref, dk_ref, dv_ref, rem_s, pg_s, dq_s):
        i = pl.program_id(1)

        @pl.when(i == 0)
        def _():
            dk_ref[...] = jnp.zeros_like(dk_ref)
            dv_ref[...] = jnp.zeros_like(dv_ref)

        q2 = _stack_heads(q_ref[...])
        qt_v = qt_ref[...]
        do_v = do_ref[...]
        do2 = _stack_heads(do_v.astype(BF16))
        dot_v = do_v.T.astype(BF16)
        key, qry = _key_query_iotas(tq)
        strict = key < qry
        upto = _tri(tq, lambda a, b: b <= a)
        before = _tri(tq, lambda a, b: b < a)
        tv = tot_ref[0]
        rem_s[0] = tv[0:1, :]
        rem_s[1] = tv[1:2, :]
        pg_s[...] = jnp.zeros_like(pg_s)
        dq_s[...] = jnp.zeros_like(dq_s)

        def step(blocks):
            nb = len(blocks)
            rows = [pl.ds(pl.multiple_of(j * tq, tq), tq) for j, _ in blocks]
            zs = [_dot(_stack_heads(k_ref[r, :]), qt_v) for r in rows]
            dws = [_dot(_stack_heads(v_ref[r, :]), dot_v) for r in rows]
            lbs, loms, prefixes = [], [], []
            for b, (_, masked) in enumerate(blocks):
                for hh in range(2):
                    lb, lom = _log_sig_pair(zs[b][hh * tq:(hh + 1) * tq])
                    if masked:
                        lom = jnp.where(strict, lom, 0.0)
                    lbs.append(lb)
                    loms.append(lom)
                prefixes.append(_split_dot_left(upto, jnp.concatenate(loms[2 * b:2 * b + 2], axis=1), 2))
            rem = [rem_s[0], rem_s[1]]
            ws, gs, gpres = [], [], []
            for b, (_, masked) in enumerate(blocks):
                for hh in range(2):
                    n = 2 * b + hh
                    blk = slice(hh * tq, (hh + 1) * tq)
                    w = jnp.exp(lbs[n] + (rem[hh] - prefixes[b][:, blk]))
                    if masked:
                        w = jnp.where(strict, w, 0.0)
                    gs.append(dws[b][blk] * w)
                    ws.append(w.astype(BF16))
                    rem[hh] = rem[hh] - jnp.sum(loms[n], axis=0, keepdims=True)
                gpres.append(_dot(before, jnp.concatenate(gs[2 * b:2 * b + 2], axis=1).astype(BF16)))
                dv_ref[rows[b], :] += _dot(jnp.concatenate(ws[2 * b:2 * b + 2], axis=1), do2)
            rem_s[0], rem_s[1] = rem
            pg = [pg_s[0], pg_s[1]]
            dq = None
            for b, (_, masked) in enumerate(blocks):
                dzs = []
                for hh in range(2):
                    n = 2 * b + hh
                    g = gs[n]
                    dz = g - jnp.exp(lbs[n]) * (g + (pg[hh] + gpres[b][:, hh * tq:(hh + 1) * tq]))
                    if masked:
                        dz = jnp.where(strict, dz, 0.0)
                    dzs.append(dz.astype(BF16))
                    pg[hh] = pg[hh] + jnp.sum(g, axis=0, keepdims=True)
                dk_ref[rows[b], :] += _dot(jnp.concatenate(dzs, axis=1), q2)
                term = _dot(_stack_heads_t(kt_ref[:, rows[b]]), jnp.concatenate(dzs, axis=0))
                dq = term if dq is None else dq + term
            pg_s[0], pg_s[1] = pg
            dq_s[...] += dq

        _walk_blocks(i, step, descending=False, group=2)
        dq_ref[...] = dq_s[...].T

    q_nat, q_t, k_nat, k_t = _attn_specs(s, tq, half)
    qstat = pl.BlockSpec((1, 2, tq), lambda p, i: (p, 0, i))
    otile = pl.BlockSpec((tq, LANES), lambda p, i: (i, p))
    oresident = pl.BlockSpec((s, LANES), lambda p, i: (0, p))
    out = jax.ShapeDtypeStruct((s, MIX // 2), F32)
    return pl.pallas_call(
        body, name=name, grid=(half, nq),
        in_specs=[q_nat, q_t, k_nat, k_t, k_nat, qstat, q_nat],
        out_specs=[otile, oresident, oresident], out_shape=[out, out, out],
        scratch_shapes=[pltpu.VMEM((2, 1, tq), F32), pltpu.VMEM((2, 1, tq), F32), pltpu.VMEM((LANES, tq), F32)],
        compiler_params=_params("parallel", "arbitrary"),
    )(qh, qt, kh, kt, vb, tot, do)


def _loss_head(y, target, name, ts=512):
    s, d = y.shape
    ts = _tile(s, ts, 8)
    nt = s // ts

    def body(y_ref, t_ref, dy_ref, l_ref, acc):
        i = pl.program_id(0)
        err = y_ref[...] - t_ref[...]
        dy_ref[...] = err * (1.0 / d)

        @pl.when(i == 0)
        def _():
            acc[...] = jnp.zeros_like(acc)

        acc[...] += jnp.sum(err * err, axis=0, keepdims=True)

        @pl.when(i == nt - 1)
        def _():
            tot = jnp.sum(acc[...], keepdims=True).reshape(1, 1) * (0.5 / d)
            l_ref[...] = jnp.broadcast_to(tot, l_ref.shape)

    row = pl.BlockSpec((ts, d), lambda i: (i, 0))
    dy, l = pl.pallas_call(
        body, name=name, grid=(nt,), in_specs=[row, row],
        out_specs=[row, pl.BlockSpec((8, LANES), lambda i: (0, 0))],
        out_shape=[jax.ShapeDtypeStruct((s, d), F32), jax.ShapeDtypeStruct((8, LANES), F32)],
        scratch_shapes=[pltpu.VMEM((1, d), F32)],
        compiler_params=_params("arbitrary"),
    )(y, target)
    return l[0, 0], dy


def _coords():
    return lax.axis_index("x"), lax.axis_index("y"), lax.axis_index("c")


def _other_chips(xi, yi):
    return [(1 - xi, yi), (xi, 1 - yi), (1 - xi, 1 - yi)]


def _all_gather(xs, name):
    n = len(xs)

    def body(*refs):
        x_refs, out_refs = refs[:n], refs[n:2 * n]
        send_sems, recv_sems, local_sems = refs[2 * n:]
        xi, yi, ci = _coords()
        me, sibling = (xi, yi, ci), (xi, yi, 1 - ci)
        chips = _other_chips(xi, yi)

        def slot(a, px, py, pc):
            return out_refs[a].at[4 * px + 2 * py + pc]

        def copy(a, k, block, to, src=None):
            return pltpu.make_async_remote_copy(
                src_ref=slot(a, *block) if src is None else src, dst_ref=slot(a, *block),
                send_sem=send_sems.at[a, k], recv_sem=recv_sems.at[a, k], device_id=to, device_id_type=MESH)

        mine = [pltpu.make_async_copy(x_refs[a], slot(a, *me), local_sems.at[a]) for a in range(n)]
        for cp in mine:
            cp.start()
        first = []
        for a in range(n):
            first.append(copy(a, 0, me, sibling, src=x_refs[a]))
            first += [copy(a, 1 + j, me, (*chip, ci), src=x_refs[a]) for j, chip in enumerate(chips)]
        for cp in first:
            cp.start()
        passed = []
        for j, chip in enumerate(chips):
            for a in range(n):
                copy(a, 1 + j, (*chip, ci), me).wait_recv()
                fwd = copy(a, 4 + j, (*chip, ci), sibling)
                fwd.start()
                passed.append(fwd)
        for a in range(n):
            copy(a, 0, sibling, me).wait_recv()
            for j, chip in enumerate(chips):
                copy(a, 4 + j, (*chip, 1 - ci), me).wait_recv()
        for cp in first + passed:
            cp.wait_send()
        for cp in mine:
            cp.wait()

    return pl.pallas_call(
        body, name=name, out_shape=[jax.ShapeDtypeStruct((N_DEV,) + x.shape, x.dtype) for x in xs],
        in_specs=[ANY] * n, out_specs=[ANY] * n,
        scratch_shapes=[pltpu.SemaphoreType.DMA((n, 7)), pltpu.SemaphoreType.DMA((n, 7)),
                        pltpu.SemaphoreType.DMA((n,))],
    )(*xs)


def _sibling_exchange(gs, name):
    n = len(gs)

    def body(*refs):
        g_refs, recv_refs = refs[:n], refs[n:2 * n]
        send_sems, recv_sems = refs[2 * n:]
        xi, yi, ci = _coords()
        cps = [pltpu.make_async_remote_copy(
            src_ref=g_refs[a].at[2 * chip + (1 - ci)], dst_ref=recv_refs[a].at[chip],
            send_sem=send_sems.at[a, chip], recv_sem=recv_sems.at[a, chip],
            device_id=(xi, yi, 1 - ci), device_id_type=MESH) for a in range(n) for chip in range(N_CHIP)]
        for cp in cps:
            cp.start()
        for cp in cps:
            cp.wait()

    return pl.pallas_call(
        body, name=name, out_shape=[jax.ShapeDtypeStruct((N_CHIP,) + g.shape[1:], g.dtype) for g in gs],
        in_specs=[ANY] * n, out_specs=[ANY] * n,
        scratch_shapes=[pltpu.SemaphoreType.DMA((n, N_CHIP)), pltpu.SemaphoreType.DMA((n, N_CHIP))],
    )(*gs)


def _pair_add(g, recv, ids, name, tr=256):
    _, r, c = g.shape
    tr = _tile(r, tr, 16)

    def body(ids_ref, g_ref, r_ref, p_ref, own_ref):
        kk = pl.program_id(1)
        tot = g_ref[0].astype(F32) + r_ref[0].astype(F32)
        p_ref[0] = tot.astype(BF16)

        @pl.when(kk == ids_ref[1])
        def _():
            own_ref[...] = tot

    grid_spec = pltpu.PrefetchScalarGridSpec(
        num_scalar_prefetch=1, grid=(r // tr, N_CHIP),
        in_specs=[pl.BlockSpec((1, tr, c), lambda i, kk, ids: (2 * kk + ids[0], i, 0)),
                  pl.BlockSpec((1, tr, c), lambda i, kk, ids: (kk, i, 0))],
        out_specs=[pl.BlockSpec((1, tr, c), lambda i, kk, ids: (kk, i, 0)),
                   pl.BlockSpec((tr, c), lambda i, kk, ids: (i, 0))])
    return pl.pallas_call(
        body, name=name, grid_spec=grid_spec,
        out_shape=[jax.ShapeDtypeStruct((N_CHIP, r, c), BF16), jax.ShapeDtypeStruct((r, c), F32)],
        compiler_params=_params("parallel", "arbitrary"),
    )(ids, g, recv)


def _chip_exchange(ps, name):
    n = len(ps)

    def body(*refs):
        p_refs, recv_refs = refs[:n], refs[n:2 * n]
        send_sems, recv_sems = refs[2 * n:]
        xi, yi, ci = _coords()
        mine = 2 * xi + yi
        chips = _other_chips(xi, yi)

        def copy(a, k, cx, cy):
            return pltpu.make_async_remote_copy(
                src_ref=p_refs[a].at[2 * cx + cy], dst_ref=recv_refs[a].at[mine],
                send_sem=send_sems.at[a, k], recv_sem=recv_sems.at[a, k],
                device_id=(cx, cy, ci), device_id_type=MESH)

        def landed(a, k, cx, cy):
            return pltpu.make_async_remote_copy(
                src_ref=p_refs[a].at[mine], dst_ref=recv_refs[a].at[2 * cx + cy],
                send_sem=send_sems.at[a, k], recv_sem=recv_sems.at[a, k],
                device_id=(cx, cy, ci), device_id_type=MESH)

        sends = [copy(a, k, cx, cy) for a in range(n) for k, (cx, cy) in enumerate(chips)]
        for cp in sends:
            cp.start()
        for a in range(n):
            for k, (cx, cy) in enumerate(chips):
                landed(a, k, cx, cy).wait_recv()
        for cp in sends:
            cp.wait_send()

    return pl.pallas_call(
        body, name=name, out_shape=[jax.ShapeDtypeStruct(p.shape, p.dtype) for p in ps],
        in_specs=[ANY] * n, out_specs=[ANY] * n,
        scratch_shapes=[pltpu.SemaphoreType.DMA((n, 3)), pltpu.SemaphoreType.DMA((n, 3))],
    )(*ps)


def _adamw_math(w, g, m, v):
    m = ADAM_B1 * m + (1.0 - ADAM_B1) * g
    v = ADAM_B2 * v + (1.0 - ADAM_B2) * (g * g)
    m_hat = m / (1.0 - ADAM_B1 ** ADAM_STEP)
    v_hat = v / (1.0 - ADAM_B2 ** ADAM_STEP)
    delta = -ADAM_LR * (m_hat / (jnp.sqrt(v_hat) + ADAM_EPS) + ADAM_WD * w)
    return delta, m, v


def _adamw_reduce(own, recv, ids, w, m, v, name, tr=256):
    r, c = w.shape
    tr = _tile(r, tr, 16)

    def body(ids_ref, own_ref, recv_ref, w_ref, m_ref, v_ref, g_ref, d_ref, mo_ref, vo_ref):
        mine = ids_ref[1]
        g = None
        for kk in range(N_CHIP):
            term = jnp.where(mine == kk, own_ref[...], recv_ref[kk].astype(F32))
            g = term if g is None else g + term
        delta, m_new, v_new = _adamw_math(w_ref[...], g, m_ref[...], v_ref[...])
        g_ref[...] = g
        d_ref[...] = delta
        mo_ref[...] = m_new
        vo_ref[...] = v_new

    row = pl.BlockSpec((tr, c), lambda i, ids: (i, 0))
    grid_spec = pltpu.PrefetchScalarGridSpec(
        num_scalar_prefetch=1, grid=(r // tr,),
        in_specs=[row, pl.BlockSpec((N_CHIP, tr, c), lambda i, ids: (0, i, 0)), row, row, row],
        out_specs=[row, row, row, row])
    out = jax.ShapeDtypeStruct((r, c), F32)
    return pl.pallas_call(
        body, name=name, grid_spec=grid_spec, out_shape=[out, out, out, out],
        compiler_params=_params("parallel"),
    )(ids, own, recv, w, m, v)


def _sum_sources(a, name):
    n, r, c = a.shape

    def body(a_ref, o_ref):
        tot = a_ref[0]
        for kk in range(1, n):
            tot = tot + a_ref[kk]
        o_ref[...] = tot

    return pl.pallas_call(
        body, name=name, out_shape=jax.ShapeDtypeStruct((r, c), F32),
        in_specs=[pl.BlockSpec(memory_space=pltpu.VMEM)], out_specs=pl.BlockSpec(memory_space=pltpu.VMEM),
    )(a)


def _adamw_small(w, g, m, v, name):
    def body(w_ref, g_ref, m_ref, v_ref, d_ref, mo_ref, vo_ref):
        delta, m_new, v_new = _adamw_math(w_ref[...], g_ref[...], m_ref[...], v_ref[...])
        d_ref[...] = delta
        mo_ref[...] = m_new
        vo_ref[...] = v_new

    vm = pl.BlockSpec(memory_space=pltpu.VMEM)
    out = jax.ShapeDtypeStruct(w.shape, F32)
    return pl.pallas_call(body, name=name, out_shape=[out, out, out], in_specs=[vm] * 4, out_specs=[vm] * 3)(w, g, m, v)


def _pack(parts, width, row_mult):
    flat = jnp.concatenate([p.reshape(-1) for p in parts])
    rows = -(-flat.shape[0] // width)
    rows = -(-rows // row_mult) * row_mult
    return jnp.pad(flat, (0, rows * width - flat.shape[0])).reshape(rows, width)


def _unpack(flat, shapes):
    out, off = [], 0
    lead = flat.shape[:-1]
    for shp in shapes:
        n = 1
        for dd in shp:
            n *= dd
        out.append(flat[..., off:off + n].reshape(lead + tuple(shp)))
        off += n
    return out


def _rows2d(w):
    return w.reshape(w.shape[0] * w.shape[1], w.shape[2])


def _cols_to_dev(g):
    l, k, n = g.shape
    return g.reshape(l * k, N_DEV, n // N_DEV).transpose(1, 0, 2)


def _rows_to_dev(g):
    l, k, n = g.shape
    rs = k // N_DEV
    return g.reshape(l, N_DEV, rs, n).transpose(1, 0, 2, 3).reshape(N_DEV, l * rs, n)


def _dev_to_cols(a, l):
    _, lk, cs = a.shape
    return a.transpose(1, 0, 2).reshape(l, lk // l, N_DEV * cs)


def _dev_to_rows(a, l):
    _, lr, n = a.shape
    rs = lr // l
    return a.reshape(N_DEV, l, rs, n).transpose(1, 0, 2, 3).reshape(l, N_DEV * rs, n)


def kernel(x, attn_norm, attn_w_in, attn_f_bias, fox_q_gain, fox_k_gain, sb_q_gain, sb_k_gain, attn_w_out, conv_norm, conv_w_in, conv_kernel, conv_w_out, ffn_norm, ffn_w_up, ffn_conv, ffn_w_down, loss_target, m_attn_norm, m_attn_w_in, m_attn_f_bias, m_fox_q_gain, m_fox_k_gain, m_sb_q_gain, m_sb_k_gain, m_attn_w_out, m_conv_norm, m_conv_w_in, m_conv_kernel, m_conv_w_out, m_ffn_norm, m_ffn_w_up, m_ffn_conv, m_ffn_w_down, v_attn_norm, v_attn_w_in, v_attn_f_bias, v_fox_q_gain, v_fox_k_gain, v_sb_q_gain, v_sb_k_gain, v_attn_w_out, v_conv_norm, v_conv_w_in, v_conv_kernel, v_conv_w_out, v_ffn_norm, v_ffn_w_up, v_ffn_conv, v_ffn_w_down):
    s = x.shape[1]
    n_attn, n_conv, depth = attn_w_in.shape[0], conv_w_in.shape[0], ffn_w_up.shape[0]
    xi, yi, ci = _coords()
    me = 4 * xi + 2 * yi + ci
    ids = jnp.stack([ci, 2 * xi + yi]).astype(jnp.int32)

    big = [attn_w_in, attn_w_out, conv_w_in, conv_w_out, ffn_w_up, ffn_w_down]
    big_m = [m_attn_w_in, m_attn_w_out, m_conv_w_in, m_conv_w_out, m_ffn_w_up, m_ffn_w_down]
    big_v = [v_attn_w_in, v_attn_w_out, v_conv_w_in, v_conv_w_out, v_ffn_w_up, v_ffn_w_down]
    big_names = ["attn_w_in", "attn_w_out", "conv_w_in", "conv_w_out", "ffn_w_up", "ffn_w_down"]
    small_sh = [conv_norm, conv_kernel, ffn_conv]
    small_sh_shapes = [w.shape for w in small_sh]
    rep = [attn_norm, attn_f_bias, fox_q_gain, fox_k_gain, sb_q_gain, sb_k_gain, ffn_norm]
    rep_shapes = [w.shape for w in rep]

    small_pack = _pack(small_sh, LANES, 8)
    gathered = _all_gather([_rows2d(w).astype(BF16) for w in big] + [small_pack], "gather_weights")
    a_w_in = _dev_to_cols(gathered[0], n_attn)
    a_w_in = jnp.pad(a_w_in, ((0, 0), (0, 0), (0, ATTN_IN_PAD - ATTN_IN)))
    a_w_out = _dev_to_rows(gathered[1], n_attn)
    c_w_in = _dev_to_cols(gathered[2], n_conv)
    c_w_out = _dev_to_rows(gathered[3], n_conv)
    f_w_up = _dev_to_cols(gathered[4], depth)
    f_w_down = _dev_to_rows(gathered[5], depth)
    cn, ckern, fconv = _unpack(gathered[6].reshape(N_DEV, -1), small_sh_shapes)
    conv_norm_f = cn.transpose(1, 0, 2).reshape(n_conv, D_MODEL)
    conv_kernel_f = ckern.transpose(1, 2, 0, 3).reshape(n_conv, 3, D_MODEL)
    ffn_conv_f = fconv.transpose(1, 2, 0, 3).reshape(depth, 3, 2 * D_FF)

    def pair_gain(fox_g, sb_g):
        f2 = jnp.concatenate([fox_g, fox_g])
        s2 = jnp.concatenate([sb_g, sb_g])
        return jnp.concatenate([jnp.tile(f2[None], (4, 1)), jnp.tile(s2[None], (4, 1))])[:, None, :]

    h = x[0]
    saved = []
    for layer in range(depth):
        i = layer // 2
        tag = "l%d" % layer
        rec = {"h_in": h}
        if layer % 2 == 0:
            xn = _rms_fwd(h, attn_norm[i], tag + "_attn_rms")
            proj = _matmul(xn, a_w_in[i], tag + "_attn_in", tn=640)
            gq, gk = pair_gain(fox_q_gain[i], sb_q_gain[i]), pair_gain(fox_k_gain[i], sb_k_gain[i])
            qh, kh, vb, qt, kt, vt = _qkv_prep(proj, gq, gk, tag + "_qkv_prep")
            logit = proj[:, 3 * MIX:3 * MIX + H_FOX].T.reshape(H_FOX, s // LANES, LANES)
            cum = _fgate_fwd(logit, attn_f_bias[i], tag + "_fgate")
            frow = cum.reshape(H_FOX // 2, 2, s)
            fcol = frow.transpose(0, 2, 1)
            o_fox, lse = _fox_fwd(qt, kh, vt, frow, fcol, tag + "_fox_fwd")
            o_sb, tot = _sb_fwd(qt, kh, vt, tag + "_sb_fwd")
            o = jnp.concatenate([o_fox, o_sb], axis=1)
            h = _matmul(o, a_w_out[i], tag + "_attn_out", add=h)
            rec.update(xn=xn, proj=proj, gq=gq, gk=gk, qh=qh, kh=kh, vb=vb, qt=qt, kt=kt, logit=logit, frow=frow,
                       fcol=fcol, o_fox=o_fox, lse=lse, tot=tot, o=o)
        else:
            xn = _rms_fwd(h, conv_norm_f[i], tag + "_conv_rms")
            proj = _matmul(xn, c_w_in[i], tag + "_conv_in", tn=1024)
            y = _sconv_fwd(proj, conv_kernel_f[i], tag + "_sconv_fwd")
            h = _matmul(y, c_w_out[i], tag + "_conv_out", add=h)
            rec.update(xn=xn, proj=proj, y=y)
        rec["h_mid"] = h
        xn2 = _rms_fwd(h, ffn_norm[layer], tag + "_ffn_rms")
        up = _matmul(xn2, f_w_up[layer], tag + "_ffn_up", tn=1408)
        act = _ffn_act_fwd(up, ffn_conv_f[layer], tag + "_ffn_act")
        h = _matmul(act, f_w_down[layer], tag + "_ffn_down", add=h, tk=2816)
        rec.update(xn2=xn2, up=up, act=act)
        saved.append(rec)

    loss_local, dh = _loss_head(h, loss_target[0], "loss_head")
    loss = lax.psum(loss_local, ("x", "y", "c"))

    g_attn_norm, g_attn_w_in, g_f_bias = [None] * n_attn, [None] * n_attn, [None] * n_attn
    g_fq, g_fk, g_sq, g_sk, g_attn_w_out = ([None] * n_attn for _ in range(5))
    g_conv_norm, g_conv_w_in, g_conv_kernel, g_conv_w_out = ([None] * n_conv for _ in range(4))
    g_ffn_norm, g_ffn_w_up, g_ffn_conv, g_ffn_w_down = ([None] * depth for _ in range(4))

    for layer in reversed(range(depth)):
        i = layer // 2
        tag = "l%d" % layer
        rec = saved[layer]
        da = _matmul(dh, f_w_down[layer].T, tag + "_ffn_down_dx", tn=1408)
        g_ffn_w_down[layer] = _matmul_tn(rec["act"], dh, tag + "_ffn_down_dw", tm=1408, tn=1024)
        dup, dwg, dwv = _ffn_act_bwd(rec["up"], ffn_conv_f[layer], da, tag + "_ffn_act_bwd")
        g_ffn_conv[layer] = jnp.concatenate([dwg, dwv], axis=1)
        g_ffn_w_up[layer] = _matmul_tn(rec["xn2"], dup, tag + "_ffn_up_dw", tn=1408)
        dxn = _matmul(dup, f_w_up[layer].T, tag + "_ffn_up_dx", tn=1024, tk=2816)
        dh, g_ffn_norm[layer] = _rms_bwd(rec["h_mid"], dxn, ffn_norm[layer], dh, tag + "_ffn_rms_bwd")
        if layer % 2 == 0:
            do = _matmul(dh, a_w_out[i].T, tag + "_attn_out_dx", tn=1024)
            g_attn_w_out[i] = _matmul_tn(rec["o"], dh, tag + "_attn_out_dw", tn=1024)
            dq_f, dk_f, dv_f, dfk, dfq = _fox_bwd(rec["qh"], rec["qt"], rec["kh"], rec["kt"], rec["vb"], rec["frow"],
                                                  rec["fcol"], rec["lse"], rec["o_fox"], do, tag + "_fox_bwd")
            dq_s, dk_s, dv_s = _sb_bwd(rec["qh"], rec["qt"], rec["kh"], rec["kt"], rec["vb"], rec["tot"], do,
                                       tag + "_sb_bwd")
            dq, dk, dv, dgq, dgk = _qkv_prep_bwd(rec["proj"], rec["gq"], rec["gk"], (dq_f, dq_s), (dk_f, dk_s),
                                                 (dv_f, dv_s), tag + "_qkv_prep_bwd")
            dcum = (dfq + dfk[:, :, 0:2].transpose(0, 2, 1)).reshape(H_FOX, s // LANES, LANES)
            dlogit, dbias = _fgate_bwd(rec["logit"], attn_f_bias[i], dcum, tag + "_fgate_bwd")
            g_f_bias[i] = dbias[:, 0]
            dgate = jnp.pad(dlogit.reshape(H_FOX, s).T, ((0, 0), (0, LANES - H_FOX))).astype(BF16)
            dproj = jnp.concatenate([dq, dk, dv, dgate], axis=1)

            def fold(dg):
                per_head = dg.reshape(16, HEAD_DIM)
                return jnp.sum(per_head[:8], axis=0), jnp.sum(per_head[8:], axis=0)

            g_fq[i], g_sq[i] = fold(dgq)
            g_fk[i], g_sk[i] = fold(dgk)
            g_attn_w_in[i] = _matmul_tn(rec["xn"], dproj, tag + "_attn_in_dw", tn=640)[:, :ATTN_IN]
            dxn = _matmul(dproj, a_w_in[i].T, tag + "_attn_in_dx", tn=1024, tk=3200)
            dh, g_attn_norm[i] = _rms_bwd(rec["h_in"], dxn, attn_norm[i], dh, tag + "_attn_rms_bwd")
        else:
            dy = _matmul(dh, c_w_out[i].T, tag + "_conv_out_dx", tn=1024)
            g_conv_w_out[i] = _matmul_tn(rec["y"], dh, tag + "_conv_out_dw", tn=1024)
            dproj, g_conv_kernel[i] = _sconv_bwd(rec["proj"], conv_kernel_f[i], dy, tag + "_sconv_bwd")
            g_conv_w_in[i] = _matmul_tn(rec["xn"], dproj, tag + "_conv_in_dw", tn=1024)
            dxn = _matmul(dproj, c_w_in[i].T, tag + "_conv_in_dx", tn=1024, tk=3072)
            dh, g_conv_norm[i] = _rms_bwd(rec["h_in"], dxn, conv_norm_f[i], dh, tag + "_conv_rms_bwd")
    grad_x = dh[None]

    gb = [_cols_to_dev(jnp.stack(g_attn_w_in).astype(BF16)), _rows_to_dev(jnp.stack(g_attn_w_out).astype(BF16)),
          _cols_to_dev(jnp.stack(g_conv_w_in).astype(BF16)), _rows_to_dev(jnp.stack(g_conv_w_out).astype(BF16)),
          _cols_to_dev(jnp.stack(g_ffn_w_up).astype(BF16)), _rows_to_dev(jnp.stack(g_ffn_w_down).astype(BF16))]
    from_sibling = _sibling_exchange(gb, "reduce_sibling")
    pairs = [_pair_add(g, r, ids, "reduce_pair_add_" + nm) for g, r, nm in zip(gb, from_sibling, big_names)]
    from_chips = _chip_exchange([p[0] for p in pairs], "reduce_chips")
    grads_big, delta_big, newm_big, newv_big = [], [], [], []
    for a, nm in enumerate(big_names):
        g_a, d_a, m_a, v_a = _adamw_reduce(pairs[a][1], from_chips[a], ids, _rows2d(big[a]), _rows2d(big_m[a]),
                                           _rows2d(big_v[a]), "adamw_" + nm)
        shp = big[a].shape
        grads_big.append(g_a.reshape(shp))
        delta_big.append(d_a.reshape(shp))
        newm_big.append(m_a.reshape(shp))
        newv_big.append(v_a.reshape(shp))

    rep_g = [jnp.stack(g_attn_norm), jnp.stack(g_f_bias), jnp.stack(g_fq), jnp.stack(g_fk), jnp.stack(g_sq),
             jnp.stack(g_sk), jnp.stack(g_ffn_norm)]
    sh_g = [jnp.stack(g_conv_norm).reshape(n_conv, N_DEV, -1).transpose(1, 0, 2),
            jnp.stack(g_conv_kernel).reshape(n_conv, 3, N_DEV, -1).transpose(2, 0, 1, 3),
            jnp.stack(g_ffn_conv).reshape(depth, 3, N_DEV, -1).transpose(2, 0, 1, 3)]
    n_rep = sum(int(a.size) for a in rep)
    n_sh = sum(int(a.size) for a in small_sh)
    partial = _pack(rep_g + [jnp.concatenate([a.reshape(N_DEV, -1) for a in sh_g], axis=1)], LANES, 8)
    total = _sum_sources(_all_gather([partial], "gather_small_grads")[0], "sum_small_grads").reshape(-1)
    rep_tot = total[:n_rep]
    sh_tot = lax.dynamic_slice_in_dim(total[n_rep:n_rep + N_DEV * n_sh].reshape(N_DEV, n_sh), me, 1, axis=0)[0]
    g_small = _pack([rep_tot, sh_tot], LANES, 8)

    def small_pack_of(rep_list, sh_list):
        return _pack(rep_list + sh_list, LANES, 8)

    d_small, m_small, v_small = _adamw_small(
        small_pack_of(rep, small_sh), g_small,
        small_pack_of([m_attn_norm, m_attn_f_bias, m_fox_q_gain, m_fox_k_gain, m_sb_q_gain, m_sb_k_gain, m_ffn_norm],
                      [m_conv_norm, m_conv_kernel, m_ffn_conv]),
        small_pack_of([v_attn_norm, v_attn_f_bias, v_fox_q_gain, v_fox_k_gain, v_sb_q_gain, v_sb_k_gain, v_ffn_norm],
                      [v_conv_norm, v_conv_kernel, v_ffn_conv]),
        "adamw_small")
    small_shapes = rep_shapes + small_sh_shapes

    def split_small(a):
        return _unpack(a.reshape(-1), small_shapes)

    def ordered(big_list, small_list):
        an, fb, fq, fk, sq, sk, fn, cno, cke, fco = small_list
        awi, awo, cwi, cwo, fwu, fwd = big_list
        return [an, awi, fb, fq, fk, sq, sk, awo, cno, cwi, cke, cwo, fn, fwu, fco, fwd]

    grads = ordered(grads_big, split_small(g_small))
    deltas = ordered(delta_big, split_small(d_small))
    new_m = ordered(newm_big, split_small(m_small))
    new_v = ordered(newv_big, split_small(v_small))
    return (loss, grad_x, *grads, *deltas, *new_m, *new_v)
```

```python
import jax
import jax.numpy as jnp
from jax import lax
from jax.experimental import pallas as pl
from jax.experimental.pallas import tpu as pltpu

F32 = jnp.float32
BF16 = jnp.bfloat16

D_MODEL = 1024
HEAD_DIM = 64
H_FOX = 8
MIX = 1024
ATTN_IN = 3 * MIX + H_FOX
ATTN_IN_PAD = 3 * MIX + 128
D_FF = 2816
EPS = 1e-6
NEG = -1e30
LANES = 128
N_DEV = 8
N_CHIP = 4

ADAM_LR = 0.001
ADAM_B1 = 0.9
ADAM_B2 = 0.999
ADAM_EPS = 1e-08
ADAM_WD = 0.01
ADAM_STEP = 10

VMEM_LIMIT = 56 * 1024 * 1024
MESH = pl.DeviceIdType.MESH
ANY = pl.BlockSpec(memory_space=pl.ANY)


def _params(*sem):
    return pltpu.CompilerParams(dimension_semantics=sem, vmem_limit_bytes=VMEM_LIMIT)


def _tile(n, target, mult=LANES):
    best = None
    for t in range(mult, min(n, target) + 1, mult):
        if n % t == 0:
            best = t
    return best if best is not None else n


def _dot(a, b):
    return jnp.dot(a, b, preferred_element_type=F32)


def _dot_tn(a, b):
    return lax.dot_general(a, b, (((0,), (0,)), ((), ())), preferred_element_type=F32)


def _split_dot(x, m, passes):
    acc = None
    rem = x
    for _ in range(passes):
        part = rem.astype(BF16)
        term = _dot(part, m)
        acc = term if acc is None else acc + term
        rem = rem - part.astype(F32)
    return acc


def _split_dot_left(m, x, passes):
    acc = None
    rem = x
    for _ in range(passes):
        part = rem.astype(BF16)
        term = _dot(m, part)
        acc = term if acc is None else acc + term
        rem = rem - part.astype(F32)
    return acc


def _matmul(a, b, name, add=None, out_dtype=F32, tm=1024, tn=512, tk=1024):
    split = a.shape[0] if a.ndim == 3 else 1
    m, kh = a.shape[-2:]
    k = split * kh
    n = b.shape[1]
    tm, tn, tk = _tile(m, tm, 8), _tile(n, tn), _tile(kh, tk)
    nk = k // tk
    per_slab = kh // tk
    has_add = add is not None

    def body(*refs):
        a_ref, b_ref = refs[0], refs[1]
        add_ref = refs[2] if has_add else None
        o_ref = refs[2 + has_add]

        def finish(acc):
            if has_add:
                acc = acc + add_ref[...]
            o_ref[...] = acc.astype(out_dtype)

        p = _dot(a_ref[...].astype(BF16), b_ref[...].astype(BF16))
        if nk == 1:
            finish(p)
        else:
            acc_ref = refs[-1]
            kk = pl.program_id(2)

            @pl.when(kk == 0)
            def _():
                acc_ref[...] = p

            @pl.when(kk > 0)
            def _():
                acc_ref[...] += p

            @pl.when(kk == nk - 1)
            def _():
                finish(acc_ref[...])

    if split == 1:
        a_spec = pl.BlockSpec((tm, tk), lambda i, j, kk: (i, kk))
    else:
        a_spec = pl.BlockSpec((None, tm, tk), lambda i, j, kk: (kk // per_slab, i, kk % per_slab))
    in_specs = [a_spec, pl.BlockSpec((tk, tn), lambda i, j, kk: (kk, j))]
    args = [a, b]
    if has_add:
        in_specs.append(pl.BlockSpec((tm, tn), lambda i, j, kk: (i, j)))
        args.append(add)
    return pl.pallas_call(
        body, name=name, grid=(m // tm, n // tn, nk), in_specs=in_specs,
        out_specs=pl.BlockSpec((tm, tn), lambda i, j, kk: (i, j)),
        out_shape=jax.ShapeDtypeStruct((m, n), out_dtype),
        scratch_shapes=[pltpu.VMEM((tm, tn), F32)] if nk > 1 else [],
        compiler_params=_params("parallel", "parallel", "arbitrary"),
    )(*args)


def _matmul_tn(a, b, name, tm=1024, tn=512, ts=2048):
    s, m = a.shape
    split = b.shape[0] if b.ndim == 3 else 1
    nh = b.shape[-1]
    n = split * nh
    tm, tn, ts = _tile(m, tm), _tile(nh, tn), _tile(s, ts, 8)
    per_slab = nh // tn
    if split == 1:
        b_spec = pl.BlockSpec((ts, tn), lambda i, j, kk: (kk, j))
    else:
        b_spec = pl.BlockSpec((None, ts, tn), lambda i, j, kk: (j // per_slab, kk, j % per_slab))

    def body(a_ref, b_ref, o_ref):
        kk = pl.program_id(2)
        p = _dot_tn(a_ref[...].astype(BF16), b_ref[...].astype(BF16))

        @pl.when(kk == 0)
        def _():
            o_ref[...] = p

        @pl.when(kk > 0)
        def _():
            o_ref[...] += p

    return pl.pallas_call(
        body, name=name, grid=(m // tm, n // tn, s // ts),
        in_specs=[pl.BlockSpec((ts, tm), lambda i, j, kk: (kk, i)), b_spec],
        out_specs=pl.BlockSpec((tm, tn), lambda i, j, kk: (i, j)),
        out_shape=jax.ShapeDtypeStruct((m, n), F32),
        compiler_params=_params("parallel", "parallel", "arbitrary"),
    )(a, b)


def _rms_fwd(h, g, name, ts=512):
    s, d = h.shape
    ts = _tile(s, ts, 8)

    def body(h_ref, g_ref, o_ref):
        x = h_ref[...]
        r = lax.rsqrt(jnp.mean(x * x, axis=-1, keepdims=True) + EPS)
        o_ref[...] = (x * r * g_ref[...]).astype(BF16)

    return pl.pallas_call(
        body, name=name, grid=(s // ts,),
        in_specs=[pl.BlockSpec((ts, d), lambda i: (i, 0)), pl.BlockSpec((1, d), lambda i: (0, 0))],
        out_specs=pl.BlockSpec((ts, d), lambda i: (i, 0)),
        out_shape=jax.ShapeDtypeStruct((s, d), BF16),
        compiler_params=_params("parallel"),
    )(h, g.reshape(1, d))


def _rms_bwd(h, dxn, g, dh_in, name, ts=512):
    s, d = h.shape
    ts = _tile(s, ts, 8)

    def body(h_ref, dxn_ref, g_ref, dhin_ref, dh_ref, dg_ref):
        i = pl.program_id(0)
        x = h_ref[...]
        r = lax.rsqrt(jnp.mean(x * x, axis=-1, keepdims=True) + EPS)
        xh = x * r
        dxn_v = dxn_ref[...]

        @pl.when(i == 0)
        def _():
            dg_ref[...] = jnp.zeros_like(dg_ref)

        dg_ref[0:1, :] += jnp.sum(dxn_v * xh, axis=0, keepdims=True)
        dxh = dxn_v * g_ref[...]
        dx = r * (dxh - xh * jnp.mean(dxh * xh, axis=-1, keepdims=True))
        dh_ref[...] = dhin_ref[...] + dx

    row = pl.BlockSpec((ts, d), lambda i: (i, 0))
    dh, dg = pl.pallas_call(
        body, name=name, grid=(s // ts,),
        in_specs=[row, row, pl.BlockSpec((1, d), lambda i: (0, 0)), row],
        out_specs=[row, pl.BlockSpec((8, d), lambda i: (0, 0))],
        out_shape=[jax.ShapeDtypeStruct((s, d), F32), jax.ShapeDtypeStruct((8, d), F32)],
        compiler_params=_params("arbitrary"),
    )(h, dxn, g.reshape(1, d), dh_in)
    return dh, dg[0]


def _shift_down(x, prev):
    rows = lax.broadcasted_iota(jnp.int32, (8, x.shape[1]), 0)
    p1, p2 = prev[7:8, :], prev[6:7, :]
    r1, r2 = pltpu.roll(x, 1, 0), pltpu.roll(x, 2, 0)
    top1 = jnp.where(rows == 0, p1, r1[0:8, :])
    top2 = jnp.where(rows == 0, p2, jnp.where(rows == 1, p1, r2[0:8, :]))
    if x.shape[0] == 8:
        return top1, top2
    return jnp.concatenate([top1, r1[8:, :]], axis=0), jnp.concatenate([top2, r2[8:, :]], axis=0)


def _shift_up(x, nxt):
    n = x.shape[0]
    rows = lax.broadcasted_iota(jnp.int32, (8, x.shape[1]), 0)
    n0, n1 = nxt[0:1, :], nxt[1:2, :]
    r1, r2 = pltpu.roll(x, n - 1, 0), pltpu.roll(x, n - 2, 0)
    end1 = jnp.where(rows == 7, n0, r1[n - 8:, :])
    end2 = jnp.where(rows == 7, n1, jnp.where(rows == 6, n0, r2[n - 8:, :]))
    return jnp.concatenate([r1[:n - 8, :], end1], axis=0), jnp.concatenate([r2[:n - 8, :], end2], axis=0)


def _conv(x, x1, x2, w):
    return w[2:3, :] * x + w[1:2, :] * x1 + w[0:1, :] * x2


def _halo_specs(ts, tc, col, n_time_blocks):
    r8 = ts // 8
    main = pl.BlockSpec((ts, tc), lambda j, i: (i, j + col))
    prev = pl.BlockSpec((8, tc), lambda j, i: (jnp.maximum(i * r8 - 1, 0), j + col))
    nxt = pl.BlockSpec((8, tc), lambda j, i: (jnp.minimum((i + 1) * r8, n_time_blocks * r8 - 1), j + col))
    return main, prev, nxt


def _silu_parts(g):
    sig = 1.0 / (1.0 + jnp.exp(-g))
    return sig, g * sig


def _ffn_act_fwd(up, cw, name, ts=256, tc=1408):
    s = up.shape[0]
    ts, tc = _tile(s, ts, 8), _tile(D_FF, tc)
    nc, nt = D_FF // tc, s // ts

    def body(g_ref, gp_ref, v_ref, vp_ref, wg_ref, wv_ref, o_ref):
        first = pl.program_id(1) == 0

        def conv(x_ref, p_ref, w_ref):
            x = x_ref[...]
            prev = jnp.where(first, 0.0, p_ref[...])
            x1, x2 = _shift_down(x, prev)
            return _conv(x, x1, x2, w_ref[...])

        ug = conv(g_ref, gp_ref, wg_ref)
        uv = conv(v_ref, vp_ref, wv_ref)
        _, silu = _silu_parts(ug)
        o_ref[...] = (silu * uv).astype(BF16)

    g_main, g_prev, _ = _halo_specs(ts, tc, 0, nt)
    v_main, v_prev, _ = _halo_specs(ts, tc, nc, nt)
    return pl.pallas_call(
        body, name=name, grid=(nc, nt),
        in_specs=[g_main, g_prev, v_main, v_prev,
                  pl.BlockSpec((3, tc), lambda j, i: (0, j)), pl.BlockSpec((3, tc), lambda j, i: (0, j + nc))],
        out_specs=pl.BlockSpec((ts, tc), lambda j, i: (i, j)),
        out_shape=jax.ShapeDtypeStruct((s, D_FF), BF16),
        compiler_params=_params("parallel", "parallel"),
    )(up, up, up, up, cw, cw)


def _ffn_act_bwd(up, cw, da, name, ts=256, tc=1408):
    s = up.shape[0]
    ts, tc = _tile(s, ts, 8), _tile(D_FF, tc)
    nc, nt = D_FF // tc, s // ts

    def body(g_ref, gp_ref, gn_ref, v_ref, vp_ref, vn_ref, da_ref, dan_ref, wg_ref, wv_ref,
             d_ref, dwg_ref, dwv_ref):
        i = pl.program_id(1)
        first, last = i == 0, i == nt - 1
        wg, wv = wg_ref[...], wv_ref[...]
        g, v = g_ref[...], v_ref[...]
        g1, g2 = _shift_down(g, jnp.where(first, 0.0, gp_ref[...]))
        v1, v2 = _shift_down(v, jnp.where(first, 0.0, vp_ref[...]))

        def d_u(ug, uv, da_v):
            sig, silu = _silu_parts(ug)
            return da_v * uv * (sig * (1.0 + ug * (1.0 - sig))), da_v * silu

        dug, duv = d_u(_conv(g, g1, g2, wg), _conv(v, v1, v2, wv), da_ref[...])
        gn, vn = gn_ref[...], vn_ref[...]
        gn1, gn2 = _shift_down(gn, g[ts - 8:, :])
        vn1, vn2 = _shift_down(vn, v[ts - 8:, :])
        dugn, duvn = d_u(_conv(gn, gn1, gn2, wg), _conv(vn, vn1, vn2, wv), dan_ref[...])
        dugn = jnp.where(last, 0.0, dugn)
        duvn = jnp.where(last, 0.0, duvn)

        def finish(du, dun, x, x1, x2, w, dx_ref, dw_ref):
            d1, d2 = _shift_up(du, dun)
            dx_ref[...] = (w[2:3, :] * du + w[1:2, :] * d1 + w[0:1, :] * d2).astype(BF16)

            @pl.when(first)
            def _():
                dw_ref[...] = jnp.zeros_like(dw_ref)

            dw_ref[0:1, :] += jnp.sum(du * x2, axis=0, keepdims=True)
            dw_ref[1:2, :] += jnp.sum(du * x1, axis=0, keepdims=True)
            dw_ref[2:3, :] += jnp.sum(du * x, axis=0, keepdims=True)

        finish(dug, dugn, g, g1, g2, wg, d_ref.at[0], dwg_ref)
        finish(duv, duvn, v, v1, v2, wv, d_ref.at[1], dwv_ref)

    g_specs = _halo_specs(ts, tc, 0, nt)
    v_specs = _halo_specs(ts, tc, nc, nt)
    da_main, _, da_next = _halo_specs(ts, tc, 0, nt)
    taps = pl.BlockSpec((8, tc), lambda j, i: (0, j))
    halves = pl.BlockSpec((2, ts, tc), lambda j, i: (0, i, j))
    d, dwg, dwv = pl.pallas_call(
        body, name=name, grid=(nc, nt),
        in_specs=[*g_specs, *v_specs, da_main, da_next,
                  pl.BlockSpec((3, tc), lambda j, i: (0, j)), pl.BlockSpec((3, tc), lambda j, i: (0, j + nc))],
        out_specs=[halves, taps, taps],
        out_shape=[jax.ShapeDtypeStruct((2, s, D_FF), BF16),
                   jax.ShapeDtypeStruct((8, D_FF), F32), jax.ShapeDtypeStruct((8, D_FF), F32)],
        compiler_params=_params("parallel", "arbitrary"),
    )(up, up, up, up, up, up, da, da, cw, cw)
    return d, dwg[:3], dwv[:3]


def _sconv_fwd(proj, ck, name, ts=256, tc=512):
    s = proj.shape[0]
    w = D_MODEL
    ts, tc = _tile(s, ts, 8), _tile(w, tc)
    nc, nt = w // tc, s // ts

    def body(b_ref, c_ref, cp_ref, u_ref, up_ref, w_ref, o_ref):
        first = pl.program_id(1) == 0
        cu = c_ref[...] * u_ref[...]
        cup = jnp.where(first, 0.0, cp_ref[...] * up_ref[...])
        x1, x2 = _shift_down(cu, cup)
        o_ref[...] = (b_ref[...] * _conv(cu, x1, x2, w_ref[...])).astype(BF16)

    b_main, _, _ = _halo_specs(ts, tc, 0, nt)
    c_main, c_prev, _ = _halo_specs(ts, tc, nc, nt)
    u_main, u_prev, _ = _halo_specs(ts, tc, 2 * nc, nt)
    return pl.pallas_call(
        body, name=name, grid=(nc, nt),
        in_specs=[b_main, c_main, c_prev, u_main, u_prev, pl.BlockSpec((3, tc), lambda j, i: (0, j))],
        out_specs=pl.BlockSpec((ts, tc), lambda j, i: (i, j)),
        out_shape=jax.ShapeDtypeStruct((s, w), BF16),
        compiler_params=_params("parallel", "parallel"),
    )(proj, proj, proj, proj, proj, ck)


def _sconv_bwd(proj, ck, dy, name, ts=256, tc=512):
    s = proj.shape[0]
    w = D_MODEL
    ts, tc = _tile(s, ts, 8), _tile(w, tc)
    nc, nt = w // tc, s // ts

    def body(b_ref, bn_ref, c_ref, cp_ref, u_ref, up_ref, dy_ref, dyn_ref, w_ref,
             d_ref, dw_ref):
        i = pl.program_id(1)
        first, last = i == 0, i == nt - 1
        wv = w_ref[...]
        b, c, u, dy_v = b_ref[...], c_ref[...], u_ref[...], dy_ref[...]
        cu = c * u
        cup = jnp.where(first, 0.0, cp_ref[...] * up_ref[...])
        x1, x2 = _shift_down(cu, cup)
        d_ref[0] = (dy_v * _conv(cu, x1, x2, wv)).astype(BF16)
        dcv = dy_v * b
        dcvn = jnp.where(last, 0.0, dyn_ref[...] * bn_ref[...])
        d1, d2 = _shift_up(dcv, dcvn)
        dcu = wv[2:3, :] * dcv + wv[1:2, :] * d1 + wv[0:1, :] * d2
        d_ref[1] = (dcu * u).astype(BF16)
        d_ref[2] = (dcu * c).astype(BF16)

        @pl.when(first)
        def _():
            dw_ref[...] = jnp.zeros_like(dw_ref)

        dw_ref[0:1, :] += jnp.sum(dcv * x2, axis=0, keepdims=True)
        dw_ref[1:2, :] += jnp.sum(dcv * x1, axis=0, keepdims=True)
        dw_ref[2:3, :] += jnp.sum(dcv * cu, axis=0, keepdims=True)

    b_main, _, b_next = _halo_specs(ts, tc, 0, nt)
    c_main, c_prev, _ = _halo_specs(ts, tc, nc, nt)
    u_main, u_prev, _ = _halo_specs(ts, tc, 2 * nc, nt)
    dy_main, _, dy_next = _halo_specs(ts, tc, 0, nt)
    d, dw = pl.pallas_call(
        body, name=name, grid=(nc, nt),
        in_specs=[b_main, b_next, c_main, c_prev, u_main, u_prev, dy_main, dy_next,
                  pl.BlockSpec((3, tc), lambda j, i: (0, j))],
        out_specs=[pl.BlockSpec((3, ts, tc), lambda j, i: (0, i, j)), pl.BlockSpec((8, tc), lambda j, i: (0, j))],
        out_shape=[jax.ShapeDtypeStruct((3, s, w), BF16), jax.ShapeDtypeStruct((8, w), F32)],
        compiler_params=_params("parallel", "arbitrary"),
    )(proj, proj, proj, proj, proj, proj, dy, dy, ck)
    return d, dw[:3]


def _low_lanes(shape):
    return lax.broadcasted_iota(jnp.int32, shape, 1) < HEAD_DIM


def _top_rows(shape):
    return lax.broadcasted_iota(jnp.int32, shape, 0) < HEAD_DIM


def _norm_pair(x):
    r = lax.rsqrt(_mean_pair(x * x) + EPS)
    return x * r, r


def _mean_pair(x):
    same_head = _tri(LANES, lambda a, b: a // HEAD_DIM == b // HEAD_DIM)
    return _split_dot(x, same_head, 3) * (1.0 / HEAD_DIM)


def _qkv_prep(proj, gq, gk, name, ts=512):
    s = proj.shape[0]
    ts = _tile(s, ts)
    npair = MIX // LANES
    scale = HEAD_DIM ** -0.5

    def body(q_ref, k_ref, v_ref, gq_ref, gk_ref, qo_ref, ko_ref, vo_ref, qt_ref, kt_ref, vt_ref):
        qn, _ = _norm_pair(q_ref[...])
        kn, _ = _norm_pair(k_ref[...])
        q = qn * gq_ref[0] * scale
        k = kn * gk_ref[0]
        v = v_ref[...]
        qo_ref[...] = q.astype(BF16)
        ko_ref[...] = k.astype(BF16)
        vo_ref[...] = v.astype(BF16)
        qt_ref[...] = q.T.astype(BF16)
        kt_ref[...] = k.T.astype(BF16)
        vt_ref[...] = v.T.astype(BF16)

    gain = pl.BlockSpec((1, 1, LANES), lambda i, p: (p, 0, 0))
    tile = pl.BlockSpec((ts, LANES), lambda i, p: (i, p))
    tile_t = pl.BlockSpec((LANES, ts), lambda i, p: (p, i))
    out = jax.ShapeDtypeStruct((s, MIX), BF16)
    out_t = jax.ShapeDtypeStruct((MIX, s), BF16)
    return pl.pallas_call(
        body, name=name, grid=(s // ts, npair),
        in_specs=[tile, pl.BlockSpec((ts, LANES), lambda i, p: (i, p + npair)),
                  pl.BlockSpec((ts, LANES), lambda i, p: (i, p + 2 * npair)), gain, gain],
        out_specs=[tile, tile, tile, tile_t, tile_t, tile_t], out_shape=[out, out, out, out_t, out_t, out_t],
        compiler_params=_params("parallel", "parallel"),
    )(proj, proj, proj, gq, gk)


def _qkv_prep_bwd(proj, gq, gk, dqs, dks, dvs, name, ts=512):
    s = proj.shape[0]
    ts = _tile(s, ts, 8)
    npair = MIX // LANES
    half = npair // 2
    scale = HEAD_DIM ** -0.5

    def body(q_ref, k_ref, gq_ref, gk_ref, dqf_ref, dqs_ref, dkf_ref, dks_ref, dvf_ref, dvs_ref,
             dq_ref, dk_ref, dv_ref, dgq_ref, dgk_ref):
        p, i = pl.program_id(0), pl.program_id(1)
        fox = p < half

        def one(x_ref, g_ref, df_ref, ds_ref, dx_ref, dg_ref, mult):
            dn = jnp.where(fox, df_ref[...], ds_ref[...]) * mult
            xh, r = _norm_pair(x_ref[...])

            @pl.when(i == 0)
            def _():
                dg_ref[...] = jnp.zeros_like(dg_ref)

            dg_ref[0, 0:1, :] += jnp.sum(dn * xh, axis=0, keepdims=True)
            dxh = dn * g_ref[0]
            dx_ref[...] = (r * (dxh - xh * _mean_pair(dxh * xh))).astype(BF16)

        one(q_ref, gq_ref, dqf_ref, dqs_ref, dq_ref, dgq_ref, scale)
        one(k_ref, gk_ref, dkf_ref, dks_ref, dk_ref, dgk_ref, 1.0)
        dv_ref[...] = jnp.where(fox, dvf_ref[...], dvs_ref[...]).astype(BF16)

    gain = pl.BlockSpec((1, 1, LANES), lambda p, i: (p, 0, 0))
    tile = pl.BlockSpec((ts, LANES), lambda p, i: (i, p))
    fpart = pl.BlockSpec((ts, LANES), lambda p, i: (i, jnp.minimum(p, half - 1)))
    spart = pl.BlockSpec((ts, LANES), lambda p, i: (i, jnp.maximum(p - half, 0)))
    dgain = pl.BlockSpec((1, 8, LANES), lambda p, i: (p, 0, 0))
    out = jax.ShapeDtypeStruct((s, MIX), BF16)
    gshape = jax.ShapeDtypeStruct((npair, 8, LANES), F32)
    dq, dk, dv, dgq, dgk = pl.pallas_call(
        body, name=name, grid=(npair, s // ts),
        in_specs=[tile, pl.BlockSpec((ts, LANES), lambda p, i: (i, p + npair)), gain, gain,
                  fpart, spart, fpart, spart, fpart, spart],
        out_specs=[tile, tile, tile, dgain, dgain], out_shape=[out, out, out, gshape, gshape],
        compiler_params=_params("parallel", "arbitrary"),
    )(proj, proj, gq, gk, dqs[0], dqs[1], dks[0], dks[1], dvs[0], dvs[1])
    return dq, dk, dv, dgq[:, 0, :], dgk[:, 0, :]


def _tri(n, rel):
    a = lax.broadcasted_iota(jnp.int32, (n, n), 0)
    b = lax.broadcasted_iota(jnp.int32, (n, n), 1)
    return rel(a, b).astype(BF16)


def _fgate_fwd(logit, bias, name):
    nh, r, _ = logit.shape

    def body(x_ref, b_ref, o_ref):
        within = _tri(LANES, lambda a, b: a <= b)
        before = _tri(r, lambda a, b: b < a)
        for hh in range(nh):
            x = x_ref[hh] + b_ref[hh]
            lf = jnp.minimum(x, 0.0) - jnp.log1p(jnp.exp(-jnp.abs(x)))
            c = _split_dot(lf, within, 3)
            tot = jnp.broadcast_to(c[:, LANES - 1:LANES], (r, LANES))
            o_ref[hh] = c + _split_dot_left(before, tot, 3)

    return pl.pallas_call(
        body, name=name, out_shape=jax.ShapeDtypeStruct((nh, r, LANES), F32),
        in_specs=[pl.BlockSpec(memory_space=pltpu.VMEM), pl.BlockSpec(memory_space=pltpu.SMEM)],
        out_specs=pl.BlockSpec(memory_space=pltpu.VMEM),
    )(logit, bias)


def _fgate_bwd(logit, bias, dcum, name):
    nh, r, _ = logit.shape

    def body(x_ref, b_ref, d_ref, dx_ref, db_ref):
        within = _tri(LANES, lambda a, b: a >= b)
        after = _tri(r, lambda a, b: b > a)
        for hh in range(nh):
            x = x_ref[hh] + b_ref[hh]
            d = d_ref[hh]
            c = _split_dot(d, within, 3)
            tot = jnp.broadcast_to(c[:, 0:1], (r, LANES))
            dlf = c + _split_dot_left(after, tot, 3)
            dx = dlf * (1.0 / (1.0 + jnp.exp(x)))
            dx_ref[hh] = dx
            db_ref[hh:hh + 1, :] = jnp.broadcast_to(jnp.sum(dx, keepdims=True).reshape(1, 1), (1, LANES))

    return pl.pallas_call(
        body, name=name,
        out_shape=[jax.ShapeDtypeStruct((nh, r, LANES), F32), jax.ShapeDtypeStruct((nh, LANES), F32)],
        in_specs=[pl.BlockSpec(memory_space=pltpu.VMEM), pl.BlockSpec(memory_space=pltpu.SMEM),
                  pl.BlockSpec(memory_space=pltpu.VMEM)],
        out_specs=[pl.BlockSpec(memory_space=pltpu.VMEM), pl.BlockSpec(memory_space=pltpu.VMEM)],
    )(logit, bias, dcum)


def _pair_masks(x):
    lo = _low_lanes(x.shape)
    zero = jnp.zeros_like(x)
    return jnp.where(lo, x, zero), jnp.where(lo, zero, x)


def _pair_masks_t(x):
    top = _top_rows(x.shape)
    zero = jnp.zeros_like(x)
    return jnp.where(top, x, zero), jnp.where(top, zero, x)


def _stack_heads(x):
    return jnp.concatenate(_pair_masks(x), axis=0)


def _stack_heads_t(x):
    return jnp.concatenate(_pair_masks_t(x), axis=1)


def _pair_colsum_t(x):
    top = _top_rows(x.shape)
    return (jnp.sum(jnp.where(top, x, 0.0), axis=0, keepdims=True),
            jnp.sum(jnp.where(top, 0.0, x), axis=0, keepdims=True))


def _key_query_iotas(t):
    return lax.broadcasted_iota(jnp.int32, (t, t), 0), lax.broadcasted_iota(jnp.int32, (t, t), 1)


def _walk_blocks(i, step, descending, group=2):
    full = i // group
    left = i - full * group

    def run(first, count):
        sign = -1 if descending else 1
        step([(first + sign * n, False) for n in range(count)])

    def leftovers():
        start = (left - 1) if descending else full * group
        sign = -1 if descending else 1
        if group == 4:
            @pl.when(left >= 2)
            def _():
                run(start, 2)

            @pl.when(left % 2 == 1)
            def _():
                run(0 if descending else i - 1, 1)
        else:
            @pl.when(left == 1)
            def _():
                run(start, 1)

    def loop(g, carry):
        run((i - 1 - group * g) if descending else group * g, group)
        return carry

    if descending:
        step([(i, True)])
        lax.fori_loop(0, full, loop, 0)
        leftovers()
    else:
        lax.fori_loop(0, full, loop, 0)
        leftovers()
        step([(i, True)])


def _attn_specs(s, tq, pair0):
    q_nat = pl.BlockSpec((tq, LANES), lambda p, i: (i, p + pair0))
    q_t = pl.BlockSpec((LANES, tq), lambda p, i: (p + pair0, i))
    k_nat = pl.BlockSpec((s, LANES), lambda p, i: (0, p + pair0))
    k_t = pl.BlockSpec((LANES, s), lambda p, i: (p + pair0, 0))
    return q_nat, q_t, k_nat, k_t


def _fox_fwd(qt, kh, vt, frow, fcol, name, tq=256):
    s = kh.shape[0]
    tq = _tile(s, tq)
    nq, half = s // tq, MIX // LANES // 2

    def body(qt_ref, k_ref, vt_ref, fr_ref, fc_ref, o_ref, lse_ref, m_s, l_s, acc_s):
        i = pl.program_id(1)
        qt_v = qt_ref[...]
        ft = fr_ref[0]
        key, qry = _key_query_iotas(tq)
        causal = key <= qry
        m_s[...] = jnp.full(m_s.shape, NEG, F32)
        l_s[...] = jnp.zeros_like(l_s)
        acc_s[...] = jnp.zeros_like(acc_s)

        top = _top_rows((LANES, tq))

        def step(blocks):
            rows = [pl.ds(pl.multiple_of(j * tq, tq), tq) for j, _ in blocks]
            zs = [_dot(_stack_heads(k_ref[r, :]), qt_v) for r in rows]
            m_cur, l_cur = [m_s[0], m_s[1]], [l_s[0], l_s[1]]
            acc = acc_s[...]
            for b, (_, masked) in enumerate(blocks):
                fk = fc_ref[0, rows[b], :]
                prs, alphas = [], []
                for hh in range(2):
                    sc = zs[b][hh * tq:(hh + 1) * tq] + (ft[hh:hh + 1, :] - fk[:, hh:hh + 1])
                    if masked:
                        sc = jnp.where(causal, sc, NEG)
                    m_new = jnp.maximum(m_cur[hh], jnp.max(sc, axis=0, keepdims=True))
                    alpha = jnp.exp(m_cur[hh] - m_new)
                    pr = jnp.exp(sc - m_new)
                    l_cur[hh] = alpha * l_cur[hh] + jnp.sum(pr, axis=0, keepdims=True)
                    m_cur[hh] = m_new
                    prs.append(pr.astype(BF16))
                    alphas.append(alpha)
                pv = _dot(_stack_heads_t(vt_ref[:, rows[b]]), jnp.concatenate(prs, axis=0))
                acc = jnp.where(top, alphas[0], alphas[1]) * acc + pv
            acc_s[...] = acc
            for hh in range(2):
                m_s[hh] = m_cur[hh]
                l_s[hh] = l_cur[hh]

        _walk_blocks(i, step, descending=False, group=4)
        o_ref[...] = (acc_s[...] / jnp.where(top, l_s[0], l_s[1])).T
        lse_ref[0, 0:1, :] = m_s[0] + jnp.log(l_s[0])
        lse_ref[0, 1:2, :] = m_s[1] + jnp.log(l_s[1])

    _, q_t, k_nat, k_t = _attn_specs(s, tq, 0)
    qstat = pl.BlockSpec((1, 2, tq), lambda p, i: (p, 0, i))
    return pl.pallas_call(
        body, name=name, grid=(half, nq),
        in_specs=[q_t, k_nat, k_t, qstat, pl.BlockSpec((1, s, 2), lambda p, i: (p, 0, 0))],
        out_specs=[pl.BlockSpec((tq, LANES), lambda p, i: (i, p)), qstat],
        out_shape=[jax.ShapeDtypeStruct((s, MIX // 2), F32), jax.ShapeDtypeStruct((half, 2, s), F32)],
        scratch_shapes=[pltpu.VMEM((2, 1, tq), F32), pltpu.VMEM((2, 1, tq), F32), pltpu.VMEM((LANES, tq), F32)],
        compiler_params=_params("parallel", "arbitrary"),
    )(qt, kh, vt, frow, fcol)


def _fox_bwd(qh, qt, kh, kt, vb, frow, fcol, lse, o, do, name, tq=256):
    s = kh.shape[0]
    tq = _tile(s, tq)
    nq, half = s // tq, MIX // LANES // 2

    def body(q_ref, qt_ref, k_ref, kt_ref, v_ref, fr_ref, fc_ref, lse_ref, o_ref, do_ref,
             dq_ref, dk_ref, dv_ref, dfk_ref, dfq_ref, dq_s, rs_s):
        i = pl.program_id(1)

        @pl.when(i == 0)
        def _():
            dk_ref[...] = jnp.zeros_like(dk_ref)
            dv_ref[...] = jnp.zeros_like(dv_ref)
            dfk_ref[...] = jnp.zeros_like(dfk_ref)

        q2 = _stack_heads(q_ref[...])
        qt_v = qt_ref[...]
        do_v = do_ref[...]
        do2 = _stack_heads(do_v.astype(BF16))
        dot_v = do_v.T.astype(BF16)
        dsum = _pair_colsum_t((do_v * o_ref[...]).T)
        ft, ls = fr_ref[0], lse_ref[0]
        key, qry = _key_query_iotas(tq)
        causal = key <= qry
        lane = lax.broadcasted_iota(jnp.int32, (2 * tq, LANES), 0) // tq
        pick2 = (lax.broadcasted_iota(jnp.int32, (2 * tq, LANES), 1) == lane).astype(BF16)
        q2_pick = jnp.concatenate([q2, pick2], axis=1)
        dq_s[...] = jnp.zeros_like(dq_s)
        rs_s[...] = jnp.zeros_like(rs_s)

        def step(blocks):
            rows = [pl.ds(pl.multiple_of(j * tq, tq), tq) for j, _ in blocks]
            zs = [_dot(_stack_heads(k_ref[r, :]), qt_v) for r in rows]
            dps = [_dot(_stack_heads(v_ref[r, :]), dot_v) for r in rows]
            rs = [rs_s[0], rs_s[1]]
            dq = None
            for b, (_, masked) in enumerate(blocks):
                fk = fc_ref[0, rows[b], :]
                prs, dss = [], []
                for hh in range(2):
                    blk = slice(hh * tq, (hh + 1) * tq)
                    sc = zs[b][blk] + (ft[hh:hh + 1, :] - fk[:, hh:hh + 1])
                    pr = jnp.exp(sc - ls[hh:hh + 1, :])
                    if masked:
                        pr = jnp.where(causal, pr, 0.0)
                    dsb = (pr * (dps[b][blk] - dsum[hh])).astype(BF16)
                    rs[hh] = rs[hh] + jnp.sum(dsb.astype(F32), axis=0, keepdims=True)
                    prs.append(pr.astype(BF16))
                    dss.append(dsb)
                dv_ref[rows[b], :] += _dot(jnp.concatenate(prs, axis=1), do2)
                both = _dot(jnp.concatenate(dss, axis=1), q2_pick)
                dk_ref[rows[b], :] += both[:, :LANES]
                dfk_ref[0, rows[b], :] -= both[:, LANES:]
                term = _dot(_stack_heads_t(kt_ref[:, rows[b]]), jnp.concatenate(dss, axis=0))
                dq = term if dq is None else dq + term
            rs_s[0], rs_s[1] = rs
            dq_s[...] += dq

        _walk_blocks(i, step, descending=False, group=2)
        dq_ref[...] = dq_s[...].T
        dfq_ref[0, 0:1, :] = rs_s[0]
        dfq_ref[0, 1:2, :] = rs_s[1]

    q_nat, q_t, k_nat, k_t = _attn_specs(s, tq, 0)
    qstat = pl.BlockSpec((1, 2, tq), lambda p, i: (p, 0, i))
    otile = pl.BlockSpec((tq, LANES), lambda p, i: (i, p))
    oresident = pl.BlockSpec((s, LANES), lambda p, i: (0, p))
    out = jax.ShapeDtypeStruct((s, MIX // 2), F32)
    return pl.pallas_call(
        body, name=name, grid=(half, nq),
        in_specs=[q_nat, q_t, k_nat, k_t, k_nat, qstat, pl.BlockSpec((1, s, 2), lambda p, i: (p, 0, 0)), qstat,
                  otile, q_nat],
        out_specs=[otile, oresident, oresident, pl.BlockSpec((1, s, LANES), lambda p, i: (p, 0, 0)), qstat],
        out_shape=[out, out, out, jax.ShapeDtypeStruct((half, s, LANES), F32),
                   jax.ShapeDtypeStruct((half, 2, s), F32)],
        scratch_shapes=[pltpu.VMEM((LANES, tq), F32), pltpu.VMEM((2, 1, tq), F32)],
        compiler_params=_params("parallel", "arbitrary"),
    )(qh, qt, kh, kt, vb, frow, fcol, lse, o, do)


def _log_sig_pair(z):
    lb = jnp.minimum(z, 0.0) - jnp.log(1.0 + jnp.exp(-jnp.abs(z)))
    return lb, lb - z


def _sb_fwd(qt, kh, vt, name, tq=256, gather=()):
    s = kh.shape[0]
    tq = _tile(s, tq)
    nq, half = s // tq, MIX // LANES // 2
    ng = len(gather)

    def body(*refs):
        qt_ref, k_ref, vt_ref = refs[:3]
        o_ref, tot_ref = refs[3 + ng:5 + ng]
        c_s, acc_s = refs[5 + 2 * ng:7 + 2 * ng]
        p, i = pl.program_id(0), pl.program_id(1)
        if ng:
            start, relay, finish = _gather_stages(refs[3:3 + ng], refs[5 + ng:5 + 2 * ng], *refs[7 + 2 * ng:])
            pl.when((p == 0) & (i == 0))(start)
            pl.when((p == half - 1) & (i == 0))(relay)
        qt_v = qt_ref[...]
        key, qry = _key_query_iotas(tq)
        strict = key < qry
        later = _tri(tq, lambda a, b: b > a)
        c_s[...] = jnp.zeros_like(c_s)
        acc_s[...] = jnp.zeros_like(acc_s)

        def step(blocks):
            rows = [pl.ds(pl.multiple_of(j * tq, tq), tq) for j, _ in blocks]
            zs = [_dot(_stack_heads(k_ref[r, :]), qt_v) for r in rows]
            lbs, loms, afters = [], [], []
            for b, (_, masked) in enumerate(blocks):
                for hh in range(2):
                    lb, lom = _log_sig_pair(zs[b][hh * tq:(hh + 1) * tq])
                    if masked:
                        lom = jnp.where(strict, lom, 0.0)
                    lbs.append(lb)
                    loms.append(lom)
                afters.append(_split_dot_left(later, jnp.concatenate(loms[2 * b:2 * b + 2], axis=1), 2))
            carry = [c_s[0], c_s[1]]
            pv = None
            for b, (_, masked) in enumerate(blocks):
                ws = []
                for hh in range(2):
                    n = 2 * b + hh
                    w = jnp.exp(lbs[n] + afters[b][:, hh * tq:(hh + 1) * tq] + carry[hh])
                    if masked:
                        w = jnp.where(strict, w, 0.0)
                    ws.append(w.astype(BF16))
                    carry[hh] = carry[hh] + jnp.sum(loms[n], axis=0, keepdims=True)
                term = _dot(_stack_heads_t(vt_ref[:, rows[b]]), jnp.concatenate(ws, axis=0))
                pv = term if pv is None else pv + term
            c_s[0], c_s[1] = carry
            acc_s[...] += pv

        _walk_blocks(i, step, descending=True, group=4)
        o_ref[...] = acc_s[...].T
        tot_ref[0, 0:1, :] = c_s[0]
        tot_ref[0, 1:2, :] = c_s[1]
        if ng:
            pl.when((p == half - 1) & (i == nq - 1))(finish)

    _, q_t, k_nat, k_t = _attn_specs(s, tq, half)
    qstat = pl.BlockSpec((1, 2, tq), lambda p, i: (p, 0, i))
    return pl.pallas_call(
        body, name=name, grid=(half, nq),
        in_specs=[q_t, k_nat, k_t] + [ANY] * ng,
        out_specs=[pl.BlockSpec((tq, LANES), lambda p, i: (i, p)), qstat] + [ANY] * ng,
        out_shape=[jax.ShapeDtypeStruct((s, MIX // 2), F32), jax.ShapeDtypeStruct((half, 2, s), F32)]
        + [jax.ShapeDtypeStruct((N_DEV,) + x.shape, x.dtype) for x in gather],
        scratch_shapes=[pltpu.VMEM((2, 1, tq), F32), pltpu.VMEM((LANES, tq), F32)]
        + (_gather_scratch(ng) if ng else []),
        compiler_params=_params("arbitrary", "arbitrary") if ng else _params("parallel", "arbitrary"),
    )(qt, kh, vt, *gather)


def _sb_bwd(qh, qt, kh, kt, vb, tot, do, name, tq=256):
    s = kh.shape[0]
    tq = _tile(s, tq)
    nq, half = s // tq, MIX // LANES // 2

    def body(q_ref, qt_ref, k_ref, kt_ref, v_ref, tot_ref, do_ref, dq_ref, dk_ref, dv_ref, rem_s, pg_s, dq_s):
        i = pl.program_id(1)

        @pl.when(i == 0)
        def _():
            dk_ref[...] = jnp.zeros_like(dk_ref)
            dv_ref[...] = jnp.zeros_like(dv_ref)

        q2 = _stack_heads(q_ref[...])
        qt_v = qt_ref[...]
        do_v = do_ref[...]
        do2 = _stack_heads(do_v.astype(BF16))
        dot_v = do_v.T.astype(BF16)
        key, qry = _key_query_iotas(tq)
        strict = key < qry
        upto = _tri(tq, lambda a, b: b <= a)
        before = _tri(tq, lambda a, b: b < a)
        tv = tot_ref[0]
        rem_s[0] = tv[0:1, :]
        rem_s[1] = tv[1:2, :]
        pg_s[...] = jnp.zeros_like(pg_s)
        dq_s[...] = jnp.zeros_like(dq_s)

        def step(blocks):
            nb = len(blocks)
            rows = [pl.ds(pl.multiple_of(j * tq, tq), tq) for j, _ in blocks]
            zs = [_dot(_stack_heads(k_ref[r, :]), qt_v) for r in rows]
            dws = [_dot(_stack_heads(v_ref[r, :]), dot_v) for r in rows]
            lbs, loms, prefixes = [], [], []
            for b, (_, masked) in enumerate(blocks):
                for hh in range(2):
                    lb, lom = _log_sig_pair(zs[b][hh * tq:(hh + 1) * tq])
                    if masked:
                        lom = jnp.where(strict, lom, 0.0)
                    lbs.append(lb)
                    loms.append(lom)
                prefixes.append(_split_dot_left(upto, jnp.concatenate(loms[2 * b:2 * b + 2], axis=1), 2))
            rem = [rem_s[0], rem_s[1]]
            ws, gs, gpres = [], [], []
            for b, (_, masked) in enumerate(blocks):
                for hh in range(2):
                    n = 2 * b + hh
                    blk = slice(hh * tq, (hh + 1) * tq)
                    w = jnp.exp(lbs[n] + (rem[hh] - prefixes[b][:, blk]))
                    if masked:
                        w = jnp.where(strict, w, 0.0)
                    gs.append(dws[b][blk] * w)
                    ws.append(w.astype(BF16))
                    rem[hh] = rem[hh] - jnp.sum(loms[n], axis=0, keepdims=True)
                gpres.append(_dot(before, jnp.concatenate(gs[2 * b:2 * b + 2], axis=1).astype(BF16)))
                dv_ref[rows[b], :] += _dot(jnp.concatenate(ws[2 * b:2 * b + 2], axis=1), do2)
            rem_s[0], rem_s[1] = rem
            pg = [pg_s[0], pg_s[1]]
            dq = None
            for b, (_, masked) in enumerate(blocks):
                dzs = []
                for hh in range(2):
                    n = 2 * b + hh
                    g = gs[n]
                    dz = g - jnp.exp(lbs[n]) * (g + (pg[hh] + gpres[b][:, hh * tq:(hh + 1) * tq]))
                    if masked:
                        dz = jnp.where(strict, dz, 0.0)
                    dzs.append(dz.astype(BF16))
                    pg[hh] = pg[hh] + jnp.sum(g, axis=0, keepdims=True)
                dk_ref[rows[b], :] += _dot(jnp.concatenate(dzs, axis=1), q2)
                term = _dot(_stack_heads_t(kt_ref[:, rows[b]]), jnp.concatenate(dzs, axis=0))
                dq = term if dq is None else dq + term
            pg_s[0], pg_s[1] = pg
            dq_s[...] += dq

        _walk_blocks(i, step, descending=False, group=2)
        dq_ref[...] = dq_s[...].T

    q_nat, q_t, k_nat, k_t = _attn_specs(s, tq, half)
    qstat = pl.BlockSpec((1, 2, tq), lambda p, i: (p, 0, i))
    otile = pl.BlockSpec((tq, LANES), lambda p, i: (i, p))
    oresident = pl.BlockSpec((s, LANES), lambda p, i: (0, p))
    out = jax.ShapeDtypeStruct((s, MIX // 2), F32)
    return pl.pallas_call(
        body, name=name, grid=(half, nq),
        in_specs=[q_nat, q_t, k_nat, k_t, k_nat, qstat, q_nat],
        out_specs=[otile, oresident, oresident], out_shape=[out, out, out],
        scratch_shapes=[pltpu.VMEM((2, 1, tq), F32), pltpu.VMEM((2, 1, tq), F32), pltpu.VMEM((LANES, tq), F32)],
        compiler_params=_params("parallel", "arbitrary"),
    )(qh, qt, kh, kt, vb, tot, do)


def _loss_head(y, target, name, ts=512):
    s, d = y.shape
    ts = _tile(s, ts, 8)
    nt = s // ts

    def body(y_ref, t_ref, dy_ref, l_ref, acc):
        i = pl.program_id(0)
        err = y_ref[...] - t_ref[...]
        dy_ref[...] = err * (1.0 / d)

        @pl.when(i == 0)
        def _():
            acc[...] = jnp.zeros_like(acc)

        acc[...] += jnp.sum(err * err, axis=0, keepdims=True)

        @pl.when(i == nt - 1)
        def _():
            tot = jnp.sum(acc[...], keepdims=True).reshape(1, 1) * (0.5 / d)
            l_ref[...] = jnp.broadcast_to(tot, l_ref.shape)

    row = pl.BlockSpec((ts, d), lambda i: (i, 0))
    dy, l = pl.pallas_call(
        body, name=name, grid=(nt,), in_specs=[row, row],
        out_specs=[row, pl.BlockSpec((8, LANES), lambda i: (0, 0))],
        out_shape=[jax.ShapeDtypeStruct((s, d), F32), jax.ShapeDtypeStruct((8, LANES), F32)],
        scratch_shapes=[pltpu.VMEM((1, d), F32)],
        compiler_params=_params("arbitrary"),
    )(y, target)
    return l[0, 0], dy


def _coords():
    return lax.axis_index("x"), lax.axis_index("y"), lax.axis_index("c")


def _other_chips(xi, yi):
    return [(1 - xi, yi), (xi, 1 - yi), (1 - xi, 1 - yi)]


def _gather_stages(x_refs, out_refs, send_sems, recv_sems, local_sems):
    n = len(x_refs)
    xi, yi, ci = _coords()
    me, sibling = (xi, yi, ci), (xi, yi, 1 - ci)
    chips = _other_chips(xi, yi)

    def slot(a, px, py, pc):
        return out_refs[a].at[4 * px + 2 * py + pc]

    def copy(a, k, block, to, src=None):
        return pltpu.make_async_remote_copy(
            src_ref=slot(a, *block) if src is None else src, dst_ref=slot(a, *block),
            send_sem=send_sems.at[a, k], recv_sem=recv_sems.at[a, k], device_id=to, device_id_type=MESH)

    def own(a):
        return pltpu.make_async_copy(x_refs[a], slot(a, *me), local_sems.at[a])

    def first(a):
        return [copy(a, 0, me, sibling, src=x_refs[a])] + [
            copy(a, 1 + j, me, (*chip, ci), src=x_refs[a]) for j, chip in enumerate(chips)]

    def passed(a, j):
        return copy(a, 4 + j, (*chips[j], ci), sibling)

    def start():
        for a in range(n):
            own(a).start()
        for a in range(n):
            for cp in first(a):
                cp.start()

    def relay():
        for j, chip in enumerate(chips):
            for a in range(n):
                copy(a, 1 + j, (*chip, ci), me).wait_recv()
                passed(a, j).start()

    def finish():
        for a in range(n):
            copy(a, 0, sibling, me).wait_recv()
            for j, chip in enumerate(chips):
                copy(a, 4 + j, (*chip, 1 - ci), me).wait_recv()
        for a in range(n):
            for cp in first(a) + [passed(a, j) for j in range(len(chips))]:
                cp.wait_send()
            own(a).wait()

    return start, relay, finish


def _gather_scratch(n):
    return [pltpu.SemaphoreType.DMA((n, 7)), pltpu.SemaphoreType.DMA((n, 7)), pltpu.SemaphoreType.DMA((n,))]


def _all_gather(xs, name):
    n = len(xs)

    def body(*refs):
        start, relay, finish = _gather_stages(refs[:n], refs[n:2 * n], *refs[2 * n:])
        start()
        relay()
        finish()

    return pl.pallas_call(
        body, name=name, out_shape=[jax.ShapeDtypeStruct((N_DEV,) + x.shape, x.dtype) for x in xs],
        in_specs=[ANY] * n, out_specs=[ANY] * n, scratch_shapes=_gather_scratch(n),
    )(*xs)


def _sibling_exchange(gs, name):
    n = len(gs)

    def body(*refs):
        g_refs, recv_refs = refs[:n], refs[n:2 * n]
        send_sems, recv_sems = refs[2 * n:]
        xi, yi, ci = _coords()
        cps = [pltpu.make_async_remote_copy(
            src_ref=g_refs[a].at[2 * chip + (1 - ci)], dst_ref=recv_refs[a].at[chip],
            send_sem=send_sems.at[a, chip], recv_sem=recv_sems.at[a, chip],
            device_id=(xi, yi, 1 - ci), device_id_type=MESH) for a in range(n) for chip in range(N_CHIP)]
        for cp in cps:
            cp.start()
        for cp in cps:
            cp.wait()

    return pl.pallas_call(
        body, name=name, out_shape=[jax.ShapeDtypeStruct((N_CHIP,) + g.shape[1:], g.dtype) for g in gs],
        in_specs=[ANY] * n, out_specs=[ANY] * n,
        scratch_shapes=[pltpu.SemaphoreType.DMA((n, N_CHIP)), pltpu.SemaphoreType.DMA((n, N_CHIP))],
    )(*gs)


def _pair_add(g, recv, ids, name, tr=256):
    _, r, c = g.shape
    tr = _tile(r, tr, 16)

    def body(ids_ref, g_ref, r_ref, p_ref, own_ref):
        kk = pl.program_id(1)
        tot = g_ref[0].astype(F32) + r_ref[0].astype(F32)
        p_ref[0] = tot.astype(BF16)

        @pl.when(kk == ids_ref[1])
        def _():
            own_ref[...] = tot

    grid_spec = pltpu.PrefetchScalarGridSpec(
        num_scalar_prefetch=1, grid=(r // tr, N_CHIP),
        in_specs=[pl.BlockSpec((1, tr, c), lambda i, kk, ids: (2 * kk + ids[0], i, 0)),
                  pl.BlockSpec((1, tr, c), lambda i, kk, ids: (kk, i, 0))],
        out_specs=[pl.BlockSpec((1, tr, c), lambda i, kk, ids: (kk, i, 0)),
                   pl.BlockSpec((tr, c), lambda i, kk, ids: (i, 0))])
    return pl.pallas_call(
        body, name=name, grid_spec=grid_spec,
        out_shape=[jax.ShapeDtypeStruct((N_CHIP, r, c), BF16), jax.ShapeDtypeStruct((r, c), F32)],
        compiler_params=_params("parallel", "arbitrary"),
    )(ids, g, recv)


def _chip_exchange(ps, name):
    n = len(ps)

    def body(*refs):
        p_refs, recv_refs = refs[:n], refs[n:2 * n]
        send_sems, recv_sems = refs[2 * n:]
        xi, yi, ci = _coords()
        mine = 2 * xi + yi
        chips = _other_chips(xi, yi)

        def copy(a, k, cx, cy):
            return pltpu.make_async_remote_copy(
                src_ref=p_refs[a].at[2 * cx + cy], dst_ref=recv_refs[a].at[mine],
                send_sem=send_sems.at[a, k], recv_sem=recv_sems.at[a, k],
                device_id=(cx, cy, ci), device_id_type=MESH)

        def landed(a, k, cx, cy):
            return pltpu.make_async_remote_copy(
                src_ref=p_refs[a].at[mine], dst_ref=recv_refs[a].at[2 * cx + cy],
                send_sem=send_sems.at[a, k], recv_sem=recv_sems.at[a, k],
                device_id=(cx, cy, ci), device_id_type=MESH)

        sends = [copy(a, k, cx, cy) for a in range(n) for k, (cx, cy) in enumerate(chips)]
        for cp in sends:
            cp.start()
        for a in range(n):
            for k, (cx, cy) in enumerate(chips):
                landed(a, k, cx, cy).wait_recv()
        for cp in sends:
            cp.wait_send()

    return pl.pallas_call(
        body, name=name, out_shape=[jax.ShapeDtypeStruct(p.shape, p.dtype) for p in ps],
        in_specs=[ANY] * n, out_specs=[ANY] * n,
        scratch_shapes=[pltpu.SemaphoreType.DMA((n, 3)), pltpu.SemaphoreType.DMA((n, 3))],
    )(*ps)


def _adamw_math(w, g, m, v):
    m = ADAM_B1 * m + (1.0 - ADAM_B1) * g
    v = ADAM_B2 * v + (1.0 - ADAM_B2) * (g * g)
    m_hat = m / (1.0 - ADAM_B1 ** ADAM_STEP)
    v_hat = v / (1.0 - ADAM_B2 ** ADAM_STEP)
    delta = -ADAM_LR * (m_hat / (jnp.sqrt(v_hat) + ADAM_EPS) + ADAM_WD * w)
    return delta, m, v


def _adamw_reduce(own, recv, ids, w, m, v, name, tr=256):
    r, c = w.shape
    tr = _tile(r, tr, 16)

    def body(ids_ref, own_ref, recv_ref, w_ref, m_ref, v_ref, g_ref, d_ref, mo_ref, vo_ref):
        mine = ids_ref[1]
        g = None
        for kk in range(N_CHIP):
            term = jnp.where(mine == kk, own_ref[...], recv_ref[kk].astype(F32))
            g = term if g is None else g + term
        delta, m_new, v_new = _adamw_math(w_ref[...], g, m_ref[...], v_ref[...])
        g_ref[...] = g
        d_ref[...] = delta
        mo_ref[...] = m_new
        vo_ref[...] = v_new

    row = pl.BlockSpec((tr, c), lambda i, ids: (i, 0))
    grid_spec = pltpu.PrefetchScalarGridSpec(
        num_scalar_prefetch=1, grid=(r // tr,),
        in_specs=[row, pl.BlockSpec((N_CHIP, tr, c), lambda i, ids: (0, i, 0)), row, row, row],
        out_specs=[row, row, row, row])
    out = jax.ShapeDtypeStruct((r, c), F32)
    return pl.pallas_call(
        body, name=name, grid_spec=grid_spec, out_shape=[out, out, out, out],
        compiler_params=_params("parallel"),
    )(ids, own, recv, w, m, v)


def _sum_sources(a, name):
    n, r, c = a.shape

    def body(a_ref, o_ref):
        tot = a_ref[0]
        for kk in range(1, n):
            tot = tot + a_ref[kk]
        o_ref[...] = tot

    return pl.pallas_call(
        body, name=name, out_shape=jax.ShapeDtypeStruct((r, c), F32),
        in_specs=[pl.BlockSpec(memory_space=pltpu.VMEM)], out_specs=pl.BlockSpec(memory_space=pltpu.VMEM),
    )(a)


def _adamw_small(w, g, m, v, name):
    def body(w_ref, g_ref, m_ref, v_ref, d_ref, mo_ref, vo_ref):
        delta, m_new, v_new = _adamw_math(w_ref[...], g_ref[...], m_ref[...], v_ref[...])
        d_ref[...] = delta
        mo_ref[...] = m_new
        vo_ref[...] = v_new

    vm = pl.BlockSpec(memory_space=pltpu.VMEM)
    out = jax.ShapeDtypeStruct(w.shape, F32)
    return pl.pallas_call(body, name=name, out_shape=[out, out, out], in_specs=[vm] * 4, out_specs=[vm] * 3)(w, g, m, v)


def _pack(parts, width, row_mult):
    flat = jnp.concatenate([p.reshape(-1) for p in parts])
    rows = -(-flat.shape[0] // width)
    rows = -(-rows // row_mult) * row_mult
    return jnp.pad(flat, (0, rows * width - flat.shape[0])).reshape(rows, width)


def _unpack(flat, shapes):
    out, off = [], 0
    lead = flat.shape[:-1]
    for shp in shapes:
        n = 1
        for dd in shp:
            n *= dd
        out.append(flat[..., off:off + n].reshape(lead + tuple(shp)))
        off += n
    return out


def _rows2d(w):
    return w.reshape(w.shape[0] * w.shape[1], w.shape[2])


def _cols_to_dev(g):
    l, k, n = g.shape
    return g.reshape(l * k, N_DEV, n // N_DEV).transpose(1, 0, 2)


def _rows_to_dev(g):
    l, k, n = g.shape
    rs = k // N_DEV
    return g.reshape(l, N_DEV, rs, n).transpose(1, 0, 2, 3).reshape(N_DEV, l * rs, n)


def _dev_to_cols(a, l):
    _, lk, cs = a.shape
    return a.transpose(1, 0, 2).reshape(l, lk // l, N_DEV * cs)


def _dev_to_rows(a, l):
    _, lr, n = a.shape
    rs = lr // l
    return a.reshape(N_DEV, l, rs, n).transpose(1, 0, 2, 3).reshape(l, N_DEV * rs, n)


def kernel(x, attn_norm, attn_w_in, attn_f_bias, fox_q_gain, fox_k_gain, sb_q_gain, sb_k_gain, attn_w_out, conv_norm, conv_w_in, conv_kernel, conv_w_out, ffn_norm, ffn_w_up, ffn_conv, ffn_w_down, loss_target, m_attn_norm, m_attn_w_in, m_attn_f_bias, m_fox_q_gain, m_fox_k_gain, m_sb_q_gain, m_sb_k_gain, m_attn_w_out, m_conv_norm, m_conv_w_in, m_conv_kernel, m_conv_w_out, m_ffn_norm, m_ffn_w_up, m_ffn_conv, m_ffn_w_down, v_attn_norm, v_attn_w_in, v_attn_f_bias, v_fox_q_gain, v_fox_k_gain, v_sb_q_gain, v_sb_k_gain, v_attn_w_out, v_conv_norm, v_conv_w_in, v_conv_kernel, v_conv_w_out, v_ffn_norm, v_ffn_w_up, v_ffn_conv, v_ffn_w_down):
    s = x.shape[1]
    n_attn, n_conv, depth = attn_w_in.shape[0], conv_w_in.shape[0], ffn_w_up.shape[0]
    xi, yi, ci = _coords()
    me = 4 * xi + 2 * yi + ci
    ids = jnp.stack([ci, 2 * xi + yi]).astype(jnp.int32)

    big = [attn_w_in, attn_w_out, conv_w_in, conv_w_out, ffn_w_up, ffn_w_down]
    big_m = [m_attn_w_in, m_attn_w_out, m_conv_w_in, m_conv_w_out, m_ffn_w_up, m_ffn_w_down]
    big_v = [v_attn_w_in, v_attn_w_out, v_conv_w_in, v_conv_w_out, v_ffn_w_up, v_ffn_w_down]
    big_names = ["attn_w_in", "attn_w_out", "conv_w_in", "conv_w_out", "ffn_w_up", "ffn_w_down"]
    small_sh = [conv_norm, conv_kernel, ffn_conv]
    small_sh_shapes = [w.shape for w in small_sh]
    rep = [attn_norm, attn_f_bias, fox_q_gain, fox_k_gain, sb_q_gain, sb_k_gain, ffn_norm]
    rep_shapes = [w.shape for w in rep]

    small_pack = _pack(small_sh, LANES, 8)
    shards_bf16 = [_rows2d(w).astype(BF16) for w in big]
    early = _all_gather([attn_w_in[0].astype(BF16), small_pack], "gather_first")
    first_w_in = jnp.pad(_dev_to_cols(early[0], 1)[0], ((0, 0), (0, ATTN_IN_PAD - ATTN_IN)))
    cn, ckern, fconv = _unpack(early[1].reshape(N_DEV, -1), small_sh_shapes)
    conv_norm_f = cn.transpose(1, 0, 2).reshape(n_conv, D_MODEL)
    conv_kernel_f = ckern.transpose(1, 2, 0, 3).reshape(n_conv, 3, D_MODEL)
    ffn_conv_f = fconv.transpose(1, 2, 0, 3).reshape(depth, 3, 2 * D_FF)

    def pair_gain(fox_g, sb_g):
        f2 = jnp.concatenate([fox_g, fox_g])
        s2 = jnp.concatenate([sb_g, sb_g])
        return jnp.concatenate([jnp.tile(f2[None], (4, 1)), jnp.tile(s2[None], (4, 1))])[:, None, :]

    h = x[0]
    saved = []
    for layer in range(depth):
        i = layer // 2
        tag = "l%d" % layer
        rec = {"h_in": h}
        if layer % 2 == 0:
            xn = _rms_fwd(h, attn_norm[i], tag + "_attn_rms")
            proj = _matmul(xn, first_w_in if layer == 0 else a_w_in[i], tag + "_attn_in", tn=640)
            gq, gk = pair_gain(fox_q_gain[i], sb_q_gain[i]), pair_gain(fox_k_gain[i], sb_k_gain[i])
            qh, kh, vb, qt, kt, vt = _qkv_prep(proj, gq, gk, tag + "_qkv_prep")
            logit = proj[:, 3 * MIX:3 * MIX + H_FOX].T.reshape(H_FOX, s // LANES, LANES)
            cum = _fgate_fwd(logit, attn_f_bias[i], tag + "_fgate")
            frow = cum.reshape(H_FOX // 2, 2, s)
            fcol = frow.transpose(0, 2, 1)
            o_fox, lse = _fox_fwd(qt, kh, vt, frow, fcol, tag + "_fox_fwd")
            if layer == 0:
                o_sb, tot, *gathered = _sb_fwd(qt, kh, vt, tag + "_sb_fwd", gather=shards_bf16)
                a_w_in = _dev_to_cols(gathered[0], n_attn)
                a_w_in = jnp.pad(a_w_in, ((0, 0), (0, 0), (0, ATTN_IN_PAD - ATTN_IN)))
                a_w_out = _dev_to_rows(gathered[1], n_attn)
                c_w_in = _dev_to_cols(gathered[2], n_conv)
                c_w_out = _dev_to_rows(gathered[3], n_conv)
                f_w_up = _dev_to_cols(gathered[4], depth)
                f_w_down = _dev_to_rows(gathered[5], depth)
            else:
                o_sb, tot = _sb_fwd(qt, kh, vt, tag + "_sb_fwd")
            o = jnp.concatenate([o_fox, o_sb], axis=1)
            h = _matmul(o, a_w_out[i], tag + "_attn_out", add=h)
            rec.update(xn=xn, proj=proj, gq=gq, gk=gk, qh=qh, kh=kh, vb=vb, qt=qt, kt=kt, logit=logit, frow=frow,
                       fcol=fcol, o_fox=o_fox, lse=lse, tot=tot, o=o)
        else:
            xn = _rms_fwd(h, conv_norm_f[i], tag + "_conv_rms")
            proj = _matmul(xn, c_w_in[i], tag + "_conv_in", tn=1024)
            y = _sconv_fwd(proj, conv_kernel_f[i], tag + "_sconv_fwd")
            h = _matmul(y, c_w_out[i], tag + "_conv_out", add=h)
            rec.update(xn=xn, proj=proj, y=y)
        rec["h_mid"] = h
        xn2 = _rms_fwd(h, ffn_norm[layer], tag + "_ffn_rms")
        up = _matmul(xn2, f_w_up[layer], tag + "_ffn_up", tn=1408)
        act = _ffn_act_fwd(up, ffn_conv_f[layer], tag + "_ffn_act")
        h = _matmul(act, f_w_down[layer], tag + "_ffn_down", add=h, tk=2816)
        rec.update(xn2=xn2, up=up, act=act)
        saved.append(rec)

    loss_local, dh = _loss_head(h, loss_target[0], "loss_head")
    loss = lax.psum(loss_local, ("x", "y", "c"))

    g_attn_norm, g_attn_w_in, g_f_bias = [None] * n_attn, [None] * n_attn, [None] * n_attn
    g_fq, g_fk, g_sq, g_sk, g_attn_w_out = ([None] * n_attn for _ in range(5))
    g_conv_norm, g_conv_w_in, g_conv_kernel, g_conv_w_out = ([None] * n_conv for _ in range(4))
    g_ffn_norm, g_ffn_w_up, g_ffn_conv, g_ffn_w_down = ([None] * depth for _ in range(4))

    for layer in reversed(range(depth)):
        i = layer // 2
        tag = "l%d" % layer
        rec = saved[layer]
        da = _matmul(dh, f_w_down[layer].T, tag + "_ffn_down_dx", tn=1408)
        g_ffn_w_down[layer] = _matmul_tn(rec["act"], dh, tag + "_ffn_down_dw", tm=1408, tn=1024)
        dup, dwg, dwv = _ffn_act_bwd(rec["up"], ffn_conv_f[layer], da, tag + "_ffn_act_bwd")
        g_ffn_conv[layer] = jnp.concatenate([dwg, dwv], axis=1)
        g_ffn_w_up[layer] = _matmul_tn(rec["xn2"], dup, tag + "_ffn_up_dw", tn=1408)
        dxn = _matmul(dup, f_w_up[layer].T, tag + "_ffn_up_dx", tn=1024, tk=2816)
        dh, g_ffn_norm[layer] = _rms_bwd(rec["h_mid"], dxn, ffn_norm[layer], dh, tag + "_ffn_rms_bwd")
        if layer % 2 == 0:
            do = _matmul(dh, a_w_out[i].T, tag + "_attn_out_dx", tn=1024)
            g_attn_w_out[i] = _matmul_tn(rec["o"], dh, tag + "_attn_out_dw", tn=1024)
            dq_f, dk_f, dv_f, dfk, dfq = _fox_bwd(rec["qh"], rec["qt"], rec["kh"], rec["kt"], rec["vb"], rec["frow"],
                                                  rec["fcol"], rec["lse"], rec["o_fox"], do, tag + "_fox_bwd")
            dq_s, dk_s, dv_s = _sb_bwd(rec["qh"], rec["qt"], rec["kh"], rec["kt"], rec["vb"], rec["tot"], do,
                                       tag + "_sb_bwd")
            dq, dk, dv, dgq, dgk = _qkv_prep_bwd(rec["proj"], rec["gq"], rec["gk"], (dq_f, dq_s), (dk_f, dk_s),
                                                 (dv_f, dv_s), tag + "_qkv_prep_bwd")
            dcum = (dfq + dfk[:, :, 0:2].transpose(0, 2, 1)).reshape(H_FOX, s // LANES, LANES)
            dlogit, dbias = _fgate_bwd(rec["logit"], attn_f_bias[i], dcum, tag + "_fgate_bwd")
            g_f_bias[i] = dbias[:, 0]
            dgate = jnp.pad(dlogit.reshape(H_FOX, s).T, ((0, 0), (0, LANES - H_FOX))).astype(BF16)
            dproj = jnp.concatenate([dq, dk, dv, dgate], axis=1)

            def fold(dg):
                per_head = dg.reshape(16, HEAD_DIM)
                return jnp.sum(per_head[:8], axis=0), jnp.sum(per_head[8:], axis=0)

            g_fq[i], g_sq[i] = fold(dgq)
            g_fk[i], g_sk[i] = fold(dgk)
            g_attn_w_in[i] = _matmul_tn(rec["xn"], dproj, tag + "_attn_in_dw", tn=640)[:, :ATTN_IN]
            dxn = _matmul(dproj, a_w_in[i].T, tag + "_attn_in_dx", tn=1024, tk=3200)
            dh, g_attn_norm[i] = _rms_bwd(rec["h_in"], dxn, attn_norm[i], dh, tag + "_attn_rms_bwd")
        else:
            dy = _matmul(dh, c_w_out[i].T, tag + "_conv_out_dx", tn=1024)
            g_conv_w_out[i] = _matmul_tn(rec["y"], dh, tag + "_conv_out_dw", tn=1024)
            dproj, g_conv_kernel[i] = _sconv_bwd(rec["proj"], conv_kernel_f[i], dy, tag + "_sconv_bwd")
            g_conv_w_in[i] = _matmul_tn(rec["xn"], dproj, tag + "_conv_in_dw", tn=1024)
            dxn = _matmul(dproj, c_w_in[i].T, tag + "_conv_in_dx", tn=1024, tk=3072)
            dh, g_conv_norm[i] = _rms_bwd(rec["h_in"], dxn, conv_norm_f[i], dh, tag + "_conv_rms_bwd")
    grad_x = dh[None]

    gb = [_cols_to_dev(jnp.stack(g_attn_w_in).astype(BF16)), _rows_to_dev(jnp.stack(g_attn_w_out).astype(BF16)),
          _cols_to_dev(jnp.stack(g_conv_w_in).astype(BF16)), _rows_to_dev(jnp.stack(g_conv_w_out).astype(BF16)),
          _cols_to_dev(jnp.stack(g_ffn_w_up).astype(BF16)), _rows_to_dev(jnp.stack(g_ffn_w_down).astype(BF16))]
    from_sibling = _sibling_exchange(gb, "reduce_sibling")
    pairs = [_pair_add(g, r, ids, "reduce_pair_add_" + nm) for g, r, nm in zip(gb, from_sibling, big_names)]
    from_chips = _chip_exchange([p[0] for p in pairs], "reduce_chips")
    grads_big, delta_big, newm_big, newv_big = [], [], [], []
    for a, nm in enumerate(big_names):
        g_a, d_a, m_a, v_a = _adamw_reduce(pairs[a][1], from_chips[a], ids, _rows2d(big[a]), _rows2d(big_m[a]),
                                           _rows2d(big_v[a]), "adamw_" + nm)
        shp = big[a].shape
        grads_big.append(g_a.reshape(shp))
        delta_big.append(d_a.reshape(shp))
        newm_big.append(m_a.reshape(shp))
        newv_big.append(v_a.reshape(shp))

    rep_g = [jnp.stack(g_attn_norm), jnp.stack(g_f_bias), jnp.stack(g_fq), jnp.stack(g_fk), jnp.stack(g_sq),
             jnp.stack(g_sk), jnp.stack(g_ffn_norm)]
    sh_g = [jnp.stack(g_conv_norm).reshape(n_conv, N_DEV, -1).transpose(1, 0, 2),
            jnp.stack(g_conv_kernel).reshape(n_conv, 3, N_DEV, -1).transpose(2, 0, 1, 3),
            jnp.stack(g_ffn_conv).reshape(depth, 3, N_DEV, -1).transpose(2, 0, 1, 3)]
    n_rep = sum(int(a.size) for a in rep)
    n_sh = sum(int(a.size) for a in small_sh)
    partial = _pack(rep_g + [jnp.concatenate([a.reshape(N_DEV, -1) for a in sh_g], axis=1)], LANES, 8)
    total = _sum_sources(_all_gather([partial], "gather_small_grads")[0], "sum_small_grads").reshape(-1)
    rep_tot = total[:n_rep]
    sh_tot = lax.dynamic_slice_in_dim(total[n_rep:n_rep + N_DEV * n_sh].reshape(N_DEV, n_sh), me, 1, axis=0)[0]
    g_small = _pack([rep_tot, sh_tot], LANES, 8)

    def small_pack_of(rep_list, sh_list):
        return _pack(rep_list + sh_list, LANES, 8)

    d_small, m_small, v_small = _adamw_small(
        small_pack_of(rep, small_sh), g_small,
        small_pack_of([m_attn_norm, m_attn_f_bias, m_fox_q_gain, m_fox_k_gain, m_sb_q_gain, m_sb_k_gain, m_ffn_norm],
                      [m_conv_norm, m_conv_kernel, m_ffn_conv]),
        small_pack_of([v_attn_norm, v_attn_f_bias, v_fox_q_gain, v_fox_k_gain, v_sb_q_gain, v_sb_k_gain, v_ffn_norm],
                      [v_conv_norm, v_conv_kernel, v_ffn_conv]),
        "adamw_small")
    small_shapes = rep_shapes + small_sh_shapes

    def split_small(a):
        return _unpack(a.reshape(-1), small_shapes)

    def ordered(big_list, small_list):
        an, fb, fq, fk, sq, sk, fn, cno, cke, fco = small_list
        awi, awo, cwi, cwo, fwu, fwd = big_list
        return [an, awi, fb, fq, fk, sq, sk, awo, cno, cwi, cke, cwo, fn, fwu, fco, fwd]

    grads = ordered(grads_big, split_small(g_small))
    deltas = ordered(delta_big, split_small(d_small))
    new_m = ordered(newm_big, split_small(m_small))
    new_v = ordered(newv_big, split_small(v_small))
    return (loss, grad_x, *grads, *deltas, *new_m, *new_v)
```

```python
import jax
import jax.numpy as jnp
from jax import lax
from jax.experimental import pallas as pl
from jax.experimental.pallas import tpu as pltpu

F32 = jnp.float32
BF16 = jnp.bfloat16

D_MODEL = 1024
HEAD_DIM = 64
H_FOX = 8
MIX = 1024
ATTN_IN = 3 * MIX + H_FOX
ATTN_IN_PAD = 3 * MIX + 128
D_FF = 2816
EPS = 1e-6
NEG = -1e30
LANES = 128
N_DEV = 8
N_CHIP = 4

ADAM_LR = 0.001
ADAM_B1 = 0.9
ADAM_B2 = 0.999
ADAM_EPS = 1e-08
ADAM_WD = 0.01
ADAM_STEP = 10

VMEM_LIMIT = 56 * 1024 * 1024
MESH = pl.DeviceIdType.MESH
ANY = pl.BlockSpec(memory_space=pl.ANY)


def _params(*sem):
    return pltpu.CompilerParams(dimension_semantics=sem, vmem_limit_bytes=VMEM_LIMIT)


def _tile(n, target, mult=LANES):
    best = None
    for t in range(mult, min(n, target) + 1, mult):
        if n % t == 0:
            best = t
    return best if best is not None else n


def _dot(a, b):
    return jnp.dot(a, b, preferred_element_type=F32)


def _dot_tn(a, b):
    return lax.dot_general(a, b, (((0,), (0,)), ((), ())), preferred_element_type=F32)


def _split_dot(x, m, passes):
    acc = None
    rem = x
    for _ in range(passes):
        part = rem.astype(BF16)
        term = _dot(part, m)
        acc = term if acc is None else acc + term
        rem = rem - part.astype(F32)
    return acc


def _split_dot_left(m, x, passes):
    acc = None
    rem = x
    for _ in range(passes):
        part = rem.astype(BF16)
        term = _dot(m, part)
        acc = term if acc is None else acc + term
        rem = rem - part.astype(F32)
    return acc


def _matmul(a, b, name, add=None, out_dtype=F32, tm=1024, tn=512, tk=1024):
    split = a.shape[0] if a.ndim == 3 else 1
    m, kh = a.shape[-2:]
    k = split * kh
    n = b.shape[1]
    tm, tn, tk = _tile(m, tm, 8), _tile(n, tn), _tile(kh, tk)
    nk = k // tk
    per_slab = kh // tk
    has_add = add is not None

    def body(*refs):
        a_ref, b_ref = refs[0], refs[1]
        add_ref = refs[2] if has_add else None
        o_ref = refs[2 + has_add]

        def finish(acc):
            if has_add:
                acc = acc + add_ref[...]
            o_ref[...] = acc.astype(out_dtype)

        p = _dot(a_ref[...].astype(BF16), b_ref[...].astype(BF16))
        if nk == 1:
            finish(p)
        else:
            acc_ref = refs[-1]
            kk = pl.program_id(2)

            @pl.when(kk == 0)
            def _():
                acc_ref[...] = p

            @pl.when(kk > 0)
            def _():
                acc_ref[...] += p

            @pl.when(kk == nk - 1)
            def _():
                finish(acc_ref[...])

    if split == 1:
        a_spec = pl.BlockSpec((tm, tk), lambda i, j, kk: (i, kk))
    else:
        a_spec = pl.BlockSpec((None, tm, tk), lambda i, j, kk: (kk // per_slab, i, kk % per_slab))
    in_specs = [a_spec, pl.BlockSpec((tk, tn), lambda i, j, kk: (kk, j))]
    args = [a, b]
    if has_add:
        in_specs.append(pl.BlockSpec((tm, tn), lambda i, j, kk: (i, j)))
        args.append(add)
    return pl.pallas_call(
        body, name=name, grid=(m // tm, n // tn, nk), in_specs=in_specs,
        out_specs=pl.BlockSpec((tm, tn), lambda i, j, kk: (i, j)),
        out_shape=jax.ShapeDtypeStruct((m, n), out_dtype),
        scratch_shapes=[pltpu.VMEM((tm, tn), F32)] if nk > 1 else [],
        compiler_params=_params("parallel", "parallel", "arbitrary"),
    )(*args)


def _matmul_tn(a, b, name, tm=1024, tn=512, ts=2048):
    s, m = a.shape
    split = b.shape[0] if b.ndim == 3 else 1
    nh = b.shape[-1]
    n = split * nh
    tm, tn, ts = _tile(m, tm), _tile(nh, tn), _tile(s, ts, 8)
    per_slab = nh // tn
    if split == 1:
        b_spec = pl.BlockSpec((ts, tn), lambda i, j, kk: (kk, j))
    else:
        b_spec = pl.BlockSpec((None, ts, tn), lambda i, j, kk: (j // per_slab, kk, j % per_slab))

    def body(a_ref, b_ref, o_ref):
        kk = pl.program_id(2)
        p = _dot_tn(a_ref[...].astype(BF16), b_ref[...].astype(BF16))

        @pl.when(kk == 0)
        def _():
            o_ref[...] = p

        @pl.when(kk > 0)
        def _():
            o_ref[...] += p

    return pl.pallas_call(
        body, name=name, grid=(m // tm, n // tn, s // ts),
        in_specs=[pl.BlockSpec((ts, tm), lambda i, j, kk: (kk, i)), b_spec],
        out_specs=pl.BlockSpec((tm, tn), lambda i, j, kk: (i, j)),
        out_shape=jax.ShapeDtypeStruct((m, n), F32),
        compiler_params=_params("parallel", "parallel", "arbitrary"),
    )(a, b)


def _rms_fwd(h, g, name, ts=512):
    s, d = h.shape
    ts = _tile(s, ts, 8)

    def body(h_ref, g_ref, o_ref):
        x = h_ref[...]
        r = lax.rsqrt(jnp.mean(x * x, axis=-1, keepdims=True) + EPS)
        o_ref[...] = (x * r * g_ref[...]).astype(BF16)

    return pl.pallas_call(
        body, name=name, grid=(s // ts,),
        in_specs=[pl.BlockSpec((ts, d), lambda i: (i, 0)), pl.BlockSpec((1, d), lambda i: (0, 0))],
        out_specs=pl.BlockSpec((ts, d), lambda i: (i, 0)),
        out_shape=jax.ShapeDtypeStruct((s, d), BF16),
        compiler_params=_params("parallel"),
    )(h, g.reshape(1, d))


def _rms_bwd(h, dxn, g, dh_in, name, ts=512):
    s, d = h.shape
    ts = _tile(s, ts, 8)

    def body(h_ref, dxn_ref, g_ref, dhin_ref, dh_ref, dg_ref):
        i = pl.program_id(0)
        x = h_ref[...]
        r = lax.rsqrt(jnp.mean(x * x, axis=-1, keepdims=True) + EPS)
        xh = x * r
        dxn_v = dxn_ref[...]

        @pl.when(i == 0)
        def _():
            dg_ref[...] = jnp.zeros_like(dg_ref)

        dg_ref[0:1, :] += jnp.sum(dxn_v * xh, axis=0, keepdims=True)
        dxh = dxn_v * g_ref[...]
        dx = r * (dxh - xh * jnp.mean(dxh * xh, axis=-1, keepdims=True))
        dh_ref[...] = dhin_ref[...] + dx

    row = pl.BlockSpec((ts, d), lambda i: (i, 0))
    dh, dg = pl.pallas_call(
        body, name=name, grid=(s // ts,),
        in_specs=[row, row, pl.BlockSpec((1, d), lambda i: (0, 0)), row],
        out_specs=[row, pl.BlockSpec((8, d), lambda i: (0, 0))],
        out_shape=[jax.ShapeDtypeStruct((s, d), F32), jax.ShapeDtypeStruct((8, d), F32)],
        compiler_params=_params("arbitrary"),
    )(h, dxn, g.reshape(1, d), dh_in)
    return dh, dg[0]


def _shift_down(x, prev):
    rows = lax.broadcasted_iota(jnp.int32, (8, x.shape[1]), 0)
    p1, p2 = prev[7:8, :], prev[6:7, :]
    r1, r2 = pltpu.roll(x, 1, 0), pltpu.roll(x, 2, 0)
    top1 = jnp.where(rows == 0, p1, r1[0:8, :])
    top2 = jnp.where(rows == 0, p2, jnp.where(rows == 1, p1, r2[0:8, :]))
    if x.shape[0] == 8:
        return top1, top2
    return jnp.concatenate([top1, r1[8:, :]], axis=0), jnp.concatenate([top2, r2[8:, :]], axis=0)


def _shift_up(x, nxt):
    n = x.shape[0]
    rows = lax.broadcasted_iota(jnp.int32, (8, x.shape[1]), 0)
    n0, n1 = nxt[0:1, :], nxt[1:2, :]
    r1, r2 = pltpu.roll(x, n - 1, 0), pltpu.roll(x, n - 2, 0)
    end1 = jnp.where(rows == 7, n0, r1[n - 8:, :])
    end2 = jnp.where(rows == 7, n1, jnp.where(rows == 6, n0, r2[n - 8:, :]))
    return jnp.concatenate([r1[:n - 8, :], end1], axis=0), jnp.concatenate([r2[:n - 8, :], end2], axis=0)


def _conv(x, x1, x2, w):
    return w[2:3, :] * x + w[1:2, :] * x1 + w[0:1, :] * x2


def _halo_specs(ts, tc, col, n_time_blocks):
    r8 = ts // 8
    main = pl.BlockSpec((ts, tc), lambda j, i: (i, j + col))
    prev = pl.BlockSpec((8, tc), lambda j, i: (jnp.maximum(i * r8 - 1, 0), j + col))
    nxt = pl.BlockSpec((8, tc), lambda j, i: (jnp.minimum((i + 1) * r8, n_time_blocks * r8 - 1), j + col))
    return main, prev, nxt


def _silu_parts(g):
    sig = 1.0 / (1.0 + jnp.exp(-g))
    return sig, g * sig


def _ffn_act_fwd(up, cw, name, ts=256, tc=1408):
    s = up.shape[0]
    ts, tc = _tile(s, ts, 8), _tile(D_FF, tc)
    nc, nt = D_FF // tc, s // ts

    def body(g_ref, gp_ref, v_ref, vp_ref, wg_ref, wv_ref, o_ref):
        first = pl.program_id(1) == 0

        def conv(x_ref, p_ref, w_ref):
            x = x_ref[...]
            prev = jnp.where(first, 0.0, p_ref[...])
            x1, x2 = _shift_down(x, prev)
            return _conv(x, x1, x2, w_ref[...])

        ug = conv(g_ref, gp_ref, wg_ref)
        uv = conv(v_ref, vp_ref, wv_ref)
        _, silu = _silu_parts(ug)
        o_ref[...] = (silu * uv).astype(BF16)

    g_main, g_prev, _ = _halo_specs(ts, tc, 0, nt)
    v_main, v_prev, _ = _halo_specs(ts, tc, nc, nt)
    return pl.pallas_call(
        body, name=name, grid=(nc, nt),
        in_specs=[g_main, g_prev, v_main, v_prev,
                  pl.BlockSpec((3, tc), lambda j, i: (0, j)), pl.BlockSpec((3, tc), lambda j, i: (0, j + nc))],
        out_specs=pl.BlockSpec((ts, tc), lambda j, i: (i, j)),
        out_shape=jax.ShapeDtypeStruct((s, D_FF), BF16),
        compiler_params=_params("parallel", "parallel"),
    )(up, up, up, up, cw, cw)


def _ffn_act_bwd(up, cw, da, name, ts=256, tc=1408):
    s = up.shape[0]
    ts, tc = _tile(s, ts, 8), _tile(D_FF, tc)
    nc, nt = D_FF // tc, s // ts

    def body(g_ref, gp_ref, gn_ref, v_ref, vp_ref, vn_ref, da_ref, dan_ref, wg_ref, wv_ref,
             d_ref, dwg_ref, dwv_ref):
        i = pl.program_id(1)
        first, last = i == 0, i == nt - 1
        wg, wv = wg_ref[...], wv_ref[...]
        g, v = g_ref[...], v_ref[...]
        g1, g2 = _shift_down(g, jnp.where(first, 0.0, gp_ref[...]))
        v1, v2 = _shift_down(v, jnp.where(first, 0.0, vp_ref[...]))

        def d_u(ug, uv, da_v):
            sig, silu = _silu_parts(ug)
            return da_v * uv * (sig * (1.0 + ug * (1.0 - sig))), da_v * silu

        dug, duv = d_u(_conv(g, g1, g2, wg), _conv(v, v1, v2, wv), da_ref[...])
        gn, vn = gn_ref[...], vn_ref[...]
        gn1, gn2 = _shift_down(gn, g[ts - 8:, :])
        vn1, vn2 = _shift_down(vn, v[ts - 8:, :])
        dugn, duvn = d_u(_conv(gn, gn1, gn2, wg), _conv(vn, vn1, vn2, wv), dan_ref[...])
        dugn = jnp.where(last, 0.0, dugn)
        duvn = jnp.where(last, 0.0, duvn)

        def finish(du, dun, x, x1, x2, w, dx_ref, dw_ref):
            d1, d2 = _shift_up(du, dun)
            dx_ref[...] = (w[2:3, :] * du + w[1:2, :] * d1 + w[0:1, :] * d2).astype(BF16)

            @pl.when(first)
            def _():
                dw_ref[...] = jnp.zeros_like(dw_ref)

            dw_ref[0:1, :] += jnp.sum(du * x2, axis=0, keepdims=True)
            dw_ref[1:2, :] += jnp.sum(du * x1, axis=0, keepdims=True)
            dw_ref[2:3, :] += jnp.sum(du * x, axis=0, keepdims=True)

        finish(dug, dugn, g, g1, g2, wg, d_ref.at[0], dwg_ref)
        finish(duv, duvn, v, v1, v2, wv, d_ref.at[1], dwv_ref)

    g_specs = _halo_specs(ts, tc, 0, nt)
    v_specs = _halo_specs(ts, tc, nc, nt)
    da_main, _, da_next = _halo_specs(ts, tc, 0, nt)
    taps = pl.BlockSpec((8, tc), lambda j, i: (0, j))
    halves = pl.BlockSpec((2, ts, tc), lambda j, i: (0, i, j))
    d, dwg, dwv = pl.pallas_call(
        body, name=name, grid=(nc, nt),
        in_specs=[*g_specs, *v_specs, da_main, da_next,
                  pl.BlockSpec((3, tc), lambda j, i: (0, j)), pl.BlockSpec((3, tc), lambda j, i: (0, j + nc))],
        out_specs=[halves, taps, taps],
        out_shape=[jax.ShapeDtypeStruct((2, s, D_FF), BF16),
                   jax.ShapeDtypeStruct((8, D_FF), F32), jax.ShapeDtypeStruct((8, D_FF), F32)],
        compiler_params=_params("parallel", "arbitrary"),
    )(up, up, up, up, up, up, da, da, cw, cw)
    return d, dwg[:3], dwv[:3]


def _sconv_fwd(proj, ck, name, ts=256, tc=512):
    s = proj.shape[0]
    w = D_MODEL
    ts, tc = _tile(s, ts, 8), _tile(w, tc)
    nc, nt = w // tc, s // ts

    def body(b_ref, c_ref, cp_ref, u_ref, up_ref, w_ref, o_ref):
        first = pl.program_id(1) == 0
        cu = c_ref[...] * u_ref[...]
        cup = jnp.where(first, 0.0, cp_ref[...] * up_ref[...])
        x1, x2 = _shift_down(cu, cup)
        o_ref[...] = (b_ref[...] * _conv(cu, x1, x2, w_ref[...])).astype(BF16)

    b_main, _, _ = _halo_specs(ts, tc, 0, nt)
    c_main, c_prev, _ = _halo_specs(ts, tc, nc, nt)
    u_main, u_prev, _ = _halo_specs(ts, tc, 2 * nc, nt)
    return pl.pallas_call(
        body, name=name, grid=(nc, nt),
        in_specs=[b_main, c_main, c_prev, u_main, u_prev, pl.BlockSpec((3, tc), lambda j, i: (0, j))],
        out_specs=pl.BlockSpec((ts, tc), lambda j, i: (i, j)),
        out_shape=jax.ShapeDtypeStruct((s, w), BF16),
        compiler_params=_params("parallel", "parallel"),
    )(proj, proj, proj, proj, proj, ck)


def _sconv_bwd(proj, ck, dy, name, ts=256, tc=512):
    s = proj.shape[0]
    w = D_MODEL
    ts, tc = _tile(s, ts, 8), _tile(w, tc)
    nc, nt = w // tc, s // ts

    def body(b_ref, bn_ref, c_ref, cp_ref, u_ref, up_ref, dy_ref, dyn_ref, w_ref,
             d_ref, dw_ref):
        i = pl.program_id(1)
        first, last = i == 0, i == nt - 1
        wv = w_ref[...]
        b, c, u, dy_v = b_ref[...], c_ref[...], u_ref[...], dy_ref[...]
        cu = c * u
        cup = jnp.where(first, 0.0, cp_ref[...] * up_ref[...])
        x1, x2 = _shift_down(cu, cup)
        d_ref[0] = (dy_v * _conv(cu, x1, x2, wv)).astype(BF16)
        dcv = dy_v * b
        dcvn = jnp.where(last, 0.0, dyn_ref[...] * bn_ref[...])
        d1, d2 = _shift_up(dcv, dcvn)
        dcu = wv[2:3, :] * dcv + wv[1:2, :] * d1 + wv[0:1, :] * d2
        d_ref[1] = (dcu * u).astype(BF16)
        d_ref[2] = (dcu * c).astype(BF16)

        @pl.when(first)
        def _():
            dw_ref[...] = jnp.zeros_like(dw_ref)

        dw_ref[0:1, :] += jnp.sum(dcv * x2, axis=0, keepdims=True)
        dw_ref[1:2, :] += jnp.sum(dcv * x1, axis=0, keepdims=True)
        dw_ref[2:3, :] += jnp.sum(dcv * cu, axis=0, keepdims=True)

    b_main, _, b_next = _halo_specs(ts, tc, 0, nt)
    c_main, c_prev, _ = _halo_specs(ts, tc, nc, nt)
    u_main, u_prev, _ = _halo_specs(ts, tc, 2 * nc, nt)
    dy_main, _, dy_next = _halo_specs(ts, tc, 0, nt)
    d, dw = pl.pallas_call(
        body, name=name, grid=(nc, nt),
        in_specs=[b_main, b_next, c_main, c_prev, u_main, u_prev, dy_main, dy_next,
                  pl.BlockSpec((3, tc), lambda j, i: (0, j))],
        out_specs=[pl.BlockSpec((3, ts, tc), lambda j, i: (0, i, j)), pl.BlockSpec((8, tc), lambda j, i: (0, j))],
        out_shape=[jax.ShapeDtypeStruct((3, s, w), BF16), jax.ShapeDtypeStruct((8, w), F32)],
        compiler_params=_params("parallel", "arbitrary"),
    )(proj, proj, proj, proj, proj, proj, dy, dy, ck)
    return d, dw[:3]


def _low_lanes(shape):
    return lax.broadcasted_iota(jnp.int32, shape, 1) < HEAD_DIM


def _top_rows(shape):
    return lax.broadcasted_iota(jnp.int32, shape, 0) < HEAD_DIM


def _norm_pair(x):
    r = lax.rsqrt(_mean_pair(x * x) + EPS)
    return x * r, r


def _mean_pair(x):
    same_head = _tri(LANES, lambda a, b: a // HEAD_DIM == b // HEAD_DIM)
    return _split_dot(x, same_head, 3) * (1.0 / HEAD_DIM)


def _qkv_prep(proj, gq, gk, name, ts=512):
    s = proj.shape[0]
    ts = _tile(s, ts)
    npair = MIX // LANES
    scale = HEAD_DIM ** -0.5

    def body(q_ref, k_ref, v_ref, gq_ref, gk_ref, qo_ref, ko_ref, vo_ref, qt_ref, kt_ref, vt_ref):
        qn, _ = _norm_pair(q_ref[...])
        kn, _ = _norm_pair(k_ref[...])
        q = qn * gq_ref[0] * scale
        k = kn * gk_ref[0]
        v = v_ref[...]
        qo_ref[...] = q.astype(BF16)
        ko_ref[...] = k.astype(BF16)
        vo_ref[...] = v.astype(BF16)
        qt_ref[...] = q.T.astype(BF16)
        kt_ref[...] = k.T.astype(BF16)
        vt_ref[...] = v.T.astype(BF16)

    gain = pl.BlockSpec((1, 1, LANES), lambda i, p: (p, 0, 0))
    tile = pl.BlockSpec((ts, LANES), lambda i, p: (i, p))
    tile_t = pl.BlockSpec((LANES, ts), lambda i, p: (p, i))
    out = jax.ShapeDtypeStruct((s, MIX), BF16)
    out_t = jax.ShapeDtypeStruct((MIX, s), BF16)
    return pl.pallas_call(
        body, name=name, grid=(s // ts, npair),
        in_specs=[tile, pl.BlockSpec((ts, LANES), lambda i, p: (i, p + npair)),
                  pl.BlockSpec((ts, LANES), lambda i, p: (i, p + 2 * npair)), gain, gain],
        out_specs=[tile, tile, tile, tile_t, tile_t, tile_t], out_shape=[out, out, out, out_t, out_t, out_t],
        compiler_params=_params("parallel", "parallel"),
    )(proj, proj, proj, gq, gk)


def _qkv_prep_bwd(proj, gq, gk, dqs, dks, dvs, name, ts=512):
    s = proj.shape[0]
    ts = _tile(s, ts, 8)
    npair = MIX // LANES
    half = npair // 2
    scale = HEAD_DIM ** -0.5

    def body(q_ref, k_ref, gq_ref, gk_ref, dqf_ref, dqs_ref, dkf_ref, dks_ref, dvf_ref, dvs_ref,
             dq_ref, dk_ref, dv_ref, dgq_ref, dgk_ref):
        p, i = pl.program_id(0), pl.program_id(1)
        fox = p < half

        def one(x_ref, g_ref, df_ref, ds_ref, dx_ref, dg_ref, mult):
            dn = jnp.where(fox, df_ref[...], ds_ref[...]) * mult
            xh, r = _norm_pair(x_ref[...])

            @pl.when(i == 0)
            def _():
                dg_ref[...] = jnp.zeros_like(dg_ref)

            dg_ref[0, 0:1, :] += jnp.sum(dn * xh, axis=0, keepdims=True)
            dxh = dn * g_ref[0]
            dx_ref[...] = (r * (dxh - xh * _mean_pair(dxh * xh))).astype(BF16)

        one(q_ref, gq_ref, dqf_ref, dqs_ref, dq_ref, dgq_ref, scale)
        one(k_ref, gk_ref, dkf_ref, dks_ref, dk_ref, dgk_ref, 1.0)
        dv_ref[...] = jnp.where(fox, dvf_ref[...], dvs_ref[...]).astype(BF16)

    gain = pl.BlockSpec((1, 1, LANES), lambda p, i: (p, 0, 0))
    tile = pl.BlockSpec((ts, LANES), lambda p, i: (i, p))
    fpart = pl.BlockSpec((ts, LANES), lambda p, i: (i, jnp.minimum(p, half - 1)))
    spart = pl.BlockSpec((ts, LANES), lambda p, i: (i, jnp.maximum(p - half, 0)))
    dgain = pl.BlockSpec((1, 8, LANES), lambda p, i: (p, 0, 0))
    out = jax.ShapeDtypeStruct((s, MIX), BF16)
    gshape = jax.ShapeDtypeStruct((npair, 8, LANES), F32)
    dq, dk, dv, dgq, dgk = pl.pallas_call(
        body, name=name, grid=(npair, s // ts),
        in_specs=[tile, pl.BlockSpec((ts, LANES), lambda p, i: (i, p + npair)), gain, gain,
                  fpart, spart, fpart, spart, fpart, spart],
        out_specs=[tile, tile, tile, dgain, dgain], out_shape=[out, out, out, gshape, gshape],
        compiler_params=_params("parallel", "arbitrary"),
    )(proj, proj, gq, gk, dqs[0], dqs[1], dks[0], dks[1], dvs[0], dvs[1])
    return dq, dk, dv, dgq[:, 0, :], dgk[:, 0, :]


def _tri(n, rel):
    a = lax.broadcasted_iota(jnp.int32, (n, n), 0)
    b = lax.broadcasted_iota(jnp.int32, (n, n), 1)
    return rel(a, b).astype(BF16)


def _fgate_fwd(logit, bias, name):
    nh, r, _ = logit.shape

    def body(x_ref, b_ref, o_ref):
        within = _tri(LANES, lambda a, b: a <= b)
        before = _tri(r, lambda a, b: b < a)
        for hh in range(nh):
            x = x_ref[hh] + b_ref[hh]
            lf = jnp.minimum(x, 0.0) - jnp.log1p(jnp.exp(-jnp.abs(x)))
            c = _split_dot(lf, within, 3)
            tot = jnp.broadcast_to(c[:, LANES - 1:LANES], (r, LANES))
            o_ref[hh] = c + _split_dot_left(before, tot, 3)

    return pl.pallas_call(
        body, name=name, out_shape=jax.ShapeDtypeStruct((nh, r, LANES), F32),
        in_specs=[pl.BlockSpec(memory_space=pltpu.VMEM), pl.BlockSpec(memory_space=pltpu.SMEM)],
        out_specs=pl.BlockSpec(memory_space=pltpu.VMEM),
    )(logit, bias)


def _fgate_bwd(logit, bias, dcum, name):
    nh, r, _ = logit.shape

    def body(x_ref, b_ref, d_ref, dx_ref, db_ref):
        within = _tri(LANES, lambda a, b: a >= b)
        after = _tri(r, lambda a, b: b > a)
        for hh in range(nh):
            x = x_ref[hh] + b_ref[hh]
            d = d_ref[hh]
            c = _split_dot(d, within, 3)
            tot = jnp.broadcast_to(c[:, 0:1], (r, LANES))
            dlf = c + _split_dot_left(after, tot, 3)
            dx = dlf * (1.0 / (1.0 + jnp.exp(x)))
            dx_ref[hh] = dx
            db_ref[hh:hh + 1, :] = jnp.broadcast_to(jnp.sum(dx, keepdims=True).reshape(1, 1), (1, LANES))

    return pl.pallas_call(
        body, name=name,
        out_shape=[jax.ShapeDtypeStruct((nh, r, LANES), F32), jax.ShapeDtypeStruct((nh, LANES), F32)],
        in_specs=[pl.BlockSpec(memory_space=pltpu.VMEM), pl.BlockSpec(memory_space=pltpu.SMEM),
                  pl.BlockSpec(memory_space=pltpu.VMEM)],
        out_specs=[pl.BlockSpec(memory_space=pltpu.VMEM), pl.BlockSpec(memory_space=pltpu.VMEM)],
    )(logit, bias, dcum)


def _pair_masks(x):
    lo = _low_lanes(x.shape)
    zero = jnp.zeros_like(x)
    return jnp.where(lo, x, zero), jnp.where(lo, zero, x)


def _pair_masks_t(x):
    top = _top_rows(x.shape)
    zero = jnp.zeros_like(x)
    return jnp.where(top, x, zero), jnp.where(top, zero, x)


def _stack_heads(x):
    return jnp.concatenate(_pair_masks(x), axis=0)


def _stack_heads_t(x):
    return jnp.concatenate(_pair_masks_t(x), axis=1)


def _pair_colsum_t(x):
    top = _top_rows(x.shape)
    return (jnp.sum(jnp.where(top, x, 0.0), axis=0, keepdims=True),
            jnp.sum(jnp.where(top, 0.0, x), axis=0, keepdims=True))


def _key_query_iotas(t):
    return lax.broadcasted_iota(jnp.int32, (t, t), 0), lax.broadcasted_iota(jnp.int32, (t, t), 1)


def _walk_blocks(i, step, descending, group=2):
    full = i // group
    left = i - full * group

    def run(first, count):
        sign = -1 if descending else 1
        step([(first + sign * n, False) for n in range(count)])

    def leftovers():
        start = (left - 1) if descending else full * group
        sign = -1 if descending else 1
        if group == 4:
            @pl.when(left >= 2)
            def _():
                run(start, 2)

            @pl.when(left % 2 == 1)
            def _():
                run(0 if descending else i - 1, 1)
        else:
            @pl.when(left == 1)
            def _():
                run(start, 1)

    def loop(g, carry):
        run((i - 1 - group * g) if descending else group * g, group)
        return carry

    if descending:
        step([(i, True)])
        lax.fori_loop(0, full, loop, 0)
        leftovers()
    else:
        lax.fori_loop(0, full, loop, 0)
        leftovers()
        step([(i, True)])


def _attn_specs(s, tq, pair0):
    q_nat = pl.BlockSpec((tq, LANES), lambda p, i: (i, p + pair0))
    q_t = pl.BlockSpec((LANES, tq), lambda p, i: (p + pair0, i))
    k_nat = pl.BlockSpec((s, LANES), lambda p, i: (0, p + pair0))
    k_t = pl.BlockSpec((LANES, s), lambda p, i: (p + pair0, 0))
    return q_nat, q_t, k_nat, k_t


def _fox_fwd(qt, kh, vt, frow, fcol, name, tq=256):
    s = kh.shape[0]
    tq = _tile(s, tq)
    nq, half = s // tq, MIX // LANES // 2

    def body(qt_ref, k_ref, vt_ref, fr_ref, fc_ref, o_ref, lse_ref, m_s, l_s, acc_s):
        i = pl.program_id(1)
        qt_v = qt_ref[...]
        ft = fr_ref[0]
        key, qry = _key_query_iotas(tq)
        causal = key <= qry
        m_s[...] = jnp.full(m_s.shape, NEG, F32)
        l_s[...] = jnp.zeros_like(l_s)
        acc_s[...] = jnp.zeros_like(acc_s)

        top = _top_rows((LANES, tq))

        def step(blocks):
            rows = [pl.ds(pl.multiple_of(j * tq, tq), tq) for j, _ in blocks]
            zs = [_dot(_stack_heads(k_ref[r, :]), qt_v) for r in rows]
            m_cur, l_cur = [m_s[0], m_s[1]], [l_s[0], l_s[1]]
            acc = acc_s[...]
            for b, (_, masked) in enumerate(blocks):
                fk = fc_ref[0, rows[b], :]
                prs, alphas = [], []
                for hh in range(2):
                    sc = zs[b][hh * tq:(hh + 1) * tq] + (ft[hh:hh + 1, :] - fk[:, hh:hh + 1])
                    if masked:
                        sc = jnp.where(causal, sc, NEG)
                    m_new = jnp.maximum(m_cur[hh], jnp.max(sc, axis=0, keepdims=True))
                    alpha = jnp.exp(m_cur[hh] - m_new)
                    pr = jnp.exp(sc - m_new)
                    l_cur[hh] = alpha * l_cur[hh] + jnp.sum(pr, axis=0, keepdims=True)
                    m_cur[hh] = m_new
                    prs.append(pr.astype(BF16))
                    alphas.append(alpha)
                pv = _dot(_stack_heads_t(vt_ref[:, rows[b]]), jnp.concatenate(prs, axis=0))
                acc = jnp.where(top, alphas[0], alphas[1]) * acc + pv
            acc_s[...] = acc
            for hh in range(2):
                m_s[hh] = m_cur[hh]
                l_s[hh] = l_cur[hh]

        _walk_blocks(i, step, descending=False, group=4)
        o_ref[...] = (acc_s[...] / jnp.where(top, l_s[0], l_s[1])).T
        lse_ref[0, 0:1, :] = m_s[0] + jnp.log(l_s[0])
        lse_ref[0, 1:2, :] = m_s[1] + jnp.log(l_s[1])

    _, q_t, k_nat, k_t = _attn_specs(s, tq, 0)
    qstat = pl.BlockSpec((1, 2, tq), lambda p, i: (p, 0, i))
    return pl.pallas_call(
        body, name=name, grid=(half, nq),
        in_specs=[q_t, k_nat, k_t, qstat, pl.BlockSpec((1, s, 2), lambda p, i: (p, 0, 0))],
        out_specs=[pl.BlockSpec((tq, LANES), lambda p, i: (i, p)), qstat],
        out_shape=[jax.ShapeDtypeStruct((s, MIX // 2), F32), jax.ShapeDtypeStruct((half, 2, s), F32)],
        scratch_shapes=[pltpu.VMEM((2, 1, tq), F32), pltpu.VMEM((2, 1, tq), F32), pltpu.VMEM((LANES, tq), F32)],
        compiler_params=_params("parallel", "arbitrary"),
    )(qt, kh, vt, frow, fcol)


def _fox_bwd(qh, qt, kh, kt, vb, frow, fcol, lse, o, do, name, tq=256):
    s = kh.shape[0]
    tq = _tile(s, tq)
    nq, half = s // tq, MIX // LANES // 2

    def body(q_ref, qt_ref, k_ref, kt_ref, v_ref, fr_ref, fc_ref, lse_ref, o_ref, do_ref,
             dq_ref, dk_ref, dv_ref, dfk_ref, dfq_ref, dq_s, rs_s):
        i = pl.program_id(1)

        @pl.when(i == 0)
        def _():
            dk_ref[...] = jnp.zeros_like(dk_ref)
            dv_ref[...] = jnp.zeros_like(dv_ref)
            dfk_ref[...] = jnp.zeros_like(dfk_ref)

        q2 = _stack_heads(q_ref[...])
        qt_v = qt_ref[...]
        do_v = do_ref[...]
        do2 = _stack_heads(do_v.astype(BF16))
        dot_v = do_v.T.astype(BF16)
        dsum = _pair_colsum_t((do_v * o_ref[...]).T)
        ft, ls = fr_ref[0], lse_ref[0]
        key, qry = _key_query_iotas(tq)
        causal = key <= qry
        lane = lax.broadcasted_iota(jnp.int32, (2 * tq, LANES), 0) // tq
        pick2 = (lax.broadcasted_iota(jnp.int32, (2 * tq, LANES), 1) == lane).astype(BF16)
        q2_pick = jnp.concatenate([q2, pick2], axis=1)
        dq_s[...] = jnp.zeros_like(dq_s)
        rs_s[...] = jnp.zeros_like(rs_s)

        def step(blocks):
            rows = [pl.ds(pl.multiple_of(j * tq, tq), tq) for j, _ in blocks]
            zs = [_dot(_stack_heads(k_ref[r, :]), qt_v) for r in rows]
            dps = [_dot(_stack_heads(v_ref[r, :]), dot_v) for r in rows]
            rs = [rs_s[0], rs_s[1]]
            dq = None
            for b, (_, masked) in enumerate(blocks):
                fk = fc_ref[0, rows[b], :]
                prs, dss = [], []
                for hh in range(2):
                    blk = slice(hh * tq, (hh + 1) * tq)
                    sc = zs[b][blk] + (ft[hh:hh + 1, :] - fk[:, hh:hh + 1])
                    pr = jnp.exp(sc - ls[hh:hh + 1, :])
                    if masked:
                        pr = jnp.where(causal, pr, 0.0)
                    dsb = (pr * (dps[b][blk] - dsum[hh])).astype(BF16)
                    rs[hh] = rs[hh] + jnp.sum(dsb.astype(F32), axis=0, keepdims=True)
                    prs.append(pr.astype(BF16))
                    dss.append(dsb)
                dv_ref[rows[b], :] += _dot(jnp.concatenate(prs, axis=1), do2)
                both = _dot(jnp.concatenate(dss, axis=1), q2_pick)
                dk_ref[rows[b], :] += both[:, :LANES]
                dfk_ref[0, rows[b], :] -= both[:, LANES:]
                term = _dot(_stack_heads_t(kt_ref[:, rows[b]]), jnp.concatenate(dss, axis=0))
                dq = term if dq is None else dq + term
            rs_s[0], rs_s[1] = rs
            dq_s[...] += dq

        _walk_blocks(i, step, descending=False, group=2)
        dq_ref[...] = dq_s[...].T
        dfq_ref[0, 0:1, :] = rs_s[0]
        dfq_ref[0, 1:2, :] = rs_s[1]

    q_nat, q_t, k_nat, k_t = _attn_specs(s, tq, 0)
    qstat = pl.BlockSpec((1, 2, tq), lambda p, i: (p, 0, i))
    otile = pl.BlockSpec((tq, LANES), lambda p, i: (i, p))
    oresident = pl.BlockSpec((s, LANES), lambda p, i: (0, p))
    out = jax.ShapeDtypeStruct((s, MIX // 2), F32)
    return pl.pallas_call(
        body, name=name, grid=(half, nq),
        in_specs=[q_nat, q_t, k_nat, k_t, k_nat, qstat, pl.BlockSpec((1, s, 2), lambda p, i: (p, 0, 0)), qstat,
                  otile, q_nat],
        out_specs=[otile, oresident, oresident, pl.BlockSpec((1, s, LANES), lambda p, i: (p, 0, 0)), qstat],
        out_shape=[out, out, out, jax.ShapeDtypeStruct((half, s, LANES), F32),
                   jax.ShapeDtypeStruct((half, 2, s), F32)],
        scratch_shapes=[pltpu.VMEM((LANES, tq), F32), pltpu.VMEM((2, 1, tq), F32)],
        compiler_params=_params("parallel", "arbitrary"),
    )(qh, qt, kh, kt, vb, frow, fcol, lse, o, do)


def _log_sig_pair(z):
    lb = jnp.minimum(z, 0.0) - jnp.log(1.0 + jnp.exp(-jnp.abs(z)))
    return lb, lb - z


def _sb_fwd(qt, kh, vt, name, tq=256, gather=()):
    s = kh.shape[0]
    tq = _tile(s, tq)
    nq, half = s // tq, MIX // LANES // 2
    ng = len(gather)

    def body(*refs):
        qt_ref, k_ref, vt_ref = refs[:3]
        o_ref, tot_ref = refs[3 + ng:5 + ng]
        c_s, acc_s = refs[5 + 2 * ng:7 + 2 * ng]
        p, i = pl.program_id(0), pl.program_id(1)
        if ng:
            start, relay, finish = _gather_stages(refs[3:3 + ng], refs[5 + ng:5 + 2 * ng], *refs[7 + 2 * ng:])
            pl.when((p == 0) & (i == 0))(start)
            pl.when((p == half - 1) & (i == 0))(relay)
        qt_v = qt_ref[...]
        key, qry = _key_query_iotas(tq)
        strict = key < qry
        later = _tri(tq, lambda a, b: b > a)
        c_s[...] = jnp.zeros_like(c_s)
        acc_s[...] = jnp.zeros_like(acc_s)

        def step(blocks):
            rows = [pl.ds(pl.multiple_of(j * tq, tq), tq) for j, _ in blocks]
            zs = [_dot(_stack_heads(k_ref[r, :]), qt_v) for r in rows]
            lbs, loms, afters = [], [], []
            for b, (_, masked) in enumerate(blocks):
                for hh in range(2):
                    lb, lom = _log_sig_pair(zs[b][hh * tq:(hh + 1) * tq])
                    if masked:
                        lom = jnp.where(strict, lom, 0.0)
                    lbs.append(lb)
                    loms.append(lom)
                afters.append(_split_dot_left(later, jnp.concatenate(loms[2 * b:2 * b + 2], axis=1), 2))
            carry = [c_s[0], c_s[1]]
            pv = None
            for b, (_, masked) in enumerate(blocks):
                ws = []
                for hh in range(2):
                    n = 2 * b + hh
                    w = jnp.exp(lbs[n] + afters[b][:, hh * tq:(hh + 1) * tq] + carry[hh])
                    if masked:
                        w = jnp.where(strict, w, 0.0)
                    ws.append(w.astype(BF16))
                    carry[hh] = carry[hh] + jnp.sum(loms[n], axis=0, keepdims=True)
                term = _dot(_stack_heads_t(vt_ref[:, rows[b]]), jnp.concatenate(ws, axis=0))
                pv = term if pv is None else pv + term
            c_s[0], c_s[1] = carry
            acc_s[...] += pv

        _walk_blocks(i, step, descending=True, group=4)
        o_ref[...] = acc_s[...].T
        tot_ref[0, 0:1, :] = c_s[0]
        tot_ref[0, 1:2, :] = c_s[1]
        if ng:
            pl.when((p == half - 1) & (i == nq - 1))(finish)

    _, q_t, k_nat, k_t = _attn_specs(s, tq, half)
    qstat = pl.BlockSpec((1, 2, tq), lambda p, i: (p, 0, i))
    return pl.pallas_call(
        body, name=name, grid=(half, nq),
        in_specs=[q_t, k_nat, k_t] + [ANY] * ng,
        out_specs=[pl.BlockSpec((tq, LANES), lambda p, i: (i, p)), qstat] + [ANY] * ng,
        out_shape=[jax.ShapeDtypeStruct((s, MIX // 2), F32), jax.ShapeDtypeStruct((half, 2, s), F32)]
        + [jax.ShapeDtypeStruct((N_DEV,) + x.shape, x.dtype) for x in gather],
        scratch_shapes=[pltpu.VMEM((2, 1, tq), F32), pltpu.VMEM((LANES, tq), F32)]
        + (_gather_scratch(ng) if ng else []),
        compiler_params=_params("arbitrary", "arbitrary") if ng else _params("parallel", "arbitrary"),
    )(qt, kh, vt, *gather)


def _sb_bwd(qh, qt, kh, kt, vb, tot, do, name, tq=256, exchange=()):
    s = kh.shape[0]
    tq = _tile(s, tq)
    nq, half = s // tq, MIX // LANES // 2
    nx = len(exchange)

    def body(*refs):
        q_ref, qt_ref, k_ref, kt_ref, v_ref, tot_ref, do_ref = refs[:7]
        dq_ref, dk_ref, dv_ref = refs[7 + nx:10 + nx]
        rem_s, pg_s, dq_s = refs[10 + 2 * nx:13 + 2 * nx]
        p, i = pl.program_id(0), pl.program_id(1)
        if nx:
            start, finish = _chip_exchange_stages(refs[7:7 + nx], refs[10 + nx:10 + 2 * nx], *refs[13 + 2 * nx:])
            pl.when((p == 0) & (i == 0))(start)

        @pl.when(i == 0)
        def _():
            dk_ref[...] = jnp.zeros_like(dk_ref)
            dv_ref[...] = jnp.zeros_like(dv_ref)

        q2 = _stack_heads(q_ref[...])
        qt_v = qt_ref[...]
        do_v = do_ref[...]
        do2 = _stack_heads(do_v.astype(BF16))
        dot_v = do_v.T.astype(BF16)
        key, qry = _key_query_iotas(tq)
        strict = key < qry
        upto = _tri(tq, lambda a, b: b <= a)
        before = _tri(tq, lambda a, b: b < a)
        tv = tot_ref[0]
        rem_s[0] = tv[0:1, :]
        rem_s[1] = tv[1:2, :]
        pg_s[...] = jnp.zeros_like(pg_s)
        dq_s[...] = jnp.zeros_like(dq_s)

        def step(blocks):
            nb = len(blocks)
            rows = [pl.ds(pl.multiple_of(j * tq, tq), tq) for j, _ in blocks]
            zs = [_dot(_stack_heads(k_ref[r, :]), qt_v) for r in rows]
            dws = [_dot(_stack_heads(v_ref[r, :]), dot_v) for r in rows]
            lbs, loms, prefixes = [], [], []
            for b, (_, masked) in enumerate(blocks):
                for hh in range(2):
                    lb, lom = _log_sig_pair(zs[b][hh * tq:(hh + 1) * tq])
                    if masked:
                        lom = jnp.where(strict, lom, 0.0)
                    lbs.append(lb)
                    loms.append(lom)
                prefixes.append(_split_dot_left(upto, jnp.concatenate(loms[2 * b:2 * b + 2], axis=1), 2))
            rem = [rem_s[0], rem_s[1]]
            ws, gs, gpres = [], [], []
            for b, (_, masked) in enumerate(blocks):
                for hh in range(2):
                    n = 2 * b + hh
                    blk = slice(hh * tq, (hh + 1) * tq)
                    w = jnp.exp(lbs[n] + (rem[hh] - prefixes[b][:, blk]))
                    if masked:
                        w = jnp.where(strict, w, 0.0)
                    gs.append(dws[b][blk] * w)
                    ws.append(w.astype(BF16))
                    rem[hh] = rem[hh] - jnp.sum(loms[n], axis=0, keepdims=True)
                gpres.append(_dot(before, jnp.concatenate(gs[2 * b:2 * b + 2], axis=1).astype(BF16)))
                dv_ref[rows[b], :] += _dot(jnp.concatenate(ws[2 * b:2 * b + 2], axis=1), do2)
            rem_s[0], rem_s[1] = rem
            pg = [pg_s[0], pg_s[1]]
            dq = None
            for b, (_, masked) in enumerate(blocks):
                dzs = []
                for hh in range(2):
                    n = 2 * b + hh
                    g = gs[n]
                    dz = g - jnp.exp(lbs[n]) * (g + (pg[hh] + gpres[b][:, hh * tq:(hh + 1) * tq]))
                    if masked:
                        dz = jnp.where(strict, dz, 0.0)
                    dzs.append(dz.astype(BF16))
                    pg[hh] = pg[hh] + jnp.sum(g, axis=0, keepdims=True)
                dk_ref[rows[b], :] += _dot(jnp.concatenate(dzs, axis=1), q2)
                term = _dot(_stack_heads_t(kt_ref[:, rows[b]]), jnp.concatenate(dzs, axis=0))
                dq = term if dq is None else dq + term
            pg_s[0], pg_s[1] = pg
            dq_s[...] += dq

        _walk_blocks(i, step, descending=False, group=2)
        dq_ref[...] = dq_s[...].T
        if nx:
            pl.when((p == half - 1) & (i == nq - 1))(finish)

    q_nat, q_t, k_nat, k_t = _attn_specs(s, tq, half)
    qstat = pl.BlockSpec((1, 2, tq), lambda p, i: (p, 0, i))
    otile = pl.BlockSpec((tq, LANES), lambda p, i: (i, p))
    oresident = pl.BlockSpec((s, LANES), lambda p, i: (0, p))
    out = jax.ShapeDtypeStruct((s, MIX // 2), F32)
    return pl.pallas_call(
        body, name=name, grid=(half, nq),
        in_specs=[q_nat, q_t, k_nat, k_t, k_nat, qstat, q_nat] + [ANY] * nx,
        out_specs=[otile, oresident, oresident] + [ANY] * nx,
        out_shape=[out, out, out] + [jax.ShapeDtypeStruct(x.shape, x.dtype) for x in exchange],
        scratch_shapes=[pltpu.VMEM((2, 1, tq), F32), pltpu.VMEM((2, 1, tq), F32), pltpu.VMEM((LANES, tq), F32)]
        + (_chip_exchange_scratch(nx) if nx else []),
        compiler_params=_params("arbitrary", "arbitrary") if nx else _params("parallel", "arbitrary"),
    )(qh, qt, kh, kt, vb, tot, do, *exchange)


def _loss_head(y, target, name, ts=512):
    s, d = y.shape
    ts = _tile(s, ts, 8)
    nt = s // ts

    def body(y_ref, t_ref, dy_ref, l_ref, acc):
        i = pl.program_id(0)
        err = y_ref[...] - t_ref[...]
        dy_ref[...] = err * (1.0 / d)

        @pl.when(i == 0)
        def _():
            acc[...] = jnp.zeros_like(acc)

        acc[...] += jnp.sum(err * err, axis=0, keepdims=True)

        @pl.when(i == nt - 1)
        def _():
            tot = jnp.sum(acc[...], keepdims=True).reshape(1, 1) * (0.5 / d)
            l_ref[...] = jnp.broadcast_to(tot, l_ref.shape)

    row = pl.BlockSpec((ts, d), lambda i: (i, 0))
    dy, l = pl.pallas_call(
        body, name=name, grid=(nt,), in_specs=[row, row],
        out_specs=[row, pl.BlockSpec((8, LANES), lambda i: (0, 0))],
        out_shape=[jax.ShapeDtypeStruct((s, d), F32), jax.ShapeDtypeStruct((8, LANES), F32)],
        scratch_shapes=[pltpu.VMEM((1, d), F32)],
        compiler_params=_params("arbitrary"),
    )(y, target)
    return l[0, 0], dy


def _coords():
    return lax.axis_index("x"), lax.axis_index("y"), lax.axis_index("c")


def _other_chips(xi, yi):
    return [(1 - xi, yi), (xi, 1 - yi), (1 - xi, 1 - yi)]


def _gather_stages(x_refs, out_refs, send_sems, recv_sems, local_sems):
    n = len(x_refs)
    xi, yi, ci = _coords()
    me, sibling = (xi, yi, ci), (xi, yi, 1 - ci)
    chips = _other_chips(xi, yi)

    def slot(a, px, py, pc):
        return out_refs[a].at[4 * px + 2 * py + pc]

    def copy(a, k, block, to, src=None):
        return pltpu.make_async_remote_copy(
            src_ref=slot(a, *block) if src is None else src, dst_ref=slot(a, *block),
            send_sem=send_sems.at[a, k], recv_sem=recv_sems.at[a, k], device_id=to, device_id_type=MESH)

    def own(a):
        return pltpu.make_async_copy(x_refs[a], slot(a, *me), local_sems.at[a])

    def first(a):
        return [copy(a, 0, me, sibling, src=x_refs[a])] + [
            copy(a, 1 + j, me, (*chip, ci), src=x_refs[a]) for j, chip in enumerate(chips)]

    def passed(a, j):
        return copy(a, 4 + j, (*chips[j], ci), sibling)

    def start():
        for a in range(n):
            own(a).start()
        for a in range(n):
            for cp in first(a):
                cp.start()

    def relay():
        for j, chip in enumerate(chips):
            for a in range(n):
                copy(a, 1 + j, (*chip, ci), me).wait_recv()
                passed(a, j).start()

    def finish():
        for a in range(n):
            copy(a, 0, sibling, me).wait_recv()
            for j, chip in enumerate(chips):
                copy(a, 4 + j, (*chip, 1 - ci), me).wait_recv()
        for a in range(n):
            for cp in first(a) + [passed(a, j) for j in range(len(chips))]:
                cp.wait_send()
            own(a).wait()

    return start, relay, finish


def _gather_scratch(n):
    return [pltpu.SemaphoreType.DMA((n, 7)), pltpu.SemaphoreType.DMA((n, 7)), pltpu.SemaphoreType.DMA((n,))]


def _all_gather(xs, name):
    n = len(xs)

    def body(*refs):
        start, relay, finish = _gather_stages(refs[:n], refs[n:2 * n], *refs[2 * n:])
        start()
        relay()
        finish()

    return pl.pallas_call(
        body, name=name, out_shape=[jax.ShapeDtypeStruct((N_DEV,) + x.shape, x.dtype) for x in xs],
        in_specs=[ANY] * n, out_specs=[ANY] * n, scratch_shapes=_gather_scratch(n),
    )(*xs)


def _sibling_exchange(gs, name):
    n = len(gs)

    def body(*refs):
        g_refs, recv_refs = refs[:n], refs[n:2 * n]
        send_sems, recv_sems = refs[2 * n:]
        xi, yi, ci = _coords()
        cps = [pltpu.make_async_remote_copy(
            src_ref=g_refs[a].at[2 * chip + (1 - ci)], dst_ref=recv_refs[a].at[chip],
            send_sem=send_sems.at[a, chip], recv_sem=recv_sems.at[a, chip],
            device_id=(xi, yi, 1 - ci), device_id_type=MESH) for a in range(n) for chip in range(N_CHIP)]
        for cp in cps:
            cp.start()
        for cp in cps:
            cp.wait()

    return pl.pallas_call(
        body, name=name, out_shape=[jax.ShapeDtypeStruct((N_CHIP,) + g.shape[1:], g.dtype) for g in gs],
        in_specs=[ANY] * n, out_specs=[ANY] * n,
        scratch_shapes=[pltpu.SemaphoreType.DMA((n, N_CHIP)), pltpu.SemaphoreType.DMA((n, N_CHIP))],
    )(*gs)


def _pair_add(g, recv, ids, name, tr=256):
    _, r, c = g.shape
    tr = _tile(r, tr, 16)

    def body(ids_ref, g_ref, r_ref, p_ref, own_ref):
        kk = pl.program_id(1)
        tot = g_ref[0].astype(F32) + r_ref[0].astype(F32)
        p_ref[0] = tot.astype(BF16)

        @pl.when(kk == ids_ref[1])
        def _():
            own_ref[...] = tot

    grid_spec = pltpu.PrefetchScalarGridSpec(
        num_scalar_prefetch=1, grid=(r // tr, N_CHIP),
        in_specs=[pl.BlockSpec((1, tr, c), lambda i, kk, ids: (2 * kk + ids[0], i, 0)),
                  pl.BlockSpec((1, tr, c), lambda i, kk, ids: (kk, i, 0))],
        out_specs=[pl.BlockSpec((1, tr, c), lambda i, kk, ids: (kk, i, 0)),
                   pl.BlockSpec((tr, c), lambda i, kk, ids: (i, 0))])
    return pl.pallas_call(
        body, name=name, grid_spec=grid_spec,
        out_shape=[jax.ShapeDtypeStruct((N_CHIP, r, c), BF16), jax.ShapeDtypeStruct((r, c), F32)],
        compiler_params=_params("parallel", "arbitrary"),
    )(ids, g, recv)


def _chip_exchange_stages(p_refs, recv_refs, send_sems, recv_sems):
    n = len(p_refs)
    xi, yi, ci = _coords()
    mine = 2 * xi + yi
    chips = _other_chips(xi, yi)

    def copy(a, k, cx, cy):
        return pltpu.make_async_remote_copy(
            src_ref=p_refs[a].at[2 * cx + cy], dst_ref=recv_refs[a].at[mine],
            send_sem=send_sems.at[a, k], recv_sem=recv_sems.at[a, k],
            device_id=(cx, cy, ci), device_id_type=MESH)

    def landed(a, k, cx, cy):
        return pltpu.make_async_remote_copy(
            src_ref=p_refs[a].at[mine], dst_ref=recv_refs[a].at[2 * cx + cy],
            send_sem=send_sems.at[a, k], recv_sem=recv_sems.at[a, k],
            device_id=(cx, cy, ci), device_id_type=MESH)

    def start():
        for a in range(n):
            for k, (cx, cy) in enumerate(chips):
                copy(a, k, cx, cy).start()

    def finish():
        for a in range(n):
            for k, (cx, cy) in enumerate(chips):
                landed(a, k, cx, cy).wait_recv()
        for a in range(n):
            for k, (cx, cy) in enumerate(chips):
                copy(a, k, cx, cy).wait_send()

    return start, finish


def _chip_exchange_scratch(n):
    return [pltpu.SemaphoreType.DMA((n, 3)), pltpu.SemaphoreType.DMA((n, 3))]


def _chip_exchange(ps, name):
    n = len(ps)

    def body(*refs):
        start, finish = _chip_exchange_stages(refs[:n], refs[n:2 * n], *refs[2 * n:])
        start()
        finish()

    return pl.pallas_call(
        body, name=name, out_shape=[jax.ShapeDtypeStruct(p.shape, p.dtype) for p in ps],
        in_specs=[ANY] * n, out_specs=[ANY] * n, scratch_shapes=_chip_exchange_scratch(n),
    )(*ps)


def _adamw_math(w, g, m, v):
    m = ADAM_B1 * m + (1.0 - ADAM_B1) * g
    v = ADAM_B2 * v + (1.0 - ADAM_B2) * (g * g)
    m_hat = m / (1.0 - ADAM_B1 ** ADAM_STEP)
    v_hat = v / (1.0 - ADAM_B2 ** ADAM_STEP)
    delta = -ADAM_LR * (m_hat / (jnp.sqrt(v_hat) + ADAM_EPS) + ADAM_WD * w)
    return delta, m, v


def _adamw_reduce(own, recv, ids, w, m, v, name, tr=256):
    r, c = w.shape
    tr = _tile(r, tr, 16)

    def body(ids_ref, own_ref, recv_ref, w_ref, m_ref, v_ref, g_ref, d_ref, mo_ref, vo_ref):
        mine = ids_ref[1]
        g = None
        for kk in range(N_CHIP):
            term = jnp.where(mine == kk, own_ref[...], recv_ref[kk].astype(F32))
            g = term if g is None else g + term
        delta, m_new, v_new = _adamw_math(w_ref[...], g, m_ref[...], v_ref[...])
        g_ref[...] = g
        d_ref[...] = delta
        mo_ref[...] = m_new
        vo_ref[...] = v_new

    row = pl.BlockSpec((tr, c), lambda i, ids: (i, 0))
    grid_spec = pltpu.PrefetchScalarGridSpec(
        num_scalar_prefetch=1, grid=(r // tr,),
        in_specs=[row, pl.BlockSpec((N_CHIP, tr, c), lambda i, ids: (0, i, 0)), row, row, row],
        out_specs=[row, row, row, row])
    out = jax.ShapeDtypeStruct((r, c), F32)
    return pl.pallas_call(
        body, name=name, grid_spec=grid_spec, out_shape=[out, out, out, out],
        compiler_params=_params("parallel"),
    )(ids, own, recv, w, m, v)


def _sum_sources(a, name):
    n, r, c = a.shape

    def body(a_ref, o_ref):
        tot = a_ref[0]
        for kk in range(1, n):
            tot = tot + a_ref[kk]
        o_ref[...] = tot

    return pl.pallas_call(
        body, name=name, out_shape=jax.ShapeDtypeStruct((r, c), F32),
        in_specs=[pl.BlockSpec(memory_space=pltpu.VMEM)], out_specs=pl.BlockSpec(memory_space=pltpu.VMEM),
    )(a)


def _adamw_small(w, g, m, v, name):
    def body(w_ref, g_ref, m_ref, v_ref, d_ref, mo_ref, vo_ref):
        delta, m_new, v_new = _adamw_math(w_ref[...], g_ref[...], m_ref[...], v_ref[...])
        d_ref[...] = delta
        mo_ref[...] = m_new
        vo_ref[...] = v_new

    vm = pl.BlockSpec(memory_space=pltpu.VMEM)
    out = jax.ShapeDtypeStruct(w.shape, F32)
    return pl.pallas_call(body, name=name, out_shape=[out, out, out], in_specs=[vm] * 4, out_specs=[vm] * 3)(w, g, m, v)


def _pack(parts, width, row_mult):
    flat = jnp.concatenate([p.reshape(-1) for p in parts])
    rows = -(-flat.shape[0] // width)
    rows = -(-rows // row_mult) * row_mult
    return jnp.pad(flat, (0, rows * width - flat.shape[0])).reshape(rows, width)


def _unpack(flat, shapes):
    out, off = [], 0
    lead = flat.shape[:-1]
    for shp in shapes:
        n = 1
        for dd in shp:
            n *= dd
        out.append(flat[..., off:off + n].reshape(lead + tuple(shp)))
        off += n
    return out


def _rows2d(w):
    return w.reshape(w.shape[0] * w.shape[1], w.shape[2])


def _cols_to_dev(g):
    l, k, n = g.shape
    return g.reshape(l * k, N_DEV, n // N_DEV).transpose(1, 0, 2)


def _rows_to_dev(g):
    l, k, n = g.shape
    rs = k // N_DEV
    return g.reshape(l, N_DEV, rs, n).transpose(1, 0, 2, 3).reshape(N_DEV, l * rs, n)


def _dev_to_cols(a, l):
    _, lk, cs = a.shape
    return a.transpose(1, 0, 2).reshape(l, lk // l, N_DEV * cs)


def _dev_to_rows(a, l):
    _, lr, n = a.shape
    rs = lr // l
    return a.reshape(N_DEV, l, rs, n).transpose(1, 0, 2, 3).reshape(l, N_DEV * rs, n)


def kernel(x, attn_norm, attn_w_in, attn_f_bias, fox_q_gain, fox_k_gain, sb_q_gain, sb_k_gain, attn_w_out, conv_norm, conv_w_in, conv_kernel, conv_w_out, ffn_norm, ffn_w_up, ffn_conv, ffn_w_down, loss_target, m_attn_norm, m_attn_w_in, m_attn_f_bias, m_fox_q_gain, m_fox_k_gain, m_sb_q_gain, m_sb_k_gain, m_attn_w_out, m_conv_norm, m_conv_w_in, m_conv_kernel, m_conv_w_out, m_ffn_norm, m_ffn_w_up, m_ffn_conv, m_ffn_w_down, v_attn_norm, v_attn_w_in, v_attn_f_bias, v_fox_q_gain, v_fox_k_gain, v_sb_q_gain, v_sb_k_gain, v_attn_w_out, v_conv_norm, v_conv_w_in, v_conv_kernel, v_conv_w_out, v_ffn_norm, v_ffn_w_up, v_ffn_conv, v_ffn_w_down):
    s = x.shape[1]
    n_attn, n_conv, depth = attn_w_in.shape[0], conv_w_in.shape[0], ffn_w_up.shape[0]
    xi, yi, ci = _coords()
    me = 4 * xi + 2 * yi + ci
    ids = jnp.stack([ci, 2 * xi + yi]).astype(jnp.int32)

    big = [attn_w_in, attn_w_out, conv_w_in, conv_w_out, ffn_w_up, ffn_w_down]
    big_m = [m_attn_w_in, m_attn_w_out, m_conv_w_in, m_conv_w_out, m_ffn_w_up, m_ffn_w_down]
    big_v = [v_attn_w_in, v_attn_w_out, v_conv_w_in, v_conv_w_out, v_ffn_w_up, v_ffn_w_down]
    big_names = ["attn_w_in", "attn_w_out", "conv_w_in", "conv_w_out", "ffn_w_up", "ffn_w_down"]
    small_sh = [conv_norm, conv_kernel, ffn_conv]
    small_sh_shapes = [w.shape for w in small_sh]
    rep = [attn_norm, attn_f_bias, fox_q_gain, fox_k_gain, sb_q_gain, sb_k_gain, ffn_norm]
    rep_shapes = [w.shape for w in rep]

    small_pack = _pack(small_sh, LANES, 8)
    shards_bf16 = [_rows2d(w).astype(BF16) for w in big]
    early = _all_gather([attn_w_in[0].astype(BF16), small_pack], "gather_first")
    first_w_in = jnp.pad(_dev_to_cols(early[0], 1)[0], ((0, 0), (0, ATTN_IN_PAD - ATTN_IN)))
    cn, ckern, fconv = _unpack(early[1].reshape(N_DEV, -1), small_sh_shapes)
    conv_norm_f = cn.transpose(1, 0, 2).reshape(n_conv, D_MODEL)
    conv_kernel_f = ckern.transpose(1, 2, 0, 3).reshape(n_conv, 3, D_MODEL)
    ffn_conv_f = fconv.transpose(1, 2, 0, 3).reshape(depth, 3, 2 * D_FF)

    def pair_gain(fox_g, sb_g):
        f2 = jnp.concatenate([fox_g, fox_g])
        s2 = jnp.concatenate([sb_g, sb_g])
        return jnp.concatenate([jnp.tile(f2[None], (4, 1)), jnp.tile(s2[None], (4, 1))])[:, None, :]

    h = x[0]
    saved = []
    for layer in range(depth):
        i = layer // 2
        tag = "l%d" % layer
        rec = {"h_in": h}
        if layer % 2 == 0:
            xn = _rms_fwd(h, attn_norm[i], tag + "_attn_rms")
            proj = _matmul(xn, first_w_in if layer == 0 else a_w_in[i], tag + "_attn_in", tn=640)
            gq, gk = pair_gain(fox_q_gain[i], sb_q_gain[i]), pair_gain(fox_k_gain[i], sb_k_gain[i])
            qh, kh, vb, qt, kt, vt = _qkv_prep(proj, gq, gk, tag + "_qkv_prep")
            logit = proj[:, 3 * MIX:3 * MIX + H_FOX].T.reshape(H_FOX, s // LANES, LANES)
            cum = _fgate_fwd(logit, attn_f_bias[i], tag + "_fgate")
            frow = cum.reshape(H_FOX // 2, 2, s)
            fcol = frow.transpose(0, 2, 1)
            o_fox, lse = _fox_fwd(qt, kh, vt, frow, fcol, tag + "_fox_fwd")
            if layer == 0:
                o_sb, tot, *gathered = _sb_fwd(qt, kh, vt, tag + "_sb_fwd", gather=shards_bf16)
                a_w_in = _dev_to_cols(gathered[0], n_attn)
                a_w_in = jnp.pad(a_w_in, ((0, 0), (0, 0), (0, ATTN_IN_PAD - ATTN_IN)))
                a_w_out = _dev_to_rows(gathered[1], n_attn)
                c_w_in = _dev_to_cols(gathered[2], n_conv)
                c_w_out = _dev_to_rows(gathered[3], n_conv)
                f_w_up = _dev_to_cols(gathered[4], depth)
                f_w_down = _dev_to_rows(gathered[5], depth)
            else:
                o_sb, tot = _sb_fwd(qt, kh, vt, tag + "_sb_fwd")
            o = jnp.concatenate([o_fox, o_sb], axis=1)
            h = _matmul(o, a_w_out[i], tag + "_attn_out", add=h, tn=1024)
            rec.update(xn=xn, proj=proj, gq=gq, gk=gk, qh=qh, kh=kh, vb=vb, qt=qt, kt=kt, logit=logit, frow=frow,
                       fcol=fcol, o_fox=o_fox, lse=lse, tot=tot, o=o)
        else:
            xn = _rms_fwd(h, conv_norm_f[i], tag + "_conv_rms")
            proj = _matmul(xn, c_w_in[i], tag + "_conv_in", tn=1024)
            y = _sconv_fwd(proj, conv_kernel_f[i], tag + "_sconv_fwd")
            h = _matmul(y, c_w_out[i], tag + "_conv_out", add=h, tn=1024)
            rec.update(xn=xn, proj=proj, y=y)
        rec["h_mid"] = h
        xn2 = _rms_fwd(h, ffn_norm[layer], tag + "_ffn_rms")
        up = _matmul(xn2, f_w_up[layer], tag + "_ffn_up", tn=1408)
        act = _ffn_act_fwd(up, ffn_conv_f[layer], tag + "_ffn_act")
        h = _matmul(act, f_w_down[layer], tag + "_ffn_down", add=h, tn=1024, tk=2816)
        rec.update(xn2=xn2, up=up, act=act)
        saved.append(rec)

    loss_local, dh = _loss_head(h, loss_target[0], "loss_head")
    loss = lax.psum(loss_local, ("x", "y", "c"))

    g_attn_norm, g_attn_w_in, g_f_bias = [None] * n_attn, [None] * n_attn, [None] * n_attn
    g_fq, g_fk, g_sq, g_sk, g_attn_w_out = ([None] * n_attn for _ in range(5))
    g_conv_norm, g_conv_w_in, g_conv_kernel, g_conv_w_out = ([None] * n_conv for _ in range(4))
    g_ffn_norm, g_ffn_w_up, g_ffn_conv, g_ffn_w_down = ([None] * depth for _ in range(4))
    late_names = [nm for nm in big_names if not nm.startswith("conv")]

    def slabs_for_devices(layers, conv_layers):
        stacks = [(g_attn_w_in, layers, _cols_to_dev), (g_attn_w_out, layers, _rows_to_dev),
                  (g_conv_w_in, conv_layers, _cols_to_dev), (g_conv_w_out, conv_layers, _rows_to_dev),
                  (g_ffn_w_up, layers, _cols_to_dev), (g_ffn_w_down, layers, _rows_to_dev)]
        return [to_dev(jnp.stack(gl[sl]).astype(BF16)) for gl, sl, to_dev in stacks if len(gl[sl])]

    for layer in reversed(range(depth)):
        i = layer // 2
        tag = "l%d" % layer
        rec = saved[layer]
        da = _matmul(dh, f_w_down[layer].T, tag + "_ffn_down_dx", tn=1408)
        g_ffn_w_down[layer] = _matmul_tn(rec["act"], dh, tag + "_ffn_down_dw", tm=1408, tn=1024)
        dup, dwg, dwv = _ffn_act_bwd(rec["up"], ffn_conv_f[layer], da, tag + "_ffn_act_bwd")
        g_ffn_conv[layer] = jnp.concatenate([dwg, dwv], axis=1)
        g_ffn_w_up[layer] = _matmul_tn(rec["xn2"], dup, tag + "_ffn_up_dw", tn=1408)
        dxn = _matmul(dup, f_w_up[layer].T, tag + "_ffn_up_dx", tn=1024, tk=2816)
        dh, g_ffn_norm[layer] = _rms_bwd(rec["h_mid"], dxn, ffn_norm[layer], dh, tag + "_ffn_rms_bwd")
        if layer % 2 == 0:
            do = _matmul(dh, a_w_out[i].T, tag + "_attn_out_dx", tn=1024)
            g_attn_w_out[i] = _matmul_tn(rec["o"], dh, tag + "_attn_out_dw", tn=1024)
            dq_f, dk_f, dv_f, dfk, dfq = _fox_bwd(rec["qh"], rec["qt"], rec["kh"], rec["kt"], rec["vb"], rec["frow"],
                                                  rec["fcol"], rec["lse"], rec["o_fox"], do, tag + "_fox_bwd")
            dq_s, dk_s, dv_s, *from_chips = _sb_bwd(
                rec["qh"], rec["qt"], rec["kh"], rec["kt"], rec["vb"], rec["tot"], do, tag + "_sb_bwd",
                exchange=[pr[0] for pr in early_pairs] if layer == 0 else ())
            if layer == 0:
                early_from_chips = from_chips
            dq, dk, dv, dgq, dgk = _qkv_prep_bwd(rec["proj"], rec["gq"], rec["gk"], (dq_f, dq_s), (dk_f, dk_s),
                                                 (dv_f, dv_s), tag + "_qkv_prep_bwd")
            dcum = (dfq + dfk[:, :, 0:2].transpose(0, 2, 1)).reshape(H_FOX, s // LANES, LANES)
            dlogit, dbias = _fgate_bwd(rec["logit"], attn_f_bias[i], dcum, tag + "_fgate_bwd")
            g_f_bias[i] = dbias[:, 0]
            dgate = jnp.pad(dlogit.reshape(H_FOX, s).T, ((0, 0), (0, LANES - H_FOX))).astype(BF16)
            dproj = jnp.concatenate([dq, dk, dv, dgate], axis=1)

            def fold(dg):
                per_head = dg.reshape(16, HEAD_DIM)
                return jnp.sum(per_head[:8], axis=0), jnp.sum(per_head[8:], axis=0)

            g_fq[i], g_sq[i] = fold(dgq)
            g_fk[i], g_sk[i] = fold(dgk)
            g_attn_w_in[i] = _matmul_tn(rec["xn"], dproj, tag + "_attn_in_dw", tn=640)[:, :ATTN_IN]
            dxn = _matmul(dproj, a_w_in[i].T, tag + "_attn_in_dx", tn=1024, tk=3200)
            dh, g_attn_norm[i] = _rms_bwd(rec["h_in"], dxn, attn_norm[i], dh, tag + "_attn_rms_bwd")
        else:
            dy = _matmul(dh, c_w_out[i].T, tag + "_conv_out_dx", tn=1024)
            g_conv_w_out[i] = _matmul_tn(rec["y"], dh, tag + "_conv_out_dw", tn=1024)
            dproj, g_conv_kernel[i] = _sconv_bwd(rec["proj"], conv_kernel_f[i], dy, tag + "_sconv_bwd")
            g_conv_w_in[i] = _matmul_tn(rec["xn"], dproj, tag + "_conv_in_dw", tn=1024)
            dxn = _matmul(dproj, c_w_in[i].T, tag + "_conv_in_dx", tn=1024, tk=3072)
            dh, g_conv_norm[i] = _rms_bwd(rec["h_in"], dxn, conv_norm_f[i], dh, tag + "_conv_rms_bwd")
        if layer == 1:
            early_g = slabs_for_devices(slice(1, None), slice(0, None))
            early_pairs = [_pair_add(g, r, ids, "reduce_pair_add_" + nm)
                           for g, r, nm in zip(early_g, _sibling_exchange(early_g, "reduce_sibling"), big_names)]
    grad_x = dh[None]

    late_g = slabs_for_devices(slice(0, 1), slice(0, 0))
    late_pairs = [_pair_add(g, r, ids, "reduce_pair_add_late_" + nm)
                  for g, r, nm in zip(late_g, _sibling_exchange(late_g, "reduce_sibling_late"), late_names)]
    late_from_chips = _chip_exchange([pr[0] for pr in late_pairs], "reduce_chips_late")

    def update(piece_pairs, piece_recv, piece_names, layers, conv_layers):
        outs = {}
        for (own, recv, nm) in zip([pr[1] for pr in piece_pairs], piece_recv, piece_names):
            sl = conv_layers if nm.startswith("conv") else layers
            which = big_names.index(nm)
            outs[nm] = _adamw_reduce(own, recv, ids, _rows2d(big[which][sl]), _rows2d(big_m[which][sl]),
                                     _rows2d(big_v[which][sl]), "adamw_%s_%d" % (nm, sl.start))
        return outs

    late_out = update(late_pairs, late_from_chips, late_names, slice(0, 1), slice(0, 0))
    early_out = update(early_pairs, early_from_chips, big_names, slice(1, None), slice(0, None))
    grads_big, delta_big, newm_big, newv_big = [], [], [], []
    for which, nm in enumerate(big_names):
        shp = big[which].shape
        for k, dest in enumerate((grads_big, delta_big, newm_big, newv_big)):
            parts = ([late_out[nm][k]] if nm in late_out else []) + [early_out[nm][k]]
            dest.append(jnp.concatenate(parts, axis=0).reshape(shp))

    rep_g = [jnp.stack(g_attn_norm), jnp.stack(g_f_bias), jnp.stack(g_fq), jnp.stack(g_fk), jnp.stack(g_sq),
             jnp.stack(g_sk), jnp.stack(g_ffn_norm)]
    sh_g = [jnp.stack(g_conv_norm).reshape(n_conv, N_DEV, -1).transpose(1, 0, 2),
            jnp.stack(g_conv_kernel).reshape(n_conv, 3, N_DEV, -1).transpose(2, 0, 1, 3),
            jnp.stack(g_ffn_conv).reshape(depth, 3, N_DEV, -1).transpose(2, 0, 1, 3)]
    n_rep = sum(int(a.size) for a in rep)
    n_sh = sum(int(a.size) for a in small_sh)
    partial = _pack(rep_g + [jnp.concatenate([a.reshape(N_DEV, -1) for a in sh_g], axis=1)], LANES, 8)
    total = _sum_sources(_all_gather([partial], "gather_small_grads")[0], "sum_small_grads").reshape(-1)
    rep_tot = total[:n_rep]
    sh_tot = lax.dynamic_slice_in_dim(total[n_rep:n_rep + N_DEV * n_sh].reshape(N_DEV, n_sh), me, 1, axis=0)[0]
    g_small = _pack([rep_tot, sh_tot], LANES, 8)

    def small_pack_of(rep_list, sh_list):
        return _pack(rep_list + sh_list, LANES, 8)

    d_small, m_small, v_small = _adamw_small(
        small_pack_of(rep, small_sh), g_small,
        small_pack_of([m_attn_norm, m_attn_f_bias, m_fox_q_gain, m_fox_k_gain, m_sb_q_gain, m_sb_k_gain, m_ffn_norm],
                      [m_conv_norm, m_conv_kernel, m_ffn_conv]),
        small_pack_of([v_attn_norm, v_attn_f_bias, v_fox_q_gain, v_fox_k_gain, v_sb_q_gain, v_sb_k_gain, v_ffn_norm],
                      [v_conv_norm, v_conv_kernel, v_ffn_conv]),
        "adamw_small")
    small_shapes = rep_shapes + small_sh_shapes

    def split_small(a):
        return _unpack(a.reshape(-1), small_shapes)

    def ordered(big_list, small_list):
        an, fb, fq, fk, sq, sk, fn, cno, cke, fco = small_list
        awi, awo, cwi, cwo, fwu, fwd = big_list
        return [an, awi, fb, fq, fk, sq, sk, awo, cno, cwi, cke, cwo, fn, fwu, fco, fwd]

    grads = ordered(grads_big, split_small(g_small))
    deltas = ordered(delta_big, split_small(d_small))
    new_m = ordered(newm_big, split_small(m_small))
    new_v = ordered(newv_big, split_small(v_small))
    return (loss, grad_x, *grads, *deltas, *new_m, *new_v)
```

```python
import jax
import jax.numpy as jnp
from jax import lax
from jax.experimental import pallas as pl
from jax.experimental.pallas import tpu as pltpu

F32 = jnp.float32
BF16 = jnp.bfloat16

D_MODEL = 1024
HEAD_DIM = 64
H_FOX = 8
MIX = 1024
ATTN_IN = 3 * MIX + H_FOX
ATTN_IN_PAD = 3 * MIX + 128
D_FF = 2816
EPS = 1e-6
NEG = -1e30
LANES = 128
N_DEV = 8
N_CHIP = 4

ADAM_LR = 0.001
ADAM_B1 = 0.9
ADAM_B2 = 0.999
ADAM_EPS = 1e-08
ADAM_WD = 0.01
ADAM_STEP = 10

VMEM_LIMIT = 56 * 1024 * 1024
MESH = pl.DeviceIdType.MESH
ANY = pl.BlockSpec(memory_space=pl.ANY)


def _params(*sem):
    return pltpu.CompilerParams(dimension_semantics=sem, vmem_limit_bytes=VMEM_LIMIT)


def _tile(n, target, mult=LANES):
    best = None
    for t in range(mult, min(n, target) + 1, mult):
        if n % t == 0:
            best = t
    return best if best is not None else n


def _dot(a, b):
    return jnp.dot(a, b, preferred_element_type=F32)


def _dot_tn(a, b):
    return lax.dot_general(a, b, (((0,), (0,)), ((), ())), preferred_element_type=F32)


def _split_dot(x, m, passes):
    acc = None
    rem = x
    for _ in range(passes):
        part = rem.astype(BF16)
        term = _dot(part, m)
        acc = term if acc is None else acc + term
        rem = rem - part.astype(F32)
    return acc


def _split_dot_left(m, x, passes):
    acc = None
    rem = x
    for _ in range(passes):
        part = rem.astype(BF16)
        term = _dot(m, part)
        acc = term if acc is None else acc + term
        rem = rem - part.astype(F32)
    return acc


def _matmul(a, b, name, add=None, out_dtype=F32, tm=1024, tn=512, tk=1024):
    split = a.shape[0] if a.ndim == 3 else 1
    m, kh = a.shape[-2:]
    k = split * kh
    n = b.shape[1]
    tm, tn, tk = _tile(m, tm, 8), _tile(n, tn), _tile(kh, tk)
    nk = k // tk
    per_slab = kh // tk
    has_add = add is not None

    def body(*refs):
        a_ref, b_ref = refs[0], refs[1]
        add_ref = refs[2] if has_add else None
        o_ref = refs[2 + has_add]

        def finish(acc):
            if has_add:
                acc = acc + add_ref[...]
            o_ref[...] = acc.astype(out_dtype)

        p = _dot(a_ref[...].astype(BF16), b_ref[...].astype(BF16))
        if nk == 1:
            finish(p)
        else:
            acc_ref = refs[-1]
            kk = pl.program_id(2)

            @pl.when(kk == 0)
            def _():
                acc_ref[...] = p

            @pl.when(kk > 0)
            def _():
                acc_ref[...] += p

            @pl.when(kk == nk - 1)
            def _():
                finish(acc_ref[...])

    if split == 1:
        a_spec = pl.BlockSpec((tm, tk), lambda i, j, kk: (i, kk))
    else:
        a_spec = pl.BlockSpec((None, tm, tk), lambda i, j, kk: (kk // per_slab, i, kk % per_slab))
    in_specs = [a_spec, pl.BlockSpec((tk, tn), lambda i, j, kk: (kk, j))]
    args = [a, b]
    if has_add:
        in_specs.append(pl.BlockSpec((tm, tn), lambda i, j, kk: (i, j)))
        args.append(add)
    return pl.pallas_call(
        body, name=name, grid=(m // tm, n // tn, nk), in_specs=in_specs,
        out_specs=pl.BlockSpec((tm, tn), lambda i, j, kk: (i, j)),
        out_shape=jax.ShapeDtypeStruct((m, n), out_dtype),
        scratch_shapes=[pltpu.VMEM((tm, tn), F32)] if nk > 1 else [],
        compiler_params=_params("parallel", "parallel", "arbitrary"),
    )(*args)


def _matmul_tn(a, b, name, tm=1024, tn=512, ts=2048):
    s, m = a.shape
    split = b.shape[0] if b.ndim == 3 else 1
    nh = b.shape[-1]
    n = split * nh
    tm, tn, ts = _tile(m, tm), _tile(nh, tn), _tile(s, ts, 8)
    per_slab = nh // tn
    if split == 1:
        b_spec = pl.BlockSpec((ts, tn), lambda i, j, kk: (kk, j))
    else:
        b_spec = pl.BlockSpec((None, ts, tn), lambda i, j, kk: (j // per_slab, kk, j % per_slab))

    def body(a_ref, b_ref, o_ref):
        kk = pl.program_id(2)
        p = _dot_tn(a_ref[...].astype(BF16), b_ref[...].astype(BF16))

        @pl.when(kk == 0)
        def _():
            o_ref[...] = p

        @pl.when(kk > 0)
        def _():
            o_ref[...] += p

    return pl.pallas_call(
        body, name=name, grid=(m // tm, n // tn, s // ts),
        in_specs=[pl.BlockSpec((ts, tm), lambda i, j, kk: (kk, i)), b_spec],
        out_specs=pl.BlockSpec((tm, tn), lambda i, j, kk: (i, j)),
        out_shape=jax.ShapeDtypeStruct((m, n), F32),
        compiler_params=_params("parallel", "parallel", "arbitrary"),
    )(a, b)


def _rms_fwd(h, g, name, ts=512):
    s, d = h.shape
    ts = _tile(s, ts, 8)

    def body(h_ref, g_ref, o_ref):
        x = h_ref[...]
        r = lax.rsqrt(jnp.mean(x * x, axis=-1, keepdims=True) + EPS)
        o_ref[...] = (x * r * g_ref[...]).astype(BF16)

    return pl.pallas_call(
        body, name=name, grid=(s // ts,),
        in_specs=[pl.BlockSpec((ts, d), lambda i: (i, 0)), pl.BlockSpec((1, d), lambda i: (0, 0))],
        out_specs=pl.BlockSpec((ts, d), lambda i: (i, 0)),
        out_shape=jax.ShapeDtypeStruct((s, d), BF16),
        compiler_params=_params("parallel"),
    )(h, g.reshape(1, d))


def _rms_bwd(h, dxn, g, dh_in, name, ts=512):
    s, d = h.shape
    ts = _tile(s, ts, 8)

    def body(h_ref, dxn_ref, g_ref, dhin_ref, dh_ref, dg_ref):
        i = pl.program_id(0)
        x = h_ref[...]
        r = lax.rsqrt(jnp.mean(x * x, axis=-1, keepdims=True) + EPS)
        xh = x * r
        dxn_v = dxn_ref[...]

        @pl.when(i == 0)
        def _():
            dg_ref[...] = jnp.zeros_like(dg_ref)

        dg_ref[0:1, :] += jnp.sum(dxn_v * xh, axis=0, keepdims=True)
        dxh = dxn_v * g_ref[...]
        dx = r * (dxh - xh * jnp.mean(dxh * xh, axis=-1, keepdims=True))
        dh_ref[...] = dhin_ref[...] + dx

    row = pl.BlockSpec((ts, d), lambda i: (i, 0))
    dh, dg = pl.pallas_call(
        body, name=name, grid=(s // ts,),
        in_specs=[row, row, pl.BlockSpec((1, d), lambda i: (0, 0)), row],
        out_specs=[row, pl.BlockSpec((8, d), lambda i: (0, 0))],
        out_shape=[jax.ShapeDtypeStruct((s, d), F32), jax.ShapeDtypeStruct((8, d), F32)],
        compiler_params=_params("arbitrary"),
    )(h, dxn, g.reshape(1, d), dh_in)
    return dh, dg[0]


def _shift_down(x, prev):
    rows = lax.broadcasted_iota(jnp.int32, (8, x.shape[1]), 0)
    p1, p2 = prev[7:8, :], prev[6:7, :]
    r1, r2 = pltpu.roll(x, 1, 0), pltpu.roll(x, 2, 0)
    top1 = jnp.where(rows == 0, p1, r1[0:8, :])
    top2 = jnp.where(rows == 0, p2, jnp.where(rows == 1, p1, r2[0:8, :]))
    if x.shape[0] == 8:
        return top1, top2
    return jnp.concatenate([top1, r1[8:, :]], axis=0), jnp.concatenate([top2, r2[8:, :]], axis=0)


def _shift_up(x, nxt):
    n = x.shape[0]
    rows = lax.broadcasted_iota(jnp.int32, (8, x.shape[1]), 0)
    n0, n1 = nxt[0:1, :], nxt[1:2, :]
    r1, r2 = pltpu.roll(x, n - 1, 0), pltpu.roll(x, n - 2, 0)
    end1 = jnp.where(rows == 7, n0, r1[n - 8:, :])
    end2 = jnp.where(rows == 7, n1, jnp.where(rows == 6, n0, r2[n - 8:, :]))
    return jnp.concatenate([r1[:n - 8, :], end1], axis=0), jnp.concatenate([r2[:n - 8, :], end2], axis=0)


def _conv(x, x1, x2, w):
    return w[2:3, :] * x + w[1:2, :] * x1 + w[0:1, :] * x2


def _halo_specs(ts, tc, col, n_time_blocks):
    r8 = ts // 8
    main = pl.BlockSpec((ts, tc), lambda j, i: (i, j + col))
    prev = pl.BlockSpec((8, tc), lambda j, i: (jnp.maximum(i * r8 - 1, 0), j + col))
    nxt = pl.BlockSpec((8, tc), lambda j, i: (jnp.minimum((i + 1) * r8, n_time_blocks * r8 - 1), j + col))
    return main, prev, nxt


def _silu_parts(g):
    sig = 1.0 / (1.0 + jnp.exp(-g))
    return sig, g * sig


def _ffn_act_fwd(up, cw, name, ts=256, tc=1408):
    s = up.shape[0]
    ts, tc = _tile(s, ts, 8), _tile(D_FF, tc)
    nc, nt = D_FF // tc, s // ts

    def body(g_ref, gp_ref, v_ref, vp_ref, wg_ref, wv_ref, o_ref):
        first = pl.program_id(1) == 0

        def conv(x_ref, p_ref, w_ref):
            x = x_ref[...]
            prev = jnp.where(first, 0.0, p_ref[...])
            x1, x2 = _shift_down(x, prev)
            return _conv(x, x1, x2, w_ref[...])

        ug = conv(g_ref, gp_ref, wg_ref)
        uv = conv(v_ref, vp_ref, wv_ref)
        _, silu = _silu_parts(ug)
        o_ref[...] = (silu * uv).astype(BF16)

    g_main, g_prev, _ = _halo_specs(ts, tc, 0, nt)
    v_main, v_prev, _ = _halo_specs(ts, tc, nc, nt)
    return pl.pallas_call(
        body, name=name, grid=(nc, nt),
        in_specs=[g_main, g_prev, v_main, v_prev,
                  pl.BlockSpec((3, tc), lambda j, i: (0, j)), pl.BlockSpec((3, tc), lambda j, i: (0, j + nc))],
        out_specs=pl.BlockSpec((ts, tc), lambda j, i: (i, j)),
        out_shape=jax.ShapeDtypeStruct((s, D_FF), BF16),
        compiler_params=_params("parallel", "parallel"),
    )(up, up, up, up, cw, cw)


def _ffn_act_bwd(up, cw, da, name, ts=256, tc=1408):
    s = up.shape[0]
    ts, tc = _tile(s, ts, 8), _tile(D_FF, tc)
    nc, nt = D_FF // tc, s // ts

    def body(g_ref, gp_ref, gn_ref, v_ref, vp_ref, vn_ref, da_ref, dan_ref, wg_ref, wv_ref,
             d_ref, dwg_ref, dwv_ref):
        i = pl.program_id(1)
        first, last = i == 0, i == nt - 1
        wg, wv = wg_ref[...], wv_ref[...]
        g, v = g_ref[...], v_ref[...]
        g1, g2 = _shift_down(g, jnp.where(first, 0.0, gp_ref[...]))
        v1, v2 = _shift_down(v, jnp.where(first, 0.0, vp_ref[...]))

        def d_u(ug, uv, da_v):
            sig, silu = _silu_parts(ug)
            return da_v * uv * (sig * (1.0 + ug * (1.0 - sig))), da_v * silu

        dug, duv = d_u(_conv(g, g1, g2, wg), _conv(v, v1, v2, wv), da_ref[...])
        gn, vn = gn_ref[...], vn_ref[...]
        gn1, gn2 = _shift_down(gn, g[ts - 8:, :])
        vn1, vn2 = _shift_down(vn, v[ts - 8:, :])
        dugn, duvn = d_u(_conv(gn, gn1, gn2, wg), _conv(vn, vn1, vn2, wv), dan_ref[...])
        dugn = jnp.where(last, 0.0, dugn)
        duvn = jnp.where(last, 0.0, duvn)

        def finish(du, dun, x, x1, x2, w, dx_ref, dw_ref):
            d1, d2 = _shift_up(du, dun)
            dx_ref[...] = (w[2:3, :] * du + w[1:2, :] * d1 + w[0:1, :] * d2).astype(BF16)

            @pl.when(first)
            def _():
                dw_ref[...] = jnp.zeros_like(dw_ref)

            dw_ref[0:1, :] += jnp.sum(du * x2, axis=0, keepdims=True)
            dw_ref[1:2, :] += jnp.sum(du * x1, axis=0, keepdims=True)
            dw_ref[2:3, :] += jnp.sum(du * x, axis=0, keepdims=True)

        finish(dug, dugn, g, g1, g2, wg, d_ref.at[0], dwg_ref)
        finish(duv, duvn, v, v1, v2, wv, d_ref.at[1], dwv_ref)

    g_specs = _halo_specs(ts, tc, 0, nt)
    v_specs = _halo_specs(ts, tc, nc, nt)
    da_main, _, da_next = _halo_specs(ts, tc, 0, nt)
    taps = pl.BlockSpec((8, tc), lambda j, i: (0, j))
    halves = pl.BlockSpec((2, ts, tc), lambda j, i: (0, i, j))
    d, dwg, dwv = pl.pallas_call(
        body, name=name, grid=(nc, nt),
        in_specs=[*g_specs, *v_specs, da_main, da_next,
                  pl.BlockSpec((3, tc), lambda j, i: (0, j)), pl.BlockSpec((3, tc), lambda j, i: (0, j + nc))],
        out_specs=[halves, taps, taps],
        out_shape=[jax.ShapeDtypeStruct((2, s, D_FF), BF16),
                   jax.ShapeDtypeStruct((8, D_FF), F32), jax.ShapeDtypeStruct((8, D_FF), F32)],
        compiler_params=_params("parallel", "arbitrary"),
    )(up, up, up, up, up, up, da, da, cw, cw)
    return d, dwg[:3], dwv[:3]


def _sconv_fwd(proj, ck, name, ts=256, tc=512):
    s = proj.shape[0]
    w = D_MODEL
    ts, tc = _tile(s, ts, 8), _tile(w, tc)
    nc, nt = w // tc, s // ts

    def body(b_ref, c_ref, cp_ref, u_ref, up_ref, w_ref, o_ref):
        first = pl.program_id(1) == 0
        cu = c_ref[...] * u_ref[...]
        cup = jnp.where(first, 0.0, cp_ref[...] * up_ref[...])
        x1, x2 = _shift_down(cu, cup)
        o_ref[...] = (b_ref[...] * _conv(cu, x1, x2, w_ref[...])).astype(BF16)

    b_main, _, _ = _halo_specs(ts, tc, 0, nt)
    c_main, c_prev, _ = _halo_specs(ts, tc, nc, nt)
    u_main, u_prev, _ = _halo_specs(ts, tc, 2 * nc, nt)
    return pl.pallas_call(
        body, name=name, grid=(nc, nt),
        in_specs=[b_main, c_main, c_prev, u_main, u_prev, pl.BlockSpec((3, tc), lambda j, i: (0, j))],
        out_specs=pl.BlockSpec((ts, tc), lambda j, i: (i, j)),
        out_shape=jax.ShapeDtypeStruct((s, w), BF16),
        compiler_params=_params("parallel", "parallel"),
    )(proj, proj, proj, proj, proj, ck)


def _sconv_bwd(proj, ck, dy, name, ts=256, tc=512):
    s = proj.shape[0]
    w = D_MODEL
    ts, tc = _tile(s, ts, 8), _tile(w, tc)
    nc, nt = w // tc, s // ts

    def body(b_ref, bn_ref, c_ref, cp_ref, u_ref, up_ref, dy_ref, dyn_ref, w_ref,
             d_ref, dw_ref):
        i = pl.program_id(1)
        first, last = i == 0, i == nt - 1
        wv = w_ref[...]
        b, c, u, dy_v = b_ref[...], c_ref[...], u_ref[...], dy_ref[...]
        cu = c * u
        cup = jnp.where(first, 0.0, cp_ref[...] * up_ref[...])
        x1, x2 = _shift_down(cu, cup)
        d_ref[0] = (dy_v * _conv(cu, x1, x2, wv)).astype(BF16)
        dcv = dy_v * b
        dcvn = jnp.where(last, 0.0, dyn_ref[...] * bn_ref[...])
        d1, d2 = _shift_up(dcv, dcvn)
        dcu = wv[2:3, :] * dcv + wv[1:2, :] * d1 + wv[0:1, :] * d2
        d_ref[1] = (dcu * u).astype(BF16)
        d_ref[2] = (dcu * c).astype(BF16)

        @pl.when(first)
        def _():
            dw_ref[...] = jnp.zeros_like(dw_ref)

        dw_ref[0:1, :] += jnp.sum(dcv * x2, axis=0, keepdims=True)
        dw_ref[1:2, :] += jnp.sum(dcv * x1, axis=0, keepdims=True)
        dw_ref[2:3, :] += jnp.sum(dcv * cu, axis=0, keepdims=True)

    b_main, _, b_next = _halo_specs(ts, tc, 0, nt)
    c_main, c_prev, _ = _halo_specs(ts, tc, nc, nt)
    u_main, u_prev, _ = _halo_specs(ts, tc, 2 * nc, nt)
    dy_main, _, dy_next = _halo_specs(ts, tc, 0, nt)
    d, dw = pl.pallas_call(
        body, name=name, grid=(nc, nt),
        in_specs=[b_main, b_next, c_main, c_prev, u_main, u_prev, dy_main, dy_next,
                  pl.BlockSpec((3, tc), lambda j, i: (0, j))],
        out_specs=[pl.BlockSpec((3, ts, tc), lambda j, i: (0, i, j)), pl.BlockSpec((8, tc), lambda j, i: (0, j))],
        out_shape=[jax.ShapeDtypeStruct((3, s, w), BF16), jax.ShapeDtypeStruct((8, w), F32)],
        compiler_params=_params("parallel", "arbitrary"),
    )(proj, proj, proj, proj, proj, proj, dy, dy, ck)
    return d, dw[:3]


def _low_lanes(shape):
    return lax.broadcasted_iota(jnp.int32, shape, 1) < HEAD_DIM


def _top_rows(shape):
    return lax.broadcasted_iota(jnp.int32, shape, 0) < HEAD_DIM


def _norm_pair(x):
    r = lax.rsqrt(_mean_pair(x * x) + EPS)
    return x * r, r


def _mean_pair(x):
    same_head = _tri(LANES, lambda a, b: a // HEAD_DIM == b // HEAD_DIM)
    return _split_dot(x, same_head, 3) * (1.0 / HEAD_DIM)


def _qkv_prep(proj, gq, gk, name, ts=512):
    s = proj.shape[0]
    ts = _tile(s, ts)
    npair = MIX // LANES
    scale = HEAD_DIM ** -0.5

    def body(q_ref, k_ref, v_ref, gq_ref, gk_ref, qo_ref, ko_ref, vo_ref, qt_ref, kt_ref, vt_ref):
        qn, _ = _norm_pair(q_ref[...])
        kn, _ = _norm_pair(k_ref[...])
        q = qn * gq_ref[0] * scale
        k = kn * gk_ref[0]
        v = v_ref[...]
        qo_ref[...] = q.astype(BF16)
        ko_ref[...] = k.astype(BF16)
        vo_ref[...] = v.astype(BF16)
        qt_ref[...] = q.T.astype(BF16)
        kt_ref[...] = k.T.astype(BF16)
        vt_ref[...] = v.T.astype(BF16)

    gain = pl.BlockSpec((1, 1, LANES), lambda i, p: (p, 0, 0))
    tile = pl.BlockSpec((ts, LANES), lambda i, p: (i, p))
    tile_t = pl.BlockSpec((LANES, ts), lambda i, p: (p, i))
    out = jax.ShapeDtypeStruct((s, MIX), BF16)
    out_t = jax.ShapeDtypeStruct((MIX, s), BF16)
    return pl.pallas_call(
        body, name=name, grid=(s // ts, npair),
        in_specs=[tile, pl.BlockSpec((ts, LANES), lambda i, p: (i, p + npair)),
                  pl.BlockSpec((ts, LANES), lambda i, p: (i, p + 2 * npair)), gain, gain],
        out_specs=[tile, tile, tile, tile_t, tile_t, tile_t], out_shape=[out, out, out, out_t, out_t, out_t],
        compiler_params=_params("parallel", "parallel"),
    )(proj, proj, proj, gq, gk)


def _qkv_prep_bwd(proj, gq, gk, dqs, dks, dvs, name, ts=512):
    s = proj.shape[0]
    ts = _tile(s, ts, 8)
    npair = MIX // LANES
    half = npair // 2
    scale = HEAD_DIM ** -0.5

    def body(q_ref, k_ref, gq_ref, gk_ref, dqf_ref, dqs_ref, dkf_ref, dks_ref, dvf_ref, dvs_ref,
             dq_ref, dk_ref, dv_ref, dgq_ref, dgk_ref):
        p, i = pl.program_id(0), pl.program_id(1)
        fox = p < half

        def one(x_ref, g_ref, df_ref, ds_ref, dx_ref, dg_ref, mult):
            dn = jnp.where(fox, df_ref[...], ds_ref[...]) * mult
            xh, r = _norm_pair(x_ref[...])

            @pl.when(i == 0)
            def _():
                dg_ref[...] = jnp.zeros_like(dg_ref)

            dg_ref[0, 0:1, :] += jnp.sum(dn * xh, axis=0, keepdims=True)
            dxh = dn * g_ref[0]
            dx_ref[...] = (r * (dxh - xh * _mean_pair(dxh * xh))).astype(BF16)

        one(q_ref, gq_ref, dqf_ref, dqs_ref, dq_ref, dgq_ref, scale)
        one(k_ref, gk_ref, dkf_ref, dks_ref, dk_ref, dgk_ref, 1.0)
        dv_ref[...] = jnp.where(fox, dvf_ref[...], dvs_ref[...]).astype(BF16)

    gain = pl.BlockSpec((1, 1, LANES), lambda p, i: (p, 0, 0))
    tile = pl.BlockSpec((ts, LANES), lambda p, i: (i, p))
    fpart = pl.BlockSpec((ts, LANES), lambda p, i: (i, jnp.minimum(p, half - 1)))
    spart = pl.BlockSpec((ts, LANES), lambda p, i: (i, jnp.maximum(p - half, 0)))
    dgain = pl.BlockSpec((1, 8, LANES), lambda p, i: (p, 0, 0))
    out = jax.ShapeDtypeStruct((s, MIX), BF16)
    gshape = jax.ShapeDtypeStruct((npair, 8, LANES), F32)
    dq, dk, dv, dgq, dgk = pl.pallas_call(
        body, name=name, grid=(npair, s // ts),
        in_specs=[tile, pl.BlockSpec((ts, LANES), lambda p, i: (i, p + npair)), gain, gain,
                  fpart, spart, fpart, spart, fpart, spart],
        out_specs=[tile, tile, tile, dgain, dgain], out_shape=[out, out, out, gshape, gshape],
        compiler_params=_params("parallel", "arbitrary"),
    )(proj, proj, gq, gk, dqs[0], dqs[1], dks[0], dks[1], dvs[0], dvs[1])
    return dq, dk, dv, dgq[:, 0, :], dgk[:, 0, :]


def _tri(n, rel):
    a = lax.broadcasted_iota(jnp.int32, (n, n), 0)
    b = lax.broadcasted_iota(jnp.int32, (n, n), 1)
    return rel(a, b).astype(BF16)


def _fgate_fwd(logit, bias, name):
    nh, r, _ = logit.shape

    def body(x_ref, b_ref, o_ref):
        within = _tri(LANES, lambda a, b: a <= b)
        before = _tri(r, lambda a, b: b < a)
        for hh in range(nh):
            x = x_ref[hh] + b_ref[hh]
            lf = jnp.minimum(x, 0.0) - jnp.log1p(jnp.exp(-jnp.abs(x)))
            c = _split_dot(lf, within, 3)
            tot = jnp.broadcast_to(c[:, LANES - 1:LANES], (r, LANES))
            o_ref[hh] = c + _split_dot_left(before, tot, 3)

    return pl.pallas_call(
        body, name=name, out_shape=jax.ShapeDtypeStruct((nh, r, LANES), F32),
        in_specs=[pl.BlockSpec(memory_space=pltpu.VMEM), pl.BlockSpec(memory_space=pltpu.SMEM)],
        out_specs=pl.BlockSpec(memory_space=pltpu.VMEM),
    )(logit, bias)


def _fgate_bwd(logit, bias, dcum, name):
    nh, r, _ = logit.shape

    def body(x_ref, b_ref, d_ref, dx_ref, db_ref):
        within = _tri(LANES, lambda a, b: a >= b)
        after = _tri(r, lambda a, b: b > a)
        for hh in range(nh):
            x = x_ref[hh] + b_ref[hh]
            d = d_ref[hh]
            c = _split_dot(d, within, 3)
            tot = jnp.broadcast_to(c[:, 0:1], (r, LANES))
            dlf = c + _split_dot_left(after, tot, 3)
            dx = dlf * (1.0 / (1.0 + jnp.exp(x)))
            dx_ref[hh] = dx
            db_ref[hh:hh + 1, :] = jnp.broadcast_to(jnp.sum(dx, keepdims=True).reshape(1, 1), (1, LANES))

    return pl.pallas_call(
        body, name=name,
        out_shape=[jax.ShapeDtypeStruct((nh, r, LANES), F32), jax.ShapeDtypeStruct((nh, LANES), F32)],
        in_specs=[pl.BlockSpec(memory_space=pltpu.VMEM), pl.BlockSpec(memory_space=pltpu.SMEM),
                  pl.BlockSpec(memory_space=pltpu.VMEM)],
        out_specs=[pl.BlockSpec(memory_space=pltpu.VMEM), pl.BlockSpec(memory_space=pltpu.VMEM)],
    )(logit, bias, dcum)


def _pair_masks(x):
    lo = _low_lanes(x.shape)
    zero = jnp.zeros_like(x)
    return jnp.where(lo, x, zero), jnp.where(lo, zero, x)


def _pair_masks_t(x):
    top = _top_rows(x.shape)
    zero = jnp.zeros_like(x)
    return jnp.where(top, x, zero), jnp.where(top, zero, x)


def _stack_heads(x):
    return jnp.concatenate(_pair_masks(x), axis=0)


def _stack_heads_t(x):
    return jnp.concatenate(_pair_masks_t(x), axis=1)


def _pair_colsum_t(x):
    top = _top_rows(x.shape)
    return (jnp.sum(jnp.where(top, x, 0.0), axis=0, keepdims=True),
            jnp.sum(jnp.where(top, 0.0, x), axis=0, keepdims=True))


def _key_query_iotas(t):
    return lax.broadcasted_iota(jnp.int32, (t, t), 0), lax.broadcasted_iota(jnp.int32, (t, t), 1)


def _walk_blocks(i, step, descending, group=2):
    full = i // group
    left = i - full * group

    def run(first, count):
        sign = -1 if descending else 1
        step([(first + sign * n, False) for n in range(count)])

    def leftovers():
        start = (left - 1) if descending else full * group
        sign = -1 if descending else 1
        if group == 4:
            @pl.when(left >= 2)
            def _():
                run(start, 2)

            @pl.when(left % 2 == 1)
            def _():
                run(0 if descending else i - 1, 1)
        else:
            @pl.when(left == 1)
            def _():
                run(start, 1)

    def loop(g, carry):
        run((i - 1 - group * g) if descending else group * g, group)
        return carry

    if descending:
        step([(i, True)])
        lax.fori_loop(0, full, loop, 0)
        leftovers()
    else:
        lax.fori_loop(0, full, loop, 0)
        leftovers()
        step([(i, True)])


def _attn_specs(s, tq, pair0):
    q_nat = pl.BlockSpec((tq, LANES), lambda p, i: (i, p + pair0))
    q_t = pl.BlockSpec((LANES, tq), lambda p, i: (p + pair0, i))
    k_nat = pl.BlockSpec((s, LANES), lambda p, i: (0, p + pair0))
    k_t = pl.BlockSpec((LANES, s), lambda p, i: (p + pair0, 0))
    return q_nat, q_t, k_nat, k_t


def _fox_fwd(qt, kh, vt, frow, fcol, name, tq=256):
    s = kh.shape[0]
    tq = _tile(s, tq)
    nq, half = s // tq, MIX // LANES // 2

    def body(qt_ref, k_ref, vt_ref, fr_ref, fc_ref, o_ref, lse_ref, m_s, l_s, acc_s):
        i = pl.program_id(1)
        qt_v = qt_ref[...]
        ft = fr_ref[0]
        key, qry = _key_query_iotas(tq)
        causal = key <= qry
        m_s[...] = jnp.full(m_s.shape, NEG, F32)
        l_s[...] = jnp.zeros_like(l_s)
        acc_s[...] = jnp.zeros_like(acc_s)

        top = _top_rows((LANES, tq))

        def step(blocks):
            rows = [pl.ds(pl.multiple_of(j * tq, tq), tq) for j, _ in blocks]
            zs = [_dot(_stack_heads(k_ref[r, :]), qt_v) for r in rows]
            m_cur, l_cur = [m_s[0], m_s[1]], [l_s[0], l_s[1]]
            acc = acc_s[...]
            for b, (_, masked) in enumerate(blocks):
                fk = fc_ref[0, rows[b], :]
                prs, alphas = [], []
                for hh in range(2):
                    sc = zs[b][hh * tq:(hh + 1) * tq] + (ft[hh:hh + 1, :] - fk[:, hh:hh + 1])
                    if masked:
                        sc = jnp.where(causal, sc, NEG)
                    m_new = jnp.maximum(m_cur[hh], jnp.max(sc, axis=0, keepdims=True))
                    alpha = jnp.exp(m_cur[hh] - m_new)
                    pr = jnp.exp(sc - m_new)
                    l_cur[hh] = alpha * l_cur[hh] + jnp.sum(pr, axis=0, keepdims=True)
                    m_cur[hh] = m_new
                    prs.append(pr.astype(BF16))
                    alphas.append(alpha)
                pv = _dot(_stack_heads_t(vt_ref[:, rows[b]]), jnp.concatenate(prs, axis=0))
                acc = jnp.where(top, alphas[0], alphas[1]) * acc + pv
            acc_s[...] = acc
            for hh in range(2):
                m_s[hh] = m_cur[hh]
                l_s[hh] = l_cur[hh]

        _walk_blocks(i, step, descending=False, group=4)
        o_ref[...] = (acc_s[...] / jnp.where(top, l_s[0], l_s[1])).T
        lse_ref[0, 0:1, :] = m_s[0] + jnp.log(l_s[0])
        lse_ref[0, 1:2, :] = m_s[1] + jnp.log(l_s[1])

    _, q_t, k_nat, k_t = _attn_specs(s, tq, 0)
    qstat = pl.BlockSpec((1, 2, tq), lambda p, i: (p, 0, i))
    return pl.pallas_call(
        body, name=name, grid=(half, nq),
        in_specs=[q_t, k_nat, k_t, qstat, pl.BlockSpec((1, s, 2), lambda p, i: (p, 0, 0))],
        out_specs=[pl.BlockSpec((tq, LANES), lambda p, i: (i, p)), qstat],
        out_shape=[jax.ShapeDtypeStruct((s, MIX // 2), F32), jax.ShapeDtypeStruct((half, 2, s), F32)],
        scratch_shapes=[pltpu.VMEM((2, 1, tq), F32), pltpu.VMEM((2, 1, tq), F32), pltpu.VMEM((LANES, tq), F32)],
        compiler_params=_params("parallel", "arbitrary"),
    )(qt, kh, vt, frow, fcol)


def _fox_bwd(qh, qt, kh, kt, vb, frow, fcol, lse, o, do, name, tq=256):
    s = kh.shape[0]
    tq = _tile(s, tq)
    nq, half = s // tq, MIX // LANES // 2

    def body(q_ref, qt_ref, k_ref, kt_ref, v_ref, fr_ref, fc_ref, lse_ref, o_ref, do_ref,
             dq_ref, dk_ref, dv_ref, dfk_ref, dfq_ref, dq_s, rs_s):
        i = pl.program_id(1)

        @pl.when(i == 0)
        def _():
            dk_ref[...] = jnp.zeros_like(dk_ref)
            dv_ref[...] = jnp.zeros_like(dv_ref)
            dfk_ref[...] = jnp.zeros_like(dfk_ref)

        q2 = _stack_heads(q_ref[...])
        qt_v = qt_ref[...]
        do_v = do_ref[...]
        do2 = _stack_heads(do_v.astype(BF16))
        dot_v = do_v.T.astype(BF16)
        dsum = _pair_colsum_t((do_v * o_ref[...]).T)
        ft, ls = fr_ref[0], lse_ref[0]
        key, qry = _key_query_iotas(tq)
        causal = key <= qry
        lane = lax.broadcasted_iota(jnp.int32, (2 * tq, LANES), 0) // tq
        pick2 = (lax.broadcasted_iota(jnp.int32, (2 * tq, LANES), 1) == lane).astype(BF16)
        q2_pick = jnp.concatenate([q2, pick2], axis=1)
        dq_s[...] = jnp.zeros_like(dq_s)
        rs_s[...] = jnp.zeros_like(rs_s)

        def step(blocks):
            rows = [pl.ds(pl.multiple_of(j * tq, tq), tq) for j, _ in blocks]
            zs = [_dot(_stack_heads(k_ref[r, :]), qt_v) for r in rows]
            dps = [_dot(_stack_heads(v_ref[r, :]), dot_v) for r in rows]
            rs = [rs_s[0], rs_s[1]]
            dq = None
            for b, (_, masked) in enumerate(blocks):
                fk = fc_ref[0, rows[b], :]
                prs, dss = [], []
                for hh in range(2):
                    blk = slice(hh * tq, (hh + 1) * tq)
                    sc = zs[b][blk] + (ft[hh:hh + 1, :] - fk[:, hh:hh + 1])
                    pr = jnp.exp(sc - ls[hh:hh + 1, :])
                    if masked:
                        pr = jnp.where(causal, pr, 0.0)
                    dsb = (pr * (dps[b][blk] - dsum[hh])).astype(BF16)
                    rs[hh] = rs[hh] + jnp.sum(dsb.astype(F32), axis=0, keepdims=True)
                    prs.append(pr.astype(BF16))
                    dss.append(dsb)
                dv_ref[rows[b], :] += _dot(jnp.concatenate(prs, axis=1), do2)
                both = _dot(jnp.concatenate(dss, axis=1), q2_pick)
                dk_ref[rows[b], :] += both[:, :LANES]
                dfk_ref[0, rows[b], :] -= both[:, LANES:]
                term = _dot(_stack_heads_t(kt_ref[:, rows[b]]), jnp.concatenate(dss, axis=0))
                dq = term if dq is None else dq + term
            rs_s[0], rs_s[1] = rs
            dq_s[...] += dq

        _walk_blocks(i, step, descending=False, group=4)
        dq_ref[...] = dq_s[...].T
        dfq_ref[0, 0:1, :] = rs_s[0]
        dfq_ref[0, 1:2, :] = rs_s[1]

    q_nat, q_t, k_nat, k_t = _attn_specs(s, tq, 0)
    qstat = pl.BlockSpec((1, 2, tq), lambda p, i: (p, 0, i))
    otile = pl.BlockSpec((tq, LANES), lambda p, i: (i, p))
    oresident = pl.BlockSpec((s, LANES), lambda p, i: (0, p))
    out = jax.ShapeDtypeStruct((s, MIX // 2), F32)
    return pl.pallas_call(
        body, name=name, grid=(half, nq),
        in_specs=[q_nat, q_t, k_nat, k_t, k_nat, qstat, pl.BlockSpec((1, s, 2), lambda p, i: (p, 0, 0)), qstat,
                  otile, q_nat],
        out_specs=[otile, oresident, oresident, pl.BlockSpec((1, s, LANES), lambda p, i: (p, 0, 0)), qstat],
        out_shape=[out, out, out, jax.ShapeDtypeStruct((half, s, LANES), F32),
                   jax.ShapeDtypeStruct((half, 2, s), F32)],
        scratch_shapes=[pltpu.VMEM((LANES, tq), F32), pltpu.VMEM((2, 1, tq), F32)],
        compiler_params=_params("parallel", "arbitrary"),
    )(qh, qt, kh, kt, vb, frow, fcol, lse, o, do)


def _log_sig_pair(z):
    zc = jnp.maximum(z, -80.0)
    lb = -jnp.log(1.0 + jnp.exp(-zc))
    return lb, lb - zc


def _sb_fwd(qt, kh, vt, name, tq=256, gather=()):
    s = kh.shape[0]
    tq = _tile(s, tq)
    nq, half = s // tq, MIX // LANES // 2
    ng = len(gather)

    def body(*refs):
        qt_ref, k_ref, vt_ref = refs[:3]
        o_ref, tot_ref = refs[3 + ng:5 + ng]
        c_s, acc_s = refs[5 + 2 * ng:7 + 2 * ng]
        p, i = pl.program_id(0), pl.program_id(1)
        if ng:
            start, relay, finish = _gather_stages(refs[3:3 + ng], refs[5 + ng:5 + 2 * ng], *refs[7 + 2 * ng:])
            pl.when((p == 0) & (i == 0))(start)
            pl.when((p == half - 1) & (i == 0))(relay)
        qt_v = qt_ref[...]
        key, qry = _key_query_iotas(tq)
        strict = key < qry
        later = _tri(tq, lambda a, b: b > a)
        c_s[...] = jnp.zeros_like(c_s)
        acc_s[...] = jnp.zeros_like(acc_s)

        def step(blocks):
            rows = [pl.ds(pl.multiple_of(j * tq, tq), tq) for j, _ in blocks]
            zs = [_dot(_stack_heads(k_ref[r, :]), qt_v) for r in rows]
            lbs, loms, afters = [], [], []
            for b, (_, masked) in enumerate(blocks):
                for hh in range(2):
                    lb, lom = _log_sig_pair(zs[b][hh * tq:(hh + 1) * tq])
                    if masked:
                        lom = jnp.where(strict, lom, 0.0)
                    lbs.append(lb)
                    loms.append(lom)
                afters.append(_split_dot_left(later, jnp.concatenate(loms[2 * b:2 * b + 2], axis=1), 2))
            carry = [c_s[0], c_s[1]]
            pv = None
            for b, (_, masked) in enumerate(blocks):
                ws = []
                for hh in range(2):
                    n = 2 * b + hh
                    w = jnp.exp(lbs[n] + afters[b][:, hh * tq:(hh + 1) * tq] + carry[hh])
                    if masked:
                        w = jnp.where(strict, w, 0.0)
                    ws.append(w.astype(BF16))
                    carry[hh] = carry[hh] + jnp.sum(loms[n], axis=0, keepdims=True)
                term = _dot(_stack_heads_t(vt_ref[:, rows[b]]), jnp.concatenate(ws, axis=0))
                pv = term if pv is None else pv + term
            c_s[0], c_s[1] = carry
            acc_s[...] += pv

        _walk_blocks(i, step, descending=True, group=4)
        o_ref[...] = acc_s[...].T
        tot_ref[0, 0:1, :] = c_s[0]
        tot_ref[0, 1:2, :] = c_s[1]
        if ng:
            pl.when((p == half - 1) & (i == nq - 1))(finish)

    _, q_t, k_nat, k_t = _attn_specs(s, tq, half)
    qstat = pl.BlockSpec((1, 2, tq), lambda p, i: (p, 0, i))
    return pl.pallas_call(
        body, name=name, grid=(half, nq),
        in_specs=[q_t, k_nat, k_t] + [ANY] * ng,
        out_specs=[pl.BlockSpec((tq, LANES), lambda p, i: (i, p)), qstat] + [ANY] * ng,
        out_shape=[jax.ShapeDtypeStruct((s, MIX // 2), F32), jax.ShapeDtypeStruct((half, 2, s), F32)]
        + [jax.ShapeDtypeStruct((N_DEV,) + x.shape, x.dtype) for x in gather],
        scratch_shapes=[pltpu.VMEM((2, 1, tq), F32), pltpu.VMEM((LANES, tq), F32)]
        + (_gather_scratch(ng) if ng else []),
        compiler_params=_params("arbitrary", "arbitrary") if ng else _params("parallel", "arbitrary"),
    )(qt, kh, vt, *gather)


def _sb_bwd(qh, qt, kh, kt, vb, tot, do, name, tq=256, exchange=()):
    s = kh.shape[0]
    tq = _tile(s, tq)
    nq, half = s // tq, MIX // LANES // 2
    nx = len(exchange)

    def body(*refs):
        q_ref, qt_ref, k_ref, kt_ref, v_ref, tot_ref, do_ref = refs[:7]
        dq_ref, dk_ref, dv_ref = refs[7 + nx:10 + nx]
        rem_s, pg_s, dq_s = refs[10 + 2 * nx:13 + 2 * nx]
        p, i = pl.program_id(0), pl.program_id(1)
        if nx:
            start, finish = _chip_exchange_stages(refs[7:7 + nx], refs[10 + nx:10 + 2 * nx], *refs[13 + 2 * nx:])
            pl.when((p == 0) & (i == 0))(start)

        @pl.when(i == 0)
        def _():
            dk_ref[...] = jnp.zeros_like(dk_ref)
            dv_ref[...] = jnp.zeros_like(dv_ref)

        q2 = _stack_heads(q_ref[...])
        qt_v = qt_ref[...]
        do_v = do_ref[...]
        do2 = _stack_heads(do_v.astype(BF16))
        dot_v = do_v.T.astype(BF16)
        key, qry = _key_query_iotas(tq)
        strict = key < qry
        upto = _tri(tq, lambda a, b: b <= a)
        before = _tri(tq, lambda a, b: b < a)
        tv = tot_ref[0]
        rem_s[0] = tv[0:1, :]
        rem_s[1] = tv[1:2, :]
        pg_s[...] = jnp.zeros_like(pg_s)
        dq_s[...] = jnp.zeros_like(dq_s)

        def step(blocks):
            nb = len(blocks)
            rows = [pl.ds(pl.multiple_of(j * tq, tq), tq) for j, _ in blocks]
            zs = [_dot(_stack_heads(k_ref[r, :]), qt_v) for r in rows]
            dws = [_dot(_stack_heads(v_ref[r, :]), dot_v) for r in rows]
            lbs, loms, prefixes = [], [], []
            for b, (_, masked) in enumerate(blocks):
                for hh in range(2):
                    lb, lom = _log_sig_pair(zs[b][hh * tq:(hh + 1) * tq])
                    if masked:
                        lom = jnp.where(strict, lom, 0.0)
                    lbs.append(lb)
                    loms.append(lom)
                prefixes.append(_split_dot_left(upto, jnp.concatenate(loms[2 * b:2 * b + 2], axis=1), 2))
            rem = [rem_s[0], rem_s[1]]
            ws, gs, gpres = [], [], []
            for b, (_, masked) in enumerate(blocks):
                for hh in range(2):
                    n = 2 * b + hh
                    blk = slice(hh * tq, (hh + 1) * tq)
                    w = jnp.exp(lbs[n] + (rem[hh] - prefixes[b][:, blk]))
                    if masked:
                        w = jnp.where(strict, w, 0.0)
                    gs.append(dws[b][blk] * w)
                    ws.append(w.astype(BF16))
                    rem[hh] = rem[hh] - jnp.sum(loms[n], axis=0, keepdims=True)
                gpres.append(_dot(before, jnp.concatenate(gs[2 * b:2 * b + 2], axis=1).astype(BF16)))
                dv_ref[rows[b], :] += _dot(jnp.concatenate(ws[2 * b:2 * b + 2], axis=1), do2)
            rem_s[0], rem_s[1] = rem
            pg = [pg_s[0], pg_s[1]]
            dq = None
            for b, (_, masked) in enumerate(blocks):
                dzs = []
                for hh in range(2):
                    n = 2 * b + hh
                    g = gs[n]
                    dz = g - jnp.exp(lbs[n]) * (g + (pg[hh] + gpres[b][:, hh * tq:(hh + 1) * tq]))
                    if masked:
                        dz = jnp.where(strict, dz, 0.0)
                    dzs.append(dz.astype(BF16))
                    pg[hh] = pg[hh] + jnp.sum(g, axis=0, keepdims=True)
                dk_ref[rows[b], :] += _dot(jnp.concatenate(dzs, axis=1), q2)
                term = _dot(_stack_heads_t(kt_ref[:, rows[b]]), jnp.concatenate(dzs, axis=0))
                dq = term if dq is None else dq + term
            pg_s[0], pg_s[1] = pg
            dq_s[...] += dq

        _walk_blocks(i, step, descending=False, group=2)
        dq_ref[...] = dq_s[...].T
        if nx:
            pl.when((p == half - 1) & (i == nq - 1))(finish)

    q_nat, q_t, k_nat, k_t = _attn_specs(s, tq, half)
    qstat = pl.BlockSpec((1, 2, tq), lambda p, i: (p, 0, i))
    otile = pl.BlockSpec((tq, LANES), lambda p, i: (i, p))
    oresident = pl.BlockSpec((s, LANES), lambda p, i: (0, p))
    out = jax.ShapeDtypeStruct((s, MIX // 2), F32)
    return pl.pallas_call(
        body, name=name, grid=(half, nq),
        in_specs=[q_nat, q_t, k_nat, k_t, k_nat, qstat, q_nat] + [ANY] * nx,
        out_specs=[otile, oresident, oresident] + [ANY] * nx,
        out_shape=[out, out, out] + [jax.ShapeDtypeStruct(x.shape, x.dtype) for x in exchange],
        scratch_shapes=[pltpu.VMEM((2, 1, tq), F32), pltpu.VMEM((2, 1, tq), F32), pltpu.VMEM((LANES, tq), F32)]
        + (_chip_exchange_scratch(nx) if nx else []),
        compiler_params=_params("arbitrary", "arbitrary") if nx else _params("parallel", "arbitrary"),
    )(qh, qt, kh, kt, vb, tot, do, *exchange)


def _loss_head(y, target, name, ts=512):
    s, d = y.shape
    ts = _tile(s, ts, 8)
    nt = s // ts

    def body(y_ref, t_ref, dy_ref, l_ref, acc):
        i = pl.program_id(0)
        err = y_ref[...] - t_ref[...]
        dy_ref[...] = err * (1.0 / d)

        @pl.when(i == 0)
        def _():
            acc[...] = jnp.zeros_like(acc)

        acc[...] += jnp.sum(err * err, axis=0, keepdims=True)

        @pl.when(i == nt - 1)
        def _():
            tot = jnp.sum(acc[...], keepdims=True).reshape(1, 1) * (0.5 / d)
            l_ref[...] = jnp.broadcast_to(tot, l_ref.shape)

    row = pl.BlockSpec((ts, d), lambda i: (i, 0))
    dy, l = pl.pallas_call(
        body, name=name, grid=(nt,), in_specs=[row, row],
        out_specs=[row, pl.BlockSpec((8, LANES), lambda i: (0, 0))],
        out_shape=[jax.ShapeDtypeStruct((s, d), F32), jax.ShapeDtypeStruct((8, LANES), F32)],
        scratch_shapes=[pltpu.VMEM((1, d), F32)],
        compiler_params=_params("arbitrary"),
    )(y, target)
    return l[0, 0], dy


def _coords():
    return lax.axis_index("x"), lax.axis_index("y"), lax.axis_index("c")


def _other_chips(xi, yi):
    return [(1 - xi, yi), (xi, 1 - yi), (1 - xi, 1 - yi)]


def _gather_stages(x_refs, out_refs, send_sems, recv_sems, local_sems):
    n = len(x_refs)
    xi, yi, ci = _coords()
    me, sibling = (xi, yi, ci), (xi, yi, 1 - ci)
    chips = _other_chips(xi, yi)

    def slot(a, px, py, pc):
        return out_refs[a].at[4 * px + 2 * py + pc]

    def copy(a, k, block, to, src=None):
        return pltpu.make_async_remote_copy(
            src_ref=slot(a, *block) if src is None else src, dst_ref=slot(a, *block),
            send_sem=send_sems.at[a, k], recv_sem=recv_sems.at[a, k], device_id=to, device_id_type=MESH)

    def own(a):
        return pltpu.make_async_copy(x_refs[a], slot(a, *me), local_sems.at[a])

    def first(a):
        return [copy(a, 0, me, sibling, src=x_refs[a])] + [
            copy(a, 1 + j, me, (*chip, ci), src=x_refs[a]) for j, chip in enumerate(chips)]

    def passed(a, j):
        return copy(a, 4 + j, (*chips[j], ci), sibling)

    def start():
        for a in range(n):
            own(a).start()
        for a in range(n):
            for cp in first(a):
                cp.start()

    def relay():
        for j, chip in enumerate(chips):
            for a in range(n):
                copy(a, 1 + j, (*chip, ci), me).wait_recv()
                passed(a, j).start()

    def finish():
        for a in range(n):
            copy(a, 0, sibling, me).wait_recv()
            for j, chip in enumerate(chips):
                copy(a, 4 + j, (*chip, 1 - ci), me).wait_recv()
        for a in range(n):
            for cp in first(a) + [passed(a, j) for j in range(len(chips))]:
                cp.wait_send()
            own(a).wait()

    return start, relay, finish


def _gather_scratch(n):
    return [pltpu.SemaphoreType.DMA((n, 7)), pltpu.SemaphoreType.DMA((n, 7)), pltpu.SemaphoreType.DMA((n,))]


def _all_gather(xs, name):
    n = len(xs)

    def body(*refs):
        start, relay, finish = _gather_stages(refs[:n], refs[n:2 * n], *refs[2 * n:])
        start()
        relay()
        finish()

    return pl.pallas_call(
        body, name=name, out_shape=[jax.ShapeDtypeStruct((N_DEV,) + x.shape, x.dtype) for x in xs],
        in_specs=[ANY] * n, out_specs=[ANY] * n, scratch_shapes=_gather_scratch(n),
    )(*xs)


def _sibling_exchange(gs, name):
    n = len(gs)

    def body(*refs):
        g_refs, recv_refs = refs[:n], refs[n:2 * n]
        send_sems, recv_sems = refs[2 * n:]
        xi, yi, ci = _coords()
        cps = [pltpu.make_async_remote_copy(
            src_ref=g_refs[a].at[2 * chip + (1 - ci)], dst_ref=recv_refs[a].at[chip],
            send_sem=send_sems.at[a, chip], recv_sem=recv_sems.at[a, chip],
            device_id=(xi, yi, 1 - ci), device_id_type=MESH) for a in range(n) for chip in range(N_CHIP)]
        for cp in cps:
            cp.start()
        for cp in cps:
            cp.wait()

    return pl.pallas_call(
        body, name=name, out_shape=[jax.ShapeDtypeStruct((N_CHIP,) + g.shape[1:], g.dtype) for g in gs],
        in_specs=[ANY] * n, out_specs=[ANY] * n,
        scratch_shapes=[pltpu.SemaphoreType.DMA((n, N_CHIP)), pltpu.SemaphoreType.DMA((n, N_CHIP))],
    )(*gs)


def _pair_add(g, recv, ids, name, tr=256):
    _, r, c = g.shape
    tr = _tile(r, tr, 16)

    def body(ids_ref, g_ref, r_ref, p_ref, own_ref):
        kk = pl.program_id(1)
        tot = g_ref[0].astype(F32) + r_ref[0].astype(F32)
        p_ref[0] = tot.astype(BF16)

        @pl.when(kk == ids_ref[1])
        def _():
            own_ref[...] = tot

    grid_spec = pltpu.PrefetchScalarGridSpec(
        num_scalar_prefetch=1, grid=(r // tr, N_CHIP),
        in_specs=[pl.BlockSpec((1, tr, c), lambda i, kk, ids: (2 * kk + ids[0], i, 0)),
                  pl.BlockSpec((1, tr, c), lambda i, kk, ids: (kk, i, 0))],
        out_specs=[pl.BlockSpec((1, tr, c), lambda i, kk, ids: (kk, i, 0)),
                   pl.BlockSpec((tr, c), lambda i, kk, ids: (i, 0))])
    return pl.pallas_call(
        body, name=name, grid_spec=grid_spec,
        out_shape=[jax.ShapeDtypeStruct((N_CHIP, r, c), BF16), jax.ShapeDtypeStruct((r, c), F32)],
        compiler_params=_params("parallel", "arbitrary"),
    )(ids, g, recv)


def _chip_exchange_stages(p_refs, recv_refs, send_sems, recv_sems):
    n = len(p_refs)
    xi, yi, ci = _coords()
    mine = 2 * xi + yi
    chips = _other_chips(xi, yi)

    def copy(a, k, cx, cy):
        return pltpu.make_async_remote_copy(
            src_ref=p_refs[a].at[2 * cx + cy], dst_ref=recv_refs[a].at[mine],
            send_sem=send_sems.at[a, k], recv_sem=recv_sems.at[a, k],
            device_id=(cx, cy, ci), device_id_type=MESH)

    def landed(a, k, cx, cy):
        return pltpu.make_async_remote_copy(
            src_ref=p_refs[a].at[mine], dst_ref=recv_refs[a].at[2 * cx + cy],
            send_sem=send_sems.at[a, k], recv_sem=recv_sems.at[a, k],
            device_id=(cx, cy, ci), device_id_type=MESH)

    def start():
        for a in range(n):
            for k, (cx, cy) in enumerate(chips):
                copy(a, k, cx, cy).start()

    def finish():
        for a in range(n):
            for k, (cx, cy) in enumerate(chips):
                landed(a, k, cx, cy).wait_recv()
        for a in range(n):
            for k, (cx, cy) in enumerate(chips):
                copy(a, k, cx, cy).wait_send()

    return start, finish


def _chip_exchange_scratch(n):
    return [pltpu.SemaphoreType.DMA((n, 3)), pltpu.SemaphoreType.DMA((n, 3))]


def _chip_exchange(ps, name):
    n = len(ps)

    def body(*refs):
        start, finish = _chip_exchange_stages(refs[:n], refs[n:2 * n], *refs[2 * n:])
        start()
        finish()

    return pl.pallas_call(
        body, name=name, out_shape=[jax.ShapeDtypeStruct(p.shape, p.dtype) for p in ps],
        in_specs=[ANY] * n, out_specs=[ANY] * n, scratch_shapes=_chip_exchange_scratch(n),
    )(*ps)


def _adamw_math(w, g, m, v):
    m = ADAM_B1 * m + (1.0 - ADAM_B1) * g
    v = ADAM_B2 * v + (1.0 - ADAM_B2) * (g * g)
    m_hat = m / (1.0 - ADAM_B1 ** ADAM_STEP)
    v_hat = v / (1.0 - ADAM_B2 ** ADAM_STEP)
    delta = -ADAM_LR * (m_hat / (jnp.sqrt(v_hat) + ADAM_EPS) + ADAM_WD * w)
    return delta, m, v


def _adamw_reduce(own, recv, ids, w, m, v, name, tr=256):
    r, c = w.shape
    tr = _tile(r, tr, 16)

    def body(ids_ref, own_ref, recv_ref, w_ref, m_ref, v_ref, g_ref, d_ref, mo_ref, vo_ref):
        mine = ids_ref[1]
        g = None
        for kk in range(N_CHIP):
            term = jnp.where(mine == kk, own_ref[...], recv_ref[kk].astype(F32))
            g = term if g is None else g + term
        delta, m_new, v_new = _adamw_math(w_ref[...], g, m_ref[...], v_ref[...])
        g_ref[...] = g
        d_ref[...] = delta
        mo_ref[...] = m_new
        vo_ref[...] = v_new

    row = pl.BlockSpec((tr, c), lambda i, ids: (i, 0))
    grid_spec = pltpu.PrefetchScalarGridSpec(
        num_scalar_prefetch=1, grid=(r // tr,),
        in_specs=[row, pl.BlockSpec((N_CHIP, tr, c), lambda i, ids: (0, i, 0)), row, row, row],
        out_specs=[row, row, row, row])
    out = jax.ShapeDtypeStruct((r, c), F32)
    return pl.pallas_call(
        body, name=name, grid_spec=grid_spec, out_shape=[out, out, out, out],
        compiler_params=_params("parallel"),
    )(ids, own, recv, w, m, v)


def _sum_sources(a, name):
    n, r, c = a.shape

    def body(a_ref, o_ref):
        tot = a_ref[0]
        for kk in range(1, n):
            tot = tot + a_ref[kk]
        o_ref[...] = tot

    return pl.pallas_call(
        body, name=name, out_shape=jax.ShapeDtypeStruct((r, c), F32),
        in_specs=[pl.BlockSpec(memory_space=pltpu.VMEM)], out_specs=pl.BlockSpec(memory_space=pltpu.VMEM),
    )(a)


def _adamw_small(w, g, m, v, name):
    def body(w_ref, g_ref, m_ref, v_ref, d_ref, mo_ref, vo_ref):
        delta, m_new, v_new = _adamw_math(w_ref[...], g_ref[...], m_ref[...], v_ref[...])
        d_ref[...] = delta
        mo_ref[...] = m_new
        vo_ref[...] = v_new

    vm = pl.BlockSpec(memory_space=pltpu.VMEM)
    out = jax.ShapeDtypeStruct(w.shape, F32)
    return pl.pallas_call(body, name=name, out_shape=[out, out, out], in_specs=[vm] * 4, out_specs=[vm] * 3)(w, g, m, v)


def _pack(parts, width, row_mult):
    flat = jnp.concatenate([p.reshape(-1) for p in parts])
    rows = -(-flat.shape[0] // width)
    rows = -(-rows // row_mult) * row_mult
    return jnp.pad(flat, (0, rows * width - flat.shape[0])).reshape(rows, width)


def _unpack(flat, shapes):
    out, off = [], 0
    lead = flat.shape[:-1]
    for shp in shapes:
        n = 1
        for dd in shp:
            n *= dd
        out.append(flat[..., off:off + n].reshape(lead + tuple(shp)))
        off += n
    return out


def _rows2d(w):
    return w.reshape(w.shape[0] * w.shape[1], w.shape[2])


def _cols_to_dev(g):
    l, k, n = g.shape
    return g.reshape(l * k, N_DEV, n // N_DEV).transpose(1, 0, 2)


def _rows_to_dev(g):
    l, k, n = g.shape
    rs = k // N_DEV
    return g.reshape(l, N_DEV, rs, n).transpose(1, 0, 2, 3).reshape(N_DEV, l * rs, n)


def _dev_to_cols(a, l):
    _, lk, cs = a.shape
    return a.transpose(1, 0, 2).reshape(l, lk // l, N_DEV * cs)


def _dev_to_rows(a, l):
    _, lr, n = a.shape
    rs = lr // l
    return a.reshape(N_DEV, l, rs, n).transpose(1, 0, 2, 3).reshape(l, N_DEV * rs, n)


def kernel(x, attn_norm, attn_w_in, attn_f_bias, fox_q_gain, fox_k_gain, sb_q_gain, sb_k_gain, attn_w_out, conv_norm, conv_w_in, conv_kernel, conv_w_out, ffn_norm, ffn_w_up, ffn_conv, ffn_w_down, loss_target, m_attn_norm, m_attn_w_in, m_attn_f_bias, m_fox_q_gain, m_fox_k_gain, m_sb_q_gain, m_sb_k_gain, m_attn_w_out, m_conv_norm, m_conv_w_in, m_conv_kernel, m_conv_w_out, m_ffn_norm, m_ffn_w_up, m_ffn_conv, m_ffn_w_down, v_attn_norm, v_attn_w_in, v_attn_f_bias, v_fox_q_gain, v_fox_k_gain, v_sb_q_gain, v_sb_k_gain, v_attn_w_out, v_conv_norm, v_conv_w_in, v_conv_kernel, v_conv_w_out, v_ffn_norm, v_ffn_w_up, v_ffn_conv, v_ffn_w_down):
    s = x.shape[1]
    n_attn, n_conv, depth = attn_w_in.shape[0], conv_w_in.shape[0], ffn_w_up.shape[0]
    xi, yi, ci = _coords()
    me = 4 * xi + 2 * yi + ci
    ids = jnp.stack([ci, 2 * xi + yi]).astype(jnp.int32)

    big = [attn_w_in, attn_w_out, conv_w_in, conv_w_out, ffn_w_up, ffn_w_down]
    big_m = [m_attn_w_in, m_attn_w_out, m_conv_w_in, m_conv_w_out, m_ffn_w_up, m_ffn_w_down]
    big_v = [v_attn_w_in, v_attn_w_out, v_conv_w_in, v_conv_w_out, v_ffn_w_up, v_ffn_w_down]
    big_names = ["attn_w_in", "attn_w_out", "conv_w_in", "conv_w_out", "ffn_w_up", "ffn_w_down"]
    small_sh = [conv_norm, conv_kernel, ffn_conv]
    small_sh_shapes = [w.shape for w in small_sh]
    rep = [attn_norm, attn_f_bias, fox_q_gain, fox_k_gain, sb_q_gain, sb_k_gain, ffn_norm]
    rep_shapes = [w.shape for w in rep]

    small_pack = _pack(small_sh, LANES, 8)
    shards_bf16 = [_rows2d(w).astype(BF16) for w in big]
    early = _all_gather([attn_w_in[0].astype(BF16), small_pack], "gather_first")
    first_w_in = jnp.pad(_dev_to_cols(early[0], 1)[0], ((0, 0), (0, ATTN_IN_PAD - ATTN_IN)))
    cn, ckern, fconv = _unpack(early[1].reshape(N_DEV, -1), small_sh_shapes)
    conv_norm_f = cn.transpose(1, 0, 2).reshape(n_conv, D_MODEL)
    conv_kernel_f = ckern.transpose(1, 2, 0, 3).reshape(n_conv, 3, D_MODEL)
    ffn_conv_f = fconv.transpose(1, 2, 0, 3).reshape(depth, 3, 2 * D_FF)

    def pair_gain(fox_g, sb_g):
        f2 = jnp.concatenate([fox_g, fox_g])
        s2 = jnp.concatenate([sb_g, sb_g])
        return jnp.concatenate([jnp.tile(f2[None], (4, 1)), jnp.tile(s2[None], (4, 1))])[:, None, :]

    h = x[0]
    saved = []
    for layer in range(depth):
        i = layer // 2
        tag = "l%d" % layer
        rec = {"h_in": h}
        if layer % 2 == 0:
            xn = _rms_fwd(h, attn_norm[i], tag + "_attn_rms")
            proj = _matmul(xn, first_w_in if layer == 0 else a_w_in[i], tag + "_attn_in", tn=640)
            gq, gk = pair_gain(fox_q_gain[i], sb_q_gain[i]), pair_gain(fox_k_gain[i], sb_k_gain[i])
            qh, kh, vb, qt, kt, vt = _qkv_prep(proj, gq, gk, tag + "_qkv_prep")
            logit = proj[:, 3 * MIX:3 * MIX + H_FOX].T.reshape(H_FOX, s // LANES, LANES)
            cum = _fgate_fwd(logit, attn_f_bias[i], tag + "_fgate")
            frow = cum.reshape(H_FOX // 2, 2, s)
            fcol = frow.transpose(0, 2, 1)
            o_fox, lse = _fox_fwd(qt, kh, vt, frow, fcol, tag + "_fox_fwd")
            if layer == 0:
                o_sb, tot, *gathered = _sb_fwd(qt, kh, vt, tag + "_sb_fwd", gather=shards_bf16)
                a_w_in = _dev_to_cols(gathered[0], n_attn)
                a_w_in = jnp.pad(a_w_in, ((0, 0), (0, 0), (0, ATTN_IN_PAD - ATTN_IN)))
                a_w_out = _dev_to_rows(gathered[1], n_attn)
                c_w_in = _dev_to_cols(gathered[2], n_conv)
                c_w_out = _dev_to_rows(gathered[3], n_conv)
                f_w_up = _dev_to_cols(gathered[4], depth)
                f_w_down = _dev_to_rows(gathered[5], depth)
            else:
                o_sb, tot = _sb_fwd(qt, kh, vt, tag + "_sb_fwd")
            o = jnp.concatenate([o_fox, o_sb], axis=1)
            h = _matmul(o, a_w_out[i], tag + "_attn_out", add=h, tn=1024)
            rec.update(xn=xn, proj=proj, gq=gq, gk=gk, qh=qh, kh=kh, vb=vb, qt=qt, kt=kt, logit=logit, frow=frow,
                       fcol=fcol, o_fox=o_fox, lse=lse, tot=tot, o=o)
        else:
            xn = _rms_fwd(h, conv_norm_f[i], tag + "_conv_rms")
            proj = _matmul(xn, c_w_in[i], tag + "_conv_in", tn=1024)
            y = _sconv_fwd(proj, conv_kernel_f[i], tag + "_sconv_fwd")
            h = _matmul(y, c_w_out[i], tag + "_conv_out", add=h, tn=1024)
            rec.update(xn=xn, proj=proj, y=y)
        rec["h_mid"] = h
        xn2 = _rms_fwd(h, ffn_norm[layer], tag + "_ffn_rms")
        up = _matmul(xn2, f_w_up[layer], tag + "_ffn_up", tn=1408)
        act = _ffn_act_fwd(up, ffn_conv_f[layer], tag + "_ffn_act")
        h = _matmul(act, f_w_down[layer], tag + "_ffn_down", add=h, tn=1024, tk=2816)
        rec.update(xn2=xn2, up=up, act=act)
        saved.append(rec)

    loss_local, dh = _loss_head(h, loss_target[0], "loss_head")
    loss = lax.psum(loss_local, ("x", "y", "c"))

    g_attn_norm, g_attn_w_in, g_f_bias = [None] * n_attn, [None] * n_attn, [None] * n_attn
    g_fq, g_fk, g_sq, g_sk, g_attn_w_out = ([None] * n_attn for _ in range(5))
    g_conv_norm, g_conv_w_in, g_conv_kernel, g_conv_w_out = ([None] * n_conv for _ in range(4))
    g_ffn_norm, g_ffn_w_up, g_ffn_conv, g_ffn_w_down = ([None] * depth for _ in range(4))
    everything = slice(0, None)
    early_layers = {nm: (slice(1, None) if nm == "attn_w_in" else everything) for nm in big_names}
    late_layers = {"attn_w_in": slice(0, 1)}

    def slabs_for_devices(layers):
        stacks = {"attn_w_in": (g_attn_w_in, _cols_to_dev), "attn_w_out": (g_attn_w_out, _rows_to_dev),
                  "conv_w_in": (g_conv_w_in, _cols_to_dev), "conv_w_out": (g_conv_w_out, _rows_to_dev),
                  "ffn_w_up": (g_ffn_w_up, _cols_to_dev), "ffn_w_down": (g_ffn_w_down, _rows_to_dev)}
        return [stacks[nm][1](jnp.stack(stacks[nm][0][layers[nm]]).astype(BF16)) for nm in big_names if nm in layers]

    for layer in reversed(range(depth)):
        i = layer // 2
        tag = "l%d" % layer
        rec = saved[layer]
        da = _matmul(dh, f_w_down[layer].T, tag + "_ffn_down_dx", tn=1408)
        g_ffn_w_down[layer] = _matmul_tn(rec["act"], dh, tag + "_ffn_down_dw", tm=1408, tn=1024)
        dup, dwg, dwv = _ffn_act_bwd(rec["up"], ffn_conv_f[layer], da, tag + "_ffn_act_bwd")
        g_ffn_conv[layer] = jnp.concatenate([dwg, dwv], axis=1)
        g_ffn_w_up[layer] = _matmul_tn(rec["xn2"], dup, tag + "_ffn_up_dw", tn=1408)
        dxn = _matmul(dup, f_w_up[layer].T, tag + "_ffn_up_dx", tn=1024, tk=2816)
        dh, g_ffn_norm[layer] = _rms_bwd(rec["h_mid"], dxn, ffn_norm[layer], dh, tag + "_ffn_rms_bwd")
        if layer % 2 == 0:
            do = _matmul(dh, a_w_out[i].T, tag + "_attn_out_dx", tn=1024)
            g_attn_w_out[i] = _matmul_tn(rec["o"], dh, tag + "_attn_out_dw", tn=1024)
            if layer == 0:
                early_g = slabs_for_devices(early_layers)
                early_pairs = [_pair_add(g, r, ids, "reduce_pair_add_" + nm)
                               for g, r, nm in zip(early_g, _sibling_exchange(early_g, "reduce_sibling"), big_names)]
            dq_f, dk_f, dv_f, dfk, dfq = _fox_bwd(rec["qh"], rec["qt"], rec["kh"], rec["kt"], rec["vb"], rec["frow"],
                                                  rec["fcol"], rec["lse"], rec["o_fox"], do, tag + "_fox_bwd")
            dq_s, dk_s, dv_s, *from_chips = _sb_bwd(
                rec["qh"], rec["qt"], rec["kh"], rec["kt"], rec["vb"], rec["tot"], do, tag + "_sb_bwd",
                exchange=[pr[0] for pr in early_pairs] if layer == 0 else ())
            if layer == 0:
                early_from_chips = from_chips
            dq, dk, dv, dgq, dgk = _qkv_prep_bwd(rec["proj"], rec["gq"], rec["gk"], (dq_f, dq_s), (dk_f, dk_s),
                                                 (dv_f, dv_s), tag + "_qkv_prep_bwd")
            dcum = (dfq + dfk[:, :, 0:2].transpose(0, 2, 1)).reshape(H_FOX, s // LANES, LANES)
            dlogit, dbias = _fgate_bwd(rec["logit"], attn_f_bias[i], dcum, tag + "_fgate_bwd")
            g_f_bias[i] = dbias[:, 0]
            dgate = jnp.pad(dlogit.reshape(H_FOX, s).T, ((0, 0), (0, LANES - H_FOX))).astype(BF16)
            dproj = jnp.concatenate([dq, dk, dv, dgate], axis=1)

            def fold(dg):
                per_head = dg.reshape(16, HEAD_DIM)
                return jnp.sum(per_head[:8], axis=0), jnp.sum(per_head[8:], axis=0)

            g_fq[i], g_sq[i] = fold(dgq)
            g_fk[i], g_sk[i] = fold(dgk)
            g_attn_w_in[i] = _matmul_tn(rec["xn"], dproj, tag + "_attn_in_dw", tn=640)[:, :ATTN_IN]
            dxn = _matmul(dproj, a_w_in[i].T, tag + "_attn_in_dx", tn=1024, tk=3200)
            dh, g_attn_norm[i] = _rms_bwd(rec["h_in"], dxn, attn_norm[i], dh, tag + "_attn_rms_bwd")
        else:
            dy = _matmul(dh, c_w_out[i].T, tag + "_conv_out_dx", tn=1024)
            g_conv_w_out[i] = _matmul_tn(rec["y"], dh, tag + "_conv_out_dw", tn=1024)
            dproj, g_conv_kernel[i] = _sconv_bwd(rec["proj"], conv_kernel_f[i], dy, tag + "_sconv_bwd")
            g_conv_w_in[i] = _matmul_tn(rec["xn"], dproj, tag + "_conv_in_dw", tn=1024)
            dxn = _matmul(dproj, c_w_in[i].T, tag + "_conv_in_dx", tn=1024, tk=3072)
            dh, g_conv_norm[i] = _rms_bwd(rec["h_in"], dxn, conv_norm_f[i], dh, tag + "_conv_rms_bwd")
    grad_x = dh[None]

    late_names = [nm for nm in big_names if nm in late_layers]
    late_g = slabs_for_devices(late_layers)
    late_pairs = [_pair_add(g, r, ids, "reduce_pair_add_late_" + nm)
                  for g, r, nm in zip(late_g, _sibling_exchange(late_g, "reduce_sibling_late"), late_names)]
    late_from_chips = _chip_exchange([pr[0] for pr in late_pairs], "reduce_chips_late")

    def update(piece_pairs, piece_recv, piece_names, layers):
        outs = {}
        for (own, recv, nm) in zip([pr[1] for pr in piece_pairs], piece_recv, piece_names):
            sl = layers[nm]
            which = big_names.index(nm)
            outs[nm] = _adamw_reduce(own, recv, ids, _rows2d(big[which][sl]), _rows2d(big_m[which][sl]),
                                     _rows2d(big_v[which][sl]), "adamw_%s_%d" % (nm, sl.start))
        return outs

    late_out = update(late_pairs, late_from_chips, late_names, late_layers)
    early_out = update(early_pairs, early_from_chips, big_names, early_layers)
    grads_big, delta_big, newm_big, newv_big = [], [], [], []
    for which, nm in enumerate(big_names):
        shp = big[which].shape
        for k, dest in enumerate((grads_big, delta_big, newm_big, newv_big)):
            parts = ([late_out[nm][k]] if nm in late_out else []) + [early_out[nm][k]]
            dest.append(jnp.concatenate(parts, axis=0).reshape(shp))

    rep_g = [jnp.stack(g_attn_norm), jnp.stack(g_f_bias), jnp.stack(g_fq), jnp.stack(g_fk), jnp.stack(g_sq),
             jnp.stack(g_sk), jnp.stack(g_ffn_norm)]
    sh_g = [jnp.stack(g_conv_norm).reshape(n_conv, N_DEV, -1).transpose(1, 0, 2),
            jnp.stack(g_conv_kernel).reshape(n_conv, 3, N_DEV, -1).transpose(2, 0, 1, 3),
            jnp.stack(g_ffn_conv).reshape(depth, 3, N_DEV, -1).transpose(2, 0, 1, 3)]
    n_rep = sum(int(a.size) for a in rep)
    n_sh = sum(int(a.size) for a in small_sh)
    partial = _pack(rep_g + [jnp.concatenate([a.reshape(N_DEV, -1) for a in sh_g], axis=1)], LANES, 8)
    total = _sum_sources(_all_gather([partial], "gather_small_grads")[0], "sum_small_grads").reshape(-1)
    rep_tot = total[:n_rep]
    sh_tot = lax.dynamic_slice_in_dim(total[n_rep:n_rep + N_DEV * n_sh].reshape(N_DEV, n_sh), me, 1, axis=0)[0]
    g_small = _pack([rep_tot, sh_tot], LANES, 8)

    def small_pack_of(rep_list, sh_list):
        return _pack(rep_list + sh_list, LANES, 8)

    d_small, m_small, v_small = _adamw_small(
        small_pack_of(rep, small_sh), g_small,
        small_pack_of([m_attn_norm, m_attn_f_bias, m_fox_q_gain, m_fox_k_gain, m_sb_q_gain, m_sb_k_gain, m_ffn_norm],
                      [m_conv_norm, m_conv_kernel, m_ffn_conv]),
        small_pack_of([v_attn_norm, v_attn_f_bias, v_fox_q_gain, v_fox_k_gain, v_sb_q_gain, v_sb_k_gain, v_ffn_norm],
                      [v_conv_norm, v_conv_kernel, v_ffn_conv]),
        "adamw_small")
    small_shapes = rep_shapes + small_sh_shapes

    def split_small(a):
        return _unpack(a.reshape(-1), small_shapes)

    def ordered(big_list, small_list):
        an, fb, fq, fk, sq, sk, fn, cno, cke, fco = small_list
        awi, awo, cwi, cwo, fwu, fwd = big_list
        return [an, awi, fb, fq, fk, sq, sk, awo, cno, cwi, cke, cwo, fn, fwu, fco, fwd]

    grads = ordered(grads_big, split_small(g_small))
    deltas = ordered(delta_big, split_small(d_small))
    new_m = ordered(newm_big, split_small(m_small))
    new_v = ordered(newv_big, split_small(v_small))
    return (loss, grad_x, *grads, *deltas, *new_m, *new_v)
```

```python
import jax
import jax.numpy as jnp
from jax import lax
from jax.experimental import pallas as pl
from jax.experimental.pallas import tpu as pltpu

F32 = jnp.float32
BF16 = jnp.bfloat16

D_MODEL = 1024
HEAD_DIM = 64
H_FOX = 8
MIX = 1024
ATTN_IN = 3 * MIX + H_FOX
ATTN_IN_PAD = 3 * MIX + 128
D_FF = 2816
EPS = 1e-6
NEG = -1e30
LANES = 128
N_DEV = 8
N_CHIP = 4

ADAM_LR = 0.001
ADAM_B1 = 0.9
ADAM_B2 = 0.999
ADAM_EPS = 1e-08
ADAM_WD = 0.01
ADAM_STEP = 10

VMEM_LIMIT = 56 * 1024 * 1024
MESH = pl.DeviceIdType.MESH
ANY = pl.BlockSpec(memory_space=pl.ANY)


def _params(*sem):
    return pltpu.CompilerParams(dimension_semantics=sem, vmem_limit_bytes=VMEM_LIMIT)


def _tile(n, target, mult=LANES):
    best = None
    for t in range(mult, min(n, target) + 1, mult):
        if n % t == 0:
            best = t
    return best if best is not None else n


def _dot(a, b):
    return jnp.dot(a, b, preferred_element_type=F32)


def _dot_tn(a, b):
    return lax.dot_general(a, b, (((0,), (0,)), ((), ())), preferred_element_type=F32)


def _split_dot(x, m, passes):
    acc = None
    rem = x
    for _ in range(passes):
        part = rem.astype(BF16)
        term = _dot(part, m)
        acc = term if acc is None else acc + term
        rem = rem - part.astype(F32)
    return acc


def _split_dot_left(m, x, passes):
    acc = None
    rem = x
    for _ in range(passes):
        part = rem.astype(BF16)
        term = _dot(m, part)
        acc = term if acc is None else acc + term
        rem = rem - part.astype(F32)
    return acc


def _matmul(a, b, name, add=None, out_dtype=F32, tm=1024, tn=512, tk=1024):
    split = a.shape[0] if a.ndim == 3 else 1
    m, kh = a.shape[-2:]
    k = split * kh
    n = b.shape[1]
    tm, tn, tk = _tile(m, tm, 8), _tile(n, tn), _tile(kh, tk)
    nk = k // tk
    per_slab = kh // tk
    has_add = add is not None

    def body(*refs):
        a_ref, b_ref = refs[0], refs[1]
        add_ref = refs[2] if has_add else None
        o_ref = refs[2 + has_add]

        def finish(acc):
            if has_add:
                acc = acc + add_ref[...]
            o_ref[...] = acc.astype(out_dtype)

        p = _dot(a_ref[...].astype(BF16), b_ref[...].astype(BF16))
        if nk == 1:
            finish(p)
        else:
            acc_ref = refs[-1]
            kk = pl.program_id(2)

            @pl.when(kk == 0)
            def _():
                acc_ref[...] = p

            @pl.when(kk > 0)
            def _():
                acc_ref[...] += p

            @pl.when(kk == nk - 1)
            def _():
                finish(acc_ref[...])

    if split == 1:
        a_spec = pl.BlockSpec((tm, tk), lambda i, j, kk: (i, kk))
    else:
        a_spec = pl.BlockSpec((None, tm, tk), lambda i, j, kk: (kk // per_slab, i, kk % per_slab))
    in_specs = [a_spec, pl.BlockSpec((tk, tn), lambda i, j, kk: (kk, j))]
    args = [a, b]
    if has_add:
        in_specs.append(pl.BlockSpec((tm, tn), lambda i, j, kk: (i, j)))
        args.append(add)
    return pl.pallas_call(
        body, name=name, grid=(m // tm, n // tn, nk), in_specs=in_specs,
        out_specs=pl.BlockSpec((tm, tn), lambda i, j, kk: (i, j)),
        out_shape=jax.ShapeDtypeStruct((m, n), out_dtype),
        scratch_shapes=[pltpu.VMEM((tm, tn), F32)] if nk > 1 else [],
        compiler_params=_params("parallel", "parallel", "arbitrary"),
    )(*args)


def _matmul_tn(a, b, name, tm=1024, tn=512, ts=2048):
    s, m = a.shape
    split = b.shape[0] if b.ndim == 3 else 1
    nh = b.shape[-1]
    n = split * nh
    tm, tn, ts = _tile(m, tm), _tile(nh, tn), _tile(s, ts, 8)
    per_slab = nh // tn
    if split == 1:
        b_spec = pl.BlockSpec((ts, tn), lambda i, j, kk: (kk, j))
    else:
        b_spec = pl.BlockSpec((None, ts, tn), lambda i, j, kk: (j // per_slab, kk, j % per_slab))

    def body(a_ref, b_ref, o_ref):
        kk = pl.program_id(2)
        p = _dot_tn(a_ref[...].astype(BF16), b_ref[...].astype(BF16))

        @pl.when(kk == 0)
        def _():
            o_ref[...] = p

        @pl.when(kk > 0)
        def _():
            o_ref[...] += p

    return pl.pallas_call(
        body, name=name, grid=(m // tm, n // tn, s // ts),
        in_specs=[pl.BlockSpec((ts, tm), lambda i, j, kk: (kk, i)), b_spec],
        out_specs=pl.BlockSpec((tm, tn), lambda i, j, kk: (i, j)),
        out_shape=jax.ShapeDtypeStruct((m, n), F32),
        compiler_params=_params("parallel", "parallel", "arbitrary"),
    )(a, b)


def _rms_fwd(h, g, name, ts=512):
    s, d = h.shape
    ts = _tile(s, ts, 8)

    def body(h_ref, g_ref, o_ref):
        x = h_ref[...]
        r = lax.rsqrt(jnp.mean(x * x, axis=-1, keepdims=True) + EPS)
        o_ref[...] = (x * r * g_ref[...]).astype(BF16)

    return pl.pallas_call(
        body, name=name, grid=(s // ts,),
        in_specs=[pl.BlockSpec((ts, d), lambda i: (i, 0)), pl.BlockSpec((1, d), lambda i: (0, 0))],
        out_specs=pl.BlockSpec((ts, d), lambda i: (i, 0)),
        out_shape=jax.ShapeDtypeStruct((s, d), BF16),
        compiler_params=_params("parallel"),
    )(h, g.reshape(1, d))


def _rms_bwd(h, dxn, g, dh_in, name, ts=512):
    s, d = h.shape
    ts = _tile(s, ts, 8)

    def body(h_ref, dxn_ref, g_ref, dhin_ref, dh_ref, dg_ref):
        i = pl.program_id(0)
        x = h_ref[...]
        r = lax.rsqrt(jnp.mean(x * x, axis=-1, keepdims=True) + EPS)
        xh = x * r
        dxn_v = dxn_ref[...]

        @pl.when(i == 0)
        def _():
            dg_ref[...] = jnp.zeros_like(dg_ref)

        dg_ref[0:1, :] += jnp.sum(dxn_v * xh, axis=0, keepdims=True)
        dxh = dxn_v * g_ref[...]
        dx = r * (dxh - xh * jnp.mean(dxh * xh, axis=-1, keepdims=True))
        dh_ref[...] = dhin_ref[...] + dx

    row = pl.BlockSpec((ts, d), lambda i: (i, 0))
    dh, dg = pl.pallas_call(
        body, name=name, grid=(s // ts,),
        in_specs=[row, row, pl.BlockSpec((1, d), lambda i: (0, 0)), row],
        out_specs=[row, pl.BlockSpec((8, d), lambda i: (0, 0))],
        out_shape=[jax.ShapeDtypeStruct((s, d), F32), jax.ShapeDtypeStruct((8, d), F32)],
        compiler_params=_params("arbitrary"),
    )(h, dxn, g.reshape(1, d), dh_in)
    return dh, dg[0]


def _shift_down(x, prev):
    rows = lax.broadcasted_iota(jnp.int32, (8, x.shape[1]), 0)
    p1, p2 = prev[7:8, :], prev[6:7, :]
    r1, r2 = pltpu.roll(x, 1, 0), pltpu.roll(x, 2, 0)
    top1 = jnp.where(rows == 0, p1, r1[0:8, :])
    top2 = jnp.where(rows == 0, p2, jnp.where(rows == 1, p1, r2[0:8, :]))
    if x.shape[0] == 8:
        return top1, top2
    return jnp.concatenate([top1, r1[8:, :]], axis=0), jnp.concatenate([top2, r2[8:, :]], axis=0)


def _shift_up(x, nxt):
    n = x.shape[0]
    rows = lax.broadcasted_iota(jnp.int32, (8, x.shape[1]), 0)
    n0, n1 = nxt[0:1, :], nxt[1:2, :]
    r1, r2 = pltpu.roll(x, n - 1, 0), pltpu.roll(x, n - 2, 0)
    end1 = jnp.where(rows == 7, n0, r1[n - 8:, :])
    end2 = jnp.where(rows == 7, n1, jnp.where(rows == 6, n0, r2[n - 8:, :]))
    return jnp.concatenate([r1[:n - 8, :], end1], axis=0), jnp.concatenate([r2[:n - 8, :], end2], axis=0)


def _conv(x, x1, x2, w):
    return w[2:3, :] * x + w[1:2, :] * x1 + w[0:1, :] * x2


def _halo_specs(ts, tc, col, n_time_blocks):
    r8 = ts // 8
    main = pl.BlockSpec((ts, tc), lambda j, i: (i, j + col))
    prev = pl.BlockSpec((8, tc), lambda j, i: (jnp.maximum(i * r8 - 1, 0), j + col))
    nxt = pl.BlockSpec((8, tc), lambda j, i: (jnp.minimum((i + 1) * r8, n_time_blocks * r8 - 1), j + col))
    return main, prev, nxt


def _silu_parts(g):
    sig = 1.0 / (1.0 + jnp.exp(-g))
    return sig, g * sig


def _ffn_act_fwd(up, cw, name, ts=256, tc=1408):
    s = up.shape[0]
    ts, tc = _tile(s, ts, 8), _tile(D_FF, tc)
    nc, nt = D_FF // tc, s // ts

    def body(g_ref, gp_ref, v_ref, vp_ref, wg_ref, wv_ref, o_ref):
        first = pl.program_id(1) == 0

        def conv(x_ref, p_ref, w_ref):
            x = x_ref[...]
            prev = jnp.where(first, 0.0, p_ref[...])
            x1, x2 = _shift_down(x, prev)
            return _conv(x, x1, x2, w_ref[...])

        ug = conv(g_ref, gp_ref, wg_ref)
        uv = conv(v_ref, vp_ref, wv_ref)
        _, silu = _silu_parts(ug)
        o_ref[...] = (silu * uv).astype(BF16)

    g_main, g_prev, _ = _halo_specs(ts, tc, 0, nt)
    v_main, v_prev, _ = _halo_specs(ts, tc, nc, nt)
    return pl.pallas_call(
        body, name=name, grid=(nc, nt),
        in_specs=[g_main, g_prev, v_main, v_prev,
                  pl.BlockSpec((3, tc), lambda j, i: (0, j)), pl.BlockSpec((3, tc), lambda j, i: (0, j + nc))],
        out_specs=pl.BlockSpec((ts, tc), lambda j, i: (i, j)),
        out_shape=jax.ShapeDtypeStruct((s, D_FF), BF16),
        compiler_params=_params("parallel", "parallel"),
    )(up, up, up, up, cw, cw)


def _ffn_act_bwd(up, cw, da, name, ts=256, tc=1408):
    s = up.shape[0]
    ts, tc = _tile(s, ts, 8), _tile(D_FF, tc)
    nc, nt = D_FF // tc, s // ts

    def body(g_ref, gp_ref, gn_ref, v_ref, vp_ref, vn_ref, da_ref, dan_ref, wg_ref, wv_ref,
             d_ref, dwg_ref, dwv_ref):
        i = pl.program_id(1)
        first, last = i == 0, i == nt - 1
        wg, wv = wg_ref[...], wv_ref[...]
        g, v = g_ref[...], v_ref[...]
        g1, g2 = _shift_down(g, jnp.where(first, 0.0, gp_ref[...]))
        v1, v2 = _shift_down(v, jnp.where(first, 0.0, vp_ref[...]))

        def d_u(ug, uv, da_v):
            sig, silu = _silu_parts(ug)
            return da_v * uv * (sig * (1.0 + ug * (1.0 - sig))), da_v * silu

        dug, duv = d_u(_conv(g, g1, g2, wg), _conv(v, v1, v2, wv), da_ref[...])
        gn, vn = gn_ref[...], vn_ref[...]
        gn1, gn2 = _shift_down(gn, g[ts - 8:, :])
        vn1, vn2 = _shift_down(vn, v[ts - 8:, :])
        dugn, duvn = d_u(_conv(gn, gn1, gn2, wg), _conv(vn, vn1, vn2, wv), dan_ref[...])
        dugn = jnp.where(last, 0.0, dugn)
        duvn = jnp.where(last, 0.0, duvn)

        def finish(du, dun, x, x1, x2, w, dx_ref, dw_ref):
            d1, d2 = _shift_up(du, dun)
            dx_ref[...] = (w[2:3, :] * du + w[1:2, :] * d1 + w[0:1, :] * d2).astype(BF16)

            @pl.when(first)
            def _():
                dw_ref[...] = jnp.zeros_like(dw_ref)

            dw_ref[0:1, :] += jnp.sum(du * x2, axis=0, keepdims=True)
            dw_ref[1:2, :] += jnp.sum(du * x1, axis=0, keepdims=True)
            dw_ref[2:3, :] += jnp.sum(du * x, axis=0, keepdims=True)

        finish(dug, dugn, g, g1, g2, wg, d_ref.at[0], dwg_ref)
        finish(duv, duvn, v, v1, v2, wv, d_ref.at[1], dwv_ref)

    g_specs = _halo_specs(ts, tc, 0, nt)
    v_specs = _halo_specs(ts, tc, nc, nt)
    da_main, _, da_next = _halo_specs(ts, tc, 0, nt)
    taps = pl.BlockSpec((8, tc), lambda j, i: (0, j))
    halves = pl.BlockSpec((2, ts, tc), lambda j, i: (0, i, j))
    d, dwg, dwv = pl.pallas_call(
        body, name=name, grid=(nc, nt),
        in_specs=[*g_specs, *v_specs, da_main, da_next,
                  pl.BlockSpec((3, tc), lambda j, i: (0, j)), pl.BlockSpec((3, tc), lambda j, i: (0, j + nc))],
        out_specs=[halves, taps, taps],
        out_shape=[jax.ShapeDtypeStruct((2, s, D_FF), BF16),
                   jax.ShapeDtypeStruct((8, D_FF), F32), jax.ShapeDtypeStruct((8, D_FF), F32)],
        compiler_params=_params("parallel", "arbitrary"),
    )(up, up, up, up, up, up, da, da, cw, cw)
    return d, dwg[:3], dwv[:3]


def _sconv_fwd(proj, ck, name, ts=256, tc=512):
    s = proj.shape[0]
    w = D_MODEL
    ts, tc = _tile(s, ts, 8), _tile(w, tc)
    nc, nt = w // tc, s // ts

    def body(b_ref, c_ref, cp_ref, u_ref, up_ref, w_ref, o_ref):
        first = pl.program_id(1) == 0
        cu = c_ref[...] * u_ref[...]
        cup = jnp.where(first, 0.0, cp_ref[...] * up_ref[...])
        x1, x2 = _shift_down(cu, cup)
        o_ref[...] = (b_ref[...] * _conv(cu, x1, x2, w_ref[...])).astype(BF16)

    b_main, _, _ = _halo_specs(ts, tc, 0, nt)
    c_main, c_prev, _ = _halo_specs(ts, tc, nc, nt)
    u_main, u_prev, _ = _halo_specs(ts, tc, 2 * nc, nt)
    return pl.pallas_call(
        body, name=name, grid=(nc, nt),
        in_specs=[b_main, c_main, c_prev, u_main, u_prev, pl.BlockSpec((3, tc), lambda j, i: (0, j))],
        out_specs=pl.BlockSpec((ts, tc), lambda j, i: (i, j)),
        out_shape=jax.ShapeDtypeStruct((s, w), BF16),
        compiler_params=_params("parallel", "parallel"),
    )(proj, proj, proj, proj, proj, ck)


def _sconv_bwd(proj, ck, dy, name, ts=256, tc=512):
    s = proj.shape[0]
    w = D_MODEL
    ts, tc = _tile(s, ts, 8), _tile(w, tc)
    nc, nt = w // tc, s // ts

    def body(b_ref, bn_ref, c_ref, cp_ref, u_ref, up_ref, dy_ref, dyn_ref, w_ref,
             d_ref, dw_ref):
        i = pl.program_id(1)
        first, last = i == 0, i == nt - 1
        wv = w_ref[...]
        b, c, u, dy_v = b_ref[...], c_ref[...], u_ref[...], dy_ref[...]
        cu = c * u
        cup = jnp.where(first, 0.0, cp_ref[...] * up_ref[...])
        x1, x2 = _shift_down(cu, cup)
        d_ref[0] = (dy_v * _conv(cu, x1, x2, wv)).astype(BF16)
        dcv = dy_v * b
        dcvn = jnp.where(last, 0.0, dyn_ref[...] * bn_ref[...])
        d1, d2 = _shift_up(dcv, dcvn)
        dcu = wv[2:3, :] * dcv + wv[1:2, :] * d1 + wv[0:1, :] * d2
        d_ref[1] = (dcu * u).astype(BF16)
        d_ref[2] = (dcu * c).astype(BF16)

        @pl.when(first)
        def _():
            dw_ref[...] = jnp.zeros_like(dw_ref)

        dw_ref[0:1, :] += jnp.sum(dcv * x2, axis=0, keepdims=True)
        dw_ref[1:2, :] += jnp.sum(dcv * x1, axis=0, keepdims=True)
        dw_ref[2:3, :] += jnp.sum(dcv * cu, axis=0, keepdims=True)

    b_main, _, b_next = _halo_specs(ts, tc, 0, nt)
    c_main, c_prev, _ = _halo_specs(ts, tc, nc, nt)
    u_main, u_prev, _ = _halo_specs(ts, tc, 2 * nc, nt)
    dy_main, _, dy_next = _halo_specs(ts, tc, 0, nt)
    d, dw = pl.pallas_call(
        body, name=name, grid=(nc, nt),
        in_specs=[b_main, b_next, c_main, c_prev, u_main, u_prev, dy_main, dy_next,
                  pl.BlockSpec((3, tc), lambda j, i: (0, j))],
        out_specs=[pl.BlockSpec((3, ts, tc), lambda j, i: (0, i, j)), pl.BlockSpec((8, tc), lambda j, i: (0, j))],
        out_shape=[jax.ShapeDtypeStruct((3, s, w), BF16), jax.ShapeDtypeStruct((8, w), F32)],
        compiler_params=_params("parallel", "arbitrary"),
    )(proj, proj, proj, proj, proj, proj, dy, dy, ck)
    return d, dw[:3]


def _low_lanes(shape):
    return lax.broadcasted_iota(jnp.int32, shape, 1) < HEAD_DIM


def _top_rows(shape):
    return lax.broadcasted_iota(jnp.int32, shape, 0) < HEAD_DIM


def _norm_pair(x):
    r = lax.rsqrt(_mean_pair(x * x) + EPS)
    return x * r, r


def _mean_pair(x):
    same_head = _tri(LANES, lambda a, b: a // HEAD_DIM == b // HEAD_DIM)
    return _split_dot(x, same_head, 3) * (1.0 / HEAD_DIM)


def _qkv_prep(proj, gq, gk, name, ts=512):
    s = proj.shape[0]
    ts = _tile(s, ts)
    npair = MIX // LANES
    scale = HEAD_DIM ** -0.5

    def body(q_ref, k_ref, v_ref, gq_ref, gk_ref, qo_ref, ko_ref, vo_ref, qt_ref, kt_ref, vt_ref):
        qn, _ = _norm_pair(q_ref[...])
        kn, _ = _norm_pair(k_ref[...])
        q = qn * gq_ref[0] * scale
        k = kn * gk_ref[0]
        v = v_ref[...]
        qo_ref[...] = q.astype(BF16)
        ko_ref[...] = k.astype(BF16)
        vo_ref[...] = v.astype(BF16)
        qt_ref[...] = q.T.astype(BF16)
        kt_ref[...] = k.T.astype(BF16)
        vt_ref[...] = v.T.astype(BF16)

    gain = pl.BlockSpec((1, 1, LANES), lambda i, p: (p, 0, 0))
    tile = pl.BlockSpec((ts, LANES), lambda i, p: (i, p))
    tile_t = pl.BlockSpec((LANES, ts), lambda i, p: (p, i))
    out = jax.ShapeDtypeStruct((s, MIX), BF16)
    out_t = jax.ShapeDtypeStruct((MIX, s), BF16)
    return pl.pallas_call(
        body, name=name, grid=(s // ts, npair),
        in_specs=[tile, pl.BlockSpec((ts, LANES), lambda i, p: (i, p + npair)),
                  pl.BlockSpec((ts, LANES), lambda i, p: (i, p + 2 * npair)), gain, gain],
        out_specs=[tile, tile, tile, tile_t, tile_t, tile_t], out_shape=[out, out, out, out_t, out_t, out_t],
        compiler_params=_params("parallel", "parallel"),
    )(proj, proj, proj, gq, gk)


def _qkv_prep_bwd(proj, gq, gk, dqs, dks, dvs, name, ts=512):
    s = proj.shape[0]
    ts = _tile(s, ts, 8)
    npair = MIX // LANES
    half = npair // 2
    scale = HEAD_DIM ** -0.5

    def body(q_ref, k_ref, gq_ref, gk_ref, dqf_ref, dqs_ref, dkf_ref, dks_ref, dvf_ref, dvs_ref,
             dq_ref, dk_ref, dv_ref, dgq_ref, dgk_ref):
        p, i = pl.program_id(0), pl.program_id(1)
        fox = p < half

        def one(x_ref, g_ref, df_ref, ds_ref, dx_ref, dg_ref, mult):
            dn = jnp.where(fox, df_ref[...], ds_ref[...]) * mult
            xh, r = _norm_pair(x_ref[...])

            @pl.when(i == 0)
            def _():
                dg_ref[...] = jnp.zeros_like(dg_ref)

            dg_ref[0, 0:1, :] += jnp.sum(dn * xh, axis=0, keepdims=True)
            dxh = dn * g_ref[0]
            dx_ref[...] = (r * (dxh - xh * _mean_pair(dxh * xh))).astype(BF16)

        one(q_ref, gq_ref, dqf_ref, dqs_ref, dq_ref, dgq_ref, scale)
        one(k_ref, gk_ref, dkf_ref, dks_ref, dk_ref, dgk_ref, 1.0)
        dv_ref[...] = jnp.where(fox, dvf_ref[...], dvs_ref[...]).astype(BF16)

    gain = pl.BlockSpec((1, 1, LANES), lambda p, i: (p, 0, 0))
    tile = pl.BlockSpec((ts, LANES), lambda p, i: (i, p))
    fpart = pl.BlockSpec((ts, LANES), lambda p, i: (i, jnp.minimum(p, half - 1)))
    spart = pl.BlockSpec((ts, LANES), lambda p, i: (i, jnp.maximum(p - half, 0)))
    dgain = pl.BlockSpec((1, 8, LANES), lambda p, i: (p, 0, 0))
    out = jax.ShapeDtypeStruct((s, MIX), BF16)
    gshape = jax.ShapeDtypeStruct((npair, 8, LANES), F32)
    dq, dk, dv, dgq, dgk = pl.pallas_call(
        body, name=name, grid=(npair, s // ts),
        in_specs=[tile, pl.BlockSpec((ts, LANES), lambda p, i: (i, p + npair)), gain, gain,
                  fpart, spart, fpart, spart, fpart, spart],
        out_specs=[tile, tile, tile, dgain, dgain], out_shape=[out, out, out, gshape, gshape],
        compiler_params=_params("parallel", "arbitrary"),
    )(proj, proj, gq, gk, dqs[0], dqs[1], dks[0], dks[1], dvs[0], dvs[1])
    return dq, dk, dv, dgq[:, 0, :], dgk[:, 0, :]


def _tri(n, rel):
    a = lax.broadcasted_iota(jnp.int32, (n, n), 0)
    b = lax.broadcasted_iota(jnp.int32, (n, n), 1)
    return rel(a, b).astype(BF16)


def _fgate_fwd(logit, bias, name):
    nh, r, _ = logit.shape

    def body(x_ref, b_ref, o_ref):
        within = _tri(LANES, lambda a, b: a <= b)
        before = _tri(r, lambda a, b: b < a)
        for hh in range(nh):
            x = x_ref[hh] + b_ref[hh]
            lf = jnp.minimum(x, 0.0) - jnp.log1p(jnp.exp(-jnp.abs(x)))
            c = _split_dot(lf, within, 3)
            tot = jnp.broadcast_to(c[:, LANES - 1:LANES], (r, LANES))
            o_ref[hh] = c + _split_dot_left(before, tot, 3)

    return pl.pallas_call(
        body, name=name, out_shape=jax.ShapeDtypeStruct((nh, r, LANES), F32),
        in_specs=[pl.BlockSpec(memory_space=pltpu.VMEM), pl.BlockSpec(memory_space=pltpu.SMEM)],
        out_specs=pl.BlockSpec(memory_space=pltpu.VMEM),
    )(logit, bias)


def _fgate_bwd(logit, bias, dcum, name):
    nh, r, _ = logit.shape

    def body(x_ref, b_ref, d_ref, dx_ref, db_ref):
        within = _tri(LANES, lambda a, b: a >= b)
        after = _tri(r, lambda a, b: b > a)
        for hh in range(nh):
            x = x_ref[hh] + b_ref[hh]
            d = d_ref[hh]
            c = _split_dot(d, within, 3)
            tot = jnp.broadcast_to(c[:, 0:1], (r, LANES))
            dlf = c + _split_dot_left(after, tot, 3)
            dx = dlf * (1.0 / (1.0 + jnp.exp(x)))
            dx_ref[hh] = dx
            db_ref[hh:hh + 1, :] = jnp.broadcast_to(jnp.sum(dx, keepdims=True).reshape(1, 1), (1, LANES))

    return pl.pallas_call(
        body, name=name,
        out_shape=[jax.ShapeDtypeStruct((nh, r, LANES), F32), jax.ShapeDtypeStruct((nh, LANES), F32)],
        in_specs=[pl.BlockSpec(memory_space=pltpu.VMEM), pl.BlockSpec(memory_space=pltpu.SMEM),
                  pl.BlockSpec(memory_space=pltpu.VMEM)],
        out_specs=[pl.BlockSpec(memory_space=pltpu.VMEM), pl.BlockSpec(memory_space=pltpu.VMEM)],
    )(logit, bias, dcum)


def _pair_masks(x):
    lo = _low_lanes(x.shape)
    zero = jnp.zeros_like(x)
    return jnp.where(lo, x, zero), jnp.where(lo, zero, x)


def _pair_masks_t(x):
    top = _top_rows(x.shape)
    zero = jnp.zeros_like(x)
    return jnp.where(top, x, zero), jnp.where(top, zero, x)


def _stack_heads(x):
    return jnp.concatenate(_pair_masks(x), axis=0)


def _stack_heads_t(x):
    return jnp.concatenate(_pair_masks_t(x), axis=1)


def _pair_colsum_t(x):
    top = _top_rows(x.shape)
    return (jnp.sum(jnp.where(top, x, 0.0), axis=0, keepdims=True),
            jnp.sum(jnp.where(top, 0.0, x), axis=0, keepdims=True))


def _key_query_iotas(t):
    return lax.broadcasted_iota(jnp.int32, (t, t), 0), lax.broadcasted_iota(jnp.int32, (t, t), 1)


def _walk_blocks(i, step, descending, group=2):
    full = i // group
    left = i - full * group

    def run(first, count):
        sign = -1 if descending else 1
        step([(first + sign * n, False) for n in range(count)])

    def leftovers():
        start = (left - 1) if descending else full * group
        sign = -1 if descending else 1
        if group == 4:
            @pl.when(left >= 2)
            def _():
                run(start, 2)

            @pl.when(left % 2 == 1)
            def _():
                run(0 if descending else i - 1, 1)
        else:
            @pl.when(left == 1)
            def _():
                run(start, 1)

    def loop(g, carry):
        run((i - 1 - group * g) if descending else group * g, group)
        return carry

    if descending:
        step([(i, True)])
        lax.fori_loop(0, full, loop, 0)
        leftovers()
        return
    total = i + 1
    groups = total // group
    rest = total - groups * group
    lax.fori_loop(0, jnp.where(rest == 0, groups - 1, groups), loop, 0)

    def closing(count):
        step([(i - count + 1 + n, n == count - 1) for n in range(count)])

    pl.when(rest == 0)(lambda: closing(group))
    if group == 4:
        pl.when(rest == 3)(lambda: run(i - 2, 2))
        pl.when(rest == 2)(lambda: closing(2))
        pl.when((rest == 1) | (rest == 3))(lambda: closing(1))
    else:
        pl.when(rest == 1)(lambda: closing(1))


def _attn_specs(s, tq, pair0):
    q_nat = pl.BlockSpec((tq, LANES), lambda p, i: (i, p + pair0))
    q_t = pl.BlockSpec((LANES, tq), lambda p, i: (p + pair0, i))
    k_nat = pl.BlockSpec((s, LANES), lambda p, i: (0, p + pair0))
    k_t = pl.BlockSpec((LANES, s), lambda p, i: (p + pair0, 0))
    return q_nat, q_t, k_nat, k_t


def _fox_fwd(qt, kh, vt, frow, fcol, name, tq=256):
    s = kh.shape[0]
    tq = _tile(s, tq)
    nq, half = s // tq, MIX // LANES // 2

    def body(qt_ref, k_ref, vt_ref, fr_ref, fc_ref, o_ref, lse_ref, m_s, l_s, acc_s):
        i = pl.program_id(1)
        qt_v = qt_ref[...]
        ft = fr_ref[0]
        key, qry = _key_query_iotas(tq)
        causal = key <= qry
        m_s[...] = jnp.full(m_s.shape, NEG, F32)
        l_s[...] = jnp.zeros_like(l_s)
        acc_s[...] = jnp.zeros_like(acc_s)

        top = _top_rows((LANES, tq))

        def step(blocks):
            rows = [pl.ds(pl.multiple_of(j * tq, tq), tq) for j, _ in blocks]
            zs = [_dot(_stack_heads(k_ref[r, :]), qt_v) for r in rows]
            m_cur, l_cur = [m_s[0], m_s[1]], [l_s[0], l_s[1]]
            acc = acc_s[...]
            for b, (_, masked) in enumerate(blocks):
                fk = fc_ref[0, rows[b], :]
                prs, alphas = [], []
                for hh in range(2):
                    sc = zs[b][hh * tq:(hh + 1) * tq] + (ft[hh:hh + 1, :] - fk[:, hh:hh + 1])
                    if masked:
                        sc = jnp.where(causal, sc, NEG)
                    m_new = jnp.maximum(m_cur[hh], jnp.max(sc, axis=0, keepdims=True))
                    alpha = jnp.exp(m_cur[hh] - m_new)
                    pr = jnp.exp(sc - m_new)
                    l_cur[hh] = alpha * l_cur[hh] + jnp.sum(pr, axis=0, keepdims=True)
                    m_cur[hh] = m_new
                    prs.append(pr.astype(BF16))
                    alphas.append(alpha)
                pv = _dot(_stack_heads_t(vt_ref[:, rows[b]]), jnp.concatenate(prs, axis=0))
                acc = jnp.where(top, alphas[0], alphas[1]) * acc + pv
            acc_s[...] = acc
            for hh in range(2):
                m_s[hh] = m_cur[hh]
                l_s[hh] = l_cur[hh]

        _walk_blocks(i, step, descending=False, group=4)
        o_ref[...] = (acc_s[...] / jnp.where(top, l_s[0], l_s[1])).T
        lse_ref[0, 0:1, :] = m_s[0] + jnp.log(l_s[0])
        lse_ref[0, 1:2, :] = m_s[1] + jnp.log(l_s[1])

    _, q_t, k_nat, k_t = _attn_specs(s, tq, 0)
    qstat = pl.BlockSpec((1, 2, tq), lambda p, i: (p, 0, i))
    return pl.pallas_call(
        body, name=name, grid=(half, nq),
        in_specs=[q_t, k_nat, k_t, qstat, pl.BlockSpec((1, s, 2), lambda p, i: (p, 0, 0))],
        out_specs=[pl.BlockSpec((tq, LANES), lambda p, i: (i, p)), qstat],
        out_shape=[jax.ShapeDtypeStruct((s, MIX // 2), F32), jax.ShapeDtypeStruct((half, 2, s), F32)],
        scratch_shapes=[pltpu.VMEM((2, 1, tq), F32), pltpu.VMEM((2, 1, tq), F32), pltpu.VMEM((LANES, tq), F32)],
        compiler_params=_params("parallel", "arbitrary"),
    )(qt, kh, vt, frow, fcol)


def _fox_bwd(qh, qt, kh, kt, vb, frow, fcol, lse, o, do, name, tq=256):
    s = kh.shape[0]
    tq = _tile(s, tq)
    nq, half = s // tq, MIX // LANES // 2

    def body(q_ref, qt_ref, k_ref, kt_ref, v_ref, fr_ref, fc_ref, lse_ref, o_ref, do_ref,
             dq_ref, dk_ref, dv_ref, dfk_ref, dfq_ref, dq_s, rs_s):
        i = pl.program_id(1)

        @pl.when(i == 0)
        def _():
            dk_ref[...] = jnp.zeros_like(dk_ref)
            dv_ref[...] = jnp.zeros_like(dv_ref)
            dfk_ref[...] = jnp.zeros_like(dfk_ref)

        q2 = _stack_heads(q_ref[...])
        qt_v = qt_ref[...]
        do_v = do_ref[...]
        do2 = _stack_heads(do_v.astype(BF16))
        dot_v = do_v.T.astype(BF16)
        dsum = _pair_colsum_t((do_v * o_ref[...]).T)
        ft, ls = fr_ref[0], lse_ref[0]
        key, qry = _key_query_iotas(tq)
        causal = key <= qry
        lane = lax.broadcasted_iota(jnp.int32, (2 * tq, LANES), 0) // tq
        pick2 = (lax.broadcasted_iota(jnp.int32, (2 * tq, LANES), 1) == lane).astype(BF16)
        q2_pick = jnp.concatenate([q2, pick2], axis=1)
        dq_s[...] = jnp.zeros_like(dq_s)
        rs_s[...] = jnp.zeros_like(rs_s)

        def step(blocks):
            rows = [pl.ds(pl.multiple_of(j * tq, tq), tq) for j, _ in blocks]
            zs = [_dot(_stack_heads(k_ref[r, :]), qt_v) for r in rows]
            dps = [_dot(_stack_heads(v_ref[r, :]), dot_v) for r in rows]
            rs = [rs_s[0], rs_s[1]]
            dq = None
            for b, (_, masked) in enumerate(blocks):
                fk = fc_ref[0, rows[b], :]
                prs, dss = [], []
                for hh in range(2):
                    blk = slice(hh * tq, (hh + 1) * tq)
                    sc = zs[b][blk] + (ft[hh:hh + 1, :] - fk[:, hh:hh + 1])
                    pr = jnp.exp(sc - ls[hh:hh + 1, :])
                    if masked:
                        pr = jnp.where(causal, pr, 0.0)
                    dsb = (pr * (dps[b][blk] - dsum[hh])).astype(BF16)
                    rs[hh] = rs[hh] + jnp.sum(dsb.astype(F32), axis=0, keepdims=True)
                    prs.append(pr.astype(BF16))
                    dss.append(dsb)
                dv_ref[rows[b], :] += _dot(jnp.concatenate(prs, axis=1), do2)
                both = _dot(jnp.concatenate(dss, axis=1), q2_pick)
                dk_ref[rows[b], :] += both[:, :LANES]
                dfk_ref[0, rows[b], :] -= both[:, LANES:]
                term = _dot(_stack_heads_t(kt_ref[:, rows[b]]), jnp.concatenate(dss, axis=0))
                dq = term if dq is None else dq + term
            rs_s[0], rs_s[1] = rs
            dq_s[...] += dq

        _walk_blocks(i, step, descending=False, group=4)
        dq_ref[...] = dq_s[...].T
        dfq_ref[0, 0:1, :] = rs_s[0]
        dfq_ref[0, 1:2, :] = rs_s[1]

    q_nat, q_t, k_nat, k_t = _attn_specs(s, tq, 0)
    qstat = pl.BlockSpec((1, 2, tq), lambda p, i: (p, 0, i))
    otile = pl.BlockSpec((tq, LANES), lambda p, i: (i, p))
    oresident = pl.BlockSpec((s, LANES), lambda p, i: (0, p))
    out = jax.ShapeDtypeStruct((s, MIX // 2), F32)
    return pl.pallas_call(
        body, name=name, grid=(half, nq),
        in_specs=[q_nat, q_t, k_nat, k_t, k_nat, qstat, pl.BlockSpec((1, s, 2), lambda p, i: (p, 0, 0)), qstat,
                  otile, q_nat],
        out_specs=[otile, oresident, oresident, pl.BlockSpec((1, s, LANES), lambda p, i: (p, 0, 0)), qstat],
        out_shape=[out, out, out, jax.ShapeDtypeStruct((half, s, LANES), F32),
                   jax.ShapeDtypeStruct((half, 2, s), F32)],
        scratch_shapes=[pltpu.VMEM((LANES, tq), F32), pltpu.VMEM((2, 1, tq), F32)],
        compiler_params=_params("parallel", "arbitrary"),
    )(qh, qt, kh, kt, vb, frow, fcol, lse, o, do)


def _log_sig_pair(z):
    zc = jnp.maximum(z, -80.0)
    lb = -jnp.log(1.0 + jnp.exp(-zc))
    return lb, lb - zc


def _sb_fwd(qt, kh, vt, name, tq=256, gather=()):
    s = kh.shape[0]
    tq = _tile(s, tq)
    nq, half = s // tq, MIX // LANES // 2
    ng = len(gather)

    def body(*refs):
        qt_ref, k_ref, vt_ref = refs[:3]
        o_ref, tot_ref = refs[3 + ng:5 + ng]
        c_s, acc_s = refs[5 + 2 * ng:7 + 2 * ng]
        p, i = pl.program_id(0), pl.program_id(1)
        if ng:
            start, relay, finish = _gather_stages(refs[3:3 + ng], refs[5 + ng:5 + 2 * ng], *refs[7 + 2 * ng:])
            pl.when((p == 0) & (i == 0))(start)
            pl.when((p == half - 1) & (i == 0))(relay)
        qt_v = qt_ref[...]
        key, qry = _key_query_iotas(tq)
        strict = key < qry
        later = _tri(tq, lambda a, b: b > a)
        c_s[...] = jnp.zeros_like(c_s)
        acc_s[...] = jnp.zeros_like(acc_s)

        def step(blocks):
            rows = [pl.ds(pl.multiple_of(j * tq, tq), tq) for j, _ in blocks]
            zs = [_dot(_stack_heads(k_ref[r, :]), qt_v) for r in rows]
            lbs, loms, afters = [], [], []
            for b, (_, masked) in enumerate(blocks):
                for hh in range(2):
                    lb, lom = _log_sig_pair(zs[b][hh * tq:(hh + 1) * tq])
                    if masked:
                        lom = jnp.where(strict, lom, 0.0)
                    lbs.append(lb)
                    loms.append(lom)
                afters.append(_split_dot_left(later, jnp.concatenate(loms[2 * b:2 * b + 2], axis=1), 2))
            carry = [c_s[0], c_s[1]]
            pv = None
            for b, (_, masked) in enumerate(blocks):
                ws = []
                for hh in range(2):
                    n = 2 * b + hh
                    w = jnp.exp(lbs[n] + afters[b][:, hh * tq:(hh + 1) * tq] + carry[hh])
                    if masked:
                        w = jnp.where(strict, w, 0.0)
                    ws.append(w.astype(BF16))
                    carry[hh] = carry[hh] + jnp.sum(loms[n], axis=0, keepdims=True)
                term = _dot(_stack_heads_t(vt_ref[:, rows[b]]), jnp.concatenate(ws, axis=0))
                pv = term if pv is None else pv + term
            c_s[0], c_s[1] = carry
            acc_s[...] += pv

        _walk_blocks(i, step, descending=True, group=4)
        o_ref[...] = acc_s[...].T
        tot_ref[0, 0:1, :] = c_s[0]
        tot_ref[0, 1:2, :] = c_s[1]
        if ng:
            pl.when((p == half - 1) & (i == nq - 1))(finish)

    _, q_t, k_nat, k_t = _attn_specs(s, tq, half)
    qstat = pl.BlockSpec((1, 2, tq), lambda p, i: (p, 0, i))
    return pl.pallas_call(
        body, name=name, grid=(half, nq),
        in_specs=[q_t, k_nat, k_t] + [ANY] * ng,
        out_specs=[pl.BlockSpec((tq, LANES), lambda p, i: (i, p)), qstat] + [ANY] * ng,
        out_shape=[jax.ShapeDtypeStruct((s, MIX // 2), F32), jax.ShapeDtypeStruct((half, 2, s), F32)]
        + [jax.ShapeDtypeStruct((N_DEV,) + x.shape, x.dtype) for x in gather],
        scratch_shapes=[pltpu.VMEM((2, 1, tq), F32), pltpu.VMEM((LANES, tq), F32)]
        + (_gather_scratch(ng) if ng else []),
        compiler_params=_params("arbitrary", "arbitrary") if ng else _params("parallel", "arbitrary"),
    )(qt, kh, vt, *gather)


def _sb_bwd(qh, qt, kh, kt, vb, tot, do, name, tq=256, exchange=()):
    s = kh.shape[0]
    tq = _tile(s, tq)
    nq, half = s // tq, MIX // LANES // 2
    nx = len(exchange)

    def body(*refs):
        q_ref, qt_ref, k_ref, kt_ref, v_ref, tot_ref, do_ref = refs[:7]
        dq_ref, dk_ref, dv_ref = refs[7 + nx:10 + nx]
        rem_s, pg_s, dq_s = refs[10 + 2 * nx:13 + 2 * nx]
        p, i = pl.program_id(0), pl.program_id(1)
        if nx:
            start, finish = _chip_exchange_stages(refs[7:7 + nx], refs[10 + nx:10 + 2 * nx], *refs[13 + 2 * nx:])
            pl.when((p == 0) & (i == 0))(start)

        @pl.when(i == 0)
        def _():
            dk_ref[...] = jnp.zeros_like(dk_ref)
            dv_ref[...] = jnp.zeros_like(dv_ref)

        q2 = _stack_heads(q_ref[...])
        qt_v = qt_ref[...]
        do_v = do_ref[...]
        do2 = _stack_heads(do_v.astype(BF16))
        dot_v = do_v.T.astype(BF16)
        key, qry = _key_query_iotas(tq)
        strict = key < qry
        upto = _tri(tq, lambda a, b: b <= a)
        before = _tri(tq, lambda a, b: b < a)
        tv = tot_ref[0]
        rem_s[0] = tv[0:1, :]
        rem_s[1] = tv[1:2, :]
        pg_s[...] = jnp.zeros_like(pg_s)
        dq_s[...] = jnp.zeros_like(dq_s)

        def step(blocks):
            nb = len(blocks)
            rows = [pl.ds(pl.multiple_of(j * tq, tq), tq) for j, _ in blocks]
            zs = [_dot(_stack_heads(k_ref[r, :]), qt_v) for r in rows]
            dws = [_dot(_stack_heads(v_ref[r, :]), dot_v) for r in rows]
            lbs, loms, prefixes = [], [], []
            for b, (_, masked) in enumerate(blocks):
                for hh in range(2):
                    lb, lom = _log_sig_pair(zs[b][hh * tq:(hh + 1) * tq])
                    if masked:
                        lom = jnp.where(strict, lom, 0.0)
                    lbs.append(lb)
                    loms.append(lom)
                prefixes.append(_split_dot_left(upto, jnp.concatenate(loms[2 * b:2 * b + 2], axis=1), 2))
            rem = [rem_s[0], rem_s[1]]
            ws, gs, gpres = [], [], []
            for b, (_, masked) in enumerate(blocks):
                for hh in range(2):
                    n = 2 * b + hh
                    blk = slice(hh * tq, (hh + 1) * tq)
                    w = jnp.exp(lbs[n] + (rem[hh] - prefixes[b][:, blk]))
                    if masked:
                        w = jnp.where(strict, w, 0.0)
                    gs.append(dws[b][blk] * w)
                    ws.append(w.astype(BF16))
                    rem[hh] = rem[hh] - jnp.sum(loms[n], axis=0, keepdims=True)
                gpres.append(_dot(before, jnp.concatenate(gs[2 * b:2 * b + 2], axis=1).astype(BF16)))
                dv_ref[rows[b], :] += _dot(jnp.concatenate(ws[2 * b:2 * b + 2], axis=1), do2)
            rem_s[0], rem_s[1] = rem
            pg = [pg_s[0], pg_s[1]]
            dq = None
            for b, (_, masked) in enumerate(blocks):
                dzs = []
                for hh in range(2):
                    n = 2 * b + hh
                    g = gs[n]
                    dz = g - jnp.exp(lbs[n]) * (g + (pg[hh] + gpres[b][:, hh * tq:(hh + 1) * tq]))
                    if masked:
                        dz = jnp.where(strict, dz, 0.0)
                    dzs.append(dz.astype(BF16))
                    pg[hh] = pg[hh] + jnp.sum(g, axis=0, keepdims=True)
                dk_ref[rows[b], :] += _dot(jnp.concatenate(dzs, axis=1), q2)
                term = _dot(_stack_heads_t(kt_ref[:, rows[b]]), jnp.concatenate(dzs, axis=0))
                dq = term if dq is None else dq + term
            pg_s[0], pg_s[1] = pg
            dq_s[...] += dq

        _walk_blocks(i, step, descending=False, group=2)
        dq_ref[...] = dq_s[...].T
        if nx:
            pl.when((p == half - 1) & (i == nq - 1))(finish)

    q_nat, q_t, k_nat, k_t = _attn_specs(s, tq, half)
    qstat = pl.BlockSpec((1, 2, tq), lambda p, i: (p, 0, i))
    otile = pl.BlockSpec((tq, LANES), lambda p, i: (i, p))
    oresident = pl.BlockSpec((s, LANES), lambda p, i: (0, p))
    out = jax.ShapeDtypeStruct((s, MIX // 2), F32)
    return pl.pallas_call(
        body, name=name, grid=(half, nq),
        in_specs=[q_nat, q_t, k_nat, k_t, k_nat, qstat, q_nat] + [ANY] * nx,
        out_specs=[otile, oresident, oresident] + [ANY] * nx,
        out_shape=[out, out, out] + [jax.ShapeDtypeStruct(x.shape, x.dtype) for x in exchange],
        scratch_shapes=[pltpu.VMEM((2, 1, tq), F32), pltpu.VMEM((2, 1, tq), F32), pltpu.VMEM((LANES, tq), F32)]
        + (_chip_exchange_scratch(nx) if nx else []),
        compiler_params=_params("arbitrary", "arbitrary") if nx else _params("parallel", "arbitrary"),
    )(qh, qt, kh, kt, vb, tot, do, *exchange)


def _loss_head(y, target, name, ts=512):
    s, d = y.shape
    ts = _tile(s, ts, 8)
    nt = s // ts

    def body(y_ref, t_ref, dy_ref, l_ref, acc):
        i = pl.program_id(0)
        err = y_ref[...] - t_ref[...]
        dy_ref[...] = err * (1.0 / d)

        @pl.when(i == 0)
        def _():
            acc[...] = jnp.zeros_like(acc)

        acc[...] += jnp.sum(err * err, axis=0, keepdims=True)

        @pl.when(i == nt - 1)
        def _():
            tot = jnp.sum(acc[...], keepdims=True).reshape(1, 1) * (0.5 / d)
            l_ref[...] = jnp.broadcast_to(tot, l_ref.shape)

    row = pl.BlockSpec((ts, d), lambda i: (i, 0))
    dy, l = pl.pallas_call(
        body, name=name, grid=(nt,), in_specs=[row, row],
        out_specs=[row, pl.BlockSpec((8, LANES), lambda i: (0, 0))],
        out_shape=[jax.ShapeDtypeStruct((s, d), F32), jax.ShapeDtypeStruct((8, LANES), F32)],
        scratch_shapes=[pltpu.VMEM((1, d), F32)],
        compiler_params=_params("arbitrary"),
    )(y, target)
    return l[0, 0], dy


def _coords():
    return lax.axis_index("x"), lax.axis_index("y"), lax.axis_index("c")


def _other_chips(xi, yi):
    return [(1 - xi, yi), (xi, 1 - yi), (1 - xi, 1 - yi)]


def _gather_stages(x_refs, out_refs, send_sems, recv_sems, local_sems):
    n = len(x_refs)
    xi, yi, ci = _coords()
    me, sibling = (xi, yi, ci), (xi, yi, 1 - ci)
    chips = _other_chips(xi, yi)

    def slot(a, px, py, pc):
        return out_refs[a].at[4 * px + 2 * py + pc]

    def copy(a, k, block, to, src=None):
        return pltpu.make_async_remote_copy(
            src_ref=slot(a, *block) if src is None else src, dst_ref=slot(a, *block),
            send_sem=send_sems.at[a, k], recv_sem=recv_sems.at[a, k], device_id=to, device_id_type=MESH)

    def own(a):
        return pltpu.make_async_copy(x_refs[a], slot(a, *me), local_sems.at[a])

    def first(a):
        return [copy(a, 0, me, sibling, src=x_refs[a])] + [
            copy(a, 1 + j, me, (*chip, ci), src=x_refs[a]) for j, chip in enumerate(chips)]

    def passed(a, j):
        return copy(a, 4 + j, (*chips[j], ci), sibling)

    def start():
        for a in range(n):
            own(a).start()
        for a in range(n):
            for cp in first(a):
                cp.start()

    def relay():
        for j, chip in enumerate(chips):
            for a in range(n):
                copy(a, 1 + j, (*chip, ci), me).wait_recv()
                passed(a, j).start()

    def finish():
        for a in range(n):
            copy(a, 0, sibling, me).wait_recv()
            for j, chip in enumerate(chips):
                copy(a, 4 + j, (*chip, 1 - ci), me).wait_recv()
        for a in range(n):
            for cp in first(a) + [passed(a, j) for j in range(len(chips))]:
                cp.wait_send()
            own(a).wait()

    return start, relay, finish


def _gather_scratch(n):
    return [pltpu.SemaphoreType.DMA((n, 7)), pltpu.SemaphoreType.DMA((n, 7)), pltpu.SemaphoreType.DMA((n,))]


def _all_gather(xs, name):
    n = len(xs)

    def body(*refs):
        start, relay, finish = _gather_stages(refs[:n], refs[n:2 * n], *refs[2 * n:])
        start()
        relay()
        finish()

    return pl.pallas_call(
        body, name=name, out_shape=[jax.ShapeDtypeStruct((N_DEV,) + x.shape, x.dtype) for x in xs],
        in_specs=[ANY] * n, out_specs=[ANY] * n, scratch_shapes=_gather_scratch(n),
    )(*xs)


def _sibling_exchange(gs, name):
    n = len(gs)

    def body(*refs):
        g_refs, recv_refs = refs[:n], refs[n:2 * n]
        send_sems, recv_sems = refs[2 * n:]
        xi, yi, ci = _coords()
        cps = [pltpu.make_async_remote_copy(
            src_ref=g_refs[a].at[2 * chip + (1 - ci)], dst_ref=recv_refs[a].at[chip],
            send_sem=send_sems.at[a, chip], recv_sem=recv_sems.at[a, chip],
            device_id=(xi, yi, 1 - ci), device_id_type=MESH) for a in range(n) for chip in range(N_CHIP)]
        for cp in cps:
            cp.start()
        for cp in cps:
            cp.wait()

    return pl.pallas_call(
        body, name=name, out_shape=[jax.ShapeDtypeStruct((N_CHIP,) + g.shape[1:], g.dtype) for g in gs],
        in_specs=[ANY] * n, out_specs=[ANY] * n,
        scratch_shapes=[pltpu.SemaphoreType.DMA((n, N_CHIP)), pltpu.SemaphoreType.DMA((n, N_CHIP))],
    )(*gs)


def _pair_add(g, recv, ids, name, tr=256):
    _, r, c = g.shape
    tr = _tile(r, tr, 16)

    def body(ids_ref, g_ref, r_ref, p_ref, own_ref):
        kk = pl.program_id(1)
        tot = g_ref[0].astype(F32) + r_ref[0].astype(F32)
        p_ref[0] = tot.astype(BF16)

        @pl.when(kk == ids_ref[1])
        def _():
            own_ref[...] = tot

    grid_spec = pltpu.PrefetchScalarGridSpec(
        num_scalar_prefetch=1, grid=(r // tr, N_CHIP),
        in_specs=[pl.BlockSpec((1, tr, c), lambda i, kk, ids: (2 * kk + ids[0], i, 0)),
                  pl.BlockSpec((1, tr, c), lambda i, kk, ids: (kk, i, 0))],
        out_specs=[pl.BlockSpec((1, tr, c), lambda i, kk, ids: (kk, i, 0)),
                   pl.BlockSpec((tr, c), lambda i, kk, ids: (i, 0))])
    return pl.pallas_call(
        body, name=name, grid_spec=grid_spec,
        out_shape=[jax.ShapeDtypeStruct((N_CHIP, r, c), BF16), jax.ShapeDtypeStruct((r, c), F32)],
        compiler_params=_params("parallel", "arbitrary"),
    )(ids, g, recv)


def _chip_exchange_stages(p_refs, recv_refs, send_sems, recv_sems):
    n = len(p_refs)
    xi, yi, ci = _coords()
    mine = 2 * xi + yi
    chips = _other_chips(xi, yi)

    def copy(a, k, cx, cy):
        return pltpu.make_async_remote_copy(
            src_ref=p_refs[a].at[2 * cx + cy], dst_ref=recv_refs[a].at[mine],
            send_sem=send_sems.at[a, k], recv_sem=recv_sems.at[a, k],
            device_id=(cx, cy, ci), device_id_type=MESH)

    def landed(a, k, cx, cy):
        return pltpu.make_async_remote_copy(
            src_ref=p_refs[a].at[mine], dst_ref=recv_refs[a].at[2 * cx + cy],
            send_sem=send_sems.at[a, k], recv_sem=recv_sems.at[a, k],
            device_id=(cx, cy, ci), device_id_type=MESH)

    def start():
        for a in range(n):
            for k, (cx, cy) in enumerate(chips):
                copy(a, k, cx, cy).start()

    def finish():
        for a in range(n):
            for k, (cx, cy) in enumerate(chips):
                landed(a, k, cx, cy).wait_recv()
        for a in range(n):
            for k, (cx, cy) in enumerate(chips):
                copy(a, k, cx, cy).wait_send()

    return start, finish


def _chip_exchange_scratch(n):
    return [pltpu.SemaphoreType.DMA((n, 3)), pltpu.SemaphoreType.DMA((n, 3))]


def _chip_exchange(ps, name):
    n = len(ps)

    def body(*refs):
        start, finish = _chip_exchange_stages(refs[:n], refs[n:2 * n], *refs[2 * n:])
        start()
        finish()

    return pl.pallas_call(
        body, name=name, out_shape=[jax.ShapeDtypeStruct(p.shape, p.dtype) for p in ps],
        in_specs=[ANY] * n, out_specs=[ANY] * n, scratch_shapes=_chip_exchange_scratch(n),
    )(*ps)


def _adamw_math(w, g, m, v):
    m = ADAM_B1 * m + (1.0 - ADAM_B1) * g
    v = ADAM_B2 * v + (1.0 - ADAM_B2) * (g * g)
    m_hat = m / (1.0 - ADAM_B1 ** ADAM_STEP)
    v_hat = v / (1.0 - ADAM_B2 ** ADAM_STEP)
    delta = -ADAM_LR * (m_hat / (jnp.sqrt(v_hat) + ADAM_EPS) + ADAM_WD * w)
    return delta, m, v


def _adamw_reduce(own, recv, ids, w, m, v, name, tr=256):
    r, c = w.shape
    tr = _tile(r, tr, 16)

    def body(ids_ref, own_ref, recv_ref, w_ref, m_ref, v_ref, g_ref, d_ref, mo_ref, vo_ref):
        mine = ids_ref[1]
        g = None
        for kk in range(N_CHIP):
            term = jnp.where(mine == kk, own_ref[...], recv_ref[kk].astype(F32))
            g = term if g is None else g + term
        delta, m_new, v_new = _adamw_math(w_ref[...], g, m_ref[...], v_ref[...])
        g_ref[...] = g
        d_ref[...] = delta
        mo_ref[...] = m_new
        vo_ref[...] = v_new

    row = pl.BlockSpec((tr, c), lambda i, ids: (i, 0))
    grid_spec = pltpu.PrefetchScalarGridSpec(
        num_scalar_prefetch=1, grid=(r // tr,),
        in_specs=[row, pl.BlockSpec((N_CHIP, tr, c), lambda i, ids: (0, i, 0)), row, row, row],
        out_specs=[row, row, row, row])
    out = jax.ShapeDtypeStruct((r, c), F32)
    return pl.pallas_call(
        body, name=name, grid_spec=grid_spec, out_shape=[out, out, out, out],
        compiler_params=_params("parallel"),
    )(ids, own, recv, w, m, v)


def _sum_sources(a, name):
    n, r, c = a.shape

    def body(a_ref, o_ref):
        tot = a_ref[0]
        for kk in range(1, n):
            tot = tot + a_ref[kk]
        o_ref[...] = tot

    return pl.pallas_call(
        body, name=name, out_shape=jax.ShapeDtypeStruct((r, c), F32),
        in_specs=[pl.BlockSpec(memory_space=pltpu.VMEM)], out_specs=pl.BlockSpec(memory_space=pltpu.VMEM),
    )(a)


def _adamw_small(w, g, m, v, name):
    def body(w_ref, g_ref, m_ref, v_ref, d_ref, mo_ref, vo_ref):
        delta, m_new, v_new = _adamw_math(w_ref[...], g_ref[...], m_ref[...], v_ref[...])
        d_ref[...] = delta
        mo_ref[...] = m_new
        vo_ref[...] = v_new

    vm = pl.BlockSpec(memory_space=pltpu.VMEM)
    out = jax.ShapeDtypeStruct(w.shape, F32)
    return pl.pallas_call(body, name=name, out_shape=[out, out, out], in_specs=[vm] * 4, out_specs=[vm] * 3)(w, g, m, v)


def _pack(parts, width, row_mult):
    flat = jnp.concatenate([p.reshape(-1) for p in parts])
    rows = -(-flat.shape[0] // width)
    rows = -(-rows // row_mult) * row_mult
    return jnp.pad(flat, (0, rows * width - flat.shape[0])).reshape(rows, width)


def _unpack(flat, shapes):
    out, off = [], 0
    lead = flat.shape[:-1]
    for shp in shapes:
        n = 1
        for dd in shp:
            n *= dd
        out.append(flat[..., off:off + n].reshape(lead + tuple(shp)))
        off += n
    return out


def _rows2d(w):
    return w.reshape(w.shape[0] * w.shape[1], w.shape[2])


def _cols_to_dev(g):
    l, k, n = g.shape
    return g.reshape(l * k, N_DEV, n // N_DEV).transpose(1, 0, 2)


def _rows_to_dev(g):
    l, k, n = g.shape
    rs = k // N_DEV
    return g.reshape(l, N_DEV, rs, n).transpose(1, 0, 2, 3).reshape(N_DEV, l * rs, n)


def _dev_to_cols(a, l):
    _, lk, cs = a.shape
    return a.transpose(1, 0, 2).reshape(l, lk // l, N_DEV * cs)


def _dev_to_rows(a, l):
    _, lr, n = a.shape
    rs = lr // l
    return a.reshape(N_DEV, l, rs, n).transpose(1, 0, 2, 3).reshape(l, N_DEV * rs, n)


def kernel(x, attn_norm, attn_w_in, attn_f_bias, fox_q_gain, fox_k_gain, sb_q_gain, sb_k_gain, attn_w_out, conv_norm, conv_w_in, conv_kernel, conv_w_out, ffn_norm, ffn_w_up, ffn_conv, ffn_w_down, loss_target, m_attn_norm, m_attn_w_in, m_attn_f_bias, m_fox_q_gain, m_fox_k_gain, m_sb_q_gain, m_sb_k_gain, m_attn_w_out, m_conv_norm, m_conv_w_in, m_conv_kernel, m_conv_w_out, m_ffn_norm, m_ffn_w_up, m_ffn_conv, m_ffn_w_down, v_attn_norm, v_attn_w_in, v_attn_f_bias, v_fox_q_gain, v_fox_k_gain, v_sb_q_gain, v_sb_k_gain, v_attn_w_out, v_conv_norm, v_conv_w_in, v_conv_kernel, v_conv_w_out, v_ffn_norm, v_ffn_w_up, v_ffn_conv, v_ffn_w_down):
    s = x.shape[1]
    n_attn, n_conv, depth = attn_w_in.shape[0], conv_w_in.shape[0], ffn_w_up.shape[0]
    xi, yi, ci = _coords()
    me = 4 * xi + 2 * yi + ci
    ids = jnp.stack([ci, 2 * xi + yi]).astype(jnp.int32)

    big = [attn_w_in, attn_w_out, conv_w_in, conv_w_out, ffn_w_up, ffn_w_down]
    big_m = [m_attn_w_in, m_attn_w_out, m_conv_w_in, m_conv_w_out, m_ffn_w_up, m_ffn_w_down]
    big_v = [v_attn_w_in, v_attn_w_out, v_conv_w_in, v_conv_w_out, v_ffn_w_up, v_ffn_w_down]
    big_names = ["attn_w_in", "attn_w_out", "conv_w_in", "conv_w_out", "ffn_w_up", "ffn_w_down"]
    small_sh = [conv_norm, conv_kernel, ffn_conv]
    small_sh_shapes = [w.shape for w in small_sh]
    rep = [attn_norm, attn_f_bias, fox_q_gain, fox_k_gain, sb_q_gain, sb_k_gain, ffn_norm]
    rep_shapes = [w.shape for w in rep]

    small_pack = _pack(small_sh, LANES, 8)
    shards_bf16 = [_rows2d(w).astype(BF16) for w in big]
    early = _all_gather([attn_w_in[0].astype(BF16), small_pack], "gather_first")
    first_w_in = jnp.pad(_dev_to_cols(early[0], 1)[0], ((0, 0), (0, ATTN_IN_PAD - ATTN_IN)))
    cn, ckern, fconv = _unpack(early[1].reshape(N_DEV, -1), small_sh_shapes)
    conv_norm_f = cn.transpose(1, 0, 2).reshape(n_conv, D_MODEL)
    conv_kernel_f = ckern.transpose(1, 2, 0, 3).reshape(n_conv, 3, D_MODEL)
    ffn_conv_f = fconv.transpose(1, 2, 0, 3).reshape(depth, 3, 2 * D_FF)

    def pair_gain(fox_g, sb_g):
        f2 = jnp.concatenate([fox_g, fox_g])
        s2 = jnp.concatenate([sb_g, sb_g])
        return jnp.concatenate([jnp.tile(f2[None], (4, 1)), jnp.tile(s2[None], (4, 1))])[:, None, :]

    h = x[0]
    saved = []
    for layer in range(depth):
        i = layer // 2
        tag = "l%d" % layer
        rec = {"h_in": h}
        if layer % 2 == 0:
            xn = _rms_fwd(h, attn_norm[i], tag + "_attn_rms")
            proj = _matmul(xn, first_w_in if layer == 0 else a_w_in[i], tag + "_attn_in", tn=640)
            gq, gk = pair_gain(fox_q_gain[i], sb_q_gain[i]), pair_gain(fox_k_gain[i], sb_k_gain[i])
            qh, kh, vb, qt, kt, vt = _qkv_prep(proj, gq, gk, tag + "_qkv_prep")
            logit = proj[:, 3 * MIX:3 * MIX + H_FOX].T.reshape(H_FOX, s // LANES, LANES)
            cum = _fgate_fwd(logit, attn_f_bias[i], tag + "_fgate")
            frow = cum.reshape(H_FOX // 2, 2, s)
            fcol = frow.transpose(0, 2, 1)
            o_fox, lse = _fox_fwd(qt, kh, vt, frow, fcol, tag + "_fox_fwd")
            if layer == 0:
                o_sb, tot, *gathered = _sb_fwd(qt, kh, vt, tag + "_sb_fwd", gather=shards_bf16)
                a_w_in = _dev_to_cols(gathered[0], n_attn)
                a_w_in = jnp.pad(a_w_in, ((0, 0), (0, 0), (0, ATTN_IN_PAD - ATTN_IN)))
                a_w_out = _dev_to_rows(gathered[1], n_attn)
                c_w_in = _dev_to_cols(gathered[2], n_conv)
                c_w_out = _dev_to_rows(gathered[3], n_conv)
                f_w_up = _dev_to_cols(gathered[4], depth)
                f_w_down = _dev_to_rows(gathered[5], depth)
            else:
                o_sb, tot = _sb_fwd(qt, kh, vt, tag + "_sb_fwd")
            o = jnp.concatenate([o_fox, o_sb], axis=1)
            h = _matmul(o, a_w_out[i], tag + "_attn_out", add=h, tn=1024)
            rec.update(xn=xn, proj=proj, gq=gq, gk=gk, qh=qh, kh=kh, vb=vb, qt=qt, kt=kt, logit=logit, frow=frow,
                       fcol=fcol, o_fox=o_fox, lse=lse, tot=tot, o=o)
        else:
            xn = _rms_fwd(h, conv_norm_f[i], tag + "_conv_rms")
            proj = _matmul(xn, c_w_in[i], tag + "_conv_in", tn=1024)
            y = _sconv_fwd(proj, conv_kernel_f[i], tag + "_sconv_fwd")
            h = _matmul(y, c_w_out[i], tag + "_conv_out", add=h, tn=1024)
            rec.update(xn=xn, proj=proj, y=y)
        rec["h_mid"] = h
        xn2 = _rms_fwd(h, ffn_norm[layer], tag + "_ffn_rms")
        up = _matmul(xn2, f_w_up[layer], tag + "_ffn_up", tn=1408)
        act = _ffn_act_fwd(up, ffn_conv_f[layer], tag + "_ffn_act")
        h = _matmul(act, f_w_down[layer], tag + "_ffn_down", add=h, tn=1024, tk=2816)
        rec.update(xn2=xn2, up=up, act=act)
        saved.append(rec)

    loss_local, dh = _loss_head(h, loss_target[0], "loss_head")
    loss = lax.psum(loss_local, ("x", "y", "c"))

    g_attn_norm, g_attn_w_in, g_f_bias = [None] * n_attn, [None] * n_attn, [None] * n_attn
    g_fq, g_fk, g_sq, g_sk, g_attn_w_out = ([None] * n_attn for _ in range(5))
    g_conv_norm, g_conv_w_in, g_conv_kernel, g_conv_w_out = ([None] * n_conv for _ in range(4))
    g_ffn_norm, g_ffn_w_up, g_ffn_conv, g_ffn_w_down = ([None] * depth for _ in range(4))
    everything = slice(0, None)
    early_layers = {nm: (slice(1, None) if nm == "attn_w_in" else everything) for nm in big_names}
    late_layers = {"attn_w_in": slice(0, 1)}

    def slabs_for_devices(layers):
        stacks = {"attn_w_in": (g_attn_w_in, _cols_to_dev), "attn_w_out": (g_attn_w_out, _rows_to_dev),
                  "conv_w_in": (g_conv_w_in, _cols_to_dev), "conv_w_out": (g_conv_w_out, _rows_to_dev),
                  "ffn_w_up": (g_ffn_w_up, _cols_to_dev), "ffn_w_down": (g_ffn_w_down, _rows_to_dev)}
        return [stacks[nm][1](jnp.stack(stacks[nm][0][layers[nm]]).astype(BF16)) for nm in big_names if nm in layers]

    for layer in reversed(range(depth)):
        i = layer // 2
        tag = "l%d" % layer
        rec = saved[layer]
        da = _matmul(dh, f_w_down[layer].T, tag + "_ffn_down_dx", tn=1408)
        g_ffn_w_down[layer] = _matmul_tn(rec["act"], dh, tag + "_ffn_down_dw", tm=1408, tn=1024)
        dup, dwg, dwv = _ffn_act_bwd(rec["up"], ffn_conv_f[layer], da, tag + "_ffn_act_bwd")
        g_ffn_conv[layer] = jnp.concatenate([dwg, dwv], axis=1)
        g_ffn_w_up[layer] = _matmul_tn(rec["xn2"], dup, tag + "_ffn_up_dw", tn=1408)
        dxn = _matmul(dup, f_w_up[layer].T, tag + "_ffn_up_dx", tn=1024, tk=2816)
        dh, g_ffn_norm[layer] = _rms_bwd(rec["h_mid"], dxn, ffn_norm[layer], dh, tag + "_ffn_rms_bwd")
        if layer % 2 == 0:
            do = _matmul(dh, a_w_out[i].T, tag + "_attn_out_dx", tn=1024)
            g_attn_w_out[i] = _matmul_tn(rec["o"], dh, tag + "_attn_out_dw", tn=1024)
            if layer == 0:
                early_g = slabs_for_devices(early_layers)
                early_pairs = [_pair_add(g, r, ids, "reduce_pair_add_" + nm)
                               for g, r, nm in zip(early_g, _sibling_exchange(early_g, "reduce_sibling"), big_names)]
            dq_f, dk_f, dv_f, dfk, dfq = _fox_bwd(rec["qh"], rec["qt"], rec["kh"], rec["kt"], rec["vb"], rec["frow"],
                                                  rec["fcol"], rec["lse"], rec["o_fox"], do, tag + "_fox_bwd")
            dq_s, dk_s, dv_s, *from_chips = _sb_bwd(
                rec["qh"], rec["qt"], rec["kh"], rec["kt"], rec["vb"], rec["tot"], do, tag + "_sb_bwd",
                exchange=[pr[0] for pr in early_pairs] if layer == 0 else ())
            if layer == 0:
                early_from_chips = from_chips
            dq, dk, dv, dgq, dgk = _qkv_prep_bwd(rec["proj"], rec["gq"], rec["gk"], (dq_f, dq_s), (dk_f, dk_s),
                                                 (dv_f, dv_s), tag + "_qkv_prep_bwd")
            dcum = (dfq + dfk[:, :, 0:2].transpose(0, 2, 1)).reshape(H_FOX, s // LANES, LANES)
            dlogit, dbias = _fgate_bwd(rec["logit"], attn_f_bias[i], dcum, tag + "_fgate_bwd")
            g_f_bias[i] = dbias[:, 0]
            dgate = jnp.pad(dlogit.reshape(H_FOX, s).T, ((0, 0), (0, LANES - H_FOX))).astype(BF16)
            dproj = jnp.concatenate([dq, dk, dv, dgate], axis=1)

            def fold(dg):
                per_head = dg.reshape(16, HEAD_DIM)
                return jnp.sum(per_head[:8], axis=0), jnp.sum(per_head[8:], axis=0)

            g_fq[i], g_sq[i] = fold(dgq)
            g_fk[i], g_sk[i] = fold(dgk)
            g_attn_w_in[i] = _matmul_tn(rec["xn"], dproj, tag + "_attn_in_dw", tn=640)[:, :ATTN_IN]
            dxn = _matmul(dproj, a_w_in[i].T, tag + "_attn_in_dx", tn=1024, tk=3200)
            dh, g_attn_norm[i] = _rms_bwd(rec["h_in"], dxn, attn_norm[i], dh, tag + "_attn_rms_bwd")
        else:
            dy = _matmul(dh, c_w_out[i].T, tag + "_conv_out_dx", tn=1024)
            g_conv_w_out[i] = _matmul_tn(rec["y"], dh, tag + "_conv_out_dw", tn=1024)
            dproj, g_conv_kernel[i] = _sconv_bwd(rec["proj"], conv_kernel_f[i], dy, tag + "_sconv_bwd")
            g_conv_w_in[i] = _matmul_tn(rec["xn"], dproj, tag + "_conv_in_dw", tn=1024)
            dxn = _matmul(dproj, c_w_in[i].T, tag + "_conv_in_dx", tn=1024, tk=3072)
            dh, g_conv_norm[i] = _rms_bwd(rec["h_in"], dxn, conv_norm_f[i], dh, tag + "_conv_rms_bwd")
    grad_x = dh[None]

    late_names = [nm for nm in big_names if nm in late_layers]
    late_g = slabs_for_devices(late_layers)
    late_pairs = [_pair_add(g, r, ids, "reduce_pair_add_late_" + nm)
                  for g, r, nm in zip(late_g, _sibling_exchange(late_g, "reduce_sibling_late"), late_names)]
    late_from_chips = _chip_exchange([pr[0] for pr in late_pairs], "reduce_chips_late")

    def update(piece_pairs, piece_recv, piece_names, layers):
        outs = {}
        for (own, recv, nm) in zip([pr[1] for pr in piece_pairs], piece_recv, piece_names):
            sl = layers[nm]
            which = big_names.index(nm)
            outs[nm] = _adamw_reduce(own, recv, ids, _rows2d(big[which][sl]), _rows2d(big_m[which][sl]),
                                     _rows2d(big_v[which][sl]), "adamw_%s_%d" % (nm, sl.start))
        return outs

    late_out = update(late_pairs, late_from_chips, late_names, late_layers)
    early_out = update(early_pairs, early_from_chips, big_names, early_layers)
    grads_big, delta_big, newm_big, newv_big = [], [], [], []
    for which, nm in enumerate(big_names):
        shp = big[which].shape
        for k, dest in enumerate((grads_big, delta_big, newm_big, newv_big)):
            parts = ([late_out[nm][k]] if nm in late_out else []) + [early_out[nm][k]]
            dest.append(jnp.concatenate(parts, axis=0).reshape(shp))

    rep_g = [jnp.stack(g_attn_norm), jnp.stack(g_f_bias), jnp.stack(g_fq), jnp.stack(g_fk), jnp.stack(g_sq),
             jnp.stack(g_sk), jnp.stack(g_ffn_norm)]
    sh_g = [jnp.stack(g_conv_norm).reshape(n_conv, N_DEV, -1).transpose(1, 0, 2),
            jnp.stack(g_conv_kernel).reshape(n_conv, 3, N_DEV, -1).transpose(2, 0, 1, 3),
            jnp.stack(g_ffn_conv).reshape(depth, 3, N_DEV, -1).transpose(2, 0, 1, 3)]
    n_rep = sum(int(a.size) for a in rep)
    n_sh = sum(int(a.size) for a in small_sh)
    partial = _pack(rep_g + [jnp.concatenate([a.reshape(N_DEV, -1) for a in sh_g], axis=1)], LANES, 8)
    total = _sum_sources(_all_gather([partial], "gather_small_grads")[0], "sum_small_grads").reshape(-1)
    rep_tot = total[:n_rep]
    sh_tot = lax.dynamic_slice_in_dim(total[n_rep:n_rep + N_DEV * n_sh].reshape(N_DEV, n_sh), me, 1, axis=0)[0]
    g_small = _pack([rep_tot, sh_tot], LANES, 8)

    def small_pack_of(rep_list, sh_list):
        return _pack(rep_list + sh_list, LANES, 8)

    d_small, m_small, v_small = _adamw_small(
        small_pack_of(rep, small_sh), g_small,
        small_pack_of([m_attn_norm, m_attn_f_bias, m_fox_q_gain, m_fox_k_gain, m_sb_q_gain, m_sb_k_gain, m_ffn_norm],
                      [m_conv_norm, m_conv_kernel, m_ffn_conv]),
        small_pack_of([v_attn_norm, v_attn_f_bias, v_fox_q_gain, v_fox_k_gain, v_sb_q_gain, v_sb_k_gain, v_ffn_norm],
                      [v_conv_norm, v_conv_kernel, v_ffn_conv]),
        "adamw_small")
    small_shapes = rep_shapes + small_sh_shapes

    def split_small(a):
        return _unpack(a.reshape(-1), small_shapes)

    def ordered(big_list, small_list):
        an, fb, fq, fk, sq, sk, fn, cno, cke, fco = small_list
        awi, awo, cwi, cwo, fwu, fwd = big_list
        return [an, awi, fb, fq, fk, sq, sk, awo, cno, cwi, cke, cwo, fn, fwu, fco, fwd]

    grads = ordered(grads_big, split_small(g_small))
    deltas = ordered(delta_big, split_small(d_small))
    new_m = ordered(newm_big, split_small(m_small))
    new_v = ordered(newv_big, split_small(v_small))
    return (loss, grad_x, *grads, *deltas, *new_m, *new_v)
```

```python
import jax
import jax.numpy as jnp
from jax import lax
from jax.experimental import pallas as pl
from jax.experimental.pallas import tpu as pltpu

F32 = jnp.float32
BF16 = jnp.bfloat16

D_MODEL = 1024
HEAD_DIM = 64
H_FOX = 8
MIX = 1024
ATTN_IN = 3 * MIX + H_FOX
ATTN_IN_PAD = 3 * MIX + 128
D_FF = 2816
EPS = 1e-6
NEG = -1e30
LANES = 128
N_DEV = 8
N_CHIP = 4

ADAM_LR = 0.001
ADAM_B1 = 0.9
ADAM_B2 = 0.999
ADAM_EPS = 1e-08
ADAM_WD = 0.01
ADAM_STEP = 10

VMEM_LIMIT = 56 * 1024 * 1024
MESH = pl.DeviceIdType.MESH
ANY = pl.BlockSpec(memory_space=pl.ANY)


def _params(*sem):
    return pltpu.CompilerParams(dimension_semantics=sem, vmem_limit_bytes=VMEM_LIMIT)


def _tile(n, target, mult=LANES):
    best = None
    for t in range(mult, min(n, target) + 1, mult):
        if n % t == 0:
            best = t
    return best if best is not None else n


def _dot(a, b):
    return jnp.dot(a, b, preferred_element_type=F32)


def _dot_tn(a, b):
    return lax.dot_general(a, b, (((0,), (0,)), ((), ())), preferred_element_type=F32)


def _split_dot(x, m, passes):
    acc = None
    rem = x
    for _ in range(passes):
        part = rem.astype(BF16)
        term = _dot(part, m)
        acc = term if acc is None else acc + term
        rem = rem - part.astype(F32)
    return acc


def _split_dot_left(m, x, passes):
    acc = None
    rem = x
    for _ in range(passes):
        part = rem.astype(BF16)
        term = _dot(m, part)
        acc = term if acc is None else acc + term
        rem = rem - part.astype(F32)
    return acc


def _matmul(a, b, name, add=None, out_dtype=F32, tm=1024, tn=512, tk=1024):
    split = a.shape[0] if a.ndim == 3 else 1
    m, kh = a.shape[-2:]
    k = split * kh
    n = b.shape[1]
    tm, tn, tk = _tile(m, tm, 8), _tile(n, tn), _tile(kh, tk)
    nk = k // tk
    per_slab = kh // tk
    has_add = add is not None

    def body(*refs):
        a_ref, b_ref = refs[0], refs[1]
        add_ref = refs[2] if has_add else None
        o_ref = refs[2 + has_add]

        def finish(acc):
            if has_add:
                acc = acc + add_ref[...]
            o_ref[...] = acc.astype(out_dtype)

        p = _dot(a_ref[...].astype(BF16), b_ref[...].astype(BF16))
        if nk == 1:
            finish(p)
        else:
            acc_ref = refs[-1]
            kk = pl.program_id(2)

            @pl.when(kk == 0)
            def _():
                acc_ref[...] = p

            @pl.when(kk > 0)
            def _():
                acc_ref[...] += p

            @pl.when(kk == nk - 1)
            def _():
                finish(acc_ref[...])

    if split == 1:
        a_spec = pl.BlockSpec((tm, tk), lambda i, j, kk: (i, kk))
    else:
        a_spec = pl.BlockSpec((None, tm, tk), lambda i, j, kk: (kk // per_slab, i, kk % per_slab))
    in_specs = [a_spec, pl.BlockSpec((tk, tn), lambda i, j, kk: (kk, j))]
    args = [a, b]
    if has_add:
        in_specs.append(pl.BlockSpec((tm, tn), lambda i, j, kk: (i, j)))
        args.append(add)
    return pl.pallas_call(
        body, name=name, grid=(m // tm, n // tn, nk), in_specs=in_specs,
        out_specs=pl.BlockSpec((tm, tn), lambda i, j, kk: (i, j)),
        out_shape=jax.ShapeDtypeStruct((m, n), out_dtype),
        scratch_shapes=[pltpu.VMEM((tm, tn), F32)] if nk > 1 else [],
        compiler_params=_params("parallel", "parallel", "arbitrary"),
    )(*args)


def _matmul_rows(a, b, name, add=None, norm_gain=None, rms_bwd=None, tm=512, tk=1024):
    split = a.shape[0] if a.ndim == 3 else 1
    m, kh = a.shape[-2:]
    n = b.shape[1]
    tm, tk = _tile(m, tm, 8), _tile(kh, tk)
    nk = split * kh // tk
    per_slab = kh // tk
    has_add, has_norm, has_bwd = add is not None, norm_gain is not None, rms_bwd is not None

    def rsqrt_mean_sq(x):
        return lax.rsqrt(jnp.mean(x * x, axis=-1, keepdims=True) + EPS)

    def body(*refs):
        it = iter(refs)
        a_ref, b_ref = next(it), next(it)
        add_ref = next(it) if has_add else None
        gain_ref = next(it) if has_norm else None
        h_ref, g_ref, dhin_ref = (next(it), next(it), next(it)) if has_bwd else (None, None, None)
        o_ref = next(it)
        xn_ref = next(it) if has_norm else None
        dg_ref = next(it) if has_bwd else None
        acc_ref = next(it) if nk > 1 else None
        i = pl.program_id(0)

        def finish(acc):
            if has_add:
                acc = acc + add_ref[...]
            if has_bwd:
                x = h_ref[...]
                r = rsqrt_mean_sq(x)
                xh = x * r

                @pl.when(i == 0)
                def _():
                    dg_ref[...] = jnp.zeros_like(dg_ref)

                dg_ref[0:1, :] += jnp.sum(acc * xh, axis=0, keepdims=True)
                dxh = acc * g_ref[...]
                acc = dhin_ref[...] + r * (dxh - xh * jnp.mean(dxh * xh, axis=-1, keepdims=True))
            o_ref[...] = acc
            if has_norm:
                xn_ref[...] = (acc * rsqrt_mean_sq(acc) * gain_ref[...]).astype(BF16)

        p = _dot(a_ref[...].astype(BF16), b_ref[...].astype(BF16))
        if nk == 1:
            finish(p)
        else:
            kk = pl.program_id(1)

            @pl.when(kk == 0)
            def _():
                acc_ref[...] = p

            @pl.when(kk > 0)
            def _():
                acc_ref[...] += p

            @pl.when(kk == nk - 1)
            def _():
                finish(acc_ref[...])

    if split == 1:
        a_spec = pl.BlockSpec((tm, tk), lambda i, kk: (i, kk))
    else:
        a_spec = pl.BlockSpec((None, tm, tk), lambda i, kk: (kk // per_slab, i, kk % per_slab))
    row = pl.BlockSpec((tm, n), lambda i, kk: (i, 0))
    vec = pl.BlockSpec((1, n), lambda i, kk: (0, 0))
    in_specs, args = [a_spec, pl.BlockSpec((tk, n), lambda i, kk: (kk, 0))], [a, b]
    if has_add:
        in_specs.append(row)
        args.append(add)
    if has_norm:
        in_specs.append(vec)
        args.append(norm_gain.reshape(1, n))
    if has_bwd:
        in_specs += [row, vec, row]
        args += [rms_bwd[0], rms_bwd[1].reshape(1, n), rms_bwd[2]]
    out_specs, out_shape = [row], [jax.ShapeDtypeStruct((m, n), F32)]
    if has_norm:
        out_specs.append(row)
        out_shape.append(jax.ShapeDtypeStruct((m, n), BF16))
    if has_bwd:
        out_specs.append(pl.BlockSpec((8, n), lambda i, kk: (0, 0)))
        out_shape.append(jax.ShapeDtypeStruct((8, n), F32))
    outs = pl.pallas_call(
        body, name=name, grid=(m // tm, nk), in_specs=in_specs, out_specs=out_specs, out_shape=out_shape,
        scratch_shapes=[pltpu.VMEM((tm, n), F32)] if nk > 1 else [],
        compiler_params=_params("arbitrary" if has_bwd else "parallel", "arbitrary"),
    )(*args)
    if has_bwd:
        return outs[0], outs[-1][0]
    return tuple(outs) if has_norm else outs[0]


def _matmul_tn(a, b, name, tm=1024, tn=512, ts=2048):
    s, m = a.shape
    split = b.shape[0] if b.ndim == 3 else 1
    nh = b.shape[-1]
    n = split * nh
    tm, tn, ts = _tile(m, tm), _tile(nh, tn), _tile(s, ts, 8)
    per_slab = nh // tn
    if split == 1:
        b_spec = pl.BlockSpec((ts, tn), lambda i, j, kk: (kk, j))
    else:
        b_spec = pl.BlockSpec((None, ts, tn), lambda i, j, kk: (j // per_slab, kk, j % per_slab))

    def body(a_ref, b_ref, o_ref):
        kk = pl.program_id(2)
        p = _dot_tn(a_ref[...].astype(BF16), b_ref[...].astype(BF16))

        @pl.when(kk == 0)
        def _():
            o_ref[...] = p

        @pl.when(kk > 0)
        def _():
            o_ref[...] += p

    return pl.pallas_call(
        body, name=name, grid=(m // tm, n // tn, s // ts),
        in_specs=[pl.BlockSpec((ts, tm), lambda i, j, kk: (kk, i)), b_spec],
        out_specs=pl.BlockSpec((tm, tn), lambda i, j, kk: (i, j)),
        out_shape=jax.ShapeDtypeStruct((m, n), F32),
        compiler_params=_params("parallel", "parallel", "arbitrary"),
    )(a, b)


def _rms_fwd(h, g, name, ts=512):
    s, d = h.shape
    ts = _tile(s, ts, 8)

    def body(h_ref, g_ref, o_ref):
        x = h_ref[...]
        r = lax.rsqrt(jnp.mean(x * x, axis=-1, keepdims=True) + EPS)
        o_ref[...] = (x * r * g_ref[...]).astype(BF16)

    return pl.pallas_call(
        body, name=name, grid=(s // ts,),
        in_specs=[pl.BlockSpec((ts, d), lambda i: (i, 0)), pl.BlockSpec((1, d), lambda i: (0, 0))],
        out_specs=pl.BlockSpec((ts, d), lambda i: (i, 0)),
        out_shape=jax.ShapeDtypeStruct((s, d), BF16),
        compiler_params=_params("parallel"),
    )(h, g.reshape(1, d))


def _rms_bwd(h, dxn, g, dh_in, name, ts=512):
    s, d = h.shape
    ts = _tile(s, ts, 8)

    def body(h_ref, dxn_ref, g_ref, dhin_ref, dh_ref, dg_ref):
        i = pl.program_id(0)
        x = h_ref[...]
        r = lax.rsqrt(jnp.mean(x * x, axis=-1, keepdims=True) + EPS)
        xh = x * r
        dxn_v = dxn_ref[...]

        @pl.when(i == 0)
        def _():
            dg_ref[...] = jnp.zeros_like(dg_ref)

        dg_ref[0:1, :] += jnp.sum(dxn_v * xh, axis=0, keepdims=True)
        dxh = dxn_v * g_ref[...]
        dx = r * (dxh - xh * jnp.mean(dxh * xh, axis=-1, keepdims=True))
        dh_ref[...] = dhin_ref[...] + dx

    row = pl.BlockSpec((ts, d), lambda i: (i, 0))
    dh, dg = pl.pallas_call(
        body, name=name, grid=(s // ts,),
        in_specs=[row, row, pl.BlockSpec((1, d), lambda i: (0, 0)), row],
        out_specs=[row, pl.BlockSpec((8, d), lambda i: (0, 0))],
        out_shape=[jax.ShapeDtypeStruct((s, d), F32), jax.ShapeDtypeStruct((8, d), F32)],
        compiler_params=_params("arbitrary"),
    )(h, dxn, g.reshape(1, d), dh_in)
    return dh, dg[0]


def _shift_down(x, prev):
    rows = lax.broadcasted_iota(jnp.int32, (8, x.shape[1]), 0)
    p1, p2 = prev[7:8, :], prev[6:7, :]
    r1, r2 = pltpu.roll(x, 1, 0), pltpu.roll(x, 2, 0)
    top1 = jnp.where(rows == 0, p1, r1[0:8, :])
    top2 = jnp.where(rows == 0, p2, jnp.where(rows == 1, p1, r2[0:8, :]))
    if x.shape[0] == 8:
        return top1, top2
    return jnp.concatenate([top1, r1[8:, :]], axis=0), jnp.concatenate([top2, r2[8:, :]], axis=0)


def _shift_up(x, nxt):
    n = x.shape[0]
    rows = lax.broadcasted_iota(jnp.int32, (8, x.shape[1]), 0)
    n0, n1 = nxt[0:1, :], nxt[1:2, :]
    r1, r2 = pltpu.roll(x, n - 1, 0), pltpu.roll(x, n - 2, 0)
    end1 = jnp.where(rows == 7, n0, r1[n - 8:, :])
    end2 = jnp.where(rows == 7, n1, jnp.where(rows == 6, n0, r2[n - 8:, :]))
    return jnp.concatenate([r1[:n - 8, :], end1], axis=0), jnp.concatenate([r2[:n - 8, :], end2], axis=0)


def _conv(x, x1, x2, w):
    return w[2:3, :] * x + w[1:2, :] * x1 + w[0:1, :] * x2


def _halo_specs(ts, tc, col, n_time_blocks):
    r8 = ts // 8
    main = pl.BlockSpec((ts, tc), lambda j, i: (i, j + col))
    prev = pl.BlockSpec((8, tc), lambda j, i: (jnp.maximum(i * r8 - 1, 0), j + col))
    nxt = pl.BlockSpec((8, tc), lambda j, i: (jnp.minimum((i + 1) * r8, n_time_blocks * r8 - 1), j + col))
    return main, prev, nxt


def _silu_parts(g):
    sig = 1.0 / (1.0 + jnp.exp(-g))
    return sig, g * sig


def _ffn_act_fwd(up, cw, name, ts=256, tc=1408):
    s = up.shape[0]
    ts, tc = _tile(s, ts, 8), _tile(D_FF, tc)
    nc, nt = D_FF // tc, s // ts

    def body(g_ref, gp_ref, v_ref, vp_ref, wg_ref, wv_ref, o_ref):
        first = pl.program_id(1) == 0

        def conv(x_ref, p_ref, w_ref):
            x = x_ref[...]
            prev = jnp.where(first, 0.0, p_ref[...])
            x1, x2 = _shift_down(x, prev)
            return _conv(x, x1, x2, w_ref[...])

        ug = conv(g_ref, gp_ref, wg_ref)
        uv = conv(v_ref, vp_ref, wv_ref)
        _, silu = _silu_parts(ug)
        o_ref[...] = (silu * uv).astype(BF16)

    g_main, g_prev, _ = _halo_specs(ts, tc, 0, nt)
    v_main, v_prev, _ = _halo_specs(ts, tc, nc, nt)
    return pl.pallas_call(
        body, name=name, grid=(nc, nt),
        in_specs=[g_main, g_prev, v_main, v_prev,
                  pl.BlockSpec((3, tc), lambda j, i: (0, j)), pl.BlockSpec((3, tc), lambda j, i: (0, j + nc))],
        out_specs=pl.BlockSpec((ts, tc), lambda j, i: (i, j)),
        out_shape=jax.ShapeDtypeStruct((s, D_FF), BF16),
        compiler_params=_params("parallel", "parallel"),
    )(up, up, up, up, cw, cw)


def _ffn_act_bwd(up, cw, da, name, ts=256, tc=1408):
    s = up.shape[0]
    ts, tc = _tile(s, ts, 8), _tile(D_FF, tc)
    nc, nt = D_FF // tc, s // ts

    def body(g_ref, gp_ref, gn_ref, v_ref, vp_ref, vn_ref, da_ref, dan_ref, wg_ref, wv_ref,
             d_ref, dwg_ref, dwv_ref):
        i = pl.program_id(1)
        first, last = i == 0, i == nt - 1
        wg, wv = wg_ref[...], wv_ref[...]
        g, v = g_ref[...], v_ref[...]
        g1, g2 = _shift_down(g, jnp.where(first, 0.0, gp_ref[...]))
        v1, v2 = _shift_down(v, jnp.where(first, 0.0, vp_ref[...]))

        def d_u(ug, uv, da_v):
            sig, silu = _silu_parts(ug)
            return da_v * uv * (sig * (1.0 + ug * (1.0 - sig))), da_v * silu

        dug, duv = d_u(_conv(g, g1, g2, wg), _conv(v, v1, v2, wv), da_ref[...])
        gn, vn = gn_ref[...], vn_ref[...]
        gn1, gn2 = _shift_down(gn, g[ts - 8:, :])
        vn1, vn2 = _shift_down(vn, v[ts - 8:, :])
        dugn, duvn = d_u(_conv(gn, gn1, gn2, wg), _conv(vn, vn1, vn2, wv), dan_ref[...])
        dugn = jnp.where(last, 0.0, dugn)
        duvn = jnp.where(last, 0.0, duvn)

        def finish(du, dun, x, x1, x2, w, dx_ref, dw_ref):
            d1, d2 = _shift_up(du, dun)
            dx_ref[...] = (w[2:3, :] * du + w[1:2, :] * d1 + w[0:1, :] * d2).astype(BF16)

            @pl.when(first)
            def _():
                dw_ref[...] = jnp.zeros_like(dw_ref)

            dw_ref[0:1, :] += jnp.sum(du * x2, axis=0, keepdims=True)
            dw_ref[1:2, :] += jnp.sum(du * x1, axis=0, keepdims=True)
            dw_ref[2:3, :] += jnp.sum(du * x, axis=0, keepdims=True)

        finish(dug, dugn, g, g1, g2, wg, d_ref.at[0], dwg_ref)
        finish(duv, duvn, v, v1, v2, wv, d_ref.at[1], dwv_ref)

    g_specs = _halo_specs(ts, tc, 0, nt)
    v_specs = _halo_specs(ts, tc, nc, nt)
    da_main, _, da_next = _halo_specs(ts, tc, 0, nt)
    taps = pl.BlockSpec((8, tc), lambda j, i: (0, j))
    halves = pl.BlockSpec((2, ts, tc), lambda j, i: (0, i, j))
    d, dwg, dwv = pl.pallas_call(
        body, name=name, grid=(nc, nt),
        in_specs=[*g_specs, *v_specs, da_main, da_next,
                  pl.BlockSpec((3, tc), lambda j, i: (0, j)), pl.BlockSpec((3, tc), lambda j, i: (0, j + nc))],
        out_specs=[halves, taps, taps],
        out_shape=[jax.ShapeDtypeStruct((2, s, D_FF), BF16),
                   jax.ShapeDtypeStruct((8, D_FF), F32), jax.ShapeDtypeStruct((8, D_FF), F32)],
        compiler_params=_params("parallel", "arbitrary"),
    )(up, up, up, up, up, up, da, da, cw, cw)
    return d, dwg[:3], dwv[:3]


def _sconv_fwd(proj, ck, name, ts=256, tc=512):
    s = proj.shape[0]
    w = D_MODEL
    ts, tc = _tile(s, ts, 8), _tile(w, tc)
    nc, nt = w // tc, s // ts

    def body(b_ref, c_ref, cp_ref, u_ref, up_ref, w_ref, o_ref):
        first = pl.program_id(1) == 0
        cu = c_ref[...] * u_ref[...]
        cup = jnp.where(first, 0.0, cp_ref[...] * up_ref[...])
        x1, x2 = _shift_down(cu, cup)
        o_ref[...] = (b_ref[...] * _conv(cu, x1, x2, w_ref[...])).astype(BF16)

    b_main, _, _ = _halo_specs(ts, tc, 0, nt)
    c_main, c_prev, _ = _halo_specs(ts, tc, nc, nt)
    u_main, u_prev, _ = _halo_specs(ts, tc, 2 * nc, nt)
    return pl.pallas_call(
        body, name=name, grid=(nc, nt),
        in_specs=[b_main, c_main, c_prev, u_main, u_prev, pl.BlockSpec((3, tc), lambda j, i: (0, j))],
        out_specs=pl.BlockSpec((ts, tc), lambda j, i: (i, j)),
        out_shape=jax.ShapeDtypeStruct((s, w), BF16),
        compiler_params=_params("parallel", "parallel"),
    )(proj, proj, proj, proj, proj, ck)


def _sconv_bwd(proj, ck, dy, name, ts=256, tc=512):
    s = proj.shape[0]
    w = D_MODEL
    ts, tc = _tile(s, ts, 8), _tile(w, tc)
    nc, nt = w // tc, s // ts

    def body(b_ref, bn_ref, c_ref, cp_ref, u_ref, up_ref, dy_ref, dyn_ref, w_ref,
             d_ref, dw_ref):
        i = pl.program_id(1)
        first, last = i == 0, i == nt - 1
        wv = w_ref[...]
        b, c, u, dy_v = b_ref[...], c_ref[...], u_ref[...], dy_ref[...]
        cu = c * u
        cup = jnp.where(first, 0.0, cp_ref[...] * up_ref[...])
        x1, x2 = _shift_down(cu, cup)
        d_ref[0] = (dy_v * _conv(cu, x1, x2, wv)).astype(BF16)
        dcv = dy_v * b
        dcvn = jnp.where(last, 0.0, dyn_ref[...] * bn_ref[...])
        d1, d2 = _shift_up(dcv, dcvn)
        dcu = wv[2:3, :] * dcv + wv[1:2, :] * d1 + wv[0:1, :] * d2
        d_ref[1] = (dcu * u).astype(BF16)
        d_ref[2] = (dcu * c).astype(BF16)

        @pl.when(first)
        def _():
            dw_ref[...] = jnp.zeros_like(dw_ref)

        dw_ref[0:1, :] += jnp.sum(dcv * x2, axis=0, keepdims=True)
        dw_ref[1:2, :] += jnp.sum(dcv * x1, axis=0, keepdims=True)
        dw_ref[2:3, :] += jnp.sum(dcv * cu, axis=0, keepdims=True)

    b_main, _, b_next = _halo_specs(ts, tc, 0, nt)
    c_main, c_prev, _ = _halo_specs(ts, tc, nc, nt)
    u_main, u_prev, _ = _halo_specs(ts, tc, 2 * nc, nt)
    dy_main, _, dy_next = _halo_specs(ts, tc, 0, nt)
    d, dw = pl.pallas_call(
        body, name=name, grid=(nc, nt),
        in_specs=[b_main, b_next, c_main, c_prev, u_main, u_prev, dy_main, dy_next,
                  pl.BlockSpec((3, tc), lambda j, i: (0, j))],
        out_specs=[pl.BlockSpec((3, ts, tc), lambda j, i: (0, i, j)), pl.BlockSpec((8, tc), lambda j, i: (0, j))],
        out_shape=[jax.ShapeDtypeStruct((3, s, w), BF16), jax.ShapeDtypeStruct((8, w), F32)],
        compiler_params=_params("parallel", "arbitrary"),
    )(proj, proj, proj, proj, proj, proj, dy, dy, ck)
    return d, dw[:3]


def _low_lanes(shape):
    return lax.broadcasted_iota(jnp.int32, shape, 1) < HEAD_DIM


def _top_rows(shape):
    return lax.broadcasted_iota(jnp.int32, shape, 0) < HEAD_DIM


def _norm_pair(x):
    r = lax.rsqrt(_mean_pair(x * x) + EPS)
    return x * r, r


def _mean_pair(x):
    same_head = _tri(LANES, lambda a, b: a // HEAD_DIM == b // HEAD_DIM)
    return _split_dot(x, same_head, 3) * (1.0 / HEAD_DIM)


def _qkv_prep(proj, gq, gk, name, ts=512):
    s = proj.shape[0]
    ts = _tile(s, ts)
    npair = MIX // LANES
    scale = HEAD_DIM ** -0.5

    def body(q_ref, k_ref, v_ref, gq_ref, gk_ref, qo_ref, ko_ref, vo_ref, qt_ref, kt_ref, vt_ref):
        qn, _ = _norm_pair(q_ref[...])
        kn, _ = _norm_pair(k_ref[...])
        q = qn * gq_ref[0] * scale
        k = kn * gk_ref[0]
        v = v_ref[...]
        qo_ref[...] = q.astype(BF16)
        ko_ref[...] = k.astype(BF16)
        vo_ref[...] = v.astype(BF16)
        qt_ref[...] = q.T.astype(BF16)
        kt_ref[...] = k.T.astype(BF16)
        vt_ref[...] = v.T.astype(BF16)

    gain = pl.BlockSpec((1, 1, LANES), lambda i, p: (p, 0, 0))
    tile = pl.BlockSpec((ts, LANES), lambda i, p: (i, p))
    tile_t = pl.BlockSpec((LANES, ts), lambda i, p: (p, i))
    out = jax.ShapeDtypeStruct((s, MIX), BF16)
    out_t = jax.ShapeDtypeStruct((MIX, s), BF16)
    return pl.pallas_call(
        body, name=name, grid=(s // ts, npair),
        in_specs=[tile, pl.BlockSpec((ts, LANES), lambda i, p: (i, p + npair)),
                  pl.BlockSpec((ts, LANES), lambda i, p: (i, p + 2 * npair)), gain, gain],
        out_specs=[tile, tile, tile, tile_t, tile_t, tile_t], out_shape=[out, out, out, out_t, out_t, out_t],
        compiler_params=_params("parallel", "parallel"),
    )(proj, proj, proj, gq, gk)


def _qkv_prep_bwd(proj, gq, gk, dqs, dks, dvs, name, ts=512):
    s = proj.shape[0]
    ts = _tile(s, ts, 8)
    npair = MIX // LANES
    half = npair // 2
    scale = HEAD_DIM ** -0.5

    def body(q_ref, k_ref, gq_ref, gk_ref, dqf_ref, dqs_ref, dkf_ref, dks_ref, dvf_ref, dvs_ref,
             dq_ref, dk_ref, dv_ref, dgq_ref, dgk_ref):
        p, i = pl.program_id(0), pl.program_id(1)
        fox = p < half

        def one(x_ref, g_ref, df_ref, ds_ref, dx_ref, dg_ref, mult):
            dn = jnp.where(fox, df_ref[...], ds_ref[...]) * mult
            xh, r = _norm_pair(x_ref[...])

            @pl.when(i == 0)
            def _():
                dg_ref[...] = jnp.zeros_like(dg_ref)

            dg_ref[0, 0:1, :] += jnp.sum(dn * xh, axis=0, keepdims=True)
            dxh = dn * g_ref[0]
            dx_ref[...] = (r * (dxh - xh * _mean_pair(dxh * xh))).astype(BF16)

        one(q_ref, gq_ref, dqf_ref, dqs_ref, dq_ref, dgq_ref, scale)
        one(k_ref, gk_ref, dkf_ref, dks_ref, dk_ref, dgk_ref, 1.0)
        dv_ref[...] = jnp.where(fox, dvf_ref[...], dvs_ref[...]).astype(BF16)

    gain = pl.BlockSpec((1, 1, LANES), lambda p, i: (p, 0, 0))
    tile = pl.BlockSpec((ts, LANES), lambda p, i: (i, p))
    fpart = pl.BlockSpec((ts, LANES), lambda p, i: (i, jnp.minimum(p, half - 1)))
    spart = pl.BlockSpec((ts, LANES), lambda p, i: (i, jnp.maximum(p - half, 0)))
    dgain = pl.BlockSpec((1, 8, LANES), lambda p, i: (p, 0, 0))
    out = jax.ShapeDtypeStruct((s, MIX), BF16)
    gshape = jax.ShapeDtypeStruct((npair, 8, LANES), F32)
    dq, dk, dv, dgq, dgk = pl.pallas_call(
        body, name=name, grid=(npair, s // ts),
        in_specs=[tile, pl.BlockSpec((ts, LANES), lambda p, i: (i, p + npair)), gain, gain,
                  fpart, spart, fpart, spart, fpart, spart],
        out_specs=[tile, tile, tile, dgain, dgain], out_shape=[out, out, out, gshape, gshape],
        compiler_params=_params("parallel", "arbitrary"),
    )(proj, proj, gq, gk, dqs[0], dqs[1], dks[0], dks[1], dvs[0], dvs[1])
    return dq, dk, dv, dgq[:, 0, :], dgk[:, 0, :]


def _tri(n, rel):
    a = lax.broadcasted_iota(jnp.int32, (n, n), 0)
    b = lax.broadcasted_iota(jnp.int32, (n, n), 1)
    return rel(a, b).astype(BF16)


def _fgate_fwd(logit, bias, name):
    nh, r, _ = logit.shape

    def body(x_ref, b_ref, o_ref):
        within = _tri(LANES, lambda a, b: a <= b)
        before = _tri(r, lambda a, b: b < a)
        for hh in range(nh):
            x = x_ref[hh] + b_ref[hh]
            lf = jnp.minimum(x, 0.0) - jnp.log1p(jnp.exp(-jnp.abs(x)))
            c = _split_dot(lf, within, 3)
            tot = jnp.broadcast_to(c[:, LANES - 1:LANES], (r, LANES))
            o_ref[hh] = c + _split_dot_left(before, tot, 3)

    return pl.pallas_call(
        body, name=name, out_shape=jax.ShapeDtypeStruct((nh, r, LANES), F32),
        in_specs=[pl.BlockSpec(memory_space=pltpu.VMEM), pl.BlockSpec(memory_space=pltpu.SMEM)],
        out_specs=pl.BlockSpec(memory_space=pltpu.VMEM),
    )(logit, bias)


def _fgate_bwd(logit, bias, dcum, name):
    nh, r, _ = logit.shape

    def body(x_ref, b_ref, d_ref, dx_ref, db_ref):
        within = _tri(LANES, lambda a, b: a >= b)
        after = _tri(r, lambda a, b: b > a)
        for hh in range(nh):
            x = x_ref[hh] + b_ref[hh]
            d = d_ref[hh]
            c = _split_dot(d, within, 3)
            tot = jnp.broadcast_to(c[:, 0:1], (r, LANES))
            dlf = c + _split_dot_left(after, tot, 3)
            dx = dlf * (1.0 / (1.0 + jnp.exp(x)))
            dx_ref[hh] = dx
            db_ref[hh:hh + 1, :] = jnp.broadcast_to(jnp.sum(dx, keepdims=True).reshape(1, 1), (1, LANES))

    return pl.pallas_call(
        body, name=name,
        out_shape=[jax.ShapeDtypeStruct((nh, r, LANES), F32), jax.ShapeDtypeStruct((nh, LANES), F32)],
        in_specs=[pl.BlockSpec(memory_space=pltpu.VMEM), pl.BlockSpec(memory_space=pltpu.SMEM),
                  pl.BlockSpec(memory_space=pltpu.VMEM)],
        out_specs=[pl.BlockSpec(memory_space=pltpu.VMEM), pl.BlockSpec(memory_space=pltpu.VMEM)],
    )(logit, bias, dcum)


def _pair_masks(x):
    lo = _low_lanes(x.shape)
    zero = jnp.zeros_like(x)
    return jnp.where(lo, x, zero), jnp.where(lo, zero, x)


def _pair_masks_t(x):
    top = _top_rows(x.shape)
    zero = jnp.zeros_like(x)
    return jnp.where(top, x, zero), jnp.where(top, zero, x)


def _stack_heads(x):
    return jnp.concatenate(_pair_masks(x), axis=0)


def _stack_heads_t(x):
    return jnp.concatenate(_pair_masks_t(x), axis=1)


def _pair_colsum_t(x):
    top = _top_rows(x.shape)
    return (jnp.sum(jnp.where(top, x, 0.0), axis=0, keepdims=True),
            jnp.sum(jnp.where(top, 0.0, x), axis=0, keepdims=True))


def _key_query_iotas(t):
    return lax.broadcasted_iota(jnp.int32, (t, t), 0), lax.broadcasted_iota(jnp.int32, (t, t), 1)


def _walk_blocks(i, step, descending, group=2):
    full = i // group
    left = i - full * group

    def run(first, count):
        sign = -1 if descending else 1
        step([(first + sign * n, False) for n in range(count)])

    def leftovers():
        start = (left - 1) if descending else full * group
        sign = -1 if descending else 1
        if group == 4:
            @pl.when(left >= 2)
            def _():
                run(start, 2)

            @pl.when(left % 2 == 1)
            def _():
                run(0 if descending else i - 1, 1)
        else:
            @pl.when(left == 1)
            def _():
                run(start, 1)

    def loop(g, carry):
        run((i - 1 - group * g) if descending else group * g, group)
        return carry

    if descending:
        step([(i, True)])
        lax.fori_loop(0, full, loop, 0)
        leftovers()
        return
    total = i + 1
    groups = total // group
    rest = total - groups * group
    lax.fori_loop(0, jnp.where(rest == 0, groups - 1, groups), loop, 0)

    def closing(count):
        step([(i - count + 1 + n, n == count - 1) for n in range(count)])

    pl.when(rest == 0)(lambda: closing(group))
    if group == 4:
        pl.when(rest == 3)(lambda: run(i - 2, 2))
        pl.when(rest == 2)(lambda: closing(2))
        pl.when((rest == 1) | (rest == 3))(lambda: closing(1))
    else:
        pl.when(rest == 1)(lambda: closing(1))


def _attn_specs(s, tq, pair0):
    q_nat = pl.BlockSpec((tq, LANES), lambda p, i: (i, p + pair0))
    q_t = pl.BlockSpec((LANES, tq), lambda p, i: (p + pair0, i))
    k_nat = pl.BlockSpec((s, LANES), lambda p, i: (0, p + pair0))
    k_t = pl.BlockSpec((LANES, s), lambda p, i: (p + pair0, 0))
    return q_nat, q_t, k_nat, k_t


def _fox_fwd(qt, kh, vt, frow, fcol, name, tq=256):
    s = kh.shape[0]
    tq = _tile(s, tq)
    nq, half = s // tq, MIX // LANES // 2

    def body(qt_ref, k_ref, vt_ref, fr_ref, fc_ref, o_ref, lse_ref, m_s, l_s, acc_s):
        i = pl.program_id(1)
        qt_v = qt_ref[...]
        ft = fr_ref[0]
        key, qry = _key_query_iotas(tq)
        causal = key <= qry
        m_s[...] = jnp.full(m_s.shape, NEG, F32)
        l_s[...] = jnp.zeros_like(l_s)
        acc_s[...] = jnp.zeros_like(acc_s)

        top = _top_rows((LANES, tq))

        def step(blocks):
            rows = [pl.ds(pl.multiple_of(j * tq, tq), tq) for j, _ in blocks]
            zs = [_dot(_stack_heads(k_ref[r, :]), qt_v) for r in rows]
            m_cur, l_cur = [m_s[0], m_s[1]], [l_s[0], l_s[1]]
            acc = acc_s[...]
            for b, (_, masked) in enumerate(blocks):
                fk = fc_ref[0, rows[b], :]
                prs, alphas = [], []
                for hh in range(2):
                    sc = zs[b][hh * tq:(hh + 1) * tq] + (ft[hh:hh + 1, :] - fk[:, hh:hh + 1])
                    if masked:
                        sc = jnp.where(causal, sc, NEG)
                    m_new = jnp.maximum(m_cur[hh], jnp.max(sc, axis=0, keepdims=True))
                    alpha = jnp.exp(m_cur[hh] - m_new)
                    pr = jnp.exp(sc - m_new)
                    l_cur[hh] = alpha * l_cur[hh] + jnp.sum(pr, axis=0, keepdims=True)
                    m_cur[hh] = m_new
                    prs.append(pr.astype(BF16))
                    alphas.append(alpha)
                pv = _dot(_stack_heads_t(vt_ref[:, rows[b]]), jnp.concatenate(prs, axis=0))
                acc = jnp.where(top, alphas[0], alphas[1]) * acc + pv
            acc_s[...] = acc
            for hh in range(2):
                m_s[hh] = m_cur[hh]
                l_s[hh] = l_cur[hh]

        _walk_blocks(i, step, descending=False, group=4)
        o_ref[...] = (acc_s[...] / jnp.where(top, l_s[0], l_s[1])).T
        lse_ref[0, 0:1, :] = m_s[0] + jnp.log(l_s[0])
        lse_ref[0, 1:2, :] = m_s[1] + jnp.log(l_s[1])

    _, q_t, k_nat, k_t = _attn_specs(s, tq, 0)
    qstat = pl.BlockSpec((1, 2, tq), lambda p, i: (p, 0, i))
    return pl.pallas_call(
        body, name=name, grid=(half, nq),
        in_specs=[q_t, k_nat, k_t, qstat, pl.BlockSpec((1, s, 2), lambda p, i: (p, 0, 0))],
        out_specs=[pl.BlockSpec((tq, LANES), lambda p, i: (i, p)), qstat],
        out_shape=[jax.ShapeDtypeStruct((s, MIX // 2), F32), jax.ShapeDtypeStruct((half, 2, s), F32)],
        scratch_shapes=[pltpu.VMEM((2, 1, tq), F32), pltpu.VMEM((2, 1, tq), F32), pltpu.VMEM((LANES, tq), F32)],
        compiler_params=_params("parallel", "arbitrary"),
    )(qt, kh, vt, frow, fcol)


def _fox_bwd(qh, qt, kh, kt, vb, frow, fcol, lse, o, do, name, tq=256):
    s = kh.shape[0]
    tq = _tile(s, tq)
    nq, half = s // tq, MIX // LANES // 2

    def body(q_ref, qt_ref, k_ref, kt_ref, v_ref, fr_ref, fc_ref, lse_ref, o_ref, do_ref,
             dq_ref, dk_ref, dv_ref, dfk_ref, dfq_ref, dq_s, rs_s):
        i = pl.program_id(1)

        @pl.when(i == 0)
        def _():
            dk_ref[...] = jnp.zeros_like(dk_ref)
            dv_ref[...] = jnp.zeros_like(dv_ref)
            dfk_ref[...] = jnp.zeros_like(dfk_ref)

        q2 = _stack_heads(q_ref[...])
        qt_v = qt_ref[...]
        do_v = do_ref[...]
        do2 = _stack_heads(do_v.astype(BF16))
        dot_v = do_v.T.astype(BF16)
        dsum = _pair_colsum_t((do_v * o_ref[...]).T)
        ft, ls = fr_ref[0], lse_ref[0]
        key, qry = _key_query_iotas(tq)
        causal = key <= qry
        lane = lax.broadcasted_iota(jnp.int32, (2 * tq, LANES), 0) // tq
        pick2 = (lax.broadcasted_iota(jnp.int32, (2 * tq, LANES), 1) == lane).astype(BF16)
        q2_pick = jnp.concatenate([q2, pick2], axis=1)
        dq_s[...] = jnp.zeros_like(dq_s)
        rs_s[...] = jnp.zeros_like(rs_s)

        def step(blocks):
            rows = [pl.ds(pl.multiple_of(j * tq, tq), tq) for j, _ in blocks]
            zs = [_dot(_stack_heads(k_ref[r, :]), qt_v) for r in rows]
            dps = [_dot(_stack_heads(v_ref[r, :]), dot_v) for r in rows]
            rs = [rs_s[0], rs_s[1]]
            dq = None
            for b, (_, masked) in enumerate(blocks):
                fk = fc_ref[0, rows[b], :]
                prs, dss = [], []
                for hh in range(2):
                    blk = slice(hh * tq, (hh + 1) * tq)
                    sc = zs[b][blk] + (ft[hh:hh + 1, :] - fk[:, hh:hh + 1])
                    pr = jnp.exp(sc - ls[hh:hh + 1, :])
                    if masked:
                        pr = jnp.where(causal, pr, 0.0)
                    dsb = (pr * (dps[b][blk] - dsum[hh])).astype(BF16)
                    rs[hh] = rs[hh] + jnp.sum(dsb.astype(F32), axis=0, keepdims=True)
                    prs.append(pr.astype(BF16))
                    dss.append(dsb)
                dv_ref[rows[b], :] += _dot(jnp.concatenate(prs, axis=1), do2)
                both = _dot(jnp.concatenate(dss, axis=1), q2_pick)
                dk_ref[rows[b], :] += both[:, :LANES]
                dfk_ref[0, rows[b], :] -= both[:, LANES:]
                term = _dot(_stack_heads_t(kt_ref[:, rows[b]]), jnp.concatenate(dss, axis=0))
                dq = term if dq is None else dq + term
            rs_s[0], rs_s[1] = rs
            dq_s[...] += dq

        _walk_blocks(i, step, descending=False, group=4)
        dq_ref[...] = dq_s[...].T
        dfq_ref[0, 0:1, :] = rs_s[0]
        dfq_ref[0, 1:2, :] = rs_s[1]

    q_nat, q_t, k_nat, k_t = _attn_specs(s, tq, 0)
    qstat = pl.BlockSpec((1, 2, tq), lambda p, i: (p, 0, i))
    otile = pl.BlockSpec((tq, LANES), lambda p, i: (i, p))
    oresident = pl.BlockSpec((s, LANES), lambda p, i: (0, p))
    out = jax.ShapeDtypeStruct((s, MIX // 2), F32)
    return pl.pallas_call(
        body, name=name, grid=(half, nq),
        in_specs=[q_nat, q_t, k_nat, k_t, k_nat, qstat, pl.BlockSpec((1, s, 2), lambda p, i: (p, 0, 0)), qstat,
                  otile, q_nat],
        out_specs=[otile, oresident, oresident, pl.BlockSpec((1, s, LANES), lambda p, i: (p, 0, 0)), qstat],
        out_shape=[out, out, out, jax.ShapeDtypeStruct((half, s, LANES), F32),
                   jax.ShapeDtypeStruct((half, 2, s), F32)],
        scratch_shapes=[pltpu.VMEM((LANES, tq), F32), pltpu.VMEM((2, 1, tq), F32)],
        compiler_params=_params("parallel", "arbitrary"),
    )(qh, qt, kh, kt, vb, frow, fcol, lse, o, do)


def _log_sig_pair(z):
    zc = jnp.maximum(z, -80.0)
    lb = -jnp.log(1.0 + jnp.exp(-zc))
    return lb, lb - zc


def _sb_fwd(qt, kh, vt, name, tq=256, gather=()):
    s = kh.shape[0]
    tq = _tile(s, tq)
    nq, half = s // tq, MIX // LANES // 2
    ng = len(gather)

    def body(*refs):
        qt_ref, k_ref, vt_ref = refs[:3]
        o_ref, tot_ref = refs[3 + ng:5 + ng]
        c_s, acc_s = refs[5 + 2 * ng:7 + 2 * ng]
        p, i = pl.program_id(0), pl.program_id(1)
        if ng:
            start, relay, finish = _gather_stages(refs[3:3 + ng], refs[5 + ng:5 + 2 * ng], *refs[7 + 2 * ng:])
            pl.when((p == 0) & (i == 0))(start)
            pl.when((p == half - 1) & (i == 0))(relay)
        qt_v = qt_ref[...]
        key, qry = _key_query_iotas(tq)
        strict = key < qry
        later = _tri(tq, lambda a, b: b > a)
        c_s[...] = jnp.zeros_like(c_s)
        acc_s[...] = jnp.zeros_like(acc_s)

        def step(blocks):
            rows = [pl.ds(pl.multiple_of(j * tq, tq), tq) for j, _ in blocks]
            zs = [_dot(_stack_heads(k_ref[r, :]), qt_v) for r in rows]
            lbs, loms, afters = [], [], []
            for b, (_, masked) in enumerate(blocks):
                for hh in range(2):
                    lb, lom = _log_sig_pair(zs[b][hh * tq:(hh + 1) * tq])
                    if masked:
                        lom = jnp.where(strict, lom, 0.0)
                    lbs.append(lb)
                    loms.append(lom)
                afters.append(_split_dot_left(later, jnp.concatenate(loms[2 * b:2 * b + 2], axis=1), 2))
            carry = [c_s[0], c_s[1]]
            pv = None
            for b, (_, masked) in enumerate(blocks):
                ws = []
                for hh in range(2):
                    n = 2 * b + hh
                    w = jnp.exp(lbs[n] + afters[b][:, hh * tq:(hh + 1) * tq] + carry[hh])
                    if masked:
                        w = jnp.where(strict, w, 0.0)
                    ws.append(w.astype(BF16))
                    carry[hh] = carry[hh] + jnp.sum(loms[n], axis=0, keepdims=True)
                term = _dot(_stack_heads_t(vt_ref[:, rows[b]]), jnp.concatenate(ws, axis=0))
                pv = term if pv is None else pv + term
            c_s[0], c_s[1] = carry
            acc_s[...] += pv

        _walk_blocks(i, step, descending=True, group=4)
        o_ref[...] = acc_s[...].T
        tot_ref[0, 0:1, :] = c_s[0]
        tot_ref[0, 1:2, :] = c_s[1]
        if ng:
            pl.when((p == half - 1) & (i == nq - 1))(finish)

    _, q_t, k_nat, k_t = _attn_specs(s, tq, half)
    qstat = pl.BlockSpec((1, 2, tq), lambda p, i: (p, 0, i))
    return pl.pallas_call(
        body, name=name, grid=(half, nq),
        in_specs=[q_t, k_nat, k_t] + [ANY] * ng,
        out_specs=[pl.BlockSpec((tq, LANES), lambda p, i: (i, p)), qstat] + [ANY] * ng,
        out_shape=[jax.ShapeDtypeStruct((s, MIX // 2), F32), jax.ShapeDtypeStruct((half, 2, s), F32)]
        + [jax.ShapeDtypeStruct((N_DEV,) + x.shape, x.dtype) for x in gather],
        scratch_shapes=[pltpu.VMEM((2, 1, tq), F32), pltpu.VMEM((LANES, tq), F32)]
        + (_gather_scratch(ng) if ng else []),
        compiler_params=_params("arbitrary", "arbitrary") if ng else _params("parallel", "arbitrary"),
    )(qt, kh, vt, *gather)


def _sb_bwd(qh, qt, kh, kt, vb, tot, do, name, tq=256, exchange=()):
    s = kh.shape[0]
    tq = _tile(s, tq)
    nq, half = s // tq, MIX // LANES // 2
    nx = len(exchange)

    def body(*refs):
        q_ref, qt_ref, k_ref, kt_ref, v_ref, tot_ref, do_ref = refs[:7]
        dq_ref, dk_ref, dv_ref = refs[7 + nx:10 + nx]
        rem_s, pg_s, dq_s = refs[10 + 2 * nx:13 + 2 * nx]
        p, i = pl.program_id(0), pl.program_id(1)
        if nx:
            start, finish = _chip_exchange_stages(refs[7:7 + nx], refs[10 + nx:10 + 2 * nx], *refs[13 + 2 * nx:])
            pl.when((p == 0) & (i == 0))(start)

        @pl.when(i == 0)
        def _():
            dk_ref[...] = jnp.zeros_like(dk_ref)
            dv_ref[...] = jnp.zeros_like(dv_ref)

        q2 = _stack_heads(q_ref[...])
        qt_v = qt_ref[...]
        do_v = do_ref[...]
        do2 = _stack_heads(do_v.astype(BF16))
        dot_v = do_v.T.astype(BF16)
        key, qry = _key_query_iotas(tq)
        strict = key < qry
        upto = _tri(tq, lambda a, b: b <= a)
        before = _tri(tq, lambda a, b: b < a)
        tv = tot_ref[0]
        rem_s[0] = tv[0:1, :]
        rem_s[1] = tv[1:2, :]
        pg_s[...] = jnp.zeros_like(pg_s)
        dq_s[...] = jnp.zeros_like(dq_s)

        def step(blocks):
            nb = len(blocks)
            rows = [pl.ds(pl.multiple_of(j * tq, tq), tq) for j, _ in blocks]
            zs = [_dot(_stack_heads(k_ref[r, :]), qt_v) for r in rows]
            dws = [_dot(_stack_heads(v_ref[r, :]), dot_v) for r in rows]
            lbs, loms, prefixes = [], [], []
            for b, (_, masked) in enumerate(blocks):
                for hh in range(2):
                    lb, lom = _log_sig_pair(zs[b][hh * tq:(hh + 1) * tq])
                    if masked:
                        lom = jnp.where(strict, lom, 0.0)
                    lbs.append(lb)
                    loms.append(lom)
                prefixes.append(_split_dot_left(upto, jnp.concatenate(loms[2 * b:2 * b + 2], axis=1), 2))
            rem = [rem_s[0], rem_s[1]]
            ws, gs, gpres = [], [], []
            for b, (_, masked) in enumerate(blocks):
                for hh in range(2):
                    n = 2 * b + hh
                    blk = slice(hh * tq, (hh + 1) * tq)
                    w = jnp.exp(lbs[n] + (rem[hh] - prefixes[b][:, blk]))
                    if masked:
                        w = jnp.where(strict, w, 0.0)
                    gs.append(dws[b][blk] * w)
                    ws.append(w.astype(BF16))
                    rem[hh] = rem[hh] - jnp.sum(loms[n], axis=0, keepdims=True)
                gpres.append(_dot(before, jnp.concatenate(gs[2 * b:2 * b + 2], axis=1).astype(BF16)))
                dv_ref[rows[b], :] += _dot(jnp.concatenate(ws[2 * b:2 * b + 2], axis=1), do2)
            rem_s[0], rem_s[1] = rem
            pg = [pg_s[0], pg_s[1]]
            dq = None
            for b, (_, masked) in enumerate(blocks):
                dzs = []
                for hh in range(2):
                    n = 2 * b + hh
                    g = gs[n]
                    dz = g - jnp.exp(lbs[n]) * (g + (pg[hh] + gpres[b][:, hh * tq:(hh + 1) * tq]))
                    if masked:
                        dz = jnp.where(strict, dz, 0.0)
                    dzs.append(dz.astype(BF16))
                    pg[hh] = pg[hh] + jnp.sum(g, axis=0, keepdims=True)
                dk_ref[rows[b], :] += _dot(jnp.concatenate(dzs, axis=1), q2)
                term = _dot(_stack_heads_t(kt_ref[:, rows[b]]), jnp.concatenate(dzs, axis=0))
                dq = term if dq is None else dq + term
            pg_s[0], pg_s[1] = pg
            dq_s[...] += dq

        _walk_blocks(i, step, descending=False, group=2)
        dq_ref[...] = dq_s[...].T
        if nx:
            pl.when((p == half - 1) & (i == nq - 1))(finish)

    q_nat, q_t, k_nat, k_t = _attn_specs(s, tq, half)
    qstat = pl.BlockSpec((1, 2, tq), lambda p, i: (p, 0, i))
    otile = pl.BlockSpec((tq, LANES), lambda p, i: (i, p))
    oresident = pl.BlockSpec((s, LANES), lambda p, i: (0, p))
    out = jax.ShapeDtypeStruct((s, MIX // 2), F32)
    return pl.pallas_call(
        body, name=name, grid=(half, nq),
        in_specs=[q_nat, q_t, k_nat, k_t, k_nat, qstat, q_nat] + [ANY] * nx,
        out_specs=[otile, oresident, oresident] + [ANY] * nx,
        out_shape=[out, out, out] + [jax.ShapeDtypeStruct(x.shape, x.dtype) for x in exchange],
        scratch_shapes=[pltpu.VMEM((2, 1, tq), F32), pltpu.VMEM((2, 1, tq), F32), pltpu.VMEM((LANES, tq), F32)]
        + (_chip_exchange_scratch(nx) if nx else []),
        compiler_params=_params("arbitrary", "arbitrary") if nx else _params("parallel", "arbitrary"),
    )(qh, qt, kh, kt, vb, tot, do, *exchange)


def _loss_head(y, target, name, ts=512):
    s, d = y.shape
    ts = _tile(s, ts, 8)
    nt = s // ts

    def body(y_ref, t_ref, dy_ref, l_ref, acc):
        i = pl.program_id(0)
        err = y_ref[...] - t_ref[...]
        dy_ref[...] = err * (1.0 / d)

        @pl.when(i == 0)
        def _():
            acc[...] = jnp.zeros_like(acc)

        acc[...] += jnp.sum(err * err, axis=0, keepdims=True)

        @pl.when(i == nt - 1)
        def _():
            tot = jnp.sum(acc[...], keepdims=True).reshape(1, 1) * (0.5 / d)
            l_ref[...] = jnp.broadcast_to(tot, l_ref.shape)

    row = pl.BlockSpec((ts, d), lambda i: (i, 0))
    dy, l = pl.pallas_call(
        body, name=name, grid=(nt,), in_specs=[row, row],
        out_specs=[row, pl.BlockSpec((8, LANES), lambda i: (0, 0))],
        out_shape=[jax.ShapeDtypeStruct((s, d), F32), jax.ShapeDtypeStruct((8, LANES), F32)],
        scratch_shapes=[pltpu.VMEM((1, d), F32)],
        compiler_params=_params("arbitrary"),
    )(y, target)
    return l[0, 0], dy


def _coords():
    return lax.axis_index("x"), lax.axis_index("y"), lax.axis_index("c")


def _other_chips(xi, yi):
    return [(1 - xi, yi), (xi, 1 - yi), (1 - xi, 1 - yi)]


def _gather_stages(x_refs, out_refs, send_sems, recv_sems, local_sems):
    n = len(x_refs)
    xi, yi, ci = _coords()
    me, sibling = (xi, yi, ci), (xi, yi, 1 - ci)
    chips = _other_chips(xi, yi)

    def slot(a, px, py, pc):
        return out_refs[a].at[4 * px + 2 * py + pc]

    def copy(a, k, block, to, src=None):
        return pltpu.make_async_remote_copy(
            src_ref=slot(a, *block) if src is None else src, dst_ref=slot(a, *block),
            send_sem=send_sems.at[a, k], recv_sem=recv_sems.at[a, k], device_id=to, device_id_type=MESH)

    def own(a):
        return pltpu.make_async_copy(x_refs[a], slot(a, *me), local_sems.at[a])

    def first(a):
        return [copy(a, 0, me, sibling, src=x_refs[a])] + [
            copy(a, 1 + j, me, (*chip, ci), src=x_refs[a]) for j, chip in enumerate(chips)]

    def passed(a, j):
        return copy(a, 4 + j, (*chips[j], ci), sibling)

    def start():
        for a in range(n):
            own(a).start()
        for a in range(n):
            for cp in first(a):
                cp.start()

    def relay():
        for j, chip in enumerate(chips):
            for a in range(n):
                copy(a, 1 + j, (*chip, ci), me).wait_recv()
                passed(a, j).start()

    def finish():
        for a in range(n):
            copy(a, 0, sibling, me).wait_recv()
            for j, chip in enumerate(chips):
                copy(a, 4 + j, (*chip, 1 - ci), me).wait_recv()
        for a in range(n):
            for cp in first(a) + [passed(a, j) for j in range(len(chips))]:
                cp.wait_send()
            own(a).wait()

    return start, relay, finish


def _gather_scratch(n):
    return [pltpu.SemaphoreType.DMA((n, 7)), pltpu.SemaphoreType.DMA((n, 7)), pltpu.SemaphoreType.DMA((n,))]


def _all_gather(xs, name):
    n = len(xs)

    def body(*refs):
        start, relay, finish = _gather_stages(refs[:n], refs[n:2 * n], *refs[2 * n:])
        start()
        relay()
        finish()

    return pl.pallas_call(
        body, name=name, out_shape=[jax.ShapeDtypeStruct((N_DEV,) + x.shape, x.dtype) for x in xs],
        in_specs=[ANY] * n, out_specs=[ANY] * n, scratch_shapes=_gather_scratch(n),
    )(*xs)


def _sibling_exchange(gs, name):
    n = len(gs)

    def body(*refs):
        g_refs, recv_refs = refs[:n], refs[n:2 * n]
        send_sems, recv_sems = refs[2 * n:]
        xi, yi, ci = _coords()
        cps = [pltpu.make_async_remote_copy(
            src_ref=g_refs[a].at[2 * chip + (1 - ci)], dst_ref=recv_refs[a].at[chip],
            send_sem=send_sems.at[a, chip], recv_sem=recv_sems.at[a, chip],
            device_id=(xi, yi, 1 - ci), device_id_type=MESH) for a in range(n) for chip in range(N_CHIP)]
        for cp in cps:
            cp.start()
        for cp in cps:
            cp.wait()

    return pl.pallas_call(
        body, name=name, out_shape=[jax.ShapeDtypeStruct((N_CHIP,) + g.shape[1:], g.dtype) for g in gs],
        in_specs=[ANY] * n, out_specs=[ANY] * n,
        scratch_shapes=[pltpu.SemaphoreType.DMA((n, N_CHIP)), pltpu.SemaphoreType.DMA((n, N_CHIP))],
    )(*gs)


def _pair_add(g, recv, ids, name, tr=256):
    _, r, c = g.shape
    tr = _tile(r, tr, 16)

    def body(ids_ref, g_ref, r_ref, p_ref, own_ref):
        kk = pl.program_id(1)
        tot = g_ref[0].astype(F32) + r_ref[0].astype(F32)
        p_ref[0] = tot.astype(BF16)

        @pl.when(kk == ids_ref[1])
        def _():
            own_ref[...] = tot

    grid_spec = pltpu.PrefetchScalarGridSpec(
        num_scalar_prefetch=1, grid=(r // tr, N_CHIP),
        in_specs=[pl.BlockSpec((1, tr, c), lambda i, kk, ids: (2 * kk + ids[0], i, 0)),
                  pl.BlockSpec((1, tr, c), lambda i, kk, ids: (kk, i, 0))],
        out_specs=[pl.BlockSpec((1, tr, c), lambda i, kk, ids: (kk, i, 0)),
                   pl.BlockSpec((tr, c), lambda i, kk, ids: (i, 0))])
    return pl.pallas_call(
        body, name=name, grid_spec=grid_spec,
        out_shape=[jax.ShapeDtypeStruct((N_CHIP, r, c), BF16), jax.ShapeDtypeStruct((r, c), F32)],
        compiler_params=_params("parallel", "arbitrary"),
    )(ids, g, recv)


def _chip_exchange_stages(p_refs, recv_refs, send_sems, recv_sems):
    n = len(p_refs)
    xi, yi, ci = _coords()
    mine = 2 * xi + yi
    chips = _other_chips(xi, yi)

    def copy(a, k, cx, cy):
        return pltpu.make_async_remote_copy(
            src_ref=p_refs[a].at[2 * cx + cy], dst_ref=recv_refs[a].at[mine],
            send_sem=send_sems.at[a, k], recv_sem=recv_sems.at[a, k],
            device_id=(cx, cy, ci), device_id_type=MESH)

    def landed(a, k, cx, cy):
        return pltpu.make_async_remote_copy(
            src_ref=p_refs[a].at[mine], dst_ref=recv_refs[a].at[2 * cx + cy],
            send_sem=send_sems.at[a, k], recv_sem=recv_sems.at[a, k],
            device_id=(cx, cy, ci), device_id_type=MESH)

    def start():
        for a in range(n):
            for k, (cx, cy) in enumerate(chips):
                copy(a, k, cx, cy).start()

    def finish():
        for a in range(n):
            for k, (cx, cy) in enumerate(chips):
                landed(a, k, cx, cy).wait_recv()
        for a in range(n):
            for k, (cx, cy) in enumerate(chips):
                copy(a, k, cx, cy).wait_send()

    return start, finish


def _chip_exchange_scratch(n):
    return [pltpu.SemaphoreType.DMA((n, 3)), pltpu.SemaphoreType.DMA((n, 3))]


def _chip_exchange(ps, name):
    n = len(ps)

    def body(*refs):
        start, finish = _chip_exchange_stages(refs[:n], refs[n:2 * n], *refs[2 * n:])
        start()
        finish()

    return pl.pallas_call(
        body, name=name, out_shape=[jax.ShapeDtypeStruct(p.shape, p.dtype) for p in ps],
        in_specs=[ANY] * n, out_specs=[ANY] * n, scratch_shapes=_chip_exchange_scratch(n),
    )(*ps)


def _adamw_math(w, g, m, v):
    m = ADAM_B1 * m + (1.0 - ADAM_B1) * g
    v = ADAM_B2 * v + (1.0 - ADAM_B2) * (g * g)
    m_hat = m / (1.0 - ADAM_B1 ** ADAM_STEP)
    v_hat = v / (1.0 - ADAM_B2 ** ADAM_STEP)
    delta = -ADAM_LR * (m_hat / (jnp.sqrt(v_hat) + ADAM_EPS) + ADAM_WD * w)
    return delta, m, v


def _adamw_reduce(own, recv, ids, w, m, v, name, tr=256):
    r, c = w.shape
    tr = _tile(r, tr, 16)

    def body(ids_ref, own_ref, recv_ref, w_ref, m_ref, v_ref, g_ref, d_ref, mo_ref, vo_ref):
        mine = ids_ref[1]
        g = None
        for kk in range(N_CHIP):
            term = jnp.where(mine == kk, own_ref[...], recv_ref[kk].astype(F32))
            g = term if g is None else g + term
        delta, m_new, v_new = _adamw_math(w_ref[...], g, m_ref[...], v_ref[...])
        g_ref[...] = g
        d_ref[...] = delta
        mo_ref[...] = m_new
        vo_ref[...] = v_new

    row = pl.BlockSpec((tr, c), lambda i, ids: (i, 0))
    grid_spec = pltpu.PrefetchScalarGridSpec(
        num_scalar_prefetch=1, grid=(r // tr,),
        in_specs=[row, pl.BlockSpec((N_CHIP, tr, c), lambda i, ids: (0, i, 0)), row, row, row],
        out_specs=[row, row, row, row])
    out = jax.ShapeDtypeStruct((r, c), F32)
    return pl.pallas_call(
        body, name=name, grid_spec=grid_spec, out_shape=[out, out, out, out],
        compiler_params=_params("parallel"),
    )(ids, own, recv, w, m, v)


def _sum_sources(a, name):
    n, r, c = a.shape

    def body(a_ref, o_ref):
        tot = a_ref[0]
        for kk in range(1, n):
            tot = tot + a_ref[kk]
        o_ref[...] = tot

    return pl.pallas_call(
        body, name=name, out_shape=jax.ShapeDtypeStruct((r, c), F32),
        in_specs=[pl.BlockSpec(memory_space=pltpu.VMEM)], out_specs=pl.BlockSpec(memory_space=pltpu.VMEM),
    )(a)


def _adamw_small(w, g, m, v, name):
    def body(w_ref, g_ref, m_ref, v_ref, d_ref, mo_ref, vo_ref):
        delta, m_new, v_new = _adamw_math(w_ref[...], g_ref[...], m_ref[...], v_ref[...])
        d_ref[...] = delta
        mo_ref[...] = m_new
        vo_ref[...] = v_new

    vm = pl.BlockSpec(memory_space=pltpu.VMEM)
    out = jax.ShapeDtypeStruct(w.shape, F32)
    return pl.pallas_call(body, name=name, out_shape=[out, out, out], in_specs=[vm] * 4, out_specs=[vm] * 3)(w, g, m, v)


def _pack(parts, width, row_mult):
    flat = jnp.concatenate([p.reshape(-1) for p in parts])
    rows = -(-flat.shape[0] // width)
    rows = -(-rows // row_mult) * row_mult
    return jnp.pad(flat, (0, rows * width - flat.shape[0])).reshape(rows, width)


def _unpack(flat, shapes):
    out, off = [], 0
    lead = flat.shape[:-1]
    for shp in shapes:
        n = 1
        for dd in shp:
            n *= dd
        out.append(flat[..., off:off + n].reshape(lead + tuple(shp)))
        off += n
    return out


def _rows2d(w):
    return w.reshape(w.shape[0] * w.shape[1], w.shape[2])


def _cols_to_dev(g):
    l, k, n = g.shape
    return g.reshape(l * k, N_DEV, n // N_DEV).transpose(1, 0, 2)


def _rows_to_dev(g):
    l, k, n = g.shape
    rs = k // N_DEV
    return g.reshape(l, N_DEV, rs, n).transpose(1, 0, 2, 3).reshape(N_DEV, l * rs, n)


def _dev_to_cols(a, l):
    _, lk, cs = a.shape
    return a.transpose(1, 0, 2).reshape(l, lk // l, N_DEV * cs)


def _dev_to_rows(a, l):
    _, lr, n = a.shape
    rs = lr // l
    return a.reshape(N_DEV, l, rs, n).transpose(1, 0, 2, 3).reshape(l, N_DEV * rs, n)


def kernel(x, attn_norm, attn_w_in, attn_f_bias, fox_q_gain, fox_k_gain, sb_q_gain, sb_k_gain, attn_w_out, conv_norm, conv_w_in, conv_kernel, conv_w_out, ffn_norm, ffn_w_up, ffn_conv, ffn_w_down, loss_target, m_attn_norm, m_attn_w_in, m_attn_f_bias, m_fox_q_gain, m_fox_k_gain, m_sb_q_gain, m_sb_k_gain, m_attn_w_out, m_conv_norm, m_conv_w_in, m_conv_kernel, m_conv_w_out, m_ffn_norm, m_ffn_w_up, m_ffn_conv, m_ffn_w_down, v_attn_norm, v_attn_w_in, v_attn_f_bias, v_fox_q_gain, v_fox_k_gain, v_sb_q_gain, v_sb_k_gain, v_attn_w_out, v_conv_norm, v_conv_w_in, v_conv_kernel, v_conv_w_out, v_ffn_norm, v_ffn_w_up, v_ffn_conv, v_ffn_w_down):
    s = x.shape[1]
    n_attn, n_conv, depth = attn_w_in.shape[0], conv_w_in.shape[0], ffn_w_up.shape[0]
    xi, yi, ci = _coords()
    me = 4 * xi + 2 * yi + ci
    ids = jnp.stack([ci, 2 * xi + yi]).astype(jnp.int32)

    big = [attn_w_in, attn_w_out, conv_w_in, conv_w_out, ffn_w_up, ffn_w_down]
    big_m = [m_attn_w_in, m_attn_w_out, m_conv_w_in, m_conv_w_out, m_ffn_w_up, m_ffn_w_down]
    big_v = [v_attn_w_in, v_attn_w_out, v_conv_w_in, v_conv_w_out, v_ffn_w_up, v_ffn_w_down]
    big_names = ["attn_w_in", "attn_w_out", "conv_w_in", "conv_w_out", "ffn_w_up", "ffn_w_down"]
    small_sh = [conv_norm, conv_kernel, ffn_conv]
    small_sh_shapes = [w.shape for w in small_sh]
    rep = [attn_norm, attn_f_bias, fox_q_gain, fox_k_gain, sb_q_gain, sb_k_gain, ffn_norm]
    rep_shapes = [w.shape for w in rep]

    small_pack = _pack(small_sh, LANES, 8)
    shards_bf16 = [_rows2d(w).astype(BF16) for w in big]
    early = _all_gather([attn_w_in[0].astype(BF16), small_pack], "gather_first")
    first_w_in = jnp.pad(_dev_to_cols(early[0], 1)[0], ((0, 0), (0, ATTN_IN_PAD - ATTN_IN)))
    cn, ckern, fconv = _unpack(early[1].reshape(N_DEV, -1), small_sh_shapes)
    conv_norm_f = cn.transpose(1, 0, 2).reshape(n_conv, D_MODEL)
    conv_kernel_f = ckern.transpose(1, 2, 0, 3).reshape(n_conv, 3, D_MODEL)
    ffn_conv_f = fconv.transpose(1, 2, 0, 3).reshape(depth, 3, 2 * D_FF)

    def pair_gain(fox_g, sb_g):
        f2 = jnp.concatenate([fox_g, fox_g])
        s2 = jnp.concatenate([sb_g, sb_g])
        return jnp.concatenate([jnp.tile(f2[None], (4, 1)), jnp.tile(s2[None], (4, 1))])[:, None, :]

    h = x[0]
    saved = []
    for layer in range(depth):
        i = layer // 2
        tag = "l%d" % layer
        rec = {"h_in": h}
        if layer % 2 == 0:
            xn = xn_next if layer else _rms_fwd(h, attn_norm[i], tag + "_attn_rms")
            proj = _matmul(xn, first_w_in if layer == 0 else a_w_in[i], tag + "_attn_in", tn=640)
            gq, gk = pair_gain(fox_q_gain[i], sb_q_gain[i]), pair_gain(fox_k_gain[i], sb_k_gain[i])
            qh, kh, vb, qt, kt, vt = _qkv_prep(proj, gq, gk, tag + "_qkv_prep")
            logit = proj[:, 3 * MIX:3 * MIX + H_FOX].T.reshape(H_FOX, s // LANES, LANES)
            cum = _fgate_fwd(logit, attn_f_bias[i], tag + "_fgate")
            frow = cum.reshape(H_FOX // 2, 2, s)
            fcol = frow.transpose(0, 2, 1)
            o_fox, lse = _fox_fwd(qt, kh, vt, frow, fcol, tag + "_fox_fwd")
            if layer == 0:
                o_sb, tot, *gathered = _sb_fwd(qt, kh, vt, tag + "_sb_fwd", gather=shards_bf16)
                a_w_in = _dev_to_cols(gathered[0], n_attn)
                a_w_in = jnp.pad(a_w_in, ((0, 0), (0, 0), (0, ATTN_IN_PAD - ATTN_IN)))
                a_w_out = _dev_to_rows(gathered[1], n_attn)
                c_w_in = _dev_to_cols(gathered[2], n_conv)
                c_w_out = _dev_to_rows(gathered[3], n_conv)
                f_w_up = _dev_to_cols(gathered[4], depth)
                f_w_down = _dev_to_rows(gathered[5], depth)
            else:
                o_sb, tot = _sb_fwd(qt, kh, vt, tag + "_sb_fwd")
            o = jnp.concatenate([o_fox, o_sb], axis=1)
            h, xn2 = _matmul_rows(o, a_w_out[i], tag + "_attn_out", add=h, norm_gain=ffn_norm[layer])
            rec.update(xn=xn, proj=proj, gq=gq, gk=gk, qh=qh, kh=kh, vb=vb, qt=qt, kt=kt, logit=logit, frow=frow,
                       fcol=fcol, o_fox=o_fox, lse=lse, tot=tot, o=o)
        else:
            xn = xn_next
            proj = _matmul(xn, c_w_in[i], tag + "_conv_in", tn=1024)
            y = _sconv_fwd(proj, conv_kernel_f[i], tag + "_sconv_fwd")
            h, xn2 = _matmul_rows(y, c_w_out[i], tag + "_conv_out", add=h, norm_gain=ffn_norm[layer])
            rec.update(xn=xn, proj=proj, y=y)
        rec["h_mid"] = h
        up = _matmul(xn2, f_w_up[layer], tag + "_ffn_up", tn=1408)
        act = _ffn_act_fwd(up, ffn_conv_f[layer], tag + "_ffn_act")
        if layer + 1 < depth:
            nxt = layer + 1
            gain = attn_norm[nxt // 2] if nxt % 2 == 0 else conv_norm_f[nxt // 2]
            h, xn_next = _matmul_rows(act, f_w_down[layer], tag + "_ffn_down", add=h, norm_gain=gain, tk=2816)
        else:
            h = _matmul(act, f_w_down[layer], tag + "_ffn_down", add=h, tn=1024, tk=2816)
        rec.update(xn2=xn2, up=up, act=act)
        saved.append(rec)

    loss_local, dh = _loss_head(h, loss_target[0], "loss_head")
    loss = lax.psum(loss_local, ("x", "y", "c"))

    g_attn_norm, g_attn_w_in, g_f_bias = [None] * n_attn, [None] * n_attn, [None] * n_attn
    g_fq, g_fk, g_sq, g_sk, g_attn_w_out = ([None] * n_attn for _ in range(5))
    g_conv_norm, g_conv_w_in, g_conv_kernel, g_conv_w_out = ([None] * n_conv for _ in range(4))
    g_ffn_norm, g_ffn_w_up, g_ffn_conv, g_ffn_w_down = ([None] * depth for _ in range(4))
    everything = slice(0, None)
    early_layers = {nm: (slice(1, None) if nm == "attn_w_in" else everything) for nm in big_names}
    late_layers = {"attn_w_in": slice(0, 1)}

    def slabs_for_devices(layers):
        stacks = {"attn_w_in": (g_attn_w_in, _cols_to_dev), "attn_w_out": (g_attn_w_out, _rows_to_dev),
                  "conv_w_in": (g_conv_w_in, _cols_to_dev), "conv_w_out": (g_conv_w_out, _rows_to_dev),
                  "ffn_w_up": (g_ffn_w_up, _cols_to_dev), "ffn_w_down": (g_ffn_w_down, _rows_to_dev)}
        return [stacks[nm][1](jnp.stack(stacks[nm][0][layers[nm]]).astype(BF16)) for nm in big_names if nm in layers]

    for layer in reversed(range(depth)):
        i = layer // 2
        tag = "l%d" % layer
        rec = saved[layer]
        da = _matmul(dh, f_w_down[layer].T, tag + "_ffn_down_dx", tn=1408)
        g_ffn_w_down[layer] = _matmul_tn(rec["act"], dh, tag + "_ffn_down_dw", tm=1408, tn=1024)
        dup, dwg, dwv = _ffn_act_bwd(rec["up"], ffn_conv_f[layer], da, tag + "_ffn_act_bwd")
        g_ffn_conv[layer] = jnp.concatenate([dwg, dwv], axis=1)
        g_ffn_w_up[layer] = _matmul_tn(rec["xn2"], dup, tag + "_ffn_up_dw", tn=1408)
        dh, g_ffn_norm[layer] = _matmul_rows(dup, f_w_up[layer].T, tag + "_ffn_up_dx", tk=2816,
                                             rms_bwd=(rec["h_mid"], ffn_norm[layer], dh))
        if layer % 2 == 0:
            do = _matmul(dh, a_w_out[i].T, tag + "_attn_out_dx", tn=1024)
            g_attn_w_out[i] = _matmul_tn(rec["o"], dh, tag + "_attn_out_dw", tn=1024)
            if layer == 0:
                early_g = slabs_for_devices(early_layers)
                early_pairs = [_pair_add(g, r, ids, "reduce_pair_add_" + nm)
                               for g, r, nm in zip(early_g, _sibling_exchange(early_g, "reduce_sibling"), big_names)]
            dq_f, dk_f, dv_f, dfk, dfq = _fox_bwd(rec["qh"], rec["qt"], rec["kh"], rec["kt"], rec["vb"], rec["frow"],
                                                  rec["fcol"], rec["lse"], rec["o_fox"], do, tag + "_fox_bwd")
            dq_s, dk_s, dv_s, *from_chips = _sb_bwd(
                rec["qh"], rec["qt"], rec["kh"], rec["kt"], rec["vb"], rec["tot"], do, tag + "_sb_bwd",
                exchange=[pr[0] for pr in early_pairs] if layer == 0 else ())
            if layer == 0:
                early_from_chips = from_chips
            dq, dk, dv, dgq, dgk = _qkv_prep_bwd(rec["proj"], rec["gq"], rec["gk"], (dq_f, dq_s), (dk_f, dk_s),
                                                 (dv_f, dv_s), tag + "_qkv_prep_bwd")
            dcum = (dfq + dfk[:, :, 0:2].transpose(0, 2, 1)).reshape(H_FOX, s // LANES, LANES)
            dlogit, dbias = _fgate_bwd(rec["logit"], attn_f_bias[i], dcum, tag + "_fgate_bwd")
            g_f_bias[i] = dbias[:, 0]
            dgate = jnp.pad(dlogit.reshape(H_FOX, s).T, ((0, 0), (0, LANES - H_FOX))).astype(BF16)
            dproj = jnp.concatenate([dq, dk, dv, dgate], axis=1)

            def fold(dg):
                per_head = dg.reshape(16, HEAD_DIM)
                return jnp.sum(per_head[:8], axis=0), jnp.sum(per_head[8:], axis=0)

            g_fq[i], g_sq[i] = fold(dgq)
            g_fk[i], g_sk[i] = fold(dgk)
            g_attn_w_in[i] = _matmul_tn(rec["xn"], dproj, tag + "_attn_in_dw", tn=640)[:, :ATTN_IN]
            dh, g_attn_norm[i] = _matmul_rows(dproj, a_w_in[i].T, tag + "_attn_in_dx", tk=3200,
                                              rms_bwd=(rec["h_in"], attn_norm[i], dh))
        else:
            dy = _matmul(dh, c_w_out[i].T, tag + "_conv_out_dx", tn=1024)
            g_conv_w_out[i] = _matmul_tn(rec["y"], dh, tag + "_conv_out_dw", tn=1024)
            dproj, g_conv_kernel[i] = _sconv_bwd(rec["proj"], conv_kernel_f[i], dy, tag + "_sconv_bwd")
            g_conv_w_in[i] = _matmul_tn(rec["xn"], dproj, tag + "_conv_in_dw", tn=1024)
            dh, g_conv_norm[i] = _matmul_rows(dproj, c_w_in[i].T, tag + "_conv_in_dx", tk=1024,
                                              rms_bwd=(rec["h_in"], conv_norm_f[i], dh))
    grad_x = dh[None]

    late_names = [nm for nm in big_names if nm in late_layers]
    late_g = slabs_for_devices(late_layers)
    late_pairs = [_pair_add(g, r, ids, "reduce_pair_add_late_" + nm)
                  for g, r, nm in zip(late_g, _sibling_exchange(late_g, "reduce_sibling_late"), late_names)]
    late_from_chips = _chip_exchange([pr[0] for pr in late_pairs], "reduce_chips_late")

    def update(piece_pairs, piece_recv, piece_names, layers):
        outs = {}
        for (own, recv, nm) in zip([pr[1] for pr in piece_pairs], piece_recv, piece_names):
            sl = layers[nm]
            which = big_names.index(nm)
            outs[nm] = _adamw_reduce(own, recv, ids, _rows2d(big[which][sl]), _rows2d(big_m[which][sl]),
                                     _rows2d(big_v[which][sl]), "adamw_%s_%d" % (nm, sl.start))
        return outs

    late_out = update(late_pairs, late_from_chips, late_names, late_layers)
    early_out = update(early_pairs, early_from_chips, big_names, early_layers)
    grads_big, delta_big, newm_big, newv_big = [], [], [], []
    for which, nm in enumerate(big_names):
        shp = big[which].shape
        for k, dest in enumerate((grads_big, delta_big, newm_big, newv_big)):
            parts = ([late_out[nm][k]] if nm in late_out else []) + [early_out[nm][k]]
            dest.append(jnp.concatenate(parts, axis=0).reshape(shp))

    rep_g = [jnp.stack(g_attn_norm), jnp.stack(g_f_bias), jnp.stack(g_fq), jnp.stack(g_fk), jnp.stack(g_sq),
             jnp.stack(g_sk), jnp.stack(g_ffn_norm)]
    sh_g = [jnp.stack(g_conv_norm).reshape(n_conv, N_DEV, -1).transpose(1, 0, 2),
            jnp.stack(g_conv_kernel).reshape(n_conv, 3, N_DEV, -1).transpose(2, 0, 1, 3),
            jnp.stack(g_ffn_conv).reshape(depth, 3, N_DEV, -1).transpose(2, 0, 1, 3)]
    n_rep = sum(int(a.size) for a in rep)
    n_sh = sum(int(a.size) for a in small_sh)
    partial = _pack(rep_g + [jnp.concatenate([a.reshape(N_DEV, -1) for a in sh_g], axis=1)], LANES, 8)
    total = _sum_sources(_all_gather([partial], "gather_small_grads")[0], "sum_small_grads").reshape(-1)
    rep_tot = total[:n_rep]
    sh_tot = lax.dynamic_slice_in_dim(total[n_rep:n_rep + N_DEV * n_sh].reshape(N_DEV, n_sh), me, 1, axis=0)[0]
    g_small = _pack([rep_tot, sh_tot], LANES, 8)

    def small_pack_of(rep_list, sh_list):
        return _pack(rep_list + sh_list, LANES, 8)

    d_small, m_small, v_small = _adamw_small(
        small_pack_of(rep, small_sh), g_small,
        small_pack_of([m_attn_norm, m_attn_f_bias, m_fox_q_gain, m_fox_k_gain, m_sb_q_gain, m_sb_k_gain, m_ffn_norm],
                      [m_conv_norm, m_conv_kernel, m_ffn_conv]),
        small_pack_of([v_attn_norm, v_attn_f_bias, v_fox_q_gain, v_fox_k_gain, v_sb_q_gain, v_sb_k_gain, v_ffn_norm],
                      [v_conv_norm, v_conv_kernel, v_ffn_conv]),
        "adamw_small")
    small_shapes = rep_shapes + small_sh_shapes

    def split_small(a):
        return _unpack(a.reshape(-1), small_shapes)

    def ordered(big_list, small_list):
        an, fb, fq, fk, sq, sk, fn, cno, cke, fco = small_list
        awi, awo, cwi, cwo, fwu, fwd = big_list
        return [an, awi, fb, fq, fk, sq, sk, awo, cno, cwi, cke, cwo, fn, fwu, fco, fwd]

    grads = ordered(grads_big, split_small(g_small))
    deltas = ordered(delta_big, split_small(d_small))
    new_m = ordered(newm_big, split_small(m_small))
    new_v = ordered(newv_big, split_small(v_small))
    return (loss, grad_x, *grads, *deltas, *new_m, *new_v)
```

```python
import jax
import jax.numpy as jnp
from jax import lax
from jax.experimental import pallas as pl
from jax.experimental.pallas import tpu as pltpu

F32 = jnp.float32
BF16 = jnp.bfloat16

D_MODEL = 1024
HEAD_DIM = 64
H_FOX = 8
MIX = 1024
ATTN_IN = 3 * MIX + H_FOX
ATTN_IN_PAD = 3 * MIX + 128
D_FF = 2816
EPS = 1e-6
NEG = -1e30
LANES = 128
N_DEV = 8
N_CHIP = 4

ADAM_LR = 0.001
ADAM_B1 = 0.9
ADAM_B2 = 0.999
ADAM_EPS = 1e-08
ADAM_WD = 0.01
ADAM_STEP = 10

VMEM_LIMIT = 56 * 1024 * 1024
MESH = pl.DeviceIdType.MESH
ANY = pl.BlockSpec(memory_space=pl.ANY)


def _params(*sem):
    return pltpu.CompilerParams(dimension_semantics=sem, vmem_limit_bytes=VMEM_LIMIT)


def _tile(n, target, mult=LANES):
    best = None
    for t in range(mult, min(n, target) + 1, mult):
        if n % t == 0:
            best = t
    return best if best is not None else n


def _dot(a, b):
    return jnp.dot(a, b, preferred_element_type=F32)


def _dot_tn(a, b):
    return lax.dot_general(a, b, (((0,), (0,)), ((), ())), preferred_element_type=F32)


def _split_dot(x, m, passes):
    acc = None
    rem = x
    for _ in range(passes):
        part = rem.astype(BF16)
        term = _dot(part, m)
        acc = term if acc is None else acc + term
        rem = rem - part.astype(F32)
    return acc


def _split_dot_left(m, x, passes):
    acc = None
    rem = x
    for _ in range(passes):
        part = rem.astype(BF16)
        term = _dot(m, part)
        acc = term if acc is None else acc + term
        rem = rem - part.astype(F32)
    return acc


def _matmul(a, b, name, add=None, out_dtype=F32, tm=1024, tn=512, tk=1024):
    split = a.shape[0] if a.ndim == 3 else 1
    m, kh = a.shape[-2:]
    k = split * kh
    n = b.shape[1]
    tm, tn, tk = _tile(m, tm, 8), _tile(n, tn), _tile(kh, tk)
    nk = k // tk
    per_slab = kh // tk
    has_add = add is not None

    def body(*refs):
        a_ref, b_ref = refs[0], refs[1]
        add_ref = refs[2] if has_add else None
        o_ref = refs[2 + has_add]

        def finish(acc):
            if has_add:
                acc = acc + add_ref[...]
            o_ref[...] = acc.astype(out_dtype)

        p = _dot(a_ref[...].astype(BF16), b_ref[...].astype(BF16))
        if nk == 1:
            finish(p)
        else:
            acc_ref = refs[-1]
            kk = pl.program_id(2)

            @pl.when(kk == 0)
            def _():
                acc_ref[...] = p

            @pl.when(kk > 0)
            def _():
                acc_ref[...] += p

            @pl.when(kk == nk - 1)
            def _():
                finish(acc_ref[...])

    if split == 1:
        a_spec = pl.BlockSpec((tm, tk), lambda i, j, kk: (i, kk))
    else:
        a_spec = pl.BlockSpec((None, tm, tk), lambda i, j, kk: (kk // per_slab, i, kk % per_slab))
    in_specs = [a_spec, pl.BlockSpec((tk, tn), lambda i, j, kk: (kk, j))]
    args = [a, b]
    if has_add:
        in_specs.append(pl.BlockSpec((tm, tn), lambda i, j, kk: (i, j)))
        args.append(add)
    return pl.pallas_call(
        body, name=name, grid=(m // tm, n // tn, nk), in_specs=in_specs,
        out_specs=pl.BlockSpec((tm, tn), lambda i, j, kk: (i, j)),
        out_shape=jax.ShapeDtypeStruct((m, n), out_dtype),
        scratch_shapes=[pltpu.VMEM((tm, tn), F32)] if nk > 1 else [],
        compiler_params=_params("parallel", "parallel", "arbitrary"),
    )(*args)


def _matmul_rows(a, b, name, add=None, norm_gain=None, rms_bwd=None, tm=512, tk=1024):
    split = a.shape[0] if a.ndim == 3 else 1
    m, kh = a.shape[-2:]
    n = b.shape[1]
    tm, tk = _tile(m, tm, 8), _tile(kh, tk)
    nk = split * kh // tk
    per_slab = kh // tk
    has_add, has_norm, has_bwd = add is not None, norm_gain is not None, rms_bwd is not None

    def rsqrt_mean_sq(x):
        return lax.rsqrt(jnp.mean(x * x, axis=-1, keepdims=True) + EPS)

    def body(*refs):
        it = iter(refs)
        a_ref, b_ref = next(it), next(it)
        add_ref = next(it) if has_add else None
        gain_ref = next(it) if has_norm else None
        h_ref, g_ref, dhin_ref = (next(it), next(it), next(it)) if has_bwd else (None, None, None)
        o_ref = next(it)
        xn_ref = next(it) if has_norm else None
        dg_ref = next(it) if has_bwd else None
        acc_ref = next(it) if nk > 1 else None
        i = pl.program_id(0)

        def finish(acc):
            if has_add:
                acc = acc + add_ref[...]
            if has_bwd:
                x = h_ref[...]
                r = rsqrt_mean_sq(x)
                xh = x * r

                @pl.when(i == 0)
                def _():
                    dg_ref[...] = jnp.zeros_like(dg_ref)

                dg_ref[0:1, :] += jnp.sum(acc * xh, axis=0, keepdims=True)
                dxh = acc * g_ref[...]
                acc = dhin_ref[...] + r * (dxh - xh * jnp.mean(dxh * xh, axis=-1, keepdims=True))
            o_ref[...] = acc
            if has_norm:
                xn_ref[...] = (acc * rsqrt_mean_sq(acc) * gain_ref[...]).astype(BF16)

        p = _dot(a_ref[...].astype(BF16), b_ref[...].astype(BF16))
        if nk == 1:
            finish(p)
        else:
            kk = pl.program_id(1)

            @pl.when(kk == 0)
            def _():
                acc_ref[...] = p

            @pl.when(kk > 0)
            def _():
                acc_ref[...] += p

            @pl.when(kk == nk - 1)
            def _():
                finish(acc_ref[...])

    if split == 1:
        a_spec = pl.BlockSpec((tm, tk), lambda i, kk: (i, kk))
    else:
        a_spec = pl.BlockSpec((None, tm, tk), lambda i, kk: (kk // per_slab, i, kk % per_slab))
    row = pl.BlockSpec((tm, n), lambda i, kk: (i, 0))
    vec = pl.BlockSpec((1, n), lambda i, kk: (0, 0))
    in_specs, args = [a_spec, pl.BlockSpec((tk, n), lambda i, kk: (kk, 0))], [a, b]
    if has_add:
        in_specs.append(row)
        args.append(add)
    if has_norm:
        in_specs.append(vec)
        args.append(norm_gain.reshape(1, n))
    if has_bwd:
        in_specs += [row, vec, row]
        args += [rms_bwd[0], rms_bwd[1].reshape(1, n), rms_bwd[2]]
    out_specs, out_shape = [row], [jax.ShapeDtypeStruct((m, n), F32)]
    if has_norm:
        out_specs.append(row)
        out_shape.append(jax.ShapeDtypeStruct((m, n), BF16))
    if has_bwd:
        out_specs.append(pl.BlockSpec((8, n), lambda i, kk: (0, 0)))
        out_shape.append(jax.ShapeDtypeStruct((8, n), F32))
    outs = pl.pallas_call(
        body, name=name, grid=(m // tm, nk), in_specs=in_specs, out_specs=out_specs, out_shape=out_shape,
        scratch_shapes=[pltpu.VMEM((tm, n), F32)] if nk > 1 else [],
        compiler_params=_params("arbitrary" if has_bwd else "parallel", "arbitrary"),
    )(*args)
    if has_bwd:
        return outs[0], outs[-1][0]
    return tuple(outs) if has_norm else outs[0]


def _matmul_tn(a, b, name, tm=1024, tn=512, ts=2048):
    s, m = a.shape
    split = b.shape[0] if b.ndim == 3 else 1
    nh = b.shape[-1]
    n = split * nh
    tm, tn, ts = _tile(m, tm), _tile(nh, tn), _tile(s, ts, 8)
    per_slab = nh // tn
    if split == 1:
        b_spec = pl.BlockSpec((ts, tn), lambda i, j, kk: (kk, j))
    else:
        b_spec = pl.BlockSpec((None, ts, tn), lambda i, j, kk: (j // per_slab, kk, j % per_slab))

    def body(a_ref, b_ref, o_ref):
        kk = pl.program_id(2)
        p = _dot_tn(a_ref[...].astype(BF16), b_ref[...].astype(BF16))

        @pl.when(kk == 0)
        def _():
            o_ref[...] = p

        @pl.when(kk > 0)
        def _():
            o_ref[...] += p

    return pl.pallas_call(
        body, name=name, grid=(m // tm, n // tn, s // ts),
        in_specs=[pl.BlockSpec((ts, tm), lambda i, j, kk: (kk, i)), b_spec],
        out_specs=pl.BlockSpec((tm, tn), lambda i, j, kk: (i, j)),
        out_shape=jax.ShapeDtypeStruct((m, n), F32),
        compiler_params=_params("parallel", "parallel", "arbitrary"),
    )(a, b)


def _rms_fwd(h, g, name, ts=512):
    s, d = h.shape
    ts = _tile(s, ts, 8)

    def body(h_ref, g_ref, o_ref):
        x = h_ref[...]
        r = lax.rsqrt(jnp.mean(x * x, axis=-1, keepdims=True) + EPS)
        o_ref[...] = (x * r * g_ref[...]).astype(BF16)

    return pl.pallas_call(
        body, name=name, grid=(s // ts,),
        in_specs=[pl.BlockSpec((ts, d), lambda i: (i, 0)), pl.BlockSpec((1, d), lambda i: (0, 0))],
        out_specs=pl.BlockSpec((ts, d), lambda i: (i, 0)),
        out_shape=jax.ShapeDtypeStruct((s, d), BF16),
        compiler_params=_params("parallel"),
    )(h, g.reshape(1, d))


def _shift_down(x, prev):
    rows = lax.broadcasted_iota(jnp.int32, (8, x.shape[1]), 0)
    p1, p2 = prev[7:8, :], prev[6:7, :]
    r1, r2 = pltpu.roll(x, 1, 0), pltpu.roll(x, 2, 0)
    top1 = jnp.where(rows == 0, p1, r1[0:8, :])
    top2 = jnp.where(rows == 0, p2, jnp.where(rows == 1, p1, r2[0:8, :]))
    if x.shape[0] == 8:
        return top1, top2
    return jnp.concatenate([top1, r1[8:, :]], axis=0), jnp.concatenate([top2, r2[8:, :]], axis=0)


def _shift_up(x, nxt):
    n = x.shape[0]
    rows = lax.broadcasted_iota(jnp.int32, (8, x.shape[1]), 0)
    n0, n1 = nxt[0:1, :], nxt[1:2, :]
    r1, r2 = pltpu.roll(x, n - 1, 0), pltpu.roll(x, n - 2, 0)
    end1 = jnp.where(rows == 7, n0, r1[n - 8:, :])
    end2 = jnp.where(rows == 7, n1, jnp.where(rows == 6, n0, r2[n - 8:, :]))
    return jnp.concatenate([r1[:n - 8, :], end1], axis=0), jnp.concatenate([r2[:n - 8, :], end2], axis=0)


def _conv(x, x1, x2, w):
    return w[2:3, :] * x + w[1:2, :] * x1 + w[0:1, :] * x2


def _halo_specs(ts, tc, col, n_time_blocks):
    r8 = ts // 8
    main = pl.BlockSpec((ts, tc), lambda j, i: (i, j + col))
    prev = pl.BlockSpec((8, tc), lambda j, i: (jnp.maximum(i * r8 - 1, 0), j + col))
    nxt = pl.BlockSpec((8, tc), lambda j, i: (jnp.minimum((i + 1) * r8, n_time_blocks * r8 - 1), j + col))
    return main, prev, nxt


def _silu_parts(g):
    sig = 1.0 / (1.0 + jnp.exp(-g))
    return sig, g * sig


def _ffn_act_fwd(up, cw, name, ts=256, tc=1408):
    s = up.shape[0]
    ts, tc = _tile(s, ts, 8), _tile(D_FF, tc)
    nc, nt = D_FF // tc, s // ts

    def body(g_ref, gp_ref, v_ref, vp_ref, wg_ref, wv_ref, o_ref):
        first = pl.program_id(1) == 0

        def conv(x_ref, p_ref, w_ref):
            x = x_ref[...]
            prev = jnp.where(first, 0.0, p_ref[...])
            x1, x2 = _shift_down(x, prev)
            return _conv(x, x1, x2, w_ref[...])

        ug = conv(g_ref, gp_ref, wg_ref)
        uv = conv(v_ref, vp_ref, wv_ref)
        _, silu = _silu_parts(ug)
        o_ref[...] = (silu * uv).astype(BF16)

    g_main, g_prev, _ = _halo_specs(ts, tc, 0, nt)
    v_main, v_prev, _ = _halo_specs(ts, tc, nc, nt)
    return pl.pallas_call(
        body, name=name, grid=(nc, nt),
        in_specs=[g_main, g_prev, v_main, v_prev,
                  pl.BlockSpec((3, tc), lambda j, i: (0, j)), pl.BlockSpec((3, tc), lambda j, i: (0, j + nc))],
        out_specs=pl.BlockSpec((ts, tc), lambda j, i: (i, j)),
        out_shape=jax.ShapeDtypeStruct((s, D_FF), BF16),
        compiler_params=_params("parallel", "parallel"),
    )(up, up, up, up, cw, cw)


def _ffn_act_bwd(up, cw, da, name, ts=256, tc=1408):
    s = up.shape[0]
    ts, tc = _tile(s, ts, 8), _tile(D_FF, tc)
    nc, nt = D_FF // tc, s // ts

    def body(g_ref, gp_ref, gn_ref, v_ref, vp_ref, vn_ref, da_ref, dan_ref, wg_ref, wv_ref,
             d_ref, dwg_ref, dwv_ref):
        i = pl.program_id(1)
        first, last = i == 0, i == nt - 1
        wg, wv = wg_ref[...], wv_ref[...]
        g, v = g_ref[...], v_ref[...]
        g1, g2 = _shift_down(g, jnp.where(first, 0.0, gp_ref[...]))
        v1, v2 = _shift_down(v, jnp.where(first, 0.0, vp_ref[...]))

        def d_u(ug, uv, da_v):
            sig, silu = _silu_parts(ug)
            return da_v * uv * (sig * (1.0 + ug * (1.0 - sig))), da_v * silu

        dug, duv = d_u(_conv(g, g1, g2, wg), _conv(v, v1, v2, wv), da_ref[...])
        gn, vn = gn_ref[...], vn_ref[...]
        gn1, gn2 = _shift_down(gn, g[ts - 8:, :])
        vn1, vn2 = _shift_down(vn, v[ts - 8:, :])
        dugn, duvn = d_u(_conv(gn, gn1, gn2, wg), _conv(vn, vn1, vn2, wv), dan_ref[...])
        dugn = jnp.where(last, 0.0, dugn)
        duvn = jnp.where(last, 0.0, duvn)

        def finish(du, dun, x, x1, x2, w, dx_ref, dw_ref):
            d1, d2 = _shift_up(du, dun)
            dx_ref[...] = (w[2:3, :] * du + w[1:2, :] * d1 + w[0:1, :] * d2).astype(BF16)

            @pl.when(first)
            def _():
                dw_ref[...] = jnp.zeros_like(dw_ref)

            dw_ref[0:1, :] += jnp.sum(du * x2, axis=0, keepdims=True)
            dw_ref[1:2, :] += jnp.sum(du * x1, axis=0, keepdims=True)
            dw_ref[2:3, :] += jnp.sum(du * x, axis=0, keepdims=True)

        finish(dug, dugn, g, g1, g2, wg, d_ref.at[0], dwg_ref)
        finish(duv, duvn, v, v1, v2, wv, d_ref.at[1], dwv_ref)

    g_specs = _halo_specs(ts, tc, 0, nt)
    v_specs = _halo_specs(ts, tc, nc, nt)
    da_main, _, da_next = _halo_specs(ts, tc, 0, nt)
    taps = pl.BlockSpec((8, tc), lambda j, i: (0, j))
    halves = pl.BlockSpec((2, ts, tc), lambda j, i: (0, i, j))
    d, dwg, dwv = pl.pallas_call(
        body, name=name, grid=(nc, nt),
        in_specs=[*g_specs, *v_specs, da_main, da_next,
                  pl.BlockSpec((3, tc), lambda j, i: (0, j)), pl.BlockSpec((3, tc), lambda j, i: (0, j + nc))],
        out_specs=[halves, taps, taps],
        out_shape=[jax.ShapeDtypeStruct((2, s, D_FF), BF16),
                   jax.ShapeDtypeStruct((8, D_FF), F32), jax.ShapeDtypeStruct((8, D_FF), F32)],
        compiler_params=_params("parallel", "arbitrary"),
    )(up, up, up, up, up, up, da, da, cw, cw)
    return d, dwg[:3], dwv[:3]


def _sconv_fwd(proj, ck, name, ts=256, tc=512):
    s = proj.shape[0]
    w = D_MODEL
    ts, tc = _tile(s, ts, 8), _tile(w, tc)
    nc, nt = w // tc, s // ts

    def body(b_ref, c_ref, cp_ref, u_ref, up_ref, w_ref, o_ref):
        first = pl.program_id(1) == 0
        cu = c_ref[...] * u_ref[...]
        cup = jnp.where(first, 0.0, cp_ref[...] * up_ref[...])
        x1, x2 = _shift_down(cu, cup)
        o_ref[...] = (b_ref[...] * _conv(cu, x1, x2, w_ref[...])).astype(BF16)

    b_main, _, _ = _halo_specs(ts, tc, 0, nt)
    c_main, c_prev, _ = _halo_specs(ts, tc, nc, nt)
    u_main, u_prev, _ = _halo_specs(ts, tc, 2 * nc, nt)
    return pl.pallas_call(
        body, name=name, grid=(nc, nt),
        in_specs=[b_main, c_main, c_prev, u_main, u_prev, pl.BlockSpec((3, tc), lambda j, i: (0, j))],
        out_specs=pl.BlockSpec((ts, tc), lambda j, i: (i, j)),
        out_shape=jax.ShapeDtypeStruct((s, w), BF16),
        compiler_params=_params("parallel", "parallel"),
    )(proj, proj, proj, proj, proj, ck)


def _sconv_bwd(proj, ck, dy, name, ts=256, tc=512):
    s = proj.shape[0]
    w = D_MODEL
    ts, tc = _tile(s, ts, 8), _tile(w, tc)
    nc, nt = w // tc, s // ts

    def body(b_ref, bn_ref, c_ref, cp_ref, u_ref, up_ref, dy_ref, dyn_ref, w_ref,
             d_ref, dw_ref):
        i = pl.program_id(1)
        first, last = i == 0, i == nt - 1
        wv = w_ref[...]
        b, c, u, dy_v = b_ref[...], c_ref[...], u_ref[...], dy_ref[...]
        cu = c * u
        cup = jnp.where(first, 0.0, cp_ref[...] * up_ref[...])
        x1, x2 = _shift_down(cu, cup)
        d_ref[0] = (dy_v * _conv(cu, x1, x2, wv)).astype(BF16)
        dcv = dy_v * b
        dcvn = jnp.where(last, 0.0, dyn_ref[...] * bn_ref[...])
        d1, d2 = _shift_up(dcv, dcvn)
        dcu = wv[2:3, :] * dcv + wv[1:2, :] * d1 + wv[0:1, :] * d2
        d_ref[1] = (dcu * u).astype(BF16)
        d_ref[2] = (dcu * c).astype(BF16)

        @pl.when(first)
        def _():
            dw_ref[...] = jnp.zeros_like(dw_ref)

        dw_ref[0:1, :] += jnp.sum(dcv * x2, axis=0, keepdims=True)
        dw_ref[1:2, :] += jnp.sum(dcv * x1, axis=0, keepdims=True)
        dw_ref[2:3, :] += jnp.sum(dcv * cu, axis=0, keepdims=True)

    b_main, _, b_next = _halo_specs(ts, tc, 0, nt)
    c_main, c_prev, _ = _halo_specs(ts, tc, nc, nt)
    u_main, u_prev, _ = _halo_specs(ts, tc, 2 * nc, nt)
    dy_main, _, dy_next = _halo_specs(ts, tc, 0, nt)
    d, dw = pl.pallas_call(
        body, name=name, grid=(nc, nt),
        in_specs=[b_main, b_next, c_main, c_prev, u_main, u_prev, dy_main, dy_next,
                  pl.BlockSpec((3, tc), lambda j, i: (0, j))],
        out_specs=[pl.BlockSpec((3, ts, tc), lambda j, i: (0, i, j)), pl.BlockSpec((8, tc), lambda j, i: (0, j))],
        out_shape=[jax.ShapeDtypeStruct((3, s, w), BF16), jax.ShapeDtypeStruct((8, w), F32)],
        compiler_params=_params("parallel", "arbitrary"),
    )(proj, proj, proj, proj, proj, proj, dy, dy, ck)
    return d, dw[:3]


def _low_lanes(shape):
    return lax.broadcasted_iota(jnp.int32, shape, 1) < HEAD_DIM


def _top_rows(shape):
    return lax.broadcasted_iota(jnp.int32, shape, 0) < HEAD_DIM


def _norm_pair(x):
    r = lax.rsqrt(_mean_pair(x * x) + EPS)
    return x * r, r


def _mean_pair(x):
    same_head = _tri(LANES, lambda a, b: a // HEAD_DIM == b // HEAD_DIM)
    return _split_dot(x, same_head, 3) * (1.0 / HEAD_DIM)


def _qkv_prep(proj, gq, gk, name, ts=512):
    s = proj.shape[0]
    ts = _tile(s, ts)
    npair = MIX // LANES
    scale = HEAD_DIM ** -0.5

    def body(q_ref, k_ref, v_ref, gq_ref, gk_ref, qo_ref, ko_ref, vo_ref, qt_ref, kt_ref, vt_ref):
        qn, _ = _norm_pair(q_ref[...])
        kn, _ = _norm_pair(k_ref[...])
        q = qn * gq_ref[0] * scale
        k = kn * gk_ref[0]
        v = v_ref[...]
        qo_ref[...] = q.astype(BF16)
        ko_ref[...] = k.astype(BF16)
        vo_ref[...] = v.astype(BF16)
        qt_ref[...] = q.T.astype(BF16)
        kt_ref[...] = k.T.astype(BF16)
        vt_ref[...] = v.T.astype(BF16)

    gain = pl.BlockSpec((1, 1, LANES), lambda i, p: (p, 0, 0))
    tile = pl.BlockSpec((ts, LANES), lambda i, p: (i, p))
    tile_t = pl.BlockSpec((LANES, ts), lambda i, p: (p, i))
    out = jax.ShapeDtypeStruct((s, MIX), BF16)
    out_t = jax.ShapeDtypeStruct((MIX, s), BF16)
    return pl.pallas_call(
        body, name=name, grid=(s // ts, npair),
        in_specs=[tile, pl.BlockSpec((ts, LANES), lambda i, p: (i, p + npair)),
                  pl.BlockSpec((ts, LANES), lambda i, p: (i, p + 2 * npair)), gain, gain],
        out_specs=[tile, tile, tile, tile_t, tile_t, tile_t], out_shape=[out, out, out, out_t, out_t, out_t],
        compiler_params=_params("parallel", "parallel"),
    )(proj, proj, proj, gq, gk)


def _qkv_prep_bwd(proj, gq, gk, dqs, dks, dvs, name, ts=512):
    s = proj.shape[0]
    ts = _tile(s, ts, 8)
    npair = MIX // LANES
    half = npair // 2
    scale = HEAD_DIM ** -0.5

    def body(q_ref, k_ref, gq_ref, gk_ref, dqf_ref, dqs_ref, dkf_ref, dks_ref, dvf_ref, dvs_ref,
             dq_ref, dk_ref, dv_ref, dgq_ref, dgk_ref):
        p, i = pl.program_id(0), pl.program_id(1)
        fox = p < half

        def one(x_ref, g_ref, df_ref, ds_ref, dx_ref, dg_ref, mult):
            dn = jnp.where(fox, df_ref[...], ds_ref[...]) * mult
            xh, r = _norm_pair(x_ref[...])

            @pl.when(i == 0)
            def _():
                dg_ref[...] = jnp.zeros_like(dg_ref)

            dg_ref[0, 0:1, :] += jnp.sum(dn * xh, axis=0, keepdims=True)
            dxh = dn * g_ref[0]
            dx_ref[...] = (r * (dxh - xh * _mean_pair(dxh * xh))).astype(BF16)

        one(q_ref, gq_ref, dqf_ref, dqs_ref, dq_ref, dgq_ref, scale)
        one(k_ref, gk_ref, dkf_ref, dks_ref, dk_ref, dgk_ref, 1.0)
        dv_ref[...] = jnp.where(fox, dvf_ref[...], dvs_ref[...]).astype(BF16)

    gain = pl.BlockSpec((1, 1, LANES), lambda p, i: (p, 0, 0))
    tile = pl.BlockSpec((ts, LANES), lambda p, i: (i, p))
    fpart = pl.BlockSpec((ts, LANES), lambda p, i: (i, jnp.minimum(p, half - 1)))
    spart = pl.BlockSpec((ts, LANES), lambda p, i: (i, jnp.maximum(p - half, 0)))
    dgain = pl.BlockSpec((1, 8, LANES), lambda p, i: (p, 0, 0))
    out = jax.ShapeDtypeStruct((s, MIX), BF16)
    gshape = jax.ShapeDtypeStruct((npair, 8, LANES), F32)
    dq, dk, dv, dgq, dgk = pl.pallas_call(
        body, name=name, grid=(npair, s // ts),
        in_specs=[tile, pl.BlockSpec((ts, LANES), lambda p, i: (i, p + npair)), gain, gain,
                  fpart, spart, fpart, spart, fpart, spart],
        out_specs=[tile, tile, tile, dgain, dgain], out_shape=[out, out, out, gshape, gshape],
        compiler_params=_params("parallel", "arbitrary"),
    )(proj, proj, gq, gk, dqs[0], dqs[1], dks[0], dks[1], dvs[0], dvs[1])
    return dq, dk, dv, dgq[:, 0, :], dgk[:, 0, :]


def _tri(n, rel):
    a = lax.broadcasted_iota(jnp.int32, (n, n), 0)
    b = lax.broadcasted_iota(jnp.int32, (n, n), 1)
    return rel(a, b).astype(BF16)


def _fgate_fwd(logit, bias, name):
    nh, r, _ = logit.shape

    def body(x_ref, b_ref, o_ref):
        within = _tri(LANES, lambda a, b: a <= b)
        before = _tri(r, lambda a, b: b < a)
        for hh in range(nh):
            x = x_ref[hh] + b_ref[hh]
            lf = jnp.minimum(x, 0.0) - jnp.log1p(jnp.exp(-jnp.abs(x)))
            c = _split_dot(lf, within, 3)
            tot = jnp.broadcast_to(c[:, LANES - 1:LANES], (r, LANES))
            o_ref[hh] = c + _split_dot_left(before, tot, 3)

    return pl.pallas_call(
        body, name=name, out_shape=jax.ShapeDtypeStruct((nh, r, LANES), F32),
        in_specs=[pl.BlockSpec(memory_space=pltpu.VMEM), pl.BlockSpec(memory_space=pltpu.SMEM)],
        out_specs=pl.BlockSpec(memory_space=pltpu.VMEM),
    )(logit, bias)


def _fgate_bwd(logit, bias, dcum, name):
    nh, r, _ = logit.shape

    def body(x_ref, b_ref, d_ref, dx_ref, db_ref):
        within = _tri(LANES, lambda a, b: a >= b)
        after = _tri(r, lambda a, b: b > a)
        for hh in range(nh):
            x = x_ref[hh] + b_ref[hh]
            d = d_ref[hh]
            c = _split_dot(d, within, 3)
            tot = jnp.broadcast_to(c[:, 0:1], (r, LANES))
            dlf = c + _split_dot_left(after, tot, 3)
            dx = dlf * (1.0 / (1.0 + jnp.exp(x)))
            dx_ref[hh] = dx
            db_ref[hh:hh + 1, :] = jnp.broadcast_to(jnp.sum(dx, keepdims=True).reshape(1, 1), (1, LANES))

    return pl.pallas_call(
        body, name=name,
        out_shape=[jax.ShapeDtypeStruct((nh, r, LANES), F32), jax.ShapeDtypeStruct((nh, LANES), F32)],
        in_specs=[pl.BlockSpec(memory_space=pltpu.VMEM), pl.BlockSpec(memory_space=pltpu.SMEM),
                  pl.BlockSpec(memory_space=pltpu.VMEM)],
        out_specs=[pl.BlockSpec(memory_space=pltpu.VMEM), pl.BlockSpec(memory_space=pltpu.VMEM)],
    )(logit, bias, dcum)


def _pair_masks(x):
    lo = _low_lanes(x.shape)
    zero = jnp.zeros_like(x)
    return jnp.where(lo, x, zero), jnp.where(lo, zero, x)


def _pair_masks_t(x):
    top = _top_rows(x.shape)
    zero = jnp.zeros_like(x)
    return jnp.where(top, x, zero), jnp.where(top, zero, x)


def _stack_heads(x):
    return jnp.concatenate(_pair_masks(x), axis=0)


def _stack_heads_t(x):
    return jnp.concatenate(_pair_masks_t(x), axis=1)


def _pair_colsum_t(x):
    top = _top_rows(x.shape)
    return (jnp.sum(jnp.where(top, x, 0.0), axis=0, keepdims=True),
            jnp.sum(jnp.where(top, 0.0, x), axis=0, keepdims=True))


def _key_query_iotas(t):
    return lax.broadcasted_iota(jnp.int32, (t, t), 0), lax.broadcasted_iota(jnp.int32, (t, t), 1)


def _walk_blocks(i, step, descending, group=2):
    full = i // group
    left = i - full * group

    def run(first, count):
        sign = -1 if descending else 1
        step([(first + sign * n, False) for n in range(count)])

    def leftovers():
        start = (left - 1) if descending else full * group
        sign = -1 if descending else 1
        if group == 4:
            @pl.when(left >= 2)
            def _():
                run(start, 2)

            @pl.when(left % 2 == 1)
            def _():
                run(0 if descending else i - 1, 1)
        else:
            @pl.when(left == 1)
            def _():
                run(start, 1)

    def loop(g, carry):
        run((i - 1 - group * g) if descending else group * g, group)
        return carry

    if descending:
        step([(i, True)])
        lax.fori_loop(0, full, loop, 0)
        leftovers()
        return
    total = i + 1
    groups = total // group
    rest = total - groups * group
    lax.fori_loop(0, jnp.where(rest == 0, groups - 1, groups), loop, 0)

    def closing(count):
        step([(i - count + 1 + n, n == count - 1) for n in range(count)])

    pl.when(rest == 0)(lambda: closing(group))
    if group == 4:
        pl.when(rest == 3)(lambda: run(i - 2, 2))
        pl.when(rest == 2)(lambda: closing(2))
        pl.when((rest == 1) | (rest == 3))(lambda: closing(1))
    else:
        pl.when(rest == 1)(lambda: closing(1))


def _attn_specs(s, tq, pair0):
    q_nat = pl.BlockSpec((tq, LANES), lambda p, i: (i, p + pair0))
    q_t = pl.BlockSpec((LANES, tq), lambda p, i: (p + pair0, i))
    k_nat = pl.BlockSpec((s, LANES), lambda p, i: (0, p + pair0))
    k_t = pl.BlockSpec((LANES, s), lambda p, i: (p + pair0, 0))
    return q_nat, q_t, k_nat, k_t


def _fox_fwd(qt, kh, vt, frow, fcol, name, tq=256):
    s = kh.shape[0]
    tq = _tile(s, tq)
    nq, half = s // tq, MIX // LANES // 2

    def body(qt_ref, k_ref, vt_ref, fr_ref, fc_ref, o_ref, lse_ref, m_s, l_s, acc_s):
        i = pl.program_id(1)
        qt_v = qt_ref[...]
        ft = fr_ref[0]
        key, qry = _key_query_iotas(tq)
        causal = key <= qry
        m_s[...] = jnp.full(m_s.shape, NEG, F32)
        l_s[...] = jnp.zeros_like(l_s)
        acc_s[...] = jnp.zeros_like(acc_s)

        top = _top_rows((LANES, tq))

        def step(blocks):
            rows = [pl.ds(pl.multiple_of(j * tq, tq), tq) for j, _ in blocks]
            zs = [_dot(_stack_heads(k_ref[r, :]), qt_v) for r in rows]
            m_cur, l_cur = [m_s[0], m_s[1]], [l_s[0], l_s[1]]
            acc = acc_s[...]
            for b, (_, masked) in enumerate(blocks):
                fk = fc_ref[0, rows[b], :]
                prs, alphas = [], []
                for hh in range(2):
                    sc = zs[b][hh * tq:(hh + 1) * tq] + (ft[hh:hh + 1, :] - fk[:, hh:hh + 1])
                    if masked:
                        sc = jnp.where(causal, sc, NEG)
                    m_new = jnp.maximum(m_cur[hh], jnp.max(sc, axis=0, keepdims=True))
                    alpha = jnp.exp(m_cur[hh] - m_new)
                    pr = jnp.exp(sc - m_new)
                    l_cur[hh] = alpha * l_cur[hh] + jnp.sum(pr, axis=0, keepdims=True)
                    m_cur[hh] = m_new
                    prs.append(pr.astype(BF16))
                    alphas.append(alpha)
                pv = _dot(_stack_heads_t(vt_ref[:, rows[b]]), jnp.concatenate(prs, axis=0))
                acc = jnp.where(top, alphas[0], alphas[1]) * acc + pv
            acc_s[...] = acc
            for hh in range(2):
                m_s[hh] = m_cur[hh]
                l_s[hh] = l_cur[hh]

        _walk_blocks(i, step, descending=False, group=4)
        o_ref[...] = (acc_s[...] / jnp.where(top, l_s[0], l_s[1])).T
        lse_ref[0, 0:1, :] = m_s[0] + jnp.log(l_s[0])
        lse_ref[0, 1:2, :] = m_s[1] + jnp.log(l_s[1])

    _, q_t, k_nat, k_t = _attn_specs(s, tq, 0)
    qstat = pl.BlockSpec((1, 2, tq), lambda p, i: (p, 0, i))
    return pl.pallas_call(
        body, name=name, grid=(half, nq),
        in_specs=[q_t, k_nat, k_t, qstat, pl.BlockSpec((1, s, 2), lambda p, i: (p, 0, 0))],
        out_specs=[pl.BlockSpec((tq, LANES), lambda p, i: (i, p)), qstat],
        out_shape=[jax.ShapeDtypeStruct((s, MIX // 2), F32), jax.ShapeDtypeStruct((half, 2, s), F32)],
        scratch_shapes=[pltpu.VMEM((2, 1, tq), F32), pltpu.VMEM((2, 1, tq), F32), pltpu.VMEM((LANES, tq), F32)],
        compiler_params=_params("parallel", "arbitrary"),
    )(qt, kh, vt, frow, fcol)


def _fox_bwd(qh, qt, kh, kt, vb, frow, fcol, lse, o, do, name, tq=256):
    s = kh.shape[0]
    tq = _tile(s, tq)
    nq, half = s // tq, MIX // LANES // 2

    def body(q_ref, qt_ref, k_ref, kt_ref, v_ref, fr_ref, fc_ref, lse_ref, o_ref, do_ref,
             dq_ref, dk_ref, dv_ref, dfk_ref, dfq_ref, dq_s, rs_s):
        i = pl.program_id(1)

        @pl.when(i == 0)
        def _():
            dk_ref[...] = jnp.zeros_like(dk_ref)
            dv_ref[...] = jnp.zeros_like(dv_ref)
            dfk_ref[...] = jnp.zeros_like(dfk_ref)

        q2 = _stack_heads(q_ref[...])
        qt_v = qt_ref[...]
        do_v = do_ref[...]
        do2 = _stack_heads(do_v.astype(BF16))
        dot_v = do_v.T.astype(BF16)
        dsum = _pair_colsum_t((do_v * o_ref[...]).T)
        ft, ls = fr_ref[0], lse_ref[0]
        key, qry = _key_query_iotas(tq)
        causal = key <= qry
        lane = lax.broadcasted_iota(jnp.int32, (2 * tq, LANES), 0) // tq
        pick2 = (lax.broadcasted_iota(jnp.int32, (2 * tq, LANES), 1) == lane).astype(BF16)
        q2_pick = jnp.concatenate([q2, pick2], axis=1)
        dq_s[...] = jnp.zeros_like(dq_s)
        rs_s[...] = jnp.zeros_like(rs_s)

        def step(blocks):
            rows = [pl.ds(pl.multiple_of(j * tq, tq), tq) for j, _ in blocks]
            zs = [_dot(_stack_heads(k_ref[r, :]), qt_v) for r in rows]
            dps = [_dot(_stack_heads(v_ref[r, :]), dot_v) for r in rows]
            rs = [rs_s[0], rs_s[1]]
            dq = None
            for b, (_, masked) in enumerate(blocks):
                fk = fc_ref[0, rows[b], :]
                prs, dss = [], []
                for hh in range(2):
                    blk = slice(hh * tq, (hh + 1) * tq)
                    sc = zs[b][blk] + (ft[hh:hh + 1, :] - fk[:, hh:hh + 1])
                    pr = jnp.exp(sc - ls[hh:hh + 1, :])
                    if masked:
                        pr = jnp.where(causal, pr, 0.0)
                    dsb = (pr * (dps[b][blk] - dsum[hh])).astype(BF16)
                    rs[hh] = rs[hh] + jnp.sum(dsb.astype(F32), axis=0, keepdims=True)
                    prs.append(pr.astype(BF16))
                    dss.append(dsb)
                dv_ref[rows[b], :] += _dot(jnp.concatenate(prs, axis=1), do2)
                both = _dot(jnp.concatenate(dss, axis=1), q2_pick)
                dk_ref[rows[b], :] += both[:, :LANES]
                dfk_ref[0, rows[b], :] -= both[:, LANES:]
                term = _dot(_stack_heads_t(kt_ref[:, rows[b]]), jnp.concatenate(dss, axis=0))
                dq = term if dq is None else dq + term
            rs_s[0], rs_s[1] = rs
            dq_s[...] += dq

        _walk_blocks(i, step, descending=False, group=4)
        dq_ref[...] = dq_s[...].T
        dfq_ref[0, 0:1, :] = rs_s[0]
        dfq_ref[0, 1:2, :] = rs_s[1]

    q_nat, q_t, k_nat, k_t = _attn_specs(s, tq, 0)
    qstat = pl.BlockSpec((1, 2, tq), lambda p, i: (p, 0, i))
    otile = pl.BlockSpec((tq, LANES), lambda p, i: (i, p))
    oresident = pl.BlockSpec((s, LANES), lambda p, i: (0, p))
    out = jax.ShapeDtypeStruct((s, MIX // 2), F32)
    return pl.pallas_call(
        body, name=name, grid=(half, nq),
        in_specs=[q_nat, q_t, k_nat, k_t, k_nat, qstat, pl.BlockSpec((1, s, 2), lambda p, i: (p, 0, 0)), qstat,
                  otile, q_nat],
        out_specs=[otile, oresident, oresident, pl.BlockSpec((1, s, LANES), lambda p, i: (p, 0, 0)), qstat],
        out_shape=[out, out, out, jax.ShapeDtypeStruct((half, s, LANES), F32),
                   jax.ShapeDtypeStruct((half, 2, s), F32)],
        scratch_shapes=[pltpu.VMEM((LANES, tq), F32), pltpu.VMEM((2, 1, tq), F32)],
        compiler_params=_params("parallel", "arbitrary"),
    )(qh, qt, kh, kt, vb, frow, fcol, lse, o, do)


def _log_sig_pair(z):
    zc = jnp.maximum(z, -80.0)
    lb = -jnp.log(1.0 + jnp.exp(-zc))
    return lb, lb - zc


def _sb_fwd(qt, kh, vt, name, tq=256, gather=()):
    s = kh.shape[0]
    tq = _tile(s, tq)
    nq, half = s // tq, MIX // LANES // 2
    ng = len(gather)

    def body(*refs):
        qt_ref, k_ref, vt_ref = refs[:3]
        o_ref, tot_ref = refs[3 + ng:5 + ng]
        c_s, acc_s = refs[5 + 2 * ng:7 + 2 * ng]
        p, i = pl.program_id(0), pl.program_id(1)
        if ng:
            start, relay, finish = _gather_stages(refs[3:3 + ng], refs[5 + ng:5 + 2 * ng], *refs[7 + 2 * ng:])
            pl.when((p == 0) & (i == 0))(start)
            pl.when((p == half - 1) & (i == 0))(relay)
        qt_v = qt_ref[...]
        key, qry = _key_query_iotas(tq)
        strict = key < qry
        later = _tri(tq, lambda a, b: b > a)
        c_s[...] = jnp.zeros_like(c_s)
        acc_s[...] = jnp.zeros_like(acc_s)

        def step(blocks):
            rows = [pl.ds(pl.multiple_of(j * tq, tq), tq) for j, _ in blocks]
            zs = [_dot(_stack_heads(k_ref[r, :]), qt_v) for r in rows]
            lbs, loms, afters = [], [], []
            for b, (_, masked) in enumerate(blocks):
                for hh in range(2):
                    lb, lom = _log_sig_pair(zs[b][hh * tq:(hh + 1) * tq])
                    if masked:
                        lom = jnp.where(strict, lom, 0.0)
                    lbs.append(lb)
                    loms.append(lom)
                afters.append(_split_dot_left(later, jnp.concatenate(loms[2 * b:2 * b + 2], axis=1), 2))
            carry = [c_s[0], c_s[1]]
            pv = None
            for b, (_, masked) in enumerate(blocks):
                ws = []
                for hh in range(2):
                    n = 2 * b + hh
                    w = jnp.exp(lbs[n] + afters[b][:, hh * tq:(hh + 1) * tq] + carry[hh])
                    if masked:
                        w = jnp.where(strict, w, 0.0)
                    ws.append(w.astype(BF16))
                    carry[hh] = carry[hh] + jnp.sum(loms[n], axis=0, keepdims=True)
                term = _dot(_stack_heads_t(vt_ref[:, rows[b]]), jnp.concatenate(ws, axis=0))
                pv = term if pv is None else pv + term
            c_s[0], c_s[1] = carry
            acc_s[...] += pv

        _walk_blocks(i, step, descending=True, group=4)
        o_ref[...] = acc_s[...].T
        tot_ref[0, 0:1, :] = c_s[0]
        tot_ref[0, 1:2, :] = c_s[1]
        if ng:
            pl.when((p == half - 1) & (i == nq - 1))(finish)

    _, q_t, k_nat, k_t = _attn_specs(s, tq, half)
    qstat = pl.BlockSpec((1, 2, tq), lambda p, i: (p, 0, i))
    return pl.pallas_call(
        body, name=name, grid=(half, nq),
        in_specs=[q_t, k_nat, k_t] + [ANY] * ng,
        out_specs=[pl.BlockSpec((tq, LANES), lambda p, i: (i, p)), qstat] + [ANY] * ng,
        out_shape=[jax.ShapeDtypeStruct((s, MIX // 2), F32), jax.ShapeDtypeStruct((half, 2, s), F32)]
        + [jax.ShapeDtypeStruct((N_DEV,) + x.shape, x.dtype) for x in gather],
        scratch_shapes=[pltpu.VMEM((2, 1, tq), F32), pltpu.VMEM((LANES, tq), F32)]
        + (_gather_scratch(ng) if ng else []),
        compiler_params=_params("arbitrary", "arbitrary") if ng else _params("parallel", "arbitrary"),
    )(qt, kh, vt, *gather)


def _sb_bwd(qh, qt, kh, kt, vb, tot, do, name, tq=256, exchange=()):
    s = kh.shape[0]
    tq = _tile(s, tq)
    nq, half = s // tq, MIX // LANES // 2
    nx = len(exchange)

    def body(*refs):
        q_ref, qt_ref, k_ref, kt_ref, v_ref, tot_ref, do_ref = refs[:7]
        dq_ref, dk_ref, dv_ref = refs[7 + nx:10 + nx]
        rem_s, pg_s, dq_s = refs[10 + 2 * nx:13 + 2 * nx]
        p, i = pl.program_id(0), pl.program_id(1)
        if nx:
            start, finish = _chip_exchange_stages(refs[7:7 + nx], refs[10 + nx:10 + 2 * nx], *refs[13 + 2 * nx:])
            pl.when((p == 0) & (i == 0))(start)

        @pl.when(i == 0)
        def _():
            dk_ref[...] = jnp.zeros_like(dk_ref)
            dv_ref[...] = jnp.zeros_like(dv_ref)

        q2 = _stack_heads(q_ref[...])
        qt_v = qt_ref[...]
        do_v = do_ref[...]
        do2 = _stack_heads(do_v.astype(BF16))
        dot_v = do_v.T.astype(BF16)
        key, qry = _key_query_iotas(tq)
        strict = key < qry
        upto = _tri(tq, lambda a, b: b <= a)
        before = _tri(tq, lambda a, b: b < a)
        tv = tot_ref[0]
        rem_s[0] = tv[0:1, :]
        rem_s[1] = tv[1:2, :]
        pg_s[...] = jnp.zeros_like(pg_s)
        dq_s[...] = jnp.zeros_like(dq_s)

        def step(blocks):
            nb = len(blocks)
            rows = [pl.ds(pl.multiple_of(j * tq, tq), tq) for j, _ in blocks]
            zs = [_dot(_stack_heads(k_ref[r, :]), qt_v) for r in rows]
            dws = [_dot(_stack_heads(v_ref[r, :]), dot_v) for r in rows]
            lbs, loms, prefixes = [], [], []
            for b, (_, masked) in enumerate(blocks):
                for hh in range(2):
                    lb, lom = _log_sig_pair(zs[b][hh * tq:(hh + 1) * tq])
                    if masked:
                        lom = jnp.where(strict, lom, 0.0)
                    lbs.append(lb)
                    loms.append(lom)
                prefixes.append(_split_dot_left(upto, jnp.concatenate(loms[2 * b:2 * b + 2], axis=1), 2))
            rem = [rem_s[0], rem_s[1]]
            ws, gs, gpres = [], [], []
            for b, (_, masked) in enumerate(blocks):
                for hh in range(2):
                    n = 2 * b + hh
                    blk = slice(hh * tq, (hh + 1) * tq)
                    w = jnp.exp(lbs[n] + (rem[hh] - prefixes[b][:, blk]))
                    if masked:
                        w = jnp.where(strict, w, 0.0)
                    gs.append(dws[b][blk] * w)
                    ws.append(w.astype(BF16))
                    rem[hh] = rem[hh] - jnp.sum(loms[n], axis=0, keepdims=True)
                gpres.append(_dot(before, jnp.concatenate(gs[2 * b:2 * b + 2], axis=1).astype(BF16)))
                dv_ref[rows[b], :] += _dot(jnp.concatenate(ws[2 * b:2 * b + 2], axis=1), do2)
            rem_s[0], rem_s[1] = rem
            pg = [pg_s[0], pg_s[1]]
            dq = None
            for b, (_, masked) in enumerate(blocks):
                dzs = []
                for hh in range(2):
                    n = 2 * b + hh
                    g = gs[n]
                    dz = g - jnp.exp(lbs[n]) * (g + (pg[hh] + gpres[b][:, hh * tq:(hh + 1) * tq]))
                    if masked:
                        dz = jnp.where(strict, dz, 0.0)
                    dzs.append(dz.astype(BF16))
                    pg[hh] = pg[hh] + jnp.sum(g, axis=0, keepdims=True)
                dk_ref[rows[b], :] += _dot(jnp.concatenate(dzs, axis=1), q2)
                term = _dot(_stack_heads_t(kt_ref[:, rows[b]]), jnp.concatenate(dzs, axis=0))
                dq = term if dq is None else dq + term
            pg_s[0], pg_s[1] = pg
            dq_s[...] += dq

        _walk_blocks(i, step, descending=False, group=2)
        dq_ref[...] = dq_s[...].T
        if nx:
            pl.when((p == half - 1) & (i == nq - 1))(finish)

    q_nat, q_t, k_nat, k_t = _attn_specs(s, tq, half)
    qstat = pl.BlockSpec((1, 2, tq), lambda p, i: (p, 0, i))
    otile = pl.BlockSpec((tq, LANES), lambda p, i: (i, p))
    oresident = pl.BlockSpec((s, LANES), lambda p, i: (0, p))
    out = jax.ShapeDtypeStruct((s, MIX // 2), F32)
    return pl.pallas_call(
        body, name=name, grid=(half, nq),
        in_specs=[q_nat, q_t, k_nat, k_t, k_nat, qstat, q_nat] + [ANY] * nx,
        out_specs=[otile, oresident, oresident] + [ANY] * nx,
        out_shape=[out, out, out] + [jax.ShapeDtypeStruct(x.shape, x.dtype) for x in exchange],
        scratch_shapes=[pltpu.VMEM((2, 1, tq), F32), pltpu.VMEM((2, 1, tq), F32), pltpu.VMEM((LANES, tq), F32)]
        + (_chip_exchange_scratch(nx) if nx else []),
        compiler_params=_params("arbitrary", "arbitrary") if nx else _params("parallel", "arbitrary"),
    )(qh, qt, kh, kt, vb, tot, do, *exchange)


def _loss_head(y, target, name, ts=512):
    s, d = y.shape
    ts = _tile(s, ts, 8)
    nt = s // ts

    def body(y_ref, t_ref, dy_ref, l_ref, acc):
        i = pl.program_id(0)
        err = y_ref[...] - t_ref[...]
        dy_ref[...] = err * (1.0 / d)

        @pl.when(i == 0)
        def _():
            acc[...] = jnp.zeros_like(acc)

        acc[...] += jnp.sum(err * err, axis=0, keepdims=True)

        @pl.when(i == nt - 1)
        def _():
            tot = jnp.sum(acc[...], keepdims=True).reshape(1, 1) * (0.5 / d)
            l_ref[...] = jnp.broadcast_to(tot, l_ref.shape)

    row = pl.BlockSpec((ts, d), lambda i: (i, 0))
    dy, l = pl.pallas_call(
        body, name=name, grid=(nt,), in_specs=[row, row],
        out_specs=[row, pl.BlockSpec((8, LANES), lambda i: (0, 0))],
        out_shape=[jax.ShapeDtypeStruct((s, d), F32), jax.ShapeDtypeStruct((8, LANES), F32)],
        scratch_shapes=[pltpu.VMEM((1, d), F32)],
        compiler_params=_params("arbitrary"),
    )(y, target)
    return l[0, 0], dy


def _coords():
    return lax.axis_index("x"), lax.axis_index("y"), lax.axis_index("c")


def _other_chips(xi, yi):
    return [(1 - xi, yi), (xi, 1 - yi), (1 - xi, 1 - yi)]


def _gather_stages(x_refs, out_refs, send_sems, recv_sems, local_sems):
    n = len(x_refs)
    xi, yi, ci = _coords()
    me, sibling = (xi, yi, ci), (xi, yi, 1 - ci)
    chips = _other_chips(xi, yi)

    def slot(a, px, py, pc):
        return out_refs[a].at[4 * px + 2 * py + pc]

    def copy(a, k, block, to, src=None):
        return pltpu.make_async_remote_copy(
            src_ref=slot(a, *block) if src is None else src, dst_ref=slot(a, *block),
            send_sem=send_sems.at[a, k], recv_sem=recv_sems.at[a, k], device_id=to, device_id_type=MESH)

    def own(a):
        return pltpu.make_async_copy(x_refs[a], slot(a, *me), local_sems.at[a])

    def first(a):
        return [copy(a, 0, me, sibling, src=x_refs[a])] + [
            copy(a, 1 + j, me, (*chip, ci), src=x_refs[a]) for j, chip in enumerate(chips)]

    def passed(a, j):
        return copy(a, 4 + j, (*chips[j], ci), sibling)

    def start():
        for a in range(n):
            own(a).start()
        for a in range(n):
            for cp in first(a):
                cp.start()

    def relay():
        for j, chip in enumerate(chips):
            for a in range(n):
                copy(a, 1 + j, (*chip, ci), me).wait_recv()
                passed(a, j).start()

    def finish():
        for a in range(n):
            copy(a, 0, sibling, me).wait_recv()
            for j, chip in enumerate(chips):
                copy(a, 4 + j, (*chip, 1 - ci), me).wait_recv()
        for a in range(n):
            for cp in first(a) + [passed(a, j) for j in range(len(chips))]:
                cp.wait_send()
            own(a).wait()

    return start, relay, finish


def _gather_scratch(n):
    return [pltpu.SemaphoreType.DMA((n, 7)), pltpu.SemaphoreType.DMA((n, 7)), pltpu.SemaphoreType.DMA((n,))]


def _all_gather(xs, name):
    n = len(xs)

    def body(*refs):
        start, relay, finish = _gather_stages(refs[:n], refs[n:2 * n], *refs[2 * n:])
        start()
        relay()
        finish()

    return pl.pallas_call(
        body, name=name, out_shape=[jax.ShapeDtypeStruct((N_DEV,) + x.shape, x.dtype) for x in xs],
        in_specs=[ANY] * n, out_specs=[ANY] * n, scratch_shapes=_gather_scratch(n),
    )(*xs)


def _sibling_exchange(gs, name):
    n = len(gs)

    def body(*refs):
        g_refs, recv_refs = refs[:n], refs[n:2 * n]
        send_sems, recv_sems = refs[2 * n:]
        xi, yi, ci = _coords()
        cps = [pltpu.make_async_remote_copy(
            src_ref=g_refs[a].at[2 * chip + (1 - ci)], dst_ref=recv_refs[a].at[chip],
            send_sem=send_sems.at[a, chip], recv_sem=recv_sems.at[a, chip],
            device_id=(xi, yi, 1 - ci), device_id_type=MESH) for a in range(n) for chip in range(N_CHIP)]
        for cp in cps:
            cp.start()
        for cp in cps:
            cp.wait()

    return pl.pallas_call(
        body, name=name, out_shape=[jax.ShapeDtypeStruct((N_CHIP,) + g.shape[1:], g.dtype) for g in gs],
        in_specs=[ANY] * n, out_specs=[ANY] * n,
        scratch_shapes=[pltpu.SemaphoreType.DMA((n, N_CHIP)), pltpu.SemaphoreType.DMA((n, N_CHIP))],
    )(*gs)


def _pair_add(g, recv, ids, name, tr=256):
    _, r, c = g.shape
    tr = _tile(r, tr, 16)

    def body(ids_ref, g_ref, r_ref, p_ref, own_ref):
        kk = pl.program_id(1)
        tot = g_ref[0].astype(F32) + r_ref[0].astype(F32)
        p_ref[0] = tot.astype(BF16)

        @pl.when(kk == ids_ref[1])
        def _():
            own_ref[...] = tot

    grid_spec = pltpu.PrefetchScalarGridSpec(
        num_scalar_prefetch=1, grid=(r // tr, N_CHIP),
        in_specs=[pl.BlockSpec((1, tr, c), lambda i, kk, ids: (2 * kk + ids[0], i, 0)),
                  pl.BlockSpec((1, tr, c), lambda i, kk, ids: (kk, i, 0))],
        out_specs=[pl.BlockSpec((1, tr, c), lambda i, kk, ids: (kk, i, 0)),
                   pl.BlockSpec((tr, c), lambda i, kk, ids: (i, 0))])
    return pl.pallas_call(
        body, name=name, grid_spec=grid_spec,
        out_shape=[jax.ShapeDtypeStruct((N_CHIP, r, c), BF16), jax.ShapeDtypeStruct((r, c), F32)],
        compiler_params=_params("parallel", "arbitrary"),
    )(ids, g, recv)


def _chip_exchange_stages(p_refs, recv_refs, send_sems, recv_sems):
    n = len(p_refs)
    xi, yi, ci = _coords()
    mine = 2 * xi + yi
    chips = _other_chips(xi, yi)

    def copy(a, k, cx, cy):
        return pltpu.make_async_remote_copy(
            src_ref=p_refs[a].at[2 * cx + cy], dst_ref=recv_refs[a].at[mine],
            send_sem=send_sems.at[a, k], recv_sem=recv_sems.at[a, k],
            device_id=(cx, cy, ci), device_id_type=MESH)

    def landed(a, k, cx, cy):
        return pltpu.make_async_remote_copy(
            src_ref=p_refs[a].at[mine], dst_ref=recv_refs[a].at[2 * cx + cy],
            send_sem=send_sems.at[a, k], recv_sem=recv_sems.at[a, k],
            device_id=(cx, cy, ci), device_id_type=MESH)

    def start():
        for a in range(n):
            for k, (cx, cy) in enumerate(chips):
                copy(a, k, cx, cy).start()

    def finish():
        for a in range(n):
            for k, (cx, cy) in enumerate(chips):
                landed(a, k, cx, cy).wait_recv()
        for a in range(n):
            for k, (cx, cy) in enumerate(chips):
                copy(a, k, cx, cy).wait_send()

    return start, finish


def _chip_exchange_scratch(n):
    return [pltpu.SemaphoreType.DMA((n, 3)), pltpu.SemaphoreType.DMA((n, 3))]


def _chip_exchange(ps, name):
    n = len(ps)

    def body(*refs):
        start, finish = _chip_exchange_stages(refs[:n], refs[n:2 * n], *refs[2 * n:])
        start()
        finish()

    return pl.pallas_call(
        body, name=name, out_shape=[jax.ShapeDtypeStruct(p.shape, p.dtype) for p in ps],
        in_specs=[ANY] * n, out_specs=[ANY] * n, scratch_shapes=_chip_exchange_scratch(n),
    )(*ps)


def _adamw_math(w, g, m, v):
    m = ADAM_B1 * m + (1.0 - ADAM_B1) * g
    v = ADAM_B2 * v + (1.0 - ADAM_B2) * (g * g)
    m_hat = m / (1.0 - ADAM_B1 ** ADAM_STEP)
    v_hat = v / (1.0 - ADAM_B2 ** ADAM_STEP)
    delta = -ADAM_LR * (m_hat / (jnp.sqrt(v_hat) + ADAM_EPS) + ADAM_WD * w)
    return delta, m, v


def _adamw_reduce(own, recv, ids, w, m, v, name, tr=256):
    r, c = w.shape
    tr = _tile(r, tr, 16)

    def body(ids_ref, own_ref, recv_ref, w_ref, m_ref, v_ref, g_ref, d_ref, mo_ref, vo_ref):
        mine = ids_ref[1]
        g = None
        for kk in range(N_CHIP):
            term = jnp.where(mine == kk, own_ref[...], recv_ref[kk].astype(F32))
            g = term if g is None else g + term
        delta, m_new, v_new = _adamw_math(w_ref[...], g, m_ref[...], v_ref[...])
        g_ref[...] = g
        d_ref[...] = delta
        mo_ref[...] = m_new
        vo_ref[...] = v_new

    row = pl.BlockSpec((tr, c), lambda i, ids: (i, 0))
    grid_spec = pltpu.PrefetchScalarGridSpec(
        num_scalar_prefetch=1, grid=(r // tr,),
        in_specs=[row, pl.BlockSpec((N_CHIP, tr, c), lambda i, ids: (0, i, 0)), row, row, row],
        out_specs=[row, row, row, row])
    out = jax.ShapeDtypeStruct((r, c), F32)
    return pl.pallas_call(
        body, name=name, grid_spec=grid_spec, out_shape=[out, out, out, out],
        compiler_params=_params("parallel"),
    )(ids, own, recv, w, m, v)


def _sum_sources(a, name):
    n, r, c = a.shape

    def body(a_ref, o_ref):
        tot = a_ref[0]
        for kk in range(1, n):
            tot = tot + a_ref[kk]
        o_ref[...] = tot

    return pl.pallas_call(
        body, name=name, out_shape=jax.ShapeDtypeStruct((r, c), F32),
        in_specs=[pl.BlockSpec(memory_space=pltpu.VMEM)], out_specs=pl.BlockSpec(memory_space=pltpu.VMEM),
    )(a)


def _adamw_small(w, g, m, v, name):
    def body(w_ref, g_ref, m_ref, v_ref, d_ref, mo_ref, vo_ref):
        delta, m_new, v_new = _adamw_math(w_ref[...], g_ref[...], m_ref[...], v_ref[...])
        d_ref[...] = delta
        mo_ref[...] = m_new
        vo_ref[...] = v_new

    vm = pl.BlockSpec(memory_space=pltpu.VMEM)
    out = jax.ShapeDtypeStruct(w.shape, F32)
    return pl.pallas_call(body, name=name, out_shape=[out, out, out], in_specs=[vm] * 4, out_specs=[vm] * 3)(w, g, m, v)


def _pack(parts, width, row_mult):
    flat = jnp.concatenate([p.reshape(-1) for p in parts])
    rows = -(-flat.shape[0] // width)
    rows = -(-rows // row_mult) * row_mult
    return jnp.pad(flat, (0, rows * width - flat.shape[0])).reshape(rows, width)


def _unpack(flat, shapes):
    out, off = [], 0
    lead = flat.shape[:-1]
    for shp in shapes:
        n = 1
        for dd in shp:
            n *= dd
        out.append(flat[..., off:off + n].reshape(lead + tuple(shp)))
        off += n
    return out


def _rows2d(w):
    return w.reshape(w.shape[0] * w.shape[1], w.shape[2])


def _cols_to_dev(g):
    l, k, n = g.shape
    return g.reshape(l * k, N_DEV, n // N_DEV).transpose(1, 0, 2)


def _rows_to_dev(g):
    l, k, n = g.shape
    rs = k // N_DEV
    return g.reshape(l, N_DEV, rs, n).transpose(1, 0, 2, 3).reshape(N_DEV, l * rs, n)


def _dev_to_cols(a, l):
    _, lk, cs = a.shape
    return a.transpose(1, 0, 2).reshape(l, lk // l, N_DEV * cs)


def _dev_to_rows(a, l):
    _, lr, n = a.shape
    rs = lr // l
    return a.reshape(N_DEV, l, rs, n).transpose(1, 0, 2, 3).reshape(l, N_DEV * rs, n)


def kernel(x, attn_norm, attn_w_in, attn_f_bias, fox_q_gain, fox_k_gain, sb_q_gain, sb_k_gain, attn_w_out, conv_norm, conv_w_in, conv_kernel, conv_w_out, ffn_norm, ffn_w_up, ffn_conv, ffn_w_down, loss_target, m_attn_norm, m_attn_w_in, m_attn_f_bias, m_fox_q_gain, m_fox_k_gain, m_sb_q_gain, m_sb_k_gain, m_attn_w_out, m_conv_norm, m_conv_w_in, m_conv_kernel, m_conv_w_out, m_ffn_norm, m_ffn_w_up, m_ffn_conv, m_ffn_w_down, v_attn_norm, v_attn_w_in, v_attn_f_bias, v_fox_q_gain, v_fox_k_gain, v_sb_q_gain, v_sb_k_gain, v_attn_w_out, v_conv_norm, v_conv_w_in, v_conv_kernel, v_conv_w_out, v_ffn_norm, v_ffn_w_up, v_ffn_conv, v_ffn_w_down):
    s = x.shape[1]
    n_attn, n_conv, depth = attn_w_in.shape[0], conv_w_in.shape[0], ffn_w_up.shape[0]
    xi, yi, ci = _coords()
    me = 4 * xi + 2 * yi + ci
    ids = jnp.stack([ci, 2 * xi + yi]).astype(jnp.int32)

    big = [attn_w_in, attn_w_out, conv_w_in, conv_w_out, ffn_w_up, ffn_w_down]
    big_m = [m_attn_w_in, m_attn_w_out, m_conv_w_in, m_conv_w_out, m_ffn_w_up, m_ffn_w_down]
    big_v = [v_attn_w_in, v_attn_w_out, v_conv_w_in, v_conv_w_out, v_ffn_w_up, v_ffn_w_down]
    big_names = ["attn_w_in", "attn_w_out", "conv_w_in", "conv_w_out", "ffn_w_up", "ffn_w_down"]
    small_sh = [conv_norm, conv_kernel, ffn_conv]
    small_sh_shapes = [w.shape for w in small_sh]
    rep = [attn_norm, attn_f_bias, fox_q_gain, fox_k_gain, sb_q_gain, sb_k_gain, ffn_norm]
    rep_shapes = [w.shape for w in rep]

    small_pack = _pack(small_sh, LANES, 8)
    shards_bf16 = [_rows2d(w).astype(BF16) for w in big]
    early = _all_gather([attn_w_in[0].astype(BF16), small_pack], "gather_first")
    first_w_in = jnp.pad(_dev_to_cols(early[0], 1)[0], ((0, 0), (0, ATTN_IN_PAD - ATTN_IN)))
    cn, ckern, fconv = _unpack(early[1].reshape(N_DEV, -1), small_sh_shapes)
    conv_norm_f = cn.transpose(1, 0, 2).reshape(n_conv, D_MODEL)
    conv_kernel_f = ckern.transpose(1, 2, 0, 3).reshape(n_conv, 3, D_MODEL)
    ffn_conv_f = fconv.transpose(1, 2, 0, 3).reshape(depth, 3, 2 * D_FF)

    def pair_gain(fox_g, sb_g):
        f2 = jnp.concatenate([fox_g, fox_g])
        s2 = jnp.concatenate([sb_g, sb_g])
        return jnp.concatenate([jnp.tile(f2[None], (4, 1)), jnp.tile(s2[None], (4, 1))])[:, None, :]

    h = x[0]
    saved = []
    for layer in range(depth):
        i = layer // 2
        tag = "l%d" % layer
        rec = {"h_in": h}
        if layer % 2 == 0:
            xn = xn_next if layer else _rms_fwd(h, attn_norm[i], tag + "_attn_rms")
            proj = _matmul(xn, first_w_in if layer == 0 else a_w_in[i], tag + "_attn_in", tn=640)
            gq, gk = pair_gain(fox_q_gain[i], sb_q_gain[i]), pair_gain(fox_k_gain[i], sb_k_gain[i])
            qh, kh, vb, qt, kt, vt = _qkv_prep(proj, gq, gk, tag + "_qkv_prep")
            logit = proj[:, 3 * MIX:3 * MIX + H_FOX].T.reshape(H_FOX, s // LANES, LANES)
            cum = _fgate_fwd(logit, attn_f_bias[i], tag + "_fgate")
            frow = cum.reshape(H_FOX // 2, 2, s)
            fcol = frow.transpose(0, 2, 1)
            o_fox, lse = _fox_fwd(qt, kh, vt, frow, fcol, tag + "_fox_fwd")
            if layer == 0:
                o_sb, tot, *gathered = _sb_fwd(qt, kh, vt, tag + "_sb_fwd", gather=shards_bf16)
                a_w_in = _dev_to_cols(gathered[0], n_attn)
                a_w_in = jnp.pad(a_w_in, ((0, 0), (0, 0), (0, ATTN_IN_PAD - ATTN_IN)))
                a_w_out = _dev_to_rows(gathered[1], n_attn)
                c_w_in = _dev_to_cols(gathered[2], n_conv)
                c_w_out = _dev_to_rows(gathered[3], n_conv)
                f_w_up = _dev_to_cols(gathered[4], depth)
                f_w_down = _dev_to_rows(gathered[5], depth)
            else:
                o_sb, tot = _sb_fwd(qt, kh, vt, tag + "_sb_fwd")
            o = jnp.concatenate([o_fox, o_sb], axis=1)
            h, xn2 = _matmul_rows(o, a_w_out[i], tag + "_attn_out", add=h, norm_gain=ffn_norm[layer], tm=1024)
            rec.update(xn=xn, proj=proj, gq=gq, gk=gk, qh=qh, kh=kh, vb=vb, qt=qt, kt=kt, logit=logit, frow=frow,
                       fcol=fcol, o_fox=o_fox, lse=lse, tot=tot, o=o)
        else:
            xn = xn_next
            proj = _matmul(xn, c_w_in[i], tag + "_conv_in", tn=1024)
            y = _sconv_fwd(proj, conv_kernel_f[i], tag + "_sconv_fwd")
            h, xn2 = _matmul_rows(y, c_w_out[i], tag + "_conv_out", add=h, norm_gain=ffn_norm[layer], tm=1024)
            rec.update(xn=xn, proj=proj, y=y)
        rec["h_mid"] = h
        up = _matmul(xn2, f_w_up[layer], tag + "_ffn_up", tn=1408)
        act = _ffn_act_fwd(up, ffn_conv_f[layer], tag + "_ffn_act")
        if layer + 1 < depth:
            nxt = layer + 1
            gain = attn_norm[nxt // 2] if nxt % 2 == 0 else conv_norm_f[nxt // 2]
            h, xn_next = _matmul_rows(act, f_w_down[layer], tag + "_ffn_down", add=h, norm_gain=gain, tm=1024,
                                      tk=1408)
        else:
            h = _matmul(act, f_w_down[layer], tag + "_ffn_down", add=h, tn=1024, tk=2816)
        rec.update(xn2=xn2, up=up, act=act)
        saved.append(rec)

    loss_local, dh = _loss_head(h, loss_target[0], "loss_head")
    loss = lax.psum(loss_local, ("x", "y", "c"))

    g_attn_norm, g_attn_w_in, g_f_bias = [None] * n_attn, [None] * n_attn, [None] * n_attn
    g_fq, g_fk, g_sq, g_sk, g_attn_w_out = ([None] * n_attn for _ in range(5))
    g_conv_norm, g_conv_w_in, g_conv_kernel, g_conv_w_out = ([None] * n_conv for _ in range(4))
    g_ffn_norm, g_ffn_w_up, g_ffn_conv, g_ffn_w_down = ([None] * depth for _ in range(4))
    everything = slice(0, None)
    early_layers = {nm: (slice(1, None) if nm == "attn_w_in" else everything) for nm in big_names}
    late_layers = {"attn_w_in": slice(0, 1)}

    def slabs_for_devices(layers):
        stacks = {"attn_w_in": (g_attn_w_in, _cols_to_dev), "attn_w_out": (g_attn_w_out, _rows_to_dev),
                  "conv_w_in": (g_conv_w_in, _cols_to_dev), "conv_w_out": (g_conv_w_out, _rows_to_dev),
                  "ffn_w_up": (g_ffn_w_up, _cols_to_dev), "ffn_w_down": (g_ffn_w_down, _rows_to_dev)}
        return [stacks[nm][1](jnp.stack(stacks[nm][0][layers[nm]]).astype(BF16)) for nm in big_names if nm in layers]

    for layer in reversed(range(depth)):
        i = layer // 2
        tag = "l%d" % layer
        rec = saved[layer]
        da = _matmul(dh, f_w_down[layer].T, tag + "_ffn_down_dx", tn=1408)
        g_ffn_w_down[layer] = _matmul_tn(rec["act"], dh, tag + "_ffn_down_dw", tm=1408, tn=1024)
        dup, dwg, dwv = _ffn_act_bwd(rec["up"], ffn_conv_f[layer], da, tag + "_ffn_act_bwd")
        g_ffn_conv[layer] = jnp.concatenate([dwg, dwv], axis=1)
        g_ffn_w_up[layer] = _matmul_tn(rec["xn2"], dup, tag + "_ffn_up_dw", tn=1408)
        dh, g_ffn_norm[layer] = _matmul_rows(dup, f_w_up[layer].T, tag + "_ffn_up_dx", tm=1024, tk=1408,
                                             rms_bwd=(rec["h_mid"], ffn_norm[layer], dh))
        if layer % 2 == 0:
            do = _matmul(dh, a_w_out[i].T, tag + "_attn_out_dx", tn=1024)
            g_attn_w_out[i] = _matmul_tn(rec["o"], dh, tag + "_attn_out_dw", tn=1024)
            if layer == 0:
                early_g = slabs_for_devices(early_layers)
                early_pairs = [_pair_add(g, r, ids, "reduce_pair_add_" + nm)
                               for g, r, nm in zip(early_g, _sibling_exchange(early_g, "reduce_sibling"), big_names)]
            dq_f, dk_f, dv_f, dfk, dfq = _fox_bwd(rec["qh"], rec["qt"], rec["kh"], rec["kt"], rec["vb"], rec["frow"],
                                                  rec["fcol"], rec["lse"], rec["o_fox"], do, tag + "_fox_bwd")
            dq_s, dk_s, dv_s, *from_chips = _sb_bwd(
                rec["qh"], rec["qt"], rec["kh"], rec["kt"], rec["vb"], rec["tot"], do, tag + "_sb_bwd",
                exchange=[pr[0] for pr in early_pairs] if layer == 0 else ())
            if layer == 0:
                early_from_chips = from_chips
            dq, dk, dv, dgq, dgk = _qkv_prep_bwd(rec["proj"], rec["gq"], rec["gk"], (dq_f, dq_s), (dk_f, dk_s),
                                                 (dv_f, dv_s), tag + "_qkv_prep_bwd")
            dcum = (dfq + dfk[:, :, 0:2].transpose(0, 2, 1)).reshape(H_FOX, s // LANES, LANES)
            dlogit, dbias = _fgate_bwd(rec["logit"], attn_f_bias[i], dcum, tag + "_fgate_bwd")
            g_f_bias[i] = dbias[:, 0]
            dgate = jnp.pad(dlogit.reshape(H_FOX, s).T, ((0, 0), (0, LANES - H_FOX))).astype(BF16)
            dproj = jnp.concatenate([dq, dk, dv, dgate], axis=1)

            def fold(dg):
                per_head = dg.reshape(16, HEAD_DIM)
                return jnp.sum(per_head[:8], axis=0), jnp.sum(per_head[8:], axis=0)

            g_fq[i], g_sq[i] = fold(dgq)
            g_fk[i], g_sk[i] = fold(dgk)
            g_attn_w_in[i] = _matmul_tn(rec["xn"], dproj, tag + "_attn_in_dw", tn=640)[:, :ATTN_IN]
            dh, g_attn_norm[i] = _matmul_rows(dproj, a_w_in[i].T, tag + "_attn_in_dx", tm=1024, tk=640,
                                              rms_bwd=(rec["h_in"], attn_norm[i], dh))
        else:
            dy = _matmul(dh, c_w_out[i].T, tag + "_conv_out_dx", tn=1024)
            g_conv_w_out[i] = _matmul_tn(rec["y"], dh, tag + "_conv_out_dw", tn=1024)
            dproj, g_conv_kernel[i] = _sconv_bwd(rec["proj"], conv_kernel_f[i], dy, tag + "_sconv_bwd")
            g_conv_w_in[i] = _matmul_tn(rec["xn"], dproj, tag + "_conv_in_dw", tn=1024)
            dh, g_conv_norm[i] = _matmul_rows(dproj, c_w_in[i].T, tag + "_conv_in_dx", tm=1024, tk=1024,
                                              rms_bwd=(rec["h_in"], conv_norm_f[i], dh))
    grad_x = dh[None]

    late_names = [nm for nm in big_names if nm in late_layers]
    late_g = slabs_for_devices(late_layers)
    late_pairs = [_pair_add(g, r, ids, "reduce_pair_add_late_" + nm)
                  for g, r, nm in zip(late_g, _sibling_exchange(late_g, "reduce_sibling_late"), late_names)]
    late_from_chips = _chip_exchange([pr[0] for pr in late_pairs], "reduce_chips_late")

    def update(piece_pairs, piece_recv, piece_names, layers):
        outs = {}
        for (own, recv, nm) in zip([pr[1] for pr in piece_pairs], piece_recv, piece_names):
            sl = layers[nm]
            which = big_names.index(nm)
            outs[nm] = _adamw_reduce(own, recv, ids, _rows2d(big[which][sl]), _rows2d(big_m[which][sl]),
                                     _rows2d(big_v[which][sl]), "adamw_%s_%d" % (nm, sl.start))
        return outs

    late_out = update(late_pairs, late_from_chips, late_names, late_layers)
    early_out = update(early_pairs, early_from_chips, big_names, early_layers)
    grads_big, delta_big, newm_big, newv_big = [], [], [], []
    for which, nm in enumerate(big_names):
        shp = big[which].shape
        for k, dest in enumerate((grads_big, delta_big, newm_big, newv_big)):
            parts = ([late_out[nm][k]] if nm in late_out else []) + [early_out[nm][k]]
            dest.append(jnp.concatenate(parts, axis=0).reshape(shp))

    rep_g = [jnp.stack(g_attn_norm), jnp.stack(g_f_bias), jnp.stack(g_fq), jnp.stack(g_fk), jnp.stack(g_sq),
             jnp.stack(g_sk), jnp.stack(g_ffn_norm)]
    sh_g = [jnp.stack(g_conv_norm).reshape(n_conv, N_DEV, -1).transpose(1, 0, 2),
            jnp.stack(g_conv_kernel).reshape(n_conv, 3, N_DEV, -1).transpose(2, 0, 1, 3),
            jnp.stack(g_ffn_conv).reshape(depth, 3, N_DEV, -1).transpose(2, 0, 1, 3)]
    n_rep = sum(int(a.size) for a in rep)
    n_sh = sum(int(a.size) for a in small_sh)
    partial = _pack(rep_g + [jnp.concatenate([a.reshape(N_DEV, -1) for a in sh_g], axis=1)], LANES, 8)
    total = _sum_sources(_all_gather([partial], "gather_small_grads")[0], "sum_small_grads").reshape(-1)
    rep_tot = total[:n_rep]
    sh_tot = lax.dynamic_slice_in_dim(total[n_rep:n_rep + N_DEV * n_sh].reshape(N_DEV, n_sh), me, 1, axis=0)[0]
    g_small = _pack([rep_tot, sh_tot], LANES, 8)

    def small_pack_of(rep_list, sh_list):
        return _pack(rep_list + sh_list, LANES, 8)

    d_small, m_small, v_small = _adamw_small(
        small_pack_of(rep, small_sh), g_small,
        small_pack_of([m_attn_norm, m_attn_f_bias, m_fox_q_gain, m_fox_k_gain, m_sb_q_gain, m_sb_k_gain, m_ffn_norm],
                      [m_conv_norm, m_conv_kernel, m_ffn_conv]),
        small_pack_of([v_attn_norm, v_attn_f_bias, v_fox_q_gain, v_fox_k_gain, v_sb_q_gain, v_sb_k_gain, v_ffn_norm],
                      [v_conv_norm, v_conv_kernel, v_ffn_conv]),
        "adamw_small")
    small_shapes = rep_shapes + small_sh_shapes

    def split_small(a):
        return _unpack(a.reshape(-1), small_shapes)

    def ordered(big_list, small_list):
        an, fb, fq, fk, sq, sk, fn, cno, cke, fco = small_list
        awi, awo, cwi, cwo, fwu, fwd = big_list
        return [an, awi, fb, fq, fk, sq, sk, awo, cno, cwi, cke, cwo, fn, fwu, fco, fwd]

    grads = ordered(grads_big, split_small(g_small))
    deltas = ordered(delta_big, split_small(d_small))
    new_m = ordered(newm_big, split_small(m_small))
    new_v = ordered(newv_big, split_small(v_small))
    return (loss, grad_x, *grads, *deltas, *new_m, *new_v)
```

```python
import jax
import jax.numpy as jnp
from jax import lax
from jax.experimental import pallas as pl
from jax.experimental.pallas import tpu as pltpu

F32 = jnp.float32
BF16 = jnp.bfloat16

D_MODEL = 1024
HEAD_DIM = 64
H_FOX = 8
MIX = 1024
ATTN_IN = 3 * MIX + H_FOX
ATTN_IN_PAD = 3 * MIX + 128
D_FF = 2816
EPS = 1e-6
NEG = -1e30
LANES = 128
N_DEV = 8
N_CHIP = 4

ADAM_LR = 0.001
ADAM_B1 = 0.9
ADAM_B2 = 0.999
ADAM_EPS = 1e-08
ADAM_WD = 0.01
ADAM_STEP = 10

VMEM_LIMIT = 56 * 1024 * 1024
MESH = pl.DeviceIdType.MESH
ANY = pl.BlockSpec(memory_space=pl.ANY)


def _params(*sem):
    return pltpu.CompilerParams(dimension_semantics=sem, vmem_limit_bytes=VMEM_LIMIT)


def _tile(n, target, mult=LANES):
    best = None
    for t in range(mult, min(n, target) + 1, mult):
        if n % t == 0:
            best = t
    return best if best is not None else n


def _dot(a, b):
    return jnp.dot(a, b, preferred_element_type=F32)


def _dot_tn(a, b):
    return lax.dot_general(a, b, (((0,), (0,)), ((), ())), preferred_element_type=F32)


def _split_dot(x, m, passes):
    acc = None
    rem = x
    for _ in range(passes):
        part = rem.astype(BF16)
        term = _dot(part, m)
        acc = term if acc is None else acc + term
        rem = rem - part.astype(F32)
    return acc


def _split_dot_left(m, x, passes):
    acc = None
    rem = x
    for _ in range(passes):
        part = rem.astype(BF16)
        term = _dot(m, part)
        acc = term if acc is None else acc + term
        rem = rem - part.astype(F32)
    return acc


def _matmul(a, b, name, add=None, out_dtype=F32, tm=1024, tn=512, tk=1024):
    split = a.shape[0] if a.ndim == 3 else 1
    m, kh = a.shape[-2:]
    k = split * kh
    n = b.shape[1]
    tm, tn, tk = _tile(m, tm, 8), _tile(n, tn), _tile(kh, tk)
    nk = k // tk
    per_slab = kh // tk
    has_add = add is not None

    def body(*refs):
        a_ref, b_ref = refs[0], refs[1]
        add_ref = refs[2] if has_add else None
        o_ref = refs[2 + has_add]

        def finish(acc):
            if has_add:
                acc = acc + add_ref[...]
            o_ref[...] = acc.astype(out_dtype)

        p = _dot(a_ref[...].astype(BF16), b_ref[...].astype(BF16))
        if nk == 1:
            finish(p)
        else:
            acc_ref = refs[-1]
            kk = pl.program_id(2)

            @pl.when(kk == 0)
            def _():
                acc_ref[...] = p

            @pl.when(kk > 0)
            def _():
                acc_ref[...] += p

            @pl.when(kk == nk - 1)
            def _():
                finish(acc_ref[...])

    if split == 1:
        a_spec = pl.BlockSpec((tm, tk), lambda i, j, kk: (i, kk))
    else:
        a_spec = pl.BlockSpec((None, tm, tk), lambda i, j, kk: (kk // per_slab, i, kk % per_slab))
    in_specs = [a_spec, pl.BlockSpec((tk, tn), lambda i, j, kk: (kk, j))]
    args = [a, b]
    if has_add:
        in_specs.append(pl.BlockSpec((tm, tn), lambda i, j, kk: (i, j)))
        args.append(add)
    return pl.pallas_call(
        body, name=name, grid=(m // tm, n // tn, nk), in_specs=in_specs,
        out_specs=pl.BlockSpec((tm, tn), lambda i, j, kk: (i, j)),
        out_shape=jax.ShapeDtypeStruct((m, n), out_dtype),
        scratch_shapes=[pltpu.VMEM((tm, tn), F32)] if nk > 1 else [],
        compiler_params=_params("parallel", "parallel", "arbitrary"),
    )(*args)


def _matmul_rows(a, b, name, add=None, norm_gain=None, rms_bwd=None, tm=512, tk=1024):
    split = a.shape[0] if a.ndim == 3 else 1
    m, kh = a.shape[-2:]
    n = b.shape[1]
    tm, tk = _tile(m, tm, 8), _tile(kh, tk)
    nk = split * kh // tk
    per_slab = kh // tk
    has_add, has_norm, has_bwd = add is not None, norm_gain is not None, rms_bwd is not None

    def rsqrt_mean_sq(x):
        return lax.rsqrt(jnp.mean(x * x, axis=-1, keepdims=True) + EPS)

    def body(*refs):
        it = iter(refs)
        a_ref, b_ref = next(it), next(it)
        add_ref = next(it) if has_add else None
        gain_ref = next(it) if has_norm else None
        h_ref, g_ref, dhin_ref = (next(it), next(it), next(it)) if has_bwd else (None, None, None)
        o_ref = next(it)
        xn_ref = next(it) if has_norm else None
        dg_ref = next(it) if has_bwd else None
        acc_ref = next(it) if nk > 1 else None
        i = pl.program_id(0)

        def finish(acc):
            if has_add:
                acc = acc + add_ref[...]
            if has_bwd:
                x = h_ref[...]
                r = rsqrt_mean_sq(x)
                xh = x * r

                @pl.when(i == 0)
                def _():
                    dg_ref[...] = jnp.zeros_like(dg_ref)

                dg_ref[0:1, :] += jnp.sum(acc * xh, axis=0, keepdims=True)
                dxh = acc * g_ref[...]
                acc = dhin_ref[...] + r * (dxh - xh * jnp.mean(dxh * xh, axis=-1, keepdims=True))
            o_ref[...] = acc
            if has_norm:
                xn_ref[...] = (acc * rsqrt_mean_sq(acc) * gain_ref[...]).astype(BF16)

        p = _dot(a_ref[...].astype(BF16), b_ref[...].astype(BF16))
        if nk == 1:
            finish(p)
        else:
            kk = pl.program_id(1)

            @pl.when(kk == 0)
            def _():
                acc_ref[...] = p

            @pl.when(kk > 0)
            def _():
                acc_ref[...] += p

            @pl.when(kk == nk - 1)
            def _():
                finish(acc_ref[...])

    if split == 1:
        a_spec = pl.BlockSpec((tm, tk), lambda i, kk: (i, kk))
    else:
        a_spec = pl.BlockSpec((None, tm, tk), lambda i, kk: (kk // per_slab, i, kk % per_slab))
    row = pl.BlockSpec((tm, n), lambda i, kk: (i, 0))
    vec = pl.BlockSpec((1, n), lambda i, kk: (0, 0))
    in_specs, args = [a_spec, pl.BlockSpec((tk, n), lambda i, kk: (kk, 0))], [a, b]
    if has_add:
        in_specs.append(row)
        args.append(add)
    if has_norm:
        in_specs.append(vec)
        args.append(norm_gain.reshape(1, n))
    if has_bwd:
        in_specs += [row, vec, row]
        args += [rms_bwd[0], rms_bwd[1].reshape(1, n), rms_bwd[2]]
    out_specs, out_shape = [row], [jax.ShapeDtypeStruct((m, n), F32)]
    if has_norm:
        out_specs.append(row)
        out_shape.append(jax.ShapeDtypeStruct((m, n), BF16))
    if has_bwd:
        out_specs.append(pl.BlockSpec((8, n), lambda i, kk: (0, 0)))
        out_shape.append(jax.ShapeDtypeStruct((8, n), F32))
    outs = pl.pallas_call(
        body, name=name, grid=(m // tm, nk), in_specs=in_specs, out_specs=out_specs, out_shape=out_shape,
        scratch_shapes=[pltpu.VMEM((tm, n), F32)] if nk > 1 else [],
        compiler_params=_params("arbitrary" if has_bwd else "parallel", "arbitrary"),
    )(*args)
    if has_bwd:
        return outs[0], outs[-1][0]
    return tuple(outs) if has_norm else outs[0]


def _matmul_tn(a, b, name, tm=1024, tn=512, ts=2048):
    s, m = a.shape
    split = b.shape[0] if b.ndim == 3 else 1
    nh = b.shape[-1]
    n = split * nh
    tm, tn, ts = _tile(m, tm), _tile(nh, tn), _tile(s, ts, 8)
    per_slab = nh // tn
    if split == 1:
        b_spec = pl.BlockSpec((ts, tn), lambda i, j, kk: (kk, j))
    else:
        b_spec = pl.BlockSpec((None, ts, tn), lambda i, j, kk: (j // per_slab, kk, j % per_slab))

    def body(a_ref, b_ref, o_ref):
        kk = pl.program_id(2)
        p = _dot_tn(a_ref[...].astype(BF16), b_ref[...].astype(BF16))

        @pl.when(kk == 0)
        def _():
            o_ref[...] = p

        @pl.when(kk > 0)
        def _():
            o_ref[...] += p

    return pl.pallas_call(
        body, name=name, grid=(m // tm, n // tn, s // ts),
        in_specs=[pl.BlockSpec((ts, tm), lambda i, j, kk: (kk, i)), b_spec],
        out_specs=pl.BlockSpec((tm, tn), lambda i, j, kk: (i, j)),
        out_shape=jax.ShapeDtypeStruct((m, n), F32),
        compiler_params=_params("parallel", "parallel", "arbitrary"),
    )(a, b)


def _rms_fwd(h, g, name, ts=512):
    s, d = h.shape
    ts = _tile(s, ts, 8)

    def body(h_ref, g_ref, o_ref):
        x = h_ref[...]
        r = lax.rsqrt(jnp.mean(x * x, axis=-1, keepdims=True) + EPS)
        o_ref[...] = (x * r * g_ref[...]).astype(BF16)

    return pl.pallas_call(
        body, name=name, grid=(s // ts,),
        in_specs=[pl.BlockSpec((ts, d), lambda i: (i, 0)), pl.BlockSpec((1, d), lambda i: (0, 0))],
        out_specs=pl.BlockSpec((ts, d), lambda i: (i, 0)),
        out_shape=jax.ShapeDtypeStruct((s, d), BF16),
        compiler_params=_params("parallel"),
    )(h, g.reshape(1, d))


def _rms_bwd(h, dxn, g, dh_in, name, ts=512):
    s, d = h.shape
    ts = _tile(s, ts, 8)

    def body(h_ref, dxn_ref, g_ref, dhin_ref, dh_ref, dg_ref):
        i = pl.program_id(0)
        x = h_ref[...]
        r = lax.rsqrt(jnp.mean(x * x, axis=-1, keepdims=True) + EPS)
        xh = x * r
        dxn_v = dxn_ref[...]

        @pl.when(i == 0)
        def _():
            dg_ref[...] = jnp.zeros_like(dg_ref)

        dg_ref[0:1, :] += jnp.sum(dxn_v * xh, axis=0, keepdims=True)
        dxh = dxn_v * g_ref[...]
        dx = r * (dxh - xh * jnp.mean(dxh * xh, axis=-1, keepdims=True))
        dh_ref[...] = dhin_ref[...] + dx

    row = pl.BlockSpec((ts, d), lambda i: (i, 0))
    dh, dg = pl.pallas_call(
        body, name=name, grid=(s // ts,),
        in_specs=[row, row, pl.BlockSpec((1, d), lambda i: (0, 0)), row],
        out_specs=[row, pl.BlockSpec((8, d), lambda i: (0, 0))],
        out_shape=[jax.ShapeDtypeStruct((s, d), F32), jax.ShapeDtypeStruct((8, d), F32)],
        compiler_params=_params("arbitrary"),
    )(h, dxn, g.reshape(1, d), dh_in)
    return dh, dg[0]


def _shift_down(x, prev):
    rows = lax.broadcasted_iota(jnp.int32, (8, x.shape[1]), 0)
    p1, p2 = prev[7:8, :], prev[6:7, :]
    r1, r2 = pltpu.roll(x, 1, 0), pltpu.roll(x, 2, 0)
    top1 = jnp.where(rows == 0, p1, r1[0:8, :])
    top2 = jnp.where(rows == 0, p2, jnp.where(rows == 1, p1, r2[0:8, :]))
    if x.shape[0] == 8:
        return top1, top2
    return jnp.concatenate([top1, r1[8:, :]], axis=0), jnp.concatenate([top2, r2[8:, :]], axis=0)


def _shift_up(x, nxt):
    n = x.shape[0]
    rows = lax.broadcasted_iota(jnp.int32, (8, x.shape[1]), 0)
    n0, n1 = nxt[0:1, :], nxt[1:2, :]
    r1, r2 = pltpu.roll(x, n - 1, 0), pltpu.roll(x, n - 2, 0)
    end1 = jnp.where(rows == 7, n0, r1[n - 8:, :])
    end2 = jnp.where(rows == 7, n1, jnp.where(rows == 6, n0, r2[n - 8:, :]))
    return jnp.concatenate([r1[:n - 8, :], end1], axis=0), jnp.concatenate([r2[:n - 8, :], end2], axis=0)


def _conv(x, x1, x2, w):
    return w[2:3, :] * x + w[1:2, :] * x1 + w[0:1, :] * x2


def _halo_specs(ts, tc, col, n_time_blocks):
    r8 = ts // 8
    main = pl.BlockSpec((ts, tc), lambda j, i: (i, j + col))
    prev = pl.BlockSpec((8, tc), lambda j, i: (jnp.maximum(i * r8 - 1, 0), j + col))
    nxt = pl.BlockSpec((8, tc), lambda j, i: (jnp.minimum((i + 1) * r8, n_time_blocks * r8 - 1), j + col))
    return main, prev, nxt


def _silu_parts(g):
    sig = 1.0 / (1.0 + jnp.exp(-g))
    return sig, g * sig


def _ffn_act_fwd(up, cw, name, ts=256, tc=1408):
    s = up.shape[0]
    ts, tc = _tile(s, ts, 8), _tile(D_FF, tc)
    nc, nt = D_FF // tc, s // ts

    def body(g_ref, gp_ref, v_ref, vp_ref, wg_ref, wv_ref, o_ref):
        first = pl.program_id(1) == 0

        def conv(x_ref, p_ref, w_ref):
            x = x_ref[...]
            prev = jnp.where(first, 0.0, p_ref[...])
            x1, x2 = _shift_down(x, prev)
            return _conv(x, x1, x2, w_ref[...])

        ug = conv(g_ref, gp_ref, wg_ref)
        uv = conv(v_ref, vp_ref, wv_ref)
        _, silu = _silu_parts(ug)
        o_ref[...] = (silu * uv).astype(BF16)

    g_main, g_prev, _ = _halo_specs(ts, tc, 0, nt)
    v_main, v_prev, _ = _halo_specs(ts, tc, nc, nt)
    return pl.pallas_call(
        body, name=name, grid=(nc, nt),
        in_specs=[g_main, g_prev, v_main, v_prev,
                  pl.BlockSpec((3, tc), lambda j, i: (0, j)), pl.BlockSpec((3, tc), lambda j, i: (0, j + nc))],
        out_specs=pl.BlockSpec((ts, tc), lambda j, i: (i, j)),
        out_shape=jax.ShapeDtypeStruct((s, D_FF), BF16),
        compiler_params=_params("parallel", "parallel"),
    )(up, up, up, up, cw, cw)


def _ffn_act_bwd(up, cw, da, name, ts=256, tc=1408):
    s = up.shape[0]
    ts, tc = _tile(s, ts, 8), _tile(D_FF, tc)
    nc, nt = D_FF // tc, s // ts

    def body(g_ref, gp_ref, gn_ref, v_ref, vp_ref, vn_ref, da_ref, dan_ref, wg_ref, wv_ref,
             d_ref, dwg_ref, dwv_ref):
        i = pl.program_id(1)
        first, last = i == 0, i == nt - 1
        wg, wv = wg_ref[...], wv_ref[...]
        g, v = g_ref[...], v_ref[...]
        g1, g2 = _shift_down(g, jnp.where(first, 0.0, gp_ref[...]))
        v1, v2 = _shift_down(v, jnp.where(first, 0.0, vp_ref[...]))

        def d_u(ug, uv, da_v):
            sig, silu = _silu_parts(ug)
            return da_v * uv * (sig * (1.0 + ug * (1.0 - sig))), da_v * silu

        dug, duv = d_u(_conv(g, g1, g2, wg), _conv(v, v1, v2, wv), da_ref[...])
        gn, vn = gn_ref[...], vn_ref[...]
        gn1, gn2 = _shift_down(gn, g[ts - 8:, :])
        vn1, vn2 = _shift_down(vn, v[ts - 8:, :])
        dugn, duvn = d_u(_conv(gn, gn1, gn2, wg), _conv(vn, vn1, vn2, wv), dan_ref[...])
        dugn = jnp.where(last, 0.0, dugn)
        duvn = jnp.where(last, 0.0, duvn)

        def finish(du, dun, x, x1, x2, w, dx_ref, dw_ref):
            d1, d2 = _shift_up(du, dun)
            dx_ref[...] = (w[2:3, :] * du + w[1:2, :] * d1 + w[0:1, :] * d2).astype(BF16)

            @pl.when(first)
            def _():
                dw_ref[...] = jnp.zeros_like(dw_ref)

            dw_ref[0:1, :] += jnp.sum(du * x2, axis=0, keepdims=True)
            dw_ref[1:2, :] += jnp.sum(du * x1, axis=0, keepdims=True)
            dw_ref[2:3, :] += jnp.sum(du * x, axis=0, keepdims=True)

        finish(dug, dugn, g, g1, g2, wg, d_ref.at[0], dwg_ref)
        finish(duv, duvn, v, v1, v2, wv, d_ref.at[1], dwv_ref)

    g_specs = _halo_specs(ts, tc, 0, nt)
    v_specs = _halo_specs(ts, tc, nc, nt)
    da_main, _, da_next = _halo_specs(ts, tc, 0, nt)
    taps = pl.BlockSpec((8, tc), lambda j, i: (0, j))
    halves = pl.BlockSpec((2, ts, tc), lambda j, i: (0, i, j))
    d, dwg, dwv = pl.pallas_call(
        body, name=name, grid=(nc, nt),
        in_specs=[*g_specs, *v_specs, da_main, da_next,
                  pl.BlockSpec((3, tc), lambda j, i: (0, j)), pl.BlockSpec((3, tc), lambda j, i: (0, j + nc))],
        out_specs=[halves, taps, taps],
        out_shape=[jax.ShapeDtypeStruct((2, s, D_FF), BF16),
                   jax.ShapeDtypeStruct((8, D_FF), F32), jax.ShapeDtypeStruct((8, D_FF), F32)],
        compiler_params=_params("parallel", "arbitrary"),
    )(up, up, up, up, up, up, da, da, cw, cw)
    return d, dwg[:3], dwv[:3]


def _sconv_fwd(proj, ck, name, ts=256, tc=512):
    s = proj.shape[0]
    w = D_MODEL
    ts, tc = _tile(s, ts, 8), _tile(w, tc)
    nc, nt = w // tc, s // ts

    def body(b_ref, c_ref, cp_ref, u_ref, up_ref, w_ref, o_ref):
        first = pl.program_id(1) == 0
        cu = c_ref[...] * u_ref[...]
        cup = jnp.where(first, 0.0, cp_ref[...] * up_ref[...])
        x1, x2 = _shift_down(cu, cup)
        o_ref[...] = (b_ref[...] * _conv(cu, x1, x2, w_ref[...])).astype(BF16)

    b_main, _, _ = _halo_specs(ts, tc, 0, nt)
    c_main, c_prev, _ = _halo_specs(ts, tc, nc, nt)
    u_main, u_prev, _ = _halo_specs(ts, tc, 2 * nc, nt)
    return pl.pallas_call(
        body, name=name, grid=(nc, nt),
        in_specs=[b_main, c_main, c_prev, u_main, u_prev, pl.BlockSpec((3, tc), lambda j, i: (0, j))],
        out_specs=pl.BlockSpec((ts, tc), lambda j, i: (i, j)),
        out_shape=jax.ShapeDtypeStruct((s, w), BF16),
        compiler_params=_params("parallel", "parallel"),
    )(proj, proj, proj, proj, proj, ck)


def _sconv_bwd(proj, ck, dy, name, ts=256, tc=512):
    s = proj.shape[0]
    w = D_MODEL
    ts, tc = _tile(s, ts, 8), _tile(w, tc)
    nc, nt = w // tc, s // ts

    def body(b_ref, bn_ref, c_ref, cp_ref, u_ref, up_ref, dy_ref, dyn_ref, w_ref,
             d_ref, dw_ref):
        i = pl.program_id(1)
        first, last = i == 0, i == nt - 1
        wv = w_ref[...]
        b, c, u, dy_v = b_ref[...], c_ref[...], u_ref[...], dy_ref[...]
        cu = c * u
        cup = jnp.where(first, 0.0, cp_ref[...] * up_ref[...])
        x1, x2 = _shift_down(cu, cup)
        d_ref[0] = (dy_v * _conv(cu, x1, x2, wv)).astype(BF16)
        dcv = dy_v * b
        dcvn = jnp.where(last, 0.0, dyn_ref[...] * bn_ref[...])
        d1, d2 = _shift_up(dcv, dcvn)
        dcu = wv[2:3, :] * dcv + wv[1:2, :] * d1 + wv[0:1, :] * d2
        d_ref[1] = (dcu * u).astype(BF16)
        d_ref[2] = (dcu * c).astype(BF16)

        @pl.when(first)
        def _():
            dw_ref[...] = jnp.zeros_like(dw_ref)

        dw_ref[0:1, :] += jnp.sum(dcv * x2, axis=0, keepdims=True)
        dw_ref[1:2, :] += jnp.sum(dcv * x1, axis=0, keepdims=True)
        dw_ref[2:3, :] += jnp.sum(dcv * cu, axis=0, keepdims=True)

    b_main, _, b_next = _halo_specs(ts, tc, 0, nt)
    c_main, c_prev, _ = _halo_specs(ts, tc, nc, nt)
    u_main, u_prev, _ = _halo_specs(ts, tc, 2 * nc, nt)
    dy_main, _, dy_next = _halo_specs(ts, tc, 0, nt)
    d, dw = pl.pallas_call(
        body, name=name, grid=(nc, nt),
        in_specs=[b_main, b_next, c_main, c_prev, u_main, u_prev, dy_main, dy_next,
                  pl.BlockSpec((3, tc), lambda j, i: (0, j))],
        out_specs=[pl.BlockSpec((3, ts, tc), lambda j, i: (0, i, j)), pl.BlockSpec((8, tc), lambda j, i: (0, j))],
        out_shape=[jax.ShapeDtypeStruct((3, s, w), BF16), jax.ShapeDtypeStruct((8, w), F32)],
        compiler_params=_params("parallel", "arbitrary"),
    )(proj, proj, proj, proj, proj, proj, dy, dy, ck)
    return d, dw[:3]


def _low_lanes(shape):
    return lax.broadcasted_iota(jnp.int32, shape, 1) < HEAD_DIM


def _top_rows(shape):
    return lax.broadcasted_iota(jnp.int32, shape, 0) < HEAD_DIM


def _norm_pair(x):
    r = lax.rsqrt(_mean_pair(x * x) + EPS)
    return x * r, r


def _mean_pair(x):
    same_head = _tri(LANES, lambda a, b: a // HEAD_DIM == b // HEAD_DIM)
    return _split_dot(x, same_head, 3) * (1.0 / HEAD_DIM)


def _qkv_prep(proj, gq, gk, name, ts=512):
    s = proj.shape[0]
    ts = _tile(s, ts)
    npair = MIX // LANES
    scale = HEAD_DIM ** -0.5

    def body(q_ref, k_ref, v_ref, gq_ref, gk_ref, qo_ref, ko_ref, vo_ref, qt_ref, kt_ref, vt_ref):
        qn, _ = _norm_pair(q_ref[...])
        kn, _ = _norm_pair(k_ref[...])
        q = qn * gq_ref[0] * scale
        k = kn * gk_ref[0]
        v = v_ref[...]
        qo_ref[...] = q.astype(BF16)
        ko_ref[...] = k.astype(BF16)
        vo_ref[...] = v.astype(BF16)
        qt_ref[...] = q.T.astype(BF16)
        kt_ref[...] = k.T.astype(BF16)
        vt_ref[...] = v.T.astype(BF16)

    gain = pl.BlockSpec((1, 1, LANES), lambda i, p: (p, 0, 0))
    tile = pl.BlockSpec((ts, LANES), lambda i, p: (i, p))
    tile_t = pl.BlockSpec((LANES, ts), lambda i, p: (p, i))
    out = jax.ShapeDtypeStruct((s, MIX), BF16)
    out_t = jax.ShapeDtypeStruct((MIX, s), BF16)
    return pl.pallas_call(
        body, name=name, grid=(s // ts, npair),
        in_specs=[tile, pl.BlockSpec((ts, LANES), lambda i, p: (i, p + npair)),
                  pl.BlockSpec((ts, LANES), lambda i, p: (i, p + 2 * npair)), gain, gain],
        out_specs=[tile, tile, tile, tile_t, tile_t, tile_t], out_shape=[out, out, out, out_t, out_t, out_t],
        compiler_params=_params("parallel", "parallel"),
    )(proj, proj, proj, gq, gk)


def _qkv_prep_bwd(proj, gq, gk, dqs, dks, dvs, name, ts=512):
    s = proj.shape[0]
    ts = _tile(s, ts, 8)
    npair = MIX // LANES
    half = npair // 2
    scale = HEAD_DIM ** -0.5

    def body(q_ref, k_ref, gq_ref, gk_ref, dqf_ref, dqs_ref, dkf_ref, dks_ref, dvf_ref, dvs_ref,
             dq_ref, dk_ref, dv_ref, dgq_ref, dgk_ref):
        p, i = pl.program_id(0), pl.program_id(1)
        fox = p < half

        def one(x_ref, g_ref, df_ref, ds_ref, dx_ref, dg_ref, mult):
            dn = jnp.where(fox, df_ref[...], ds_ref[...]) * mult
            xh, r = _norm_pair(x_ref[...])

            @pl.when(i == 0)
            def _():
                dg_ref[...] = jnp.zeros_like(dg_ref)

            dg_ref[0, 0:1, :] += jnp.sum(dn * xh, axis=0, keepdims=True)
            dxh = dn * g_ref[0]
            dx_ref[...] = (r * (dxh - xh * _mean_pair(dxh * xh))).astype(BF16)

        one(q_ref, gq_ref, dqf_ref, dqs_ref, dq_ref, dgq_ref, scale)
        one(k_ref, gk_ref, dkf_ref, dks_ref, dk_ref, dgk_ref, 1.0)
        dv_ref[...] = jnp.where(fox, dvf_ref[...], dvs_ref[...]).astype(BF16)

    gain = pl.BlockSpec((1, 1, LANES), lambda p, i: (p, 0, 0))
    tile = pl.BlockSpec((ts, LANES), lambda p, i: (i, p))
    fpart = pl.BlockSpec((ts, LANES), lambda p, i: (i, jnp.minimum(p, half - 1)))
    spart = pl.BlockSpec((ts, LANES), lambda p, i: (i, jnp.maximum(p - half, 0)))
    dgain = pl.BlockSpec((1, 8, LANES), lambda p, i: (p, 0, 0))
    out = jax.ShapeDtypeStruct((s, MIX), BF16)
    gshape = jax.ShapeDtypeStruct((npair, 8, LANES), F32)
    dq, dk, dv, dgq, dgk = pl.pallas_call(
        body, name=name, grid=(npair, s // ts),
        in_specs=[tile, pl.BlockSpec((ts, LANES), lambda p, i: (i, p + npair)), gain, gain,
                  fpart, spart, fpart, spart, fpart, spart],
        out_specs=[tile, tile, tile, dgain, dgain], out_shape=[out, out, out, gshape, gshape],
        compiler_params=_params("parallel", "arbitrary"),
    )(proj, proj, gq, gk, dqs[0], dqs[1], dks[0], dks[1], dvs[0], dvs[1])
    return dq, dk, dv, dgq[:, 0, :], dgk[:, 0, :]


def _tri(n, rel):
    a = lax.broadcasted_iota(jnp.int32, (n, n), 0)
    b = lax.broadcasted_iota(jnp.int32, (n, n), 1)
    return rel(a, b).astype(BF16)


def _fgate_fwd(logit, bias, name):
    nh, r, _ = logit.shape

    def body(x_ref, b_ref, o_ref):
        within = _tri(LANES, lambda a, b: a <= b)
        before = _tri(r, lambda a, b: b < a)
        for hh in range(nh):
            x = x_ref[hh] + b_ref[hh]
            lf = jnp.minimum(x, 0.0) - jnp.log1p(jnp.exp(-jnp.abs(x)))
            c = _split_dot(lf, within, 3)
            tot = jnp.broadcast_to(c[:, LANES - 1:LANES], (r, LANES))
            o_ref[hh] = c + _split_dot_left(before, tot, 3)

    return pl.pallas_call(
        body, name=name, out_shape=jax.ShapeDtypeStruct((nh, r, LANES), F32),
        in_specs=[pl.BlockSpec(memory_space=pltpu.VMEM), pl.BlockSpec(memory_space=pltpu.SMEM)],
        out_specs=pl.BlockSpec(memory_space=pltpu.VMEM),
    )(logit, bias)


def _fgate_bwd(logit, bias, dcum, name):
    nh, r, _ = logit.shape

    def body(x_ref, b_ref, d_ref, dx_ref, db_ref):
        within = _tri(LANES, lambda a, b: a >= b)
        after = _tri(r, lambda a, b: b > a)
        for hh in range(nh):
            x = x_ref[hh] + b_ref[hh]
            d = d_ref[hh]
            c = _split_dot(d, within, 3)
            tot = jnp.broadcast_to(c[:, 0:1], (r, LANES))
            dlf = c + _split_dot_left(after, tot, 3)
            dx = dlf * (1.0 / (1.0 + jnp.exp(x)))
            dx_ref[hh] = dx
            db_ref[hh:hh + 1, :] = jnp.broadcast_to(jnp.sum(dx, keepdims=True).reshape(1, 1), (1, LANES))

    return pl.pallas_call(
        body, name=name,
        out_shape=[jax.ShapeDtypeStruct((nh, r, LANES), F32), jax.ShapeDtypeStruct((nh, LANES), F32)],
        in_specs=[pl.BlockSpec(memory_space=pltpu.VMEM), pl.BlockSpec(memory_space=pltpu.SMEM),
                  pl.BlockSpec(memory_space=pltpu.VMEM)],
        out_specs=[pl.BlockSpec(memory_space=pltpu.VMEM), pl.BlockSpec(memory_space=pltpu.VMEM)],
    )(logit, bias, dcum)


def _pair_masks(x):
    lo = _low_lanes(x.shape)
    zero = jnp.zeros_like(x)
    return jnp.where(lo, x, zero), jnp.where(lo, zero, x)


def _pair_masks_t(x):
    top = _top_rows(x.shape)
    zero = jnp.zeros_like(x)
    return jnp.where(top, x, zero), jnp.where(top, zero, x)


def _stack_heads(x):
    return jnp.concatenate(_pair_masks(x), axis=0)


def _stack_heads_t(x):
    return jnp.concatenate(_pair_masks_t(x), axis=1)


def _pair_colsum_t(x):
    top = _top_rows(x.shape)
    return (jnp.sum(jnp.where(top, x, 0.0), axis=0, keepdims=True),
            jnp.sum(jnp.where(top, 0.0, x), axis=0, keepdims=True))


def _key_query_iotas(t):
    return lax.broadcasted_iota(jnp.int32, (t, t), 0), lax.broadcasted_iota(jnp.int32, (t, t), 1)


def _walk_blocks(i, step, descending, group=2):
    full = i // group
    left = i - full * group

    def run(first, count):
        sign = -1 if descending else 1
        step([(first + sign * n, False) for n in range(count)])

    def leftovers():
        start = (left - 1) if descending else full * group
        sign = -1 if descending else 1
        if group == 4:
            @pl.when(left >= 2)
            def _():
                run(start, 2)

            @pl.when(left % 2 == 1)
            def _():
                run(0 if descending else i - 1, 1)
        else:
            @pl.when(left == 1)
            def _():
                run(start, 1)

    def loop(g, carry):
        run((i - 1 - group * g) if descending else group * g, group)
        return carry

    if descending:
        opens = i >= group - 1
        pl.when(opens)(lambda: step([(i, True)] + [(i - n, False) for n in range(1, group)]))
        pl.when(jnp.logical_not(opens))(lambda: step([(i, True)]))
        after = jnp.where(opens, i + 1 - group, i)
        full = after // group
        left = after - full * group

        def loop_down(g, carry):
            run(after - 1 - group * g, group)
            return carry

        lax.fori_loop(0, full, loop_down, 0)
        leftovers()
        return
    total = i + 1
    groups = total // group
    rest = total - groups * group
    lax.fori_loop(0, jnp.where(rest == 0, groups - 1, groups), loop, 0)

    def closing(count):
        step([(i - count + 1 + n, n == count - 1) for n in range(count)])

    pl.when(rest == 0)(lambda: closing(group))
    if group == 4:
        pl.when(rest == 3)(lambda: run(i - 2, 2))
        pl.when(rest == 2)(lambda: closing(2))
        pl.when((rest == 1) | (rest == 3))(lambda: closing(1))
    else:
        pl.when(rest == 1)(lambda: closing(1))


def _attn_specs(s, tq, pair0):
    q_nat = pl.BlockSpec((tq, LANES), lambda p, i: (i, p + pair0))
    q_t = pl.BlockSpec((LANES, tq), lambda p, i: (p + pair0, i))
    k_nat = pl.BlockSpec((s, LANES), lambda p, i: (0, p + pair0))
    k_t = pl.BlockSpec((LANES, s), lambda p, i: (p + pair0, 0))
    return q_nat, q_t, k_nat, k_t


def _fox_fwd(qt, kh, vt, frow, fcol, name, tq=256):
    s = kh.shape[0]
    tq = _tile(s, tq)
    nq, half = s // tq, MIX // LANES // 2

    def body(qt_ref, k_ref, vt_ref, fr_ref, fc_ref, o_ref, lse_ref, m_s, l_s, acc_s):
        i = pl.program_id(1)
        qt_v = qt_ref[...]
        ft = fr_ref[0]
        key, qry = _key_query_iotas(tq)
        causal = key <= qry
        m_s[...] = jnp.full(m_s.shape, NEG, F32)
        l_s[...] = jnp.zeros_like(l_s)
        acc_s[...] = jnp.zeros_like(acc_s)

        top = _top_rows((LANES, tq))

        def step(blocks):
            rows = [pl.ds(pl.multiple_of(j * tq, tq), tq) for j, _ in blocks]
            zs = [_dot(_stack_heads(k_ref[r, :]), qt_v) for r in rows]
            m_cur, l_cur = [m_s[0], m_s[1]], [l_s[0], l_s[1]]
            acc = acc_s[...]
            for b, (_, masked) in enumerate(blocks):
                fk = fc_ref[0, rows[b], :]
                prs, alphas = [], []
                for hh in range(2):
                    sc = zs[b][hh * tq:(hh + 1) * tq] + (ft[hh:hh + 1, :] - fk[:, hh:hh + 1])
                    if masked:
                        sc = jnp.where(causal, sc, NEG)
                    m_new = jnp.maximum(m_cur[hh], jnp.max(sc, axis=0, keepdims=True))
                    alpha = jnp.exp(m_cur[hh] - m_new)
                    pr = jnp.exp(sc - m_new)
                    l_cur[hh] = alpha * l_cur[hh] + jnp.sum(pr, axis=0, keepdims=True)
                    m_cur[hh] = m_new
                    prs.append(pr.astype(BF16))
                    alphas.append(alpha)
                pv = _dot(_stack_heads_t(vt_ref[:, rows[b]]), jnp.concatenate(prs, axis=0))
                acc = jnp.where(top, alphas[0], alphas[1]) * acc + pv
            acc_s[...] = acc
            for hh in range(2):
                m_s[hh] = m_cur[hh]
                l_s[hh] = l_cur[hh]

        _walk_blocks(i, step, descending=False, group=4)
        o_ref[...] = (acc_s[...] / jnp.where(top, l_s[0], l_s[1])).T
        lse_ref[0, 0:1, :] = m_s[0] + jnp.log(l_s[0])
        lse_ref[0, 1:2, :] = m_s[1] + jnp.log(l_s[1])

    _, q_t, k_nat, k_t = _attn_specs(s, tq, 0)
    qstat = pl.BlockSpec((1, 2, tq), lambda p, i: (p, 0, i))
    return pl.pallas_call(
        body, name=name, grid=(half, nq),
        in_specs=[q_t, k_nat, k_t, qstat, pl.BlockSpec((1, s, 2), lambda p, i: (p, 0, 0))],
        out_specs=[pl.BlockSpec((tq, LANES), lambda p, i: (i, p)), qstat],
        out_shape=[jax.ShapeDtypeStruct((s, MIX // 2), F32), jax.ShapeDtypeStruct((half, 2, s), F32)],
        scratch_shapes=[pltpu.VMEM((2, 1, tq), F32), pltpu.VMEM((2, 1, tq), F32), pltpu.VMEM((LANES, tq), F32)],
        compiler_params=_params("parallel", "arbitrary"),
    )(qt, kh, vt, frow, fcol)


def _fox_bwd(qh, qt, kh, kt, vb, frow, fcol, lse, o, do, name, tq=256):
    s = kh.shape[0]
    tq = _tile(s, tq)
    nq, half = s // tq, MIX // LANES // 2

    def body(q_ref, qt_ref, k_ref, kt_ref, v_ref, fr_ref, fc_ref, lse_ref, o_ref, do_ref,
             dq_ref, dk_ref, dv_ref, dfk_ref, dfq_ref, dq_s, rs_s):
        i = pl.program_id(1)

        @pl.when(i == 0)
        def _():
            dk_ref[...] = jnp.zeros_like(dk_ref)
            dv_ref[...] = jnp.zeros_like(dv_ref)
            dfk_ref[...] = jnp.zeros_like(dfk_ref)

        q2 = _stack_heads(q_ref[...])
        qt_v = qt_ref[...]
        do_v = do_ref[...]
        do2 = _stack_heads(do_v.astype(BF16))
        dot_v = do_v.T.astype(BF16)
        dsum = _pair_colsum_t((do_v * o_ref[...]).T)
        ft, ls = fr_ref[0], lse_ref[0]
        key, qry = _key_query_iotas(tq)
        causal = key <= qry
        lane = lax.broadcasted_iota(jnp.int32, (2 * tq, LANES), 0) // tq
        pick2 = (lax.broadcasted_iota(jnp.int32, (2 * tq, LANES), 1) == lane).astype(BF16)
        q2_pick = jnp.concatenate([q2, pick2], axis=1)
        dq_s[...] = jnp.zeros_like(dq_s)
        rs_s[...] = jnp.zeros_like(rs_s)

        def step(blocks):
            rows = [pl.ds(pl.multiple_of(j * tq, tq), tq) for j, _ in blocks]
            zs = [_dot(_stack_heads(k_ref[r, :]), qt_v) for r in rows]
            dps = [_dot(_stack_heads(v_ref[r, :]), dot_v) for r in rows]
            rs = [rs_s[0], rs_s[1]]
            dq = None
            for b, (_, masked) in enumerate(blocks):
                fk = fc_ref[0, rows[b], :]
                prs, dss = [], []
                for hh in range(2):
                    blk = slice(hh * tq, (hh + 1) * tq)
                    sc = zs[b][blk] + (ft[hh:hh + 1, :] - fk[:, hh:hh + 1])
                    pr = jnp.exp(sc - ls[hh:hh + 1, :])
                    if masked:
                        pr = jnp.where(causal, pr, 0.0)
                    dsb = (pr * (dps[b][blk] - dsum[hh])).astype(BF16)
                    rs[hh] = rs[hh] + jnp.sum(dsb.astype(F32), axis=0, keepdims=True)
                    prs.append(pr.astype(BF16))
                    dss.append(dsb)
                dv_ref[rows[b], :] += _dot(jnp.concatenate(prs, axis=1), do2)
                both = _dot(jnp.concatenate(dss, axis=1), q2_pick)
                dk_ref[rows[b], :] += both[:, :LANES]
                dfk_ref[0, rows[b], :] -= both[:, LANES:]
                term = _dot(_stack_heads_t(kt_ref[:, rows[b]]), jnp.concatenate(dss, axis=0))
                dq = term if dq is None else dq + term
            rs_s[0], rs_s[1] = rs
            dq_s[...] += dq

        _walk_blocks(i, step, descending=False, group=4)
        dq_ref[...] = dq_s[...].T
        dfq_ref[0, 0:1, :] = rs_s[0]
        dfq_ref[0, 1:2, :] = rs_s[1]

    q_nat, q_t, k_nat, k_t = _attn_specs(s, tq, 0)
    qstat = pl.BlockSpec((1, 2, tq), lambda p, i: (p, 0, i))
    otile = pl.BlockSpec((tq, LANES), lambda p, i: (i, p))
    oresident = pl.BlockSpec((s, LANES), lambda p, i: (0, p))
    out = jax.ShapeDtypeStruct((s, MIX // 2), F32)
    return pl.pallas_call(
        body, name=name, grid=(half, nq),
        in_specs=[q_nat, q_t, k_nat, k_t, k_nat, qstat, pl.BlockSpec((1, s, 2), lambda p, i: (p, 0, 0)), qstat,
                  otile, q_nat],
        out_specs=[otile, oresident, oresident, pl.BlockSpec((1, s, LANES), lambda p, i: (p, 0, 0)), qstat],
        out_shape=[out, out, out, jax.ShapeDtypeStruct((half, s, LANES), F32),
                   jax.ShapeDtypeStruct((half, 2, s), F32)],
        scratch_shapes=[pltpu.VMEM((LANES, tq), F32), pltpu.VMEM((2, 1, tq), F32)],
        compiler_params=_params("parallel", "arbitrary"),
    )(qh, qt, kh, kt, vb, frow, fcol, lse, o, do)


def _log_sig_pair(z):
    zc = jnp.maximum(z, -80.0)
    lb = -jnp.log(1.0 + jnp.exp(-zc))
    return lb, lb - zc


def _sb_fwd(qt, kh, vt, name, tq=256, gather=()):
    s = kh.shape[0]
    tq = _tile(s, tq)
    nq, half = s // tq, MIX // LANES // 2
    ng = len(gather)

    def body(*refs):
        qt_ref, k_ref, vt_ref = refs[:3]
        o_ref, tot_ref = refs[3 + ng:5 + ng]
        c_s, acc_s = refs[5 + 2 * ng:7 + 2 * ng]
        p, i = pl.program_id(0), pl.program_id(1)
        if ng:
            start, relay, finish = _gather_stages(refs[3:3 + ng], refs[5 + ng:5 + 2 * ng], *refs[7 + 2 * ng:])
            pl.when((p == 0) & (i == 0))(start)
            pl.when((p == half - 1) & (i == 0))(relay)
        qt_v = qt_ref[...]
        key, qry = _key_query_iotas(tq)
        strict = key < qry
        later = _tri(tq, lambda a, b: b > a)
        c_s[...] = jnp.zeros_like(c_s)
        acc_s[...] = jnp.zeros_like(acc_s)

        def step(blocks):
            rows = [pl.ds(pl.multiple_of(j * tq, tq), tq) for j, _ in blocks]
            zs = [_dot(_stack_heads(k_ref[r, :]), qt_v) for r in rows]
            lbs, loms, afters = [], [], []
            for b, (_, masked) in enumerate(blocks):
                for hh in range(2):
                    lb, lom = _log_sig_pair(zs[b][hh * tq:(hh + 1) * tq])
                    if masked:
                        lom = jnp.where(strict, lom, 0.0)
                    lbs.append(lb)
                    loms.append(lom)
                afters.append(_split_dot_left(later, jnp.concatenate(loms[2 * b:2 * b + 2], axis=1), 2))
            carry = [c_s[0], c_s[1]]
            pv = None
            for b, (_, masked) in enumerate(blocks):
                ws = []
                for hh in range(2):
                    n = 2 * b + hh
                    w = jnp.exp(lbs[n] + afters[b][:, hh * tq:(hh + 1) * tq] + carry[hh])
                    if masked:
                        w = jnp.where(strict, w, 0.0)
                    ws.append(w.astype(BF16))
                    carry[hh] = carry[hh] + jnp.sum(loms[n], axis=0, keepdims=True)
                term = _dot(_stack_heads_t(vt_ref[:, rows[b]]), jnp.concatenate(ws, axis=0))
                pv = term if pv is None else pv + term
            c_s[0], c_s[1] = carry
            acc_s[...] += pv

        _walk_blocks(i, step, descending=True, group=4)
        o_ref[...] = acc_s[...].T
        tot_ref[0, 0:1, :] = c_s[0]
        tot_ref[0, 1:2, :] = c_s[1]
        if ng:
            pl.when((p == half - 1) & (i == nq - 1))(finish)

    _, q_t, k_nat, k_t = _attn_specs(s, tq, half)
    qstat = pl.BlockSpec((1, 2, tq), lambda p, i: (p, 0, i))
    return pl.pallas_call(
        body, name=name, grid=(half, nq),
        in_specs=[q_t, k_nat, k_t] + [ANY] * ng,
        out_specs=[pl.BlockSpec((tq, LANES), lambda p, i: (i, p)), qstat] + [ANY] * ng,
        out_shape=[jax.ShapeDtypeStruct((s, MIX // 2), F32), jax.ShapeDtypeStruct((half, 2, s), F32)]
        + [jax.ShapeDtypeStruct((N_DEV,) + x.shape, x.dtype) for x in gather],
        scratch_shapes=[pltpu.VMEM((2, 1, tq), F32), pltpu.VMEM((LANES, tq), F32)]
        + (_gather_scratch(ng) if ng else []),
        compiler_params=_params("arbitrary", "arbitrary") if ng else _params("parallel", "arbitrary"),
    )(qt, kh, vt, *gather)


def _sb_bwd(qh, qt, kh, kt, vb, tot, do, name, tq=256, exchange=()):
    s = kh.shape[0]
    tq = _tile(s, tq)
    nq, half = s // tq, MIX // LANES // 2
    nx = len(exchange)

    def body(*refs):
        q_ref, qt_ref, k_ref, kt_ref, v_ref, tot_ref, do_ref = refs[:7]
        dq_ref, dk_ref, dv_ref = refs[7 + nx:10 + nx]
        rem_s, pg_s, dq_s = refs[10 + 2 * nx:13 + 2 * nx]
        p, i = pl.program_id(0), pl.program_id(1)
        if nx:
            start, finish = _chip_exchange_stages(refs[7:7 + nx], refs[10 + nx:10 + 2 * nx], *refs[13 + 2 * nx:])
            pl.when((p == 0) & (i == 0))(start)

        @pl.when(i == 0)
        def _():
            dk_ref[...] = jnp.zeros_like(dk_ref)
            dv_ref[...] = jnp.zeros_like(dv_ref)

        q2 = _stack_heads(q_ref[...])
        qt_v = qt_ref[...]
        do_v = do_ref[...]
        do2 = _stack_heads(do_v.astype(BF16))
        dot_v = do_v.T.astype(BF16)
        key, qry = _key_query_iotas(tq)
        strict = key < qry
        upto = _tri(tq, lambda a, b: b <= a)
        before = _tri(tq, lambda a, b: b < a)
        tv = tot_ref[0]
        rem_s[0] = tv[0:1, :]
        rem_s[1] = tv[1:2, :]
        pg_s[...] = jnp.zeros_like(pg_s)
        dq_s[...] = jnp.zeros_like(dq_s)

        def step(blocks):
            nb = len(blocks)
            rows = [pl.ds(pl.multiple_of(j * tq, tq), tq) for j, _ in blocks]
            zs = [_dot(_stack_heads(k_ref[r, :]), qt_v) for r in rows]
            dws = [_dot(_stack_heads(v_ref[r, :]), dot_v) for r in rows]
            lbs, loms, prefixes = [], [], []
            for b, (_, masked) in enumerate(blocks):
                for hh in range(2):
                    lb, lom = _log_sig_pair(zs[b][hh * tq:(hh + 1) * tq])
                    if masked:
                        lom = jnp.where(strict, lom, 0.0)
                    lbs.append(lb)
                    loms.append(lom)
                prefixes.append(_split_dot_left(upto, jnp.concatenate(loms[2 * b:2 * b + 2], axis=1), 2))
            rem = [rem_s[0], rem_s[1]]
            ws, gs, gpres = [], [], []
            for b, (_, masked) in enumerate(blocks):
                for hh in range(2):
                    n = 2 * b + hh
                    blk = slice(hh * tq, (hh + 1) * tq)
                    w = jnp.exp(lbs[n] + (rem[hh] - prefixes[b][:, blk]))
                    if masked:
                        w = jnp.where(strict, w, 0.0)
                    gs.append(dws[b][blk] * w)
                    ws.append(w.astype(BF16))
                    rem[hh] = rem[hh] - jnp.sum(loms[n], axis=0, keepdims=True)
                gpres.append(_dot(before, jnp.concatenate(gs[2 * b:2 * b + 2], axis=1).astype(BF16)))
                dv_ref[rows[b], :] += _dot(jnp.concatenate(ws[2 * b:2 * b + 2], axis=1), do2)
            rem_s[0], rem_s[1] = rem
            pg = [pg_s[0], pg_s[1]]
            dq = None
            for b, (_, masked) in enumerate(blocks):
                dzs = []
                for hh in range(2):
                    n = 2 * b + hh
                    g = gs[n]
                    dz = g - jnp.exp(lbs[n]) * (g + (pg[hh] + gpres[b][:, hh * tq:(hh + 1) * tq]))
                    if masked:
                        dz = jnp.where(strict, dz, 0.0)
                    dzs.append(dz.astype(BF16))
                    pg[hh] = pg[hh] + jnp.sum(g, axis=0, keepdims=True)
                dk_ref[rows[b], :] += _dot(jnp.concatenate(dzs, axis=1), q2)
                term = _dot(_stack_heads_t(kt_ref[:, rows[b]]), jnp.concatenate(dzs, axis=0))
                dq = term if dq is None else dq + term
            pg_s[0], pg_s[1] = pg
            dq_s[...] += dq

        _walk_blocks(i, step, descending=False, group=2)
        dq_ref[...] = dq_s[...].T
        if nx:
            pl.when((p == half - 1) & (i == nq - 1))(finish)

    q_nat, q_t, k_nat, k_t = _attn_specs(s, tq, half)
    qstat = pl.BlockSpec((1, 2, tq), lambda p, i: (p, 0, i))
    otile = pl.BlockSpec((tq, LANES), lambda p, i: (i, p))
    oresident = pl.BlockSpec((s, LANES), lambda p, i: (0, p))
    out = jax.ShapeDtypeStruct((s, MIX // 2), F32)
    return pl.pallas_call(
        body, name=name, grid=(half, nq),
        in_specs=[q_nat, q_t, k_nat, k_t, k_nat, qstat, q_nat] + [ANY] * nx,
        out_specs=[otile, oresident, oresident] + [ANY] * nx,
        out_shape=[out, out, out] + [jax.ShapeDtypeStruct(x.shape, x.dtype) for x in exchange],
        scratch_shapes=[pltpu.VMEM((2, 1, tq), F32), pltpu.VMEM((2, 1, tq), F32), pltpu.VMEM((LANES, tq), F32)]
        + (_chip_exchange_scratch(nx) if nx else []),
        compiler_params=_params("arbitrary", "arbitrary") if nx else _params("parallel", "arbitrary"),
    )(qh, qt, kh, kt, vb, tot, do, *exchange)


def _loss_head(y, target, name, ts=512):
    s, d = y.shape
    ts = _tile(s, ts, 8)
    nt = s // ts

    def body(y_ref, t_ref, dy_ref, l_ref, acc):
        i = pl.program_id(0)
        err = y_ref[...] - t_ref[...]
        dy_ref[...] = err * (1.0 / d)

        @pl.when(i == 0)
        def _():
            acc[...] = jnp.zeros_like(acc)

        acc[...] += jnp.sum(err * err, axis=0, keepdims=True)

        @pl.when(i == nt - 1)
        def _():
            tot = jnp.sum(acc[...], keepdims=True).reshape(1, 1) * (0.5 / d)
            l_ref[...] = jnp.broadcast_to(tot, l_ref.shape)

    row = pl.BlockSpec((ts, d), lambda i: (i, 0))
    dy, l = pl.pallas_call(
        body, name=name, grid=(nt,), in_specs=[row, row],
        out_specs=[row, pl.BlockSpec((8, LANES), lambda i: (0, 0))],
        out_shape=[jax.ShapeDtypeStruct((s, d), F32), jax.ShapeDtypeStruct((8, LANES), F32)],
        scratch_shapes=[pltpu.VMEM((1, d), F32)],
        compiler_params=_params("arbitrary"),
    )(y, target)
    return l[0, 0], dy


def _coords():
    return lax.axis_index("x"), lax.axis_index("y"), lax.axis_index("c")


def _other_chips(xi, yi):
    return [(1 - xi, yi), (xi, 1 - yi), (1 - xi, 1 - yi)]


def _gather_stages(x_refs, out_refs, send_sems, recv_sems, local_sems):
    n = len(x_refs)
    xi, yi, ci = _coords()
    me, sibling = (xi, yi, ci), (xi, yi, 1 - ci)
    chips = _other_chips(xi, yi)

    def slot(a, px, py, pc):
        return out_refs[a].at[4 * px + 2 * py + pc]

    def copy(a, k, block, to, src=None):
        return pltpu.make_async_remote_copy(
            src_ref=slot(a, *block) if src is None else src, dst_ref=slot(a, *block),
            send_sem=send_sems.at[a, k], recv_sem=recv_sems.at[a, k], device_id=to, device_id_type=MESH)

    def own(a):
        return pltpu.make_async_copy(x_refs[a], slot(a, *me), local_sems.at[a])

    def first(a):
        return [copy(a, 0, me, sibling, src=x_refs[a])] + [
            copy(a, 1 + j, me, (*chip, ci), src=x_refs[a]) for j, chip in enumerate(chips)]

    def passed(a, j):
        return copy(a, 4 + j, (*chips[j], ci), sibling)

    def start():
        for a in range(n):
            own(a).start()
        for a in range(n):
            for cp in first(a):
                cp.start()

    def relay():
        for j, chip in enumerate(chips):
            for a in range(n):
                copy(a, 1 + j, (*chip, ci), me).wait_recv()
                passed(a, j).start()

    def finish():
        for a in range(n):
            copy(a, 0, sibling, me).wait_recv()
            for j, chip in enumerate(chips):
                copy(a, 4 + j, (*chip, 1 - ci), me).wait_recv()
        for a in range(n):
            for cp in first(a) + [passed(a, j) for j in range(len(chips))]:
                cp.wait_send()
            own(a).wait()

    return start, relay, finish


def _gather_scratch(n):
    return [pltpu.SemaphoreType.DMA((n, 7)), pltpu.SemaphoreType.DMA((n, 7)), pltpu.SemaphoreType.DMA((n,))]


def _all_gather(xs, name):
    n = len(xs)

    def body(*refs):
        start, relay, finish = _gather_stages(refs[:n], refs[n:2 * n], *refs[2 * n:])
        start()
        relay()
        finish()

    return pl.pallas_call(
        body, name=name, out_shape=[jax.ShapeDtypeStruct((N_DEV,) + x.shape, x.dtype) for x in xs],
        in_specs=[ANY] * n, out_specs=[ANY] * n, scratch_shapes=_gather_scratch(n),
    )(*xs)


def _sibling_exchange(gs, name):
    n = len(gs)

    def body(*refs):
        g_refs, recv_refs = refs[:n], refs[n:2 * n]
        send_sems, recv_sems = refs[2 * n:]
        xi, yi, ci = _coords()
        cps = [pltpu.make_async_remote_copy(
            src_ref=g_refs[a].at[2 * chip + (1 - ci)], dst_ref=recv_refs[a].at[chip],
            send_sem=send_sems.at[a, chip], recv_sem=recv_sems.at[a, chip],
            device_id=(xi, yi, 1 - ci), device_id_type=MESH) for a in range(n) for chip in range(N_CHIP)]
        for cp in cps:
            cp.start()
        for cp in cps:
            cp.wait()

    return pl.pallas_call(
        body, name=name, out_shape=[jax.ShapeDtypeStruct((N_CHIP,) + g.shape[1:], g.dtype) for g in gs],
        in_specs=[ANY] * n, out_specs=[ANY] * n,
        scratch_shapes=[pltpu.SemaphoreType.DMA((n, N_CHIP)), pltpu.SemaphoreType.DMA((n, N_CHIP))],
    )(*gs)


def _pair_add(g, recv, ids, name, tr=256):
    _, r, c = g.shape
    tr = _tile(r, tr, 16)

    def body(ids_ref, g_ref, r_ref, p_ref, own_ref):
        kk = pl.program_id(1)
        tot = g_ref[0].astype(F32) + r_ref[0].astype(F32)
        p_ref[0] = tot.astype(BF16)

        @pl.when(kk == ids_ref[1])
        def _():
            own_ref[...] = tot

    grid_spec = pltpu.PrefetchScalarGridSpec(
        num_scalar_prefetch=1, grid=(r // tr, N_CHIP),
        in_specs=[pl.BlockSpec((1, tr, c), lambda i, kk, ids: (2 * kk + ids[0], i, 0)),
                  pl.BlockSpec((1, tr, c), lambda i, kk, ids: (kk, i, 0))],
        out_specs=[pl.BlockSpec((1, tr, c), lambda i, kk, ids: (kk, i, 0)),
                   pl.BlockSpec((tr, c), lambda i, kk, ids: (i, 0))])
    return pl.pallas_call(
        body, name=name, grid_spec=grid_spec,
        out_shape=[jax.ShapeDtypeStruct((N_CHIP, r, c), BF16), jax.ShapeDtypeStruct((r, c), F32)],
        compiler_params=_params("parallel", "arbitrary"),
    )(ids, g, recv)


def _chip_exchange_stages(p_refs, recv_refs, send_sems, recv_sems):
    n = len(p_refs)
    xi, yi, ci = _coords()
    mine = 2 * xi + yi
    chips = _other_chips(xi, yi)

    def copy(a, k, cx, cy):
        return pltpu.make_async_remote_copy(
            src_ref=p_refs[a].at[2 * cx + cy], dst_ref=recv_refs[a].at[mine],
            send_sem=send_sems.at[a, k], recv_sem=recv_sems.at[a, k],
            device_id=(cx, cy, ci), device_id_type=MESH)

    def landed(a, k, cx, cy):
        return pltpu.make_async_remote_copy(
            src_ref=p_refs[a].at[mine], dst_ref=recv_refs[a].at[2 * cx + cy],
            send_sem=send_sems.at[a, k], recv_sem=recv_sems.at[a, k],
            device_id=(cx, cy, ci), device_id_type=MESH)

    def start():
        for a in range(n):
            for k, (cx, cy) in enumerate(chips):
                copy(a, k, cx, cy).start()

    def finish():
        for a in range(n):
            for k, (cx, cy) in enumerate(chips):
                landed(a, k, cx, cy).wait_recv()
        for a in range(n):
            for k, (cx, cy) in enumerate(chips):
                copy(a, k, cx, cy).wait_send()

    return start, finish


def _chip_exchange_scratch(n):
    return [pltpu.SemaphoreType.DMA((n, 3)), pltpu.SemaphoreType.DMA((n, 3))]


def _chip_exchange(ps, name):
    n = len(ps)

    def body(*refs):
        start, finish = _chip_exchange_stages(refs[:n], refs[n:2 * n], *refs[2 * n:])
        start()
        finish()

    return pl.pallas_call(
        body, name=name, out_shape=[jax.ShapeDtypeStruct(p.shape, p.dtype) for p in ps],
        in_specs=[ANY] * n, out_specs=[ANY] * n, scratch_shapes=_chip_exchange_scratch(n),
    )(*ps)


def _adamw_math(w, g, m, v):
    m = ADAM_B1 * m + (1.0 - ADAM_B1) * g
    v = ADAM_B2 * v + (1.0 - ADAM_B2) * (g * g)
    m_hat = m / (1.0 - ADAM_B1 ** ADAM_STEP)
    v_hat = v / (1.0 - ADAM_B2 ** ADAM_STEP)
    delta = -ADAM_LR * (m_hat / (jnp.sqrt(v_hat) + ADAM_EPS) + ADAM_WD * w)
    return delta, m, v


def _adamw_reduce(own, recv, ids, w, m, v, name, tr=256):
    r, c = w.shape
    tr = _tile(r, tr, 16)

    def body(ids_ref, own_ref, recv_ref, w_ref, m_ref, v_ref, g_ref, d_ref, mo_ref, vo_ref):
        mine = ids_ref[1]
        g = None
        for kk in range(N_CHIP):
            term = jnp.where(mine == kk, own_ref[...], recv_ref[kk].astype(F32))
            g = term if g is None else g + term
        delta, m_new, v_new = _adamw_math(w_ref[...], g, m_ref[...], v_ref[...])
        g_ref[...] = g
        d_ref[...] = delta
        mo_ref[...] = m_new
        vo_ref[...] = v_new

    row = pl.BlockSpec((tr, c), lambda i, ids: (i, 0))
    grid_spec = pltpu.PrefetchScalarGridSpec(
        num_scalar_prefetch=1, grid=(r // tr,),
        in_specs=[row, pl.BlockSpec((N_CHIP, tr, c), lambda i, ids: (0, i, 0)), row, row, row],
        out_specs=[row, row, row, row])
    out = jax.ShapeDtypeStruct((r, c), F32)
    return pl.pallas_call(
        body, name=name, grid_spec=grid_spec, out_shape=[out, out, out, out],
        compiler_params=_params("parallel"),
    )(ids, own, recv, w, m, v)


def _sum_sources(a, name):
    n, r, c = a.shape

    def body(a_ref, o_ref):
        tot = a_ref[0]
        for kk in range(1, n):
            tot = tot + a_ref[kk]
        o_ref[...] = tot

    return pl.pallas_call(
        body, name=name, out_shape=jax.ShapeDtypeStruct((r, c), F32),
        in_specs=[pl.BlockSpec(memory_space=pltpu.VMEM)], out_specs=pl.BlockSpec(memory_space=pltpu.VMEM),
    )(a)


def _adamw_small(w, g, m, v, name):
    def body(w_ref, g_ref, m_ref, v_ref, d_ref, mo_ref, vo_ref):
        delta, m_new, v_new = _adamw_math(w_ref[...], g_ref[...], m_ref[...], v_ref[...])
        d_ref[...] = delta
        mo_ref[...] = m_new
        vo_ref[...] = v_new

    vm = pl.BlockSpec(memory_space=pltpu.VMEM)
    out = jax.ShapeDtypeStruct(w.shape, F32)
    return pl.pallas_call(body, name=name, out_shape=[out, out, out], in_specs=[vm] * 4, out_specs=[vm] * 3)(w, g, m, v)


def _pack(parts, width, row_mult):
    flat = jnp.concatenate([p.reshape(-1) for p in parts])
    rows = -(-flat.shape[0] // width)
    rows = -(-rows // row_mult) * row_mult
    return jnp.pad(flat, (0, rows * width - flat.shape[0])).reshape(rows, width)


def _unpack(flat, shapes):
    out, off = [], 0
    lead = flat.shape[:-1]
    for shp in shapes:
        n = 1
        for dd in shp:
            n *= dd
        out.append(flat[..., off:off + n].reshape(lead + tuple(shp)))
        off += n
    return out


def _rows2d(w):
    return w.reshape(w.shape[0] * w.shape[1], w.shape[2])


def _cols_to_dev(g):
    l, k, n = g.shape
    return g.reshape(l * k, N_DEV, n // N_DEV).transpose(1, 0, 2)


def _rows_to_dev(g):
    l, k, n = g.shape
    rs = k // N_DEV
    return g.reshape(l, N_DEV, rs, n).transpose(1, 0, 2, 3).reshape(N_DEV, l * rs, n)


def _dev_to_cols(a, l):
    _, lk, cs = a.shape
    return a.transpose(1, 0, 2).reshape(l, lk // l, N_DEV * cs)


def _dev_to_rows(a, l):
    _, lr, n = a.shape
    rs = lr // l
    return a.reshape(N_DEV, l, rs, n).transpose(1, 0, 2, 3).reshape(l, N_DEV * rs, n)


def kernel(x, attn_norm, attn_w_in, attn_f_bias, fox_q_gain, fox_k_gain, sb_q_gain, sb_k_gain, attn_w_out, conv_norm, conv_w_in, conv_kernel, conv_w_out, ffn_norm, ffn_w_up, ffn_conv, ffn_w_down, loss_target, m_attn_norm, m_attn_w_in, m_attn_f_bias, m_fox_q_gain, m_fox_k_gain, m_sb_q_gain, m_sb_k_gain, m_attn_w_out, m_conv_norm, m_conv_w_in, m_conv_kernel, m_conv_w_out, m_ffn_norm, m_ffn_w_up, m_ffn_conv, m_ffn_w_down, v_attn_norm, v_attn_w_in, v_attn_f_bias, v_fox_q_gain, v_fox_k_gain, v_sb_q_gain, v_sb_k_gain, v_attn_w_out, v_conv_norm, v_conv_w_in, v_conv_kernel, v_conv_w_out, v_ffn_norm, v_ffn_w_up, v_ffn_conv, v_ffn_w_down):
    s = x.shape[1]
    n_attn, n_conv, depth = attn_w_in.shape[0], conv_w_in.shape[0], ffn_w_up.shape[0]
    xi, yi, ci = _coords()
    me = 4 * xi + 2 * yi + ci
    ids = jnp.stack([ci, 2 * xi + yi]).astype(jnp.int32)

    big = [attn_w_in, attn_w_out, conv_w_in, conv_w_out, ffn_w_up, ffn_w_down]
    big_m = [m_attn_w_in, m_attn_w_out, m_conv_w_in, m_conv_w_out, m_ffn_w_up, m_ffn_w_down]
    big_v = [v_attn_w_in, v_attn_w_out, v_conv_w_in, v_conv_w_out, v_ffn_w_up, v_ffn_w_down]
    big_names = ["attn_w_in", "attn_w_out", "conv_w_in", "conv_w_out", "ffn_w_up", "ffn_w_down"]
    small_sh = [conv_norm, conv_kernel, ffn_conv]
    small_sh_shapes = [w.shape for w in small_sh]
    rep = [attn_norm, attn_f_bias, fox_q_gain, fox_k_gain, sb_q_gain, sb_k_gain, ffn_norm]
    rep_shapes = [w.shape for w in rep]

    small_pack = _pack(small_sh, LANES, 8)
    shards_bf16 = [_rows2d(w).astype(BF16) for w in big]
    early = _all_gather([attn_w_in[0].astype(BF16), small_pack], "gather_first")
    first_w_in = jnp.pad(_dev_to_cols(early[0], 1)[0], ((0, 0), (0, ATTN_IN_PAD - ATTN_IN)))
    cn, ckern, fconv = _unpack(early[1].reshape(N_DEV, -1), small_sh_shapes)
    conv_norm_f = cn.transpose(1, 0, 2).reshape(n_conv, D_MODEL)
    conv_kernel_f = ckern.transpose(1, 2, 0, 3).reshape(n_conv, 3, D_MODEL)
    ffn_conv_f = fconv.transpose(1, 2, 0, 3).reshape(depth, 3, 2 * D_FF)

    def pair_gain(fox_g, sb_g):
        f2 = jnp.concatenate([fox_g, fox_g])
        s2 = jnp.concatenate([sb_g, sb_g])
        return jnp.concatenate([jnp.tile(f2[None], (4, 1)), jnp.tile(s2[None], (4, 1))])[:, None, :]

    h = x[0]
    saved = []
    for layer in range(depth):
        i = layer // 2
        tag = "l%d" % layer
        rec = {"h_in": h}
        if layer % 2 == 0:
            xn = xn_next if layer else _rms_fwd(h, attn_norm[i], tag + "_attn_rms")
            proj = _matmul(xn, first_w_in if layer == 0 else a_w_in[i], tag + "_attn_in", tn=640)
            gq, gk = pair_gain(fox_q_gain[i], sb_q_gain[i]), pair_gain(fox_k_gain[i], sb_k_gain[i])
            qh, kh, vb, qt, kt, vt = _qkv_prep(proj, gq, gk, tag + "_qkv_prep")
            logit = proj[:, 3 * MIX:3 * MIX + H_FOX].T.reshape(H_FOX, s // LANES, LANES)
            cum = _fgate_fwd(logit, attn_f_bias[i], tag + "_fgate")
            frow = cum.reshape(H_FOX // 2, 2, s)
            fcol = frow.transpose(0, 2, 1)
            o_fox, lse = _fox_fwd(qt, kh, vt, frow, fcol, tag + "_fox_fwd")
            if layer == 0:
                o_sb, tot, *gathered = _sb_fwd(qt, kh, vt, tag + "_sb_fwd", gather=shards_bf16)
                a_w_in = _dev_to_cols(gathered[0], n_attn)
                a_w_in = jnp.pad(a_w_in, ((0, 0), (0, 0), (0, ATTN_IN_PAD - ATTN_IN)))
                a_w_out = _dev_to_rows(gathered[1], n_attn)
                c_w_in = _dev_to_cols(gathered[2], n_conv)
                c_w_out = _dev_to_rows(gathered[3], n_conv)
                f_w_up = _dev_to_cols(gathered[4], depth)
                f_w_down = _dev_to_rows(gathered[5], depth)
            else:
                o_sb, tot = _sb_fwd(qt, kh, vt, tag + "_sb_fwd")
            o = jnp.concatenate([o_fox, o_sb], axis=1)
            h, xn2 = _matmul_rows(o, a_w_out[i], tag + "_attn_out", add=h, norm_gain=ffn_norm[layer])
            rec.update(xn=xn, proj=proj, gq=gq, gk=gk, qh=qh, kh=kh, vb=vb, qt=qt, kt=kt, logit=logit, frow=frow,
                       fcol=fcol, o_fox=o_fox, lse=lse, tot=tot, o=o)
        else:
            xn = xn_next
            proj = _matmul(xn, c_w_in[i], tag + "_conv_in", tn=1024)
            y = _sconv_fwd(proj, conv_kernel_f[i], tag + "_sconv_fwd")
            h, xn2 = _matmul_rows(y, c_w_out[i], tag + "_conv_out", add=h, norm_gain=ffn_norm[layer])
            rec.update(xn=xn, proj=proj, y=y)
        rec["h_mid"] = h
        up = _matmul(xn2, f_w_up[layer], tag + "_ffn_up", tn=1408)
        act = _ffn_act_fwd(up, ffn_conv_f[layer], tag + "_ffn_act")
        if layer + 1 < depth:
            nxt = layer + 1
            gain = attn_norm[nxt // 2] if nxt % 2 == 0 else conv_norm_f[nxt // 2]
            h, xn_next = _matmul_rows(act, f_w_down[layer], tag + "_ffn_down", add=h, norm_gain=gain, tk=2816)
        else:
            h = _matmul(act, f_w_down[layer], tag + "_ffn_down", add=h, tn=1024, tk=2816)
        rec.update(xn2=xn2, up=up, act=act)
        saved.append(rec)

    loss_local, dh = _loss_head(h, loss_target[0], "loss_head")
    loss = lax.psum(loss_local, ("x", "y", "c"))

    g_attn_norm, g_attn_w_in, g_f_bias = [None] * n_attn, [None] * n_attn, [None] * n_attn
    g_fq, g_fk, g_sq, g_sk, g_attn_w_out = ([None] * n_attn for _ in range(5))
    g_conv_norm, g_conv_w_in, g_conv_kernel, g_conv_w_out = ([None] * n_conv for _ in range(4))
    g_ffn_norm, g_ffn_w_up, g_ffn_conv, g_ffn_w_down = ([None] * depth for _ in range(4))
    everything = slice(0, None)
    early_layers = {nm: (slice(1, None) if nm == "attn_w_in" else everything) for nm in big_names}
    late_layers = {"attn_w_in": slice(0, 1)}

    def slabs_for_devices(layers):
        stacks = {"attn_w_in": (g_attn_w_in, _cols_to_dev), "attn_w_out": (g_attn_w_out, _rows_to_dev),
                  "conv_w_in": (g_conv_w_in, _cols_to_dev), "conv_w_out": (g_conv_w_out, _rows_to_dev),
                  "ffn_w_up": (g_ffn_w_up, _cols_to_dev), "ffn_w_down": (g_ffn_w_down, _rows_to_dev)}
        return [stacks[nm][1](jnp.stack(stacks[nm][0][layers[nm]]).astype(BF16)) for nm in big_names if nm in layers]

    for layer in reversed(range(depth)):
        i = layer // 2
        tag = "l%d" % layer
        rec = saved[layer]
        da = _matmul(dh, f_w_down[layer].T, tag + "_ffn_down_dx", tn=1408)
        g_ffn_w_down[layer] = _matmul_tn(rec["act"], dh, tag + "_ffn_down_dw", tm=1408, tn=1024)
        dup, dwg, dwv = _ffn_act_bwd(rec["up"], ffn_conv_f[layer], da, tag + "_ffn_act_bwd")
        g_ffn_conv[layer] = jnp.concatenate([dwg, dwv], axis=1)
        g_ffn_w_up[layer] = _matmul_tn(rec["xn2"], dup, tag + "_ffn_up_dw", tn=1408)
        dh, g_ffn_norm[layer] = _matmul_rows(dup, f_w_up[layer].T, tag + "_ffn_up_dx", tk=2816,
                                             rms_bwd=(rec["h_mid"], ffn_norm[layer], dh))
        if layer % 2 == 0:
            do = _matmul(dh, a_w_out[i].T, tag + "_attn_out_dx", tn=1024)
            g_attn_w_out[i] = _matmul_tn(rec["o"], dh, tag + "_attn_out_dw", tn=1024)
            if layer == 0:
                early_g = slabs_for_devices(early_layers)
                early_pairs = [_pair_add(g, r, ids, "reduce_pair_add_" + nm)
                               for g, r, nm in zip(early_g, _sibling_exchange(early_g, "reduce_sibling"), big_names)]
            dq_f, dk_f, dv_f, dfk, dfq = _fox_bwd(rec["qh"], rec["qt"], rec["kh"], rec["kt"], rec["vb"], rec["frow"],
                                                  rec["fcol"], rec["lse"], rec["o_fox"], do, tag + "_fox_bwd")
            dq_s, dk_s, dv_s, *from_chips = _sb_bwd(
                rec["qh"], rec["qt"], rec["kh"], rec["kt"], rec["vb"], rec["tot"], do, tag + "_sb_bwd",
                exchange=[pr[0] for pr in early_pairs] if layer == 0 else ())
            if layer == 0:
                early_from_chips = from_chips
            dq, dk, dv, dgq, dgk = _qkv_prep_bwd(rec["proj"], rec["gq"], rec["gk"], (dq_f, dq_s), (dk_f, dk_s),
                                                 (dv_f, dv_s), tag + "_qkv_prep_bwd")
            dcum = (dfq + dfk[:, :, 0:2].transpose(0, 2, 1)).reshape(H_FOX, s // LANES, LANES)
            dlogit, dbias = _fgate_bwd(rec["logit"], attn_f_bias[i], dcum, tag + "_fgate_bwd")
            g_f_bias[i] = dbias[:, 0]
            dgate = jnp.pad(dlogit.reshape(H_FOX, s).T, ((0, 0), (0, LANES - H_FOX))).astype(BF16)
            dproj = jnp.concatenate([dq, dk, dv, dgate], axis=1)

            def fold(dg):
                per_head = dg.reshape(16, HEAD_DIM)
                return jnp.sum(per_head[:8], axis=0), jnp.sum(per_head[8:], axis=0)

            g_fq[i], g_sq[i] = fold(dgq)
            g_fk[i], g_sk[i] = fold(dgk)
            g_attn_w_in[i] = _matmul_tn(rec["xn"], dproj, tag + "_attn_in_dw", tn=640)[:, :ATTN_IN]
            dh, g_attn_norm[i] = _matmul_rows(dproj, a_w_in[i].T, tag + "_attn_in_dx", tk=3200,
                                              rms_bwd=(rec["h_in"], attn_norm[i], dh))
        else:
            dy = _matmul(dh, c_w_out[i].T, tag + "_conv_out_dx", tn=1024)
            g_conv_w_out[i] = _matmul_tn(rec["y"], dh, tag + "_conv_out_dw", tn=1024)
            dproj, g_conv_kernel[i] = _sconv_bwd(rec["proj"], conv_kernel_f[i], dy, tag + "_sconv_bwd")
            g_conv_w_in[i] = _matmul_tn(rec["xn"], dproj, tag + "_conv_in_dw", tn=1024)
            dh, g_conv_norm[i] = _matmul_rows(dproj, c_w_in[i].T, tag + "_conv_in_dx", tm=1024, tk=1024,
                                              rms_bwd=(rec["h_in"], conv_norm_f[i], dh))
    grad_x = dh[None]

    late_names = [nm for nm in big_names if nm in late_layers]
    late_g = slabs_for_devices(late_layers)
    late_pairs = [_pair_add(g, r, ids, "reduce_pair_add_late_" + nm)
                  for g, r, nm in zip(late_g, _sibling_exchange(late_g, "reduce_sibling_late"), late_names)]
    late_from_chips = _chip_exchange([pr[0] for pr in late_pairs], "reduce_chips_late")

    def update(piece_pairs, piece_recv, piece_names, layers):
        outs = {}
        for (own, recv, nm) in zip([pr[1] for pr in piece_pairs], piece_recv, piece_names):
            sl = layers[nm]
            which = big_names.index(nm)
            outs[nm] = _adamw_reduce(own, recv, ids, _rows2d(big[which][sl]), _rows2d(big_m[which][sl]),
                                     _rows2d(big_v[which][sl]), "adamw_%s_%d" % (nm, sl.start))
        return outs

    late_out = update(late_pairs, late_from_chips, late_names, late_layers)
    early_out = update(early_pairs, early_from_chips, big_names, early_layers)
    grads_big, delta_big, newm_big, newv_big = [], [], [], []
    for which, nm in enumerate(big_names):
        shp = big[which].shape
        for k, dest in enumerate((grads_big, delta_big, newm_big, newv_big)):
            parts = ([late_out[nm][k]] if nm in late_out else []) + [early_out[nm][k]]
            dest.append(jnp.concatenate(parts, axis=0).reshape(shp))

    rep_g = [jnp.stack(g_attn_norm), jnp.stack(g_f_bias), jnp.stack(g_fq), jnp.stack(g_fk), jnp.stack(g_sq),
             jnp.stack(g_sk), jnp.stack(g_ffn_norm)]
    sh_g = [jnp.stack(g_conv_norm).reshape(n_conv, N_DEV, -1).transpose(1, 0, 2),
            jnp.stack(g_conv_kernel).reshape(n_conv, 3, N_DEV, -1).transpose(2, 0, 1, 3),
            jnp.stack(g_ffn_conv).reshape(depth, 3, N_DEV, -1).transpose(2, 0, 1, 3)]
    n_rep = sum(int(a.size) for a in rep)
    n_sh = sum(int(a.size) for a in small_sh)
    partial = _pack(rep_g + [jnp.concatenate([a.reshape(N_DEV, -1) for a in sh_g], axis=1)], LANES, 8)
    total = _sum_sources(_all_gather([partial], "gather_small_grads")[0], "sum_small_grads").reshape(-1)
    rep_tot = total[:n_rep]
    sh_tot = lax.dynamic_slice_in_dim(total[n_rep:n_rep + N_DEV * n_sh].reshape(N_DEV, n_sh), me, 1, axis=0)[0]
    g_small = _pack([rep_tot, sh_tot], LANES, 8)

    def small_pack_of(rep_list, sh_list):
        return _pack(rep_list + sh_list, LANES, 8)

    d_small, m_small, v_small = _adamw_small(
        small_pack_of(rep, small_sh), g_small,
        small_pack_of([m_attn_norm, m_attn_f_bias, m_fox_q_gain, m_fox_k_gain, m_sb_q_gain, m_sb_k_gain, m_ffn_norm],
                      [m_conv_norm, m_conv_kernel, m_ffn_conv]),
        small_pack_of([v_attn_norm, v_attn_f_bias, v_fox_q_gain, v_fox_k_gain, v_sb_q_gain, v_sb_k_gain, v_ffn_norm],
                      [v_conv_norm, v_conv_kernel, v_ffn_conv]),
        "adamw_small")
    small_shapes = rep_shapes + small_sh_shapes

    def split_small(a):
        return _unpack(a.reshape(-1), small_shapes)

    def ordered(big_list, small_list):
        an, fb, fq, fk, sq, sk, fn, cno, cke, fco = small_list
        awi, awo, cwi, cwo, fwu, fwd = big_list
        return [an, awi, fb, fq, fk, sq, sk, awo, cno, cwi, cke, cwo, fn, fwu, fco, fwd]

    grads = ordered(grads_big, split_small(g_small))
    deltas = ordered(delta_big, split_small(d_small))
    new_m = ordered(newm_big, split_small(m_small))
    new_v = ordered(newv_big, split_small(v_small))
    return (loss, grad_x, *grads, *deltas, *new_m, *new_v)
```

```python
import jax
import jax.numpy as jnp
from jax import lax
from jax.experimental import pallas as pl
from jax.experimental.pallas import tpu as pltpu

F32 = jnp.float32
BF16 = jnp.bfloat16

D_MODEL = 1024
HEAD_DIM = 64
H_FOX = 8
MIX = 1024
ATTN_IN = 3 * MIX + H_FOX
ATTN_IN_PAD = 3 * MIX + 128
D_FF = 2816
EPS = 1e-6
NEG = -1e30
LANES = 128
N_DEV = 8
N_CHIP = 4

ADAM_LR = 0.001
ADAM_B1 = 0.9
ADAM_B2 = 0.999
ADAM_EPS = 1e-08
ADAM_WD = 0.01
ADAM_STEP = 10

VMEM_LIMIT = 56 * 1024 * 1024
MESH = pl.DeviceIdType.MESH
ANY = pl.BlockSpec(memory_space=pl.ANY)


def _params(*sem):
    return pltpu.CompilerParams(dimension_semantics=sem, vmem_limit_bytes=VMEM_LIMIT)


def _tile(n, target, mult=LANES):
    best = None
    for t in range(mult, min(n, target) + 1, mult):
        if n % t == 0:
            best = t
    return best if best is not None else n


def _dot(a, b):
    return jnp.dot(a, b, preferred_element_type=F32)


def _dot_tn(a, b):
    return lax.dot_general(a, b, (((0,), (0,)), ((), ())), preferred_element_type=F32)


def _split_dot(x, m, passes):
    acc = None
    rem = x
    for _ in range(passes):
        part = rem.astype(BF16)
        term = _dot(part, m)
        acc = term if acc is None else acc + term
        rem = rem - part.astype(F32)
    return acc


def _split_dot_left(m, x, passes):
    acc = None
    rem = x
    for _ in range(passes):
        part = rem.astype(BF16)
        term = _dot(m, part)
        acc = term if acc is None else acc + term
        rem = rem - part.astype(F32)
    return acc


def _matmul(a, b, name, add=None, out_dtype=F32, tm=1024, tn=512, tk=1024):
    split = a.shape[0] if a.ndim == 3 else 1
    m, kh = a.shape[-2:]
    k = split * kh
    n = b.shape[1]
    tm, tn, tk = _tile(m, tm, 8), _tile(n, tn), _tile(kh, tk)
    nk = k // tk
    per_slab = kh // tk
    has_add = add is not None

    def body(*refs):
        a_ref, b_ref = refs[0], refs[1]
        add_ref = refs[2] if has_add else None
        o_ref = refs[2 + has_add]

        def finish(acc):
            if has_add:
                acc = acc + add_ref[...]
            o_ref[...] = acc.astype(out_dtype)

        p = _dot(a_ref[...].astype(BF16), b_ref[...].astype(BF16))
        if nk == 1:
            finish(p)
        else:
            acc_ref = refs[-1]
            kk = pl.program_id(2)

            @pl.when(kk == 0)
            def _():
                acc_ref[...] = p

            @pl.when(kk > 0)
            def _():
                acc_ref[...] += p

            @pl.when(kk == nk - 1)
            def _():
                finish(acc_ref[...])

    if split == 1:
        a_spec = pl.BlockSpec((tm, tk), lambda i, j, kk: (i, kk))
    else:
        a_spec = pl.BlockSpec((None, tm, tk), lambda i, j, kk: (kk // per_slab, i, kk % per_slab))
    in_specs = [a_spec, pl.BlockSpec((tk, tn), lambda i, j, kk: (kk, j))]
    args = [a, b]
    if has_add:
        in_specs.append(pl.BlockSpec((tm, tn), lambda i, j, kk: (i, j)))
        args.append(add)
    return pl.pallas_call(
        body, name=name, grid=(m // tm, n // tn, nk), in_specs=in_specs,
        out_specs=pl.BlockSpec((tm, tn), lambda i, j, kk: (i, j)),
        out_shape=jax.ShapeDtypeStruct((m, n), out_dtype),
        scratch_shapes=[pltpu.VMEM((tm, tn), F32)] if nk > 1 else [],
        compiler_params=_params("parallel", "parallel", "arbitrary"),
    )(*args)


def _matmul_rows(a, b, name, add=None, norm_gain=None, rms_bwd=None, tm=512, tk=1024):
    split = a.shape[0] if a.ndim == 3 else 1
    m, kh = a.shape[-2:]
    n = b.shape[1]
    tm, tk = _tile(m, tm, 8), _tile(kh, tk)
    nk = split * kh // tk
    per_slab = kh // tk
    has_add, has_norm, has_bwd = add is not None, norm_gain is not None, rms_bwd is not None

    def rsqrt_mean_sq(x):
        return lax.rsqrt(jnp.mean(x * x, axis=-1, keepdims=True) + EPS)

    def body(*refs):
        it = iter(refs)
        a_ref, b_ref = next(it), next(it)
        add_ref = next(it) if has_add else None
        gain_ref = next(it) if has_norm else None
        h_ref, g_ref, dhin_ref = (next(it), next(it), next(it)) if has_bwd else (None, None, None)
        o_ref = next(it)
        xn_ref = next(it) if has_norm else None
        dg_ref = next(it) if has_bwd else None
        acc_ref = next(it) if nk > 1 else None
        i = pl.program_id(0)

        def finish(acc):
            if has_add:
                acc = acc + add_ref[...]
            if has_bwd:
                x = h_ref[...]
                r = rsqrt_mean_sq(x)
                xh = x * r

                @pl.when(i == 0)
                def _():
                    dg_ref[...] = jnp.zeros_like(dg_ref)

                dg_ref[0:1, :] += jnp.sum(acc * xh, axis=0, keepdims=True)
                dxh = acc * g_ref[...]
                acc = dhin_ref[...] + r * (dxh - xh * jnp.mean(dxh * xh, axis=-1, keepdims=True))
            o_ref[...] = acc
            if has_norm:
                xn_ref[...] = (acc * rsqrt_mean_sq(acc) * gain_ref[...]).astype(BF16)

        p = _dot(a_ref[...].astype(BF16), b_ref[...].astype(BF16))
        if nk == 1:
            finish(p)
        else:
            kk = pl.program_id(1)

            @pl.when(kk == 0)
            def _():
                acc_ref[...] = p

            @pl.when(kk > 0)
            def _():
                acc_ref[...] += p

            @pl.when(kk == nk - 1)
            def _():
                finish(acc_ref[...])

    if split == 1:
        a_spec = pl.BlockSpec((tm, tk), lambda i, kk: (i, kk))
    else:
        a_spec = pl.BlockSpec((None, tm, tk), lambda i, kk: (kk // per_slab, i, kk % per_slab))
    row = pl.BlockSpec((tm, n), lambda i, kk: (i, 0))
    vec = pl.BlockSpec((1, n), lambda i, kk: (0, 0))
    in_specs, args = [a_spec, pl.BlockSpec((tk, n), lambda i, kk: (kk, 0))], [a, b]
    if has_add:
        in_specs.append(row)
        args.append(add)
    if has_norm:
        in_specs.append(vec)
        args.append(norm_gain.reshape(1, n))
    if has_bwd:
        in_specs += [row, vec, row]
        args += [rms_bwd[0], rms_bwd[1].reshape(1, n), rms_bwd[2]]
    out_specs, out_shape = [row], [jax.ShapeDtypeStruct((m, n), F32)]
    if has_norm:
        out_specs.append(row)
        out_shape.append(jax.ShapeDtypeStruct((m, n), BF16))
    if has_bwd:
        out_specs.append(pl.BlockSpec((8, n), lambda i, kk: (0, 0)))
        out_shape.append(jax.ShapeDtypeStruct((8, n), F32))
    outs = pl.pallas_call(
        body, name=name, grid=(m // tm, nk), in_specs=in_specs, out_specs=out_specs, out_shape=out_shape,
        scratch_shapes=[pltpu.VMEM((tm, n), F32)] if nk > 1 else [],
        compiler_params=_params("arbitrary" if has_bwd else "parallel", "arbitrary"),
    )(*args)
    if has_bwd:
        return outs[0], outs[-1][0]
    return tuple(outs) if has_norm else outs[0]


def _matmul_tn(a, b, name, tm=1024, tn=512, ts=2048):
    s, m = a.shape
    split = b.shape[0] if b.ndim == 3 else 1
    nh = b.shape[-1]
    n = split * nh
    tm, tn, ts = _tile(m, tm), _tile(nh, tn), _tile(s, ts, 8)
    per_slab = nh // tn
    if split == 1:
        b_spec = pl.BlockSpec((ts, tn), lambda i, j, kk: (kk, j))
    else:
        b_spec = pl.BlockSpec((None, ts, tn), lambda i, j, kk: (j // per_slab, kk, j % per_slab))

    def body(a_ref, b_ref, o_ref):
        kk = pl.program_id(2)
        p = _dot_tn(a_ref[...].astype(BF16), b_ref[...].astype(BF16))

        @pl.when(kk == 0)
        def _():
            o_ref[...] = p

        @pl.when(kk > 0)
        def _():
            o_ref[...] += p

    return pl.pallas_call(
        body, name=name, grid=(m // tm, n // tn, s // ts),
        in_specs=[pl.BlockSpec((ts, tm), lambda i, j, kk: (kk, i)), b_spec],
        out_specs=pl.BlockSpec((tm, tn), lambda i, j, kk: (i, j)),
        out_shape=jax.ShapeDtypeStruct((m, n), F32),
        compiler_params=_params("parallel", "parallel", "arbitrary"),
    )(a, b)


def _rms_fwd(h, g, name, ts=512):
    s, d = h.shape
    ts = _tile(s, ts, 8)

    def body(h_ref, g_ref, o_ref):
        x = h_ref[...]
        r = lax.rsqrt(jnp.mean(x * x, axis=-1, keepdims=True) + EPS)
        o_ref[...] = (x * r * g_ref[...]).astype(BF16)

    return pl.pallas_call(
        body, name=name, grid=(s // ts,),
        in_specs=[pl.BlockSpec((ts, d), lambda i: (i, 0)), pl.BlockSpec((1, d), lambda i: (0, 0))],
        out_specs=pl.BlockSpec((ts, d), lambda i: (i, 0)),
        out_shape=jax.ShapeDtypeStruct((s, d), BF16),
        compiler_params=_params("parallel"),
    )(h, g.reshape(1, d))


def _rms_bwd(h, dxn, g, dh_in, name, ts=512):
    s, d = h.shape
    ts = _tile(s, ts, 8)

    def body(h_ref, dxn_ref, g_ref, dhin_ref, dh_ref, dg_ref):
        i = pl.program_id(0)
        x = h_ref[...]
        r = lax.rsqrt(jnp.mean(x * x, axis=-1, keepdims=True) + EPS)
        xh = x * r
        dxn_v = dxn_ref[...]

        @pl.when(i == 0)
        def _():
            dg_ref[...] = jnp.zeros_like(dg_ref)

        dg_ref[0:1, :] += jnp.sum(dxn_v * xh, axis=0, keepdims=True)
        dxh = dxn_v * g_ref[...]
        dx = r * (dxh - xh * jnp.mean(dxh * xh, axis=-1, keepdims=True))
        dh_ref[...] = dhin_ref[...] + dx

    row = pl.BlockSpec((ts, d), lambda i: (i, 0))
    dh, dg = pl.pallas_call(
        body, name=name, grid=(s // ts,),
        in_specs=[row, row, pl.BlockSpec((1, d), lambda i: (0, 0)), row],
        out_specs=[row, pl.BlockSpec((8, d), lambda i: (0, 0))],
        out_shape=[jax.ShapeDtypeStruct((s, d), F32), jax.ShapeDtypeStruct((8, d), F32)],
        compiler_params=_params("arbitrary"),
    )(h, dxn, g.reshape(1, d), dh_in)
    return dh, dg[0]


def _shift_down(x, prev):
    rows = lax.broadcasted_iota(jnp.int32, (8, x.shape[1]), 0)
    p1, p2 = prev[7:8, :], prev[6:7, :]
    r1, r2 = pltpu.roll(x, 1, 0), pltpu.roll(x, 2, 0)
    top1 = jnp.where(rows == 0, p1, r1[0:8, :])
    top2 = jnp.where(rows == 0, p2, jnp.where(rows == 1, p1, r2[0:8, :]))
    if x.shape[0] == 8:
        return top1, top2
    return jnp.concatenate([top1, r1[8:, :]], axis=0), jnp.concatenate([top2, r2[8:, :]], axis=0)


def _shift_up(x, nxt):
    n = x.shape[0]
    rows = lax.broadcasted_iota(jnp.int32, (8, x.shape[1]), 0)
    n0, n1 = nxt[0:1, :], nxt[1:2, :]
    r1, r2 = pltpu.roll(x, n - 1, 0), pltpu.roll(x, n - 2, 0)
    end1 = jnp.where(rows == 7, n0, r1[n - 8:, :])
    end2 = jnp.where(rows == 7, n1, jnp.where(rows == 6, n0, r2[n - 8:, :]))
    return jnp.concatenate([r1[:n - 8, :], end1], axis=0), jnp.concatenate([r2[:n - 8, :], end2], axis=0)


def _conv(x, x1, x2, w):
    return w[2:3, :] * x + w[1:2, :] * x1 + w[0:1, :] * x2


def _halo_specs(ts, tc, col, n_time_blocks):
    r8 = ts // 8
    main = pl.BlockSpec((ts, tc), lambda j, i: (i, j + col))
    prev = pl.BlockSpec((8, tc), lambda j, i: (jnp.maximum(i * r8 - 1, 0), j + col))
    nxt = pl.BlockSpec((8, tc), lambda j, i: (jnp.minimum((i + 1) * r8, n_time_blocks * r8 - 1), j + col))
    return main, prev, nxt


def _silu_parts(g):
    sig = 1.0 / (1.0 + jnp.exp(-g))
    return sig, g * sig


def _ffn_act_fwd(up, cw, name, ts=512, tc=1408):
    s = up.shape[0]
    ts, tc = _tile(s, ts, 8), _tile(D_FF, tc)
    nc, nt = D_FF // tc, s // ts

    def body(g_ref, gp_ref, v_ref, vp_ref, wg_ref, wv_ref, o_ref):
        first = pl.program_id(1) == 0

        def conv(x_ref, p_ref, w_ref):
            x = x_ref[...]
            prev = jnp.where(first, 0.0, p_ref[...])
            x1, x2 = _shift_down(x, prev)
            return _conv(x, x1, x2, w_ref[...])

        ug = conv(g_ref, gp_ref, wg_ref)
        uv = conv(v_ref, vp_ref, wv_ref)
        _, silu = _silu_parts(ug)
        o_ref[...] = (silu * uv).astype(BF16)

    g_main, g_prev, _ = _halo_specs(ts, tc, 0, nt)
    v_main, v_prev, _ = _halo_specs(ts, tc, nc, nt)
    return pl.pallas_call(
        body, name=name, grid=(nc, nt),
        in_specs=[g_main, g_prev, v_main, v_prev,
                  pl.BlockSpec((3, tc), lambda j, i: (0, j)), pl.BlockSpec((3, tc), lambda j, i: (0, j + nc))],
        out_specs=pl.BlockSpec((ts, tc), lambda j, i: (i, j)),
        out_shape=jax.ShapeDtypeStruct((s, D_FF), BF16),
        compiler_params=_params("parallel", "parallel"),
    )(up, up, up, up, cw, cw)


def _ffn_act_bwd(up, cw, da, name, ts=256, tc=1408):
    s = up.shape[0]
    ts, tc = _tile(s, ts, 8), _tile(D_FF, tc)
    nc, nt = D_FF // tc, s // ts

    def body(g_ref, gp_ref, gn_ref, v_ref, vp_ref, vn_ref, da_ref, dan_ref, wg_ref, wv_ref,
             d_ref, dwg_ref, dwv_ref):
        i = pl.program_id(1)
        first, last = i == 0, i == nt - 1
        wg, wv = wg_ref[...], wv_ref[...]
        g, v = g_ref[...], v_ref[...]
        g1, g2 = _shift_down(g, jnp.where(first, 0.0, gp_ref[...]))
        v1, v2 = _shift_down(v, jnp.where(first, 0.0, vp_ref[...]))

        def d_u(ug, uv, da_v):
            sig, silu = _silu_parts(ug)
            return da_v * uv * (sig * (1.0 + ug * (1.0 - sig))), da_v * silu

        dug, duv = d_u(_conv(g, g1, g2, wg), _conv(v, v1, v2, wv), da_ref[...])
        gn, vn = gn_ref[...], vn_ref[...]
        gn1, gn2 = _shift_down(gn, g[ts - 8:, :])
        vn1, vn2 = _shift_down(vn, v[ts - 8:, :])
        dugn, duvn = d_u(_conv(gn, gn1, gn2, wg), _conv(vn, vn1, vn2, wv), dan_ref[...])
        dugn = jnp.where(last, 0.0, dugn)
        duvn = jnp.where(last, 0.0, duvn)

        def finish(du, dun, x, x1, x2, w, dx_ref, dw_ref):
            d1, d2 = _shift_up(du, dun)
            dx_ref[...] = (w[2:3, :] * du + w[1:2, :] * d1 + w[0:1, :] * d2).astype(BF16)

            @pl.when(first)
            def _():
                dw_ref[...] = jnp.zeros_like(dw_ref)

            dw_ref[0:1, :] += jnp.sum(du * x2, axis=0, keepdims=True)
            dw_ref[1:2, :] += jnp.sum(du * x1, axis=0, keepdims=True)
            dw_ref[2:3, :] += jnp.sum(du * x, axis=0, keepdims=True)

        finish(dug, dugn, g, g1, g2, wg, d_ref.at[0], dwg_ref)
        finish(duv, duvn, v, v1, v2, wv, d_ref.at[1], dwv_ref)

    g_specs = _halo_specs(ts, tc, 0, nt)
    v_specs = _halo_specs(ts, tc, nc, nt)
    da_main, _, da_next = _halo_specs(ts, tc, 0, nt)
    taps = pl.BlockSpec((8, tc), lambda j, i: (0, j))
    halves = pl.BlockSpec((2, ts, tc), lambda j, i: (0, i, j))
    d, dwg, dwv = pl.pallas_call(
        body, name=name, grid=(nc, nt),
        in_specs=[*g_specs, *v_specs, da_main, da_next,
                  pl.BlockSpec((3, tc), lambda j, i: (0, j)), pl.BlockSpec((3, tc), lambda j, i: (0, j + nc))],
        out_specs=[halves, taps, taps],
        out_shape=[jax.ShapeDtypeStruct((2, s, D_FF), BF16),
                   jax.ShapeDtypeStruct((8, D_FF), F32), jax.ShapeDtypeStruct((8, D_FF), F32)],
        compiler_params=_params("parallel", "arbitrary"),
    )(up, up, up, up, up, up, da, da, cw, cw)
    return d, dwg[:3], dwv[:3]


def _sconv_fwd(proj, ck, name, ts=256, tc=1024):
    s = proj.shape[0]
    w = D_MODEL
    ts, tc = _tile(s, ts, 8), _tile(w, tc)
    nc, nt = w // tc, s // ts

    def body(b_ref, c_ref, cp_ref, u_ref, up_ref, w_ref, o_ref):
        first = pl.program_id(1) == 0
        cu = c_ref[...] * u_ref[...]
        cup = jnp.where(first, 0.0, cp_ref[...] * up_ref[...])
        x1, x2 = _shift_down(cu, cup)
        o_ref[...] = (b_ref[...] * _conv(cu, x1, x2, w_ref[...])).astype(BF16)

    b_main, _, _ = _halo_specs(ts, tc, 0, nt)
    c_main, c_prev, _ = _halo_specs(ts, tc, nc, nt)
    u_main, u_prev, _ = _halo_specs(ts, tc, 2 * nc, nt)
    return pl.pallas_call(
        body, name=name, grid=(nc, nt),
        in_specs=[b_main, c_main, c_prev, u_main, u_prev, pl.BlockSpec((3, tc), lambda j, i: (0, j))],
        out_specs=pl.BlockSpec((ts, tc), lambda j, i: (i, j)),
        out_shape=jax.ShapeDtypeStruct((s, w), BF16),
        compiler_params=_params("parallel", "parallel"),
    )(proj, proj, proj, proj, proj, ck)


def _sconv_bwd(proj, ck, dy, name, ts=256, tc=1024):
    s = proj.shape[0]
    w = D_MODEL
    ts, tc = _tile(s, ts, 8), _tile(w, tc)
    nc, nt = w // tc, s // ts

    def body(b_ref, bn_ref, c_ref, cp_ref, u_ref, up_ref, dy_ref, dyn_ref, w_ref,
             d_ref, dw_ref):
        i = pl.program_id(1)
        first, last = i == 0, i == nt - 1
        wv = w_ref[...]
        b, c, u, dy_v = b_ref[...], c_ref[...], u_ref[...], dy_ref[...]
        cu = c * u
        cup = jnp.where(first, 0.0, cp_ref[...] * up_ref[...])
        x1, x2 = _shift_down(cu, cup)
        d_ref[0] = (dy_v * _conv(cu, x1, x2, wv)).astype(BF16)
        dcv = dy_v * b
        dcvn = jnp.where(last, 0.0, dyn_ref[...] * bn_ref[...])
        d1, d2 = _shift_up(dcv, dcvn)
        dcu = wv[2:3, :] * dcv + wv[1:2, :] * d1 + wv[0:1, :] * d2
        d_ref[1] = (dcu * u).astype(BF16)
        d_ref[2] = (dcu * c).astype(BF16)

        @pl.when(first)
        def _():
            dw_ref[...] = jnp.zeros_like(dw_ref)

        dw_ref[0:1, :] += jnp.sum(dcv * x2, axis=0, keepdims=True)
        dw_ref[1:2, :] += jnp.sum(dcv * x1, axis=0, keepdims=True)
        dw_ref[2:3, :] += jnp.sum(dcv * cu, axis=0, keepdims=True)

    b_main, _, b_next = _halo_specs(ts, tc, 0, nt)
    c_main, c_prev, _ = _halo_specs(ts, tc, nc, nt)
    u_main, u_prev, _ = _halo_specs(ts, tc, 2 * nc, nt)
    dy_main, _, dy_next = _halo_specs(ts, tc, 0, nt)
    d, dw = pl.pallas_call(
        body, name=name, grid=(nc, nt),
        in_specs=[b_main, b_next, c_main, c_prev, u_main, u_prev, dy_main, dy_next,
                  pl.BlockSpec((3, tc), lambda j, i: (0, j))],
        out_specs=[pl.BlockSpec((3, ts, tc), lambda j, i: (0, i, j)), pl.BlockSpec((8, tc), lambda j, i: (0, j))],
        out_shape=[jax.ShapeDtypeStruct((3, s, w), BF16), jax.ShapeDtypeStruct((8, w), F32)],
        compiler_params=_params("parallel", "arbitrary"),
    )(proj, proj, proj, proj, proj, proj, dy, dy, ck)
    return d, dw[:3]


def _low_lanes(shape):
    return lax.broadcasted_iota(jnp.int32, shape, 1) < HEAD_DIM


def _top_rows(shape):
    return lax.broadcasted_iota(jnp.int32, shape, 0) < HEAD_DIM


def _norm_pair(x):
    r = lax.rsqrt(_mean_pair(x * x) + EPS)
    return x * r, r


def _mean_pair(x):
    same_head = _tri(LANES, lambda a, b: a // HEAD_DIM == b // HEAD_DIM)
    return _split_dot(x, same_head, 3) * (1.0 / HEAD_DIM)


def _qkv_prep(proj, gq, gk, name, ts=512):
    s = proj.shape[0]
    ts = _tile(s, ts)
    npair = MIX // LANES
    scale = HEAD_DIM ** -0.5

    def body(q_ref, k_ref, v_ref, gq_ref, gk_ref, qo_ref, ko_ref, vo_ref, qt_ref, kt_ref, vt_ref):
        qn, _ = _norm_pair(q_ref[...])
        kn, _ = _norm_pair(k_ref[...])
        q = qn * gq_ref[0] * scale
        k = kn * gk_ref[0]
        v = v_ref[...]
        qo_ref[...] = q.astype(BF16)
        ko_ref[...] = k.astype(BF16)
        vo_ref[...] = v.astype(BF16)
        qt_ref[...] = q.T.astype(BF16)
        kt_ref[...] = k.T.astype(BF16)
        vt_ref[...] = v.T.astype(BF16)

    gain = pl.BlockSpec((1, 1, LANES), lambda i, p: (p, 0, 0))
    tile = pl.BlockSpec((ts, LANES), lambda i, p: (i, p))
    tile_t = pl.BlockSpec((LANES, ts), lambda i, p: (p, i))
    out = jax.ShapeDtypeStruct((s, MIX), BF16)
    out_t = jax.ShapeDtypeStruct((MIX, s), BF16)
    return pl.pallas_call(
        body, name=name, grid=(s // ts, npair),
        in_specs=[tile, pl.BlockSpec((ts, LANES), lambda i, p: (i, p + npair)),
                  pl.BlockSpec((ts, LANES), lambda i, p: (i, p + 2 * npair)), gain, gain],
        out_specs=[tile, tile, tile, tile_t, tile_t, tile_t], out_shape=[out, out, out, out_t, out_t, out_t],
        compiler_params=_params("parallel", "parallel"),
    )(proj, proj, proj, gq, gk)


def _qkv_prep_bwd(proj, gq, gk, dqs, dks, dvs, name, ts=512):
    s = proj.shape[0]
    ts = _tile(s, ts, 8)
    npair = MIX // LANES
    half = npair // 2
    scale = HEAD_DIM ** -0.5

    def body(q_ref, k_ref, gq_ref, gk_ref, dqf_ref, dqs_ref, dkf_ref, dks_ref, dvf_ref, dvs_ref,
             dq_ref, dk_ref, dv_ref, dgq_ref, dgk_ref):
        p, i = pl.program_id(0), pl.program_id(1)
        fox = p < half

        def one(x_ref, g_ref, df_ref, ds_ref, dx_ref, dg_ref, mult):
            dn = jnp.where(fox, df_ref[...], ds_ref[...]) * mult
            xh, r = _norm_pair(x_ref[...])

            @pl.when(i == 0)
            def _():
                dg_ref[...] = jnp.zeros_like(dg_ref)

            dg_ref[0, 0:1, :] += jnp.sum(dn * xh, axis=0, keepdims=True)
            dxh = dn * g_ref[0]
            dx_ref[...] = (r * (dxh - xh * _mean_pair(dxh * xh))).astype(BF16)

        one(q_ref, gq_ref, dqf_ref, dqs_ref, dq_ref, dgq_ref, scale)
        one(k_ref, gk_ref, dkf_ref, dks_ref, dk_ref, dgk_ref, 1.0)
        dv_ref[...] = jnp.where(fox, dvf_ref[...], dvs_ref[...]).astype(BF16)

    gain = pl.BlockSpec((1, 1, LANES), lambda p, i: (p, 0, 0))
    tile = pl.BlockSpec((ts, LANES), lambda p, i: (i, p))
    fpart = pl.BlockSpec((ts, LANES), lambda p, i: (i, jnp.minimum(p, half - 1)))
    spart = pl.BlockSpec((ts, LANES), lambda p, i: (i, jnp.maximum(p - half, 0)))
    dgain = pl.BlockSpec((1, 8, LANES), lambda p, i: (p, 0, 0))
    out = jax.ShapeDtypeStruct((s, MIX), BF16)
    gshape = jax.ShapeDtypeStruct((npair, 8, LANES), F32)
    dq, dk, dv, dgq, dgk = pl.pallas_call(
        body, name=name, grid=(npair, s // ts),
        in_specs=[tile, pl.BlockSpec((ts, LANES), lambda p, i: (i, p + npair)), gain, gain,
                  fpart, spart, fpart, spart, fpart, spart],
        out_specs=[tile, tile, tile, dgain, dgain], out_shape=[out, out, out, gshape, gshape],
        compiler_params=_params("parallel", "arbitrary"),
    )(proj, proj, gq, gk, dqs[0], dqs[1], dks[0], dks[1], dvs[0], dvs[1])
    return dq, dk, dv, dgq[:, 0, :], dgk[:, 0, :]


def _tri(n, rel):
    a = lax.broadcasted_iota(jnp.int32, (n, n), 0)
    b = lax.broadcasted_iota(jnp.int32, (n, n), 1)
    return rel(a, b).astype(BF16)


def _fgate_fwd(logit, bias, name):
    nh, r, _ = logit.shape

    def body(x_ref, b_ref, o_ref):
        within = _tri(LANES, lambda a, b: a <= b)
        before = _tri(r, lambda a, b: b < a)
        for hh in range(nh):
            x = x_ref[hh] + b_ref[hh]
            lf = jnp.minimum(x, 0.0) - jnp.log1p(jnp.exp(-jnp.abs(x)))
            c = _split_dot(lf, within, 3)
            tot = jnp.broadcast_to(c[:, LANES - 1:LANES], (r, LANES))
            o_ref[hh] = c + _split_dot_left(before, tot, 3)

    return pl.pallas_call(
        body, name=name, out_shape=jax.ShapeDtypeStruct((nh, r, LANES), F32),
        in_specs=[pl.BlockSpec(memory_space=pltpu.VMEM), pl.BlockSpec(memory_space=pltpu.SMEM)],
        out_specs=pl.BlockSpec(memory_space=pltpu.VMEM),
    )(logit, bias)


def _fgate_bwd(logit, bias, dcum, name):
    nh, r, _ = logit.shape

    def body(x_ref, b_ref, d_ref, dx_ref, db_ref):
        within = _tri(LANES, lambda a, b: a >= b)
        after = _tri(r, lambda a, b: b > a)
        for hh in range(nh):
            x = x_ref[hh] + b_ref[hh]
            d = d_ref[hh]
            c = _split_dot(d, within, 3)
            tot = jnp.broadcast_to(c[:, 0:1], (r, LANES))
            dlf = c + _split_dot_left(after, tot, 3)
            dx = dlf * (1.0 / (1.0 + jnp.exp(x)))
            dx_ref[hh] = dx
            db_ref[hh:hh + 1, :] = jnp.broadcast_to(jnp.sum(dx, keepdims=True).reshape(1, 1), (1, LANES))

    return pl.pallas_call(
        body, name=name,
        out_shape=[jax.ShapeDtypeStruct((nh, r, LANES), F32), jax.ShapeDtypeStruct((nh, LANES), F32)],
        in_specs=[pl.BlockSpec(memory_space=pltpu.VMEM), pl.BlockSpec(memory_space=pltpu.SMEM),
                  pl.BlockSpec(memory_space=pltpu.VMEM)],
        out_specs=[pl.BlockSpec(memory_space=pltpu.VMEM), pl.BlockSpec(memory_space=pltpu.VMEM)],
    )(logit, bias, dcum)


def _pair_masks(x):
    lo = _low_lanes(x.shape)
    zero = jnp.zeros_like(x)
    return jnp.where(lo, x, zero), jnp.where(lo, zero, x)


def _pair_masks_t(x):
    top = _top_rows(x.shape)
    zero = jnp.zeros_like(x)
    return jnp.where(top, x, zero), jnp.where(top, zero, x)


def _stack_heads(x):
    return jnp.concatenate(_pair_masks(x), axis=0)


def _stack_heads_t(x):
    return jnp.concatenate(_pair_masks_t(x), axis=1)


def _pair_colsum_t(x):
    top = _top_rows(x.shape)
    return (jnp.sum(jnp.where(top, x, 0.0), axis=0, keepdims=True),
            jnp.sum(jnp.where(top, 0.0, x), axis=0, keepdims=True))


def _key_query_iotas(t):
    return lax.broadcasted_iota(jnp.int32, (t, t), 0), lax.broadcasted_iota(jnp.int32, (t, t), 1)


def _walk_blocks(i, step, descending, group=2):
    full = i // group
    left = i - full * group

    def run(first, count):
        sign = -1 if descending else 1
        step([(first + sign * n, False) for n in range(count)])

    def leftovers():
        start = (left - 1) if descending else full * group
        sign = -1 if descending else 1
        if group == 4:
            @pl.when(left >= 2)
            def _():
                run(start, 2)

            @pl.when(left % 2 == 1)
            def _():
                run(0 if descending else i - 1, 1)
        else:
            @pl.when(left == 1)
            def _():
                run(start, 1)

    def loop(g, carry):
        run((i - 1 - group * g) if descending else group * g, group)
        return carry

    if descending:
        opens = i >= group - 1
        pl.when(opens)(lambda: step([(i, True)] + [(i - n, False) for n in range(1, group)]))
        pl.when(jnp.logical_not(opens))(lambda: step([(i, True)]))
        after = jnp.where(opens, i + 1 - group, i)
        full = after // group
        left = after - full * group

        def loop_down(g, carry):
            run(after - 1 - group * g, group)
            return carry

        lax.fori_loop(0, full, loop_down, 0)
        leftovers()
        return
    total = i + 1
    groups = total // group
    rest = total - groups * group
    lax.fori_loop(0, jnp.where(rest == 0, groups - 1, groups), loop, 0)

    def closing(count):
        step([(i - count + 1 + n, n == count - 1) for n in range(count)])

    pl.when(rest == 0)(lambda: closing(group))
    if group == 4:
        pl.when(rest == 3)(lambda: run(i - 2, 2))
        pl.when(rest == 2)(lambda: closing(2))
        pl.when((rest == 1) | (rest == 3))(lambda: closing(1))
    else:
        pl.when(rest == 1)(lambda: closing(1))


def _attn_specs(s, tq, pair0):
    q_nat = pl.BlockSpec((tq, LANES), lambda p, i: (i, p + pair0))
    q_t = pl.BlockSpec((LANES, tq), lambda p, i: (p + pair0, i))
    k_nat = pl.BlockSpec((s, LANES), lambda p, i: (0, p + pair0))
    k_t = pl.BlockSpec((LANES, s), lambda p, i: (p + pair0, 0))
    return q_nat, q_t, k_nat, k_t


def _fox_fwd(qt, kh, vt, frow, fcol, name, tq=256):
    s = kh.shape[0]
    tq = _tile(s, tq)
    nq, half = s // tq, MIX // LANES // 2

    def body(qt_ref, k_ref, vt_ref, fr_ref, fc_ref, o_ref, lse_ref, m_s, l_s, acc_s):
        i = pl.program_id(1)
        qt_v = qt_ref[...]
        ft = fr_ref[0]
        key, qry = _key_query_iotas(tq)
        causal = key <= qry
        m_s[...] = jnp.full(m_s.shape, NEG, F32)
        l_s[...] = jnp.zeros_like(l_s)
        acc_s[...] = jnp.zeros_like(acc_s)

        top = _top_rows((LANES, tq))

        def step(blocks):
            rows = [pl.ds(pl.multiple_of(j * tq, tq), tq) for j, _ in blocks]
            zs = [_dot(_stack_heads(k_ref[r, :]), qt_v) for r in rows]
            m_cur, l_cur = [m_s[0], m_s[1]], [l_s[0], l_s[1]]
            acc = acc_s[...]
            for b, (_, masked) in enumerate(blocks):
                fk = fc_ref[0, rows[b], :]
                prs, alphas = [], []
                for hh in range(2):
                    sc = zs[b][hh * tq:(hh + 1) * tq] + (ft[hh:hh + 1, :] - fk[:, hh:hh + 1])
                    if masked:
                        sc = jnp.where(causal, sc, NEG)
                    m_new = jnp.maximum(m_cur[hh], jnp.max(sc, axis=0, keepdims=True))
                    alpha = jnp.exp(m_cur[hh] - m_new)
                    pr = jnp.exp(sc - m_new)
                    l_cur[hh] = alpha * l_cur[hh] + jnp.sum(pr, axis=0, keepdims=True)
                    m_cur[hh] = m_new
                    prs.append(pr.astype(BF16))
                    alphas.append(alpha)
                pv = _dot(_stack_heads_t(vt_ref[:, rows[b]]), jnp.concatenate(prs, axis=0))
                acc = jnp.where(top, alphas[0], alphas[1]) * acc + pv
            acc_s[...] = acc
            for hh in range(2):
                m_s[hh] = m_cur[hh]
                l_s[hh] = l_cur[hh]

        _walk_blocks(i, step, descending=False, group=4)
        o_ref[...] = (acc_s[...] / jnp.where(top, l_s[0], l_s[1])).T
        lse_ref[0, 0:1, :] = m_s[0] + jnp.log(l_s[0])
        lse_ref[0, 1:2, :] = m_s[1] + jnp.log(l_s[1])

    _, q_t, k_nat, k_t = _attn_specs(s, tq, 0)
    qstat = pl.BlockSpec((1, 2, tq), lambda p, i: (p, 0, i))
    return pl.pallas_call(
        body, name=name, grid=(half, nq),
        in_specs=[q_t, k_nat, k_t, qstat, pl.BlockSpec((1, s, 2), lambda p, i: (p, 0, 0))],
        out_specs=[pl.BlockSpec((tq, LANES), lambda p, i: (i, p)), qstat],
        out_shape=[jax.ShapeDtypeStruct((s, MIX // 2), F32), jax.ShapeDtypeStruct((half, 2, s), F32)],
        scratch_shapes=[pltpu.VMEM((2, 1, tq), F32), pltpu.VMEM((2, 1, tq), F32), pltpu.VMEM((LANES, tq), F32)],
        compiler_params=_params("parallel", "arbitrary"),
    )(qt, kh, vt, frow, fcol)


def _fox_bwd(qh, qt, kh, kt, vb, frow, fcol, lse, o, do, name, tq=256):
    s = kh.shape[0]
    tq = _tile(s, tq)
    nq, half = s // tq, MIX // LANES // 2

    def body(q_ref, qt_ref, k_ref, kt_ref, v_ref, fr_ref, fc_ref, lse_ref, o_ref, do_ref,
             dq_ref, dk_ref, dv_ref, dfk_ref, dfq_ref, dq_s, rs_s):
        i = pl.program_id(1)

        @pl.when(i == 0)
        def _():
            dk_ref[...] = jnp.zeros_like(dk_ref)
            dv_ref[...] = jnp.zeros_like(dv_ref)
            dfk_ref[...] = jnp.zeros_like(dfk_ref)

        q2 = _stack_heads(q_ref[...])
        qt_v = qt_ref[...]
        do_v = do_ref[...]
        do2 = _stack_heads(do_v.astype(BF16))
        dot_v = do_v.T.astype(BF16)
        dsum = _pair_colsum_t((do_v * o_ref[...]).T)
        ft, ls = fr_ref[0], lse_ref[0]
        key, qry = _key_query_iotas(tq)
        causal = key <= qry
        lane = lax.broadcasted_iota(jnp.int32, (2 * tq, LANES), 0) // tq
        pick2 = (lax.broadcasted_iota(jnp.int32, (2 * tq, LANES), 1) == lane).astype(BF16)
        q2_pick = jnp.concatenate([q2, pick2], axis=1)
        dq_s[...] = jnp.zeros_like(dq_s)
        rs_s[...] = jnp.zeros_like(rs_s)

        def step(blocks):
            rows = [pl.ds(pl.multiple_of(j * tq, tq), tq) for j, _ in blocks]
            zs = [_dot(_stack_heads(k_ref[r, :]), qt_v) for r in rows]
            dps = [_dot(_stack_heads(v_ref[r, :]), dot_v) for r in rows]
            rs = [rs_s[0], rs_s[1]]
            dq = None
            for b, (_, masked) in enumerate(blocks):
                fk = fc_ref[0, rows[b], :]
                prs, dss = [], []
                for hh in range(2):
                    blk = slice(hh * tq, (hh + 1) * tq)
                    sc = zs[b][blk] + (ft[hh:hh + 1, :] - fk[:, hh:hh + 1])
                    pr = jnp.exp(sc - ls[hh:hh + 1, :])
                    if masked:
                        pr = jnp.where(causal, pr, 0.0)
                    dsb = (pr * (dps[b][blk] - dsum[hh])).astype(BF16)
                    rs[hh] = rs[hh] + jnp.sum(dsb.astype(F32), axis=0, keepdims=True)
                    prs.append(pr.astype(BF16))
                    dss.append(dsb)
                dv_ref[rows[b], :] += _dot(jnp.concatenate(prs, axis=1), do2)
                both = _dot(jnp.concatenate(dss, axis=1), q2_pick)
                dk_ref[rows[b], :] += both[:, :LANES]
                dfk_ref[0, rows[b], :] -= both[:, LANES:]
                term = _dot(_stack_heads_t(kt_ref[:, rows[b]]), jnp.concatenate(dss, axis=0))
                dq = term if dq is None else dq + term
            rs_s[0], rs_s[1] = rs
            dq_s[...] += dq

        _walk_blocks(i, step, descending=False, group=4)
        dq_ref[...] = dq_s[...].T
        dfq_ref[0, 0:1, :] = rs_s[0]
        dfq_ref[0, 1:2, :] = rs_s[1]

    q_nat, q_t, k_nat, k_t = _attn_specs(s, tq, 0)
    qstat = pl.BlockSpec((1, 2, tq), lambda p, i: (p, 0, i))
    otile = pl.BlockSpec((tq, LANES), lambda p, i: (i, p))
    oresident = pl.BlockSpec((s, LANES), lambda p, i: (0, p))
    out = jax.ShapeDtypeStruct((s, MIX // 2), F32)
    return pl.pallas_call(
        body, name=name, grid=(half, nq),
        in_specs=[q_nat, q_t, k_nat, k_t, k_nat, qstat, pl.BlockSpec((1, s, 2), lambda p, i: (p, 0, 0)), qstat,
                  otile, q_nat],
        out_specs=[otile, oresident, oresident, pl.BlockSpec((1, s, LANES), lambda p, i: (p, 0, 0)), qstat],
        out_shape=[out, out, out, jax.ShapeDtypeStruct((half, s, LANES), F32),
                   jax.ShapeDtypeStruct((half, 2, s), F32)],
        scratch_shapes=[pltpu.VMEM((LANES, tq), F32), pltpu.VMEM((2, 1, tq), F32)],
        compiler_params=_params("parallel", "arbitrary"),
    )(qh, qt, kh, kt, vb, frow, fcol, lse, o, do)


def _log_sig_pair(z):
    zc = jnp.maximum(z, -80.0)
    lb = -jnp.log(1.0 + jnp.exp(-zc))
    return lb, lb - zc


def _sb_fwd(qt, kh, vt, name, tq=256, gather=()):
    s = kh.shape[0]
    tq = _tile(s, tq)
    nq, half = s // tq, MIX // LANES // 2
    ng = len(gather)

    def body(*refs):
        qt_ref, k_ref, vt_ref = refs[:3]
        o_ref, tot_ref = refs[3 + ng:5 + ng]
        c_s, acc_s = refs[5 + 2 * ng:7 + 2 * ng]
        p, i = pl.program_id(0), pl.program_id(1)
        if ng:
            start, relay, finish = _gather_stages(refs[3:3 + ng], refs[5 + ng:5 + 2 * ng], *refs[7 + 2 * ng:])
            pl.when((p == 0) & (i == 0))(start)
            pl.when((p == half - 1) & (i == 0))(relay)
        qt_v = qt_ref[...]
        key, qry = _key_query_iotas(tq)
        strict = key < qry
        later = _tri(tq, lambda a, b: b > a)
        c_s[...] = jnp.zeros_like(c_s)
        acc_s[...] = jnp.zeros_like(acc_s)

        def step(blocks):
            rows = [pl.ds(pl.multiple_of(j * tq, tq), tq) for j, _ in blocks]
            zs = [_dot(_stack_heads(k_ref[r, :]), qt_v) for r in rows]
            lbs, loms, afters = [], [], []
            for b, (_, masked) in enumerate(blocks):
                for hh in range(2):
                    lb, lom = _log_sig_pair(zs[b][hh * tq:(hh + 1) * tq])
                    if masked:
                        lom = jnp.where(strict, lom, 0.0)
                    lbs.append(lb)
                    loms.append(lom)
                afters.append(_split_dot_left(later, jnp.concatenate(loms[2 * b:2 * b + 2], axis=1), 2))
            carry = [c_s[0], c_s[1]]
            pv = None
            for b, (_, masked) in enumerate(blocks):
                ws = []
                for hh in range(2):
                    n = 2 * b + hh
                    w = jnp.exp(lbs[n] + afters[b][:, hh * tq:(hh + 1) * tq] + carry[hh])
                    if masked:
                        w = jnp.where(strict, w, 0.0)
                    ws.append(w.astype(BF16))
                    carry[hh] = carry[hh] + jnp.sum(loms[n], axis=0, keepdims=True)
                term = _dot(_stack_heads_t(vt_ref[:, rows[b]]), jnp.concatenate(ws, axis=0))
                pv = term if pv is None else pv + term
            c_s[0], c_s[1] = carry
            acc_s[...] += pv

        _walk_blocks(i, step, descending=True, group=4)
        o_ref[...] = acc_s[...].T
        tot_ref[0, 0:1, :] = c_s[0]
        tot_ref[0, 1:2, :] = c_s[1]
        if ng:
            pl.when((p == half - 1) & (i == nq - 1))(finish)

    _, q_t, k_nat, k_t = _attn_specs(s, tq, half)
    qstat = pl.BlockSpec((1, 2, tq), lambda p, i: (p, 0, i))
    return pl.pallas_call(
        body, name=name, grid=(half, nq),
        in_specs=[q_t, k_nat, k_t] + [ANY] * ng,
        out_specs=[pl.BlockSpec((tq, LANES), lambda p, i: (i, p)), qstat] + [ANY] * ng,
        out_shape=[jax.ShapeDtypeStruct((s, MIX // 2), F32), jax.ShapeDtypeStruct((half, 2, s), F32)]
        + [jax.ShapeDtypeStruct((N_DEV,) + x.shape, x.dtype) for x in gather],
        scratch_shapes=[pltpu.VMEM((2, 1, tq), F32), pltpu.VMEM((LANES, tq), F32)]
        + (_gather_scratch(ng) if ng else []),
        compiler_params=_params("arbitrary", "arbitrary") if ng else _params("parallel", "arbitrary"),
    )(qt, kh, vt, *gather)


def _sb_bwd(qh, qt, kh, kt, vb, tot, do, name, tq=256, exchange=()):
    s = kh.shape[0]
    tq = _tile(s, tq)
    nq, half = s // tq, MIX // LANES // 2
    nx = len(exchange)

    def body(*refs):
        q_ref, qt_ref, k_ref, kt_ref, v_ref, tot_ref, do_ref = refs[:7]
        dq_ref, dk_ref, dv_ref = refs[7 + nx:10 + nx]
        rem_s, pg_s, dq_s = refs[10 + 2 * nx:13 + 2 * nx]
        p, i = pl.program_id(0), pl.program_id(1)
        if nx:
            start, finish = _chip_exchange_stages(refs[7:7 + nx], refs[10 + nx:10 + 2 * nx], *refs[13 + 2 * nx:])
            pl.when((p == 0) & (i == 0))(start)

        @pl.when(i == 0)
        def _():
            dk_ref[...] = jnp.zeros_like(dk_ref)
            dv_ref[...] = jnp.zeros_like(dv_ref)

        q2 = _stack_heads(q_ref[...])
        qt_v = qt_ref[...]
        do_v = do_ref[...]
        do2 = _stack_heads(do_v.astype(BF16))
        dot_v = do_v.T.astype(BF16)
        key, qry = _key_query_iotas(tq)
        strict = key < qry
        upto = _tri(tq, lambda a, b: b <= a)
        before = _tri(tq, lambda a, b: b < a)
        tv = tot_ref[0]
        rem_s[0] = tv[0:1, :]
        rem_s[1] = tv[1:2, :]
        pg_s[...] = jnp.zeros_like(pg_s)
        dq_s[...] = jnp.zeros_like(dq_s)

        def step(blocks):
            nb = len(blocks)
            rows = [pl.ds(pl.multiple_of(j * tq, tq), tq) for j, _ in blocks]
            zs = [_dot(_stack_heads(k_ref[r, :]), qt_v) for r in rows]
            dws = [_dot(_stack_heads(v_ref[r, :]), dot_v) for r in rows]
            lbs, loms, prefixes = [], [], []
            for b, (_, masked) in enumerate(blocks):
                for hh in range(2):
                    lb, lom = _log_sig_pair(zs[b][hh * tq:(hh + 1) * tq])
                    if masked:
                        lom = jnp.where(strict, lom, 0.0)
                    lbs.append(lb)
                    loms.append(lom)
                prefixes.append(_split_dot_left(upto, jnp.concatenate(loms[2 * b:2 * b + 2], axis=1), 2))
            rem = [rem_s[0], rem_s[1]]
            ws, gs, gpres = [], [], []
            for b, (_, masked) in enumerate(blocks):
                for hh in range(2):
                    n = 2 * b + hh
                    blk = slice(hh * tq, (hh + 1) * tq)
                    w = jnp.exp(lbs[n] + (rem[hh] - prefixes[b][:, blk]))
                    if masked:
                        w = jnp.where(strict, w, 0.0)
                    gs.append(dws[b][blk] * w)
                    ws.append(w.astype(BF16))
                    rem[hh] = rem[hh] - jnp.sum(loms[n], axis=0, keepdims=True)
                gpres.append(_dot(before, jnp.concatenate(gs[2 * b:2 * b + 2], axis=1).astype(BF16)))
                dv_ref[rows[b], :] += _dot(jnp.concatenate(ws[2 * b:2 * b + 2], axis=1), do2)
            rem_s[0], rem_s[1] = rem
            pg = [pg_s[0], pg_s[1]]
            dq = None
            for b, (_, masked) in enumerate(blocks):
                dzs = []
                for hh in range(2):
                    n = 2 * b + hh
                    g = gs[n]
                    dz = g - jnp.exp(lbs[n]) * (g + (pg[hh] + gpres[b][:, hh * tq:(hh + 1) * tq]))
                    if masked:
                        dz = jnp.where(strict, dz, 0.0)
                    dzs.append(dz.astype(BF16))
                    pg[hh] = pg[hh] + jnp.sum(g, axis=0, keepdims=True)
                dk_ref[rows[b], :] += _dot(jnp.concatenate(dzs, axis=1), q2)
                term = _dot(_stack_heads_t(kt_ref[:, rows[b]]), jnp.concatenate(dzs, axis=0))
                dq = term if dq is None else dq + term
            pg_s[0], pg_s[1] = pg
            dq_s[...] += dq

        _walk_blocks(i, step, descending=False, group=2)
        dq_ref[...] = dq_s[...].T
        if nx:
            pl.when((p == half - 1) & (i == nq - 1))(finish)

    q_nat, q_t, k_nat, k_t = _attn_specs(s, tq, half)
    qstat = pl.BlockSpec((1, 2, tq), lambda p, i: (p, 0, i))
    otile = pl.BlockSpec((tq, LANES), lambda p, i: (i, p))
    oresident = pl.BlockSpec((s, LANES), lambda p, i: (0, p))
    out = jax.ShapeDtypeStruct((s, MIX // 2), F32)
    return pl.pallas_call(
        body, name=name, grid=(half, nq),
        in_specs=[q_nat, q_t, k_nat, k_t, k_nat, qstat, q_nat] + [ANY] * nx,
        out_specs=[otile, oresident, oresident] + [ANY] * nx,
        out_shape=[out, out, out] + [jax.ShapeDtypeStruct(x.shape, x.dtype) for x in exchange],
        scratch_shapes=[pltpu.VMEM((2, 1, tq), F32), pltpu.VMEM((2, 1, tq), F32), pltpu.VMEM((LANES, tq), F32)]
        + (_chip_exchange_scratch(nx) if nx else []),
        compiler_params=_params("arbitrary", "arbitrary") if nx else _params("parallel", "arbitrary"),
    )(qh, qt, kh, kt, vb, tot, do, *exchange)


def _loss_head(y, target, name, ts=512):
    s, d = y.shape
    ts = _tile(s, ts, 8)
    nt = s // ts

    def body(y_ref, t_ref, dy_ref, l_ref, acc):
        i = pl.program_id(0)
        err = y_ref[...] - t_ref[...]
        dy_ref[...] = err * (1.0 / d)

        @pl.when(i == 0)
        def _():
            acc[...] = jnp.zeros_like(acc)

        acc[...] += jnp.sum(err * err, axis=0, keepdims=True)

        @pl.when(i == nt - 1)
        def _():
            tot = jnp.sum(acc[...], keepdims=True).reshape(1, 1) * (0.5 / d)
            l_ref[...] = jnp.broadcast_to(tot, l_ref.shape)

    row = pl.BlockSpec((ts, d), lambda i: (i, 0))
    dy, l = pl.pallas_call(
        body, name=name, grid=(nt,), in_specs=[row, row],
        out_specs=[row, pl.BlockSpec((8, LANES), lambda i: (0, 0))],
        out_shape=[jax.ShapeDtypeStruct((s, d), F32), jax.ShapeDtypeStruct((8, LANES), F32)],
        scratch_shapes=[pltpu.VMEM((1, d), F32)],
        compiler_params=_params("arbitrary"),
    )(y, target)
    return l[0, 0], dy


def _coords():
    return lax.axis_index("x"), lax.axis_index("y"), lax.axis_index("c")


def _other_chips(xi, yi):
    return [(1 - xi, yi), (xi, 1 - yi), (1 - xi, 1 - yi)]


def _gather_stages(x_refs, out_refs, send_sems, recv_sems, local_sems):
    n = len(x_refs)
    xi, yi, ci = _coords()
    me, sibling = (xi, yi, ci), (xi, yi, 1 - ci)
    chips = _other_chips(xi, yi)

    def slot(a, px, py, pc):
        return out_refs[a].at[4 * px + 2 * py + pc]

    def copy(a, k, block, to, src=None):
        return pltpu.make_async_remote_copy(
            src_ref=slot(a, *block) if src is None else src, dst_ref=slot(a, *block),
            send_sem=send_sems.at[a, k], recv_sem=recv_sems.at[a, k], device_id=to, device_id_type=MESH)

    def own(a):
        return pltpu.make_async_copy(x_refs[a], slot(a, *me), local_sems.at[a])

    def first(a):
        return [copy(a, 0, me, sibling, src=x_refs[a])] + [
            copy(a, 1 + j, me, (*chip, ci), src=x_refs[a]) for j, chip in enumerate(chips)]

    def passed(a, j):
        return copy(a, 4 + j, (*chips[j], ci), sibling)

    def start():
        for a in range(n):
            own(a).start()
        for a in range(n):
            for cp in first(a):
                cp.start()

    def relay():
        for j, chip in enumerate(chips):
            for a in range(n):
                copy(a, 1 + j, (*chip, ci), me).wait_recv()
                passed(a, j).start()

    def finish():
        for a in range(n):
            copy(a, 0, sibling, me).wait_recv()
            for j, chip in enumerate(chips):
                copy(a, 4 + j, (*chip, 1 - ci), me).wait_recv()
        for a in range(n):
            for cp in first(a) + [passed(a, j) for j in range(len(chips))]:
                cp.wait_send()
            own(a).wait()

    return start, relay, finish


def _gather_scratch(n):
    return [pltpu.SemaphoreType.DMA((n, 7)), pltpu.SemaphoreType.DMA((n, 7)), pltpu.SemaphoreType.DMA((n,))]


def _all_gather(xs, name):
    n = len(xs)

    def body(*refs):
        start, relay, finish = _gather_stages(refs[:n], refs[n:2 * n], *refs[2 * n:])
        start()
        relay()
        finish()

    return pl.pallas_call(
        body, name=name, out_shape=[jax.ShapeDtypeStruct((N_DEV,) + x.shape, x.dtype) for x in xs],
        in_specs=[ANY] * n, out_specs=[ANY] * n, scratch_shapes=_gather_scratch(n),
    )(*xs)


def _sibling_exchange(gs, name):
    n = len(gs)

    def body(*refs):
        g_refs, recv_refs = refs[:n], refs[n:2 * n]
        send_sems, recv_sems = refs[2 * n:]
        xi, yi, ci = _coords()
        cps = [pltpu.make_async_remote_copy(
            src_ref=g_refs[a].at[2 * chip + (1 - ci)], dst_ref=recv_refs[a].at[chip],
            send_sem=send_sems.at[a, chip], recv_sem=recv_sems.at[a, chip],
            device_id=(xi, yi, 1 - ci), device_id_type=MESH) for a in range(n) for chip in range(N_CHIP)]
        for cp in cps:
            cp.start()
        for cp in cps:
            cp.wait()

    return pl.pallas_call(
        body, name=name, out_shape=[jax.ShapeDtypeStruct((N_CHIP,) + g.shape[1:], g.dtype) for g in gs],
        in_specs=[ANY] * n, out_specs=[ANY] * n,
        scratch_shapes=[pltpu.SemaphoreType.DMA((n, N_CHIP)), pltpu.SemaphoreType.DMA((n, N_CHIP))],
    )(*gs)


def _pair_add(g, recv, ids, name, tr=256):
    _, r, c = g.shape
    tr = _tile(r, tr, 16)

    def body(ids_ref, g_ref, r_ref, p_ref, own_ref):
        kk = pl.program_id(1)
        tot = g_ref[0].astype(F32) + r_ref[0].astype(F32)
        p_ref[0] = tot.astype(BF16)

        @pl.when(kk == ids_ref[1])
        def _():
            own_ref[...] = tot

    grid_spec = pltpu.PrefetchScalarGridSpec(
        num_scalar_prefetch=1, grid=(r // tr, N_CHIP),
        in_specs=[pl.BlockSpec((1, tr, c), lambda i, kk, ids: (2 * kk + ids[0], i, 0)),
                  pl.BlockSpec((1, tr, c), lambda i, kk, ids: (kk, i, 0))],
        out_specs=[pl.BlockSpec((1, tr, c), lambda i, kk, ids: (kk, i, 0)),
                   pl.BlockSpec((tr, c), lambda i, kk, ids: (i, 0))])
    return pl.pallas_call(
        body, name=name, grid_spec=grid_spec,
        out_shape=[jax.ShapeDtypeStruct((N_CHIP, r, c), BF16), jax.ShapeDtypeStruct((r, c), F32)],
        compiler_params=_params("parallel", "arbitrary"),
    )(ids, g, recv)


def _chip_exchange_stages(p_refs, recv_refs, send_sems, recv_sems):
    n = len(p_refs)
    xi, yi, ci = _coords()
    mine = 2 * xi + yi
    chips = _other_chips(xi, yi)

    def copy(a, k, cx, cy):
        return pltpu.make_async_remote_copy(
            src_ref=p_refs[a].at[2 * cx + cy], dst_ref=recv_refs[a].at[mine],
            send_sem=send_sems.at[a, k], recv_sem=recv_sems.at[a, k],
            device_id=(cx, cy, ci), device_id_type=MESH)

    def landed(a, k, cx, cy):
        return pltpu.make_async_remote_copy(
            src_ref=p_refs[a].at[mine], dst_ref=recv_refs[a].at[2 * cx + cy],
            send_sem=send_sems.at[a, k], recv_sem=recv_sems.at[a, k],
            device_id=(cx, cy, ci), device_id_type=MESH)

    def start():
        for a in range(n):
            for k, (cx, cy) in enumerate(chips):
                copy(a, k, cx, cy).start()

    def finish():
        for a in range(n):
            for k, (cx, cy) in enumerate(chips):
                landed(a, k, cx, cy).wait_recv()
        for a in range(n):
            for k, (cx, cy) in enumerate(chips):
                copy(a, k, cx, cy).wait_send()

    return start, finish


def _chip_exchange_scratch(n):
    return [pltpu.SemaphoreType.DMA((n, 3)), pltpu.SemaphoreType.DMA((n, 3))]


def _chip_exchange(ps, name):
    n = len(ps)

    def body(*refs):
        start, finish = _chip_exchange_stages(refs[:n], refs[n:2 * n], *refs[2 * n:])
        start()
        finish()

    return pl.pallas_call(
        body, name=name, out_shape=[jax.ShapeDtypeStruct(p.shape, p.dtype) for p in ps],
        in_specs=[ANY] * n, out_specs=[ANY] * n, scratch_shapes=_chip_exchange_scratch(n),
    )(*ps)


def _adamw_math(w, g, m, v):
    m = ADAM_B1 * m + (1.0 - ADAM_B1) * g
    v = ADAM_B2 * v + (1.0 - ADAM_B2) * (g * g)
    m_hat = m / (1.0 - ADAM_B1 ** ADAM_STEP)
    v_hat = v / (1.0 - ADAM_B2 ** ADAM_STEP)
    delta = -ADAM_LR * (m_hat / (jnp.sqrt(v_hat) + ADAM_EPS) + ADAM_WD * w)
    return delta, m, v


def _adamw_reduce(own, recv, ids, w, m, v, name, tr=256):
    r, c = w.shape
    tr = _tile(r, tr, 16)

    def body(ids_ref, own_ref, recv_ref, w_ref, m_ref, v_ref, g_ref, d_ref, mo_ref, vo_ref):
        mine = ids_ref[1]
        g = None
        for kk in range(N_CHIP):
            term = jnp.where(mine == kk, own_ref[...], recv_ref[kk].astype(F32))
            g = term if g is None else g + term
        delta, m_new, v_new = _adamw_math(w_ref[...], g, m_ref[...], v_ref[...])
        g_ref[...] = g
        d_ref[...] = delta
        mo_ref[...] = m_new
        vo_ref[...] = v_new

    row = pl.BlockSpec((tr, c), lambda i, ids: (i, 0))
    grid_spec = pltpu.PrefetchScalarGridSpec(
        num_scalar_prefetch=1, grid=(r // tr,),
        in_specs=[row, pl.BlockSpec((N_CHIP, tr, c), lambda i, ids: (0, i, 0)), row, row, row],
        out_specs=[row, row, row, row])
    out = jax.ShapeDtypeStruct((r, c), F32)
    return pl.pallas_call(
        body, name=name, grid_spec=grid_spec, out_shape=[out, out, out, out],
        compiler_params=_params("parallel"),
    )(ids, own, recv, w, m, v)


def _sum_sources(a, name):
    n, r, c = a.shape

    def body(a_ref, o_ref):
        tot = a_ref[0]
        for kk in range(1, n):
            tot = tot + a_ref[kk]
        o_ref[...] = tot

    return pl.pallas_call(
        body, name=name, out_shape=jax.ShapeDtypeStruct((r, c), F32),
        in_specs=[pl.BlockSpec(memory_space=pltpu.VMEM)], out_specs=pl.BlockSpec(memory_space=pltpu.VMEM),
    )(a)


def _adamw_small(w, g, m, v, name):
    def body(w_ref, g_ref, m_ref, v_ref, d_ref, mo_ref, vo_ref):
        delta, m_new, v_new = _adamw_math(w_ref[...], g_ref[...], m_ref[...], v_ref[...])
        d_ref[...] = delta
        mo_ref[...] = m_new
        vo_ref[...] = v_new

    vm = pl.BlockSpec(memory_space=pltpu.VMEM)
    out = jax.ShapeDtypeStruct(w.shape, F32)
    return pl.pallas_call(body, name=name, out_shape=[out, out, out], in_specs=[vm] * 4, out_specs=[vm] * 3)(w, g, m, v)


def _pack(parts, width, row_mult):
    flat = jnp.concatenate([p.reshape(-1) for p in parts])
    rows = -(-flat.shape[0] // width)
    rows = -(-rows // row_mult) * row_mult
    return jnp.pad(flat, (0, rows * width - flat.shape[0])).reshape(rows, width)


def _unpack(flat, shapes):
    out, off = [], 0
    lead = flat.shape[:-1]
    for shp in shapes:
        n = 1
        for dd in shp:
            n *= dd
        out.append(flat[..., off:off + n].reshape(lead + tuple(shp)))
        off += n
    return out


def _rows2d(w):
    return w.reshape(w.shape[0] * w.shape[1], w.shape[2])


def _cols_to_dev(g):
    l, k, n = g.shape
    return g.reshape(l * k, N_DEV, n // N_DEV).transpose(1, 0, 2)


def _rows_to_dev(g):
    l, k, n = g.shape
    rs = k // N_DEV
    return g.reshape(l, N_DEV, rs, n).transpose(1, 0, 2, 3).reshape(N_DEV, l * rs, n)


def _dev_to_cols(a, l):
    _, lk, cs = a.shape
    return a.transpose(1, 0, 2).reshape(l, lk // l, N_DEV * cs)


def _dev_to_rows(a, l):
    _, lr, n = a.shape
    rs = lr // l
    return a.reshape(N_DEV, l, rs, n).transpose(1, 0, 2, 3).reshape(l, N_DEV * rs, n)


def kernel(x, attn_norm, attn_w_in, attn_f_bias, fox_q_gain, fox_k_gain, sb_q_gain, sb_k_gain, attn_w_out, conv_norm, conv_w_in, conv_kernel, conv_w_out, ffn_norm, ffn_w_up, ffn_conv, ffn_w_down, loss_target, m_attn_norm, m_attn_w_in, m_attn_f_bias, m_fox_q_gain, m_fox_k_gain, m_sb_q_gain, m_sb_k_gain, m_attn_w_out, m_conv_norm, m_conv_w_in, m_conv_kernel, m_conv_w_out, m_ffn_norm, m_ffn_w_up, m_ffn_conv, m_ffn_w_down, v_attn_norm, v_attn_w_in, v_attn_f_bias, v_fox_q_gain, v_fox_k_gain, v_sb_q_gain, v_sb_k_gain, v_attn_w_out, v_conv_norm, v_conv_w_in, v_conv_kernel, v_conv_w_out, v_ffn_norm, v_ffn_w_up, v_ffn_conv, v_ffn_w_down):
    s = x.shape[1]
    n_attn, n_conv, depth = attn_w_in.shape[0], conv_w_in.shape[0], ffn_w_up.shape[0]
    xi, yi, ci = _coords()
    me = 4 * xi + 2 * yi + ci
    ids = jnp.stack([ci, 2 * xi + yi]).astype(jnp.int32)

    big = [attn_w_in, attn_w_out, conv_w_in, conv_w_out, ffn_w_up, ffn_w_down]
    big_m = [m_attn_w_in, m_attn_w_out, m_conv_w_in, m_conv_w_out, m_ffn_w_up, m_ffn_w_down]
    big_v = [v_attn_w_in, v_attn_w_out, v_conv_w_in, v_conv_w_out, v_ffn_w_up, v_ffn_w_down]
    big_names = ["attn_w_in", "attn_w_out", "conv_w_in", "conv_w_out", "ffn_w_up", "ffn_w_down"]
    small_sh = [conv_norm, conv_kernel, ffn_conv]
    small_sh_shapes = [w.shape for w in small_sh]
    rep = [attn_norm, attn_f_bias, fox_q_gain, fox_k_gain, sb_q_gain, sb_k_gain, ffn_norm]
    rep_shapes = [w.shape for w in rep]

    small_pack = _pack(small_sh, LANES, 8)
    shards_bf16 = [_rows2d(w).astype(BF16) for w in big]
    early = _all_gather([attn_w_in[0].astype(BF16), small_pack], "gather_first")
    first_w_in = jnp.pad(_dev_to_cols(early[0], 1)[0], ((0, 0), (0, ATTN_IN_PAD - ATTN_IN)))
    cn, ckern, fconv = _unpack(early[1].reshape(N_DEV, -1), small_sh_shapes)
    conv_norm_f = cn.transpose(1, 0, 2).reshape(n_conv, D_MODEL)
    conv_kernel_f = ckern.transpose(1, 2, 0, 3).reshape(n_conv, 3, D_MODEL)
    ffn_conv_f = fconv.transpose(1, 2, 0, 3).reshape(depth, 3, 2 * D_FF)

    def pair_gain(fox_g, sb_g):
        f2 = jnp.concatenate([fox_g, fox_g])
        s2 = jnp.concatenate([sb_g, sb_g])
        return jnp.concatenate([jnp.tile(f2[None], (4, 1)), jnp.tile(s2[None], (4, 1))])[:, None, :]

    h = x[0]
    saved = []
    for layer in range(depth):
        i = layer // 2
        tag = "l%d" % layer
        rec = {"h_in": h}
        if layer % 2 == 0:
            xn = xn_next if layer else _rms_fwd(h, attn_norm[i], tag + "_attn_rms")
            proj = _matmul(xn, first_w_in if layer == 0 else a_w_in[i], tag + "_attn_in", tn=640)
            gq, gk = pair_gain(fox_q_gain[i], sb_q_gain[i]), pair_gain(fox_k_gain[i], sb_k_gain[i])
            qh, kh, vb, qt, kt, vt = _qkv_prep(proj, gq, gk, tag + "_qkv_prep")
            logit = proj[:, 3 * MIX:3 * MIX + H_FOX].T.reshape(H_FOX, s // LANES, LANES)
            cum = _fgate_fwd(logit, attn_f_bias[i], tag + "_fgate")
            frow = cum.reshape(H_FOX // 2, 2, s)
            fcol = frow.transpose(0, 2, 1)
            o_fox, lse = _fox_fwd(qt, kh, vt, frow, fcol, tag + "_fox_fwd")
            if layer == 0:
                o_sb, tot, *gathered = _sb_fwd(qt, kh, vt, tag + "_sb_fwd", gather=shards_bf16)
                a_w_in = _dev_to_cols(gathered[0], n_attn)
                a_w_in = jnp.pad(a_w_in, ((0, 0), (0, 0), (0, ATTN_IN_PAD - ATTN_IN)))
                a_w_out = _dev_to_rows(gathered[1], n_attn)
                c_w_in = _dev_to_cols(gathered[2], n_conv)
                c_w_out = _dev_to_rows(gathered[3], n_conv)
                f_w_up = _dev_to_cols(gathered[4], depth)
                f_w_down = _dev_to_rows(gathered[5], depth)
            else:
                o_sb, tot = _sb_fwd(qt, kh, vt, tag + "_sb_fwd")
            o = jnp.concatenate([o_fox, o_sb], axis=1)
            h, xn2 = _matmul_rows(o, a_w_out[i], tag + "_attn_out", add=h, norm_gain=ffn_norm[layer], tm=1024)
            rec.update(xn=xn, proj=proj, gq=gq, gk=gk, qh=qh, kh=kh, vb=vb, qt=qt, kt=kt, logit=logit, frow=frow,
                       fcol=fcol, o_fox=o_fox, lse=lse, tot=tot, o=o)
        else:
            xn = xn_next
            proj = _matmul(xn, c_w_in[i], tag + "_conv_in", tn=1024)
            y = _sconv_fwd(proj, conv_kernel_f[i], tag + "_sconv_fwd")
            h, xn2 = _matmul_rows(y, c_w_out[i], tag + "_conv_out", add=h, norm_gain=ffn_norm[layer], tm=1024)
            rec.update(xn=xn, proj=proj, y=y)
        rec["h_mid"] = h
        up = _matmul(xn2, f_w_up[layer], tag + "_ffn_up", tn=1408)
        act = _ffn_act_fwd(up, ffn_conv_f[layer], tag + "_ffn_act")
        if layer + 1 < depth:
            nxt = layer + 1
            gain = attn_norm[nxt // 2] if nxt % 2 == 0 else conv_norm_f[nxt // 2]
            h, xn_next = _matmul_rows(act, f_w_down[layer], tag + "_ffn_down", add=h, norm_gain=gain, tk=2816)
        else:
            h = _matmul(act, f_w_down[layer], tag + "_ffn_down", add=h, tn=1024, tk=2816)
        rec.update(xn2=xn2, up=up, act=act)
        saved.append(rec)

    loss_local, dh = _loss_head(h, loss_target[0], "loss_head")
    loss = lax.psum(loss_local, ("x", "y", "c"))

    g_attn_norm, g_attn_w_in, g_f_bias = [None] * n_attn, [None] * n_attn, [None] * n_attn
    g_fq, g_fk, g_sq, g_sk, g_attn_w_out = ([None] * n_attn for _ in range(5))
    g_conv_norm, g_conv_w_in, g_conv_kernel, g_conv_w_out = ([None] * n_conv for _ in range(4))
    g_ffn_norm, g_ffn_w_up, g_ffn_conv, g_ffn_w_down = ([None] * depth for _ in range(4))
    everything = slice(0, None)
    early_layers = {nm: (slice(1, None) if nm == "attn_w_in" else everything) for nm in big_names}
    late_layers = {"attn_w_in": slice(0, 1)}

    def slabs_for_devices(layers):
        stacks = {"attn_w_in": (g_attn_w_in, _cols_to_dev), "attn_w_out": (g_attn_w_out, _rows_to_dev),
                  "conv_w_in": (g_conv_w_in, _cols_to_dev), "conv_w_out": (g_conv_w_out, _rows_to_dev),
                  "ffn_w_up": (g_ffn_w_up, _cols_to_dev), "ffn_w_down": (g_ffn_w_down, _rows_to_dev)}
        return [stacks[nm][1](jnp.stack(stacks[nm][0][layers[nm]]).astype(BF16)) for nm in big_names if nm in layers]

    for layer in reversed(range(depth)):
        i = layer // 2
        tag = "l%d" % layer
        rec = saved[layer]
        da = _matmul(dh, f_w_down[layer].T, tag + "_ffn_down_dx", tn=1408)
        g_ffn_w_down[layer] = _matmul_tn(rec["act"], dh, tag + "_ffn_down_dw", tm=1408, tn=1024)
        dup, dwg, dwv = _ffn_act_bwd(rec["up"], ffn_conv_f[layer], da, tag + "_ffn_act_bwd")
        g_ffn_conv[layer] = jnp.concatenate([dwg, dwv], axis=1)
        g_ffn_w_up[layer] = _matmul_tn(rec["xn2"], dup, tag + "_ffn_up_dw", tn=1408)
        dh, g_ffn_norm[layer] = _matmul_rows(dup, f_w_up[layer].T, tag + "_ffn_up_dx", tk=2816,
                                             rms_bwd=(rec["h_mid"], ffn_norm[layer], dh))
        if layer % 2 == 0:
            do = _matmul(dh, a_w_out[i].T, tag + "_attn_out_dx", tn=1024)
            g_attn_w_out[i] = _matmul_tn(rec["o"], dh, tag + "_attn_out_dw", tn=1024)
            if layer == 0:
                early_g = slabs_for_devices(early_layers)
                early_pairs = [_pair_add(g, r, ids, "reduce_pair_add_" + nm)
                               for g, r, nm in zip(early_g, _sibling_exchange(early_g, "reduce_sibling"), big_names)]
            dq_f, dk_f, dv_f, dfk, dfq = _fox_bwd(rec["qh"], rec["qt"], rec["kh"], rec["kt"], rec["vb"], rec["frow"],
                                                  rec["fcol"], rec["lse"], rec["o_fox"], do, tag + "_fox_bwd")
            dq_s, dk_s, dv_s, *from_chips = _sb_bwd(
                rec["qh"], rec["qt"], rec["kh"], rec["kt"], rec["vb"], rec["tot"], do, tag + "_sb_bwd",
                exchange=[pr[0] for pr in early_pairs] if layer == 0 else ())
            if layer == 0:
                early_from_chips = from_chips
            dq, dk, dv, dgq, dgk = _qkv_prep_bwd(rec["proj"], rec["gq"], rec["gk"], (dq_f, dq_s), (dk_f, dk_s),
                                                 (dv_f, dv_s), tag + "_qkv_prep_bwd")
            dcum = (dfq + dfk[:, :, 0:2].transpose(0, 2, 1)).reshape(H_FOX, s // LANES, LANES)
            dlogit, dbias = _fgate_bwd(rec["logit"], attn_f_bias[i], dcum, tag + "_fgate_bwd")
            g_f_bias[i] = dbias[:, 0]
            dgate = jnp.pad(dlogit.reshape(H_FOX, s).T, ((0, 0), (0, LANES - H_FOX))).astype(BF16)
            dproj = jnp.concatenate([dq, dk, dv, dgate], axis=1)

            def fold(dg):
                per_head = dg.reshape(16, HEAD_DIM)
                return jnp.sum(per_head[:8], axis=0), jnp.sum(per_head[8:], axis=0)

            g_fq[i], g_sq[i] = fold(dgq)
            g_fk[i], g_sk[i] = fold(dgk)
            g_attn_w_in[i] = _matmul_tn(rec["xn"], dproj, tag + "_attn_in_dw", tn=640)[:, :ATTN_IN]
            dh, g_attn_norm[i] = _matmul_rows(dproj, a_w_in[i].T, tag + "_attn_in_dx", tk=3200,
                                              rms_bwd=(rec["h_in"], attn_norm[i], dh))
        else:
            dy = _matmul(dh, c_w_out[i].T, tag + "_conv_out_dx", tn=1024)
            g_conv_w_out[i] = _matmul_tn(rec["y"], dh, tag + "_conv_out_dw", tn=1024)
            dproj, g_conv_kernel[i] = _sconv_bwd(rec["proj"], conv_kernel_f[i], dy, tag + "_sconv_bwd")
            g_conv_w_in[i] = _matmul_tn(rec["xn"], dproj, tag + "_conv_in_dw", tn=1024)
            dh, g_conv_norm[i] = _matmul_rows(dproj, c_w_in[i].T, tag + "_conv_in_dx", tm=1024, tk=1024,
                                              rms_bwd=(rec["h_in"], conv_norm_f[i], dh))
    grad_x = dh[None]

    late_names = [nm for nm in big_names if nm in late_layers]
    late_g = slabs_for_devices(late_layers)
    late_pairs = [_pair_add(g, r, ids, "reduce_pair_add_late_" + nm)
                  for g, r, nm in zip(late_g, _sibling_exchange(late_g, "reduce_sibling_late"), late_names)]
    late_from_chips = _chip_exchange([pr[0] for pr in late_pairs], "reduce_chips_late")

    def update(piece_pairs, piece_recv, piece_names, layers):
        outs = {}
        for (own, recv, nm) in zip([pr[1] for pr in piece_pairs], piece_recv, piece_names):
            sl = layers[nm]
            which = big_names.index(nm)
            outs[nm] = _adamw_reduce(own, recv, ids, _rows2d(big[which][sl]), _rows2d(big_m[which][sl]),
                                     _rows2d(big_v[which][sl]), "adamw_%s_%d" % (nm, sl.start))
        return outs

    late_out = update(late_pairs, late_from_chips, late_names, late_layers)
    early_out = update(early_pairs, early_from_chips, big_names, early_layers)
    grads_big, delta_big, newm_big, newv_big = [], [], [], []
    for which, nm in enumerate(big_names):
        shp = big[which].shape
        for k, dest in enumerate((grads_big, delta_big, newm_big, newv_big)):
            parts = ([late_out[nm][k]] if nm in late_out else []) + [early_out[nm][k]]
            dest.append(jnp.concatenate(parts, axis=0).reshape(shp))

    rep_g = [jnp.stack(g_attn_norm), jnp.stack(g_f_bias), jnp.stack(g_fq), jnp.stack(g_fk), jnp.stack(g_sq),
             jnp.stack(g_sk), jnp.stack(g_ffn_norm)]
    sh_g = [jnp.stack(g_conv_norm).reshape(n_conv, N_DEV, -1).transpose(1, 0, 2),
            jnp.stack(g_conv_kernel).reshape(n_conv, 3, N_DEV, -1).transpose(2, 0, 1, 3),
            jnp.stack(g_ffn_conv).reshape(depth, 3, N_DEV, -1).transpose(2, 0, 1, 3)]
    n_rep = sum(int(a.size) for a in rep)
    n_sh = sum(int(a.size) for a in small_sh)
    partial = _pack(rep_g + [jnp.concatenate([a.reshape(N_DEV, -1) for a in sh_g], axis=1)], LANES, 8)
    total = _sum_sources(_all_gather([partial], "gather_small_grads")[0], "sum_small_grads").reshape(-1)
    rep_tot = total[:n_rep]
    sh_tot = lax.dynamic_slice_in_dim(total[n_rep:n_rep + N_DEV * n_sh].reshape(N_DEV, n_sh), me, 1, axis=0)[0]
    g_small = _pack([rep_tot, sh_tot], LANES, 8)

    def small_pack_of(rep_list, sh_list):
        return _pack(rep_list + sh_list, LANES, 8)

    d_small, m_small, v_small = _adamw_small(
        small_pack_of(rep, small_sh), g_small,
        small_pack_of([m_attn_norm, m_attn_f_bias, m_fox_q_gain, m_fox_k_gain, m_sb_q_gain, m_sb_k_gain, m_ffn_norm],
                      [m_conv_norm, m_conv_kernel, m_ffn_conv]),
        small_pack_of([v_attn_norm, v_attn_f_bias, v_fox_q_gain, v_fox_k_gain, v_sb_q_gain, v_sb_k_gain, v_ffn_norm],
                      [v_conv_norm, v_conv_kernel, v_ffn_conv]),
        "adamw_small")
    small_shapes = rep_shapes + small_sh_shapes

    def split_small(a):
        return _unpack(a.reshape(-1), small_shapes)

    def ordered(big_list, small_list):
        an, fb, fq, fk, sq, sk, fn, cno, cke, fco = small_list
        awi, awo, cwi, cwo, fwu, fwd = big_list
        return [an, awi, fb, fq, fk, sq, sk, awo, cno, cwi, cke, cwo, fn, fwu, fco, fwd]

    grads = ordered(grads_big, split_small(g_small))
    deltas = ordered(delta_big, split_small(d_small))
    new_m = ordered(newm_big, split_small(m_small))
    new_v = ordered(newv_big, split_small(v_small))
    return (loss, grad_x, *grads, *deltas, *new_m, *new_v)
```

```python
import jax
import jax.numpy as jnp
from jax import lax
from jax.experimental import pallas as pl
from jax.experimental.pallas import tpu as pltpu

F32 = jnp.float32
BF16 = jnp.bfloat16

D_MODEL = 1024
HEAD_DIM = 64
H_FOX = 8
MIX = 1024
ATTN_IN = 3 * MIX + H_FOX
ATTN_IN_PAD = 3 * MIX + 128
D_FF = 2816
EPS = 1e-6
NEG = -1e30
LANES = 128
N_DEV = 8
N_CHIP = 4

ADAM_LR = 0.001
ADAM_B1 = 0.9
ADAM_B2 = 0.999
ADAM_EPS = 1e-08
ADAM_WD = 0.01
ADAM_STEP = 10

VMEM_LIMIT = 56 * 1024 * 1024
MESH = pl.DeviceIdType.MESH
ANY = pl.BlockSpec(memory_space=pl.ANY)


def _params(*sem):
    return pltpu.CompilerParams(dimension_semantics=sem, vmem_limit_bytes=VMEM_LIMIT)


def _tile(n, target, mult=LANES):
    best = None
    for t in range(mult, min(n, target) + 1, mult):
        if n % t == 0:
            best = t
    return best if best is not None else n


def _dot(a, b):
    return jnp.dot(a, b, preferred_element_type=F32)


def _dot_tn(a, b):
    return lax.dot_general(a, b, (((0,), (0,)), ((), ())), preferred_element_type=F32)


def _split_dot(x, m, passes):
    acc = None
    rem = x
    for _ in range(passes):
        part = rem.astype(BF16)
        term = _dot(part, m)
        acc = term if acc is None else acc + term
        rem = rem - part.astype(F32)
    return acc


def _split_dot_left(m, x, passes):
    acc = None
    rem = x
    for _ in range(passes):
        part = rem.astype(BF16)
        term = _dot(m, part)
        acc = term if acc is None else acc + term
        rem = rem - part.astype(F32)
    return acc


def _matmul(a, b, name, add=None, out_dtype=F32, tm=1024, tn=512, tk=1024):
    split = a.shape[0] if a.ndim == 3 else 1
    m, kh = a.shape[-2:]
    k = split * kh
    n = b.shape[1]
    tm, tn, tk = _tile(m, tm, 8), _tile(n, tn), _tile(kh, tk)
    nk = k // tk
    per_slab = kh // tk
    has_add = add is not None

    def body(*refs):
        a_ref, b_ref = refs[0], refs[1]
        add_ref = refs[2] if has_add else None
        o_ref = refs[2 + has_add]

        def finish(acc):
            if has_add:
                acc = acc + add_ref[...]
            o_ref[...] = acc.astype(out_dtype)

        p = _dot(a_ref[...].astype(BF16), b_ref[...].astype(BF16))
        if nk == 1:
            finish(p)
        else:
            acc_ref = refs[-1]
            kk = pl.program_id(2)

            @pl.when(kk == 0)
            def _():
                acc_ref[...] = p

            @pl.when(kk > 0)
            def _():
                acc_ref[...] += p

            @pl.when(kk == nk - 1)
            def _():
                finish(acc_ref[...])

    if split == 1:
        a_spec = pl.BlockSpec((tm, tk), lambda i, j, kk: (i, kk))
    else:
        a_spec = pl.BlockSpec((None, tm, tk), lambda i, j, kk: (kk // per_slab, i, kk % per_slab))
    in_specs = [a_spec, pl.BlockSpec((tk, tn), lambda i, j, kk: (kk, j))]
    args = [a, b]
    if has_add:
        in_specs.append(pl.BlockSpec((tm, tn), lambda i, j, kk: (i, j)))
        args.append(add)
    return pl.pallas_call(
        body, name=name, grid=(m // tm, n // tn, nk), in_specs=in_specs,
        out_specs=pl.BlockSpec((tm, tn), lambda i, j, kk: (i, j)),
        out_shape=jax.ShapeDtypeStruct((m, n), out_dtype),
        scratch_shapes=[pltpu.VMEM((tm, tn), F32)] if nk > 1 else [],
        compiler_params=_params("parallel", "parallel", "arbitrary"),
    )(*args)


def _matmul_rows(a, b, name, add=None, norm_gain=None, rms_bwd=None, tm=512, tk=1024):
    split = a.shape[0] if a.ndim == 3 else 1
    m, kh = a.shape[-2:]
    n = b.shape[1]
    tm, tk = _tile(m, tm, 8), _tile(kh, tk)
    nk = split * kh // tk
    per_slab = kh // tk
    has_add, has_norm, has_bwd = add is not None, norm_gain is not None, rms_bwd is not None

    def rsqrt_mean_sq(x):
        return lax.rsqrt(jnp.mean(x * x, axis=-1, keepdims=True) + EPS)

    def body(*refs):
        it = iter(refs)
        a_ref, b_ref = next(it), next(it)
        add_ref = next(it) if has_add else None
        gain_ref = next(it) if has_norm else None
        h_ref, g_ref, dhin_ref = (next(it), next(it), next(it)) if has_bwd else (None, None, None)
        o_ref = next(it)
        xn_ref = next(it) if has_norm else None
        dg_ref = next(it) if has_bwd else None
        acc_ref = next(it) if nk > 1 else None
        i = pl.program_id(0)

        def finish(acc):
            if has_add:
                acc = acc + add_ref[...]
            if has_bwd:
                x = h_ref[...]
                r = rsqrt_mean_sq(x)
                xh = x * r

                @pl.when(i == 0)
                def _():
                    dg_ref[...] = jnp.zeros_like(dg_ref)

                dg_ref[0:1, :] += jnp.sum(acc * xh, axis=0, keepdims=True)
                dxh = acc * g_ref[...]
                acc = dhin_ref[...] + r * (dxh - xh * jnp.mean(dxh * xh, axis=-1, keepdims=True))
            o_ref[...] = acc
            if has_norm:
                xn_ref[...] = (acc * rsqrt_mean_sq(acc) * gain_ref[...]).astype(BF16)

        p = _dot(a_ref[...].astype(BF16), b_ref[...].astype(BF16))
        if nk == 1:
            finish(p)
        else:
            kk = pl.program_id(1)

            @pl.when(kk == 0)
            def _():
                acc_ref[...] = p

            @pl.when(kk > 0)
            def _():
                acc_ref[...] += p

            @pl.when(kk == nk - 1)
            def _():
                finish(acc_ref[...])

    if split == 1:
        a_spec = pl.BlockSpec((tm, tk), lambda i, kk: (i, kk))
    else:
        a_spec = pl.BlockSpec((None, tm, tk), lambda i, kk: (kk // per_slab, i, kk % per_slab))
    row = pl.BlockSpec((tm, n), lambda i, kk: (i, 0))
    vec = pl.BlockSpec((1, n), lambda i, kk: (0, 0))
    in_specs, args = [a_spec, pl.BlockSpec((tk, n), lambda i, kk: (kk, 0))], [a, b]
    if has_add:
        in_specs.append(row)
        args.append(add)
    if has_norm:
        in_specs.append(vec)
        args.append(norm_gain.reshape(1, n))
    if has_bwd:
        in_specs += [row, vec, row]
        args += [rms_bwd[0], rms_bwd[1].reshape(1, n), rms_bwd[2]]
    out_specs, out_shape = [row], [jax.ShapeDtypeStruct((m, n), F32)]
    if has_norm:
        out_specs.append(row)
        out_shape.append(jax.ShapeDtypeStruct((m, n), BF16))
    if has_bwd:
        out_specs.append(pl.BlockSpec((8, n), lambda i, kk: (0, 0)))
        out_shape.append(jax.ShapeDtypeStruct((8, n), F32))
    outs = pl.pallas_call(
        body, name=name, grid=(m // tm, nk), in_specs=in_specs, out_specs=out_specs, out_shape=out_shape,
        scratch_shapes=[pltpu.VMEM((tm, n), F32)] if nk > 1 else [],
        compiler_params=_params("arbitrary" if has_bwd else "parallel", "arbitrary"),
    )(*args)
    if has_bwd:
        return outs[0], outs[-1][0]
    return tuple(outs) if has_norm else outs[0]


def _matmul_tn(a, b, name, tm=1024, tn=512, ts=2048):
    s, m = a.shape
    split = b.shape[0] if b.ndim == 3 else 1
    nh = b.shape[-1]
    n = split * nh
    tm, tn, ts = _tile(m, tm), _tile(nh, tn), _tile(s, ts, 8)
    per_slab = nh // tn
    if split == 1:
        b_spec = pl.BlockSpec((ts, tn), lambda i, j, kk: (kk, j))
    else:
        b_spec = pl.BlockSpec((None, ts, tn), lambda i, j, kk: (j // per_slab, kk, j % per_slab))

    def body(a_ref, b_ref, o_ref):
        kk = pl.program_id(2)
        p = _dot_tn(a_ref[...].astype(BF16), b_ref[...].astype(BF16))

        @pl.when(kk == 0)
        def _():
            o_ref[...] = p

        @pl.when(kk > 0)
        def _():
            o_ref[...] += p

    return pl.pallas_call(
        body, name=name, grid=(m // tm, n // tn, s // ts),
        in_specs=[pl.BlockSpec((ts, tm), lambda i, j, kk: (kk, i)), b_spec],
        out_specs=pl.BlockSpec((tm, tn), lambda i, j, kk: (i, j)),
        out_shape=jax.ShapeDtypeStruct((m, n), F32),
        compiler_params=_params("parallel", "parallel", "arbitrary"),
    )(a, b)


def _rms_fwd(h, g, name, ts=512):
    s, d = h.shape
    ts = _tile(s, ts, 8)

    def body(h_ref, g_ref, o_ref):
        x = h_ref[...]
        r = lax.rsqrt(jnp.mean(x * x, axis=-1, keepdims=True) + EPS)
        o_ref[...] = (x * r * g_ref[...]).astype(BF16)

    return pl.pallas_call(
        body, name=name, grid=(s // ts,),
        in_specs=[pl.BlockSpec((ts, d), lambda i: (i, 0)), pl.BlockSpec((1, d), lambda i: (0, 0))],
        out_specs=pl.BlockSpec((ts, d), lambda i: (i, 0)),
        out_shape=jax.ShapeDtypeStruct((s, d), BF16),
        compiler_params=_params("parallel"),
    )(h, g.reshape(1, d))


def _rms_bwd(h, dxn, g, dh_in, name, ts=512):
    s, d = h.shape
    ts = _tile(s, ts, 8)

    def body(h_ref, dxn_ref, g_ref, dhin_ref, dh_ref, dg_ref):
        i = pl.program_id(0)
        x = h_ref[...]
        r = lax.rsqrt(jnp.mean(x * x, axis=-1, keepdims=True) + EPS)
        xh = x * r
        dxn_v = dxn_ref[...]

        @pl.when(i == 0)
        def _():
            dg_ref[...] = jnp.zeros_like(dg_ref)

        dg_ref[0:1, :] += jnp.sum(dxn_v * xh, axis=0, keepdims=True)
        dxh = dxn_v * g_ref[...]
        dx = r * (dxh - xh * jnp.mean(dxh * xh, axis=-1, keepdims=True))
        dh_ref[...] = dhin_ref[...] + dx

    row = pl.BlockSpec((ts, d), lambda i: (i, 0))
    dh, dg = pl.pallas_call(
        body, name=name, grid=(s // ts,),
        in_specs=[row, row, pl.BlockSpec((1, d), lambda i: (0, 0)), row],
        out_specs=[row, pl.BlockSpec((8, d), lambda i: (0, 0))],
        out_shape=[jax.ShapeDtypeStruct((s, d), F32), jax.ShapeDtypeStruct((8, d), F32)],
        compiler_params=_params("arbitrary"),
    )(h, dxn, g.reshape(1, d), dh_in)
    return dh, dg[0]


def _shift_down(x, prev):
    rows = lax.broadcasted_iota(jnp.int32, (8, x.shape[1]), 0)
    p1, p2 = prev[7:8, :], prev[6:7, :]
    r1, r2 = pltpu.roll(x, 1, 0), pltpu.roll(x, 2, 0)
    top1 = jnp.where(rows == 0, p1, r1[0:8, :])
    top2 = jnp.where(rows == 0, p2, jnp.where(rows == 1, p1, r2[0:8, :]))
    if x.shape[0] == 8:
        return top1, top2
    return jnp.concatenate([top1, r1[8:, :]], axis=0), jnp.concatenate([top2, r2[8:, :]], axis=0)


def _shift_up(x, nxt):
    n = x.shape[0]
    rows = lax.broadcasted_iota(jnp.int32, (8, x.shape[1]), 0)
    n0, n1 = nxt[0:1, :], nxt[1:2, :]
    r1, r2 = pltpu.roll(x, n - 1, 0), pltpu.roll(x, n - 2, 0)
    end1 = jnp.where(rows == 7, n0, r1[n - 8:, :])
    end2 = jnp.where(rows == 7, n1, jnp.where(rows == 6, n0, r2[n - 8:, :]))
    return jnp.concatenate([r1[:n - 8, :], end1], axis=0), jnp.concatenate([r2[:n - 8, :], end2], axis=0)


def _conv(x, x1, x2, w):
    return w[2:3, :] * x + w[1:2, :] * x1 + w[0:1, :] * x2


def _halo_specs(ts, tc, col, n_time_blocks):
    r8 = ts // 8
    main = pl.BlockSpec((ts, tc), lambda j, i: (i, j + col))
    prev = pl.BlockSpec((8, tc), lambda j, i: (jnp.maximum(i * r8 - 1, 0), j + col))
    nxt = pl.BlockSpec((8, tc), lambda j, i: (jnp.minimum((i + 1) * r8, n_time_blocks * r8 - 1), j + col))
    return main, prev, nxt


def _silu_parts(g):
    sig = 1.0 / (1.0 + jnp.exp(-g))
    return sig, g * sig


def _ffn_act_fwd(up, cw, name, ts=512, tc=1408):
    s = up.shape[0]
    ts, tc = _tile(s, ts, 8), _tile(D_FF, tc)
    nc, nt = D_FF // tc, s // ts

    def body(g_ref, gp_ref, v_ref, vp_ref, wg_ref, wv_ref, o_ref):
        first = pl.program_id(1) == 0

        def conv(x_ref, p_ref, w_ref):
            x = x_ref[...]
            prev = jnp.where(first, 0.0, p_ref[...])
            x1, x2 = _shift_down(x, prev)
            return _conv(x, x1, x2, w_ref[...])

        ug = conv(g_ref, gp_ref, wg_ref)
        uv = conv(v_ref, vp_ref, wv_ref)
        _, silu = _silu_parts(ug)
        o_ref[...] = (silu * uv).astype(BF16)

    g_main, g_prev, _ = _halo_specs(ts, tc, 0, nt)
    v_main, v_prev, _ = _halo_specs(ts, tc, nc, nt)
    return pl.pallas_call(
        body, name=name, grid=(nc, nt),
        in_specs=[g_main, g_prev, v_main, v_prev,
                  pl.BlockSpec((3, tc), lambda j, i: (0, j)), pl.BlockSpec((3, tc), lambda j, i: (0, j + nc))],
        out_specs=pl.BlockSpec((ts, tc), lambda j, i: (i, j)),
        out_shape=jax.ShapeDtypeStruct((s, D_FF), BF16),
        compiler_params=_params("parallel", "parallel"),
    )(up, up, up, up, cw, cw)


def _ffn_act_bwd(up, cw, da, name, ts=256, tc=1408):
    s = up.shape[0]
    ts, tc = _tile(s, ts, 8), _tile(D_FF, tc)
    nc, nt = D_FF // tc, s // ts

    def body(g_ref, gp_ref, gn_ref, v_ref, vp_ref, vn_ref, da_ref, dan_ref, wg_ref, wv_ref,
             d_ref, dwg_ref, dwv_ref):
        i = pl.program_id(1)
        first, last = i == 0, i == nt - 1
        wg, wv = wg_ref[...], wv_ref[...]
        g, v = g_ref[...], v_ref[...]
        g1, g2 = _shift_down(g, jnp.where(first, 0.0, gp_ref[...]))
        v1, v2 = _shift_down(v, jnp.where(first, 0.0, vp_ref[...]))

        def d_u(ug, uv, da_v):
            sig, silu = _silu_parts(ug)
            return da_v * uv * (sig * (1.0 + ug * (1.0 - sig))), da_v * silu

        dug, duv = d_u(_conv(g, g1, g2, wg), _conv(v, v1, v2, wv), da_ref[...])
        gn, vn = gn_ref[...], vn_ref[...]
        gn1, gn2 = _shift_down(gn, g[ts - 8:, :])
        vn1, vn2 = _shift_down(vn, v[ts - 8:, :])
        dugn, duvn = d_u(_conv(gn, gn1, gn2, wg), _conv(vn, vn1, vn2, wv), dan_ref[...])
        dugn = jnp.where(last, 0.0, dugn)
        duvn = jnp.where(last, 0.0, duvn)

        def finish(du, dun, x, x1, x2, w, dx_ref, dw_ref):
            d1, d2 = _shift_up(du, dun)
            dx_ref[...] = (w[2:3, :] * du + w[1:2, :] * d1 + w[0:1, :] * d2).astype(BF16)

            @pl.when(first)
            def _():
                dw_ref[...] = jnp.zeros_like(dw_ref)

            dw_ref[0:1, :] += jnp.sum(du * x2, axis=0, keepdims=True)
            dw_ref[1:2, :] += jnp.sum(du * x1, axis=0, keepdims=True)
            dw_ref[2:3, :] += jnp.sum(du * x, axis=0, keepdims=True)

        finish(dug, dugn, g, g1, g2, wg, d_ref.at[0], dwg_ref)
        finish(duv, duvn, v, v1, v2, wv, d_ref.at[1], dwv_ref)

    g_specs = _halo_specs(ts, tc, 0, nt)
    v_specs = _halo_specs(ts, tc, nc, nt)
    da_main, _, da_next = _halo_specs(ts, tc, 0, nt)
    taps = pl.BlockSpec((8, tc), lambda j, i: (0, j))
    halves = pl.BlockSpec((2, ts, tc), lambda j, i: (0, i, j))
    d, dwg, dwv = pl.pallas_call(
        body, name=name, grid=(nc, nt),
        in_specs=[*g_specs, *v_specs, da_main, da_next,
                  pl.BlockSpec((3, tc), lambda j, i: (0, j)), pl.BlockSpec((3, tc), lambda j, i: (0, j + nc))],
        out_specs=[halves, taps, taps],
        out_shape=[jax.ShapeDtypeStruct((2, s, D_FF), BF16),
                   jax.ShapeDtypeStruct((8, D_FF), F32), jax.ShapeDtypeStruct((8, D_FF), F32)],
        compiler_params=_params("parallel", "arbitrary"),
    )(up, up, up, up, up, up, da, da, cw, cw)
    return d, dwg[:3], dwv[:3]


def _sconv_fwd(proj, ck, name, ts=256, tc=1024):
    s = proj.shape[0]
    w = D_MODEL
    ts, tc = _tile(s, ts, 8), _tile(w, tc)
    nc, nt = w // tc, s // ts

    def body(b_ref, c_ref, cp_ref, u_ref, up_ref, w_ref, o_ref):
        first = pl.program_id(1) == 0
        cu = c_ref[...] * u_ref[...]
        cup = jnp.where(first, 0.0, cp_ref[...] * up_ref[...])
        x1, x2 = _shift_down(cu, cup)
        o_ref[...] = (b_ref[...] * _conv(cu, x1, x2, w_ref[...])).astype(BF16)

    b_main, _, _ = _halo_specs(ts, tc, 0, nt)
    c_main, c_prev, _ = _halo_specs(ts, tc, nc, nt)
    u_main, u_prev, _ = _halo_specs(ts, tc, 2 * nc, nt)
    return pl.pallas_call(
        body, name=name, grid=(nc, nt),
        in_specs=[b_main, c_main, c_prev, u_main, u_prev, pl.BlockSpec((3, tc), lambda j, i: (0, j))],
        out_specs=pl.BlockSpec((ts, tc), lambda j, i: (i, j)),
        out_shape=jax.ShapeDtypeStruct((s, w), BF16),
        compiler_params=_params("parallel", "parallel"),
    )(proj, proj, proj, proj, proj, ck)


def _sconv_bwd(proj, ck, dy, name, ts=256, tc=1024):
    s = proj.shape[0]
    w = D_MODEL
    ts, tc = _tile(s, ts, 8), _tile(w, tc)
    nc, nt = w // tc, s // ts

    def body(b_ref, bn_ref, c_ref, cp_ref, u_ref, up_ref, dy_ref, dyn_ref, w_ref,
             d_ref, dw_ref):
        i = pl.program_id(1)
        first, last = i == 0, i == nt - 1
        wv = w_ref[...]
        b, c, u, dy_v = b_ref[...], c_ref[...], u_ref[...], dy_ref[...]
        cu = c * u
        cup = jnp.where(first, 0.0, cp_ref[...] * up_ref[...])
        x1, x2 = _shift_down(cu, cup)
        d_ref[0] = (dy_v * _conv(cu, x1, x2, wv)).astype(BF16)
        dcv = dy_v * b
        dcvn = jnp.where(last, 0.0, dyn_ref[...] * bn_ref[...])
        d1, d2 = _shift_up(dcv, dcvn)
        dcu = wv[2:3, :] * dcv + wv[1:2, :] * d1 + wv[0:1, :] * d2
        d_ref[1] = (dcu * u).astype(BF16)
        d_ref[2] = (dcu * c).astype(BF16)

        @pl.when(first)
        def _():
            dw_ref[...] = jnp.zeros_like(dw_ref)

        dw_ref[0:1, :] += jnp.sum(dcv * x2, axis=0, keepdims=True)
        dw_ref[1:2, :] += jnp.sum(dcv * x1, axis=0, keepdims=True)
        dw_ref[2:3, :] += jnp.sum(dcv * cu, axis=0, keepdims=True)

    b_main, _, b_next = _halo_specs(ts, tc, 0, nt)
    c_main, c_prev, _ = _halo_specs(ts, tc, nc, nt)
    u_main, u_prev, _ = _halo_specs(ts, tc, 2 * nc, nt)
    dy_main, _, dy_next = _halo_specs(ts, tc, 0, nt)
    d, dw = pl.pallas_call(
        body, name=name, grid=(nc, nt),
        in_specs=[b_main, b_next, c_main, c_prev, u_main, u_prev, dy_main, dy_next,
                  pl.BlockSpec((3, tc), lambda j, i: (0, j))],
        out_specs=[pl.BlockSpec((3, ts, tc), lambda j, i: (0, i, j)), pl.BlockSpec((8, tc), lambda j, i: (0, j))],
        out_shape=[jax.ShapeDtypeStruct((3, s, w), BF16), jax.ShapeDtypeStruct((8, w), F32)],
        compiler_params=_params("parallel", "arbitrary"),
    )(proj, proj, proj, proj, proj, proj, dy, dy, ck)
    return d, dw[:3]


def _low_lanes(shape):
    return lax.broadcasted_iota(jnp.int32, shape, 1) < HEAD_DIM


def _top_rows(shape):
    return lax.broadcasted_iota(jnp.int32, shape, 0) < HEAD_DIM


def _norm_pair(x):
    r = lax.rsqrt(_mean_pair(x * x) + EPS)
    return x * r, r


def _mean_pair(x):
    same_head = _tri(LANES, lambda a, b: a // HEAD_DIM == b // HEAD_DIM)
    return _split_dot(x, same_head, 3) * (1.0 / HEAD_DIM)


def _qkv_prep(proj, gq, gk, name, ts=1024):
    s = proj.shape[0]
    ts = _tile(s, ts)
    npair = MIX // LANES
    scale = HEAD_DIM ** -0.5

    def body(q_ref, k_ref, v_ref, gq_ref, gk_ref, qo_ref, ko_ref, vo_ref, qt_ref, kt_ref, vt_ref):
        qn, _ = _norm_pair(q_ref[...])
        kn, _ = _norm_pair(k_ref[...])
        q = qn * gq_ref[0] * scale
        k = kn * gk_ref[0]
        v = v_ref[...]
        qo_ref[...] = q.astype(BF16)
        ko_ref[...] = k.astype(BF16)
        vo_ref[...] = v.astype(BF16)
        qt_ref[...] = q.T.astype(BF16)
        kt_ref[...] = k.T.astype(BF16)
        vt_ref[...] = v.T.astype(BF16)

    gain = pl.BlockSpec((1, 1, LANES), lambda i, p: (p, 0, 0))
    tile = pl.BlockSpec((ts, LANES), lambda i, p: (i, p))
    tile_t = pl.BlockSpec((LANES, ts), lambda i, p: (p, i))
    out = jax.ShapeDtypeStruct((s, MIX), BF16)
    out_t = jax.ShapeDtypeStruct((MIX, s), BF16)
    return pl.pallas_call(
        body, name=name, grid=(s // ts, npair),
        in_specs=[tile, pl.BlockSpec((ts, LANES), lambda i, p: (i, p + npair)),
                  pl.BlockSpec((ts, LANES), lambda i, p: (i, p + 2 * npair)), gain, gain],
        out_specs=[tile, tile, tile, tile_t, tile_t, tile_t], out_shape=[out, out, out, out_t, out_t, out_t],
        compiler_params=_params("parallel", "parallel"),
    )(proj, proj, proj, gq, gk)


def _qkv_prep_bwd(proj, gq, gk, dqs, dks, dvs, name, ts=1024):
    s = proj.shape[0]
    ts = _tile(s, ts, 8)
    npair = MIX // LANES
    half = npair // 2
    scale = HEAD_DIM ** -0.5

    def body(q_ref, k_ref, gq_ref, gk_ref, dqf_ref, dqs_ref, dkf_ref, dks_ref, dvf_ref, dvs_ref,
             dq_ref, dk_ref, dv_ref, dgq_ref, dgk_ref):
        p, i = pl.program_id(0), pl.program_id(1)
        fox = p < half

        def one(x_ref, g_ref, df_ref, ds_ref, dx_ref, dg_ref, mult):
            dn = jnp.where(fox, df_ref[...], ds_ref[...]) * mult
            xh, r = _norm_pair(x_ref[...])

            @pl.when(i == 0)
            def _():
                dg_ref[...] = jnp.zeros_like(dg_ref)

            dg_ref[0, 0:1, :] += jnp.sum(dn * xh, axis=0, keepdims=True)
            dxh = dn * g_ref[0]
            dx_ref[...] = (r * (dxh - xh * _mean_pair(dxh * xh))).astype(BF16)

        one(q_ref, gq_ref, dqf_ref, dqs_ref, dq_ref, dgq_ref, scale)
        one(k_ref, gk_ref, dkf_ref, dks_ref, dk_ref, dgk_ref, 1.0)
        dv_ref[...] = jnp.where(fox, dvf_ref[...], dvs_ref[...]).astype(BF16)

    gain = pl.BlockSpec((1, 1, LANES), lambda p, i: (p, 0, 0))
    tile = pl.BlockSpec((ts, LANES), lambda p, i: (i, p))
    fpart = pl.BlockSpec((ts, LANES), lambda p, i: (i, jnp.minimum(p, half - 1)))
    spart = pl.BlockSpec((ts, LANES), lambda p, i: (i, jnp.maximum(p - half, 0)))
    dgain = pl.BlockSpec((1, 8, LANES), lambda p, i: (p, 0, 0))
    out = jax.ShapeDtypeStruct((s, MIX), BF16)
    gshape = jax.ShapeDtypeStruct((npair, 8, LANES), F32)
    dq, dk, dv, dgq, dgk = pl.pallas_call(
        body, name=name, grid=(npair, s // ts),
        in_specs=[tile, pl.BlockSpec((ts, LANES), lambda p, i: (i, p + npair)), gain, gain,
                  fpart, spart, fpart, spart, fpart, spart],
        out_specs=[tile, tile, tile, dgain, dgain], out_shape=[out, out, out, gshape, gshape],
        compiler_params=_params("parallel", "arbitrary"),
    )(proj, proj, gq, gk, dqs[0], dqs[1], dks[0], dks[1], dvs[0], dvs[1])
    return dq, dk, dv, dgq[:, 0, :], dgk[:, 0, :]


def _tri(n, rel):
    a = lax.broadcasted_iota(jnp.int32, (n, n), 0)
    b = lax.broadcasted_iota(jnp.int32, (n, n), 1)
    return rel(a, b).astype(BF16)


def _fgate_fwd(logit, bias, name):
    nh, r, _ = logit.shape

    def body(x_ref, b_ref, o_ref):
        within = _tri(LANES, lambda a, b: a <= b)
        before = _tri(r, lambda a, b: b < a)
        for hh in range(nh):
            x = x_ref[hh] + b_ref[hh]
            lf = jnp.minimum(x, 0.0) - jnp.log1p(jnp.exp(-jnp.abs(x)))
            c = _split_dot(lf, within, 3)
            tot = jnp.broadcast_to(c[:, LANES - 1:LANES], (r, LANES))
            o_ref[hh] = c + _split_dot_left(before, tot, 3)

    return pl.pallas_call(
        body, name=name, out_shape=jax.ShapeDtypeStruct((nh, r, LANES), F32),
        in_specs=[pl.BlockSpec(memory_space=pltpu.VMEM), pl.BlockSpec(memory_space=pltpu.SMEM)],
        out_specs=pl.BlockSpec(memory_space=pltpu.VMEM),
    )(logit, bias)


def _fgate_bwd(logit, bias, dcum, name):
    nh, r, _ = logit.shape

    def body(x_ref, b_ref, d_ref, dx_ref, db_ref):
        within = _tri(LANES, lambda a, b: a >= b)
        after = _tri(r, lambda a, b: b > a)
        for hh in range(nh):
            x = x_ref[hh] + b_ref[hh]
            d = d_ref[hh]
            c = _split_dot(d, within, 3)
            tot = jnp.broadcast_to(c[:, 0:1], (r, LANES))
            dlf = c + _split_dot_left(after, tot, 3)
            dx = dlf * (1.0 / (1.0 + jnp.exp(x)))
            dx_ref[hh] = dx
            db_ref[hh:hh + 1, :] = jnp.broadcast_to(jnp.sum(dx, keepdims=True).reshape(1, 1), (1, LANES))

    return pl.pallas_call(
        body, name=name,
        out_shape=[jax.ShapeDtypeStruct((nh, r, LANES), F32), jax.ShapeDtypeStruct((nh, LANES), F32)],
        in_specs=[pl.BlockSpec(memory_space=pltpu.VMEM), pl.BlockSpec(memory_space=pltpu.SMEM),
                  pl.BlockSpec(memory_space=pltpu.VMEM)],
        out_specs=[pl.BlockSpec(memory_space=pltpu.VMEM), pl.BlockSpec(memory_space=pltpu.VMEM)],
    )(logit, bias, dcum)


def _pair_masks(x):
    lo = _low_lanes(x.shape)
    zero = jnp.zeros_like(x)
    return jnp.where(lo, x, zero), jnp.where(lo, zero, x)


def _pair_masks_t(x):
    top = _top_rows(x.shape)
    zero = jnp.zeros_like(x)
    return jnp.where(top, x, zero), jnp.where(top, zero, x)


def _stack_heads(x):
    return jnp.concatenate(_pair_masks(x), axis=0)


def _stack_heads_t(x):
    return jnp.concatenate(_pair_masks_t(x), axis=1)


def _pair_colsum_t(x):
    top = _top_rows(x.shape)
    return (jnp.sum(jnp.where(top, x, 0.0), axis=0, keepdims=True),
            jnp.sum(jnp.where(top, 0.0, x), axis=0, keepdims=True))


def _key_query_iotas(t):
    return lax.broadcasted_iota(jnp.int32, (t, t), 0), lax.broadcasted_iota(jnp.int32, (t, t), 1)


def _walk_blocks(i, step, descending, group=2):
    full = i // group
    left = i - full * group

    def run(first, count):
        sign = -1 if descending else 1
        step([(first + sign * n, False) for n in range(count)])

    def leftovers():
        start = (left - 1) if descending else full * group
        sign = -1 if descending else 1
        if group == 4:
            @pl.when(left >= 2)
            def _():
                run(start, 2)

            @pl.when(left % 2 == 1)
            def _():
                run(0 if descending else i - 1, 1)
        else:
            @pl.when(left == 1)
            def _():
                run(start, 1)

    def loop(g, carry):
        run((i - 1 - group * g) if descending else group * g, group)
        return carry

    if descending:
        opens = i >= group - 1
        pl.when(opens)(lambda: step([(i, True)] + [(i - n, False) for n in range(1, group)]))
        pl.when(jnp.logical_not(opens))(lambda: step([(i, True)]))
        after = jnp.where(opens, i + 1 - group, i)
        full = after // group
        left = after - full * group

        def loop_down(g, carry):
            run(after - 1 - group * g, group)
            return carry

        lax.fori_loop(0, full, loop_down, 0)
        leftovers()
        return
    total = i + 1
    groups = total // group
    rest = total - groups * group
    lax.fori_loop(0, jnp.where(rest == 0, groups - 1, groups), loop, 0)

    def closing(count):
        step([(i - count + 1 + n, n == count - 1) for n in range(count)])

    pl.when(rest == 0)(lambda: closing(group))
    if group == 4:
        pl.when(rest == 3)(lambda: run(i - 2, 2))
        pl.when(rest == 2)(lambda: closing(2))
        pl.when((rest == 1) | (rest == 3))(lambda: closing(1))
    else:
        pl.when(rest == 1)(lambda: closing(1))


def _attn_specs(s, tq, pair0):
    q_nat = pl.BlockSpec((tq, LANES), lambda p, i: (i, p + pair0))
    q_t = pl.BlockSpec((LANES, tq), lambda p, i: (p + pair0, i))
    k_nat = pl.BlockSpec((s, LANES), lambda p, i: (0, p + pair0))
    k_t = pl.BlockSpec((LANES, s), lambda p, i: (p + pair0, 0))
    return q_nat, q_t, k_nat, k_t


def _fox_fwd(qt, kh, vt, frow, fcol, name, tq=256):
    s = kh.shape[0]
    tq = _tile(s, tq)
    nq, half = s // tq, MIX // LANES // 2

    def body(qt_ref, k_ref, vt_ref, fr_ref, fc_ref, o_ref, lse_ref, m_s, l_s, acc_s):
        i = pl.program_id(1)
        qt_v = qt_ref[...]
        ft = fr_ref[0]
        key, qry = _key_query_iotas(tq)
        causal = key <= qry
        m_s[...] = jnp.full(m_s.shape, NEG, F32)
        l_s[...] = jnp.zeros_like(l_s)
        acc_s[...] = jnp.zeros_like(acc_s)

        top = _top_rows((LANES, tq))

        def step(blocks):
            rows = [pl.ds(pl.multiple_of(j * tq, tq), tq) for j, _ in blocks]
            zs = [_dot(_stack_heads(k_ref[r, :]), qt_v) for r in rows]
            m_cur, l_cur = [m_s[0], m_s[1]], [l_s[0], l_s[1]]
            acc = acc_s[...]
            for b, (_, masked) in enumerate(blocks):
                fk = fc_ref[0, rows[b], :]
                prs, alphas = [], []
                for hh in range(2):
                    sc = zs[b][hh * tq:(hh + 1) * tq] + (ft[hh:hh + 1, :] - fk[:, hh:hh + 1])
                    if masked:
                        sc = jnp.where(causal, sc, NEG)
                    m_new = jnp.maximum(m_cur[hh], jnp.max(sc, axis=0, keepdims=True))
                    alpha = jnp.exp(m_cur[hh] - m_new)
                    pr = jnp.exp(sc - m_new)
                    l_cur[hh] = alpha * l_cur[hh] + jnp.sum(pr, axis=0, keepdims=True)
                    m_cur[hh] = m_new
                    prs.append(pr.astype(BF16))
                    alphas.append(alpha)
                pv = _dot(_stack_heads_t(vt_ref[:, rows[b]]), jnp.concatenate(prs, axis=0))
                acc = jnp.where(top, alphas[0], alphas[1]) * acc + pv
            acc_s[...] = acc
            for hh in range(2):
                m_s[hh] = m_cur[hh]
                l_s[hh] = l_cur[hh]

        _walk_blocks(i, step, descending=False, group=4)
        o_ref[...] = (acc_s[...] / jnp.where(top, l_s[0], l_s[1])).T
        lse_ref[0, 0:1, :] = m_s[0] + jnp.log(l_s[0])
        lse_ref[0, 1:2, :] = m_s[1] + jnp.log(l_s[1])

    _, q_t, k_nat, k_t = _attn_specs(s, tq, 0)
    qstat = pl.BlockSpec((1, 2, tq), lambda p, i: (p, 0, i))
    return pl.pallas_call(
        body, name=name, grid=(half, nq),
        in_specs=[q_t, k_nat, k_t, qstat, pl.BlockSpec((1, s, 2), lambda p, i: (p, 0, 0))],
        out_specs=[pl.BlockSpec((tq, LANES), lambda p, i: (i, p)), qstat],
        out_shape=[jax.ShapeDtypeStruct((s, MIX // 2), F32), jax.ShapeDtypeStruct((half, 2, s), F32)],
        scratch_shapes=[pltpu.VMEM((2, 1, tq), F32), pltpu.VMEM((2, 1, tq), F32), pltpu.VMEM((LANES, tq), F32)],
        compiler_params=_params("parallel", "arbitrary"),
    )(qt, kh, vt, frow, fcol)


def _fox_bwd(qh, qt, kh, kt, vb, frow, fcol, lse, o, do, name, tq=256):
    s = kh.shape[0]
    tq = _tile(s, tq)
    nq, half = s // tq, MIX // LANES // 2

    def body(q_ref, qt_ref, k_ref, kt_ref, v_ref, fr_ref, fc_ref, lse_ref, o_ref, do_ref,
             dq_ref, dk_ref, dv_ref, dfk_ref, dfq_ref, dq_s, rs_s):
        i = pl.program_id(1)

        @pl.when(i == 0)
        def _():
            dk_ref[...] = jnp.zeros_like(dk_ref)
            dv_ref[...] = jnp.zeros_like(dv_ref)
            dfk_ref[...] = jnp.zeros_like(dfk_ref)

        q2 = _stack_heads(q_ref[...])
        qt_v = qt_ref[...]
        do_v = do_ref[...]
        do2 = _stack_heads(do_v.astype(BF16))
        dot_v = do_v.T.astype(BF16)
        dsum = _pair_colsum_t((do_v * o_ref[...]).T)
        ft, ls = fr_ref[0], lse_ref[0]
        key, qry = _key_query_iotas(tq)
        causal = key <= qry
        lane = lax.broadcasted_iota(jnp.int32, (2 * tq, LANES), 0) // tq
        pick2 = (lax.broadcasted_iota(jnp.int32, (2 * tq, LANES), 1) == lane).astype(BF16)
        q2_pick = jnp.concatenate([q2, pick2], axis=1)
        dq_s[...] = jnp.zeros_like(dq_s)
        rs_s[...] = jnp.zeros_like(rs_s)

        def step(blocks):
            rows = [pl.ds(pl.multiple_of(j * tq, tq), tq) for j, _ in blocks]
            zs = [_dot(_stack_heads(k_ref[r, :]), qt_v) for r in rows]
            dps = [_dot(_stack_heads(v_ref[r, :]), dot_v) for r in rows]
            rs = [rs_s[0], rs_s[1]]
            dq = None
            for b, (_, masked) in enumerate(blocks):
                fk = fc_ref[0, rows[b], :]
                prs, dss = [], []
                for hh in range(2):
                    blk = slice(hh * tq, (hh + 1) * tq)
                    sc = zs[b][blk] + (ft[hh:hh + 1, :] - fk[:, hh:hh + 1])
                    pr = jnp.exp(sc - ls[hh:hh + 1, :])
                    if masked:
                        pr = jnp.where(causal, pr, 0.0)
                    dsb = (pr * (dps[b][blk] - dsum[hh])).astype(BF16)
                    rs[hh] = rs[hh] + jnp.sum(dsb.astype(F32), axis=0, keepdims=True)
                    prs.append(pr.astype(BF16))
                    dss.append(dsb)
                dv_ref[rows[b], :] += _dot(jnp.concatenate(prs, axis=1), do2)
                both = _dot(jnp.concatenate(dss, axis=1), q2_pick)
                dk_ref[rows[b], :] += both[:, :LANES]
                dfk_ref[0, rows[b], :] -= both[:, LANES:]
                term = _dot(_stack_heads_t(kt_ref[:, rows[b]]), jnp.concatenate(dss, axis=0))
                dq = term if dq is None else dq + term
            rs_s[0], rs_s[1] = rs
            dq_s[...] += dq

        _walk_blocks(i, step, descending=False, group=4)
        dq_ref[...] = dq_s[...].T
        dfq_ref[0, 0:1, :] = rs_s[0]
        dfq_ref[0, 1:2, :] = rs_s[1]

    q_nat, q_t, k_nat, k_t = _attn_specs(s, tq, 0)
    qstat = pl.BlockSpec((1, 2, tq), lambda p, i: (p, 0, i))
    otile = pl.BlockSpec((tq, LANES), lambda p, i: (i, p))
    oresident = pl.BlockSpec((s, LANES), lambda p, i: (0, p))
    out = jax.ShapeDtypeStruct((s, MIX // 2), F32)
    return pl.pallas_call(
        body, name=name, grid=(half, nq),
        in_specs=[q_nat, q_t, k_nat, k_t, k_nat, qstat, pl.BlockSpec((1, s, 2), lambda p, i: (p, 0, 0)), qstat,
                  otile, q_nat],
        out_specs=[otile, oresident, oresident, pl.BlockSpec((1, s, LANES), lambda p, i: (p, 0, 0)), qstat],
        out_shape=[out, out, out, jax.ShapeDtypeStruct((half, s, LANES), F32),
                   jax.ShapeDtypeStruct((half, 2, s), F32)],
        scratch_shapes=[pltpu.VMEM((LANES, tq), F32), pltpu.VMEM((2, 1, tq), F32)],
        compiler_params=_params("parallel", "arbitrary"),
    )(qh, qt, kh, kt, vb, frow, fcol, lse, o, do)


def _log_sig_pair(z):
    zc = jnp.maximum(z, -80.0)
    lb = -jnp.log(1.0 + jnp.exp(-zc))
    return lb, lb - zc


def _sb_fwd(qt, kh, vt, name, tq=256, gather=()):
    s = kh.shape[0]
    tq = _tile(s, tq)
    nq, half = s // tq, MIX // LANES // 2
    ng = len(gather)

    def body(*refs):
        qt_ref, k_ref, vt_ref = refs[:3]
        o_ref, tot_ref = refs[3 + ng:5 + ng]
        c_s, acc_s = refs[5 + 2 * ng:7 + 2 * ng]
        p, i = pl.program_id(0), pl.program_id(1)
        if ng:
            start, relay, finish = _gather_stages(refs[3:3 + ng], refs[5 + ng:5 + 2 * ng], *refs[7 + 2 * ng:])
            pl.when((p == 0) & (i == 0))(start)
            pl.when((p == half - 1) & (i == 0))(relay)
        qt_v = qt_ref[...]
        key, qry = _key_query_iotas(tq)
        strict = key < qry
        later = _tri(tq, lambda a, b: b > a)
        c_s[...] = jnp.zeros_like(c_s)
        acc_s[...] = jnp.zeros_like(acc_s)

        def step(blocks):
            rows = [pl.ds(pl.multiple_of(j * tq, tq), tq) for j, _ in blocks]
            zs = [_dot(_stack_heads(k_ref[r, :]), qt_v) for r in rows]
            lbs, loms, afters = [], [], []
            for b, (_, masked) in enumerate(blocks):
                for hh in range(2):
                    lb, lom = _log_sig_pair(zs[b][hh * tq:(hh + 1) * tq])
                    if masked:
                        lom = jnp.where(strict, lom, 0.0)
                    lbs.append(lb)
                    loms.append(lom)
                afters.append(_split_dot_left(later, jnp.concatenate(loms[2 * b:2 * b + 2], axis=1), 2))
            carry = [c_s[0], c_s[1]]
            pv = None
            for b, (_, masked) in enumerate(blocks):
                ws = []
                for hh in range(2):
                    n = 2 * b + hh
                    w = jnp.exp(lbs[n] + afters[b][:, hh * tq:(hh + 1) * tq] + carry[hh])
                    if masked:
                        w = jnp.where(strict, w, 0.0)
                    ws.append(w.astype(BF16))
                    carry[hh] = carry[hh] + jnp.sum(loms[n], axis=0, keepdims=True)
                term = _dot(_stack_heads_t(vt_ref[:, rows[b]]), jnp.concatenate(ws, axis=0))
                pv = term if pv is None else pv + term
            c_s[0], c_s[1] = carry
            acc_s[...] += pv

        _walk_blocks(i, step, descending=True, group=4)
        o_ref[...] = acc_s[...].T
        tot_ref[0, 0:1, :] = c_s[0]
        tot_ref[0, 1:2, :] = c_s[1]
        if ng:
            pl.when((p == half - 1) & (i == nq - 1))(finish)

    _, q_t, k_nat, k_t = _attn_specs(s, tq, half)
    qstat = pl.BlockSpec((1, 2, tq), lambda p, i: (p, 0, i))
    return pl.pallas_call(
        body, name=name, grid=(half, nq),
        in_specs=[q_t, k_nat, k_t] + [ANY] * ng,
        out_specs=[pl.BlockSpec((tq, LANES), lambda p, i: (i, p)), qstat] + [ANY] * ng,
        out_shape=[jax.ShapeDtypeStruct((s, MIX // 2), F32), jax.ShapeDtypeStruct((half, 2, s), F32)]
        + [jax.ShapeDtypeStruct((N_DEV,) + x.shape, x.dtype) for x in gather],
        scratch_shapes=[pltpu.VMEM((2, 1, tq), F32), pltpu.VMEM((LANES, tq), F32)]
        + (_gather_scratch(ng) if ng else []),
        compiler_params=_params("arbitrary", "arbitrary") if ng else _params("parallel", "arbitrary"),
    )(qt, kh, vt, *gather)


def _sb_bwd(qh, qt, kh, kt, vb, tot, do, name, tq=256, exchange=()):
    s = kh.shape[0]
    tq = _tile(s, tq)
    nq, half = s // tq, MIX // LANES // 2
    nx = len(exchange)

    def body(*refs):
        q_ref, qt_ref, k_ref, kt_ref, v_ref, tot_ref, do_ref = refs[:7]
        dq_ref, dk_ref, dv_ref = refs[7 + nx:10 + nx]
        rem_s, pg_s, dq_s = refs[10 + 2 * nx:13 + 2 * nx]
        p, i = pl.program_id(0), pl.program_id(1)
        if nx:
            start, finish = _chip_exchange_stages(refs[7:7 + nx], refs[10 + nx:10 + 2 * nx], *refs[13 + 2 * nx:])
            pl.when((p == 0) & (i == 0))(start)

        @pl.when(i == 0)
        def _():
            dk_ref[...] = jnp.zeros_like(dk_ref)
            dv_ref[...] = jnp.zeros_like(dv_ref)

        q2 = _stack_heads(q_ref[...])
        qt_v = qt_ref[...]
        do_v = do_ref[...]
        do2 = _stack_heads(do_v.astype(BF16))
        dot_v = do_v.T.astype(BF16)
        key, qry = _key_query_iotas(tq)
        strict = key < qry
        upto = _tri(tq, lambda a, b: b <= a)
        before = _tri(tq, lambda a, b: b < a)
        tv = tot_ref[0]
        rem_s[0] = tv[0:1, :]
        rem_s[1] = tv[1:2, :]
        pg_s[...] = jnp.zeros_like(pg_s)
        dq_s[...] = jnp.zeros_like(dq_s)

        def step(blocks):
            nb = len(blocks)
            rows = [pl.ds(pl.multiple_of(j * tq, tq), tq) for j, _ in blocks]
            zs = [_dot(_stack_heads(k_ref[r, :]), qt_v) for r in rows]
            dws = [_dot(_stack_heads(v_ref[r, :]), dot_v) for r in rows]
            lbs, loms, prefixes = [], [], []
            for b, (_, masked) in enumerate(blocks):
                for hh in range(2):
                    lb, lom = _log_sig_pair(zs[b][hh * tq:(hh + 1) * tq])
                    if masked:
                        lom = jnp.where(strict, lom, 0.0)
                    lbs.append(lb)
                    loms.append(lom)
                prefixes.append(_split_dot_left(upto, jnp.concatenate(loms[2 * b:2 * b + 2], axis=1), 2))
            rem = [rem_s[0], rem_s[1]]
            ws, gs, gpres = [], [], []
            for b, (_, masked) in enumerate(blocks):
                for hh in range(2):
                    n = 2 * b + hh
                    blk = slice(hh * tq, (hh + 1) * tq)
                    w = jnp.exp(lbs[n] + (rem[hh] - prefixes[b][:, blk]))
                    if masked:
                        w = jnp.where(strict, w, 0.0)
                    gs.append(dws[b][blk] * w)
                    ws.append(w.astype(BF16))
                    rem[hh] = rem[hh] - jnp.sum(loms[n], axis=0, keepdims=True)
                gpres.append(_dot(before, jnp.concatenate(gs[2 * b:2 * b + 2], axis=1).astype(BF16)))
                dv_ref[rows[b], :] += _dot(jnp.concatenate(ws[2 * b:2 * b + 2], axis=1), do2)
            rem_s[0], rem_s[1] = rem
            pg = [pg_s[0], pg_s[1]]
            dq = None
            for b, (_, masked) in enumerate(blocks):
                dzs = []
                for hh in range(2):
                    n = 2 * b + hh
                    g = gs[n]
                    dz = g - jnp.exp(lbs[n]) * (g + (pg[hh] + gpres[b][:, hh * tq:(hh + 1) * tq]))
                    if masked:
                        dz = jnp.where(strict, dz, 0.0)
                    dzs.append(dz.astype(BF16))
                    pg[hh] = pg[hh] + jnp.sum(g, axis=0, keepdims=True)
                dk_ref[rows[b], :] += _dot(jnp.concatenate(dzs, axis=1), q2)
                term = _dot(_stack_heads_t(kt_ref[:, rows[b]]), jnp.concatenate(dzs, axis=0))
                dq = term if dq is None else dq + term
            pg_s[0], pg_s[1] = pg
            dq_s[...] += dq

        _walk_blocks(i, step, descending=False, group=2)
        dq_ref[...] = dq_s[...].T
        if nx:
            pl.when((p == half - 1) & (i == nq - 1))(finish)

    q_nat, q_t, k_nat, k_t = _attn_specs(s, tq, half)
    qstat = pl.BlockSpec((1, 2, tq), lambda p, i: (p, 0, i))
    otile = pl.BlockSpec((tq, LANES), lambda p, i: (i, p))
    oresident = pl.BlockSpec((s, LANES), lambda p, i: (0, p))
    out = jax.ShapeDtypeStruct((s, MIX // 2), F32)
    return pl.pallas_call(
        body, name=name, grid=(half, nq),
        in_specs=[q_nat, q_t, k_nat, k_t, k_nat, qstat, q_nat] + [ANY] * nx,
        out_specs=[otile, oresident, oresident] + [ANY] * nx,
        out_shape=[out, out, out] + [jax.ShapeDtypeStruct(x.shape, x.dtype) for x in exchange],
        scratch_shapes=[pltpu.VMEM((2, 1, tq), F32), pltpu.VMEM((2, 1, tq), F32), pltpu.VMEM((LANES, tq), F32)]
        + (_chip_exchange_scratch(nx) if nx else []),
        compiler_params=_params("arbitrary", "arbitrary") if nx else _params("parallel", "arbitrary"),
    )(qh, qt, kh, kt, vb, tot, do, *exchange)


def _loss_head(y, target, name, ts=1024):
    s, d = y.shape
    ts = _tile(s, ts, 8)
    nt = s // ts

    def body(y_ref, t_ref, dy_ref, l_ref, acc):
        i = pl.program_id(0)
        err = y_ref[...] - t_ref[...]
        dy_ref[...] = err * (1.0 / d)

        @pl.when(i == 0)
        def _():
            acc[...] = jnp.zeros_like(acc)

        acc[...] += jnp.sum(err * err, axis=0, keepdims=True)

        @pl.when(i == nt - 1)
        def _():
            tot = jnp.sum(acc[...], keepdims=True).reshape(1, 1) * (0.5 / d)
            l_ref[...] = jnp.broadcast_to(tot, l_ref.shape)

    row = pl.BlockSpec((ts, d), lambda i: (i, 0))
    dy, l = pl.pallas_call(
        body, name=name, grid=(nt,), in_specs=[row, row],
        out_specs=[row, pl.BlockSpec((8, LANES), lambda i: (0, 0))],
        out_shape=[jax.ShapeDtypeStruct((s, d), F32), jax.ShapeDtypeStruct((8, LANES), F32)],
        scratch_shapes=[pltpu.VMEM((1, d), F32)],
        compiler_params=_params("arbitrary"),
    )(y, target)
    return l[0, 0], dy


def _coords():
    return lax.axis_index("x"), lax.axis_index("y"), lax.axis_index("c")


def _other_chips(xi, yi):
    return [(1 - xi, yi), (xi, 1 - yi), (1 - xi, 1 - yi)]


def _gather_stages(x_refs, out_refs, send_sems, recv_sems, local_sems):
    n = len(x_refs)
    xi, yi, ci = _coords()
    me, sibling = (xi, yi, ci), (xi, yi, 1 - ci)
    chips = _other_chips(xi, yi)

    def slot(a, px, py, pc):
        return out_refs[a].at[4 * px + 2 * py + pc]

    def copy(a, k, block, to, src=None):
        return pltpu.make_async_remote_copy(
            src_ref=slot(a, *block) if src is None else src, dst_ref=slot(a, *block),
            send_sem=send_sems.at[a, k], recv_sem=recv_sems.at[a, k], device_id=to, device_id_type=MESH)

    def own(a):
        return pltpu.make_async_copy(x_refs[a], slot(a, *me), local_sems.at[a])

    def first(a):
        return [copy(a, 0, me, sibling, src=x_refs[a])] + [
            copy(a, 1 + j, me, (*chip, ci), src=x_refs[a]) for j, chip in enumerate(chips)]

    def passed(a, j):
        return copy(a, 4 + j, (*chips[j], ci), sibling)

    def start():
        for a in range(n):
            own(a).start()
        for a in range(n):
            for cp in first(a):
                cp.start()

    def relay():
        for j, chip in enumerate(chips):
            for a in range(n):
                copy(a, 1 + j, (*chip, ci), me).wait_recv()
                passed(a, j).start()

    def finish():
        for a in range(n):
            copy(a, 0, sibling, me).wait_recv()
            for j, chip in enumerate(chips):
                copy(a, 4 + j, (*chip, 1 - ci), me).wait_recv()
        for a in range(n):
            for cp in first(a) + [passed(a, j) for j in range(len(chips))]:
                cp.wait_send()
            own(a).wait()

    return start, relay, finish


def _gather_scratch(n):
    return [pltpu.SemaphoreType.DMA((n, 7)), pltpu.SemaphoreType.DMA((n, 7)), pltpu.SemaphoreType.DMA((n,))]


def _all_gather(xs, name):
    n = len(xs)

    def body(*refs):
        start, relay, finish = _gather_stages(refs[:n], refs[n:2 * n], *refs[2 * n:])
        start()
        relay()
        finish()

    return pl.pallas_call(
        body, name=name, out_shape=[jax.ShapeDtypeStruct((N_DEV,) + x.shape, x.dtype) for x in xs],
        in_specs=[ANY] * n, out_specs=[ANY] * n, scratch_shapes=_gather_scratch(n),
    )(*xs)


def _sibling_exchange(gs, name):
    n = len(gs)

    def body(*refs):
        g_refs, recv_refs = refs[:n], refs[n:2 * n]
        send_sems, recv_sems = refs[2 * n:]
        xi, yi, ci = _coords()
        cps = [pltpu.make_async_remote_copy(
            src_ref=g_refs[a].at[2 * chip + (1 - ci)], dst_ref=recv_refs[a].at[chip],
            send_sem=send_sems.at[a, chip], recv_sem=recv_sems.at[a, chip],
            device_id=(xi, yi, 1 - ci), device_id_type=MESH) for a in range(n) for chip in range(N_CHIP)]
        for cp in cps:
            cp.start()
        for cp in cps:
            cp.wait()

    return pl.pallas_call(
        body, name=name, out_shape=[jax.ShapeDtypeStruct((N_CHIP,) + g.shape[1:], g.dtype) for g in gs],
        in_specs=[ANY] * n, out_specs=[ANY] * n,
        scratch_shapes=[pltpu.SemaphoreType.DMA((n, N_CHIP)), pltpu.SemaphoreType.DMA((n, N_CHIP))],
    )(*gs)


def _pair_add(g, recv, ids, name, tr=512):
    _, r, c = g.shape
    tr = _tile(r, tr, 16)

    def body(ids_ref, g_ref, r_ref, p_ref, own_ref):
        kk = pl.program_id(1)
        tot = g_ref[0].astype(F32) + r_ref[0].astype(F32)
        p_ref[0] = tot.astype(BF16)

        @pl.when(kk == ids_ref[1])
        def _():
            own_ref[...] = tot

    grid_spec = pltpu.PrefetchScalarGridSpec(
        num_scalar_prefetch=1, grid=(r // tr, N_CHIP),
        in_specs=[pl.BlockSpec((1, tr, c), lambda i, kk, ids: (2 * kk + ids[0], i, 0)),
                  pl.BlockSpec((1, tr, c), lambda i, kk, ids: (kk, i, 0))],
        out_specs=[pl.BlockSpec((1, tr, c), lambda i, kk, ids: (kk, i, 0)),
                   pl.BlockSpec((tr, c), lambda i, kk, ids: (i, 0))])
    return pl.pallas_call(
        body, name=name, grid_spec=grid_spec,
        out_shape=[jax.ShapeDtypeStruct((N_CHIP, r, c), BF16), jax.ShapeDtypeStruct((r, c), F32)],
        compiler_params=_params("parallel", "arbitrary"),
    )(ids, g, recv)


def _chip_exchange_stages(p_refs, recv_refs, send_sems, recv_sems):
    n = len(p_refs)
    xi, yi, ci = _coords()
    mine = 2 * xi + yi
    chips = _other_chips(xi, yi)

    def copy(a, k, cx, cy):
        return pltpu.make_async_remote_copy(
            src_ref=p_refs[a].at[2 * cx + cy], dst_ref=recv_refs[a].at[mine],
            send_sem=send_sems.at[a, k], recv_sem=recv_sems.at[a, k],
            device_id=(cx, cy, ci), device_id_type=MESH)

    def landed(a, k, cx, cy):
        return pltpu.make_async_remote_copy(
            src_ref=p_refs[a].at[mine], dst_ref=recv_refs[a].at[2 * cx + cy],
            send_sem=send_sems.at[a, k], recv_sem=recv_sems.at[a, k],
            device_id=(cx, cy, ci), device_id_type=MESH)

    def start():
        for a in range(n):
            for k, (cx, cy) in enumerate(chips):
                copy(a, k, cx, cy).start()

    def finish():
        for a in range(n):
            for k, (cx, cy) in enumerate(chips):
                landed(a, k, cx, cy).wait_recv()
        for a in range(n):
            for k, (cx, cy) in enumerate(chips):
                copy(a, k, cx, cy).wait_send()

    return start, finish


def _chip_exchange_scratch(n):
    return [pltpu.SemaphoreType.DMA((n, 3)), pltpu.SemaphoreType.DMA((n, 3))]


def _chip_exchange(ps, name):
    n = len(ps)

    def body(*refs):
        start, finish = _chip_exchange_stages(refs[:n], refs[n:2 * n], *refs[2 * n:])
        start()
        finish()

    return pl.pallas_call(
        body, name=name, out_shape=[jax.ShapeDtypeStruct(p.shape, p.dtype) for p in ps],
        in_specs=[ANY] * n, out_specs=[ANY] * n, scratch_shapes=_chip_exchange_scratch(n),
    )(*ps)


def _adamw_math(w, g, m, v):
    m = ADAM_B1 * m + (1.0 - ADAM_B1) * g
    v = ADAM_B2 * v + (1.0 - ADAM_B2) * (g * g)
    m_hat = m / (1.0 - ADAM_B1 ** ADAM_STEP)
    v_hat = v / (1.0 - ADAM_B2 ** ADAM_STEP)
    delta = -ADAM_LR * (m_hat / (jnp.sqrt(v_hat) + ADAM_EPS) + ADAM_WD * w)
    return delta, m, v


def _adamw_reduce(own, recv, ids, w, m, v, name, tr=512):
    r, c = w.shape
    tr = _tile(r, tr, 16)

    def body(ids_ref, own_ref, recv_ref, w_ref, m_ref, v_ref, g_ref, d_ref, mo_ref, vo_ref):
        mine = ids_ref[1]
        g = None
        for kk in range(N_CHIP):
            term = jnp.where(mine == kk, own_ref[...], recv_ref[kk].astype(F32))
            g = term if g is None else g + term
        delta, m_new, v_new = _adamw_math(w_ref[...], g, m_ref[...], v_ref[...])
        g_ref[...] = g
        d_ref[...] = delta
        mo_ref[...] = m_new
        vo_ref[...] = v_new

    row = pl.BlockSpec((tr, c), lambda i, ids: (i, 0))
    grid_spec = pltpu.PrefetchScalarGridSpec(
        num_scalar_prefetch=1, grid=(r // tr,),
        in_specs=[row, pl.BlockSpec((N_CHIP, tr, c), lambda i, ids: (0, i, 0)), row, row, row],
        out_specs=[row, row, row, row])
    out = jax.ShapeDtypeStruct((r, c), F32)
    return pl.pallas_call(
        body, name=name, grid_spec=grid_spec, out_shape=[out, out, out, out],
        compiler_params=_params("parallel"),
    )(ids, own, recv, w, m, v)


def _sum_sources(a, name):
    n, r, c = a.shape

    def body(a_ref, o_ref):
        tot = a_ref[0]
        for kk in range(1, n):
            tot = tot + a_ref[kk]
        o_ref[...] = tot

    return pl.pallas_call(
        body, name=name, out_shape=jax.ShapeDtypeStruct((r, c), F32),
        in_specs=[pl.BlockSpec(memory_space=pltpu.VMEM)], out_specs=pl.BlockSpec(memory_space=pltpu.VMEM),
    )(a)


def _adamw_small(w, g, m, v, name):
    def body(w_ref, g_ref, m_ref, v_ref, d_ref, mo_ref, vo_ref):
        delta, m_new, v_new = _adamw_math(w_ref[...], g_ref[...], m_ref[...], v_ref[...])
        d_ref[...] = delta
        mo_ref[...] = m_new
        vo_ref[...] = v_new

    vm = pl.BlockSpec(memory_space=pltpu.VMEM)
    out = jax.ShapeDtypeStruct(w.shape, F32)
    return pl.pallas_call(body, name=name, out_shape=[out, out, out], in_specs=[vm] * 4, out_specs=[vm] * 3)(w, g, m, v)


def _pack(parts, width, row_mult):
    flat = jnp.concatenate([p.reshape(-1) for p in parts])
    rows = -(-flat.shape[0] // width)
    rows = -(-rows // row_mult) * row_mult
    return jnp.pad(flat, (0, rows * width - flat.shape[0])).reshape(rows, width)


def _unpack(flat, shapes):
    out, off = [], 0
    lead = flat.shape[:-1]
    for shp in shapes:
        n = 1
        for dd in shp:
            n *= dd
        out.append(flat[..., off:off + n].reshape(lead + tuple(shp)))
        off += n
    return out


def _rows2d(w):
    return w.reshape(w.shape[0] * w.shape[1], w.shape[2])


def _cols_to_dev(g):
    l, k, n = g.shape
    return g.reshape(l * k, N_DEV, n // N_DEV).transpose(1, 0, 2)


def _rows_to_dev(g):
    l, k, n = g.shape
    rs = k // N_DEV
    return g.reshape(l, N_DEV, rs, n).transpose(1, 0, 2, 3).reshape(N_DEV, l * rs, n)


def _dev_to_cols(a, l):
    _, lk, cs = a.shape
    return a.transpose(1, 0, 2).reshape(l, lk // l, N_DEV * cs)


def _dev_to_rows(a, l):
    _, lr, n = a.shape
    rs = lr // l
    return a.reshape(N_DEV, l, rs, n).transpose(1, 0, 2, 3).reshape(l, N_DEV * rs, n)


def kernel(x, attn_norm, attn_w_in, attn_f_bias, fox_q_gain, fox_k_gain, sb_q_gain, sb_k_gain, attn_w_out, conv_norm, conv_w_in, conv_kernel, conv_w_out, ffn_norm, ffn_w_up, ffn_conv, ffn_w_down, loss_target, m_attn_norm, m_attn_w_in, m_attn_f_bias, m_fox_q_gain, m_fox_k_gain, m_sb_q_gain, m_sb_k_gain, m_attn_w_out, m_conv_norm, m_conv_w_in, m_conv_kernel, m_conv_w_out, m_ffn_norm, m_ffn_w_up, m_ffn_conv, m_ffn_w_down, v_attn_norm, v_attn_w_in, v_attn_f_bias, v_fox_q_gain, v_fox_k_gain, v_sb_q_gain, v_sb_k_gain, v_attn_w_out, v_conv_norm, v_conv_w_in, v_conv_kernel, v_conv_w_out, v_ffn_norm, v_ffn_w_up, v_ffn_conv, v_ffn_w_down):
    s = x.shape[1]
    n_attn, n_conv, depth = attn_w_in.shape[0], conv_w_in.shape[0], ffn_w_up.shape[0]
    xi, yi, ci = _coords()
    me = 4 * xi + 2 * yi + ci
    ids = jnp.stack([ci, 2 * xi + yi]).astype(jnp.int32)

    big = [attn_w_in, attn_w_out, conv_w_in, conv_w_out, ffn_w_up, ffn_w_down]
    big_m = [m_attn_w_in, m_attn_w_out, m_conv_w_in, m_conv_w_out, m_ffn_w_up, m_ffn_w_down]
    big_v = [v_attn_w_in, v_attn_w_out, v_conv_w_in, v_conv_w_out, v_ffn_w_up, v_ffn_w_down]
    big_names = ["attn_w_in", "attn_w_out", "conv_w_in", "conv_w_out", "ffn_w_up", "ffn_w_down"]
    small_sh = [conv_norm, conv_kernel, ffn_conv]
    small_sh_shapes = [w.shape for w in small_sh]
    rep = [attn_norm, attn_f_bias, fox_q_gain, fox_k_gain, sb_q_gain, sb_k_gain, ffn_norm]
    rep_shapes = [w.shape for w in rep]

    small_pack = _pack(small_sh, LANES, 8)
    shards_bf16 = [_rows2d(w).astype(BF16) for w in big]
    early = _all_gather([attn_w_in[0].astype(BF16), small_pack], "gather_first")
    first_w_in = jnp.pad(_dev_to_cols(early[0], 1)[0], ((0, 0), (0, ATTN_IN_PAD - ATTN_IN)))
    cn, ckern, fconv = _unpack(early[1].reshape(N_DEV, -1), small_sh_shapes)
    conv_norm_f = cn.transpose(1, 0, 2).reshape(n_conv, D_MODEL)
    conv_kernel_f = ckern.transpose(1, 2, 0, 3).reshape(n_conv, 3, D_MODEL)
    ffn_conv_f = fconv.transpose(1, 2, 0, 3).reshape(depth, 3, 2 * D_FF)

    def pair_gain(fox_g, sb_g):
        f2 = jnp.concatenate([fox_g, fox_g])
        s2 = jnp.concatenate([sb_g, sb_g])
        return jnp.concatenate([jnp.tile(f2[None], (4, 1)), jnp.tile(s2[None], (4, 1))])[:, None, :]

    h = x[0]
    saved = []
    for layer in range(depth):
        i = layer // 2
        tag = "l%d" % layer
        rec = {"h_in": h}
        if layer % 2 == 0:
            xn = xn_next if layer else _rms_fwd(h, attn_norm[i], tag + "_attn_rms")
            proj = _matmul(xn, first_w_in if layer == 0 else a_w_in[i], tag + "_attn_in", tn=640)
            gq, gk = pair_gain(fox_q_gain[i], sb_q_gain[i]), pair_gain(fox_k_gain[i], sb_k_gain[i])
            qh, kh, vb, qt, kt, vt = _qkv_prep(proj, gq, gk, tag + "_qkv_prep")
            logit = proj[:, 3 * MIX:3 * MIX + H_FOX].T.reshape(H_FOX, s // LANES, LANES)
            cum = _fgate_fwd(logit, attn_f_bias[i], tag + "_fgate")
            frow = cum.reshape(H_FOX // 2, 2, s)
            fcol = frow.transpose(0, 2, 1)
            o_fox, lse = _fox_fwd(qt, kh, vt, frow, fcol, tag + "_fox_fwd")
            if layer == 0:
                o_sb, tot, *gathered = _sb_fwd(qt, kh, vt, tag + "_sb_fwd", gather=shards_bf16)
                a_w_in = _dev_to_cols(gathered[0], n_attn)
                a_w_in = jnp.pad(a_w_in, ((0, 0), (0, 0), (0, ATTN_IN_PAD - ATTN_IN)))
                a_w_out = _dev_to_rows(gathered[1], n_attn)
                c_w_in = _dev_to_cols(gathered[2], n_conv)
                c_w_out = _dev_to_rows(gathered[3], n_conv)
                f_w_up = _dev_to_cols(gathered[4], depth)
                f_w_down = _dev_to_rows(gathered[5], depth)
            else:
                o_sb, tot = _sb_fwd(qt, kh, vt, tag + "_sb_fwd")
            o = jnp.concatenate([o_fox, o_sb], axis=1)
            h, xn2 = _matmul_rows(o, a_w_out[i], tag + "_attn_out", add=h, norm_gain=ffn_norm[layer], tm=1024)
            rec.update(xn=xn, proj=proj, gq=gq, gk=gk, qh=qh, kh=kh, vb=vb, qt=qt, kt=kt, logit=logit, frow=frow,
                       fcol=fcol, o_fox=o_fox, lse=lse, tot=tot, o=o)
        else:
            xn = xn_next
            proj = _matmul(xn, c_w_in[i], tag + "_conv_in", tn=1024)
            y = _sconv_fwd(proj, conv_kernel_f[i], tag + "_sconv_fwd")
            h, xn2 = _matmul_rows(y, c_w_out[i], tag + "_conv_out", add=h, norm_gain=ffn_norm[layer], tm=1024)
            rec.update(xn=xn, proj=proj, y=y)
        rec["h_mid"] = h
        up = _matmul(xn2, f_w_up[layer], tag + "_ffn_up", tn=1408)
        act = _ffn_act_fwd(up, ffn_conv_f[layer], tag + "_ffn_act")
        if layer + 1 < depth:
            nxt = layer + 1
            gain = attn_norm[nxt // 2] if nxt % 2 == 0 else conv_norm_f[nxt // 2]
            h, xn_next = _matmul_rows(act, f_w_down[layer], tag + "_ffn_down", add=h, norm_gain=gain, tk=2816)
        else:
            h = _matmul(act, f_w_down[layer], tag + "_ffn_down", add=h, tn=1024, tk=2816)
        rec.update(xn2=xn2, up=up, act=act)
        saved.append(rec)

    loss_local, dh = _loss_head(h, loss_target[0], "loss_head")
    loss = lax.psum(loss_local, ("x", "y", "c"))

    g_attn_norm, g_attn_w_in, g_f_bias = [None] * n_attn, [None] * n_attn, [None] * n_attn
    g_fq, g_fk, g_sq, g_sk, g_attn_w_out = ([None] * n_attn for _ in range(5))
    g_conv_norm, g_conv_w_in, g_conv_kernel, g_conv_w_out = ([None] * n_conv for _ in range(4))
    g_ffn_norm, g_ffn_w_up, g_ffn_conv, g_ffn_w_down = ([None] * depth for _ in range(4))
    everything = slice(0, None)
    early_layers = {nm: (slice(1, None) if nm == "attn_w_in" else everything) for nm in big_names}
    late_layers = {"attn_w_in": slice(0, 1)}

    def slabs_for_devices(layers):
        stacks = {"attn_w_in": (g_attn_w_in, _cols_to_dev), "attn_w_out": (g_attn_w_out, _rows_to_dev),
                  "conv_w_in": (g_conv_w_in, _cols_to_dev), "conv_w_out": (g_conv_w_out, _rows_to_dev),
                  "ffn_w_up": (g_ffn_w_up, _cols_to_dev), "ffn_w_down": (g_ffn_w_down, _rows_to_dev)}
        return [stacks[nm][1](jnp.stack(stacks[nm][0][layers[nm]]).astype(BF16)) for nm in big_names if nm in layers]

    for layer in reversed(range(depth)):
        i = layer // 2
        tag = "l%d" % layer
        rec = saved[layer]
        da = _matmul(dh, f_w_down[layer].T, tag + "_ffn_down_dx", tn=1408)
        g_ffn_w_down[layer] = _matmul_tn(rec["act"], dh, tag + "_ffn_down_dw", tm=1408, tn=1024)
        dup, dwg, dwv = _ffn_act_bwd(rec["up"], ffn_conv_f[layer], da, tag + "_ffn_act_bwd")
        g_ffn_conv[layer] = jnp.concatenate([dwg, dwv], axis=1)
        g_ffn_w_up[layer] = _matmul_tn(rec["xn2"], dup, tag + "_ffn_up_dw", tn=1408)
        dh, g_ffn_norm[layer] = _matmul_rows(dup, f_w_up[layer].T, tag + "_ffn_up_dx", tk=2816,
                                             rms_bwd=(rec["h_mid"], ffn_norm[layer], dh))
        if layer % 2 == 0:
            do = _matmul(dh, a_w_out[i].T, tag + "_attn_out_dx", tn=1024)
            g_attn_w_out[i] = _matmul_tn(rec["o"], dh, tag + "_attn_out_dw", tn=1024)
            if layer == 0:
                early_g = slabs_for_devices(early_layers)
                early_pairs = [_pair_add(g, r, ids, "reduce_pair_add_" + nm)
                               for g, r, nm in zip(early_g, _sibling_exchange(early_g, "reduce_sibling"), big_names)]
            dq_f, dk_f, dv_f, dfk, dfq = _fox_bwd(rec["qh"], rec["qt"], rec["kh"], rec["kt"], rec["vb"], rec["frow"],
                                                  rec["fcol"], rec["lse"], rec["o_fox"], do, tag + "_fox_bwd")
            dq_s, dk_s, dv_s, *from_chips = _sb_bwd(
                rec["qh"], rec["qt"], rec["kh"], rec["kt"], rec["vb"], rec["tot"], do, tag + "_sb_bwd",
                exchange=[pr[0] for pr in early_pairs] if layer == 0 else ())
            if layer == 0:
                early_from_chips = from_chips
            dq, dk, dv, dgq, dgk = _qkv_prep_bwd(rec["proj"], rec["gq"], rec["gk"], (dq_f, dq_s), (dk_f, dk_s),
                                                 (dv_f, dv_s), tag + "_qkv_prep_bwd")
            dcum = (dfq + dfk[:, :, 0:2].transpose(0, 2, 1)).reshape(H_FOX, s // LANES, LANES)
            dlogit, dbias = _fgate_bwd(rec["logit"], attn_f_bias[i], dcum, tag + "_fgate_bwd")
            g_f_bias[i] = dbias[:, 0]
            dgate = jnp.pad(dlogit.reshape(H_FOX, s).T, ((0, 0), (0, LANES - H_FOX))).astype(BF16)
            dproj = jnp.concatenate([dq, dk, dv, dgate], axis=1)

            def fold(dg):
                per_head = dg.reshape(16, HEAD_DIM)
                return jnp.sum(per_head[:8], axis=0), jnp.sum(per_head[8:], axis=0)

            g_fq[i], g_sq[i] = fold(dgq)
            g_fk[i], g_sk[i] = fold(dgk)
            g_attn_w_in[i] = _matmul_tn(rec["xn"], dproj, tag + "_attn_in_dw", tn=640)[:, :ATTN_IN]
            dh, g_attn_norm[i] = _matmul_rows(dproj, a_w_in[i].T, tag + "_attn_in_dx", tk=3200,
                                              rms_bwd=(rec["h_in"], attn_norm[i], dh))
        else:
            dy = _matmul(dh, c_w_out[i].T, tag + "_conv_out_dx", tn=1024)
            g_conv_w_out[i] = _matmul_tn(rec["y"], dh, tag + "_conv_out_dw", tn=1024)
            dproj, g_conv_kernel[i] = _sconv_bwd(rec["proj"], conv_kernel_f[i], dy, tag + "_sconv_bwd")
            g_conv_w_in[i] = _matmul_tn(rec["xn"], dproj, tag + "_conv_in_dw", tn=1024)
            dh, g_conv_norm[i] = _matmul_rows(dproj, c_w_in[i].T, tag + "_conv_in_dx", tm=1024, tk=1024,
                                              rms_bwd=(rec["h_in"], conv_norm_f[i], dh))
    grad_x = dh[None]

    late_names = [nm for nm in big_names if nm in late_layers]
    late_g = slabs_for_devices(late_layers)
    late_pairs = [_pair_add(g, r, ids, "reduce_pair_add_late_" + nm)
                  for g, r, nm in zip(late_g, _sibling_exchange(late_g, "reduce_sibling_late"), late_names)]
    late_from_chips = _chip_exchange([pr[0] for pr in late_pairs], "reduce_chips_late")

    def update(piece_pairs, piece_recv, piece_names, layers):
        outs = {}
        for (own, recv, nm) in zip([pr[1] for pr in piece_pairs], piece_recv, piece_names):
            sl = layers[nm]
            which = big_names.index(nm)
            outs[nm] = _adamw_reduce(own, recv, ids, _rows2d(big[which][sl]), _rows2d(big_m[which][sl]),
                                     _rows2d(big_v[which][sl]), "adamw_%s_%d" % (nm, sl.start))
        return outs

    late_out = update(late_pairs, late_from_chips, late_names, late_layers)
    early_out = update(early_pairs, early_from_chips, big_names, early_layers)
    grads_big, delta_big, newm_big, newv_big = [], [], [], []
    for which, nm in enumerate(big_names):
        shp = big[which].shape
        for k, dest in enumerate((grads_big, delta_big, newm_big, newv_big)):
            parts = ([late_out[nm][k]] if nm in late_out else []) + [early_out[nm][k]]
            dest.append(jnp.concatenate(parts, axis=0).reshape(shp))

    rep_g = [jnp.stack(g_attn_norm), jnp.stack(g_f_bias), jnp.stack(g_fq), jnp.stack(g_fk), jnp.stack(g_sq),
             jnp.stack(g_sk), jnp.stack(g_ffn_norm)]
    sh_g = [jnp.stack(g_conv_norm).reshape(n_conv, N_DEV, -1).transpose(1, 0, 2),
            jnp.stack(g_conv_kernel).reshape(n_conv, 3, N_DEV, -1).transpose(2, 0, 1, 3),
            jnp.stack(g_ffn_conv).reshape(depth, 3, N_DEV, -1).transpose(2, 0, 1, 3)]
    n_rep = sum(int(a.size) for a in rep)
    n_sh = sum(int(a.size) for a in small_sh)
    partial = _pack(rep_g + [jnp.concatenate([a.reshape(N_DEV, -1) for a in sh_g], axis=1)], LANES, 8)
    total = _sum_sources(_all_gather([partial], "gather_small_grads")[0], "sum_small_grads").reshape(-1)
    rep_tot = total[:n_rep]
    sh_tot = lax.dynamic_slice_in_dim(total[n_rep:n_rep + N_DEV * n_sh].reshape(N_DEV, n_sh), me, 1, axis=0)[0]
    g_small = _pack([rep_tot, sh_tot], LANES, 8)

    def small_pack_of(rep_list, sh_list):
        return _pack(rep_list + sh_list, LANES, 8)

    d_small, m_small, v_small = _adamw_small(
        small_pack_of(rep, small_sh), g_small,
        small_pack_of([m_attn_norm, m_attn_f_bias, m_fox_q_gain, m_fox_k_gain, m_sb_q_gain, m_sb_k_gain, m_ffn_norm],
                      [m_conv_norm, m_conv_kernel, m_ffn_conv]),
        small_pack_of([v_attn_norm, v_attn_f_bias, v_fox_q_gain, v_fox_k_gain, v_sb_q_gain, v_sb_k_gain, v_ffn_norm],
                      [v_conv_norm, v_conv_kernel, v_ffn_conv]),
        "adamw_small")
    small_shapes = rep_shapes + small_sh_shapes

    def split_small(a):
        return _unpack(a.reshape(-1), small_shapes)

    def ordered(big_list, small_list):
        an, fb, fq, fk, sq, sk, fn, cno, cke, fco = small_list
        awi, awo, cwi, cwo, fwu, fwd = big_list
        return [an, awi, fb, fq, fk, sq, sk, awo, cno, cwi, cke, cwo, fn, fwu, fco, fwd]

    grads = ordered(grads_big, split_small(g_small))
    deltas = ordered(delta_big, split_small(d_small))
    new_m = ordered(newm_big, split_small(m_small))
    new_v = ordered(newv_big, split_small(v_small))
    return (loss, grad_x, *grads, *deltas, *new_m, *new_v)
```
